```python
import math
import jax, jax.numpy as jnp
from jax import lax
import numpy as np

D_MODEL = 1024
BATCH = 2
SEQ = 8192
DEPTH = 1

N_DIFF_HEADS = 4
DIFF_HEAD_DIM = 64
DIFF_V_DIM = 2 * DIFF_HEAD_DIM
N_MOBA_HEADS = 8
MOBA_HEAD_DIM = 64
MOBA_BLOCK = 256
MOBA_TOPK = 3
Q_BLOCK = 128
N_HEADS_TOTAL = N_DIFF_HEADS + N_MOBA_HEADS
N_BUCKETS = 32
MAX_DISTANCE = 128
N_EXPERTS = 32
TOP_K = 4
D_FF = D_MODEL
SWIGLU_LIMIT = 7.0
SWIGLU_ALPHA = 1.702
RMS_EPS = 1e-5

DIFF_QK_WIDTH = N_DIFF_HEADS * 2 * DIFF_HEAD_DIM
DIFF_V_WIDTH = N_DIFF_HEADS * DIFF_V_DIM
MOBA_WIDTH = N_MOBA_HEADS * MOBA_HEAD_DIM
SPLIT_WIDTHS = (DIFF_QK_WIDTH, DIFF_QK_WIDTH, DIFF_V_WIDTH,
                MOBA_WIDTH, MOBA_WIDTH, MOBA_WIDTH, D_MODEL, D_MODEL)
IN_WIDTH = sum(SPLIT_WIDTHS)
SPLIT_POINTS = tuple(int(v) for v in np.cumsum(SPLIT_WIDTHS)[:-1])

kernel_name = "hybrid_diffattn_moba_moe_adaln"


def rmsnorm(x, g):
    xf = x.astype(jnp.float32)
    y = xf * lax.rsqrt(jnp.mean(xf * xf, axis=-1, keepdims=True) + RMS_EPS)
    return y.astype(x.dtype) * g


def t5_bucket(dist):
    n = jnp.maximum(dist, 0)
    max_exact = N_BUCKETS // 2
    nf = jnp.maximum(n, max_exact).astype(jnp.float32)
    large = max_exact + (jnp.log(nf / max_exact) / math.log(MAX_DISTANCE / max_exact)
                         * (N_BUCKETS - max_exact)).astype(jnp.int32)
    large = jnp.minimum(large, N_BUCKETS - 1)
    return jnp.where(n < max_exact, n, large)


def diff_attention(q, k, v, bias_tab, lam, lam_init, subln_g):
    B, H, _, S, d = q.shape
    dv = v.shape[-1]
    nq = S // Q_BLOCK
    scale = d ** -0.5
    k_pos = jnp.arange(S)

    def one_block(i):
        q0 = i * Q_BLOCK
        qb = lax.dynamic_slice_in_dim(q, q0, Q_BLOCK, axis=3)
        s = jnp.einsum('bhmqd,bhmkd->bhmqk', qb, k).astype(jnp.float32) * scale
        dist = (q0 + jnp.arange(Q_BLOCK))[:, None] - k_pos[None, :]
        bias = jnp.transpose(bias_tab[t5_bucket(dist)], (2, 0, 1)).astype(jnp.float32)
        s = jnp.where(dist >= 0, s + bias[None, :, None], -jnp.inf)
        p = jax.nn.softmax(s, axis=-1)
        p = p[:, :, 0] - lam * p[:, :, 1]
        return jnp.einsum('bhqk,bhkd->bhqd', p.astype(v.dtype), v)

    o = lax.map(one_block, jnp.arange(nq))
    o = jnp.transpose(o, (1, 0, 3, 2, 4)).reshape(B, S, H, dv)
    o = rmsnorm(o, subln_g) * (1.0 - lam_init)
    return o.reshape(B, S, H * dv)


def moba_attention(q, k, v, bias_tab):
    B, H, S, d = q.shape
    nb = -(-S // MOBA_BLOCK)
    pad = nb * MOBA_BLOCK - S
    k_pad = jnp.pad(k, ((0, 0), (0, 0), (0, pad), (0, 0)))
    v_pad = jnp.pad(v, ((0, 0), (0, 0), (0, pad), (0, 0)))
    k_blocks = k_pad.reshape(B, H, nb, MOBA_BLOCK, d)
    v_blocks = v_pad.reshape(B, H, nb, MOBA_BLOCK, d)
    k_mean = jnp.mean(k_blocks.astype(jnp.float32), axis=3).astype(k.dtype)
    k_sel = min(MOBA_TOPK, nb)
    scale = d ** -0.5
    bi = jnp.arange(B)[:, None, None, None]
    hi = jnp.arange(H)[None, :, None, None]
    hi5 = jnp.arange(H)[None, :, None, None, None]
    blk_ar = jnp.arange(MOBA_BLOCK)
    nq = S // Q_BLOCK

    def one_block(i):
        q0 = i * Q_BLOCK
        own = q0 // MOBA_BLOCK
        qb = lax.dynamic_slice_in_dim(q, q0, Q_BLOCK, axis=2)
        q_pos = q0 + jnp.arange(Q_BLOCK)
        gs = jnp.einsum('bhqd,bhnd->bhqn', qb, k_mean).astype(jnp.float32)
        gs = jnp.where(jnp.arange(nb) < own, gs, -jnp.inf)
        _, sel = lax.top_k(gs, k_sel)
        sel_valid = jnp.arange(k_sel) < own
        ks = k_blocks[bi, hi, sel]
        vs = v_blocks[bi, hi, sel]
        s_sel = jnp.einsum('bhqd,bhqjkd->bhqjk', qb, ks).astype(jnp.float32) * scale
        sel_pos = sel[..., None] * MOBA_BLOCK + blk_ar
        dist_sel = q_pos[:, None, None] - sel_pos
        s_sel = s_sel + bias_tab[t5_bucket(dist_sel), hi5].astype(jnp.float32)
        s_sel = jnp.where(sel_valid[:, None], s_sel, -jnp.inf)
        s_sel = s_sel.reshape(B, H, Q_BLOCK, k_sel * MOBA_BLOCK)
        k_own = lax.dynamic_slice_in_dim(k_pad, own * MOBA_BLOCK, MOBA_BLOCK, axis=2)
        v_own = lax.dynamic_slice_in_dim(v_pad, own * MOBA_BLOCK, MOBA_BLOCK, axis=2)
        s_own = jnp.einsum('bhqd,bhkd->bhqk', qb, k_own).astype(jnp.float32) * scale
        dist_own = q_pos[:, None] - (own * MOBA_BLOCK + blk_ar)[None, :]
        bias_own = jnp.transpose(bias_tab[t5_bucket(dist_own)], (2, 0, 1)).astype(jnp.float32)
        s_own = jnp.where(dist_own >= 0, s_own + bias_own[None], -jnp.inf)
        p = jax.nn.softmax(jnp.concatenate([s_sel, s_own], axis=-1), axis=-1).astype(v.dtype)
        p_sel = p[..., :k_sel * MOBA_BLOCK].reshape(B, H, Q_BLOCK, k_sel, MOBA_BLOCK)
        p_own = p[..., k_sel * MOBA_BLOCK:]
        return (jnp.einsum('bhqjk,bhqjkd->bhqd', p_sel, vs)
                + jnp.einsum('bhqk,bhkd->bhqd', p_own, v_own))

    o = lax.map(one_block, jnp.arange(nq))
    return jnp.transpose(o, (1, 0, 3, 2, 4)).reshape(B, S, H * d)


def moe_ffn(h, w_router, b_router, w_gate_up, b_gate_up, w_down, b_down):
    B, S, D = h.shape
    t = h.reshape(B * S, D)
    logits = (t @ w_router + b_router).astype(jnp.float32)
    top_val, top_idx = lax.top_k(logits, TOP_K)
    top_w = jax.nn.softmax(top_val, axis=-1)
    combine = jnp.sum(jax.nn.one_hot(top_idx, N_EXPERTS, dtype=jnp.float32)
                      * top_w[..., None], axis=1).astype(t.dtype)
    out = jnp.zeros_like(t)
    for e in range(N_EXPERTS):
        gu = t @ w_gate_up[e] + b_gate_up[e]
        gate = jnp.minimum(gu[:, :D_FF], SWIGLU_LIMIT)
        up = jnp.clip(gu[:, D_FF:], -SWIGLU_LIMIT, SWIGLU_LIMIT)
        act = (up + 1.0) * gate * jax.nn.sigmoid(SWIGLU_ALPHA * gate)
        out = out + combine[:, e:e + 1] * (act @ w_down[e] + b_down[e])
    return out.reshape(B, S, D)


def setup_inputs(seed: int = 0) -> dict:
    key = jax.random.key(seed)
    ks = jax.random.split(key, 24)
    f32 = jnp.float32
    nrm = lambda k, shape, s: jax.random.normal(k, shape, f32) * s
    L, D, E = DEPTH, D_MODEL, N_EXPERTS
    return {
        "x": nrm(ks[0], (BATCH, SEQ, D), 1.0),
        "c": nrm(ks[1], (BATCH, D), 1.0),
        "rel_bias": nrm(ks[2], (N_BUCKETS, N_HEADS_TOTAL), 0.5),
        "w_ada": nrm(ks[3], (L, D, 6 * D), 0.5 * D ** -0.5),
        "b_ada": nrm(ks[4], (L, 6 * D), 0.02),
        "g_mix": 1.0 + nrm(ks[5], (L, D), 0.02),
        "w_in": nrm(ks[6], (L, D, IN_WIDTH), D ** -0.5),
        "lambda_q1": nrm(ks[7], (L, DIFF_HEAD_DIM), 0.1),
        "lambda_k1": nrm(ks[8], (L, DIFF_HEAD_DIM), 0.1),
        "lambda_q2": nrm(ks[9], (L, DIFF_HEAD_DIM), 0.1),
        "lambda_k2": nrm(ks[10], (L, DIFF_HEAD_DIM), 0.1),
        "subln_g": 1.0 + nrm(ks[11], (L, DIFF_V_DIM), 0.02),
        "w_br_a": nrm(ks[12], (L, DIFF_V_WIDTH, D), DIFF_V_WIDTH ** -0.5),
        "w_br_b": nrm(ks[13], (L, MOBA_WIDTH, D), MOBA_WIDTH ** -0.5),
        "w_out": nrm(ks[14], (L, D, D), D ** -0.5),
        "g_ffn": 1.0 + nrm(ks[15], (L, D), 0.02),
        "w_router": nrm(ks[16], (L, D, E), D ** -0.5),
        "b_router": nrm(ks[17], (L, E), 0.01),
        "w_gate_up": nrm(ks[18], (L, E, D, 2 * D_FF), D ** -0.5),
        "b_gate_up": nrm(ks[19], (L, E, 2 * D_FF), 0.01),
        "w_down": nrm(ks[20], (L, E, D_FF, D), D_FF ** -0.5),
        "b_down": nrm(ks[21], (L, E, D), 0.01),
        "g_final": 1.0 + nrm(ks[22], (D,), 0.02),
    }


def reference(x, c, rel_bias, w_ada, b_ada, g_mix, w_in, lambda_q1, lambda_k1,
              lambda_q2, lambda_k2, subln_g, w_br_a, w_br_b, w_out, g_ffn,
              w_router, b_router, w_gate_up, b_gate_up, w_down, b_down, g_final):
    B, S, D = x.shape
    bias_a = rel_bias[:, :N_DIFF_HEADS]
    bias_b = rel_bias[:, N_DIFF_HEADS:]
    for l in range(DEPTH):
        ada = c @ w_ada[l] + b_ada[l]
        sh_m, sc_m, gt_m, sh_f, sc_f, gt_f = [a[:, None, :] for a in jnp.split(ada, 6, axis=-1)]

        h = rmsnorm(x, g_mix[l]) * (1.0 + sc_m) + sh_m
        proj = h @ w_in[l]
        q_a, k_a, v_a, q_b, k_b, v_b, gl_a, gl_b = jnp.split(proj, SPLIT_POINTS, axis=-1)
        q_a = q_a.reshape(B, S, N_DIFF_HEADS, 2, DIFF_HEAD_DIM).transpose(0, 2, 3, 1, 4)
        k_a = k_a.reshape(B, S, N_DIFF_HEADS, 2, DIFF_HEAD_DIM).transpose(0, 2, 3, 1, 4)
        v_a = v_a.reshape(B, S, N_DIFF_HEADS, DIFF_V_DIM).transpose(0, 2, 1, 3)
        q_b = q_b.reshape(B, S, N_MOBA_HEADS, MOBA_HEAD_DIM).transpose(0, 2, 1, 3)
        k_b = k_b.reshape(B, S, N_MOBA_HEADS, MOBA_HEAD_DIM).transpose(0, 2, 1, 3)
        v_b = v_b.reshape(B, S, N_MOBA_HEADS, MOBA_HEAD_DIM).transpose(0, 2, 1, 3)

        lam_init = 0.8 - 0.6 * math.exp(-0.3 * l)
        lam = (jnp.exp(jnp.sum(lambda_q1[l].astype(jnp.float32) * lambda_k1[l].astype(jnp.float32)))
               - jnp.exp(jnp.sum(lambda_q2[l].astype(jnp.float32) * lambda_k2[l].astype(jnp.float32)))
               + lam_init)
        y_a = diff_attention(q_a, k_a, v_a, bias_a, lam, lam_init, subln_g[l])
        y_b = moba_attention(q_b, k_b, v_b, bias_b)

        merged = (jax.nn.sigmoid(gl_a) * (y_a @ w_br_a[l])
                  + jax.nn.sigmoid(gl_b) * (y_b @ w_br_b[l]))
        x = x + gt_m * (merged @ w_out[l])

        h = rmsnorm(x, g_ffn[l]) * (1.0 + sc_f) + sh_f
        x = x + gt_f * moe_ffn(h, w_router[l], b_router[l], w_gate_up[l],
                               b_gate_up[l], w_down[l], b_down[l])
    return rmsnorm(x, g_final)
```

```python
import functools
import math

import jax
import jax.numpy as jnp
import numpy as np
from jax import lax
from jax.experimental import pallas as pl
from jax.experimental.pallas import tpu as pltpu

F32 = jnp.float32
BF16 = jnp.bfloat16
I32 = jnp.int32

D_MODEL = 1024
N_DIFF_HEADS = 4
DIFF_HEAD_DIM = 64
N_MOBA_HEADS = 8
MOBA_HEAD_DIM = 64
MOBA_BLOCK = 256
MOBA_TOPK = 3
N_HEADS_TOTAL = N_DIFF_HEADS + N_MOBA_HEADS
N_BUCKETS = 32
MAX_DISTANCE = 128
N_EXPERTS = 32
TOP_K = 4
D_FF = D_MODEL
SWIGLU_LIMIT = 7.0
SWIGLU_ALPHA = 1.702
RMS_EPS = 1e-5
LAM_INIT = 0.8 - 0.6 * math.exp(-0.3 * 0)
IN_WIDTH = 5120

LANES = 128
VMEM_LIMIT_BYTES = 56 * 1024 * 1024

ATT_T = 256
N_PAIRS = 4
PROJ_TM = 512
MERGE_TM = 512
MOE_TM = 256
DISPATCH_TM = 1024
COMBINE_TM = 256
NEG = -1e30


def _t5_bucket_np(dist):
    n = np.maximum(dist, 0)
    max_exact = N_BUCKETS // 2
    nf = np.maximum(n, max_exact).astype(np.float32)
    large = max_exact + (np.log(nf / max_exact) / math.log(MAX_DISTANCE / max_exact)
                         * (N_BUCKETS - max_exact)).astype(np.int32)
    large = np.minimum(large, N_BUCKETS - 1)
    return np.where(n < max_exact, n, large).astype(np.int32)


def _rms(x, g):
    return x * lax.rsqrt(jnp.mean(x * x, axis=-1, keepdims=True) + RMS_EPS) * g


def _ada_kernel(c_ref, w_ref, b_ref, o_ref):
    o_ref[...] = jnp.dot(c_ref[...].astype(BF16), w_ref[...].astype(BF16),
                         preferred_element_type=F32) + b_ref[...]


def _ada(c8, w, b):
    d, n = w.shape
    tn = 1536
    return pl.pallas_call(
        _ada_kernel,
        out_shape=jax.ShapeDtypeStruct((8, n), F32),
        grid=(n // tn,),
        in_specs=[pl.BlockSpec((8, d), lambda j: (0, 0)),
                  pl.BlockSpec((d, tn), lambda j: (0, j)),
                  pl.BlockSpec((1, tn), lambda j: (0, j))],
        out_specs=pl.BlockSpec((8, tn), lambda j: (0, j)),
        compiler_params=pltpu.CompilerParams(dimension_semantics=("arbitrary",),
                                             vmem_limit_bytes=VMEM_LIMIT_BYTES),
        name="ada",
    )(c8, w, b)


def _bias_kernel(tab_ref, bd_ref, bp_ref, o_ref):
    g = pl.program_id(0)
    p = pl.program_id(1)
    t = pl.program_id(2)
    head = jnp.where(g == 0, p, N_DIFF_HEADS + 2 * p + t)
    bd = bd_ref[...]
    bp = bp_ref[...]
    d0 = jnp.where(bd < 0, NEG, 0.0).astype(F32)
    d1 = jnp.zeros(bp.shape, F32)
    for b in range(N_BUCKETS):
        val = tab_ref[head * N_BUCKETS + b]
        d0 = jnp.where(bd == b, val, d0)
        d1 = jnp.where(bp == b, val, d1)
    o_ref[0, 0, 0] = d0
    o_ref[0, 0, 1] = d1


def _bias_tiles(rel_bias):
    t = ATT_T
    ii = np.arange(t)[:, None]
    jj = np.arange(t)[None, :]
    bd = np.where(ii >= jj, _t5_bucket_np(ii - jj), -1).astype(np.int32)
    bp = _t5_bucket_np(t + ii - jj)
    tab = rel_bias.T.reshape(-1)
    return pl.pallas_call(
        _bias_kernel,
        out_shape=jax.ShapeDtypeStruct((2, N_PAIRS, 2, 2 * t, t), F32),
        grid=(2, N_PAIRS, 2),
        in_specs=[pl.BlockSpec(memory_space=pltpu.SMEM),
                  pl.BlockSpec((t, t), lambda g, p, h: (0, 0)),
                  pl.BlockSpec((t, t), lambda g, p, h: (0, 0))],
        out_specs=pl.BlockSpec((1, 1, 2, t, t), lambda g, p, h: (g, p, 0, h, 0)),
        compiler_params=pltpu.CompilerParams(
            dimension_semantics=("arbitrary", "arbitrary", "arbitrary")),
        name="bias_tiles",
    )(tab, jnp.asarray(bd), jnp.asarray(bp))


def _proj_kernel(x_ref, g_ref, sh_ref, sc_ref, w_ref, o_ref):
    x = x_ref[0]
    h = _rms(x, g_ref[...]) * (1.0 + sc_ref[0]) + sh_ref[0]
    hb = h.astype(BF16)
    n_total = w_ref.shape[1]
    for n0 in range(0, n_total, 512):
        o_ref[0, :, n0:n0 + 512] = jnp.dot(
            hb, w_ref[:, n0:n0 + 512], preferred_element_type=F32).astype(BF16)


def _proj(x, g_mix, mod, w_in_bf):
    b, s, d = x.shape
    n = w_in_bf.shape[1]
    tm = min(PROJ_TM, s)
    return pl.pallas_call(
        _proj_kernel,
        out_shape=jax.ShapeDtypeStruct((b, s, n), BF16),
        grid=(b, s // tm),
        in_specs=[pl.BlockSpec((1, tm, d), lambda bi, i: (bi, i, 0)),
                  pl.BlockSpec((1, d), lambda bi, i: (0, 0)),
                  pl.BlockSpec((1, 1, d), lambda bi, i: (bi * 6 + 0, 0, 0)),
                  pl.BlockSpec((1, 1, d), lambda bi, i: (bi * 6 + 1, 0, 0)),
                  pl.BlockSpec((d, n), lambda bi, i: (0, 0))],
        out_specs=pl.BlockSpec((1, tm, n), lambda bi, i: (bi, i, 0)),
        compiler_params=pltpu.CompilerParams(dimension_semantics=("arbitrary", "arbitrary"),
                                             vmem_limit_bytes=VMEM_LIMIT_BYTES),
        name="proj",
    )(x, g_mix, mod, mod, w_in_bf)


def _stack_q(q_ref, q2_ref, scale):
    t = ATT_T
    q = q_ref[0].astype(F32) * scale
    lane = lax.broadcasted_iota(I32, (t, LANES), 1)
    q2_ref[0:t, :] = jnp.where(lane < 64, q, 0.0).astype(BF16)
    q2_ref[t:2 * t, :] = jnp.where(lane >= 64, q, 0.0).astype(BF16)


def _flash_init(m_ref, l_ref, acc_ref):
    m_ref[...] = jnp.full(m_ref.shape, NEG, F32)
    l_ref[...] = jnp.zeros(l_ref.shape, F32)
    acc_ref[...] = jnp.zeros(acc_ref.shape, F32)


def _flash_update(q2_ref, kt, vt, add, m_ref, l_ref, acc_ref):
    s = lax.dot_general(q2_ref[...], kt, (((1,), (1,)), ((), ())),
                        preferred_element_type=F32)
    s = s + add
    m_prev = m_ref[...]
    m_new = jnp.maximum(m_prev, jnp.max(s, axis=1, keepdims=True))
    alpha = jnp.exp(m_prev - m_new)
    p = jnp.exp(s - m_new)
    l_ref[...] = alpha * l_ref[...] + jnp.sum(p, axis=1, keepdims=True)
    acc_ref[...] = alpha * acc_ref[...] + jnp.dot(p.astype(BF16), vt,
                                                  preferred_element_type=F32)
    m_ref[...] = m_new


def _far_bias_col(tab_ref, head_top, head_bot):
    t = ATT_T
    c_top = tab_ref[head_top * N_BUCKETS + (N_BUCKETS - 1)]
    c_bot = tab_ref[head_bot * N_BUCKETS + (N_BUCKETS - 1)]
    row = lax.broadcasted_iota(I32, (2 * t, 1), 0)
    return jnp.where(row < t, c_top, c_bot)


def _kv_tile(ref, n):
    t = ATT_T
    return ref[0, pl.ds(pl.multiple_of(n * t, t), t), :]


def _diff_kernel(tab_ref, q_ref, k_ref, v_ref, d_ref, lamv_ref, g_ref, o_ref,
                 q2_ref, m_ref, l_ref, acc_ref):
    t = ATT_T
    p = pl.program_id(1)
    i = pl.program_id(2)
    _stack_q(q_ref, q2_ref, DIFF_HEAD_DIM ** -0.5)
    _flash_init(m_ref, l_ref, acc_ref)
    cfar = _far_bias_col(tab_ref, p, p)

    def far_body(n, carry):
        _flash_update(q2_ref, _kv_tile(k_ref, n), _kv_tile(v_ref, n), cfar,
                      m_ref, l_ref, acc_ref)
        return carry

    lax.fori_loop(0, jnp.maximum(i - 1, 0), far_body, 0)

    @pl.when(i >= 1)
    def _():
        _flash_update(q2_ref, _kv_tile(k_ref, i - 1), _kv_tile(v_ref, i - 1), d_ref[0, 0, 1],
                      m_ref, l_ref, acc_ref)

    _flash_update(q2_ref, _kv_tile(k_ref, i), _kv_tile(v_ref, i), d_ref[0, 0, 0],
                  m_ref, l_ref, acc_ref)

    lv = lamv_ref[...]
    lam = (jnp.exp(jnp.sum(lv[0:1] * lv[1:2], axis=1, keepdims=True))
           - jnp.exp(jnp.sum(lv[2:3] * lv[3:4], axis=1, keepdims=True)) + LAM_INIT)
    acc = acc_ref[...]
    l = l_ref[...]
    o = acc[0:t] / l[0:t] - lam * (acc[t:2 * t] / l[t:2 * t])
    y = _rms(o, g_ref[...]) * (1.0 - LAM_INIT)
    o_ref[0] = y.astype(BF16)


def _moba_kernel(tab_ref, q_ref, k_ref, v_ref, d_ref, o_ref,
                 q2_ref, m_ref, l_ref, acc_ref, kmean_ref, sel_ref):
    t = ATT_T
    p = pl.program_id(1)
    i = pl.program_id(2)
    nb = k_ref.shape[1] // t

    @pl.when(i == 0)
    def _():
        kmean_ref[...] = jnp.zeros(kmean_ref.shape, F32)
        for n in range(nb):
            blk = k_ref[0, n * t:(n + 1) * t, :].astype(F32)
            kmean_ref[n:n + 1, :] = jnp.sum(blk, axis=0, keepdims=True) * (1.0 / t)

    _stack_q(q_ref, q2_ref, MOBA_HEAD_DIM ** -0.5)
    _flash_init(m_ref, l_ref, acc_ref)
    head_top = N_DIFF_HEADS + 2 * p
    cfar = _far_bias_col(tab_ref, head_top, head_top + 1)

    gs = lax.dot_general(q2_ref[...], kmean_ref[...].astype(BF16), (((1,), (1,)), ((), ())),
                         preferred_element_type=F32)
    lane = lax.broadcasted_iota(I32, (2 * t, LANES), 1)
    lane_f = lane.astype(F32)
    g = jnp.where(lane < i, gs, -jnp.inf)
    sel = jnp.full((2 * t, LANES), NEG, F32)
    for _ in range(MOBA_TOPK):
        mx = jnp.max(g, axis=1, keepdims=True)
        first = jnp.min(jnp.where(g == mx, lane_f, float(LANES)), axis=1, keepdims=True)
        hit = lane_f == first
        sel = jnp.where(hit, 0.0, sel)
        g = jnp.where(hit, -jnp.inf, g)
    sel_ref[...] = sel

    def sel_col(n):
        return jnp.sum(jnp.where(lane == n, sel_ref[...], 0.0), axis=1, keepdims=True)

    def far_body(n, carry):
        _flash_update(q2_ref, _kv_tile(k_ref, n), _kv_tile(v_ref, n), cfar + sel_col(n),
                      m_ref, l_ref, acc_ref)
        return carry

    lax.fori_loop(0, jnp.maximum(i - 1, 0), far_body, 0)

    @pl.when(i >= 1)
    def _():
        _flash_update(q2_ref, _kv_tile(k_ref, i - 1), _kv_tile(v_ref, i - 1),
                      d_ref[0, 0, 1] + sel_col(i - 1), m_ref, l_ref, acc_ref)

    _flash_update(q2_ref, _kv_tile(k_ref, i), _kv_tile(v_ref, i), d_ref[0, 0, 0],
                  m_ref, l_ref, acc_ref)

    acc = acc_ref[...]
    l = l_ref[...]
    lane_t = lax.broadcasted_iota(I32, (t, LANES), 1)
    o = jnp.where(lane_t < 64, acc[0:t] / l[0:t], acc[t:2 * t] / l[t:2 * t])
    o_ref[0] = o.astype(BF16)


def _attention(kind, tab, proj, dtiles, extra, col0):
    b, s, _ = proj.shape
    t = ATT_T
    nq = s // t
    qc, kc, vc = col0
    gidx = 0 if kind == "diff" else 1
    in_specs = [pl.BlockSpec(memory_space=pltpu.SMEM),
                pl.BlockSpec((1, t, LANES), lambda bi, p, i: (bi, i, qc + p)),
                pl.BlockSpec((1, s, LANES), lambda bi, p, i: (bi, 0, kc + p)),
                pl.BlockSpec((1, s, LANES), lambda bi, p, i: (bi, 0, vc + p)),
                pl.BlockSpec((1, 1, 2, 2 * t, t), lambda bi, p, i: (gidx, p, 0, 0, 0))]
    scratch = [pltpu.VMEM((2 * t, LANES), BF16),
               pltpu.VMEM((2 * t, 1), F32),
               pltpu.VMEM((2 * t, 1), F32),
               pltpu.VMEM((2 * t, LANES), F32)]
    if kind == "diff":
        lamv, subln_g = extra
        in_specs += [pl.BlockSpec(lamv.shape, lambda bi, p, i: (0, 0)),
                     pl.BlockSpec(subln_g.shape, lambda bi, p, i: (0, 0))]
        args = (tab, proj, proj, proj, dtiles, lamv, subln_g)
        body = _diff_kernel
    else:
        scratch += [pltpu.VMEM((LANES, LANES), F32), pltpu.VMEM((2 * t, LANES), F32)]
        args = (tab, proj, proj, proj, dtiles)
        body = _moba_kernel
    return pl.pallas_call(
        body,
        out_shape=jax.ShapeDtypeStruct((b, s, N_PAIRS * LANES), BF16),
        grid=(b, N_PAIRS, nq),
        in_specs=in_specs,
        out_specs=pl.BlockSpec((1, t, LANES), lambda bi, p, i: (bi, i, p)),
        scratch_shapes=scratch,
        compiler_params=pltpu.CompilerParams(
            dimension_semantics=("arbitrary", "arbitrary", "arbitrary"),
            vmem_limit_bytes=VMEM_LIMIT_BYTES),
        name=kind + "_attn",
    )(*args)


def _merge_kernel(x_ref, ya_ref, yb_ref, gla_ref, glb_ref, wa_ref, wb_ref, wo_ref,
                  gt_ref, shf_ref, scf_ref, gf_ref, wr_ref, br_ref,
                  x1_ref, h2_ref, ti_ref, tw_ref):
    a = jnp.dot(ya_ref[0], wa_ref[...], preferred_element_type=F32)
    b = jnp.dot(yb_ref[0], wb_ref[...], preferred_element_type=F32)
    merged = (jax.nn.sigmoid(gla_ref[0].astype(F32)) * a
              + jax.nn.sigmoid(glb_ref[0].astype(F32)) * b)
    mo = jnp.dot(merged.astype(BF16), wo_ref[...], preferred_element_type=F32)
    x1 = x_ref[0] + gt_ref[0] * mo
    x1_ref[0] = x1
    h = _rms(x1, gf_ref[...]) * (1.0 + scf_ref[0]) + shf_ref[0]
    h2_ref[0] = h
    logits = jnp.dot(h.astype(BF16), wr_ref[...], preferred_element_type=F32) + br_ref[...]
    tm = logits.shape[0]
    lane = lax.broadcasted_iota(I32, (tm, LANES), 1)
    lane_f = lane.astype(F32)
    g = jnp.where(lane < N_EXPERTS, logits, -jnp.inf)
    vals, idxs = [], []
    for _ in range(TOP_K):
        mx = jnp.max(g, axis=1, keepdims=True)
        first = jnp.min(jnp.where(g == mx, lane_f, float(LANES)), axis=1, keepdims=True)
        vals.append(mx)
        idxs.append(first)
        g = jnp.where(lane_f == first, -jnp.inf, g)
    es = [jnp.exp(v - vals[0]) for v in vals]
    den = es[0] + es[1] + es[2] + es[3]
    ti = jnp.zeros((tm, LANES), F32)
    tw = jnp.zeros((tm, LANES), F32)
    for k in range(TOP_K):
        ti = jnp.where(lane == k, idxs[k], ti)
        tw = jnp.where(lane == k, es[k] / den, tw)
    ti_ref[0] = ti.astype(I32)
    tw_ref[0] = tw


def _merge(x, proj, y_a, y_b, w_br_a, w_br_b, w_out, mod, g_ffn, w_router, b_router):
    b, s, d = x.shape
    tm = min(MERGE_TM, s)
    full = lambda shape: pl.BlockSpec(shape, lambda bi, i: (0,) * len(shape))
    tile = lambda w, c=0: pl.BlockSpec((1, tm, w), lambda bi, i: (bi, i, c))
    modspec = lambda j: pl.BlockSpec((1, 1, d), lambda bi, i: (bi * 6 + j, 0, 0))
    return pl.pallas_call(
        _merge_kernel,
        out_shape=(jax.ShapeDtypeStruct((b, s, d), F32),
                   jax.ShapeDtypeStruct((b, s, d), F32),
                   jax.ShapeDtypeStruct((b, s, LANES), I32),
                   jax.ShapeDtypeStruct((b, s, LANES), F32)),
        grid=(b, s // tm),
        in_specs=[tile(d), tile(512), tile(512), tile(d, 3), tile(d, 4),
                  full(w_br_a.shape), full(w_br_b.shape), full(w_out.shape),
                  modspec(2), modspec(3), modspec(4), full(g_ffn.shape),
                  full(w_router.shape), full(b_router.shape)],
        out_specs=(tile(d), tile(d), tile(LANES), tile(LANES)),
        compiler_params=pltpu.CompilerParams(dimension_semantics=("arbitrary", "arbitrary"),
                                             vmem_limit_bytes=VMEM_LIMIT_BYTES),
        name="merge",
    )(x, y_a, y_b, proj, proj, w_br_a, w_br_b, w_out, mod, mod, mod, g_ffn,
      w_router, b_router)


def _row_copy(src, i, dst, j, sem):
    return pltpu.make_async_copy(src.at[pl.ds(i, 1)], dst.at[pl.ds(j, 1)], sem)


def _dispatch_kernel(padtile_ref, pos_ref, h2_ref, xs_ref, z_ref, zsem, sem):
    step = pl.program_id(0)
    tm = DISPATCH_TM
    mt = MOE_TM

    def zero_copy(e):
        start = pl.multiple_of(padtile_ref[e] * mt, mt)
        return pltpu.make_async_copy(z_ref, xs_ref.at[pl.ds(start, mt)], zsem)

    @pl.when(step == 0)
    def _():
        z_ref[...] = jnp.zeros(z_ref.shape, F32)
        for e in range(N_EXPERTS):
            @pl.when(padtile_ref[e] >= 0)
            def _():
                zero_copy(e).start()
        for e in range(N_EXPERTS):
            @pl.when(padtile_ref[e] >= 0)
            def _():
                zero_copy(e).wait()

    n_tok = pos_ref.shape[0] // TOP_K

    def issue(j, carry):
        tok = step * tm + j
        for k in range(TOP_K):
            _row_copy(h2_ref, tok, xs_ref, pos_ref[j * TOP_K + k], sem).start()
        return carry

    lax.fori_loop(0, n_tok, issue, 0)

    def drain(j, carry):
        for k in range(TOP_K):
            _row_copy(h2_ref, 0, xs_ref, 0, sem).wait()
        return carry

    lax.fori_loop(0, n_tok, drain, 0)


def _dispatch(h2, pos_flat, padtile, n_rows):
    t, d = h2.shape
    tm = min(DISPATCH_TM, t)
    assert tm == DISPATCH_TM or t < DISPATCH_TM
    return pl.pallas_call(
        _dispatch_kernel,
        out_shape=jax.ShapeDtypeStruct((n_rows, d), F32),
        grid_spec=pltpu.PrefetchScalarGridSpec(
            num_scalar_prefetch=1,
            grid=(t // tm,),
            in_specs=[pl.BlockSpec((tm * TOP_K,), lambda s, pt: (s,), memory_space=pltpu.SMEM),
                      pl.BlockSpec(memory_space=pl.ANY)],
            out_specs=pl.BlockSpec(memory_space=pl.ANY),
            scratch_shapes=[pltpu.VMEM((MOE_TM, d), F32),
                            pltpu.SemaphoreType.DMA, pltpu.SemaphoreType.DMA]),
        compiler_params=pltpu.CompilerParams(dimension_semantics=("arbitrary",),
                                             vmem_limit_bytes=VMEM_LIMIT_BYTES),
        name="dispatch",
    )(padtile, pos_flat, h2)


def _moe_kernel(te_ref, nu_ref, xs_ref, wgu_ref, bgu_ref, wd_ref, bd_ref, ys_ref,
                wgu_bf, wd_bf):
    r = pl.program_id(0)
    valid = r < nu_ref[0]
    prev = te_ref[jnp.maximum(r - 1, 0)]
    fresh = jnp.logical_or(r == 0, te_ref[r] != prev)

    @pl.when(jnp.logical_and(valid, fresh))
    def _():
        wgu_bf[...] = wgu_ref[0].astype(BF16)
        wd_bf[...] = wd_ref[0].astype(BF16)

    @pl.when(valid)
    def _():
        x = xs_ref[...].astype(BF16)
        gu = jnp.dot(x, wgu_bf[...], preferred_element_type=F32) + bgu_ref[0]
        gate = jnp.minimum(gu[:, :D_FF], SWIGLU_LIMIT)
        up = jnp.clip(gu[:, D_FF:], -SWIGLU_LIMIT, SWIGLU_LIMIT)
        act = (up + 1.0) * gate * jax.nn.sigmoid(SWIGLU_ALPHA * gate)
        ys_ref[...] = jnp.dot(act.astype(BF16), wd_bf[...],
                              preferred_element_type=F32) + bd_ref[0]


def _moe(xs, tile_expert, n_used, w_gate_up, b_gate_up, w_down, b_down):
    n_rows, d = xs.shape
    tm = MOE_TM
    n_tiles = n_rows // tm
    e, _, f2 = w_gate_up.shape
    row = lambda r, te, nu: (jnp.minimum(r, nu[0] - 1), 0)
    return pl.pallas_call(
        _moe_kernel,
        out_shape=jax.ShapeDtypeStruct((n_rows, d), F32),
        grid_spec=pltpu.PrefetchScalarGridSpec(
            num_scalar_prefetch=2,
            grid=(n_tiles,),
            in_specs=[pl.BlockSpec((tm, d), row),
                      pl.BlockSpec((1, d, f2), lambda r, te, nu: (te[r], 0, 0)),
                      pl.BlockSpec((1, 1, f2), lambda r, te, nu: (te[r], 0, 0)),
                      pl.BlockSpec((1, D_FF, d), lambda r, te, nu: (te[r], 0, 0)),
                      pl.BlockSpec((1, 1, d), lambda r, te, nu: (te[r], 0, 0))],
            out_specs=pl.BlockSpec((tm, d), row),
            scratch_shapes=[pltpu.VMEM((d, f2), BF16), pltpu.VMEM((D_FF, d), BF16)]),
        compiler_params=pltpu.CompilerParams(dimension_semantics=("arbitrary",),
                                             vmem_limit_bytes=VMEM_LIMIT_BYTES),
        name="moe",
    )(tile_expert, n_used, xs, w_gate_up, b_gate_up, w_down, b_down)


def _combine_kernel(pos_ref, ys_ref, x1_ref, tw_ref, gt_ref, gf_ref, o_ref, gbuf, sem):
    tm = x1_ref.shape[0]

    def issue(j, carry):
        for k in range(TOP_K):
            pltpu.make_async_copy(ys_ref.at[pl.ds(pos_ref[j * TOP_K + k], 1)],
                                  gbuf.at[k, pl.ds(j, 1)], sem).start()
        return carry

    lax.fori_loop(0, tm, issue, 0)

    def drain(j, carry):
        for k in range(TOP_K):
            pltpu.make_async_copy(ys_ref.at[pl.ds(0, 1)], gbuf.at[k, pl.ds(0, 1)], sem).wait()
        return carry

    lax.fori_loop(0, tm, drain, 0)

    tw = tw_ref[...]
    acc = tw[:, 0:1] * gbuf[0]
    for k in range(1, TOP_K):
        acc = acc + tw[:, k:k + 1] * gbuf[k]
    x2 = x1_ref[...] + gt_ref[0] * acc
    o_ref[...] = _rms(x2, gf_ref[...])


def _combine(ys, pos_flat, x1, tw, mod, g_final, seq):
    t, d = x1.shape
    tm = min(COMBINE_TM, t)
    return pl.pallas_call(
        _combine_kernel,
        out_shape=jax.ShapeDtypeStruct((t, d), F32),
        grid=(t // tm,),
        in_specs=[pl.BlockSpec((tm * TOP_K,), lambda s: (s,), memory_space=pltpu.SMEM),
                  pl.BlockSpec(memory_space=pl.ANY),
                  pl.BlockSpec((tm, d), lambda s: (s, 0)),
                  pl.BlockSpec((tm, LANES), lambda s: (s, 0)),
                  pl.BlockSpec((1, 1, d), lambda s: ((s * tm) // seq * 6 + 5, 0, 0)),
                  pl.BlockSpec((1, d), lambda s: (0, 0))],
        out_specs=pl.BlockSpec((tm, d), lambda s: (s, 0)),
        scratch_shapes=[pltpu.VMEM((TOP_K, tm, d), F32), pltpu.SemaphoreType.DMA],
        compiler_params=pltpu.CompilerParams(dimension_semantics=("arbitrary",),
                                             vmem_limit_bytes=VMEM_LIMIT_BYTES),
        name="combine",
    )(pos_flat, ys, x1, tw, mod, g_final)


def _routing_tables(top_idx, n_tiles):
    tm = MOE_TM
    e_flat = top_idx.reshape(-1)
    onehot = (e_flat[:, None] == jnp.arange(N_EXPERTS, dtype=I32)[None, :]).astype(I32)
    csum = jnp.cumsum(onehot, axis=0)
    counts = csum[-1]
    rank = jnp.sum(csum * onehot, axis=1) - 1
    tiles_per = (counts + tm - 1) // tm
    tile_end = jnp.cumsum(tiles_per)
    tile_start = tile_end - tiles_per
    pos = (jnp.sum(onehot * tile_start[None, :], axis=1) * tm + rank).astype(I32)
    n_used = tile_end[-1:].astype(I32)
    tile_ids = jnp.arange(n_tiles, dtype=I32)
    tile_expert = jnp.minimum(
        jnp.sum((tile_ids[:, None] >= tile_end[None, :]).astype(I32), axis=1),
        N_EXPERTS - 1).astype(I32)
    has_pad = (counts % tm) != 0
    padtile = jnp.where(has_pad, tile_end - 1, -1).astype(I32)
    return pos, tile_expert, n_used, padtile


def kernel(x, c, rel_bias, w_ada, b_ada, g_mix, w_in, lambda_q1, lambda_k1, lambda_q2,
           lambda_k2, subln_g, w_br_a, w_br_b, w_out, g_ffn, w_router, b_router,
           w_gate_up, b_gate_up, w_down, b_down, g_final):
    b, s, d = x.shape
    assert d == D_MODEL and s % ATT_T == 0 and MOBA_BLOCK == ATT_T
    t_tok = b * s

    c8 = jnp.zeros((8, d), F32).at[:b].set(c)
    ada = _ada(c8, w_ada[0], b_ada[0].reshape(1, -1))
    mod = ada[:b].reshape(b * 6, 1, d)

    dtiles = _bias_tiles(rel_bias)
    tab = rel_bias.T.reshape(-1)

    proj = _proj(x, g_mix[0].reshape(1, d), mod, w_in[0].astype(BF16))

    lamv = jnp.concatenate([lambda_q1, lambda_k1, lambda_q2, lambda_k2], axis=0)
    y_a = _attention("diff", tab, proj, dtiles, (lamv, subln_g[0].reshape(1, -1)), (0, 4, 8))
    y_b = _attention("moba", tab, proj, dtiles, None, (12, 16, 20))

    wr = jnp.zeros((d, LANES), BF16).at[:, :N_EXPERTS].set(w_router[0].astype(BF16))
    br = jnp.zeros((1, LANES), F32).at[0, :N_EXPERTS].set(b_router[0])
    x1, h2, ti, tw = _merge(x, proj, y_a, y_b, w_br_a[0].astype(BF16), w_br_b[0].astype(BF16),
                            w_out[0].astype(BF16), mod, g_ffn[0].reshape(1, d), wr, br)

    n_tiles = (t_tok * TOP_K) // MOE_TM + N_EXPERTS
    top_idx = ti.reshape(t_tok, LANES)[:, :TOP_K]
    pos, tile_expert, n_used, padtile = _routing_tables(top_idx, n_tiles)

    xs = _dispatch(h2.reshape(t_tok, d), pos, padtile, n_tiles * MOE_TM)
    ys = _moe(xs, tile_expert, n_used, w_gate_up[0], b_gate_up[0].reshape(N_EXPERTS, 1, -1),
              w_down[0], b_down[0].reshape(N_EXPERTS, 1, -1))
    out = _combine(ys, pos, x1.reshape(t_tok, d), tw.reshape(t_tok, LANES), mod,
                   g_final.reshape(1, d), s)
    return out.reshape(b, s, d)
```

```python
import functools
import math

import jax
import jax.numpy as jnp
import numpy as np
from jax import lax
from jax.experimental import pallas as pl
from jax.experimental.pallas import tpu as pltpu

F32 = jnp.float32
BF16 = jnp.bfloat16
I32 = jnp.int32

D_MODEL = 1024
N_DIFF_HEADS = 4
DIFF_HEAD_DIM = 64
N_MOBA_HEADS = 8
MOBA_HEAD_DIM = 64
MOBA_BLOCK = 256
MOBA_TOPK = 3
N_HEADS_TOTAL = N_DIFF_HEADS + N_MOBA_HEADS
N_BUCKETS = 32
MAX_DISTANCE = 128
N_EXPERTS = 32
TOP_K = 4
D_FF = D_MODEL
SWIGLU_LIMIT = 7.0
SWIGLU_ALPHA = 1.702
RMS_EPS = 1e-5
LAM_INIT = 0.8 - 0.6 * math.exp(-0.3 * 0)
IN_WIDTH = 5120

LANES = 128
VMEM_LIMIT_BYTES = 56 * 1024 * 1024

ATT_T = 256
N_PAIRS = 4
PROJ_TM = 512
MERGE_TM = 512
MOE_TM = 256
COMBINE_TM = 256
NEG = -1e30


def _t5_bucket_np(dist):
    n = np.maximum(dist, 0)
    max_exact = N_BUCKETS // 2
    nf = np.maximum(n, max_exact).astype(np.float32)
    large = max_exact + (np.log(nf / max_exact) / math.log(MAX_DISTANCE / max_exact)
                         * (N_BUCKETS - max_exact)).astype(np.int32)
    large = np.minimum(large, N_BUCKETS - 1)
    return np.where(n < max_exact, n, large).astype(np.int32)


def _rms(x, g):
    return x * lax.rsqrt(jnp.mean(x * x, axis=-1, keepdims=True) + RMS_EPS) * g


def _ada_kernel(c_ref, w_ref, b_ref, o_ref):
    o_ref[...] = jnp.dot(c_ref[...].astype(BF16), w_ref[...].astype(BF16),
                         preferred_element_type=F32) + b_ref[...]


def _ada(c8, w, b):
    d, n = w.shape
    tn = 1536
    return pl.pallas_call(
        _ada_kernel,
        out_shape=jax.ShapeDtypeStruct((8, n), F32),
        grid=(n // tn,),
        in_specs=[pl.BlockSpec((8, d), lambda j: (0, 0)),
                  pl.BlockSpec((d, tn), lambda j: (0, j)),
                  pl.BlockSpec((1, tn), lambda j: (0, j))],
        out_specs=pl.BlockSpec((8, tn), lambda j: (0, j)),
        compiler_params=pltpu.CompilerParams(dimension_semantics=("arbitrary",),
                                             vmem_limit_bytes=VMEM_LIMIT_BYTES),
        name="ada",
    )(c8, w, b)


def _bias_kernel(tab_ref, bd_ref, bp_ref, o_ref):
    g = pl.program_id(0)
    p = pl.program_id(1)
    t = pl.program_id(2)
    head = jnp.where(g == 0, p, N_DIFF_HEADS + 2 * p + t)
    bd = bd_ref[...]
    bp = bp_ref[...]
    d0 = jnp.where(bd < 0, NEG, 0.0).astype(F32)
    d1 = jnp.zeros(bp.shape, F32)
    for b in range(N_BUCKETS):
        val = tab_ref[head * N_BUCKETS + b]
        d0 = jnp.where(bd == b, val, d0)
        d1 = jnp.where(bp == b, val, d1)
    o_ref[0, 0, 0] = d0
    o_ref[0, 0, 1] = d1


def _bias_tiles(rel_bias):
    t = ATT_T
    ii = np.arange(t)[:, None]
    jj = np.arange(t)[None, :]
    bd = np.where(ii >= jj, _t5_bucket_np(ii - jj), -1).astype(np.int32)
    bp = _t5_bucket_np(t + ii - jj)
    tab = rel_bias.T.reshape(-1)
    return pl.pallas_call(
        _bias_kernel,
        out_shape=jax.ShapeDtypeStruct((2, N_PAIRS, 2, 2 * t, t), F32),
        grid=(2, N_PAIRS, 2),
        in_specs=[pl.BlockSpec(memory_space=pltpu.SMEM),
                  pl.BlockSpec((t, t), lambda g, p, h: (0, 0)),
                  pl.BlockSpec((t, t), lambda g, p, h: (0, 0))],
        out_specs=pl.BlockSpec((1, 1, 2, t, t), lambda g, p, h: (g, p, 0, h, 0)),
        compiler_params=pltpu.CompilerParams(
            dimension_semantics=("arbitrary", "arbitrary", "arbitrary")),
        name="bias_tiles",
    )(tab, jnp.asarray(bd), jnp.asarray(bp))


def _proj_kernel(x_ref, g_ref, sh_ref, sc_ref, w_ref, o_ref):
    x = x_ref[0]
    h = _rms(x, g_ref[...]) * (1.0 + sc_ref[0]) + sh_ref[0]
    hb = h.astype(BF16)
    n_total = w_ref.shape[1]
    for n0 in range(0, n_total, 512):
        o_ref[0, :, n0:n0 + 512] = jnp.dot(
            hb, w_ref[:, n0:n0 + 512], preferred_element_type=F32).astype(BF16)


def _proj(x, g_mix, mod, w_in_bf):
    b, s, d = x.shape
    n = w_in_bf.shape[1]
    tm = min(PROJ_TM, s)
    return pl.pallas_call(
        _proj_kernel,
        out_shape=jax.ShapeDtypeStruct((b, s, n), BF16),
        grid=(b, s // tm),
        in_specs=[pl.BlockSpec((1, tm, d), lambda bi, i: (bi, i, 0)),
                  pl.BlockSpec((1, d), lambda bi, i: (0, 0)),
                  pl.BlockSpec((1, 1, d), lambda bi, i: (bi * 6 + 0, 0, 0)),
                  pl.BlockSpec((1, 1, d), lambda bi, i: (bi * 6 + 1, 0, 0)),
                  pl.BlockSpec((d, n), lambda bi, i: (0, 0))],
        out_specs=pl.BlockSpec((1, tm, n), lambda bi, i: (bi, i, 0)),
        compiler_params=pltpu.CompilerParams(dimension_semantics=("arbitrary", "arbitrary"),
                                             vmem_limit_bytes=VMEM_LIMIT_BYTES),
        name="proj",
    )(x, g_mix, mod, mod, w_in_bf)


def _stack_q(q_ref, q2_ref, scale):
    t = ATT_T
    q = q_ref[0].astype(F32) * scale
    lane = lax.broadcasted_iota(I32, (t, LANES), 1)
    q2_ref[0:t, :] = jnp.where(lane < 64, q, 0.0).astype(BF16)
    q2_ref[t:2 * t, :] = jnp.where(lane >= 64, q, 0.0).astype(BF16)


def _flash_init(m_ref, l_ref, acc_ref):
    m_ref[...] = jnp.full(m_ref.shape, NEG, F32)
    l_ref[...] = jnp.zeros(l_ref.shape, F32)
    acc_ref[...] = jnp.zeros(acc_ref.shape, F32)


def _flash_update(q2_ref, kt, vt, add, m_ref, l_ref, acc_ref):
    s = lax.dot_general(q2_ref[...], kt, (((1,), (1,)), ((), ())),
                        preferred_element_type=F32)
    s = s + add
    m_prev = m_ref[...]
    m_new = jnp.maximum(m_prev, jnp.max(s, axis=1, keepdims=True))
    alpha = jnp.exp(m_prev - m_new)
    p = jnp.exp(s - m_new)
    l_ref[...] = alpha * l_ref[...] + jnp.sum(p, axis=1, keepdims=True)
    acc_ref[...] = alpha * acc_ref[...] + jnp.dot(p.astype(BF16), vt,
                                                  preferred_element_type=F32)
    m_ref[...] = m_new


def _far_bias_col(tab_ref, head_top, head_bot):
    t = ATT_T
    c_top = tab_ref[head_top * N_BUCKETS + (N_BUCKETS - 1)]
    c_bot = tab_ref[head_bot * N_BUCKETS + (N_BUCKETS - 1)]
    row = lax.broadcasted_iota(I32, (2 * t, 1), 0)
    return jnp.where(row < t, c_top, c_bot)


def _kv_tile(ref, n):
    t = ATT_T
    return ref[0, pl.ds(pl.multiple_of(n * t, t), t), :]


def _diff_kernel(tab_ref, q_ref, k_ref, v_ref, d_ref, lamv_ref, g_ref, o_ref,
                 q2_ref, m_ref, l_ref, acc_ref):
    t = ATT_T
    p = pl.program_id(1)
    i = pl.program_id(2)
    _stack_q(q_ref, q2_ref, DIFF_HEAD_DIM ** -0.5)
    _flash_init(m_ref, l_ref, acc_ref)
    cfar = _far_bias_col(tab_ref, p, p)

    def far_body(n, carry):
        _flash_update(q2_ref, _kv_tile(k_ref, n), _kv_tile(v_ref, n), cfar,
                      m_ref, l_ref, acc_ref)
        return carry

    lax.fori_loop(0, jnp.maximum(i - 1, 0), far_body, 0)

    @pl.when(i >= 1)
    def _():
        _flash_update(q2_ref, _kv_tile(k_ref, i - 1), _kv_tile(v_ref, i - 1), d_ref[0, 0, 1],
                      m_ref, l_ref, acc_ref)

    _flash_update(q2_ref, _kv_tile(k_ref, i), _kv_tile(v_ref, i), d_ref[0, 0, 0],
                  m_ref, l_ref, acc_ref)

    lv = lamv_ref[...]
    lam = (jnp.exp(jnp.sum(lv[0:1] * lv[1:2], axis=1, keepdims=True))
           - jnp.exp(jnp.sum(lv[2:3] * lv[3:4], axis=1, keepdims=True)) + LAM_INIT)
    acc = acc_ref[...]
    l = l_ref[...]
    o = acc[0:t] / l[0:t] - lam * (acc[t:2 * t] / l[t:2 * t])
    y = _rms(o, g_ref[...]) * (1.0 - LAM_INIT)
    o_ref[0] = y.astype(BF16)


def _moba_kernel(tab_ref, q_ref, k_ref, v_ref, d_ref, o_ref,
                 q2_ref, m_ref, l_ref, acc_ref, kmean_ref, sel_ref):
    t = ATT_T
    p = pl.program_id(1)
    i = pl.program_id(2)
    nb = k_ref.shape[1] // t

    @pl.when(i == 0)
    def _():
        kmean_ref[...] = jnp.zeros(kmean_ref.shape, F32)
        for n in range(nb):
            blk = k_ref[0, n * t:(n + 1) * t, :].astype(F32)
            kmean_ref[n:n + 1, :] = jnp.sum(blk, axis=0, keepdims=True) * (1.0 / t)

    _stack_q(q_ref, q2_ref, MOBA_HEAD_DIM ** -0.5)
    _flash_init(m_ref, l_ref, acc_ref)
    head_top = N_DIFF_HEADS + 2 * p
    cfar = _far_bias_col(tab_ref, head_top, head_top + 1)

    gs = lax.dot_general(q2_ref[...], kmean_ref[...].astype(BF16), (((1,), (1,)), ((), ())),
                         preferred_element_type=F32)
    lane = lax.broadcasted_iota(I32, (2 * t, LANES), 1)
    lane_f = lane.astype(F32)
    g = jnp.where(lane < i, gs, -jnp.inf)
    sel = jnp.full((2 * t, LANES), NEG, F32)
    for _ in range(MOBA_TOPK):
        mx = jnp.max(g, axis=1, keepdims=True)
        first = jnp.min(jnp.where(g == mx, lane_f, float(LANES)), axis=1, keepdims=True)
        hit = lane_f == first
        sel = jnp.where(hit, 0.0, sel)
        g = jnp.where(hit, -jnp.inf, g)
    sel_ref[...] = sel

    def sel_col(n):
        return jnp.sum(jnp.where(lane == n, sel_ref[...], 0.0), axis=1, keepdims=True)

    def far_body(n, carry):
        _flash_update(q2_ref, _kv_tile(k_ref, n), _kv_tile(v_ref, n), cfar + sel_col(n),
                      m_ref, l_ref, acc_ref)
        return carry

    lax.fori_loop(0, jnp.maximum(i - 1, 0), far_body, 0)

    @pl.when(i >= 1)
    def _():
        _flash_update(q2_ref, _kv_tile(k_ref, i - 1), _kv_tile(v_ref, i - 1),
                      d_ref[0, 0, 1] + sel_col(i - 1), m_ref, l_ref, acc_ref)

    _flash_update(q2_ref, _kv_tile(k_ref, i), _kv_tile(v_ref, i), d_ref[0, 0, 0],
                  m_ref, l_ref, acc_ref)

    acc = acc_ref[...]
    l = l_ref[...]
    lane_t = lax.broadcasted_iota(I32, (t, LANES), 1)
    o = jnp.where(lane_t < 64, acc[0:t] / l[0:t], acc[t:2 * t] / l[t:2 * t])
    o_ref[0] = o.astype(BF16)


def _attention(kind, tab, proj, dtiles, extra, col0):
    b, s, _ = proj.shape
    t = ATT_T
    nq = s // t
    qc, kc, vc = col0
    gidx = 0 if kind == "diff" else 1
    in_specs = [pl.BlockSpec(memory_space=pltpu.SMEM),
                pl.BlockSpec((1, t, LANES), lambda bi, p, i: (bi, i, qc + p)),
                pl.BlockSpec((1, s, LANES), lambda bi, p, i: (bi, 0, kc + p)),
                pl.BlockSpec((1, s, LANES), lambda bi, p, i: (bi, 0, vc + p)),
                pl.BlockSpec((1, 1, 2, 2 * t, t), lambda bi, p, i: (gidx, p, 0, 0, 0))]
    scratch = [pltpu.VMEM((2 * t, LANES), BF16),
               pltpu.VMEM((2 * t, 1), F32),
               pltpu.VMEM((2 * t, 1), F32),
               pltpu.VMEM((2 * t, LANES), F32)]
    if kind == "diff":
        lamv, subln_g = extra
        in_specs += [pl.BlockSpec(lamv.shape, lambda bi, p, i: (0, 0)),
                     pl.BlockSpec(subln_g.shape, lambda bi, p, i: (0, 0))]
        args = (tab, proj, proj, proj, dtiles, lamv, subln_g)
        body = _diff_kernel
    else:
        scratch += [pltpu.VMEM((LANES, LANES), F32), pltpu.VMEM((2 * t, LANES), F32)]
        args = (tab, proj, proj, proj, dtiles)
        body = _moba_kernel
    return pl.pallas_call(
        body,
        out_shape=jax.ShapeDtypeStruct((b, s, N_PAIRS * LANES), BF16),
        grid=(b, N_PAIRS, nq),
        in_specs=in_specs,
        out_specs=pl.BlockSpec((1, t, LANES), lambda bi, p, i: (bi, i, p)),
        scratch_shapes=scratch,
        compiler_params=pltpu.CompilerParams(
            dimension_semantics=("arbitrary", "arbitrary", "arbitrary"),
            vmem_limit_bytes=VMEM_LIMIT_BYTES),
        name=kind + "_attn",
    )(*args)


def _merge_kernel(x_ref, ya_ref, yb_ref, gla_ref, glb_ref, wa_ref, wb_ref, wo_ref,
                  gt_ref, shf_ref, scf_ref, gf_ref, wr_ref, br_ref,
                  x1_ref, h2_ref, ti_ref, tw_ref):
    a = jnp.dot(ya_ref[0], wa_ref[...], preferred_element_type=F32)
    b = jnp.dot(yb_ref[0], wb_ref[...], preferred_element_type=F32)
    merged = (jax.nn.sigmoid(gla_ref[0].astype(F32)) * a
              + jax.nn.sigmoid(glb_ref[0].astype(F32)) * b)
    mo = jnp.dot(merged.astype(BF16), wo_ref[...], preferred_element_type=F32)
    x1 = x_ref[0] + gt_ref[0] * mo
    x1_ref[0] = x1
    h = _rms(x1, gf_ref[...]) * (1.0 + scf_ref[0]) + shf_ref[0]
    h2_ref[0] = h
    logits = jnp.dot(h.astype(BF16), wr_ref[...], preferred_element_type=F32) + br_ref[...]
    tm = logits.shape[0]
    lane = lax.broadcasted_iota(I32, (tm, LANES), 1)
    lane_f = lane.astype(F32)
    g = jnp.where(lane < N_EXPERTS, logits, -jnp.inf)
    vals, idxs = [], []
    for _ in range(TOP_K):
        mx = jnp.max(g, axis=1, keepdims=True)
        first = jnp.min(jnp.where(g == mx, lane_f, float(LANES)), axis=1, keepdims=True)
        vals.append(mx)
        idxs.append(first)
        g = jnp.where(lane_f == first, -jnp.inf, g)
    es = [jnp.exp(v - vals[0]) for v in vals]
    den = es[0] + es[1] + es[2] + es[3]
    ti = jnp.zeros((tm, LANES), F32)
    tw = jnp.zeros((tm, LANES), F32)
    for k in range(TOP_K):
        ti = jnp.where(lane == k, idxs[k], ti)
        tw = jnp.where(lane == k, es[k] / den, tw)
    ti_ref[0] = ti.astype(I32)
    tw_ref[0] = tw


def _merge(x, proj, y_a, y_b, w_br_a, w_br_b, w_out, mod, g_ffn, w_router, b_router):
    b, s, d = x.shape
    tm = min(MERGE_TM, s)
    full = lambda shape: pl.BlockSpec(shape, lambda bi, i: (0,) * len(shape))
    tile = lambda w, c=0: pl.BlockSpec((1, tm, w), lambda bi, i: (bi, i, c))
    modspec = lambda j: pl.BlockSpec((1, 1, d), lambda bi, i: (bi * 6 + j, 0, 0))
    return pl.pallas_call(
        _merge_kernel,
        out_shape=(jax.ShapeDtypeStruct((b, s, d), F32),
                   jax.ShapeDtypeStruct((b, s, d), F32),
                   jax.ShapeDtypeStruct((b, s, LANES), I32),
                   jax.ShapeDtypeStruct((b, s, LANES), F32)),
        grid=(b, s // tm),
        in_specs=[tile(d), tile(512), tile(512), tile(d, 3), tile(d, 4),
                  full(w_br_a.shape), full(w_br_b.shape), full(w_out.shape),
                  modspec(2), modspec(3), modspec(4), full(g_ffn.shape),
                  full(w_router.shape), full(b_router.shape)],
        out_specs=(tile(d), tile(d), tile(LANES), tile(LANES)),
        compiler_params=pltpu.CompilerParams(dimension_semantics=("arbitrary", "arbitrary"),
                                             vmem_limit_bytes=VMEM_LIMIT_BYTES),
        name="merge",
    )(x, y_a, y_b, proj, proj, w_br_a, w_br_b, w_out, mod, mod, mod, g_ffn,
      w_router, b_router)


def _moe_kernel(te_ref, nu_ref, tok_ref, tok_next_ref, h2_ref, wgu_ref, bgu_ref, wd_ref, bd_ref,
                ys_ref, xbuf, sems, wgu_bf, wd_bf):
    r = pl.program_id(0)
    n_used = nu_ref[0]
    valid = r < n_used
    slot = r % 2
    tm = xbuf.shape[1]

    def gather(idx_ref, s):
        def body(j, carry):
            pltpu.make_async_copy(h2_ref.at[pl.ds(idx_ref[j], 1)],
                                  xbuf.at[s, pl.ds(j, 1)], sems.at[s]).start()
            return carry
        lax.fori_loop(0, tm, body, 0)

    @pl.when(r == 0)
    def _():
        gather(tok_ref, 0)

    @pl.when(r + 1 < n_used)
    def _():
        gather(tok_next_ref, 1 - slot)

    prev = te_ref[jnp.maximum(r - 1, 0)]
    fresh = jnp.logical_or(r == 0, te_ref[r] != prev)

    @pl.when(jnp.logical_and(valid, fresh))
    def _():
        wgu_bf[...] = wgu_ref[0].astype(BF16)
        wd_bf[...] = wd_ref[0].astype(BF16)

    @pl.when(valid)
    def _():
        pltpu.make_async_copy(h2_ref.at[pl.ds(0, tm)], xbuf.at[slot], sems.at[slot]).wait()
        x = xbuf[slot].astype(BF16)
        gu = jnp.dot(x, wgu_bf[...], preferred_element_type=F32) + bgu_ref[0]
        gate = jnp.minimum(gu[:, :D_FF], SWIGLU_LIMIT)
        up = jnp.clip(gu[:, D_FF:], -SWIGLU_LIMIT, SWIGLU_LIMIT)
        act = (up + 1.0) * gate * jax.nn.sigmoid(SWIGLU_ALPHA * gate)
        ys_ref[...] = jnp.dot(act.astype(BF16), wd_bf[...],
                              preferred_element_type=F32) + bd_ref[0]


def _moe(h2, row_token, tile_expert, n_used, w_gate_up, b_gate_up, w_down, b_down):
    _, d = h2.shape
    tm = MOE_TM
    n_tiles = tile_expert.shape[0]
    e, _, f2 = w_gate_up.shape
    row = lambda r, te, nu: (jnp.minimum(r, nu[0] - 1), 0)
    return pl.pallas_call(
        _moe_kernel,
        out_shape=jax.ShapeDtypeStruct((n_tiles * tm, d), F32),
        grid_spec=pltpu.PrefetchScalarGridSpec(
            num_scalar_prefetch=2,
            grid=(n_tiles,),
            in_specs=[pl.BlockSpec((tm,), lambda r, te, nu: (r,), memory_space=pltpu.SMEM),
                      pl.BlockSpec((tm,), lambda r, te, nu: (jnp.minimum(r + 1, n_tiles - 1),),
                                   memory_space=pltpu.SMEM),
                      pl.BlockSpec(memory_space=pl.ANY),
                      pl.BlockSpec((1, d, f2), lambda r, te, nu: (te[r], 0, 0)),
                      pl.BlockSpec((1, 1, f2), lambda r, te, nu: (te[r], 0, 0)),
                      pl.BlockSpec((1, D_FF, d), lambda r, te, nu: (te[r], 0, 0)),
                      pl.BlockSpec((1, 1, d), lambda r, te, nu: (te[r], 0, 0))],
            out_specs=pl.BlockSpec((tm, d), row),
            scratch_shapes=[pltpu.VMEM((2, tm, d), F32), pltpu.SemaphoreType.DMA((2,)),
                            pltpu.VMEM((d, f2), BF16), pltpu.VMEM((D_FF, d), BF16)]),
        compiler_params=pltpu.CompilerParams(dimension_semantics=("arbitrary",),
                                             vmem_limit_bytes=VMEM_LIMIT_BYTES),
        name="moe",
    )(tile_expert, n_used, row_token, row_token, h2, w_gate_up, b_gate_up, w_down, b_down)


def _combine_kernel(pos_ref, ys_ref, x1_ref, tw_ref, gt_ref, gf_ref, o_ref, gbuf, sem):
    tm = x1_ref.shape[0]

    def issue(j, carry):
        for k in range(TOP_K):
            pltpu.make_async_copy(ys_ref.at[pl.ds(pos_ref[j * TOP_K + k], 1)],
                                  gbuf.at[k, pl.ds(j, 1)], sem).start()
        return carry

    lax.fori_loop(0, tm, issue, 0)

    def drain(j, carry):
        for k in range(TOP_K):
            pltpu.make_async_copy(ys_ref.at[pl.ds(0, 1)], gbuf.at[k, pl.ds(0, 1)], sem).wait()
        return carry

    lax.fori_loop(0, tm, drain, 0)

    tw = tw_ref[...]
    acc = tw[:, 0:1] * gbuf[0]
    for k in range(1, TOP_K):
        acc = acc + tw[:, k:k + 1] * gbuf[k]
    x2 = x1_ref[...] + gt_ref[0] * acc
    o_ref[...] = _rms(x2, gf_ref[...])


def _combine(ys, pos_flat, x1, tw, mod, g_final, seq):
    t, d = x1.shape
    tm = min(COMBINE_TM, t)
    return pl.pallas_call(
        _combine_kernel,
        out_shape=jax.ShapeDtypeStruct((t, d), F32),
        grid=(t // tm,),
        in_specs=[pl.BlockSpec((tm * TOP_K,), lambda s: (s,), memory_space=pltpu.SMEM),
                  pl.BlockSpec(memory_space=pl.ANY),
                  pl.BlockSpec((tm, d), lambda s: (s, 0)),
                  pl.BlockSpec((tm, LANES), lambda s: (s, 0)),
                  pl.BlockSpec((1, 1, d), lambda s: ((s * tm) // seq * 6 + 5, 0, 0)),
                  pl.BlockSpec((1, d), lambda s: (0, 0))],
        out_specs=pl.BlockSpec((tm, d), lambda s: (s, 0)),
        scratch_shapes=[pltpu.VMEM((TOP_K, tm, d), F32), pltpu.SemaphoreType.DMA],
        compiler_params=pltpu.CompilerParams(dimension_semantics=("arbitrary",),
                                             vmem_limit_bytes=VMEM_LIMIT_BYTES),
        name="combine",
    )(pos_flat, ys, x1, tw, mod, g_final)


def _routing_tables(top_idx, n_tiles):
    tm = MOE_TM
    e_flat = top_idx.reshape(-1)
    onehot = (e_flat[:, None] == jnp.arange(N_EXPERTS, dtype=I32)[None, :]).astype(I32)
    csum = jnp.cumsum(onehot, axis=0)
    counts = csum[-1]
    rank = jnp.sum(csum * onehot, axis=1) - 1
    tiles_per = (counts + tm - 1) // tm
    tile_end = jnp.cumsum(tiles_per)
    tile_start = tile_end - tiles_per
    pos = (jnp.sum(onehot * tile_start[None, :], axis=1) * tm + rank).astype(I32)
    n_used = tile_end[-1:].astype(I32)
    tile_ids = jnp.arange(n_tiles, dtype=I32)
    tile_expert = jnp.minimum(
        jnp.sum((tile_ids[:, None] >= tile_end[None, :]).astype(I32), axis=1),
        N_EXPERTS - 1).astype(I32)
    order = jnp.argsort(e_flat, stable=True).astype(I32)
    first_pair = jnp.cumsum(counts) - counts
    rows = jnp.arange(n_tiles * tm, dtype=I32)
    e_row = tile_expert[rows // tm]
    j_row = rows - tile_start[e_row] * tm
    live = jnp.logical_and(j_row < counts[e_row], rows // tm < n_used[0])
    src = jnp.clip(first_pair[e_row] + j_row, 0, e_flat.shape[0] - 1)
    row_token = jnp.where(live, order[src] // TOP_K, 0).astype(I32)
    return pos, tile_expert, n_used, row_token


def kernel(x, c, rel_bias, w_ada, b_ada, g_mix, w_in, lambda_q1, lambda_k1, lambda_q2,
           lambda_k2, subln_g, w_br_a, w_br_b, w_out, g_ffn, w_router, b_router,
           w_gate_up, b_gate_up, w_down, b_down, g_final):
    b, s, d = x.shape
    assert d == D_MODEL and s % ATT_T == 0 and MOBA_BLOCK == ATT_T
    t_tok = b * s

    c8 = jnp.zeros((8, d), F32).at[:b].set(c)
    ada = _ada(c8, w_ada[0], b_ada[0].reshape(1, -1))
    mod = ada[:b].reshape(b * 6, 1, d)

    dtiles = _bias_tiles(rel_bias)
    tab = rel_bias.T.reshape(-1)

    proj = _proj(x, g_mix[0].reshape(1, d), mod, w_in[0].astype(BF16))

    lamv = jnp.concatenate([lambda_q1, lambda_k1, lambda_q2, lambda_k2], axis=0)
    y_a = _attention("diff", tab, proj, dtiles, (lamv, subln_g[0].reshape(1, -1)), (0, 4, 8))
    y_b = _attention("moba", tab, proj, dtiles, None, (12, 16, 20))

    wr = jnp.zeros((d, LANES), BF16).at[:, :N_EXPERTS].set(w_router[0].astype(BF16))
    br = jnp.zeros((1, LANES), F32).at[0, :N_EXPERTS].set(b_router[0])
    x1, h2, ti, tw = _merge(x, proj, y_a, y_b, w_br_a[0].astype(BF16), w_br_b[0].astype(BF16),
                            w_out[0].astype(BF16), mod, g_ffn[0].reshape(1, d), wr, br)

    n_tiles = (t_tok * TOP_K) // MOE_TM + N_EXPERTS
    top_idx = ti.reshape(t_tok, LANES)[:, :TOP_K]
    pos, tile_expert, n_used, row_token = _routing_tables(top_idx, n_tiles)

    ys = _moe(h2.reshape(t_tok, d), row_token, tile_expert, n_used, w_gate_up[0],
              b_gate_up[0].reshape(N_EXPERTS, 1, -1), w_down[0],
              b_down[0].reshape(N_EXPERTS, 1, -1))
    out = _combine(ys, pos, x1.reshape(t_tok, d), tw.reshape(t_tok, LANES), mod,
                   g_final.reshape(1, d), s)
    return out.reshape(b, s, d)
```

```python
import functools
import math

import jax
import jax.numpy as jnp
import numpy as np
from jax import lax
from jax.experimental import pallas as pl
from jax.experimental.pallas import tpu as pltpu

F32 = jnp.float32
BF16 = jnp.bfloat16
I32 = jnp.int32

D_MODEL = 1024
N_DIFF_HEADS = 4
DIFF_HEAD_DIM = 64
N_MOBA_HEADS = 8
MOBA_HEAD_DIM = 64
MOBA_BLOCK = 256
MOBA_TOPK = 3
N_HEADS_TOTAL = N_DIFF_HEADS + N_MOBA_HEADS
N_BUCKETS = 32
MAX_DISTANCE = 128
N_EXPERTS = 32
TOP_K = 4
D_FF = D_MODEL
SWIGLU_LIMIT = 7.0
SWIGLU_ALPHA = 1.702
RMS_EPS = 1e-5
LAM_INIT = 0.8 - 0.6 * math.exp(-0.3 * 0)
IN_WIDTH = 5120

LANES = 128
VMEM_LIMIT_BYTES = 56 * 1024 * 1024

ATT_T = 256
N_PAIRS = 4
PROJ_TM = 512
MERGE_TM = 512
MOE_TM = 256
COMBINE_TM = 256
NEG = -1e30


def _t5_bucket_np(dist):
    n = np.maximum(dist, 0)
    max_exact = N_BUCKETS // 2
    nf = np.maximum(n, max_exact).astype(np.float32)
    large = max_exact + (np.log(nf / max_exact) / math.log(MAX_DISTANCE / max_exact)
                         * (N_BUCKETS - max_exact)).astype(np.int32)
    large = np.minimum(large, N_BUCKETS - 1)
    return np.where(n < max_exact, n, large).astype(np.int32)


def _rms(x, g):
    return x * lax.rsqrt(jnp.mean(x * x, axis=-1, keepdims=True) + RMS_EPS) * g


def _ada_kernel(c_ref, w_ref, b_ref, o_ref):
    o_ref[...] = jnp.dot(c_ref[...].astype(BF16), w_ref[...].astype(BF16),
                         preferred_element_type=F32) + b_ref[...]


def _ada(c8, w, b):
    d, n = w.shape
    tn = 1536
    return pl.pallas_call(
        _ada_kernel,
        out_shape=jax.ShapeDtypeStruct((8, n), F32),
        grid=(n // tn,),
        in_specs=[pl.BlockSpec((8, d), lambda j: (0, 0)),
                  pl.BlockSpec((d, tn), lambda j: (0, j)),
                  pl.BlockSpec((1, tn), lambda j: (0, j))],
        out_specs=pl.BlockSpec((8, tn), lambda j: (0, j)),
        compiler_params=pltpu.CompilerParams(dimension_semantics=("arbitrary",),
                                             vmem_limit_bytes=VMEM_LIMIT_BYTES),
        name="ada",
    )(c8, w, b)


def _bias_kernel(tab_ref, bd_ref, bp_ref, o_ref):
    g = pl.program_id(0)
    p = pl.program_id(1)
    t = pl.program_id(2)
    head = jnp.where(g == 0, p, N_DIFF_HEADS + 2 * p + t)
    bd = bd_ref[...]
    bp = bp_ref[...]
    d0 = jnp.where(bd < 0, NEG, 0.0).astype(F32)
    d1 = jnp.zeros(bp.shape, F32)
    for b in range(N_BUCKETS):
        val = tab_ref[head * N_BUCKETS + b]
        d0 = jnp.where(bd == b, val, d0)
        d1 = jnp.where(bp == b, val, d1)
    o_ref[0, 0, 0] = d0
    o_ref[0, 0, 1] = d1


def _bias_tiles(rel_bias):
    t = ATT_T
    jj = np.arange(t)[:, None]
    ii = np.arange(t)[None, :]
    bd = np.where(ii >= jj, _t5_bucket_np(ii - jj), -1).astype(np.int32)
    bp = _t5_bucket_np(t + ii - jj)
    tab = rel_bias.T.reshape(-1)
    return pl.pallas_call(
        _bias_kernel,
        out_shape=jax.ShapeDtypeStruct((2, N_PAIRS, 2, t, 2 * t), F32),
        grid=(2, N_PAIRS, 2),
        in_specs=[pl.BlockSpec(memory_space=pltpu.SMEM),
                  pl.BlockSpec((t, t), lambda g, p, h: (0, 0)),
                  pl.BlockSpec((t, t), lambda g, p, h: (0, 0))],
        out_specs=pl.BlockSpec((1, 1, 2, t, t), lambda g, p, h: (g, p, 0, 0, h)),
        compiler_params=pltpu.CompilerParams(
            dimension_semantics=("arbitrary", "arbitrary", "arbitrary")),
        name="bias_tiles",
    )(tab, jnp.asarray(bd), jnp.asarray(bp))


def _proj_kernel(x_ref, g_ref, sh_ref, sc_ref, w_ref, o_ref):
    x = x_ref[0]
    h = _rms(x, g_ref[...]) * (1.0 + sc_ref[0]) + sh_ref[0]
    hb = h.astype(BF16)
    n_total = w_ref.shape[1]
    for n0 in range(0, n_total, 512):
        o_ref[0, :, n0:n0 + 512] = jnp.dot(
            hb, w_ref[:, n0:n0 + 512], preferred_element_type=F32).astype(BF16)


def _proj(x, g_mix, mod, w_in_bf):
    b, s, d = x.shape
    n = w_in_bf.shape[1]
    tm = min(PROJ_TM, s)
    return pl.pallas_call(
        _proj_kernel,
        out_shape=jax.ShapeDtypeStruct((b, s, n), BF16),
        grid=(b, s // tm),
        in_specs=[pl.BlockSpec((1, tm, d), lambda bi, i: (bi, i, 0)),
                  pl.BlockSpec((1, d), lambda bi, i: (0, 0)),
                  pl.BlockSpec((1, 1, d), lambda bi, i: (bi * 6 + 0, 0, 0)),
                  pl.BlockSpec((1, 1, d), lambda bi, i: (bi * 6 + 1, 0, 0)),
                  pl.BlockSpec((d, n), lambda bi, i: (0, 0))],
        out_specs=pl.BlockSpec((1, tm, n), lambda bi, i: (bi, i, 0)),
        compiler_params=pltpu.CompilerParams(dimension_semantics=("arbitrary", "arbitrary"),
                                             vmem_limit_bytes=VMEM_LIMIT_BYTES),
        name="proj",
    )(x, g_mix, mod, mod, w_in_bf)


def _stack_q(q_ref, q2_ref, scale):
    t = ATT_T
    row = lax.broadcasted_iota(I32, (LANES, t), 0)
    for p in range(N_PAIRS):
        q = q_ref[0, 0, p * LANES:(p + 1) * LANES, :].astype(F32) * scale
        q2_ref[p, :, 0:t] = jnp.where(row < 64, q, 0.0).astype(BF16)
        q2_ref[p, :, t:2 * t] = jnp.where(row >= 64, q, 0.0).astype(BF16)


def _flash_init(m_ref, l_ref, acc_ref):
    m_ref[...] = jnp.full(m_ref.shape, NEG, F32)
    l_ref[...] = jnp.zeros(l_ref.shape, F32)
    acc_ref[...] = jnp.zeros(acc_ref.shape, F32)


def _flash_update(p, n, add, q2_ref, k_ref, v_ref, m_ref, l_ref, acc_ref):
    t = ATT_T
    kt = k_ref[0, pl.ds(pl.multiple_of(n * t, t), t), p * LANES:(p + 1) * LANES]
    vt = v_ref[0, n, p * LANES:(p + 1) * LANES, :]
    s = jnp.dot(kt, q2_ref[p], preferred_element_type=F32) + add
    m_prev = m_ref[p]
    m_new = jnp.maximum(m_prev, jnp.max(s, axis=0, keepdims=True))
    alpha = jnp.exp(m_prev - m_new)
    pt = jnp.exp(s - m_new)
    l_ref[p] = alpha * l_ref[p] + jnp.sum(pt, axis=0, keepdims=True)
    acc_ref[p] = alpha * acc_ref[p] + jnp.dot(vt, pt.astype(BF16), preferred_element_type=F32)
    m_ref[p] = m_new


def _far_bias_row(tab_ref, head_left, head_right):
    t = ATT_T
    c_left = tab_ref[head_left * N_BUCKETS + (N_BUCKETS - 1)]
    c_right = tab_ref[head_right * N_BUCKETS + (N_BUCKETS - 1)]
    col = lax.broadcasted_iota(I32, (1, 2 * t), 1)
    return jnp.where(col < t, c_left, c_right)


def _flash_sweep(i, far_add, prev_add, own_add, refs):
    def far_body(n, carry):
        for p in range(N_PAIRS):
            _flash_update(p, n, far_add(p, n), *refs)
        return carry

    lax.fori_loop(0, jnp.maximum(i - 1, 0), far_body, 0)

    @pl.when(i >= 1)
    def _():
        for p in range(N_PAIRS):
            _flash_update(p, i - 1, prev_add(p), *refs)

    for p in range(N_PAIRS):
        _flash_update(p, i, own_add(p), *refs)


def _diff_kernel(tab_ref, q_ref, k_ref, v_ref, d_ref, lamv_ref, g_ref, o_ref,
                 q2_ref, m_ref, l_ref, acc_ref):
    t = ATT_T
    i = pl.program_id(1)
    _stack_q(q_ref, q2_ref, DIFF_HEAD_DIM ** -0.5)
    _flash_init(m_ref, l_ref, acc_ref)
    refs = (q2_ref, k_ref, v_ref, m_ref, l_ref, acc_ref)
    _flash_sweep(i,
                 lambda p, n: _far_bias_row(tab_ref, p, p),
                 lambda p: d_ref[0, p, 1],
                 lambda p: d_ref[0, p, 0],
                 refs)

    lv = lamv_ref[...]
    lam = (jnp.exp(jnp.sum(lv[0:1] * lv[1:2], axis=1, keepdims=True))
           - jnp.exp(jnp.sum(lv[2:3] * lv[3:4], axis=1, keepdims=True)) + LAM_INIT)
    for p in range(N_PAIRS):
        acc = acc_ref[p]
        l = l_ref[p]
        o = acc[:, 0:t] / l[:, 0:t] - lam * (acc[:, t:2 * t] / l[:, t:2 * t])
        y = o * lax.rsqrt(jnp.mean(o * o, axis=0, keepdims=True) + RMS_EPS) * g_ref[...]
        o_ref[0, 0, p * LANES:(p + 1) * LANES, :] = (y * (1.0 - LAM_INIT)).astype(BF16)


def _moba_kernel(tab_ref, q_ref, k_ref, v_ref, d_ref, o_ref,
                 q2_ref, m_ref, l_ref, acc_ref, kmean_ref, sel_ref):
    t = ATT_T
    i = pl.program_id(1)
    nb = k_ref.shape[1] // t
    nbp = kmean_ref.shape[1]

    @pl.when(i == 0)
    def _():
        kmean_ref[...] = jnp.zeros(kmean_ref.shape, F32)
        for p in range(N_PAIRS):
            for n in range(nb):
                blk = k_ref[0, n * t:(n + 1) * t, p * LANES:(p + 1) * LANES].astype(F32)
                kmean_ref[p, n:n + 1, :] = jnp.sum(blk, axis=0, keepdims=True) * (1.0 / t)

    _stack_q(q_ref, q2_ref, MOBA_HEAD_DIM ** -0.5)
    _flash_init(m_ref, l_ref, acc_ref)

    blk_id = lax.broadcasted_iota(I32, (nbp, 2 * t), 0)
    blk_f = blk_id.astype(F32)
    for p in range(N_PAIRS):
        gs = jnp.dot(kmean_ref[p].astype(BF16), q2_ref[p], preferred_element_type=F32)
        g = jnp.where(blk_id < i, gs, -jnp.inf)
        sel = jnp.full((nbp, 2 * t), NEG, F32)
        for _ in range(MOBA_TOPK):
            mx = jnp.max(g, axis=0, keepdims=True)
            first = jnp.min(jnp.where(g == mx, blk_f, float(nbp)), axis=0, keepdims=True)
            hit = blk_f == first
            sel = jnp.where(hit, 0.0, sel)
            g = jnp.where(hit, -jnp.inf, g)
        sel_ref[p] = sel

    def far_row(p):
        head = N_DIFF_HEADS + 2 * p
        return _far_bias_row(tab_ref, head, head + 1)

    refs = (q2_ref, k_ref, v_ref, m_ref, l_ref, acc_ref)
    _flash_sweep(i,
                 lambda p, n: far_row(p) + sel_ref[p, pl.ds(n, 1), :],
                 lambda p: d_ref[0, p, 1] + sel_ref[p, pl.ds(i - 1, 1), :],
                 lambda p: d_ref[0, p, 0],
                 refs)

    row = lax.broadcasted_iota(I32, (LANES, t), 0)
    for p in range(N_PAIRS):
        acc = acc_ref[p]
        l = l_ref[p]
        o = jnp.where(row < 64, acc[:, 0:t] / l[:, 0:t], acc[:, t:2 * t] / l[:, t:2 * t])
        o_ref[0, 0, p * LANES:(p + 1) * LANES, :] = o.astype(BF16)


def _to_kv_major(proj, col0, nq):
    b, s, _ = proj.shape
    w = N_PAIRS * LANES
    return jnp.swapaxes(proj[:, :, col0:col0 + w].reshape(b, nq, ATT_T, w), 2, 3)


def _attention(kind, tab, proj, dtiles, extra, cols):
    b, s, _ = proj.shape
    t = ATT_T
    nq = s // t
    w = N_PAIRS * LANES
    qc, kc, vc = cols
    gidx = 0 if kind == "diff" else 1
    q_t = _to_kv_major(proj, qc, nq)
    v_t = _to_kv_major(proj, vc, nq)
    in_specs = [pl.BlockSpec(memory_space=pltpu.SMEM),
                pl.BlockSpec((1, 1, w, t), lambda bi, i: (bi, i, 0, 0)),
                pl.BlockSpec((1, s, w), lambda bi, i: (bi, 0, kc // w)),
                pl.BlockSpec((1, nq, w, t), lambda bi, i: (bi, 0, 0, 0)),
                pl.BlockSpec((1, N_PAIRS, 2, t, 2 * t), lambda bi, i: (gidx, 0, 0, 0, 0))]
    scratch = [pltpu.VMEM((N_PAIRS, LANES, 2 * t), BF16),
               pltpu.VMEM((N_PAIRS, 1, 2 * t), F32),
               pltpu.VMEM((N_PAIRS, 1, 2 * t), F32),
               pltpu.VMEM((N_PAIRS, LANES, 2 * t), F32)]
    if kind == "diff":
        lamv, subln_g = extra
        in_specs += [pl.BlockSpec(lamv.shape, lambda bi, i: (0, 0)),
                     pl.BlockSpec(subln_g.shape, lambda bi, i: (0, 0))]
        args = (tab, q_t, proj, v_t, dtiles, lamv, subln_g)
        body = _diff_kernel
    else:
        nbp = max(32, nq)
        scratch += [pltpu.VMEM((N_PAIRS, nbp, LANES), F32), pltpu.VMEM((N_PAIRS, nbp, 2 * t), F32)]
        args = (tab, q_t, proj, v_t, dtiles)
        body = _moba_kernel
    y_t = pl.pallas_call(
        body,
        out_shape=jax.ShapeDtypeStruct((b, nq, w, t), BF16),
        grid=(b, nq),
        in_specs=in_specs,
        out_specs=pl.BlockSpec((1, 1, w, t), lambda bi, i: (bi, i, 0, 0)),
        scratch_shapes=scratch,
        compiler_params=pltpu.CompilerParams(
            dimension_semantics=("arbitrary", "arbitrary"),
            vmem_limit_bytes=VMEM_LIMIT_BYTES),
        name=kind + "_attn",
    )(*args)
    return jnp.swapaxes(y_t, 2, 3).reshape(b, s, w)


def _merge_kernel(x_ref, ya_ref, yb_ref, gla_ref, glb_ref, wa_ref, wb_ref, wo_ref,
                  gt_ref, shf_ref, scf_ref, gf_ref, wr_ref, br_ref,
                  x1_ref, h2_ref, ti_ref, tw_ref):
    a = jnp.dot(ya_ref[0], wa_ref[...], preferred_element_type=F32)
    b = jnp.dot(yb_ref[0], wb_ref[...], preferred_element_type=F32)
    merged = (jax.nn.sigmoid(gla_ref[0].astype(F32)) * a
              + jax.nn.sigmoid(glb_ref[0].astype(F32)) * b)
    mo = jnp.dot(merged.astype(BF16), wo_ref[...], preferred_element_type=F32)
    x1 = x_ref[0] + gt_ref[0] * mo
    x1_ref[0] = x1
    h = _rms(x1, gf_ref[...]) * (1.0 + scf_ref[0]) + shf_ref[0]
    h2_ref[0] = h
    logits = jnp.dot(h.astype(BF16), wr_ref[...], preferred_element_type=F32) + br_ref[...]
    tm = logits.shape[0]
    lane = lax.broadcasted_iota(I32, (tm, LANES), 1)
    lane_f = lane.astype(F32)
    g = jnp.where(lane < N_EXPERTS, logits, -jnp.inf)
    vals, idxs = [], []
    for _ in range(TOP_K):
        mx = jnp.max(g, axis=1, keepdims=True)
        first = jnp.min(jnp.where(g == mx, lane_f, float(LANES)), axis=1, keepdims=True)
        vals.append(mx)
        idxs.append(first)
        g = jnp.where(lane_f == first, -jnp.inf, g)
    es = [jnp.exp(v - vals[0]) for v in vals]
    den = es[0] + es[1] + es[2] + es[3]
    ti = jnp.zeros((tm, LANES), F32)
    tw = jnp.zeros((tm, LANES), F32)
    for k in range(TOP_K):
        ti = jnp.where(lane == k, idxs[k], ti)
        tw = jnp.where(lane == k, es[k] / den, tw)
    ti_ref[0] = ti.astype(I32)
    tw_ref[0] = tw


def _merge(x, proj, y_a, y_b, w_br_a, w_br_b, w_out, mod, g_ffn, w_router, b_router):
    b, s, d = x.shape
    tm = min(MERGE_TM, s)
    full = lambda shape: pl.BlockSpec(shape, lambda bi, i: (0,) * len(shape))
    tile = lambda w, c=0: pl.BlockSpec((1, tm, w), lambda bi, i: (bi, i, c))
    modspec = lambda j: pl.BlockSpec((1, 1, d), lambda bi, i: (bi * 6 + j, 0, 0))
    return pl.pallas_call(
        _merge_kernel,
        out_shape=(jax.ShapeDtypeStruct((b, s, d), F32),
                   jax.ShapeDtypeStruct((b, s, d), F32),
                   jax.ShapeDtypeStruct((b, s, LANES), I32),
                   jax.ShapeDtypeStruct((b, s, LANES), F32)),
        grid=(b, s // tm),
        in_specs=[tile(d), tile(512), tile(512), tile(d, 3), tile(d, 4),
                  full(w_br_a.shape), full(w_br_b.shape), full(w_out.shape),
                  modspec(2), modspec(3), modspec(4), full(g_ffn.shape),
                  full(w_router.shape), full(b_router.shape)],
        out_specs=(tile(d), tile(d), tile(LANES), tile(LANES)),
        compiler_params=pltpu.CompilerParams(dimension_semantics=("arbitrary", "arbitrary"),
                                             vmem_limit_bytes=VMEM_LIMIT_BYTES),
        name="merge",
    )(x, y_a, y_b, proj, proj, w_br_a, w_br_b, w_out, mod, mod, mod, g_ffn,
      w_router, b_router)


def _moe_kernel(te_ref, nu_ref, tok_ref, tok_next_ref, h2_ref, wgu_ref, bgu_ref, wd_ref, bd_ref,
                ys_ref, xbuf, sems, wgu_bf, wd_bf):
    r = pl.program_id(0)
    n_used = nu_ref[0]
    valid = r < n_used
    slot = r % 2
    tm = xbuf.shape[1]

    def gather(idx_ref, s):
        def body(j, carry):
            pltpu.make_async_copy(h2_ref.at[pl.ds(idx_ref[j], 1)],
                                  xbuf.at[s, pl.ds(j, 1)], sems.at[s]).start()
            return carry
        lax.fori_loop(0, tm, body, 0)

    @pl.when(r == 0)
    def _():
        gather(tok_ref, 0)

    @pl.when(r + 1 < n_used)
    def _():
        gather(tok_next_ref, 1 - slot)

    prev = te_ref[jnp.maximum(r - 1, 0)]
    fresh = jnp.logical_or(r == 0, te_ref[r] != prev)

    @pl.when(jnp.logical_and(valid, fresh))
    def _():
        wgu_bf[...] = wgu_ref[0].astype(BF16)
        wd_bf[...] = wd_ref[0].astype(BF16)

    @pl.when(valid)
    def _():
        pltpu.make_async_copy(h2_ref.at[pl.ds(0, tm)], xbuf.at[slot], sems.at[slot]).wait()
        x = xbuf[slot].astype(BF16)
        gu = jnp.dot(x, wgu_bf[...], preferred_element_type=F32) + bgu_ref[0]
        gate = jnp.minimum(gu[:, :D_FF], SWIGLU_LIMIT)
        up = jnp.clip(gu[:, D_FF:], -SWIGLU_LIMIT, SWIGLU_LIMIT)
        act = (up + 1.0) * gate * jax.nn.sigmoid(SWIGLU_ALPHA * gate)
        ys_ref[...] = jnp.dot(act.astype(BF16), wd_bf[...],
                              preferred_element_type=F32) + bd_ref[0]


def _moe(h2, row_token, tile_expert, n_used, w_gate_up, b_gate_up, w_down, b_down):
    _, d = h2.shape
    tm = MOE_TM
    n_tiles = tile_expert.shape[0]
    e, _, f2 = w_gate_up.shape
    row = lambda r, te, nu: (jnp.minimum(r, nu[0] - 1), 0)
    return pl.pallas_call(
        _moe_kernel,
        out_shape=jax.ShapeDtypeStruct((n_tiles * tm, d), F32),
        grid_spec=pltpu.PrefetchScalarGridSpec(
            num_scalar_prefetch=2,
            grid=(n_tiles,),
            in_specs=[pl.BlockSpec((tm,), lambda r, te, nu: (r,), memory_space=pltpu.SMEM),
                      pl.BlockSpec((tm,), lambda r, te, nu: (jnp.minimum(r + 1, n_tiles - 1),),
                                   memory_space=pltpu.SMEM),
                      pl.BlockSpec(memory_space=pl.ANY),
                      pl.BlockSpec((1, d, f2), lambda r, te, nu: (te[r], 0, 0)),
                      pl.BlockSpec((1, 1, f2), lambda r, te, nu: (te[r], 0, 0)),
                      pl.BlockSpec((1, D_FF, d), lambda r, te, nu: (te[r], 0, 0)),
                      pl.BlockSpec((1, 1, d), lambda r, te, nu: (te[r], 0, 0))],
            out_specs=pl.BlockSpec((tm, d), row),
            scratch_shapes=[pltpu.VMEM((2, tm, d), F32), pltpu.SemaphoreType.DMA((2,)),
                            pltpu.VMEM((d, f2), BF16), pltpu.VMEM((D_FF, d), BF16)]),
        compiler_params=pltpu.CompilerParams(dimension_semantics=("arbitrary",),
                                             vmem_limit_bytes=VMEM_LIMIT_BYTES),
        name="moe",
    )(tile_expert, n_used, row_token, row_token, h2, w_gate_up, b_gate_up, w_down, b_down)


def _combine_kernel(pos_ref, ys_ref, x1_ref, tw_ref, gt_ref, gf_ref, o_ref, gbuf, sem):
    tm = x1_ref.shape[0]

    def issue(j, carry):
        for k in range(TOP_K):
            pltpu.make_async_copy(ys_ref.at[pl.ds(pos_ref[j * TOP_K + k], 1)],
                                  gbuf.at[k, pl.ds(j, 1)], sem).start()
        return carry

    lax.fori_loop(0, tm, issue, 0)

    def drain(j, carry):
        for k in range(TOP_K):
            pltpu.make_async_copy(ys_ref.at[pl.ds(0, 1)], gbuf.at[k, pl.ds(0, 1)], sem).wait()
        return carry

    lax.fori_loop(0, tm, drain, 0)

    tw = tw_ref[...]
    acc = tw[:, 0:1] * gbuf[0]
    for k in range(1, TOP_K):
        acc = acc + tw[:, k:k + 1] * gbuf[k]
    x2 = x1_ref[...] + gt_ref[0] * acc
    o_ref[...] = _rms(x2, gf_ref[...])


def _combine(ys, pos_flat, x1, tw, mod, g_final, seq):
    t, d = x1.shape
    tm = min(COMBINE_TM, t)
    return pl.pallas_call(
        _combine_kernel,
        out_shape=jax.ShapeDtypeStruct((t, d), F32),
        grid=(t // tm,),
        in_specs=[pl.BlockSpec((tm * TOP_K,), lambda s: (s,), memory_space=pltpu.SMEM),
                  pl.BlockSpec(memory_space=pl.ANY),
                  pl.BlockSpec((tm, d), lambda s: (s, 0)),
                  pl.BlockSpec((tm, LANES), lambda s: (s, 0)),
                  pl.BlockSpec((1, 1, d), lambda s: ((s * tm) // seq * 6 + 5, 0, 0)),
                  pl.BlockSpec((1, d), lambda s: (0, 0))],
        out_specs=pl.BlockSpec((tm, d), lambda s: (s, 0)),
        scratch_shapes=[pltpu.VMEM((TOP_K, tm, d), F32), pltpu.SemaphoreType.DMA],
        compiler_params=pltpu.CompilerParams(dimension_semantics=("arbitrary",),
                                             vmem_limit_bytes=VMEM_LIMIT_BYTES),
        name="combine",
    )(pos_flat, ys, x1, tw, mod, g_final)


def _routing_tables(top_idx, n_tiles):
    tm = MOE_TM
    e_flat = top_idx.reshape(-1)
    onehot = (e_flat[:, None] == jnp.arange(N_EXPERTS, dtype=I32)[None, :]).astype(I32)
    csum = jnp.cumsum(onehot, axis=0)
    counts = csum[-1]
    rank = jnp.sum(csum * onehot, axis=1) - 1
    tiles_per = (counts + tm - 1) // tm
    tile_end = jnp.cumsum(tiles_per)
    tile_start = tile_end - tiles_per
    pos = (jnp.sum(onehot * tile_start[None, :], axis=1) * tm + rank).astype(I32)
    n_used = tile_end[-1:].astype(I32)
    tile_ids = jnp.arange(n_tiles, dtype=I32)
    tile_expert = jnp.minimum(
        jnp.sum((tile_ids[:, None] >= tile_end[None, :]).astype(I32), axis=1),
        N_EXPERTS - 1).astype(I32)
    order = jnp.argsort(e_flat, stable=True).astype(I32)
    first_pair = jnp.cumsum(counts) - counts
    rows = jnp.arange(n_tiles * tm, dtype=I32)
    e_row = tile_expert[rows // tm]
    j_row = rows - tile_start[e_row] * tm
    live = jnp.logical_and(j_row < counts[e_row], rows // tm < n_used[0])
    src = jnp.clip(first_pair[e_row] + j_row, 0, e_flat.shape[0] - 1)
    row_token = jnp.where(live, order[src] // TOP_K, 0).astype(I32)
    return pos, tile_expert, n_used, row_token


def kernel(x, c, rel_bias, w_ada, b_ada, g_mix, w_in, lambda_q1, lambda_k1, lambda_q2,
           lambda_k2, subln_g, w_br_a, w_br_b, w_out, g_ffn, w_router, b_router,
           w_gate_up, b_gate_up, w_down, b_down, g_final):
    b, s, d = x.shape
    assert d == D_MODEL and s % ATT_T == 0 and MOBA_BLOCK == ATT_T
    t_tok = b * s

    c8 = jnp.zeros((8, d), F32).at[:b].set(c)
    ada = _ada(c8, w_ada[0], b_ada[0].reshape(1, -1))
    mod = ada[:b].reshape(b * 6, 1, d)

    dtiles = _bias_tiles(rel_bias)
    tab = rel_bias.T.reshape(-1)

    proj = _proj(x, g_mix[0].reshape(1, d), mod, w_in[0].astype(BF16))

    lamv = jnp.concatenate([lambda_q1, lambda_k1, lambda_q2, lambda_k2], axis=0)
    y_a = _attention("diff", tab, proj, dtiles, (lamv, subln_g[0].reshape(-1, 1)), (0, 512, 1024))
    y_b = _attention("moba", tab, proj, dtiles, None, (1536, 2048, 2560))

    wr = jnp.zeros((d, LANES), BF16).at[:, :N_EXPERTS].set(w_router[0].astype(BF16))
    br = jnp.zeros((1, LANES), F32).at[0, :N_EXPERTS].set(b_router[0])
    x1, h2, ti, tw = _merge(x, proj, y_a, y_b, w_br_a[0].astype(BF16), w_br_b[0].astype(BF16),
                            w_out[0].astype(BF16), mod, g_ffn[0].reshape(1, d), wr, br)

    n_tiles = (t_tok * TOP_K) // MOE_TM + N_EXPERTS
    top_idx = ti.reshape(t_tok, LANES)[:, :TOP_K]
    pos, tile_expert, n_used, row_token = _routing_tables(top_idx, n_tiles)

    ys = _moe(h2.reshape(t_tok, d), row_token, tile_expert, n_used, w_gate_up[0],
              b_gate_up[0].reshape(N_EXPERTS, 1, -1), w_down[0],
              b_down[0].reshape(N_EXPERTS, 1, -1))
    out = _combine(ys, pos, x1.reshape(t_tok, d), tw.reshape(t_tok, LANES), mod,
                   g_final.reshape(1, d), s)
    return out.reshape(b, s, d)
```

```python
import functools
import math

import jax
import jax.numpy as jnp
import numpy as np
from jax import lax
from jax.experimental import pallas as pl
from jax.experimental.pallas import tpu as pltpu

F32 = jnp.float32
BF16 = jnp.bfloat16
I32 = jnp.int32

D_MODEL = 1024
N_DIFF_HEADS = 4
DIFF_HEAD_DIM = 64
N_MOBA_HEADS = 8
MOBA_HEAD_DIM = 64
MOBA_BLOCK = 256
MOBA_TOPK = 3
N_HEADS_TOTAL = N_DIFF_HEADS + N_MOBA_HEADS
N_BUCKETS = 32
MAX_DISTANCE = 128
N_EXPERTS = 32
TOP_K = 4
D_FF = D_MODEL
SWIGLU_LIMIT = 7.0
SWIGLU_ALPHA = 1.702
RMS_EPS = 1e-5
LAM_INIT = 0.8 - 0.6 * math.exp(-0.3 * 0)
IN_WIDTH = 5120

LANES = 128
VMEM_LIMIT_BYTES = 56 * 1024 * 1024

ATT_T = 256
N_PAIRS = 4
PROJ_TM = 512
MERGE_TM = 512
MOE_TM = 256
MOE_FF_CHUNK = 256
COMBINE_TM = 256
NEG = -1e30


def _t5_bucket_np(dist):
    n = np.maximum(dist, 0)
    max_exact = N_BUCKETS // 2
    nf = np.maximum(n, max_exact).astype(np.float32)
    large = max_exact + (np.log(nf / max_exact) / math.log(MAX_DISTANCE / max_exact)
                         * (N_BUCKETS - max_exact)).astype(np.int32)
    large = np.minimum(large, N_BUCKETS - 1)
    return np.where(n < max_exact, n, large).astype(np.int32)


def _rms(x, g):
    return x * lax.rsqrt(jnp.mean(x * x, axis=-1, keepdims=True) + RMS_EPS) * g


def _ada_kernel(c_ref, w_ref, b_ref, o_ref):
    o_ref[...] = jnp.dot(c_ref[...].astype(BF16), w_ref[...].astype(BF16),
                         preferred_element_type=F32) + b_ref[...]


def _ada(c8, w, b):
    d, n = w.shape
    tn = 1536
    return pl.pallas_call(
        _ada_kernel,
        out_shape=jax.ShapeDtypeStruct((8, n), F32),
        grid=(n // tn,),
        in_specs=[pl.BlockSpec((8, d), lambda j: (0, 0)),
                  pl.BlockSpec((d, tn), lambda j: (0, j)),
                  pl.BlockSpec((1, tn), lambda j: (0, j))],
        out_specs=pl.BlockSpec((8, tn), lambda j: (0, j)),
        compiler_params=pltpu.CompilerParams(dimension_semantics=("arbitrary",),
                                             vmem_limit_bytes=VMEM_LIMIT_BYTES),
        name="ada",
    )(c8, w, b)


def _bias_kernel(tab_ref, bd_ref, bp_ref, o_ref):
    g = pl.program_id(0)
    p = pl.program_id(1)
    t = pl.program_id(2)
    head = jnp.where(g == 0, p, N_DIFF_HEADS + 2 * p + t)
    bd = bd_ref[...]
    bp = bp_ref[...]
    d0 = jnp.where(bd < 0, NEG, 0.0).astype(F32)
    d1 = jnp.zeros(bp.shape, F32)
    for b in range(N_BUCKETS):
        val = tab_ref[head * N_BUCKETS + b]
        d0 = jnp.where(bd == b, val, d0)
        d1 = jnp.where(bp == b, val, d1)
    o_ref[0, 0, 0] = d0
    o_ref[0, 0, 1] = d1


def _bias_tiles(rel_bias):
    t = ATT_T
    jj = np.arange(t)[:, None]
    ii = np.arange(t)[None, :]
    bd = np.where(ii >= jj, _t5_bucket_np(ii - jj), -1).astype(np.int32)
    bp = _t5_bucket_np(t + ii - jj)
    tab = rel_bias.T.reshape(-1)
    return pl.pallas_call(
        _bias_kernel,
        out_shape=jax.ShapeDtypeStruct((2, N_PAIRS, 2, t, 2 * t), F32),
        grid=(2, N_PAIRS, 2),
        in_specs=[pl.BlockSpec(memory_space=pltpu.SMEM),
                  pl.BlockSpec((t, t), lambda g, p, h: (0, 0)),
                  pl.BlockSpec((t, t), lambda g, p, h: (0, 0))],
        out_specs=pl.BlockSpec((1, 1, 2, t, t), lambda g, p, h: (g, p, 0, 0, h)),
        compiler_params=pltpu.CompilerParams(
            dimension_semantics=("arbitrary", "arbitrary", "arbitrary")),
        name="bias_tiles",
    )(tab, jnp.asarray(bd), jnp.asarray(bp))


def _proj_kernel(x_ref, g_ref, sh_ref, sc_ref, w_ref, o_ref):
    x = x_ref[0]
    h = _rms(x, g_ref[...]) * (1.0 + sc_ref[0]) + sh_ref[0]
    hb = h.astype(BF16)
    n_total = w_ref.shape[1]
    for n0 in range(0, n_total, 512):
        o_ref[0, :, n0:n0 + 512] = jnp.dot(
            hb, w_ref[:, n0:n0 + 512], preferred_element_type=F32).astype(BF16)


def _proj(x, g_mix, mod, w_in_bf):
    b, s, d = x.shape
    n = w_in_bf.shape[1]
    tm = min(PROJ_TM, s)
    return pl.pallas_call(
        _proj_kernel,
        out_shape=jax.ShapeDtypeStruct((b, s, n), BF16),
        grid=(b, s // tm),
        in_specs=[pl.BlockSpec((1, tm, d), lambda bi, i: (bi, i, 0)),
                  pl.BlockSpec((1, d), lambda bi, i: (0, 0)),
                  pl.BlockSpec((1, 1, d), lambda bi, i: (bi * 6 + 0, 0, 0)),
                  pl.BlockSpec((1, 1, d), lambda bi, i: (bi * 6 + 1, 0, 0)),
                  pl.BlockSpec((d, n), lambda bi, i: (0, 0))],
        out_specs=pl.BlockSpec((1, tm, n), lambda bi, i: (bi, i, 0)),
        compiler_params=pltpu.CompilerParams(dimension_semantics=("arbitrary", "arbitrary"),
                                             vmem_limit_bytes=VMEM_LIMIT_BYTES),
        name="proj",
    )(x, g_mix, mod, mod, w_in_bf)


def _stack_q(q_ref, q2_ref, scale):
    t = ATT_T
    row = lax.broadcasted_iota(I32, (LANES, t), 0)
    for p in range(N_PAIRS):
        q = q_ref[0, 0, p * LANES:(p + 1) * LANES, :].astype(F32) * scale
        q2_ref[p, :, 0:t] = jnp.where(row < 64, q, 0.0).astype(BF16)
        q2_ref[p, :, t:2 * t] = jnp.where(row >= 64, q, 0.0).astype(BF16)


def _flash_init(m_ref, l_ref, acc_ref):
    m_ref[...] = jnp.full(m_ref.shape, NEG, F32)
    l_ref[...] = jnp.zeros(l_ref.shape, F32)
    acc_ref[...] = jnp.zeros(acc_ref.shape, F32)


def _flash_update(p, n, add, q2_ref, k_ref, v_ref, m_ref, l_ref, acc_ref):
    t = ATT_T
    kt = k_ref[0, pl.ds(pl.multiple_of(n * t, t), t), p * LANES:(p + 1) * LANES]
    vt = v_ref[0, n, p * LANES:(p + 1) * LANES, :]
    s = jnp.dot(kt, q2_ref[p], preferred_element_type=F32) + add
    m_prev = m_ref[p]
    m_new = jnp.maximum(m_prev, jnp.max(s, axis=0, keepdims=True))
    alpha = jnp.exp(m_prev - m_new)
    pt = jnp.exp(s - m_new)
    l_ref[p] = alpha * l_ref[p] + jnp.sum(pt, axis=0, keepdims=True)
    acc_ref[p] = alpha * acc_ref[p] + jnp.dot(vt, pt.astype(BF16), preferred_element_type=F32)
    m_ref[p] = m_new


def _far_bias_row(tab_ref, head_left, head_right):
    t = ATT_T
    c_left = tab_ref[head_left * N_BUCKETS + (N_BUCKETS - 1)]
    c_right = tab_ref[head_right * N_BUCKETS + (N_BUCKETS - 1)]
    col = lax.broadcasted_iota(I32, (1, 2 * t), 1)
    return jnp.where(col < t, c_left, c_right)


def _flash_sweep(i, far_add, prev_add, own_add, refs):
    def far_body(n, carry):
        for p in range(N_PAIRS):
            _flash_update(p, n, far_add(p, n), *refs)
        return carry

    lax.fori_loop(0, jnp.maximum(i - 1, 0), far_body, 0)

    @pl.when(i >= 1)
    def _():
        for p in range(N_PAIRS):
            _flash_update(p, i - 1, prev_add(p), *refs)

    for p in range(N_PAIRS):
        _flash_update(p, i, own_add(p), *refs)


def _diff_kernel(tab_ref, q_ref, k_ref, v_ref, d_ref, lamv_ref, g_ref, o_ref,
                 q2_ref, m_ref, l_ref, acc_ref):
    t = ATT_T
    i = pl.program_id(1)
    _stack_q(q_ref, q2_ref, DIFF_HEAD_DIM ** -0.5)
    _flash_init(m_ref, l_ref, acc_ref)
    refs = (q2_ref, k_ref, v_ref, m_ref, l_ref, acc_ref)
    _flash_sweep(i,
                 lambda p, n: _far_bias_row(tab_ref, p, p),
                 lambda p: d_ref[0, p, 1],
                 lambda p: d_ref[0, p, 0],
                 refs)

    lv = lamv_ref[...]
    lam = (jnp.exp(jnp.sum(lv[0:1] * lv[1:2], axis=1, keepdims=True))
           - jnp.exp(jnp.sum(lv[2:3] * lv[3:4], axis=1, keepdims=True)) + LAM_INIT)
    for p in range(N_PAIRS):
        acc = acc_ref[p]
        l = l_ref[p]
        o = acc[:, 0:t] / l[:, 0:t] - lam * (acc[:, t:2 * t] / l[:, t:2 * t])
        y = o * lax.rsqrt(jnp.mean(o * o, axis=0, keepdims=True) + RMS_EPS) * g_ref[...]
        o_ref[0, 0, p * LANES:(p + 1) * LANES, :] = (y * (1.0 - LAM_INIT)).astype(BF16)


def _moba_kernel(tab_ref, q_ref, k_ref, v_ref, d_ref, o_ref,
                 q2_ref, m_ref, l_ref, acc_ref, kmean_ref, sel_ref):
    t = ATT_T
    i = pl.program_id(1)
    nb = k_ref.shape[1] // t
    nbp = kmean_ref.shape[1]

    @pl.when(i == 0)
    def _():
        kmean_ref[...] = jnp.zeros(kmean_ref.shape, F32)
        for p in range(N_PAIRS):
            for n in range(nb):
                blk = k_ref[0, n * t:(n + 1) * t, p * LANES:(p + 1) * LANES].astype(F32)
                kmean_ref[p, n:n + 1, :] = jnp.sum(blk, axis=0, keepdims=True) * (1.0 / t)

    _stack_q(q_ref, q2_ref, MOBA_HEAD_DIM ** -0.5)
    _flash_init(m_ref, l_ref, acc_ref)

    blk_id = lax.broadcasted_iota(I32, (nbp, 2 * t), 0)
    blk_f = blk_id.astype(F32)
    for p in range(N_PAIRS):
        gs = jnp.dot(kmean_ref[p].astype(BF16), q2_ref[p], preferred_element_type=F32)
        g = jnp.where(blk_id < i, gs, -jnp.inf)
        sel = jnp.full((nbp, 2 * t), NEG, F32)
        for _ in range(MOBA_TOPK):
            mx = jnp.max(g, axis=0, keepdims=True)
            first = jnp.min(jnp.where(g == mx, blk_f, float(nbp)), axis=0, keepdims=True)
            hit = blk_f == first
            sel = jnp.where(hit, 0.0, sel)
            g = jnp.where(hit, -jnp.inf, g)
        sel_ref[p] = sel

    def far_row(p):
        head = N_DIFF_HEADS + 2 * p
        return _far_bias_row(tab_ref, head, head + 1)

    refs = (q2_ref, k_ref, v_ref, m_ref, l_ref, acc_ref)
    _flash_sweep(i,
                 lambda p, n: far_row(p) + sel_ref[p, pl.ds(n, 1), :],
                 lambda p: d_ref[0, p, 1] + sel_ref[p, pl.ds(i - 1, 1), :],
                 lambda p: d_ref[0, p, 0],
                 refs)

    row = lax.broadcasted_iota(I32, (LANES, t), 0)
    for p in range(N_PAIRS):
        acc = acc_ref[p]
        l = l_ref[p]
        o = jnp.where(row < 64, acc[:, 0:t] / l[:, 0:t], acc[:, t:2 * t] / l[:, t:2 * t])
        o_ref[0, 0, p * LANES:(p + 1) * LANES, :] = o.astype(BF16)


def _to_kv_major(proj, col0, nq):
    b, s, _ = proj.shape
    w = N_PAIRS * LANES
    return jnp.swapaxes(proj[:, :, col0:col0 + w].reshape(b, nq, ATT_T, w), 2, 3)


def _attention(kind, tab, proj, dtiles, extra, cols):
    b, s, _ = proj.shape
    t = ATT_T
    nq = s // t
    w = N_PAIRS * LANES
    qc, kc, vc = cols
    gidx = 0 if kind == "diff" else 1
    q_t = _to_kv_major(proj, qc, nq)
    v_t = _to_kv_major(proj, vc, nq)
    in_specs = [pl.BlockSpec(memory_space=pltpu.SMEM),
                pl.BlockSpec((1, 1, w, t), lambda bi, i: (bi, i, 0, 0)),
                pl.BlockSpec((1, s, w), lambda bi, i: (bi, 0, kc // w)),
                pl.BlockSpec((1, nq, w, t), lambda bi, i: (bi, 0, 0, 0)),
                pl.BlockSpec((1, N_PAIRS, 2, t, 2 * t), lambda bi, i: (gidx, 0, 0, 0, 0))]
    scratch = [pltpu.VMEM((N_PAIRS, LANES, 2 * t), BF16),
               pltpu.VMEM((N_PAIRS, 1, 2 * t), F32),
               pltpu.VMEM((N_PAIRS, 1, 2 * t), F32),
               pltpu.VMEM((N_PAIRS, LANES, 2 * t), F32)]
    if kind == "diff":
        lamv, subln_g = extra
        in_specs += [pl.BlockSpec(lamv.shape, lambda bi, i: (0, 0)),
                     pl.BlockSpec(subln_g.shape, lambda bi, i: (0, 0))]
        args = (tab, q_t, proj, v_t, dtiles, lamv, subln_g)
        body = _diff_kernel
    else:
        nbp = max(32, nq)
        scratch += [pltpu.VMEM((N_PAIRS, nbp, LANES), F32), pltpu.VMEM((N_PAIRS, nbp, 2 * t), F32)]
        args = (tab, q_t, proj, v_t, dtiles)
        body = _moba_kernel
    y_t = pl.pallas_call(
        body,
        out_shape=jax.ShapeDtypeStruct((b, nq, w, t), BF16),
        grid=(b, nq),
        in_specs=in_specs,
        out_specs=pl.BlockSpec((1, 1, w, t), lambda bi, i: (bi, i, 0, 0)),
        scratch_shapes=scratch,
        compiler_params=pltpu.CompilerParams(
            dimension_semantics=("arbitrary", "arbitrary"),
            vmem_limit_bytes=VMEM_LIMIT_BYTES),
        name=kind + "_attn",
    )(*args)
    return jnp.swapaxes(y_t, 2, 3).reshape(b, s, w)


def _merge_kernel(x_ref, ya_ref, yb_ref, gla_ref, glb_ref, wa_ref, wb_ref, wo_ref,
                  gt_ref, shf_ref, scf_ref, gf_ref, wr_ref, br_ref,
                  x1_ref, h2_ref, ti_ref, tw_ref):
    a = jnp.dot(ya_ref[0], wa_ref[...], preferred_element_type=F32)
    b = jnp.dot(yb_ref[0], wb_ref[...], preferred_element_type=F32)
    merged = (jax.nn.sigmoid(gla_ref[0].astype(F32)) * a
              + jax.nn.sigmoid(glb_ref[0].astype(F32)) * b)
    mo = jnp.dot(merged.astype(BF16), wo_ref[...], preferred_element_type=F32)
    x1 = x_ref[0] + gt_ref[0] * mo
    x1_ref[0] = x1
    h = _rms(x1, gf_ref[...]) * (1.0 + scf_ref[0]) + shf_ref[0]
    h2_ref[0] = h
    logits = jnp.dot(h.astype(BF16), wr_ref[...], preferred_element_type=F32) + br_ref[...]
    tm = logits.shape[0]
    lane = lax.broadcasted_iota(I32, (tm, LANES), 1)
    lane_f = lane.astype(F32)
    g = jnp.where(lane < N_EXPERTS, logits, -jnp.inf)
    vals, idxs = [], []
    for _ in range(TOP_K):
        mx = jnp.max(g, axis=1, keepdims=True)
        first = jnp.min(jnp.where(g == mx, lane_f, float(LANES)), axis=1, keepdims=True)
        vals.append(mx)
        idxs.append(first)
        g = jnp.where(lane_f == first, -jnp.inf, g)
    es = [jnp.exp(v - vals[0]) for v in vals]
    den = es[0] + es[1] + es[2] + es[3]
    ti = jnp.zeros((tm, LANES), F32)
    tw = jnp.zeros((tm, LANES), F32)
    for k in range(TOP_K):
        ti = jnp.where(lane == k, idxs[k], ti)
        tw = jnp.where(lane == k, es[k] / den, tw)
    ti_ref[0] = ti.astype(I32)
    tw_ref[0] = tw


def _merge(x, proj, y_a, y_b, w_br_a, w_br_b, w_out, mod, g_ffn, w_router, b_router):
    b, s, d = x.shape
    tm = min(MERGE_TM, s)
    full = lambda shape: pl.BlockSpec(shape, lambda bi, i: (0,) * len(shape))
    tile = lambda w, c=0: pl.BlockSpec((1, tm, w), lambda bi, i: (bi, i, c))
    modspec = lambda j: pl.BlockSpec((1, 1, d), lambda bi, i: (bi * 6 + j, 0, 0))
    return pl.pallas_call(
        _merge_kernel,
        out_shape=(jax.ShapeDtypeStruct((b, s, d), F32),
                   jax.ShapeDtypeStruct((b, s, d), F32),
                   jax.ShapeDtypeStruct((b, s, LANES), I32),
                   jax.ShapeDtypeStruct((b, s, LANES), F32)),
        grid=(b, s // tm),
        in_specs=[tile(d), tile(512), tile(512), tile(d, 3), tile(d, 4),
                  full(w_br_a.shape), full(w_br_b.shape), full(w_out.shape),
                  modspec(2), modspec(3), modspec(4), full(g_ffn.shape),
                  full(w_router.shape), full(b_router.shape)],
        out_specs=(tile(d), tile(d), tile(LANES), tile(LANES)),
        compiler_params=pltpu.CompilerParams(dimension_semantics=("arbitrary", "arbitrary"),
                                             vmem_limit_bytes=VMEM_LIMIT_BYTES),
        name="merge",
    )(x, y_a, y_b, proj, proj, w_br_a, w_br_b, w_out, mod, mod, mod, g_ffn,
      w_router, b_router)


def _moe_kernel(te_ref, nu_ref, base_ref, tok_a, tok_b, h2_ref, wgu_ref, bgu_ref, wd_ref, bd_ref,
                ys_ref, xbuf, sems, wgu_bf, wd_bf):
    r = pl.program_id(0)
    n_used = nu_ref[0]
    n_tiles = te_ref.shape[0]
    tm = xbuf.shape[1]
    cur = r % 2
    prv = 1 - cur
    off = jnp.bitwise_and(base_ref[jnp.minimum(r, n_tiles - 1)], tm - 1)

    def start_row(j):
        idx = off + j
        tok = jnp.where(idx < tm, tok_a[jnp.minimum(idx, tm - 1)], tok_b[jnp.maximum(idx - tm, 0)])
        pltpu.make_async_copy(h2_ref.at[pl.ds(tok, 1)], xbuf.at[cur, pl.ds(j, 1)],
                              sems.at[cur]).start()

    def wait_tile(s):
        pltpu.make_async_copy(h2_ref.at[pl.ds(0, tm)], xbuf.at[s], sems.at[s]).wait()

    @pl.when(r == 0)
    def _():
        def body(j, carry):
            start_row(j)
            return carry
        lax.fori_loop(0, tm, body, 0)

    busy = jnp.logical_and(r >= 1, r <= n_used)
    e_prev = te_ref[jnp.maximum(r - 1, 0)]
    fresh = jnp.logical_or(r == 1, e_prev != te_ref[jnp.maximum(r - 2, 0)])

    @pl.when(jnp.logical_and(busy, fresh))
    def _():
        wgu_bf[...] = wgu_ref[0].astype(BF16)
        wd_bf[...] = wd_ref[0].astype(BF16)

    @pl.when(busy)
    def _():
        wait_tile(prv)
        x = xbuf[prv].astype(BF16)
        fc = MOE_FF_CHUNK
        n_chunks = D_FF // fc
        rows_per = tm // n_chunks
        y = None
        for c in range(n_chunks):
            lo, hi = c * fc, (c + 1) * fc
            gate = jnp.dot(x, wgu_bf[:, lo:hi], preferred_element_type=F32) + bgu_ref[0][:, lo:hi]
            up = (jnp.dot(x, wgu_bf[:, D_FF + lo:D_FF + hi], preferred_element_type=F32)
                  + bgu_ref[0][:, D_FF + lo:D_FF + hi])
            gate = jnp.minimum(gate, SWIGLU_LIMIT)
            up = jnp.clip(up, -SWIGLU_LIMIT, SWIGLU_LIMIT)
            act = (up + 1.0) * gate * jax.nn.sigmoid(SWIGLU_ALPHA * gate)
            part = jnp.dot(act.astype(BF16), wd_bf[lo:hi, :], preferred_element_type=F32)
            y = part if y is None else y + part
            for j in range(c * rows_per, (c + 1) * rows_per):
                start_row(j)
        ys_ref[...] = y + bd_ref[0]

        @pl.when(r == n_used)
        def _():
            wait_tile(cur)


def _moe(h2, tok_sorted, base, tile_expert, n_used, w_gate_up, b_gate_up, w_down, b_down):
    _, d = h2.shape
    tm = MOE_TM
    n_tiles = tile_expert.shape[0]
    e, _, f2 = w_gate_up.shape
    shift = tm.bit_length() - 1
    assert tm == 1 << shift and tok_sorted.shape[0] % tm == 0

    def tok_block(k):
        return lambda r, te, nu, bs: (
            jnp.right_shift(bs[jnp.minimum(r, n_tiles - 1)], shift) + k,)

    wsel = lambda r, te, nu, bs: (te[jnp.clip(r - 1, 0, n_tiles - 1)], 0, 0)
    return pl.pallas_call(
        _moe_kernel,
        out_shape=jax.ShapeDtypeStruct((n_tiles * tm, d), F32),
        grid_spec=pltpu.PrefetchScalarGridSpec(
            num_scalar_prefetch=3,
            grid=(n_tiles + 1,),
            in_specs=[pl.BlockSpec((tm,), tok_block(0), memory_space=pltpu.SMEM),
                      pl.BlockSpec((tm,), tok_block(1), memory_space=pltpu.SMEM),
                      pl.BlockSpec(memory_space=pl.ANY),
                      pl.BlockSpec((1, d, f2), wsel),
                      pl.BlockSpec((1, 1, f2), wsel),
                      pl.BlockSpec((1, D_FF, d), wsel),
                      pl.BlockSpec((1, 1, d), wsel)],
            out_specs=pl.BlockSpec(
                (tm, d), lambda r, te, nu, bs: (jnp.clip(r - 1, 0, nu[0] - 1), 0)),
            scratch_shapes=[pltpu.VMEM((2, tm, d), F32), pltpu.SemaphoreType.DMA((2,)),
                            pltpu.VMEM((d, f2), BF16), pltpu.VMEM((D_FF, d), BF16)]),
        compiler_params=pltpu.CompilerParams(dimension_semantics=("arbitrary",),
                                             vmem_limit_bytes=VMEM_LIMIT_BYTES),
        name="moe",
    )(tile_expert, n_used, base, tok_sorted, tok_sorted, h2, w_gate_up, b_gate_up, w_down,
      b_down)


def _combine_kernel(pos_ref, ys_ref, x1_ref, tw_ref, gt_ref, gf_ref, o_ref, gbuf, sem):
    tm = x1_ref.shape[0]

    def issue(j, carry):
        for k in range(TOP_K):
            pltpu.make_async_copy(ys_ref.at[pl.ds(pos_ref[j * TOP_K + k], 1)],
                                  gbuf.at[k, pl.ds(j, 1)], sem).start()
        return carry

    lax.fori_loop(0, tm, issue, 0)

    def drain(j, carry):
        for k in range(TOP_K):
            pltpu.make_async_copy(ys_ref.at[pl.ds(0, 1)], gbuf.at[k, pl.ds(0, 1)], sem).wait()
        return carry

    lax.fori_loop(0, tm, drain, 0)

    tw = tw_ref[...]
    acc = tw[:, 0:1] * gbuf[0]
    for k in range(1, TOP_K):
        acc = acc + tw[:, k:k + 1] * gbuf[k]
    x2 = x1_ref[...] + gt_ref[0] * acc
    o_ref[...] = _rms(x2, gf_ref[...])


def _combine(ys, pos_flat, x1, tw, mod, g_final, seq):
    t, d = x1.shape
    tm = min(COMBINE_TM, t)
    return pl.pallas_call(
        _combine_kernel,
        out_shape=jax.ShapeDtypeStruct((t, d), F32),
        grid=(t // tm,),
        in_specs=[pl.BlockSpec((tm * TOP_K,), lambda s: (s,), memory_space=pltpu.SMEM),
                  pl.BlockSpec(memory_space=pl.ANY),
                  pl.BlockSpec((tm, d), lambda s: (s, 0)),
                  pl.BlockSpec((tm, LANES), lambda s: (s, 0)),
                  pl.BlockSpec((1, 1, d), lambda s: ((s * tm) // seq * 6 + 5, 0, 0)),
                  pl.BlockSpec((1, d), lambda s: (0, 0))],
        out_specs=pl.BlockSpec((tm, d), lambda s: (s, 0)),
        scratch_shapes=[pltpu.VMEM((TOP_K, tm, d), F32), pltpu.SemaphoreType.DMA],
        compiler_params=pltpu.CompilerParams(dimension_semantics=("arbitrary",),
                                             vmem_limit_bytes=VMEM_LIMIT_BYTES),
        name="combine",
    )(pos_flat, ys, x1, tw, mod, g_final)


def _routing_tables(top_idx, n_tiles):
    tm = MOE_TM
    e_flat = top_idx.reshape(-1)
    onehot = (e_flat[:, None] == jnp.arange(N_EXPERTS, dtype=I32)[None, :]).astype(I32)
    csum = jnp.cumsum(onehot, axis=0)
    counts = csum[-1]
    rank = jnp.sum(csum * onehot, axis=1) - 1
    tiles_per = (counts + tm - 1) // tm
    tile_end = jnp.cumsum(tiles_per)
    tile_start = tile_end - tiles_per
    pos = (jnp.sum(onehot * tile_start[None, :], axis=1) * tm + rank).astype(I32)
    n_used = tile_end[-1:].astype(I32)
    tile_ids = jnp.arange(n_tiles, dtype=I32)
    tile_expert = jnp.minimum(
        jnp.sum((tile_ids[:, None] >= tile_end[None, :]).astype(I32), axis=1),
        N_EXPERTS - 1).astype(I32)
    order = jnp.argsort(e_flat, stable=True).astype(I32)
    tok_sorted = jnp.concatenate([order // TOP_K, jnp.zeros((tm,), I32)])
    first_pair = jnp.cumsum(counts) - counts
    te_onehot = (tile_expert[:, None] == jnp.arange(N_EXPERTS, dtype=I32)[None, :]).astype(I32)
    base = (jnp.sum(te_onehot * first_pair[None, :], axis=1)
            + (tile_ids - jnp.sum(te_onehot * tile_start[None, :], axis=1)) * tm)
    base = jnp.where(tile_ids < n_used[0], base, 0).astype(I32)
    return pos, tile_expert, n_used, tok_sorted, base


def kernel(x, c, rel_bias, w_ada, b_ada, g_mix, w_in, lambda_q1, lambda_k1, lambda_q2,
           lambda_k2, subln_g, w_br_a, w_br_b, w_out, g_ffn, w_router, b_router,
           w_gate_up, b_gate_up, w_down, b_down, g_final):
    b, s, d = x.shape
    assert d == D_MODEL and s % ATT_T == 0 and MOBA_BLOCK == ATT_T
    t_tok = b * s

    c8 = jnp.zeros((8, d), F32).at[:b].set(c)
    ada = _ada(c8, w_ada[0], b_ada[0].reshape(1, -1))
    mod = ada[:b].reshape(b * 6, 1, d)

    dtiles = _bias_tiles(rel_bias)
    tab = rel_bias.T.reshape(-1)

    proj = _proj(x, g_mix[0].reshape(1, d), mod, w_in[0].astype(BF16))

    lamv = jnp.concatenate([lambda_q1, lambda_k1, lambda_q2, lambda_k2], axis=0)
    y_a = _attention("diff", tab, proj, dtiles, (lamv, subln_g[0].reshape(-1, 1)), (0, 512, 1024))
    y_b = _attention("moba", tab, proj, dtiles, None, (1536, 2048, 2560))

    wr = jnp.zeros((d, LANES), BF16).at[:, :N_EXPERTS].set(w_router[0].astype(BF16))
    br = jnp.zeros((1, LANES), F32).at[0, :N_EXPERTS].set(b_router[0])
    x1, h2, ti, tw = _merge(x, proj, y_a, y_b, w_br_a[0].astype(BF16), w_br_b[0].astype(BF16),
                            w_out[0].astype(BF16), mod, g_ffn[0].reshape(1, d), wr, br)

    n_tiles = (t_tok * TOP_K) // MOE_TM + N_EXPERTS
    top_idx = ti.reshape(t_tok, LANES)[:, :TOP_K]
    pos, tile_expert, n_used, tok_sorted, base = _routing_tables(top_idx, n_tiles)

    ys = _moe(h2.reshape(t_tok, d), tok_sorted, base, tile_expert, n_used, w_gate_up[0],
              b_gate_up[0].reshape(N_EXPERTS, 1, -1), w_down[0],
              b_down[0].reshape(N_EXPERTS, 1, -1))
    out = _combine(ys, pos, x1.reshape(t_tok, d), tw.reshape(t_tok, LANES), mod,
                   g_final.reshape(1, d), s)
    return out.reshape(b, s, d)
```

```python
import functools
import math

import jax
import jax.numpy as jnp
import numpy as np
from jax import lax
from jax.experimental import pallas as pl
from jax.experimental.pallas import tpu as pltpu

F32 = jnp.float32
BF16 = jnp.bfloat16
I32 = jnp.int32

D_MODEL = 1024
N_DIFF_HEADS = 4
DIFF_HEAD_DIM = 64
N_MOBA_HEADS = 8
MOBA_HEAD_DIM = 64
MOBA_BLOCK = 256
MOBA_TOPK = 3
N_HEADS_TOTAL = N_DIFF_HEADS + N_MOBA_HEADS
N_BUCKETS = 32
MAX_DISTANCE = 128
N_EXPERTS = 32
TOP_K = 4
D_FF = D_MODEL
SWIGLU_LIMIT = 7.0
SWIGLU_ALPHA = 1.702
RMS_EPS = 1e-5
LAM_INIT = 0.8 - 0.6 * math.exp(-0.3 * 0)
IN_WIDTH = 5120

LANES = 128
VMEM_LIMIT_BYTES = 56 * 1024 * 1024

ATT_T = 256
N_PAIRS = 4
V_ROWS = LANES + 16
PROJ_TM = 512
MERGE_TM = 512
MOE_TM = 256
MOE_FF_CHUNK = 256
COMBINE_TM = 256
NEG = -1e30


def _t5_bucket_np(dist):
    n = np.maximum(dist, 0)
    max_exact = N_BUCKETS // 2
    nf = np.maximum(n, max_exact).astype(np.float32)
    large = max_exact + (np.log(nf / max_exact) / math.log(MAX_DISTANCE / max_exact)
                         * (N_BUCKETS - max_exact)).astype(np.int32)
    large = np.minimum(large, N_BUCKETS - 1)
    return np.where(n < max_exact, n, large).astype(np.int32)


def _rms(x, g):
    return x * lax.rsqrt(jnp.mean(x * x, axis=-1, keepdims=True) + RMS_EPS) * g


def _ada_kernel(c_ref, w_ref, b_ref, o_ref):
    o_ref[...] = jnp.dot(c_ref[...].astype(BF16), w_ref[...].astype(BF16),
                         preferred_element_type=F32) + b_ref[...]


def _ada(c8, w, b):
    d, n = w.shape
    tn = 1536
    return pl.pallas_call(
        _ada_kernel,
        out_shape=jax.ShapeDtypeStruct((8, n), F32),
        grid=(n // tn,),
        in_specs=[pl.BlockSpec((8, d), lambda j: (0, 0)),
                  pl.BlockSpec((d, tn), lambda j: (0, j)),
                  pl.BlockSpec((1, tn), lambda j: (0, j))],
        out_specs=pl.BlockSpec((8, tn), lambda j: (0, j)),
        compiler_params=pltpu.CompilerParams(dimension_semantics=("arbitrary",),
                                             vmem_limit_bytes=VMEM_LIMIT_BYTES),
        name="ada",
    )(c8, w, b)


def _bias_kernel(tab_ref, bd_ref, bp_ref, o_ref):
    g = pl.program_id(0)
    p = pl.program_id(1)
    t = pl.program_id(2)
    head = jnp.where(g == 0, p, N_DIFF_HEADS + 2 * p + t)
    bd = bd_ref[...]
    bp = bp_ref[...]
    d0 = jnp.where(bd < 0, NEG, 0.0).astype(F32)
    d1 = jnp.zeros(bp.shape, F32)
    for b in range(N_BUCKETS):
        val = tab_ref[head * N_BUCKETS + b]
        d0 = jnp.where(bd == b, val, d0)
        d1 = jnp.where(bp == b, val, d1)
    o_ref[0, 0, 0] = d0
    o_ref[0, 0, 1] = d1


def _bias_tiles(rel_bias):
    t = ATT_T
    jj = np.arange(t)[:, None]
    ii = np.arange(t)[None, :]
    bd = np.where(ii >= jj, _t5_bucket_np(ii - jj), -1).astype(np.int32)
    bp = _t5_bucket_np(t + ii - jj)
    tab = rel_bias.T.reshape(-1)
    return pl.pallas_call(
        _bias_kernel,
        out_shape=jax.ShapeDtypeStruct((2, N_PAIRS, 2, t, 2 * t), F32),
        grid=(2, N_PAIRS, 2),
        in_specs=[pl.BlockSpec(memory_space=pltpu.SMEM),
                  pl.BlockSpec((t, t), lambda g, p, h: (0, 0)),
                  pl.BlockSpec((t, t), lambda g, p, h: (0, 0))],
        out_specs=pl.BlockSpec((1, 1, 2, t, t), lambda g, p, h: (g, p, 0, 0, h)),
        compiler_params=pltpu.CompilerParams(
            dimension_semantics=("arbitrary", "arbitrary", "arbitrary")),
        name="bias_tiles",
    )(tab, jnp.asarray(bd), jnp.asarray(bp))


def _proj_kernel(x_ref, g_ref, sh_ref, sc_ref, w_ref, o_ref):
    x = x_ref[0]
    h = _rms(x, g_ref[...]) * (1.0 + sc_ref[0]) + sh_ref[0]
    hb = h.astype(BF16)
    n_total = w_ref.shape[1]
    for n0 in range(0, n_total, 512):
        o_ref[0, :, n0:n0 + 512] = jnp.dot(
            hb, w_ref[:, n0:n0 + 512], preferred_element_type=F32).astype(BF16)


def _proj(x, g_mix, mod, w_in_bf):
    b, s, d = x.shape
    n = w_in_bf.shape[1]
    tm = min(PROJ_TM, s)
    return pl.pallas_call(
        _proj_kernel,
        out_shape=jax.ShapeDtypeStruct((b, s, n), BF16),
        grid=(b, s // tm),
        in_specs=[pl.BlockSpec((1, tm, d), lambda bi, i: (bi, i, 0)),
                  pl.BlockSpec((1, d), lambda bi, i: (0, 0)),
                  pl.BlockSpec((1, 1, d), lambda bi, i: (bi * 6 + 0, 0, 0)),
                  pl.BlockSpec((1, 1, d), lambda bi, i: (bi * 6 + 1, 0, 0)),
                  pl.BlockSpec((d, n), lambda bi, i: (0, 0))],
        out_specs=pl.BlockSpec((1, tm, n), lambda bi, i: (bi, i, 0)),
        compiler_params=pltpu.CompilerParams(dimension_semantics=("arbitrary", "arbitrary"),
                                             vmem_limit_bytes=VMEM_LIMIT_BYTES),
        name="proj",
    )(x, g_mix, mod, mod, w_in_bf)


def _stack_q(q_ref, q2_ref, scale):
    t = ATT_T
    row = lax.broadcasted_iota(I32, (LANES, t), 0)
    for p in range(N_PAIRS):
        q = q_ref[0, 0, p * LANES:(p + 1) * LANES, :].astype(F32) * scale
        q2_ref[p, :, 0:t] = jnp.where(row < 64, q, 0.0).astype(BF16)
        q2_ref[p, :, t:2 * t] = jnp.where(row >= 64, q, 0.0).astype(BF16)


def _flash_init(m_ref, acc_ref):
    m_ref[...] = jnp.full(m_ref.shape, NEG, F32)
    acc_ref[...] = jnp.zeros(acc_ref.shape, F32)


def _flash_update(p, h, n, tile_bias, row_bias, live, q2_ref, k_ref, v_ref, m_ref, acc_ref):
    t = ATT_T
    cs = slice(h * t, (h + 1) * t)
    kt = k_ref[0, pl.ds(pl.multiple_of(n * t, t), t), p * LANES:(p + 1) * LANES]
    vt = v_ref[0, n, p * V_ROWS:(p + 1) * V_ROWS, :]
    s = jnp.dot(kt, q2_ref[p, :, cs], preferred_element_type=F32)
    if tile_bias is not None:
        s = s + tile_bias
    mx = jnp.max(s, axis=0, keepdims=True)
    if row_bias is not None:
        mx = mx + row_bias
    if live is not None:
        mx = jnp.where(live, mx, NEG)
    m_prev = m_ref[p, :, cs]
    m_new = jnp.maximum(m_prev, mx)
    alpha = jnp.exp(m_prev - m_new)
    shift = m_new if row_bias is None else m_new - row_bias
    if live is not None:
        shift = jnp.where(live, shift, -NEG)
    pt = jnp.exp(s - shift)
    acc_ref[p, :, cs] = alpha * acc_ref[p, :, cs] + jnp.dot(vt, pt.astype(BF16),
                                                            preferred_element_type=F32)
    m_ref[p, :, cs] = m_new


def _far_bias(tab_ref, head):
    return tab_ref[head * N_BUCKETS + (N_BUCKETS - 1)]


def _flash_sweep(i, far_args, prev_args, own_args, refs):
    chains = [(p, h) for p in range(N_PAIRS) for h in range(2)]

    def far_body(n, carry):
        for p, h in chains:
            _flash_update(p, h, n, *far_args(p, h, n), *refs)
        return carry

    lax.fori_loop(0, jnp.maximum(i - 1, 0), far_body, 0)

    @pl.when(i >= 1)
    def _():
        for p, h in chains:
            _flash_update(p, h, i - 1, *prev_args(p, h), *refs)

    for p, h in chains:
        _flash_update(p, h, i, *own_args(p, h), *refs)


def _diff_kernel(tab_ref, q_ref, k_ref, v_ref, d_ref, lamv_ref, g_ref, o_ref,
                 q2_ref, m_ref, acc_ref):
    t = ATT_T
    i = pl.program_id(1)
    _stack_q(q_ref, q2_ref, DIFF_HEAD_DIM ** -0.5)
    _flash_init(m_ref, acc_ref)
    refs = (q2_ref, k_ref, v_ref, m_ref, acc_ref)
    _flash_sweep(i,
                 lambda p, h, n: (None, _far_bias(tab_ref, p), None),
                 lambda p, h: (d_ref[0, p, 1, :, h * t:(h + 1) * t], None, None),
                 lambda p, h: (d_ref[0, p, 0, :, h * t:(h + 1) * t], None, None),
                 refs)

    lv = lamv_ref[...]
    lam = (jnp.exp(jnp.sum(lv[0:1] * lv[1:2], axis=1, keepdims=True))
           - jnp.exp(jnp.sum(lv[2:3] * lv[3:4], axis=1, keepdims=True)) + LAM_INIT)
    for p in range(N_PAIRS):
        acc = acc_ref[p, 0:LANES, :]
        l = acc_ref[p, LANES:LANES + 1, :]
        o = acc[:, 0:t] / l[:, 0:t] - lam * (acc[:, t:2 * t] / l[:, t:2 * t])
        y = o * lax.rsqrt(jnp.mean(o * o, axis=0, keepdims=True) + RMS_EPS) * g_ref[...]
        o_ref[0, 0, p * LANES:(p + 1) * LANES, :] = (y * (1.0 - LAM_INIT)).astype(BF16)


def _moba_kernel(tab_ref, q_ref, k_ref, v_ref, d_ref, o_ref,
                 q2_ref, m_ref, acc_ref, kmean_ref, sel_ref):
    t = ATT_T
    i = pl.program_id(1)
    nb = k_ref.shape[1] // t
    nbp = kmean_ref.shape[1]

    @pl.when(i == 0)
    def _():
        kmean_ref[...] = jnp.zeros(kmean_ref.shape, F32)
        for p in range(N_PAIRS):
            for n in range(nb):
                blk = k_ref[0, n * t:(n + 1) * t, p * LANES:(p + 1) * LANES].astype(F32)
                kmean_ref[p, n:n + 1, :] = jnp.sum(blk, axis=0, keepdims=True) * (1.0 / t)

    _stack_q(q_ref, q2_ref, MOBA_HEAD_DIM ** -0.5)
    _flash_init(m_ref, acc_ref)

    blk_id = lax.broadcasted_iota(I32, (nbp, 2 * t), 0)
    blk_f = blk_id.astype(F32)
    for p in range(N_PAIRS):
        gs = jnp.dot(kmean_ref[p].astype(BF16), q2_ref[p], preferred_element_type=F32)
        g = jnp.where(blk_id < i, gs, -jnp.inf)
        sel = jnp.full((nbp, 2 * t), NEG, F32)
        for _ in range(MOBA_TOPK):
            mx = jnp.max(g, axis=0, keepdims=True)
            first = jnp.min(jnp.where(g == mx, blk_f, float(nbp)), axis=0, keepdims=True)
            hit = blk_f == first
            sel = jnp.where(hit, 0.0, sel)
            g = jnp.where(hit, -jnp.inf, g)
        sel_ref[p] = sel

    def live(p, h, n):
        return sel_ref[p, pl.ds(n, 1), h * t:(h + 1) * t] > -1.0

    refs = (q2_ref, k_ref, v_ref, m_ref, acc_ref)
    _flash_sweep(i,
                 lambda p, h, n: (None, _far_bias(tab_ref, N_DIFF_HEADS + 2 * p + h),
                                  live(p, h, n)),
                 lambda p, h: (d_ref[0, p, 1, :, h * t:(h + 1) * t], None, live(p, h, i - 1)),
                 lambda p, h: (d_ref[0, p, 0, :, h * t:(h + 1) * t], None, None),
                 refs)

    row = lax.broadcasted_iota(I32, (LANES, t), 0)
    for p in range(N_PAIRS):
        acc = acc_ref[p, 0:LANES, :]
        l = acc_ref[p, LANES:LANES + 1, :]
        o = jnp.where(row < 64, acc[:, 0:t] / l[:, 0:t], acc[:, t:2 * t] / l[:, t:2 * t])
        o_ref[0, 0, p * LANES:(p + 1) * LANES, :] = o.astype(BF16)


def _to_kv_major(proj, col0, nq):
    b, s, _ = proj.shape
    w = N_PAIRS * LANES
    return jnp.swapaxes(proj[:, :, col0:col0 + w].reshape(b, nq, ATT_T, w), 2, 3)


def _values_kv_major(proj, col0, nq):
    b, s, _ = proj.shape
    v = proj[:, :, col0:col0 + N_PAIRS * LANES].reshape(b, nq, ATT_T, N_PAIRS, LANES)
    ones = jnp.ones((b, nq, ATT_T, N_PAIRS, V_ROWS - LANES), BF16)
    v = jnp.concatenate([v, ones], axis=-1)
    return jnp.transpose(v, (0, 1, 3, 4, 2)).reshape(b, nq, N_PAIRS * V_ROWS, ATT_T)


def _attention(kind, tab, proj, dtiles, extra, cols):
    b, s, _ = proj.shape
    t = ATT_T
    nq = s // t
    w = N_PAIRS * LANES
    qc, kc, vc = cols
    gidx = 0 if kind == "diff" else 1
    q_t = _to_kv_major(proj, qc, nq)
    v_t = _values_kv_major(proj, vc, nq)
    in_specs = [pl.BlockSpec(memory_space=pltpu.SMEM),
                pl.BlockSpec((1, 1, w, t), lambda bi, i: (bi, i, 0, 0)),
                pl.BlockSpec((1, s, w), lambda bi, i: (bi, 0, kc // w)),
                pl.BlockSpec((1, nq, N_PAIRS * V_ROWS, t), lambda bi, i: (bi, 0, 0, 0)),
                pl.BlockSpec((1, N_PAIRS, 2, t, 2 * t), lambda bi, i: (gidx, 0, 0, 0, 0))]
    scratch = [pltpu.VMEM((N_PAIRS, LANES, 2 * t), BF16),
               pltpu.VMEM((N_PAIRS, 1, 2 * t), F32),
               pltpu.VMEM((N_PAIRS, V_ROWS, 2 * t), F32)]
    if kind == "diff":
        lamv, subln_g = extra
        in_specs += [pl.BlockSpec(lamv.shape, lambda bi, i: (0, 0)),
                     pl.BlockSpec(subln_g.shape, lambda bi, i: (0, 0))]
        args = (tab, q_t, proj, v_t, dtiles, lamv, subln_g)
        body = _diff_kernel
    else:
        nbp = max(32, nq)
        scratch += [pltpu.VMEM((N_PAIRS, nbp, LANES), F32), pltpu.VMEM((N_PAIRS, nbp, 2 * t), F32)]
        args = (tab, q_t, proj, v_t, dtiles)
        body = _moba_kernel
    y_t = pl.pallas_call(
        body,
        out_shape=jax.ShapeDtypeStruct((b, nq, w, t), BF16),
        grid=(b, nq),
        in_specs=in_specs,
        out_specs=pl.BlockSpec((1, 1, w, t), lambda bi, i: (bi, i, 0, 0)),
        scratch_shapes=scratch,
        compiler_params=pltpu.CompilerParams(
            dimension_semantics=("arbitrary", "arbitrary"),
            vmem_limit_bytes=VMEM_LIMIT_BYTES),
        name=kind + "_attn",
    )(*args)
    return jnp.swapaxes(y_t, 2, 3).reshape(b, s, w)


def _merge_kernel(x_ref, ya_ref, yb_ref, gla_ref, glb_ref, wa_ref, wb_ref, wo_ref,
                  gt_ref, shf_ref, scf_ref, gf_ref, wr_ref, br_ref,
                  x1_ref, h2_ref, ti_ref, tw_ref):
    a = jnp.dot(ya_ref[0], wa_ref[...], preferred_element_type=F32)
    b = jnp.dot(yb_ref[0], wb_ref[...], preferred_element_type=F32)
    merged = (jax.nn.sigmoid(gla_ref[0].astype(F32)) * a
              + jax.nn.sigmoid(glb_ref[0].astype(F32)) * b)
    mo = jnp.dot(merged.astype(BF16), wo_ref[...], preferred_element_type=F32)
    x1 = x_ref[0] + gt_ref[0] * mo
    x1_ref[0] = x1
    h = _rms(x1, gf_ref[...]) * (1.0 + scf_ref[0]) + shf_ref[0]
    h2_ref[0] = h
    logits = jnp.dot(h.astype(BF16), wr_ref[...], preferred_element_type=F32) + br_ref[...]
    tm = logits.shape[0]
    lane = lax.broadcasted_iota(I32, (tm, LANES), 1)
    lane_f = lane.astype(F32)
    g = jnp.where(lane < N_EXPERTS, logits, -jnp.inf)
    vals, idxs = [], []
    for _ in range(TOP_K):
        mx = jnp.max(g, axis=1, keepdims=True)
        first = jnp.min(jnp.where(g == mx, lane_f, float(LANES)), axis=1, keepdims=True)
        vals.append(mx)
        idxs.append(first)
        g = jnp.where(lane_f == first, -jnp.inf, g)
    es = [jnp.exp(v - vals[0]) for v in vals]
    den = es[0] + es[1] + es[2] + es[3]
    ti = jnp.zeros((tm, LANES), F32)
    tw = jnp.zeros((tm, LANES), F32)
    for k in range(TOP_K):
        ti = jnp.where(lane == k, idxs[k], ti)
        tw = jnp.where(lane == k, es[k] / den, tw)
    ti_ref[0] = ti.astype(I32)
    tw_ref[0] = tw


def _merge(x, proj, y_a, y_b, w_br_a, w_br_b, w_out, mod, g_ffn, w_router, b_router):
    b, s, d = x.shape
    tm = min(MERGE_TM, s)
    full = lambda shape: pl.BlockSpec(shape, lambda bi, i: (0,) * len(shape))
    tile = lambda w, c=0: pl.BlockSpec((1, tm, w), lambda bi, i: (bi, i, c))
    modspec = lambda j: pl.BlockSpec((1, 1, d), lambda bi, i: (bi * 6 + j, 0, 0))
    return pl.pallas_call(
        _merge_kernel,
        out_shape=(jax.ShapeDtypeStruct((b, s, d), F32),
                   jax.ShapeDtypeStruct((b, s, d), F32),
                   jax.ShapeDtypeStruct((b, s, LANES), I32),
                   jax.ShapeDtypeStruct((b, s, LANES), F32)),
        grid=(b, s // tm),
        in_specs=[tile(d), tile(512), tile(512), tile(d, 3), tile(d, 4),
                  full(w_br_a.shape), full(w_br_b.shape), full(w_out.shape),
                  modspec(2), modspec(3), modspec(4), full(g_ffn.shape),
                  full(w_router.shape), full(b_router.shape)],
        out_specs=(tile(d), tile(d), tile(LANES), tile(LANES)),
        compiler_params=pltpu.CompilerParams(dimension_semantics=("arbitrary", "arbitrary"),
                                             vmem_limit_bytes=VMEM_LIMIT_BYTES),
        name="merge",
    )(x, y_a, y_b, proj, proj, w_br_a, w_br_b, w_out, mod, mod, mod, g_ffn,
      w_router, b_router)


def _moe_kernel(te_ref, nu_ref, base_ref, tok_a, tok_b, h2_ref, wgu_ref, bgu_ref, wd_ref, bd_ref,
                ys_ref, xbuf, sems, wgu_bf, wd_bf):
    r = pl.program_id(0)
    n_used = nu_ref[0]
    n_tiles = te_ref.shape[0]
    tm = xbuf.shape[1]
    cur = r % 2
    prv = 1 - cur
    off = jnp.bitwise_and(base_ref[jnp.minimum(r, n_tiles - 1)], tm - 1)

    def start_row(j):
        idx = off + j
        tok = jnp.where(idx < tm, tok_a[jnp.minimum(idx, tm - 1)], tok_b[jnp.maximum(idx - tm, 0)])
        pltpu.make_async_copy(h2_ref.at[pl.ds(tok, 1)], xbuf.at[cur, pl.ds(j, 1)],
                              sems.at[cur]).start()

    def wait_tile(s):
        pltpu.make_async_copy(h2_ref.at[pl.ds(0, tm)], xbuf.at[s], sems.at[s]).wait()

    @pl.when(r == 0)
    def _():
        def body(j, carry):
            start_row(j)
            return carry
        lax.fori_loop(0, tm, body, 0)

    busy = jnp.logical_and(r >= 1, r <= n_used)
    e_prev = te_ref[jnp.maximum(r - 1, 0)]
    fresh = jnp.logical_or(r == 1, e_prev != te_ref[jnp.maximum(r - 2, 0)])

    @pl.when(jnp.logical_and(busy, fresh))
    def _():
        wgu_bf[...] = wgu_ref[0].astype(BF16)
        wd_bf[...] = wd_ref[0].astype(BF16)

    @pl.when(busy)
    def _():
        wait_tile(prv)
        x = xbuf[prv].astype(BF16)
        fc = MOE_FF_CHUNK
        n_chunks = D_FF // fc
        rows_per = tm // n_chunks
        y = None
        for c in range(n_chunks):
            lo, hi = c * fc, (c + 1) * fc
            gate = jnp.dot(x, wgu_bf[:, lo:hi], preferred_element_type=F32) + bgu_ref[0][:, lo:hi]
            up = (jnp.dot(x, wgu_bf[:, D_FF + lo:D_FF + hi], preferred_element_type=F32)
                  + bgu_ref[0][:, D_FF + lo:D_FF + hi])
            gate = jnp.minimum(gate, SWIGLU_LIMIT)
            up = jnp.clip(up, -SWIGLU_LIMIT, SWIGLU_LIMIT)
            act = (up + 1.0) * gate * jax.nn.sigmoid(SWIGLU_ALPHA * gate)
            part = jnp.dot(act.astype(BF16), wd_bf[lo:hi, :], preferred_element_type=F32)
            y = part if y is None else y + part
            for j in range(c * rows_per, (c + 1) * rows_per):
                start_row(j)
        ys_ref[...] = y + bd_ref[0]

        @pl.when(r == n_used)
        def _():
            wait_tile(cur)


def _moe(h2, tok_sorted, base, tile_expert, n_used, w_gate_up, b_gate_up, w_down, b_down):
    _, d = h2.shape
    tm = MOE_TM
    n_tiles = tile_expert.shape[0]
    e, _, f2 = w_gate_up.shape
    shift = tm.bit_length() - 1
    assert tm == 1 << shift and tok_sorted.shape[0] % tm == 0

    def tok_block(k):
        return lambda r, te, nu, bs: (
            jnp.right_shift(bs[jnp.minimum(r, n_tiles - 1)], shift) + k,)

    wsel = lambda r, te, nu, bs: (te[jnp.clip(r - 1, 0, n_tiles - 1)], 0, 0)
    return pl.pallas_call(
        _moe_kernel,
        out_shape=jax.ShapeDtypeStruct((n_tiles * tm, d), F32),
        grid_spec=pltpu.PrefetchScalarGridSpec(
            num_scalar_prefetch=3,
            grid=(n_tiles + 1,),
            in_specs=[pl.BlockSpec((tm,), tok_block(0), memory_space=pltpu.SMEM),
                      pl.BlockSpec((tm,), tok_block(1), memory_space=pltpu.SMEM),
                      pl.BlockSpec(memory_space=pl.ANY),
                      pl.BlockSpec((1, d, f2), wsel),
                      pl.BlockSpec((1, 1, f2), wsel),
                      pl.BlockSpec((1, D_FF, d), wsel),
                      pl.BlockSpec((1, 1, d), wsel)],
            out_specs=pl.BlockSpec(
                (tm, d), lambda r, te, nu, bs: (jnp.clip(r - 1, 0, nu[0] - 1), 0)),
            scratch_shapes=[pltpu.VMEM((2, tm, d), F32), pltpu.SemaphoreType.DMA((2,)),
                            pltpu.VMEM((d, f2), BF16), pltpu.VMEM((D_FF, d), BF16)]),
        compiler_params=pltpu.CompilerParams(dimension_semantics=("arbitrary",),
                                             vmem_limit_bytes=VMEM_LIMIT_BYTES),
        name="moe",
    )(tile_expert, n_used, base, tok_sorted, tok_sorted, h2, w_gate_up, b_gate_up, w_down,
      b_down)


def _combine_kernel(pos_ref, ys_ref, x1_ref, tw_ref, gt_ref, gf_ref, o_ref, gbuf, sem):
    tm = x1_ref.shape[0]

    def issue(j, carry):
        for k in range(TOP_K):
            pltpu.make_async_copy(ys_ref.at[pl.ds(pos_ref[j * TOP_K + k], 1)],
                                  gbuf.at[k, pl.ds(j, 1)], sem).start()
        return carry

    lax.fori_loop(0, tm, issue, 0)

    def drain(j, carry):
        for k in range(TOP_K):
            pltpu.make_async_copy(ys_ref.at[pl.ds(0, 1)], gbuf.at[k, pl.ds(0, 1)], sem).wait()
        return carry

    lax.fori_loop(0, tm, drain, 0)

    tw = tw_ref[...]
    acc = tw[:, 0:1] * gbuf[0]
    for k in range(1, TOP_K):
        acc = acc + tw[:, k:k + 1] * gbuf[k]
    x2 = x1_ref[...] + gt_ref[0] * acc
    o_ref[...] = _rms(x2, gf_ref[...])


def _combine(ys, pos_flat, x1, tw, mod, g_final, seq):
    t, d = x1.shape
    tm = min(COMBINE_TM, t)
    return pl.pallas_call(
        _combine_kernel,
        out_shape=jax.ShapeDtypeStruct((t, d), F32),
        grid=(t // tm,),
        in_specs=[pl.BlockSpec((tm * TOP_K,), lambda s: (s,), memory_space=pltpu.SMEM),
                  pl.BlockSpec(memory_space=pl.ANY),
                  pl.BlockSpec((tm, d), lambda s: (s, 0)),
                  pl.BlockSpec((tm, LANES), lambda s: (s, 0)),
                  pl.BlockSpec((1, 1, d), lambda s: ((s * tm) // seq * 6 + 5, 0, 0)),
                  pl.BlockSpec((1, d), lambda s: (0, 0))],
        out_specs=pl.BlockSpec((tm, d), lambda s: (s, 0)),
        scratch_shapes=[pltpu.VMEM((TOP_K, tm, d), F32), pltpu.SemaphoreType.DMA],
        compiler_params=pltpu.CompilerParams(dimension_semantics=("arbitrary",),
                                             vmem_limit_bytes=VMEM_LIMIT_BYTES),
        name="combine",
    )(pos_flat, ys, x1, tw, mod, g_final)


def _routing_tables(top_idx, n_tiles):
    tm = MOE_TM
    e_flat = top_idx.reshape(-1)
    onehot = (e_flat[:, None] == jnp.arange(N_EXPERTS, dtype=I32)[None, :]).astype(I32)
    csum = jnp.cumsum(onehot, axis=0)
    counts = csum[-1]
    rank = jnp.sum(csum * onehot, axis=1) - 1
    tiles_per = (counts + tm - 1) // tm
    tile_end = jnp.cumsum(tiles_per)
    tile_start = tile_end - tiles_per
    pos = (jnp.sum(onehot * tile_start[None, :], axis=1) * tm + rank).astype(I32)
    n_used = tile_end[-1:].astype(I32)
    tile_ids = jnp.arange(n_tiles, dtype=I32)
    tile_expert = jnp.minimum(
        jnp.sum((tile_ids[:, None] >= tile_end[None, :]).astype(I32), axis=1),
        N_EXPERTS - 1).astype(I32)
    order = jnp.argsort(e_flat, stable=True).astype(I32)
    tok_sorted = jnp.concatenate([order // TOP_K, jnp.zeros((tm,), I32)])
    first_pair = jnp.cumsum(counts) - counts
    te_onehot = (tile_expert[:, None] == jnp.arange(N_EXPERTS, dtype=I32)[None, :]).astype(I32)
    base = (jnp.sum(te_onehot * first_pair[None, :], axis=1)
            + (tile_ids - jnp.sum(te_onehot * tile_start[None, :], axis=1)) * tm)
    base = jnp.where(tile_ids < n_used[0], base, 0).astype(I32)
    return pos, tile_expert, n_used, tok_sorted, base


def kernel(x, c, rel_bias, w_ada, b_ada, g_mix, w_in, lambda_q1, lambda_k1, lambda_q2,
           lambda_k2, subln_g, w_br_a, w_br_b, w_out, g_ffn, w_router, b_router,
           w_gate_up, b_gate_up, w_down, b_down, g_final):
    b, s, d = x.shape
    assert d == D_MODEL and s % ATT_T == 0 and MOBA_BLOCK == ATT_T
    t_tok = b * s

    c8 = jnp.zeros((8, d), F32).at[:b].set(c)
    ada = _ada(c8, w_ada[0], b_ada[0].reshape(1, -1))
    mod = ada[:b].reshape(b * 6, 1, d)

    dtiles = _bias_tiles(rel_bias)
    tab = rel_bias.T.reshape(-1)

    proj = _proj(x, g_mix[0].reshape(1, d), mod, w_in[0].astype(BF16))

    lamv = jnp.concatenate([lambda_q1, lambda_k1, lambda_q2, lambda_k2], axis=0)
    y_a = _attention("diff", tab, proj, dtiles, (lamv, subln_g[0].reshape(-1, 1)), (0, 512, 1024))
    y_b = _attention("moba", tab, proj, dtiles, None, (1536, 2048, 2560))

    wr = jnp.zeros((d, LANES), BF16).at[:, :N_EXPERTS].set(w_router[0].astype(BF16))
    br = jnp.zeros((1, LANES), F32).at[0, :N_EXPERTS].set(b_router[0])
    x1, h2, ti, tw = _merge(x, proj, y_a, y_b, w_br_a[0].astype(BF16), w_br_b[0].astype(BF16),
                            w_out[0].astype(BF16), mod, g_ffn[0].reshape(1, d), wr, br)

    n_tiles = (t_tok * TOP_K) // MOE_TM + N_EXPERTS
    top_idx = ti.reshape(t_tok, LANES)[:, :TOP_K]
    pos, tile_expert, n_used, tok_sorted, base = _routing_tables(top_idx, n_tiles)

    ys = _moe(h2.reshape(t_tok, d), tok_sorted, base, tile_expert, n_used, w_gate_up[0],
              b_gate_up[0].reshape(N_EXPERTS, 1, -1), w_down[0],
              b_down[0].reshape(N_EXPERTS, 1, -1))
    out = _combine(ys, pos, x1.reshape(t_tok, d), tw.reshape(t_tok, LANES), mod,
                   g_final.reshape(1, d), s)
    return out.reshape(b, s, d)
```

```python
import functools
import math

import jax
import jax.numpy as jnp
import numpy as np
from jax import lax
from jax.experimental import pallas as pl
from jax.experimental.pallas import tpu as pltpu

F32 = jnp.float32
BF16 = jnp.bfloat16
I32 = jnp.int32

D_MODEL = 1024
N_DIFF_HEADS = 4
DIFF_HEAD_DIM = 64
N_MOBA_HEADS = 8
MOBA_HEAD_DIM = 64
MOBA_BLOCK = 256
MOBA_TOPK = 3
N_HEADS_TOTAL = N_DIFF_HEADS + N_MOBA_HEADS
N_BUCKETS = 32
MAX_DISTANCE = 128
N_EXPERTS = 32
TOP_K = 4
D_FF = D_MODEL
SWIGLU_LIMIT = 7.0
SWIGLU_ALPHA = 1.702
RMS_EPS = 1e-5
LAM_INIT = 0.8 - 0.6 * math.exp(-0.3 * 0)
IN_WIDTH = 5120

LANES = 128
VMEM_LIMIT_BYTES = 56 * 1024 * 1024

ATT_T = 256
N_PAIRS = 4
V_ROWS = LANES + 16
PROJ_TM = 512
MERGE_TM = 512
MOE_TM = 256
MOE_FF_CHUNK = 256
COMBINE_TM = 256
NEG = -1e30


def _t5_bucket_np(dist):
    n = np.maximum(dist, 0)
    max_exact = N_BUCKETS // 2
    nf = np.maximum(n, max_exact).astype(np.float32)
    large = max_exact + (np.log(nf / max_exact) / math.log(MAX_DISTANCE / max_exact)
                         * (N_BUCKETS - max_exact)).astype(np.int32)
    large = np.minimum(large, N_BUCKETS - 1)
    return np.where(n < max_exact, n, large).astype(np.int32)


def _rms(x, g):
    return x * lax.rsqrt(jnp.mean(x * x, axis=-1, keepdims=True) + RMS_EPS) * g


def _ada_kernel(c_ref, w_ref, b_ref, o_ref):
    o_ref[...] = jnp.dot(c_ref[...].astype(BF16), w_ref[...].astype(BF16),
                         preferred_element_type=F32) + b_ref[...]


def _ada(c8, w, b):
    d, n = w.shape
    tn = 1536
    return pl.pallas_call(
        _ada_kernel,
        out_shape=jax.ShapeDtypeStruct((8, n), F32),
        grid=(n // tn,),
        in_specs=[pl.BlockSpec((8, d), lambda j: (0, 0)),
                  pl.BlockSpec((d, tn), lambda j: (0, j)),
                  pl.BlockSpec((1, tn), lambda j: (0, j))],
        out_specs=pl.BlockSpec((8, tn), lambda j: (0, j)),
        compiler_params=pltpu.CompilerParams(dimension_semantics=("arbitrary",),
                                             vmem_limit_bytes=VMEM_LIMIT_BYTES),
        name="ada",
    )(c8, w, b)


def _bias_kernel(tab_ref, bd_ref, bp_ref, o_ref):
    g = pl.program_id(0)
    p = pl.program_id(1)
    t = pl.program_id(2)
    head = jnp.where(g == 0, p, N_DIFF_HEADS + 2 * p + t)
    bd = bd_ref[...]
    bp = bp_ref[...]
    d0 = jnp.where(bd < 0, NEG, 0.0).astype(F32)
    d1 = jnp.zeros(bp.shape, F32)
    for b in range(N_BUCKETS):
        val = tab_ref[head * N_BUCKETS + b]
        d0 = jnp.where(bd == b, val, d0)
        d1 = jnp.where(bp == b, val, d1)
    o_ref[0, 0, 0] = d0
    o_ref[0, 0, 1] = d1


def _bias_tiles(rel_bias):
    t = ATT_T
    jj = np.arange(t)[:, None]
    ii = np.arange(t)[None, :]
    bd = np.where(ii >= jj, _t5_bucket_np(ii - jj), -1).astype(np.int32)
    bp = _t5_bucket_np(t + ii - jj)
    tab = rel_bias.T.reshape(-1)
    return pl.pallas_call(
        _bias_kernel,
        out_shape=jax.ShapeDtypeStruct((2, N_PAIRS, 2, t, 2 * t), F32),
        grid=(2, N_PAIRS, 2),
        in_specs=[pl.BlockSpec(memory_space=pltpu.SMEM),
                  pl.BlockSpec((t, t), lambda g, p, h: (0, 0)),
                  pl.BlockSpec((t, t), lambda g, p, h: (0, 0))],
        out_specs=pl.BlockSpec((1, 1, 2, t, t), lambda g, p, h: (g, p, 0, 0, h)),
        compiler_params=pltpu.CompilerParams(
            dimension_semantics=("arbitrary", "arbitrary", "arbitrary")),
        name="bias_tiles",
    )(tab, jnp.asarray(bd), jnp.asarray(bp))


def _proj_kernel(x_ref, g_ref, sh_ref, sc_ref, w_ref, o_ref):
    x = x_ref[0]
    h = _rms(x, g_ref[...]) * (1.0 + sc_ref[0]) + sh_ref[0]
    hb = h.astype(BF16)
    n_total = w_ref.shape[1]
    for n0 in range(0, n_total, 512):
        o_ref[0, :, n0:n0 + 512] = jnp.dot(
            hb, w_ref[:, n0:n0 + 512], preferred_element_type=F32).astype(BF16)


def _proj(x, g_mix, mod, w_in_bf):
    b, s, d = x.shape
    n = w_in_bf.shape[1]
    tm = min(PROJ_TM, s)
    return pl.pallas_call(
        _proj_kernel,
        out_shape=jax.ShapeDtypeStruct((b, s, n), BF16),
        grid=(b, s // tm),
        in_specs=[pl.BlockSpec((1, tm, d), lambda bi, i: (bi, i, 0)),
                  pl.BlockSpec((1, d), lambda bi, i: (0, 0)),
                  pl.BlockSpec((1, 1, d), lambda bi, i: (bi * 6 + 0, 0, 0)),
                  pl.BlockSpec((1, 1, d), lambda bi, i: (bi * 6 + 1, 0, 0)),
                  pl.BlockSpec((d, n), lambda bi, i: (0, 0))],
        out_specs=pl.BlockSpec((1, tm, n), lambda bi, i: (bi, i, 0)),
        compiler_params=pltpu.CompilerParams(dimension_semantics=("arbitrary", "arbitrary"),
                                             vmem_limit_bytes=VMEM_LIMIT_BYTES),
        name="proj",
    )(x, g_mix, mod, mod, w_in_bf)


def _stack_q(q_ref, q2_ref, scale):
    t = ATT_T
    row = lax.broadcasted_iota(I32, (LANES, t), 0)
    for p in range(N_PAIRS):
        q = q_ref[0, 0, p * LANES:(p + 1) * LANES, :].astype(F32) * scale
        q2_ref[p, :, 0:t] = jnp.where(row < 64, q, 0.0).astype(BF16)
        q2_ref[p, :, t:2 * t] = jnp.where(row >= 64, q, 0.0).astype(BF16)


def _flash_init(m_ref, acc_ref):
    m_ref[...] = jnp.full(m_ref.shape, NEG, F32)
    acc_ref[...] = jnp.zeros(acc_ref.shape, F32)


def _flash_scores(p, h, n, q2_ref, k_ref, s_ref):
    t = ATT_T
    kt = k_ref[0, pl.ds(pl.multiple_of(n * t, t), t), p * LANES:(p + 1) * LANES]
    s_ref[2 * p + h] = jnp.dot(kt, q2_ref[p, :, h * t:(h + 1) * t],
                               preferred_element_type=F32)


def _flash_absorb(p, h, n, tile_bias, row_bias, live, v_ref, s_ref, m_ref, acc_ref):
    t = ATT_T
    cs = slice(h * t, (h + 1) * t)
    vt = v_ref[0, n, p * V_ROWS:(p + 1) * V_ROWS, :]
    s = s_ref[2 * p + h]
    if tile_bias is not None:
        s = s + tile_bias
    mx = jnp.max(s, axis=0, keepdims=True)
    if row_bias is not None:
        mx = mx + row_bias
    if live is not None:
        mx = jnp.where(live, mx, NEG)
    m_prev = m_ref[p, :, cs]
    m_new = jnp.maximum(m_prev, mx)
    alpha = jnp.exp(m_prev - m_new)
    shift = m_new if row_bias is None else m_new - row_bias
    if live is not None:
        shift = jnp.where(live, shift, -NEG)
    pt = jnp.exp(s - shift)
    acc_ref[p, :, cs] = alpha * acc_ref[p, :, cs] + jnp.dot(vt, pt.astype(BF16),
                                                            preferred_element_type=F32)
    m_ref[p, :, cs] = m_new


def _far_bias(tab_ref, head):
    return tab_ref[head * N_BUCKETS + (N_BUCKETS - 1)]


def _flash_sweep(i, far_args, prev_args, own_args, refs):
    q2_ref, k_ref, v_ref, s_ref, m_ref, acc_ref = refs
    chains = [(p, h) for p in range(N_PAIRS) for h in range(2)]

    def tile(n, args):
        for p, h in chains:
            _flash_scores(p, h, n, q2_ref, k_ref, s_ref)
        for p, h in chains:
            _flash_absorb(p, h, n, *args(p, h), v_ref, s_ref, m_ref, acc_ref)

    def far_body(n, carry):
        tile(n, lambda p, h: far_args(p, h, n))
        return carry

    lax.fori_loop(0, jnp.maximum(i - 1, 0), far_body, 0)

    @pl.when(i >= 1)
    def _():
        tile(i - 1, prev_args)

    tile(i, own_args)


def _diff_kernel(tab_ref, q_ref, k_ref, v_ref, d_ref, lamv_ref, g_ref, o_ref,
                 q2_ref, s_ref, m_ref, acc_ref):
    t = ATT_T
    i = pl.program_id(1)
    _stack_q(q_ref, q2_ref, DIFF_HEAD_DIM ** -0.5)
    _flash_init(m_ref, acc_ref)
    refs = (q2_ref, k_ref, v_ref, s_ref, m_ref, acc_ref)
    _flash_sweep(i,
                 lambda p, h, n: (None, _far_bias(tab_ref, p), None),
                 lambda p, h: (d_ref[0, p, 1, :, h * t:(h + 1) * t], None, None),
                 lambda p, h: (d_ref[0, p, 0, :, h * t:(h + 1) * t], None, None),
                 refs)

    lv = lamv_ref[...]
    lam = (jnp.exp(jnp.sum(lv[0:1] * lv[1:2], axis=1, keepdims=True))
           - jnp.exp(jnp.sum(lv[2:3] * lv[3:4], axis=1, keepdims=True)) + LAM_INIT)
    for p in range(N_PAIRS):
        acc = acc_ref[p, 0:LANES, :]
        l = acc_ref[p, LANES:LANES + 1, :]
        o = acc[:, 0:t] / l[:, 0:t] - lam * (acc[:, t:2 * t] / l[:, t:2 * t])
        y = o * lax.rsqrt(jnp.mean(o * o, axis=0, keepdims=True) + RMS_EPS) * g_ref[...]
        o_ref[0, 0, p * LANES:(p + 1) * LANES, :] = (y * (1.0 - LAM_INIT)).astype(BF16)


def _moba_kernel(tab_ref, q_ref, k_ref, v_ref, d_ref, o_ref,
                 q2_ref, s_ref, m_ref, acc_ref, kmean_ref, sel_ref):
    t = ATT_T
    i = pl.program_id(1)
    nb = k_ref.shape[1] // t
    nbp = kmean_ref.shape[1]

    @pl.when(i == 0)
    def _():
        kmean_ref[...] = jnp.zeros(kmean_ref.shape, F32)
        for p in range(N_PAIRS):
            for n in range(nb):
                blk = k_ref[0, n * t:(n + 1) * t, p * LANES:(p + 1) * LANES].astype(F32)
                kmean_ref[p, n:n + 1, :] = jnp.sum(blk, axis=0, keepdims=True) * (1.0 / t)

    _stack_q(q_ref, q2_ref, MOBA_HEAD_DIM ** -0.5)
    _flash_init(m_ref, acc_ref)

    blk_id = lax.broadcasted_iota(I32, (nbp, 2 * t), 0)
    blk_f = blk_id.astype(F32)
    for p in range(N_PAIRS):
        gs = jnp.dot(kmean_ref[p].astype(BF16), q2_ref[p], preferred_element_type=F32)
        g = jnp.where(blk_id < i, gs, -jnp.inf)
        sel = jnp.full((nbp, 2 * t), NEG, F32)
        for _ in range(MOBA_TOPK):
            mx = jnp.max(g, axis=0, keepdims=True)
            first = jnp.min(jnp.where(g == mx, blk_f, float(nbp)), axis=0, keepdims=True)
            hit = blk_f == first
            sel = jnp.where(hit, 0.0, sel)
            g = jnp.where(hit, -jnp.inf, g)
        sel_ref[p] = sel

    def live(p, h, n):
        return sel_ref[p, pl.ds(n, 1), h * t:(h + 1) * t] > -1.0

    refs = (q2_ref, k_ref, v_ref, s_ref, m_ref, acc_ref)
    _flash_sweep(i,
                 lambda p, h, n: (None, _far_bias(tab_ref, N_DIFF_HEADS + 2 * p + h),
                                  live(p, h, n)),
                 lambda p, h: (d_ref[0, p, 1, :, h * t:(h + 1) * t], None, live(p, h, i - 1)),
                 lambda p, h: (d_ref[0, p, 0, :, h * t:(h + 1) * t], None, None),
                 refs)

    row = lax.broadcasted_iota(I32, (LANES, t), 0)
    for p in range(N_PAIRS):
        acc = acc_ref[p, 0:LANES, :]
        l = acc_ref[p, LANES:LANES + 1, :]
        o = jnp.where(row < 64, acc[:, 0:t] / l[:, 0:t], acc[:, t:2 * t] / l[:, t:2 * t])
        o_ref[0, 0, p * LANES:(p + 1) * LANES, :] = o.astype(BF16)


def _to_kv_major(proj, col0, nq):
    b, s, _ = proj.shape
    w = N_PAIRS * LANES
    return jnp.swapaxes(proj[:, :, col0:col0 + w].reshape(b, nq, ATT_T, w), 2, 3)


def _values_kv_major(proj, col0, nq):
    b, s, _ = proj.shape
    v = proj[:, :, col0:col0 + N_PAIRS * LANES].reshape(b, nq, ATT_T, N_PAIRS, LANES)
    ones = jnp.ones((b, nq, ATT_T, N_PAIRS, V_ROWS - LANES), BF16)
    v = jnp.concatenate([v, ones], axis=-1)
    return jnp.transpose(v, (0, 1, 3, 4, 2)).reshape(b, nq, N_PAIRS * V_ROWS, ATT_T)


def _attention(kind, tab, proj, dtiles, extra, cols):
    b, s, _ = proj.shape
    t = ATT_T
    nq = s // t
    w = N_PAIRS * LANES
    qc, kc, vc = cols
    gidx = 0 if kind == "diff" else 1
    q_t = _to_kv_major(proj, qc, nq)
    v_t = _values_kv_major(proj, vc, nq)
    in_specs = [pl.BlockSpec(memory_space=pltpu.SMEM),
                pl.BlockSpec((1, 1, w, t), lambda bi, i: (bi, i, 0, 0)),
                pl.BlockSpec((1, s, w), lambda bi, i: (bi, 0, kc // w)),
                pl.BlockSpec((1, nq, N_PAIRS * V_ROWS, t), lambda bi, i: (bi, 0, 0, 0)),
                pl.BlockSpec((1, N_PAIRS, 2, t, 2 * t), lambda bi, i: (gidx, 0, 0, 0, 0))]
    scratch = [pltpu.VMEM((N_PAIRS, LANES, 2 * t), BF16),
               pltpu.VMEM((2 * N_PAIRS, t, t), F32),
               pltpu.VMEM((N_PAIRS, 1, 2 * t), F32),
               pltpu.VMEM((N_PAIRS, V_ROWS, 2 * t), F32)]
    if kind == "diff":
        lamv, subln_g = extra
        in_specs += [pl.BlockSpec(lamv.shape, lambda bi, i: (0, 0)),
                     pl.BlockSpec(subln_g.shape, lambda bi, i: (0, 0))]
        args = (tab, q_t, proj, v_t, dtiles, lamv, subln_g)
        body = _diff_kernel
    else:
        nbp = max(32, nq)
        scratch += [pltpu.VMEM((N_PAIRS, nbp, LANES), F32), pltpu.VMEM((N_PAIRS, nbp, 2 * t), F32)]
        args = (tab, q_t, proj, v_t, dtiles)
        body = _moba_kernel
    y_t = pl.pallas_call(
        body,
        out_shape=jax.ShapeDtypeStruct((b, nq, w, t), BF16),
        grid=(b, nq),
        in_specs=in_specs,
        out_specs=pl.BlockSpec((1, 1, w, t), lambda bi, i: (bi, i, 0, 0)),
        scratch_shapes=scratch,
        compiler_params=pltpu.CompilerParams(
            dimension_semantics=("arbitrary", "arbitrary"),
            vmem_limit_bytes=VMEM_LIMIT_BYTES),
        name=kind + "_attn",
    )(*args)
    return jnp.swapaxes(y_t, 2, 3).reshape(b, s, w)


def _merge_kernel(x_ref, ya_ref, yb_ref, gla_ref, glb_ref, wa_ref, wb_ref, wo_ref,
                  gt_ref, shf_ref, scf_ref, gf_ref, wr_ref, br_ref,
                  x1_ref, h2_ref, ti_ref, tw_ref):
    a = jnp.dot(ya_ref[0], wa_ref[...], preferred_element_type=F32)
    b = jnp.dot(yb_ref[0], wb_ref[...], preferred_element_type=F32)
    merged = (jax.nn.sigmoid(gla_ref[0].astype(F32)) * a
              + jax.nn.sigmoid(glb_ref[0].astype(F32)) * b)
    mo = jnp.dot(merged.astype(BF16), wo_ref[...], preferred_element_type=F32)
    x1 = x_ref[0] + gt_ref[0] * mo
    x1_ref[0] = x1
    h = _rms(x1, gf_ref[...]) * (1.0 + scf_ref[0]) + shf_ref[0]
    h2_ref[0] = h
    logits = jnp.dot(h.astype(BF16), wr_ref[...], preferred_element_type=F32) + br_ref[...]
    tm = logits.shape[0]
    lane = lax.broadcasted_iota(I32, (tm, LANES), 1)
    lane_f = lane.astype(F32)
    g = jnp.where(lane < N_EXPERTS, logits, -jnp.inf)
    vals, idxs = [], []
    for _ in range(TOP_K):
        mx = jnp.max(g, axis=1, keepdims=True)
        first = jnp.min(jnp.where(g == mx, lane_f, float(LANES)), axis=1, keepdims=True)
        vals.append(mx)
        idxs.append(first)
        g = jnp.where(lane_f == first, -jnp.inf, g)
    es = [jnp.exp(v - vals[0]) for v in vals]
    den = es[0] + es[1] + es[2] + es[3]
    ti = jnp.zeros((tm, LANES), F32)
    tw = jnp.zeros((tm, LANES), F32)
    for k in range(TOP_K):
        ti = jnp.where(lane == k, idxs[k], ti)
        tw = jnp.where(lane == k, es[k] / den, tw)
    ti_ref[0] = ti.astype(I32)
    tw_ref[0] = tw


def _merge(x, proj, y_a, y_b, w_br_a, w_br_b, w_out, mod, g_ffn, w_router, b_router):
    b, s, d = x.shape
    tm = min(MERGE_TM, s)
    full = lambda shape: pl.BlockSpec(shape, lambda bi, i: (0,) * len(shape))
    tile = lambda w, c=0: pl.BlockSpec((1, tm, w), lambda bi, i: (bi, i, c))
    modspec = lambda j: pl.BlockSpec((1, 1, d), lambda bi, i: (bi * 6 + j, 0, 0))
    return pl.pallas_call(
        _merge_kernel,
        out_shape=(jax.ShapeDtypeStruct((b, s, d), F32),
                   jax.ShapeDtypeStruct((b, s, d), F32),
                   jax.ShapeDtypeStruct((b, s, LANES), I32),
                   jax.ShapeDtypeStruct((b, s, LANES), F32)),
        grid=(b, s // tm),
        in_specs=[tile(d), tile(512), tile(512), tile(d, 3), tile(d, 4),
                  full(w_br_a.shape), full(w_br_b.shape), full(w_out.shape),
                  modspec(2), modspec(3), modspec(4), full(g_ffn.shape),
                  full(w_router.shape), full(b_router.shape)],
        out_specs=(tile(d), tile(d), tile(LANES), tile(LANES)),
        compiler_params=pltpu.CompilerParams(dimension_semantics=("arbitrary", "arbitrary"),
                                             vmem_limit_bytes=VMEM_LIMIT_BYTES),
        name="merge",
    )(x, y_a, y_b, proj, proj, w_br_a, w_br_b, w_out, mod, mod, mod, g_ffn,
      w_router, b_router)


def _moe_kernel(te_ref, nu_ref, base_ref, tok_a, tok_b, h2_ref, wgu_ref, bgu_ref, wd_ref, bd_ref,
                ys_ref, xbuf, sems, wgu_bf, wd_bf):
    r = pl.program_id(0)
    n_used = nu_ref[0]
    n_tiles = te_ref.shape[0]
    tm = xbuf.shape[1]
    cur = r % 2
    prv = 1 - cur
    off = jnp.bitwise_and(base_ref[jnp.minimum(r, n_tiles - 1)], tm - 1)

    def start_row(j):
        idx = off + j
        tok = jnp.where(idx < tm, tok_a[jnp.minimum(idx, tm - 1)], tok_b[jnp.maximum(idx - tm, 0)])
        pltpu.make_async_copy(h2_ref.at[pl.ds(tok, 1)], xbuf.at[cur, pl.ds(j, 1)],
                              sems.at[cur]).start()

    def wait_tile(s):
        pltpu.make_async_copy(h2_ref.at[pl.ds(0, tm)], xbuf.at[s], sems.at[s]).wait()

    @pl.when(r == 0)
    def _():
        def body(j, carry):
            start_row(j)
            return carry
        lax.fori_loop(0, tm, body, 0)

    busy = jnp.logical_and(r >= 1, r <= n_used)
    e_prev = te_ref[jnp.maximum(r - 1, 0)]
    fresh = jnp.logical_or(r == 1, e_prev != te_ref[jnp.maximum(r - 2, 0)])

    @pl.when(jnp.logical_and(busy, fresh))
    def _():
        wgu_bf[...] = wgu_ref[0].astype(BF16)
        wd_bf[...] = wd_ref[0].astype(BF16)

    @pl.when(busy)
    def _():
        wait_tile(prv)
        x = xbuf[prv].astype(BF16)
        fc = MOE_FF_CHUNK
        n_chunks = D_FF // fc
        rows_per = tm // n_chunks
        y = None
        for c in range(n_chunks):
            lo, hi = c * fc, (c + 1) * fc
            gate = jnp.dot(x, wgu_bf[:, lo:hi], preferred_element_type=F32) + bgu_ref[0][:, lo:hi]
            up = (jnp.dot(x, wgu_bf[:, D_FF + lo:D_FF + hi], preferred_element_type=F32)
                  + bgu_ref[0][:, D_FF + lo:D_FF + hi])
            gate = jnp.minimum(gate, SWIGLU_LIMIT)
            up = jnp.clip(up, -SWIGLU_LIMIT, SWIGLU_LIMIT)
            act = (up + 1.0) * gate * jax.nn.sigmoid(SWIGLU_ALPHA * gate)
            part = jnp.dot(act.astype(BF16), wd_bf[lo:hi, :], preferred_element_type=F32)
            y = part if y is None else y + part
            for j in range(c * rows_per, (c + 1) * rows_per):
                start_row(j)
        ys_ref[...] = y + bd_ref[0]

        @pl.when(r == n_used)
        def _():
            wait_tile(cur)


def _moe(h2, tok_sorted, base, tile_expert, n_used, w_gate_up, b_gate_up, w_down, b_down):
    _, d = h2.shape
    tm = MOE_TM
    n_tiles = tile_expert.shape[0]
    e, _, f2 = w_gate_up.shape
    shift = tm.bit_length() - 1
    assert tm == 1 << shift and tok_sorted.shape[0] % tm == 0

    def tok_block(k):
        return lambda r, te, nu, bs: (
            jnp.right_shift(bs[jnp.minimum(r, n_tiles - 1)], shift) + k,)

    wsel = lambda r, te, nu, bs: (te[jnp.clip(r - 1, 0, n_tiles - 1)], 0, 0)
    return pl.pallas_call(
        _moe_kernel,
        out_shape=jax.ShapeDtypeStruct((n_tiles * tm, d), F32),
        grid_spec=pltpu.PrefetchScalarGridSpec(
            num_scalar_prefetch=3,
            grid=(n_tiles + 1,),
            in_specs=[pl.BlockSpec((tm,), tok_block(0), memory_space=pltpu.SMEM),
                      pl.BlockSpec((tm,), tok_block(1), memory_space=pltpu.SMEM),
                      pl.BlockSpec(memory_space=pl.ANY),
                      pl.BlockSpec((1, d, f2), wsel),
                      pl.BlockSpec((1, 1, f2), wsel),
                      pl.BlockSpec((1, D_FF, d), wsel),
                      pl.BlockSpec((1, 1, d), wsel)],
            out_specs=pl.BlockSpec(
                (tm, d), lambda r, te, nu, bs: (jnp.clip(r - 1, 0, nu[0] - 1), 0)),
            scratch_shapes=[pltpu.VMEM((2, tm, d), F32), pltpu.SemaphoreType.DMA((2,)),
                            pltpu.VMEM((d, f2), BF16), pltpu.VMEM((D_FF, d), BF16)]),
        compiler_params=pltpu.CompilerParams(dimension_semantics=("arbitrary",),
                                             vmem_limit_bytes=VMEM_LIMIT_BYTES),
        name="moe",
    )(tile_expert, n_used, base, tok_sorted, tok_sorted, h2, w_gate_up, b_gate_up, w_down,
      b_down)


def _combine_kernel(pos_ref, ys_ref, x1_ref, tw_ref, gt_ref, gf_ref, o_ref, gbuf, sem):
    tm = x1_ref.shape[0]

    def issue(j, carry):
        for k in range(TOP_K):
            pltpu.make_async_copy(ys_ref.at[pl.ds(pos_ref[j * TOP_K + k], 1)],
                                  gbuf.at[k, pl.ds(j, 1)], sem).start()
        return carry

    lax.fori_loop(0, tm, issue, 0)

    def drain(j, carry):
        for k in range(TOP_K):
            pltpu.make_async_copy(ys_ref.at[pl.ds(0, 1)], gbuf.at[k, pl.ds(0, 1)], sem).wait()
        return carry

    lax.fori_loop(0, tm, drain, 0)

    tw = tw_ref[...]
    acc = tw[:, 0:1] * gbuf[0]
    for k in range(1, TOP_K):
        acc = acc + tw[:, k:k + 1] * gbuf[k]
    x2 = x1_ref[...] + gt_ref[0] * acc
    o_ref[...] = _rms(x2, gf_ref[...])


def _combine(ys, pos_flat, x1, tw, mod, g_final, seq):
    t, d = x1.shape
    tm = min(COMBINE_TM, t)
    return pl.pallas_call(
        _combine_kernel,
        out_shape=jax.ShapeDtypeStruct((t, d), F32),
        grid=(t // tm,),
        in_specs=[pl.BlockSpec((tm * TOP_K,), lambda s: (s,), memory_space=pltpu.SMEM),
                  pl.BlockSpec(memory_space=pl.ANY),
                  pl.BlockSpec((tm, d), lambda s: (s, 0)),
                  pl.BlockSpec((tm, LANES), lambda s: (s, 0)),
                  pl.BlockSpec((1, 1, d), lambda s: ((s * tm) // seq * 6 + 5, 0, 0)),
                  pl.BlockSpec((1, d), lambda s: (0, 0))],
        out_specs=pl.BlockSpec((tm, d), lambda s: (s, 0)),
        scratch_shapes=[pltpu.VMEM((TOP_K, tm, d), F32), pltpu.SemaphoreType.DMA],
        compiler_params=pltpu.CompilerParams(dimension_semantics=("arbitrary",),
                                             vmem_limit_bytes=VMEM_LIMIT_BYTES),
        name="combine",
    )(pos_flat, ys, x1, tw, mod, g_final)


def _routing_tables(top_idx, n_tiles):
    tm = MOE_TM
    e_flat = top_idx.reshape(-1)
    onehot = (e_flat[:, None] == jnp.arange(N_EXPERTS, dtype=I32)[None, :]).astype(I32)
    csum = jnp.cumsum(onehot, axis=0)
    counts = csum[-1]
    rank = jnp.sum(csum * onehot, axis=1) - 1
    tiles_per = (counts + tm - 1) // tm
    tile_end = jnp.cumsum(tiles_per)
    tile_start = tile_end - tiles_per
    pos = (jnp.sum(onehot * tile_start[None, :], axis=1) * tm + rank).astype(I32)
    n_used = tile_end[-1:].astype(I32)
    tile_ids = jnp.arange(n_tiles, dtype=I32)
    tile_expert = jnp.minimum(
        jnp.sum((tile_ids[:, None] >= tile_end[None, :]).astype(I32), axis=1),
        N_EXPERTS - 1).astype(I32)
    order = jnp.argsort(e_flat, stable=True).astype(I32)
    tok_sorted = jnp.concatenate([order // TOP_K, jnp.zeros((tm,), I32)])
    first_pair = jnp.cumsum(counts) - counts
    te_onehot = (tile_expert[:, None] == jnp.arange(N_EXPERTS, dtype=I32)[None, :]).astype(I32)
    base = (jnp.sum(te_onehot * first_pair[None, :], axis=1)
            + (tile_ids - jnp.sum(te_onehot * tile_start[None, :], axis=1)) * tm)
    base = jnp.where(tile_ids < n_used[0], base, 0).astype(I32)
    return pos, tile_expert, n_used, tok_sorted, base


def kernel(x, c, rel_bias, w_ada, b_ada, g_mix, w_in, lambda_q1, lambda_k1, lambda_q2,
           lambda_k2, subln_g, w_br_a, w_br_b, w_out, g_ffn, w_router, b_router,
           w_gate_up, b_gate_up, w_down, b_down, g_final):
    b, s, d = x.shape
    assert d == D_MODEL and s % ATT_T == 0 and MOBA_BLOCK == ATT_T
    t_tok = b * s

    c8 = jnp.zeros((8, d), F32).at[:b].set(c)
    ada = _ada(c8, w_ada[0], b_ada[0].reshape(1, -1))
    mod = ada[:b].reshape(b * 6, 1, d)

    dtiles = _bias_tiles(rel_bias)
    tab = rel_bias.T.reshape(-1)

    proj = _proj(x, g_mix[0].reshape(1, d), mod, w_in[0].astype(BF16))

    lamv = jnp.concatenate([lambda_q1, lambda_k1, lambda_q2, lambda_k2], axis=0)
    y_a = _attention("diff", tab, proj, dtiles, (lamv, subln_g[0].reshape(-1, 1)), (0, 512, 1024))
    y_b = _attention("moba", tab, proj, dtiles, None, (1536, 2048, 2560))

    wr = jnp.zeros((d, LANES), BF16).at[:, :N_EXPERTS].set(w_router[0].astype(BF16))
    br = jnp.zeros((1, LANES), F32).at[0, :N_EXPERTS].set(b_router[0])
    x1, h2, ti, tw = _merge(x, proj, y_a, y_b, w_br_a[0].astype(BF16), w_br_b[0].astype(BF16),
                            w_out[0].astype(BF16), mod, g_ffn[0].reshape(1, d), wr, br)

    n_tiles = (t_tok * TOP_K) // MOE_TM + N_EXPERTS
    top_idx = ti.reshape(t_tok, LANES)[:, :TOP_K]
    pos, tile_expert, n_used, tok_sorted, base = _routing_tables(top_idx, n_tiles)

    ys = _moe(h2.reshape(t_tok, d), tok_sorted, base, tile_expert, n_used, w_gate_up[0],
              b_gate_up[0].reshape(N_EXPERTS, 1, -1), w_down[0],
              b_down[0].reshape(N_EXPERTS, 1, -1))
    out = _combine(ys, pos, x1.reshape(t_tok, d), tw.reshape(t_tok, LANES), mod,
                   g_final.reshape(1, d), s)
    return out.reshape(b, s, d)
```

```python
import functools
import math

import jax
import jax.numpy as jnp
import numpy as np
from jax import lax
from jax.experimental import pallas as pl
from jax.experimental.pallas import tpu as pltpu

F32 = jnp.float32
BF16 = jnp.bfloat16
I32 = jnp.int32

D_MODEL = 1024
N_DIFF_HEADS = 4
DIFF_HEAD_DIM = 64
N_MOBA_HEADS = 8
MOBA_HEAD_DIM = 64
MOBA_BLOCK = 256
MOBA_TOPK = 3
N_HEADS_TOTAL = N_DIFF_HEADS + N_MOBA_HEADS
N_BUCKETS = 32
MAX_DISTANCE = 128
N_EXPERTS = 32
TOP_K = 4
D_FF = D_MODEL
SWIGLU_LIMIT = 7.0
SWIGLU_ALPHA = 1.702
RMS_EPS = 1e-5
LAM_INIT = 0.8 - 0.6 * math.exp(-0.3 * 0)
IN_WIDTH = 5120

LANES = 128
VMEM_LIMIT_BYTES = 56 * 1024 * 1024

ATT_T = 256
N_PAIRS = 4
V_ROWS = LANES + 16
PROJ_TM = 512
MERGE_TM = 512
MOE_TM = 256
MOE_FF_CHUNK = 256
COMBINE_TM = 256
NEG = -1e30


def _t5_bucket_np(dist):
    n = np.maximum(dist, 0)
    max_exact = N_BUCKETS // 2
    nf = np.maximum(n, max_exact).astype(np.float32)
    large = max_exact + (np.log(nf / max_exact) / math.log(MAX_DISTANCE / max_exact)
                         * (N_BUCKETS - max_exact)).astype(np.int32)
    large = np.minimum(large, N_BUCKETS - 1)
    return np.where(n < max_exact, n, large).astype(np.int32)


def _rms(x, g):
    return x * lax.rsqrt(jnp.mean(x * x, axis=-1, keepdims=True) + RMS_EPS) * g


def _ada_kernel(c_ref, w_ref, b_ref, o_ref):
    o_ref[...] = jnp.dot(c_ref[...].astype(BF16), w_ref[...].astype(BF16),
                         preferred_element_type=F32) + b_ref[...]


def _ada(c8, w, b):
    d, n = w.shape
    tn = 1536
    return pl.pallas_call(
        _ada_kernel,
        out_shape=jax.ShapeDtypeStruct((8, n), F32),
        grid=(n // tn,),
        in_specs=[pl.BlockSpec((8, d), lambda j: (0, 0)),
                  pl.BlockSpec((d, tn), lambda j: (0, j)),
                  pl.BlockSpec((1, tn), lambda j: (0, j))],
        out_specs=pl.BlockSpec((8, tn), lambda j: (0, j)),
        compiler_params=pltpu.CompilerParams(dimension_semantics=("arbitrary",),
                                             vmem_limit_bytes=VMEM_LIMIT_BYTES),
        name="ada",
    )(c8, w, b)


def _bias_kernel(tab_ref, bd_ref, bp_ref, o_ref):
    g = pl.program_id(0)
    p = pl.program_id(1)
    t = pl.program_id(2)
    head = jnp.where(g == 0, p, N_DIFF_HEADS + 2 * p + t)
    bd = bd_ref[...]
    bp = bp_ref[...]
    d0 = jnp.where(bd < 0, NEG, 0.0).astype(F32)
    d1 = jnp.zeros(bp.shape, F32)
    for b in range(N_BUCKETS):
        val = tab_ref[head * N_BUCKETS + b]
        d0 = jnp.where(bd == b, val, d0)
        d1 = jnp.where(bp == b, val, d1)
    o_ref[0, 0, 0] = d0
    o_ref[0, 0, 1] = d1


def _bias_tiles(rel_bias):
    t = ATT_T
    jj = np.arange(t)[:, None]
    ii = np.arange(t)[None, :]
    bd = np.where(ii >= jj, _t5_bucket_np(ii - jj), -1).astype(np.int32)
    bp = _t5_bucket_np(t + ii - jj)
    tab = rel_bias.T.reshape(-1)
    return pl.pallas_call(
        _bias_kernel,
        out_shape=jax.ShapeDtypeStruct((2, N_PAIRS, 2, t, 2 * t), F32),
        grid=(2, N_PAIRS, 2),
        in_specs=[pl.BlockSpec(memory_space=pltpu.SMEM),
                  pl.BlockSpec((t, t), lambda g, p, h: (0, 0)),
                  pl.BlockSpec((t, t), lambda g, p, h: (0, 0))],
        out_specs=pl.BlockSpec((1, 1, 2, t, t), lambda g, p, h: (g, p, 0, 0, h)),
        compiler_params=pltpu.CompilerParams(
            dimension_semantics=("arbitrary", "arbitrary", "arbitrary")),
        name="bias_tiles",
    )(tab, jnp.asarray(bd), jnp.asarray(bp))


def _proj_kernel(x_ref, g_ref, sh_ref, sc_ref, w_ref, o_ref):
    x = x_ref[0]
    h = _rms(x, g_ref[...]) * (1.0 + sc_ref[0]) + sh_ref[0]
    hb = h.astype(BF16)
    n_total = w_ref.shape[1]
    for n0 in range(0, n_total, 512):
        o_ref[0, :, n0:n0 + 512] = jnp.dot(
            hb, w_ref[:, n0:n0 + 512], preferred_element_type=F32).astype(BF16)


def _proj(x, g_mix, mod, w_in_bf):
    b, s, d = x.shape
    n = w_in_bf.shape[1]
    tm = min(PROJ_TM, s)
    return pl.pallas_call(
        _proj_kernel,
        out_shape=jax.ShapeDtypeStruct((b, s, n), BF16),
        grid=(b, s // tm),
        in_specs=[pl.BlockSpec((1, tm, d), lambda bi, i: (bi, i, 0)),
                  pl.BlockSpec((1, d), lambda bi, i: (0, 0)),
                  pl.BlockSpec((1, 1, d), lambda bi, i: (bi * 6 + 0, 0, 0)),
                  pl.BlockSpec((1, 1, d), lambda bi, i: (bi * 6 + 1, 0, 0)),
                  pl.BlockSpec((d, n), lambda bi, i: (0, 0))],
        out_specs=pl.BlockSpec((1, tm, n), lambda bi, i: (bi, i, 0)),
        compiler_params=pltpu.CompilerParams(dimension_semantics=("arbitrary", "arbitrary"),
                                             vmem_limit_bytes=VMEM_LIMIT_BYTES),
        name="proj",
    )(x, g_mix, mod, mod, w_in_bf)


def _stack_q(q_ref, q2_ref, scale):
    t = ATT_T
    row = lax.broadcasted_iota(I32, (LANES, t), 0)
    for p in range(N_PAIRS):
        q = q_ref[0, 0, p * LANES:(p + 1) * LANES, :].astype(F32) * scale
        q2_ref[p, :, 0:t] = jnp.where(row < 64, q, 0.0).astype(BF16)
        q2_ref[p, :, t:2 * t] = jnp.where(row >= 64, q, 0.0).astype(BF16)


def _flash_init(m_ref, acc_ref):
    m_ref[...] = jnp.full(m_ref.shape, NEG, F32)
    acc_ref[...] = jnp.zeros(acc_ref.shape, F32)


def _flash_scores(p, h, n, q2_ref, k_ref, s_ref):
    t = ATT_T
    kt = k_ref[0, pl.ds(pl.multiple_of(n * t, t), t), p * LANES:(p + 1) * LANES]
    s_ref[2 * p + h] = jnp.dot(kt, q2_ref[p, :, h * t:(h + 1) * t],
                               preferred_element_type=F32)


def _flash_absorb(p, h, n, tile_bias, row_bias, live, v_ref, s_ref, m_ref, acc_ref):
    t = ATT_T
    cs = slice(h * t, (h + 1) * t)
    vt = v_ref[0, n, p * V_ROWS:(p + 1) * V_ROWS, :]
    s = s_ref[2 * p + h]
    if tile_bias is not None:
        s = s + tile_bias
    mx = jnp.max(s, axis=0, keepdims=True)
    if row_bias is not None:
        mx = mx + row_bias
    if live is not None:
        mx = jnp.where(live, mx, NEG)
    m_prev = m_ref[p, :, cs]
    m_new = jnp.maximum(m_prev, mx)
    alpha = jnp.exp(m_prev - m_new)
    shift = m_new if row_bias is None else m_new - row_bias
    if live is not None:
        shift = jnp.where(live, shift, -NEG)
    pt = jnp.exp(s - shift)
    acc_ref[p, :, cs] = alpha * acc_ref[p, :, cs] + jnp.dot(vt, pt.astype(BF16),
                                                            preferred_element_type=F32)
    m_ref[p, :, cs] = m_new


def _far_bias(tab_ref, head):
    return tab_ref[head * N_BUCKETS + (N_BUCKETS - 1)]


def _flash_sweep(i, far_args, prev_args, own_args, refs):
    q2_ref, k_ref, v_ref, s_ref, m_ref, acc_ref = refs
    chains = [(p, h) for p in range(N_PAIRS) for h in range(2)]

    def tile(n, args):
        for p, h in chains:
            _flash_scores(p, h, n, q2_ref, k_ref, s_ref)
        for p, h in chains:
            _flash_absorb(p, h, n, *args(p, h), v_ref, s_ref, m_ref, acc_ref)

    def far_body(n, carry):
        tile(n, lambda p, h: far_args(p, h, n))
        return carry

    lax.fori_loop(0, jnp.maximum(i - 1, 0), far_body, 0)

    @pl.when(i >= 1)
    def _():
        tile(i - 1, prev_args)

    tile(i, own_args)


def _diff_kernel(tab_ref, q_ref, k_ref, v_ref, d_ref, lamv_ref, g_ref, o_ref,
                 q2_ref, s_ref, m_ref, acc_ref):
    t = ATT_T
    i = pl.program_id(1)
    _stack_q(q_ref, q2_ref, DIFF_HEAD_DIM ** -0.5)
    _flash_init(m_ref, acc_ref)
    refs = (q2_ref, k_ref, v_ref, s_ref, m_ref, acc_ref)
    _flash_sweep(i,
                 lambda p, h, n: (None, _far_bias(tab_ref, p), None),
                 lambda p, h: (d_ref[0, p, 1, :, h * t:(h + 1) * t], None, None),
                 lambda p, h: (d_ref[0, p, 0, :, h * t:(h + 1) * t], None, None),
                 refs)

    lv = lamv_ref[...]
    lam = (jnp.exp(jnp.sum(lv[0:1] * lv[1:2], axis=1, keepdims=True))
           - jnp.exp(jnp.sum(lv[2:3] * lv[3:4], axis=1, keepdims=True)) + LAM_INIT)
    for p in range(N_PAIRS):
        acc = acc_ref[p, 0:LANES, :]
        l = acc_ref[p, LANES:LANES + 1, :]
        o = acc[:, 0:t] / l[:, 0:t] - lam * (acc[:, t:2 * t] / l[:, t:2 * t])
        y = o * lax.rsqrt(jnp.mean(o * o, axis=0, keepdims=True) + RMS_EPS) * g_ref[...]
        o_ref[0, 0, p * LANES:(p + 1) * LANES, :] = (y * (1.0 - LAM_INIT)).astype(BF16)


def _moba_kernel(tab_ref, q_ref, k_ref, v_ref, d_ref, o_ref,
                 q2_ref, s_ref, m_ref, acc_ref, kmean_ref, sel_ref):
    t = ATT_T
    i = pl.program_id(1)
    nb = k_ref.shape[1] // t
    nbp = kmean_ref.shape[1]

    @pl.when(i == 0)
    def _():
        kmean_ref[...] = jnp.zeros(kmean_ref.shape, F32)
        for p in range(N_PAIRS):
            for n in range(nb):
                blk = k_ref[0, n * t:(n + 1) * t, p * LANES:(p + 1) * LANES].astype(F32)
                kmean_ref[p, n:n + 1, :] = jnp.sum(blk, axis=0, keepdims=True) * (1.0 / t)

    _stack_q(q_ref, q2_ref, MOBA_HEAD_DIM ** -0.5)
    _flash_init(m_ref, acc_ref)

    blk_id = lax.broadcasted_iota(I32, (nbp, 2 * t), 0)
    blk_f = blk_id.astype(F32)
    for p in range(N_PAIRS):
        gs = jnp.dot(kmean_ref[p].astype(BF16), q2_ref[p], preferred_element_type=F32)
        g = jnp.where(blk_id < i, gs, -jnp.inf)
        sel = jnp.full((nbp, 2 * t), NEG, F32)
        for _ in range(MOBA_TOPK):
            mx = jnp.max(g, axis=0, keepdims=True)
            first = jnp.min(jnp.where(g == mx, blk_f, float(nbp)), axis=0, keepdims=True)
            hit = blk_f == first
            sel = jnp.where(hit, 0.0, sel)
            g = jnp.where(hit, -jnp.inf, g)
        sel_ref[p] = sel

    def live(p, h, n):
        return sel_ref[p, pl.ds(n, 1), h * t:(h + 1) * t] > -1.0

    refs = (q2_ref, k_ref, v_ref, s_ref, m_ref, acc_ref)
    _flash_sweep(i,
                 lambda p, h, n: (None, _far_bias(tab_ref, N_DIFF_HEADS + 2 * p + h),
                                  live(p, h, n)),
                 lambda p, h: (d_ref[0, p, 1, :, h * t:(h + 1) * t], None, live(p, h, i - 1)),
                 lambda p, h: (d_ref[0, p, 0, :, h * t:(h + 1) * t], None, None),
                 refs)

    row = lax.broadcasted_iota(I32, (LANES, t), 0)
    for p in range(N_PAIRS):
        acc = acc_ref[p, 0:LANES, :]
        l = acc_ref[p, LANES:LANES + 1, :]
        o = jnp.where(row < 64, acc[:, 0:t] / l[:, 0:t], acc[:, t:2 * t] / l[:, t:2 * t])
        o_ref[0, 0, p * LANES:(p + 1) * LANES, :] = o.astype(BF16)


def _to_kv_major(proj, col0, nq):
    b, s, _ = proj.shape
    w = N_PAIRS * LANES
    return jnp.swapaxes(proj[:, :, col0:col0 + w].reshape(b, nq, ATT_T, w), 2, 3)


def _values_kv_major(proj, col0, nq):
    b, s, _ = proj.shape
    v = proj[:, :, col0:col0 + N_PAIRS * LANES].reshape(b, nq, ATT_T, N_PAIRS, LANES)
    ones = jnp.ones((b, nq, ATT_T, N_PAIRS, V_ROWS - LANES), BF16)
    v = jnp.concatenate([v, ones], axis=-1)
    return jnp.transpose(v, (0, 1, 3, 4, 2)).reshape(b, nq, N_PAIRS * V_ROWS, ATT_T)


def _attention(kind, tab, proj, dtiles, extra, cols):
    b, s, _ = proj.shape
    t = ATT_T
    nq = s // t
    w = N_PAIRS * LANES
    qc, kc, vc = cols
    gidx = 0 if kind == "diff" else 1
    q_t = _to_kv_major(proj, qc, nq)
    v_t = _values_kv_major(proj, vc, nq)
    in_specs = [pl.BlockSpec(memory_space=pltpu.SMEM),
                pl.BlockSpec((1, 1, w, t), lambda bi, i: (bi, i, 0, 0)),
                pl.BlockSpec((1, s, w), lambda bi, i: (bi, 0, kc // w)),
                pl.BlockSpec((1, nq, N_PAIRS * V_ROWS, t), lambda bi, i: (bi, 0, 0, 0)),
                pl.BlockSpec((1, N_PAIRS, 2, t, 2 * t), lambda bi, i: (gidx, 0, 0, 0, 0))]
    scratch = [pltpu.VMEM((N_PAIRS, LANES, 2 * t), BF16),
               pltpu.VMEM((2 * N_PAIRS, t, t), F32),
               pltpu.VMEM((N_PAIRS, 1, 2 * t), F32),
               pltpu.VMEM((N_PAIRS, V_ROWS, 2 * t), F32)]
    if kind == "diff":
        lamv, subln_g = extra
        in_specs += [pl.BlockSpec(lamv.shape, lambda bi, i: (0, 0)),
                     pl.BlockSpec(subln_g.shape, lambda bi, i: (0, 0))]
        args = (tab, q_t, proj, v_t, dtiles, lamv, subln_g)
        body = _diff_kernel
    else:
        nbp = max(32, nq)
        scratch += [pltpu.VMEM((N_PAIRS, nbp, LANES), F32), pltpu.VMEM((N_PAIRS, nbp, 2 * t), F32)]
        args = (tab, q_t, proj, v_t, dtiles)
        body = _moba_kernel
    y_t = pl.pallas_call(
        body,
        out_shape=jax.ShapeDtypeStruct((b, nq, w, t), BF16),
        grid=(b, nq),
        in_specs=in_specs,
        out_specs=pl.BlockSpec((1, 1, w, t), lambda bi, i: (bi, i, 0, 0)),
        scratch_shapes=scratch,
        compiler_params=pltpu.CompilerParams(
            dimension_semantics=("arbitrary", "arbitrary"),
            vmem_limit_bytes=VMEM_LIMIT_BYTES),
        name=kind + "_attn",
    )(*args)
    return jnp.swapaxes(y_t, 2, 3).reshape(b, s, w)


def _merge_kernel(x_ref, ya_ref, yb_ref, gla_ref, glb_ref, wa_ref, wb_ref, wo_ref,
                  gt_ref, shf_ref, scf_ref, gf_ref, wr_ref, br_ref,
                  x1_ref, h2_ref, ti_ref, tw_ref):
    a = jnp.dot(ya_ref[0], wa_ref[...], preferred_element_type=F32)
    b = jnp.dot(yb_ref[0], wb_ref[...], preferred_element_type=F32)
    merged = (jax.nn.sigmoid(gla_ref[0].astype(F32)) * a
              + jax.nn.sigmoid(glb_ref[0].astype(F32)) * b)
    mo = jnp.dot(merged.astype(BF16), wo_ref[...], preferred_element_type=F32)
    x1 = x_ref[0] + gt_ref[0] * mo
    x1_ref[0] = x1
    h = _rms(x1, gf_ref[...]) * (1.0 + scf_ref[0]) + shf_ref[0]
    h2_ref[0] = h
    logits = jnp.dot(h.astype(BF16), wr_ref[...], preferred_element_type=F32) + br_ref[...]
    tm = logits.shape[0]
    lane = lax.broadcasted_iota(I32, (tm, LANES), 1)
    lane_f = lane.astype(F32)
    g = jnp.where(lane < N_EXPERTS, logits, -jnp.inf)
    vals, idxs = [], []
    for _ in range(TOP_K):
        mx = jnp.max(g, axis=1, keepdims=True)
        first = jnp.min(jnp.where(g == mx, lane_f, float(LANES)), axis=1, keepdims=True)
        vals.append(mx)
        idxs.append(first)
        g = jnp.where(lane_f == first, -jnp.inf, g)
    es = [jnp.exp(v - vals[0]) for v in vals]
    den = es[0] + es[1] + es[2] + es[3]
    ti = jnp.zeros((tm, LANES), F32)
    tw = jnp.zeros((tm, LANES), F32)
    for k in range(TOP_K):
        ti = jnp.where(lane == k, idxs[k], ti)
        tw = jnp.where(lane == k, es[k] / den, tw)
    ti_ref[0] = ti.astype(I32)
    tw_ref[0] = tw


def _merge(x, proj, y_a, y_b, w_br_a, w_br_b, w_out, mod, g_ffn, w_router, b_router):
    b, s, d = x.shape
    tm = min(MERGE_TM, s)
    full = lambda shape: pl.BlockSpec(shape, lambda bi, i: (0,) * len(shape))
    tile = lambda w, c=0: pl.BlockSpec((1, tm, w), lambda bi, i: (bi, i, c))
    modspec = lambda j: pl.BlockSpec((1, 1, d), lambda bi, i: (bi * 6 + j, 0, 0))
    return pl.pallas_call(
        _merge_kernel,
        out_shape=(jax.ShapeDtypeStruct((b, s, d), F32),
                   jax.ShapeDtypeStruct((b, s, d), F32),
                   jax.ShapeDtypeStruct((b, s, LANES), I32),
                   jax.ShapeDtypeStruct((b, s, LANES), F32)),
        grid=(b, s // tm),
        in_specs=[tile(d), tile(512), tile(512), tile(d, 3), tile(d, 4),
                  full(w_br_a.shape), full(w_br_b.shape), full(w_out.shape),
                  modspec(2), modspec(3), modspec(4), full(g_ffn.shape),
                  full(w_router.shape), full(b_router.shape)],
        out_specs=(tile(d), tile(d), tile(LANES), tile(LANES)),
        compiler_params=pltpu.CompilerParams(dimension_semantics=("arbitrary", "arbitrary"),
                                             vmem_limit_bytes=VMEM_LIMIT_BYTES),
        name="merge",
    )(x, y_a, y_b, proj, proj, w_br_a, w_br_b, w_out, mod, mod, mod, g_ffn,
      w_router, b_router)


def _moe_kernel(te_ref, nu_ref, tok_ref, h2_ref, wgu_ref, bgu_ref, wd_ref, bd_ref,
                ys_ref, xbuf, sems, wgu_bf, wd_bf, gu_ref):
    r = pl.program_id(0)
    n_used = nu_ref[0]
    tm = xbuf.shape[1]
    cur = r % 2
    prv = 1 - cur

    def start_row(j):
        pltpu.make_async_copy(h2_ref.at[pl.ds(tok_ref[j], 1)], xbuf.at[cur, pl.ds(j, 1)],
                              sems.at[cur]).start()

    def wait_tile(s):
        pltpu.make_async_copy(h2_ref.at[pl.ds(0, tm)], xbuf.at[s], sems.at[s]).wait()

    @pl.when(r == 0)
    def _():
        def body(j, carry):
            start_row(j)
            return carry
        lax.fori_loop(0, tm, body, 0)

    busy = jnp.logical_and(r >= 1, r <= n_used)
    e_prev = te_ref[jnp.maximum(r - 1, 0)]
    fresh = jnp.logical_or(r == 1, e_prev != te_ref[jnp.maximum(r - 2, 0)])

    @pl.when(jnp.logical_and(busy, fresh))
    def _():
        wgu_bf[...] = wgu_ref[0].astype(BF16)
        wd_bf[...] = wd_ref[0].astype(BF16)

    @pl.when(busy)
    def _():
        wait_tile(prv)
        x = xbuf[prv].astype(BF16)
        fc = MOE_FF_CHUNK
        n_gu = 2 * D_FF // fc
        rows_per = tm // n_gu
        for c in range(n_gu):
            lo, hi = c * fc, (c + 1) * fc
            gu_ref[:, lo:hi] = (jnp.dot(x, wgu_bf[:, lo:hi], preferred_element_type=F32)
                                + bgu_ref[0][:, lo:hi])
            for j in range(c * rows_per, (c + 1) * rows_per):
                start_row(j)
        y = None
        for c in range(D_FF // fc):
            lo, hi = c * fc, (c + 1) * fc
            gate = jnp.minimum(gu_ref[:, lo:hi], SWIGLU_LIMIT)
            up = jnp.clip(gu_ref[:, D_FF + lo:D_FF + hi], -SWIGLU_LIMIT, SWIGLU_LIMIT)
            act = (up + 1.0) * gate * jax.nn.sigmoid(SWIGLU_ALPHA * gate)
            part = jnp.dot(act.astype(BF16), wd_bf[lo:hi, :], preferred_element_type=F32)
            y = part if y is None else y + part
        ys_ref[...] = y + bd_ref[0]

        @pl.when(r == n_used)
        def _():
            wait_tile(cur)


def _moe(h2, row_token, tile_expert, n_used, w_gate_up, b_gate_up, w_down, b_down):
    _, d = h2.shape
    tm = MOE_TM
    n_tiles = tile_expert.shape[0]
    e, _, f2 = w_gate_up.shape
    wsel = lambda r, te, nu: (te[jnp.clip(r - 1, 0, n_tiles - 1)], 0, 0)
    return pl.pallas_call(
        _moe_kernel,
        out_shape=jax.ShapeDtypeStruct((n_tiles * tm, d), F32),
        grid_spec=pltpu.PrefetchScalarGridSpec(
            num_scalar_prefetch=2,
            grid=(n_tiles + 1,),
            in_specs=[pl.BlockSpec((tm,), lambda r, te, nu: (jnp.minimum(r, n_tiles - 1),),
                                   memory_space=pltpu.SMEM),
                      pl.BlockSpec(memory_space=pl.ANY),
                      pl.BlockSpec((1, d, f2), wsel),
                      pl.BlockSpec((1, 1, f2), wsel),
                      pl.BlockSpec((1, D_FF, d), wsel),
                      pl.BlockSpec((1, 1, d), wsel)],
            out_specs=pl.BlockSpec(
                (tm, d), lambda r, te, nu: (jnp.clip(r - 1, 0, nu[0] - 1), 0)),
            scratch_shapes=[pltpu.VMEM((2, tm, d), F32), pltpu.SemaphoreType.DMA((2,)),
                            pltpu.VMEM((d, f2), BF16), pltpu.VMEM((D_FF, d), BF16),
                            pltpu.VMEM((tm, f2), F32)]),
        compiler_params=pltpu.CompilerParams(dimension_semantics=("arbitrary",),
                                             vmem_limit_bytes=VMEM_LIMIT_BYTES),
        name="moe",
    )(tile_expert, n_used, row_token, h2, w_gate_up, b_gate_up, w_down, b_down)


def _combine_kernel(pos_ref, ys_ref, x1_ref, tw_ref, gt_ref, gf_ref, o_ref, gbuf, sem):
    tm = x1_ref.shape[0]

    def issue(j, carry):
        for k in range(TOP_K):
            pltpu.make_async_copy(ys_ref.at[pl.ds(pos_ref[j * TOP_K + k], 1)],
                                  gbuf.at[k, pl.ds(j, 1)], sem).start()
        return carry

    lax.fori_loop(0, tm, issue, 0)

    def drain(j, carry):
        for k in range(TOP_K):
            pltpu.make_async_copy(ys_ref.at[pl.ds(0, 1)], gbuf.at[k, pl.ds(0, 1)], sem).wait()
        return carry

    lax.fori_loop(0, tm, drain, 0)

    tw = tw_ref[...]
    acc = tw[:, 0:1] * gbuf[0]
    for k in range(1, TOP_K):
        acc = acc + tw[:, k:k + 1] * gbuf[k]
    x2 = x1_ref[...] + gt_ref[0] * acc
    o_ref[...] = _rms(x2, gf_ref[...])


def _combine(ys, pos_flat, x1, tw, mod, g_final, seq):
    t, d = x1.shape
    tm = min(COMBINE_TM, t)
    return pl.pallas_call(
        _combine_kernel,
        out_shape=jax.ShapeDtypeStruct((t, d), F32),
        grid=(t // tm,),
        in_specs=[pl.BlockSpec((tm * TOP_K,), lambda s: (s,), memory_space=pltpu.SMEM),
                  pl.BlockSpec(memory_space=pl.ANY),
                  pl.BlockSpec((tm, d), lambda s: (s, 0)),
                  pl.BlockSpec((tm, LANES), lambda s: (s, 0)),
                  pl.BlockSpec((1, 1, d), lambda s: ((s * tm) // seq * 6 + 5, 0, 0)),
                  pl.BlockSpec((1, d), lambda s: (0, 0))],
        out_specs=pl.BlockSpec((tm, d), lambda s: (s, 0)),
        scratch_shapes=[pltpu.VMEM((TOP_K, tm, d), F32), pltpu.SemaphoreType.DMA],
        compiler_params=pltpu.CompilerParams(dimension_semantics=("arbitrary",),
                                             vmem_limit_bytes=VMEM_LIMIT_BYTES),
        name="combine",
    )(pos_flat, ys, x1, tw, mod, g_final)


def _routing_tables(top_idx, n_tiles):
    tm = MOE_TM
    e_flat = top_idx.reshape(-1)
    onehot = (e_flat[:, None] == jnp.arange(N_EXPERTS, dtype=I32)[None, :]).astype(I32)
    csum = jnp.cumsum(onehot, axis=0)
    counts = csum[-1]
    rank = jnp.sum(csum * onehot, axis=1) - 1
    tiles_per = (counts + tm - 1) // tm
    tile_end = jnp.cumsum(tiles_per)
    tile_start = tile_end - tiles_per
    pos = (jnp.sum(onehot * tile_start[None, :], axis=1) * tm + rank).astype(I32)
    n_used = tile_end[-1:].astype(I32)
    tile_ids = jnp.arange(n_tiles, dtype=I32)
    tile_expert = jnp.minimum(
        jnp.sum((tile_ids[:, None] >= tile_end[None, :]).astype(I32), axis=1),
        N_EXPERTS - 1).astype(I32)
    big = 1 << 17
    assert e_flat.shape[0] < big
    experts = jnp.arange(N_EXPERTS, dtype=I32)[:, None]
    slot = jnp.arange(tm, dtype=I32)[None, :]
    n_pad = (tiles_per * tm - counts)[:, None]
    key_pad = jnp.where(slot < n_pad, experts * big + counts[:, None] + slot,
                        jnp.iinfo(jnp.int32).max)
    keys = jnp.concatenate([e_flat * big + rank, key_pad.reshape(-1)]).astype(I32)
    toks = jnp.concatenate([jnp.arange(e_flat.shape[0], dtype=I32) // TOP_K,
                            jnp.zeros((N_EXPERTS * tm,), I32)])
    _, row_token = lax.sort((keys, toks), num_keys=1)
    return pos, tile_expert, n_used, row_token


def kernel(x, c, rel_bias, w_ada, b_ada, g_mix, w_in, lambda_q1, lambda_k1, lambda_q2,
           lambda_k2, subln_g, w_br_a, w_br_b, w_out, g_ffn, w_router, b_router,
           w_gate_up, b_gate_up, w_down, b_down, g_final):
    b, s, d = x.shape
    assert d == D_MODEL and s % ATT_T == 0 and MOBA_BLOCK == ATT_T
    t_tok = b * s

    c8 = jnp.zeros((8, d), F32).at[:b].set(c)
    ada = _ada(c8, w_ada[0], b_ada[0].reshape(1, -1))
    mod = ada[:b].reshape(b * 6, 1, d)

    dtiles = _bias_tiles(rel_bias)
    tab = rel_bias.T.reshape(-1)

    proj = _proj(x, g_mix[0].reshape(1, d), mod, w_in[0].astype(BF16))

    lamv = jnp.concatenate([lambda_q1, lambda_k1, lambda_q2, lambda_k2], axis=0)
    y_a = _attention("diff", tab, proj, dtiles, (lamv, subln_g[0].reshape(-1, 1)), (0, 512, 1024))
    y_b = _attention("moba", tab, proj, dtiles, None, (1536, 2048, 2560))

    wr = jnp.zeros((d, LANES), BF16).at[:, :N_EXPERTS].set(w_router[0].astype(BF16))
    br = jnp.zeros((1, LANES), F32).at[0, :N_EXPERTS].set(b_router[0])
    x1, h2, ti, tw = _merge(x, proj, y_a, y_b, w_br_a[0].astype(BF16), w_br_b[0].astype(BF16),
                            w_out[0].astype(BF16), mod, g_ffn[0].reshape(1, d), wr, br)

    n_tiles = (t_tok * TOP_K) // MOE_TM + N_EXPERTS
    top_idx = ti.reshape(t_tok, LANES)[:, :TOP_K]
    pos, tile_expert, n_used, row_token = _routing_tables(top_idx, n_tiles)

    ys = _moe(h2.reshape(t_tok, d), row_token, tile_expert, n_used, w_gate_up[0],
              b_gate_up[0].reshape(N_EXPERTS, 1, -1), w_down[0],
              b_down[0].reshape(N_EXPERTS, 1, -1))
    out = _combine(ys, pos, x1.reshape(t_tok, d), tw.reshape(t_tok, LANES), mod,
                   g_final.reshape(1, d), s)
    return out.reshape(b, s, d)
```

```python
import functools
import math

import jax
import jax.numpy as jnp
import numpy as np
from jax import lax
from jax.experimental import pallas as pl
from jax.experimental.pallas import tpu as pltpu

F32 = jnp.float32
BF16 = jnp.bfloat16
I32 = jnp.int32

D_MODEL = 1024
N_DIFF_HEADS = 4
DIFF_HEAD_DIM = 64
N_MOBA_HEADS = 8
MOBA_HEAD_DIM = 64
MOBA_BLOCK = 256
MOBA_TOPK = 3
N_HEADS_TOTAL = N_DIFF_HEADS + N_MOBA_HEADS
N_BUCKETS = 32
MAX_DISTANCE = 128
N_EXPERTS = 32
TOP_K = 4
D_FF = D_MODEL
SWIGLU_LIMIT = 7.0
SWIGLU_ALPHA = 1.702
RMS_EPS = 1e-5
LAM_INIT = 0.8 - 0.6 * math.exp(-0.3 * 0)
IN_WIDTH = 5120

LANES = 128
VMEM_LIMIT_BYTES = 56 * 1024 * 1024

ATT_T = 256
N_PAIRS = 4
V_ROWS = LANES + 16
PROJ_TM = 512
MERGE_TM = 512
MOE_TM = 256
MOE_FF_CHUNK = 256
COMBINE_TM = 256
NEG = -1e30


def _t5_bucket_np(dist):
    n = np.maximum(dist, 0)
    max_exact = N_BUCKETS // 2
    nf = np.maximum(n, max_exact).astype(np.float32)
    large = max_exact + (np.log(nf / max_exact) / math.log(MAX_DISTANCE / max_exact)
                         * (N_BUCKETS - max_exact)).astype(np.int32)
    large = np.minimum(large, N_BUCKETS - 1)
    return np.where(n < max_exact, n, large).astype(np.int32)


def _rms(x, g):
    return x * lax.rsqrt(jnp.mean(x * x, axis=-1, keepdims=True) + RMS_EPS) * g


ROW_CHUNKS = D_MODEL // LANES


def _store_token_tiles(ref, x):
    n = x.shape[0]
    for c in range(ROW_CHUNKS):
        ref[pl.ds(c, n, stride=ROW_CHUNKS), :] = x[:, c * LANES:(c + 1) * LANES]


def _load_token_tiles(ref):
    n = ref.shape[0] // ROW_CHUNKS
    return jnp.concatenate([ref[pl.ds(c, n, stride=ROW_CHUNKS), :] for c in range(ROW_CHUNKS)],
                           axis=1)


def _ada_kernel(c_ref, w_ref, b_ref, o_ref):
    o_ref[...] = jnp.dot(c_ref[...].astype(BF16), w_ref[...].astype(BF16),
                         preferred_element_type=F32) + b_ref[...]


def _ada(c8, w, b):
    d, n = w.shape
    tn = 1536
    return pl.pallas_call(
        _ada_kernel,
        out_shape=jax.ShapeDtypeStruct((8, n), F32),
        grid=(n // tn,),
        in_specs=[pl.BlockSpec((8, d), lambda j: (0, 0)),
                  pl.BlockSpec((d, tn), lambda j: (0, j)),
                  pl.BlockSpec((1, tn), lambda j: (0, j))],
        out_specs=pl.BlockSpec((8, tn), lambda j: (0, j)),
        compiler_params=pltpu.CompilerParams(dimension_semantics=("arbitrary",),
                                             vmem_limit_bytes=VMEM_LIMIT_BYTES),
        name="ada",
    )(c8, w, b)


def _bias_kernel(tab_ref, bd_ref, bp_ref, o_ref):
    g = pl.program_id(0)
    p = pl.program_id(1)
    t = pl.program_id(2)
    head = jnp.where(g == 0, p, N_DIFF_HEADS + 2 * p + t)
    bd = bd_ref[...]
    bp = bp_ref[...]
    d0 = jnp.where(bd < 0, NEG, 0.0).astype(F32)
    d1 = jnp.zeros(bp.shape, F32)
    for b in range(N_BUCKETS):
        val = tab_ref[head * N_BUCKETS + b]
        d0 = jnp.where(bd == b, val, d0)
        d1 = jnp.where(bp == b, val, d1)
    o_ref[0, 0, 0] = d0
    o_ref[0, 0, 1] = d1


def _bias_tiles(rel_bias):
    t = ATT_T
    jj = np.arange(t)[:, None]
    ii = np.arange(t)[None, :]
    bd = np.where(ii >= jj, _t5_bucket_np(ii - jj), -1).astype(np.int32)
    bp = _t5_bucket_np(t + ii - jj)
    tab = rel_bias.T.reshape(-1)
    return pl.pallas_call(
        _bias_kernel,
        out_shape=jax.ShapeDtypeStruct((2, N_PAIRS, 2, t, 2 * t), F32),
        grid=(2, N_PAIRS, 2),
        in_specs=[pl.BlockSpec(memory_space=pltpu.SMEM),
                  pl.BlockSpec((t, t), lambda g, p, h: (0, 0)),
                  pl.BlockSpec((t, t), lambda g, p, h: (0, 0))],
        out_specs=pl.BlockSpec((1, 1, 2, t, t), lambda g, p, h: (g, p, 0, 0, h)),
        compiler_params=pltpu.CompilerParams(
            dimension_semantics=("arbitrary", "arbitrary", "arbitrary")),
        name="bias_tiles",
    )(tab, jnp.asarray(bd), jnp.asarray(bp))


def _proj_kernel(x_ref, g_ref, sh_ref, sc_ref, w_ref, o_ref):
    x = x_ref[0]
    h = _rms(x, g_ref[...]) * (1.0 + sc_ref[0]) + sh_ref[0]
    hb = h.astype(BF16)
    n_total = w_ref.shape[1]
    for n0 in range(0, n_total, 512):
        o_ref[0, :, n0:n0 + 512] = jnp.dot(
            hb, w_ref[:, n0:n0 + 512], preferred_element_type=F32).astype(BF16)


def _proj(x, g_mix, mod, w_in_bf):
    b, s, d = x.shape
    n = w_in_bf.shape[1]
    tm = min(PROJ_TM, s)
    return pl.pallas_call(
        _proj_kernel,
        out_shape=jax.ShapeDtypeStruct((b, s, n), BF16),
        grid=(b, s // tm),
        in_specs=[pl.BlockSpec((1, tm, d), lambda bi, i: (bi, i, 0)),
                  pl.BlockSpec((1, d), lambda bi, i: (0, 0)),
                  pl.BlockSpec((1, 1, d), lambda bi, i: (bi * 6 + 0, 0, 0)),
                  pl.BlockSpec((1, 1, d), lambda bi, i: (bi * 6 + 1, 0, 0)),
                  pl.BlockSpec((d, n), lambda bi, i: (0, 0))],
        out_specs=pl.BlockSpec((1, tm, n), lambda bi, i: (bi, i, 0)),
        compiler_params=pltpu.CompilerParams(dimension_semantics=("arbitrary", "arbitrary"),
                                             vmem_limit_bytes=VMEM_LIMIT_BYTES),
        name="proj",
    )(x, g_mix, mod, mod, w_in_bf)


def _stack_q(q_ref, q2_ref, scale):
    t = ATT_T
    row = lax.broadcasted_iota(I32, (LANES, t), 0)
    for p in range(N_PAIRS):
        q = q_ref[0, 0, p * LANES:(p + 1) * LANES, :].astype(F32) * scale
        q2_ref[p, :, 0:t] = jnp.where(row < 64, q, 0.0).astype(BF16)
        q2_ref[p, :, t:2 * t] = jnp.where(row >= 64, q, 0.0).astype(BF16)


def _flash_init(m_ref, acc_ref):
    m_ref[...] = jnp.full(m_ref.shape, NEG, F32)
    acc_ref[...] = jnp.zeros(acc_ref.shape, F32)


def _flash_scores(p, h, n, q2_ref, k_ref, s_ref):
    t = ATT_T
    kt = k_ref[0, pl.ds(pl.multiple_of(n * t, t), t), p * LANES:(p + 1) * LANES]
    s_ref[2 * p + h] = jnp.dot(kt, q2_ref[p, :, h * t:(h + 1) * t],
                               preferred_element_type=F32)


def _flash_absorb(p, h, n, tile_bias, row_bias, live, v_ref, s_ref, m_ref, acc_ref):
    t = ATT_T
    cs = slice(h * t, (h + 1) * t)
    vt = v_ref[0, n, p * V_ROWS:(p + 1) * V_ROWS, :]
    s = s_ref[2 * p + h]
    if tile_bias is not None:
        s = s + tile_bias
    mx = jnp.max(s, axis=0, keepdims=True)
    if row_bias is not None:
        mx = mx + row_bias
    if live is not None:
        mx = jnp.where(live, mx, NEG)
    m_prev = m_ref[p, :, cs]
    m_new = jnp.maximum(m_prev, mx)
    alpha = jnp.exp(m_prev - m_new)
    shift = m_new if row_bias is None else m_new - row_bias
    if live is not None:
        shift = jnp.where(live, shift, -NEG)
    pt = jnp.exp(s - shift)
    acc_ref[p, :, cs] = alpha * acc_ref[p, :, cs] + jnp.dot(vt, pt.astype(BF16),
                                                            preferred_element_type=F32)
    m_ref[p, :, cs] = m_new


def _far_bias(tab_ref, head):
    return tab_ref[head * N_BUCKETS + (N_BUCKETS - 1)]


def _flash_sweep(i, far_args, prev_args, own_args, refs):
    q2_ref, k_ref, v_ref, s_ref, m_ref, acc_ref = refs
    chains = [(p, h) for p in range(N_PAIRS) for h in range(2)]

    def tile(n, args):
        for p, h in chains:
            _flash_scores(p, h, n, q2_ref, k_ref, s_ref)
        for p, h in chains:
            _flash_absorb(p, h, n, *args(p, h), v_ref, s_ref, m_ref, acc_ref)

    def far_body(n, carry):
        tile(n, lambda p, h: far_args(p, h, n))
        return carry

    lax.fori_loop(0, jnp.maximum(i - 1, 0), far_body, 0)

    @pl.when(i >= 1)
    def _():
        tile(i - 1, prev_args)

    tile(i, own_args)


def _diff_kernel(tab_ref, q_ref, k_ref, v_ref, d_ref, lamv_ref, g_ref, o_ref,
                 q2_ref, s_ref, m_ref, acc_ref):
    t = ATT_T
    i = pl.program_id(1)
    _stack_q(q_ref, q2_ref, DIFF_HEAD_DIM ** -0.5)
    _flash_init(m_ref, acc_ref)
    refs = (q2_ref, k_ref, v_ref, s_ref, m_ref, acc_ref)
    _flash_sweep(i,
                 lambda p, h, n: (None, _far_bias(tab_ref, p), None),
                 lambda p, h: (d_ref[0, p, 1, :, h * t:(h + 1) * t], None, None),
                 lambda p, h: (d_ref[0, p, 0, :, h * t:(h + 1) * t], None, None),
                 refs)

    lv = lamv_ref[...]
    lam = (jnp.exp(jnp.sum(lv[0:1] * lv[1:2], axis=1, keepdims=True))
           - jnp.exp(jnp.sum(lv[2:3] * lv[3:4], axis=1, keepdims=True)) + LAM_INIT)
    for p in range(N_PAIRS):
        acc = acc_ref[p, 0:LANES, :]
        l = acc_ref[p, LANES:LANES + 1, :]
        o = acc[:, 0:t] / l[:, 0:t] - lam * (acc[:, t:2 * t] / l[:, t:2 * t])
        y = o * lax.rsqrt(jnp.mean(o * o, axis=0, keepdims=True) + RMS_EPS) * g_ref[...]
        o_ref[0, 0, p * LANES:(p + 1) * LANES, :] = (y * (1.0 - LAM_INIT)).astype(BF16)


def _moba_kernel(tab_ref, q_ref, k_ref, v_ref, d_ref, o_ref,
                 q2_ref, s_ref, m_ref, acc_ref, kmean_ref, sel_ref):
    t = ATT_T
    i = pl.program_id(1)
    nb = k_ref.shape[1] // t
    nbp = kmean_ref.shape[1]

    @pl.when(i == 0)
    def _():
        kmean_ref[...] = jnp.zeros(kmean_ref.shape, F32)
        for p in range(N_PAIRS):
            for n in range(nb):
                blk = k_ref[0, n * t:(n + 1) * t, p * LANES:(p + 1) * LANES].astype(F32)
                kmean_ref[p, n:n + 1, :] = jnp.sum(blk, axis=0, keepdims=True) * (1.0 / t)

    _stack_q(q_ref, q2_ref, MOBA_HEAD_DIM ** -0.5)
    _flash_init(m_ref, acc_ref)

    blk_id = lax.broadcasted_iota(I32, (nbp, 2 * t), 0)
    blk_f = blk_id.astype(F32)
    for p in range(N_PAIRS):
        gs = jnp.dot(kmean_ref[p].astype(BF16), q2_ref[p], preferred_element_type=F32)
        g = jnp.where(blk_id < i, gs, -jnp.inf)
        sel = jnp.full((nbp, 2 * t), NEG, F32)
        for _ in range(MOBA_TOPK):
            mx = jnp.max(g, axis=0, keepdims=True)
            first = jnp.min(jnp.where(g == mx, blk_f, float(nbp)), axis=0, keepdims=True)
            hit = blk_f == first
            sel = jnp.where(hit, 0.0, sel)
            g = jnp.where(hit, -jnp.inf, g)
        sel_ref[p] = sel

    def live(p, h, n):
        return sel_ref[p, pl.ds(n, 1), h * t:(h + 1) * t] > -1.0

    refs = (q2_ref, k_ref, v_ref, s_ref, m_ref, acc_ref)
    _flash_sweep(i,
                 lambda p, h, n: (None, _far_bias(tab_ref, N_DIFF_HEADS + 2 * p + h),
                                  live(p, h, n)),
                 lambda p, h: (d_ref[0, p, 1, :, h * t:(h + 1) * t], None, live(p, h, i - 1)),
                 lambda p, h: (d_ref[0, p, 0, :, h * t:(h + 1) * t], None, None),
                 refs)

    row = lax.broadcasted_iota(I32, (LANES, t), 0)
    for p in range(N_PAIRS):
        acc = acc_ref[p, 0:LANES, :]
        l = acc_ref[p, LANES:LANES + 1, :]
        o = jnp.where(row < 64, acc[:, 0:t] / l[:, 0:t], acc[:, t:2 * t] / l[:, t:2 * t])
        o_ref[0, 0, p * LANES:(p + 1) * LANES, :] = o.astype(BF16)


def _to_kv_major(proj, col0, nq):
    b, s, _ = proj.shape
    w = N_PAIRS * LANES
    return jnp.swapaxes(proj[:, :, col0:col0 + w].reshape(b, nq, ATT_T, w), 2, 3)


def _values_kv_major(proj, col0, nq):
    b, s, _ = proj.shape
    v = proj[:, :, col0:col0 + N_PAIRS * LANES].reshape(b, nq, ATT_T, N_PAIRS, LANES)
    ones = jnp.ones((b, nq, ATT_T, N_PAIRS, V_ROWS - LANES), BF16)
    v = jnp.concatenate([v, ones], axis=-1)
    return jnp.transpose(v, (0, 1, 3, 4, 2)).reshape(b, nq, N_PAIRS * V_ROWS, ATT_T)


def _attention(kind, tab, proj, dtiles, extra, cols):
    b, s, _ = proj.shape
    t = ATT_T
    nq = s // t
    w = N_PAIRS * LANES
    qc, kc, vc = cols
    gidx = 0 if kind == "diff" else 1
    q_t = _to_kv_major(proj, qc, nq)
    v_t = _values_kv_major(proj, vc, nq)
    in_specs = [pl.BlockSpec(memory_space=pltpu.SMEM),
                pl.BlockSpec((1, 1, w, t), lambda bi, i: (bi, i, 0, 0)),
                pl.BlockSpec((1, s, w), lambda bi, i: (bi, 0, kc // w)),
                pl.BlockSpec((1, nq, N_PAIRS * V_ROWS, t), lambda bi, i: (bi, 0, 0, 0)),
                pl.BlockSpec((1, N_PAIRS, 2, t, 2 * t), lambda bi, i: (gidx, 0, 0, 0, 0))]
    scratch = [pltpu.VMEM((N_PAIRS, LANES, 2 * t), BF16),
               pltpu.VMEM((2 * N_PAIRS, t, t), F32),
               pltpu.VMEM((N_PAIRS, 1, 2 * t), F32),
               pltpu.VMEM((N_PAIRS, V_ROWS, 2 * t), F32)]
    if kind == "diff":
        lamv, subln_g = extra
        in_specs += [pl.BlockSpec(lamv.shape, lambda bi, i: (0, 0)),
                     pl.BlockSpec(subln_g.shape, lambda bi, i: (0, 0))]
        args = (tab, q_t, proj, v_t, dtiles, lamv, subln_g)
        body = _diff_kernel
    else:
        nbp = max(32, nq)
        scratch += [pltpu.VMEM((N_PAIRS, nbp, LANES), F32), pltpu.VMEM((N_PAIRS, nbp, 2 * t), F32)]
        args = (tab, q_t, proj, v_t, dtiles)
        body = _moba_kernel
    y_t = pl.pallas_call(
        body,
        out_shape=jax.ShapeDtypeStruct((b, nq, w, t), BF16),
        grid=(b, nq),
        in_specs=in_specs,
        out_specs=pl.BlockSpec((1, 1, w, t), lambda bi, i: (bi, i, 0, 0)),
        scratch_shapes=scratch,
        compiler_params=pltpu.CompilerParams(
            dimension_semantics=("arbitrary", "arbitrary"),
            vmem_limit_bytes=VMEM_LIMIT_BYTES),
        name=kind + "_attn",
    )(*args)
    return jnp.swapaxes(y_t, 2, 3).reshape(b, s, w)


def _merge_kernel(x_ref, ya_ref, yb_ref, gla_ref, glb_ref, wa_ref, wb_ref, wo_ref,
                  gt_ref, shf_ref, scf_ref, gf_ref, wr_ref, br_ref,
                  x1_ref, h2_ref, ti_ref, tw_ref):
    a = jnp.dot(ya_ref[0], wa_ref[...], preferred_element_type=F32)
    b = jnp.dot(yb_ref[0], wb_ref[...], preferred_element_type=F32)
    merged = (jax.nn.sigmoid(gla_ref[0].astype(F32)) * a
              + jax.nn.sigmoid(glb_ref[0].astype(F32)) * b)
    mo = jnp.dot(merged.astype(BF16), wo_ref[...], preferred_element_type=F32)
    x1 = x_ref[0] + gt_ref[0] * mo
    x1_ref[0] = x1
    h = _rms(x1, gf_ref[...]) * (1.0 + scf_ref[0]) + shf_ref[0]
    _store_token_tiles(h2_ref.at[0], h)
    logits = jnp.dot(h.astype(BF16), wr_ref[...], preferred_element_type=F32) + br_ref[...]
    tm = logits.shape[0]
    lane = lax.broadcasted_iota(I32, (tm, LANES), 1)
    lane_f = lane.astype(F32)
    g = jnp.where(lane < N_EXPERTS, logits, -jnp.inf)
    vals, idxs = [], []
    for _ in range(TOP_K):
        mx = jnp.max(g, axis=1, keepdims=True)
        first = jnp.min(jnp.where(g == mx, lane_f, float(LANES)), axis=1, keepdims=True)
        vals.append(mx)
        idxs.append(first)
        g = jnp.where(lane_f == first, -jnp.inf, g)
    es = [jnp.exp(v - vals[0]) for v in vals]
    den = es[0] + es[1] + es[2] + es[3]
    ti = jnp.zeros((tm, LANES), F32)
    tw = jnp.zeros((tm, LANES), F32)
    for k in range(TOP_K):
        ti = jnp.where(lane == k, idxs[k], ti)
        tw = jnp.where(lane == k, es[k] / den, tw)
    ti_ref[0] = ti.astype(I32)
    tw_ref[0] = tw


def _merge(x, proj, y_a, y_b, w_br_a, w_br_b, w_out, mod, g_ffn, w_router, b_router):
    b, s, d = x.shape
    tm = min(MERGE_TM, s)
    full = lambda shape: pl.BlockSpec(shape, lambda bi, i: (0,) * len(shape))
    tile = lambda w, c=0: pl.BlockSpec((1, tm, w), lambda bi, i: (bi, i, c))
    modspec = lambda j: pl.BlockSpec((1, 1, d), lambda bi, i: (bi * 6 + j, 0, 0))
    return pl.pallas_call(
        _merge_kernel,
        out_shape=(jax.ShapeDtypeStruct((b, s, d), F32),
                   jax.ShapeDtypeStruct((b, s * ROW_CHUNKS, LANES), F32),
                   jax.ShapeDtypeStruct((b, s, LANES), I32),
                   jax.ShapeDtypeStruct((b, s, LANES), F32)),
        grid=(b, s // tm),
        in_specs=[tile(d), tile(512), tile(512), tile(d, 3), tile(d, 4),
                  full(w_br_a.shape), full(w_br_b.shape), full(w_out.shape),
                  modspec(2), modspec(3), modspec(4), full(g_ffn.shape),
                  full(w_router.shape), full(b_router.shape)],
        out_specs=(tile(d), pl.BlockSpec((1, tm * ROW_CHUNKS, LANES), lambda bi, i: (bi, i, 0)),
                   tile(LANES), tile(LANES)),
        compiler_params=pltpu.CompilerParams(dimension_semantics=("arbitrary", "arbitrary"),
                                             vmem_limit_bytes=VMEM_LIMIT_BYTES),
        name="merge",
    )(x, y_a, y_b, proj, proj, w_br_a, w_br_b, w_out, mod, mod, mod, g_ffn,
      w_router, b_router)


def _moe_kernel(te_ref, nu_ref, tok_ref, h2_ref, wgu_ref, bgu_ref, wd_ref, bd_ref,
                ys_ref, xbuf, sems, wgu_bf, wd_bf, gu_ref):
    r = pl.program_id(0)
    n_used = nu_ref[0]
    rc = ROW_CHUNKS
    tm = xbuf.shape[1] // rc
    cur = r % 2
    prv = 1 - cur

    def start_row(j, priority):
        src = h2_ref.at[pl.ds(pl.multiple_of(tok_ref[j], rc), rc)]
        dst = xbuf.at[cur, pl.ds(pl.multiple_of(j * rc, rc), rc)]
        pltpu.make_async_copy(src, dst, sems.at[cur]).start(priority=priority)

    def wait_tile(s):
        pltpu.make_async_copy(h2_ref.at[pl.ds(0, tm * rc)], xbuf.at[s], sems.at[s]).wait()

    @pl.when(r == 0)
    def _():
        def body(j, carry):
            start_row(j, 0)
            return carry
        lax.fori_loop(0, tm, body, 0)

    busy = jnp.logical_and(r >= 1, r <= n_used)
    e_prev = te_ref[jnp.maximum(r - 1, 0)]
    fresh = jnp.logical_or(r == 1, e_prev != te_ref[jnp.maximum(r - 2, 0)])

    @pl.when(jnp.logical_and(busy, fresh))
    def _():
        wgu_bf[...] = wgu_ref[0].astype(BF16)
        wd_bf[...] = wd_ref[0].astype(BF16)

    @pl.when(busy)
    def _():
        wait_tile(prv)
        x = _load_token_tiles(xbuf.at[prv]).astype(BF16)
        fc = MOE_FF_CHUNK
        n_gu = 2 * D_FF // fc
        rows_per = tm // n_gu
        for c in range(n_gu):
            lo, hi = c * fc, (c + 1) * fc
            gu_ref[:, lo:hi] = (jnp.dot(x, wgu_bf[:, lo:hi], preferred_element_type=F32)
                                + bgu_ref[0][:, lo:hi])
            for j in range(c * rows_per, (c + 1) * rows_per):
                start_row(j, j % 2)
        y = None
        for c in range(D_FF // fc):
            lo, hi = c * fc, (c + 1) * fc
            gate = jnp.minimum(gu_ref[:, lo:hi], SWIGLU_LIMIT)
            up = jnp.clip(gu_ref[:, D_FF + lo:D_FF + hi], -SWIGLU_LIMIT, SWIGLU_LIMIT)
            act = (up + 1.0) * gate * jax.nn.sigmoid(SWIGLU_ALPHA * gate)
            part = jnp.dot(act.astype(BF16), wd_bf[lo:hi, :], preferred_element_type=F32)
            y = part if y is None else y + part
        _store_token_tiles(ys_ref, y + bd_ref[0])

        @pl.when(r == n_used)
        def _():
            wait_tile(cur)


def _moe(h2_tiles, row_start, tile_expert, n_used, w_gate_up, b_gate_up, w_down, b_down):
    d = D_MODEL
    rc = ROW_CHUNKS
    tm = MOE_TM
    n_tiles = tile_expert.shape[0]
    e, _, f2 = w_gate_up.shape
    wsel = lambda r, te, nu: (te[jnp.clip(r - 1, 0, n_tiles - 1)], 0, 0)
    return pl.pallas_call(
        _moe_kernel,
        out_shape=jax.ShapeDtypeStruct((n_tiles * tm * rc, LANES), F32),
        grid_spec=pltpu.PrefetchScalarGridSpec(
            num_scalar_prefetch=2,
            grid=(n_tiles + 1,),
            in_specs=[pl.BlockSpec((tm,), lambda r, te, nu: (jnp.minimum(r, n_tiles - 1),),
                                   memory_space=pltpu.SMEM),
                      pl.BlockSpec(memory_space=pl.ANY),
                      pl.BlockSpec((1, d, f2), wsel),
                      pl.BlockSpec((1, 1, f2), wsel),
                      pl.BlockSpec((1, D_FF, d), wsel),
                      pl.BlockSpec((1, 1, d), wsel)],
            out_specs=pl.BlockSpec(
                (tm * rc, LANES), lambda r, te, nu: (jnp.clip(r - 1, 0, nu[0] - 1), 0)),
            scratch_shapes=[pltpu.VMEM((2, tm * rc, LANES), F32), pltpu.SemaphoreType.DMA((2,)),
                            pltpu.VMEM((d, f2), BF16), pltpu.VMEM((D_FF, d), BF16),
                            pltpu.VMEM((tm, f2), F32)]),
        compiler_params=pltpu.CompilerParams(dimension_semantics=("arbitrary",),
                                             vmem_limit_bytes=VMEM_LIMIT_BYTES),
        name="moe",
    )(tile_expert, n_used, row_start, h2_tiles, w_gate_up, b_gate_up, w_down, b_down)


def _combine_kernel(pos_ref, ys_ref, x1_ref, tw_ref, gt_ref, gf_ref, o_ref, gbuf, sem):
    tm = x1_ref.shape[0]
    rc = ROW_CHUNKS

    def issue(j, carry):
        for k in range(TOP_K):
            src = ys_ref.at[pl.ds(pl.multiple_of(pos_ref[j * TOP_K + k], rc), rc)]
            dst = gbuf.at[k, pl.ds(pl.multiple_of(j * rc, rc), rc)]
            pltpu.make_async_copy(src, dst, sem).start(priority=k % 2)
        return carry

    lax.fori_loop(0, tm, issue, 0)
    for k in range(TOP_K):
        pltpu.make_async_copy(ys_ref.at[pl.ds(0, tm * rc)], gbuf.at[k], sem).wait()

    tw = tw_ref[...]
    acc = tw[:, 0:1] * _load_token_tiles(gbuf.at[0])
    for k in range(1, TOP_K):
        acc = acc + tw[:, k:k + 1] * _load_token_tiles(gbuf.at[k])
    x2 = x1_ref[...] + gt_ref[0] * acc
    o_ref[...] = _rms(x2, gf_ref[...])


def _combine(ys, pos_flat, x1, tw, mod, g_final, seq):
    t, d = x1.shape
    tm = min(COMBINE_TM, t)
    return pl.pallas_call(
        _combine_kernel,
        out_shape=jax.ShapeDtypeStruct((t, d), F32),
        grid=(t // tm,),
        in_specs=[pl.BlockSpec((tm * TOP_K,), lambda s: (s,), memory_space=pltpu.SMEM),
                  pl.BlockSpec(memory_space=pl.ANY),
                  pl.BlockSpec((tm, d), lambda s: (s, 0)),
                  pl.BlockSpec((tm, LANES), lambda s: (s, 0)),
                  pl.BlockSpec((1, 1, d), lambda s: ((s * tm) // seq * 6 + 5, 0, 0)),
                  pl.BlockSpec((1, d), lambda s: (0, 0))],
        out_specs=pl.BlockSpec((tm, d), lambda s: (s, 0)),
        scratch_shapes=[pltpu.VMEM((TOP_K, tm * ROW_CHUNKS, LANES), F32),
                        pltpu.SemaphoreType.DMA],
        compiler_params=pltpu.CompilerParams(dimension_semantics=("arbitrary",),
                                             vmem_limit_bytes=VMEM_LIMIT_BYTES),
        name="combine",
    )(pos_flat, ys, x1, tw, mod, g_final)


def _routing_tables(top_idx, n_tiles):
    tm = MOE_TM
    e_flat = top_idx.reshape(-1)
    onehot = (e_flat[:, None] == jnp.arange(N_EXPERTS, dtype=I32)[None, :]).astype(I32)
    csum = jnp.cumsum(onehot, axis=0)
    counts = csum[-1]
    rank = jnp.sum(csum * onehot, axis=1) - 1
    tiles_per = (counts + tm - 1) // tm
    tile_end = jnp.cumsum(tiles_per)
    tile_start = tile_end - tiles_per
    pos = (jnp.sum(onehot * tile_start[None, :], axis=1) * tm + rank).astype(I32)
    n_used = tile_end[-1:].astype(I32)
    tile_ids = jnp.arange(n_tiles, dtype=I32)
    tile_expert = jnp.minimum(
        jnp.sum((tile_ids[:, None] >= tile_end[None, :]).astype(I32), axis=1),
        N_EXPERTS - 1).astype(I32)
    big = 1 << 17
    assert e_flat.shape[0] < big
    experts = jnp.arange(N_EXPERTS, dtype=I32)[:, None]
    slot = jnp.arange(tm, dtype=I32)[None, :]
    n_pad = (tiles_per * tm - counts)[:, None]
    key_pad = jnp.where(slot < n_pad, experts * big + counts[:, None] + slot,
                        jnp.iinfo(jnp.int32).max)
    keys = jnp.concatenate([e_flat * big + rank, key_pad.reshape(-1)]).astype(I32)
    toks = jnp.concatenate([jnp.arange(e_flat.shape[0], dtype=I32) // TOP_K,
                            jnp.zeros((N_EXPERTS * tm,), I32)])
    _, row_token = lax.sort((keys, toks), num_keys=1)
    return pos, tile_expert, n_used, row_token


def kernel(x, c, rel_bias, w_ada, b_ada, g_mix, w_in, lambda_q1, lambda_k1, lambda_q2,
           lambda_k2, subln_g, w_br_a, w_br_b, w_out, g_ffn, w_router, b_router,
           w_gate_up, b_gate_up, w_down, b_down, g_final):
    b, s, d = x.shape
    assert d == D_MODEL and s % ATT_T == 0 and MOBA_BLOCK == ATT_T
    t_tok = b * s

    c8 = jnp.zeros((8, d), F32).at[:b].set(c)
    ada = _ada(c8, w_ada[0], b_ada[0].reshape(1, -1))
    mod = ada[:b].reshape(b * 6, 1, d)

    dtiles = _bias_tiles(rel_bias)
    tab = rel_bias.T.reshape(-1)

    proj = _proj(x, g_mix[0].reshape(1, d), mod, w_in[0].astype(BF16))

    lamv = jnp.concatenate([lambda_q1, lambda_k1, lambda_q2, lambda_k2], axis=0)
    y_a = _attention("diff", tab, proj, dtiles, (lamv, subln_g[0].reshape(-1, 1)), (0, 512, 1024))
    y_b = _attention("moba", tab, proj, dtiles, None, (1536, 2048, 2560))

    wr = jnp.zeros((d, LANES), BF16).at[:, :N_EXPERTS].set(w_router[0].astype(BF16))
    br = jnp.zeros((1, LANES), F32).at[0, :N_EXPERTS].set(b_router[0])
    x1, h2, ti, tw = _merge(x, proj, y_a, y_b, w_br_a[0].astype(BF16), w_br_b[0].astype(BF16),
                            w_out[0].astype(BF16), mod, g_ffn[0].reshape(1, d), wr, br)

    n_tiles = (t_tok * TOP_K) // MOE_TM + N_EXPERTS
    top_idx = ti.reshape(t_tok, LANES)[:, :TOP_K]
    pos, tile_expert, n_used, row_token = _routing_tables(top_idx, n_tiles)

    ys = _moe(h2.reshape(t_tok * ROW_CHUNKS, LANES), row_token * ROW_CHUNKS, tile_expert, n_used,
              w_gate_up[0], b_gate_up[0].reshape(N_EXPERTS, 1, -1), w_down[0],
              b_down[0].reshape(N_EXPERTS, 1, -1))
    out = _combine(ys, pos * ROW_CHUNKS, x1.reshape(t_tok, d), tw.reshape(t_tok, LANES), mod,
                   g_final.reshape(1, d), s)
    return out.reshape(b, s, d)
```

```python
import functools
import math

import jax
import jax.numpy as jnp
import numpy as np
from jax import lax
from jax.experimental import pallas as pl
from jax.experimental.pallas import tpu as pltpu

F32 = jnp.float32
BF16 = jnp.bfloat16
I32 = jnp.int32

D_MODEL = 1024
N_DIFF_HEADS = 4
DIFF_HEAD_DIM = 64
N_MOBA_HEADS = 8
MOBA_HEAD_DIM = 64
MOBA_BLOCK = 256
MOBA_TOPK = 3
N_HEADS_TOTAL = N_DIFF_HEADS + N_MOBA_HEADS
N_BUCKETS = 32
MAX_DISTANCE = 128
N_EXPERTS = 32
TOP_K = 4
D_FF = D_MODEL
SWIGLU_LIMIT = 7.0
SWIGLU_ALPHA = 1.702
RMS_EPS = 1e-5
LAM_INIT = 0.8 - 0.6 * math.exp(-0.3 * 0)
IN_WIDTH = 5120

LANES = 128
VMEM_LIMIT_BYTES = 56 * 1024 * 1024

ATT_T = 256
N_PAIRS = 4
V_ROWS = LANES + 16
PROJ_TM = 512
MERGE_TM = 512
MOE_TM = 512
MOE_FF_CHUNK = 256
COMBINE_TM = 256
NEG = -1e30


def _t5_bucket_np(dist):
    n = np.maximum(dist, 0)
    max_exact = N_BUCKETS // 2
    nf = np.maximum(n, max_exact).astype(np.float32)
    large = max_exact + (np.log(nf / max_exact) / math.log(MAX_DISTANCE / max_exact)
                         * (N_BUCKETS - max_exact)).astype(np.int32)
    large = np.minimum(large, N_BUCKETS - 1)
    return np.where(n < max_exact, n, large).astype(np.int32)


def _rms(x, g):
    return x * lax.rsqrt(jnp.mean(x * x, axis=-1, keepdims=True) + RMS_EPS) * g


ROW_CHUNKS = D_MODEL // LANES


def _store_token_tiles(ref, x):
    n = x.shape[0]
    for c in range(ROW_CHUNKS):
        ref[pl.ds(c, n, stride=ROW_CHUNKS), :] = x[:, c * LANES:(c + 1) * LANES]


def _load_token_tiles(ref):
    n = ref.shape[0] // ROW_CHUNKS
    return jnp.concatenate([ref[pl.ds(c, n, stride=ROW_CHUNKS), :] for c in range(ROW_CHUNKS)],
                           axis=1)


def _ada_kernel(c_ref, w_ref, b_ref, o_ref):
    o_ref[...] = jnp.dot(c_ref[...].astype(BF16), w_ref[...].astype(BF16),
                         preferred_element_type=F32) + b_ref[...]


def _ada(c8, w, b):
    d, n = w.shape
    tn = 1536
    return pl.pallas_call(
        _ada_kernel,
        out_shape=jax.ShapeDtypeStruct((8, n), F32),
        grid=(n // tn,),
        in_specs=[pl.BlockSpec((8, d), lambda j: (0, 0)),
                  pl.BlockSpec((d, tn), lambda j: (0, j)),
                  pl.BlockSpec((1, tn), lambda j: (0, j))],
        out_specs=pl.BlockSpec((8, tn), lambda j: (0, j)),
        compiler_params=pltpu.CompilerParams(dimension_semantics=("arbitrary",),
                                             vmem_limit_bytes=VMEM_LIMIT_BYTES),
        name="ada",
    )(c8, w, b)


def _bias_kernel(tab_ref, bd_ref, bp_ref, o_ref):
    g = pl.program_id(0)
    p = pl.program_id(1)
    t = pl.program_id(2)
    head = jnp.where(g == 0, p, N_DIFF_HEADS + 2 * p + t)
    bd = bd_ref[...]
    bp = bp_ref[...]
    d0 = jnp.where(bd < 0, NEG, 0.0).astype(F32)
    d1 = jnp.zeros(bp.shape, F32)
    for b in range(N_BUCKETS):
        val = tab_ref[head * N_BUCKETS + b]
        d0 = jnp.where(bd == b, val, d0)
        d1 = jnp.where(bp == b, val, d1)
    o_ref[0, 0, 0] = d0
    o_ref[0, 0, 1] = d1


def _bias_tiles(rel_bias):
    t = ATT_T
    jj = np.arange(t)[:, None]
    ii = np.arange(t)[None, :]
    bd = np.where(ii >= jj, _t5_bucket_np(ii - jj), -1).astype(np.int32)
    bp = _t5_bucket_np(t + ii - jj)
    tab = rel_bias.T.reshape(-1)
    return pl.pallas_call(
        _bias_kernel,
        out_shape=jax.ShapeDtypeStruct((2, N_PAIRS, 2, t, 2 * t), F32),
        grid=(2, N_PAIRS, 2),
        in_specs=[pl.BlockSpec(memory_space=pltpu.SMEM),
                  pl.BlockSpec((t, t), lambda g, p, h: (0, 0)),
                  pl.BlockSpec((t, t), lambda g, p, h: (0, 0))],
        out_specs=pl.BlockSpec((1, 1, 2, t, t), lambda g, p, h: (g, p, 0, 0, h)),
        compiler_params=pltpu.CompilerParams(
            dimension_semantics=("arbitrary", "arbitrary", "arbitrary")),
        name="bias_tiles",
    )(tab, jnp.asarray(bd), jnp.asarray(bp))


def _proj_kernel(x_ref, g_ref, sh_ref, sc_ref, w_ref, o_ref):
    x = x_ref[0]
    h = _rms(x, g_ref[...]) * (1.0 + sc_ref[0]) + sh_ref[0]
    hb = h.astype(BF16)
    n_total = w_ref.shape[1]
    for n0 in range(0, n_total, 512):
        o_ref[0, :, n0:n0 + 512] = jnp.dot(
            hb, w_ref[:, n0:n0 + 512], preferred_element_type=F32).astype(BF16)


def _proj(x, g_mix, mod, w_in_bf):
    b, s, d = x.shape
    n = w_in_bf.shape[1]
    tm = min(PROJ_TM, s)
    return pl.pallas_call(
        _proj_kernel,
        out_shape=jax.ShapeDtypeStruct((b, s, n), BF16),
        grid=(b, s // tm),
        in_specs=[pl.BlockSpec((1, tm, d), lambda bi, i: (bi, i, 0)),
                  pl.BlockSpec((1, d), lambda bi, i: (0, 0)),
                  pl.BlockSpec((1, 1, d), lambda bi, i: (bi * 6 + 0, 0, 0)),
                  pl.BlockSpec((1, 1, d), lambda bi, i: (bi * 6 + 1, 0, 0)),
                  pl.BlockSpec((d, n), lambda bi, i: (0, 0))],
        out_specs=pl.BlockSpec((1, tm, n), lambda bi, i: (bi, i, 0)),
        compiler_params=pltpu.CompilerParams(dimension_semantics=("arbitrary", "arbitrary"),
                                             vmem_limit_bytes=VMEM_LIMIT_BYTES),
        name="proj",
    )(x, g_mix, mod, mod, w_in_bf)


def _stack_q(q_ref, q2_ref, scale):
    t = ATT_T
    row = lax.broadcasted_iota(I32, (LANES, t), 0)
    for p in range(N_PAIRS):
        q = q_ref[0, 0, p * LANES:(p + 1) * LANES, :].astype(F32) * scale
        q2_ref[p, :, 0:t] = jnp.where(row < 64, q, 0.0).astype(BF16)
        q2_ref[p, :, t:2 * t] = jnp.where(row >= 64, q, 0.0).astype(BF16)


def _flash_init(m_ref, acc_ref):
    m_ref[...] = jnp.full(m_ref.shape, NEG, F32)
    acc_ref[...] = jnp.zeros(acc_ref.shape, F32)


def _flash_scores(p, h, n, q2_ref, k_ref, s_ref):
    t = ATT_T
    kt = k_ref[0, pl.ds(pl.multiple_of(n * t, t), t), p * LANES:(p + 1) * LANES]
    s_ref[2 * p + h] = jnp.dot(kt, q2_ref[p, :, h * t:(h + 1) * t],
                               preferred_element_type=F32)


def _flash_absorb(p, h, n, tile_bias, row_bias, live, v_ref, s_ref, m_ref, acc_ref):
    t = ATT_T
    cs = slice(h * t, (h + 1) * t)
    vt = v_ref[0, n, p * V_ROWS:(p + 1) * V_ROWS, :]
    s = s_ref[2 * p + h]
    if tile_bias is not None:
        s = s + tile_bias
    mx = jnp.max(s, axis=0, keepdims=True)
    if row_bias is not None:
        mx = mx + row_bias
    if live is not None:
        mx = jnp.where(live, mx, NEG)
    m_prev = m_ref[p, :, cs]
    m_new = jnp.maximum(m_prev, mx)
    alpha = jnp.exp(m_prev - m_new)
    shift = m_new if row_bias is None else m_new - row_bias
    if live is not None:
        shift = jnp.where(live, shift, -NEG)
    pt = jnp.exp(s - shift)
    acc_ref[p, :, cs] = alpha * acc_ref[p, :, cs] + jnp.dot(vt, pt.astype(BF16),
                                                            preferred_element_type=F32)
    m_ref[p, :, cs] = m_new


def _far_bias(tab_ref, head):
    return tab_ref[head * N_BUCKETS + (N_BUCKETS - 1)]


def _flash_sweep(i, far_args, prev_args, own_args, refs):
    q2_ref, k_ref, v_ref, s_ref, m_ref, acc_ref = refs
    chains = [(p, h) for p in range(N_PAIRS) for h in range(2)]

    def tile(n, args):
        for p, h in chains:
            _flash_scores(p, h, n, q2_ref, k_ref, s_ref)
        for p, h in chains:
            _flash_absorb(p, h, n, *args(p, h), v_ref, s_ref, m_ref, acc_ref)

    def far_body(n, carry):
        tile(n, lambda p, h: far_args(p, h, n))
        return carry

    lax.fori_loop(0, jnp.maximum(i - 1, 0), far_body, 0)

    @pl.when(i >= 1)
    def _():
        tile(i - 1, prev_args)

    tile(i, own_args)


def _diff_kernel(tab_ref, q_ref, k_ref, v_ref, d_ref, lamv_ref, g_ref, o_ref,
                 q2_ref, s_ref, m_ref, acc_ref):
    t = ATT_T
    i = pl.program_id(1)
    _stack_q(q_ref, q2_ref, DIFF_HEAD_DIM ** -0.5)
    _flash_init(m_ref, acc_ref)
    refs = (q2_ref, k_ref, v_ref, s_ref, m_ref, acc_ref)
    _flash_sweep(i,
                 lambda p, h, n: (None, _far_bias(tab_ref, p), None),
                 lambda p, h: (d_ref[0, p, 1, :, h * t:(h + 1) * t], None, None),
                 lambda p, h: (d_ref[0, p, 0, :, h * t:(h + 1) * t], None, None),
                 refs)

    lv = lamv_ref[...]
    lam = (jnp.exp(jnp.sum(lv[0:1] * lv[1:2], axis=1, keepdims=True))
           - jnp.exp(jnp.sum(lv[2:3] * lv[3:4], axis=1, keepdims=True)) + LAM_INIT)
    for p in range(N_PAIRS):
        acc = acc_ref[p, 0:LANES, :]
        l = acc_ref[p, LANES:LANES + 1, :]
        o = acc[:, 0:t] / l[:, 0:t] - lam * (acc[:, t:2 * t] / l[:, t:2 * t])
        y = o * lax.rsqrt(jnp.mean(o * o, axis=0, keepdims=True) + RMS_EPS) * g_ref[...]
        o_ref[0, 0, p * LANES:(p + 1) * LANES, :] = (y * (1.0 - LAM_INIT)).astype(BF16)


def _moba_kernel(tab_ref, q_ref, k_ref, v_ref, d_ref, o_ref,
                 q2_ref, s_ref, m_ref, acc_ref, kmean_ref, sel_ref):
    t = ATT_T
    i = pl.program_id(1)
    nb = k_ref.shape[1] // t
    nbp = kmean_ref.shape[1]

    @pl.when(i == 0)
    def _():
        kmean_ref[...] = jnp.zeros(kmean_ref.shape, F32)
        for p in range(N_PAIRS):
            for n in range(nb):
                blk = k_ref[0, n * t:(n + 1) * t, p * LANES:(p + 1) * LANES].astype(F32)
                kmean_ref[p, n:n + 1, :] = jnp.sum(blk, axis=0, keepdims=True) * (1.0 / t)

    _stack_q(q_ref, q2_ref, MOBA_HEAD_DIM ** -0.5)
    _flash_init(m_ref, acc_ref)

    blk_id = lax.broadcasted_iota(I32, (nbp, 2 * t), 0)
    blk_f = blk_id.astype(F32)
    for p in range(N_PAIRS):
        gs = jnp.dot(kmean_ref[p].astype(BF16), q2_ref[p], preferred_element_type=F32)
        g = jnp.where(blk_id < i, gs, -jnp.inf)
        sel = jnp.full((nbp, 2 * t), NEG, F32)
        for _ in range(MOBA_TOPK):
            mx = jnp.max(g, axis=0, keepdims=True)
            first = jnp.min(jnp.where(g == mx, blk_f, float(nbp)), axis=0, keepdims=True)
            hit = blk_f == first
            sel = jnp.where(hit, 0.0, sel)
            g = jnp.where(hit, -jnp.inf, g)
        sel_ref[p] = sel

    def live(p, h, n):
        return sel_ref[p, pl.ds(n, 1), h * t:(h + 1) * t] > -1.0

    refs = (q2_ref, k_ref, v_ref, s_ref, m_ref, acc_ref)
    _flash_sweep(i,
                 lambda p, h, n: (None, _far_bias(tab_ref, N_DIFF_HEADS + 2 * p + h),
                                  live(p, h, n)),
                 lambda p, h: (d_ref[0, p, 1, :, h * t:(h + 1) * t], None, live(p, h, i - 1)),
                 lambda p, h: (d_ref[0, p, 0, :, h * t:(h + 1) * t], None, None),
                 refs)

    row = lax.broadcasted_iota(I32, (LANES, t), 0)
    for p in range(N_PAIRS):
        acc = acc_ref[p, 0:LANES, :]
        l = acc_ref[p, LANES:LANES + 1, :]
        o = jnp.where(row < 64, acc[:, 0:t] / l[:, 0:t], acc[:, t:2 * t] / l[:, t:2 * t])
        o_ref[0, 0, p * LANES:(p + 1) * LANES, :] = o.astype(BF16)


def _to_kv_major(proj, col0, nq):
    b, s, _ = proj.shape
    w = N_PAIRS * LANES
    return jnp.swapaxes(proj[:, :, col0:col0 + w].reshape(b, nq, ATT_T, w), 2, 3)


def _values_kv_major(proj, col0, nq):
    b, s, _ = proj.shape
    v = proj[:, :, col0:col0 + N_PAIRS * LANES].reshape(b, nq, ATT_T, N_PAIRS, LANES)
    ones = jnp.ones((b, nq, ATT_T, N_PAIRS, V_ROWS - LANES), BF16)
    v = jnp.concatenate([v, ones], axis=-1)
    return jnp.transpose(v, (0, 1, 3, 4, 2)).reshape(b, nq, N_PAIRS * V_ROWS, ATT_T)


def _attention(kind, tab, proj, dtiles, extra, cols):
    b, s, _ = proj.shape
    t = ATT_T
    nq = s // t
    w = N_PAIRS * LANES
    qc, kc, vc = cols
    gidx = 0 if kind == "diff" else 1
    q_t = _to_kv_major(proj, qc, nq)
    v_t = _values_kv_major(proj, vc, nq)
    in_specs = [pl.BlockSpec(memory_space=pltpu.SMEM),
                pl.BlockSpec((1, 1, w, t), lambda bi, i: (bi, i, 0, 0)),
                pl.BlockSpec((1, s, w), lambda bi, i: (bi, 0, kc // w)),
                pl.BlockSpec((1, nq, N_PAIRS * V_ROWS, t), lambda bi, i: (bi, 0, 0, 0)),
                pl.BlockSpec((1, N_PAIRS, 2, t, 2 * t), lambda bi, i: (gidx, 0, 0, 0, 0))]
    scratch = [pltpu.VMEM((N_PAIRS, LANES, 2 * t), BF16),
               pltpu.VMEM((2 * N_PAIRS, t, t), F32),
               pltpu.VMEM((N_PAIRS, 1, 2 * t), F32),
               pltpu.VMEM((N_PAIRS, V_ROWS, 2 * t), F32)]
    if kind == "diff":
        lamv, subln_g = extra
        in_specs += [pl.BlockSpec(lamv.shape, lambda bi, i: (0, 0)),
                     pl.BlockSpec(subln_g.shape, lambda bi, i: (0, 0))]
        args = (tab, q_t, proj, v_t, dtiles, lamv, subln_g)
        body = _diff_kernel
    else:
        nbp = max(32, nq)
        scratch += [pltpu.VMEM((N_PAIRS, nbp, LANES), F32), pltpu.VMEM((N_PAIRS, nbp, 2 * t), F32)]
        args = (tab, q_t, proj, v_t, dtiles)
        body = _moba_kernel
    y_t = pl.pallas_call(
        body,
        out_shape=jax.ShapeDtypeStruct((b, nq, w, t), BF16),
        grid=(b, nq),
        in_specs=in_specs,
        out_specs=pl.BlockSpec((1, 1, w, t), lambda bi, i: (bi, i, 0, 0)),
        scratch_shapes=scratch,
        compiler_params=pltpu.CompilerParams(
            dimension_semantics=("arbitrary", "arbitrary"),
            vmem_limit_bytes=VMEM_LIMIT_BYTES),
        name=kind + "_attn",
    )(*args)
    return jnp.swapaxes(y_t, 2, 3).reshape(b, s, w)


def _merge_kernel(x_ref, ya_ref, yb_ref, gla_ref, glb_ref, wa_ref, wb_ref, wo_ref,
                  gt_ref, shf_ref, scf_ref, gf_ref, wr_ref, br_ref,
                  x1_ref, h2_ref, ti_ref, tw_ref):
    a = jnp.dot(ya_ref[0], wa_ref[...], preferred_element_type=F32)
    b = jnp.dot(yb_ref[0], wb_ref[...], preferred_element_type=F32)
    merged = (jax.nn.sigmoid(gla_ref[0].astype(F32)) * a
              + jax.nn.sigmoid(glb_ref[0].astype(F32)) * b)
    mo = jnp.dot(merged.astype(BF16), wo_ref[...], preferred_element_type=F32)
    x1 = x_ref[0] + gt_ref[0] * mo
    x1_ref[0] = x1
    h = _rms(x1, gf_ref[...]) * (1.0 + scf_ref[0]) + shf_ref[0]
    _store_token_tiles(h2_ref.at[0], h)
    logits = jnp.dot(h.astype(BF16), wr_ref[...], preferred_element_type=F32) + br_ref[...]
    tm = logits.shape[0]
    lane = lax.broadcasted_iota(I32, (tm, LANES), 1)
    lane_f = lane.astype(F32)
    g = jnp.where(lane < N_EXPERTS, logits, -jnp.inf)
    vals, idxs = [], []
    for _ in range(TOP_K):
        mx = jnp.max(g, axis=1, keepdims=True)
        first = jnp.min(jnp.where(g == mx, lane_f, float(LANES)), axis=1, keepdims=True)
        vals.append(mx)
        idxs.append(first)
        g = jnp.where(lane_f == first, -jnp.inf, g)
    es = [jnp.exp(v - vals[0]) for v in vals]
    den = es[0] + es[1] + es[2] + es[3]
    ti = jnp.zeros((tm, LANES), F32)
    tw = jnp.zeros((tm, LANES), F32)
    for k in range(TOP_K):
        ti = jnp.where(lane == k, idxs[k], ti)
        tw = jnp.where(lane == k, es[k] / den, tw)
    ti_ref[0] = ti.astype(I32)
    tw_ref[0] = tw


def _merge(x, proj, y_a, y_b, w_br_a, w_br_b, w_out, mod, g_ffn, w_router, b_router):
    b, s, d = x.shape
    tm = min(MERGE_TM, s)
    full = lambda shape: pl.BlockSpec(shape, lambda bi, i: (0,) * len(shape))
    tile = lambda w, c=0: pl.BlockSpec((1, tm, w), lambda bi, i: (bi, i, c))
    modspec = lambda j: pl.BlockSpec((1, 1, d), lambda bi, i: (bi * 6 + j, 0, 0))
    return pl.pallas_call(
        _merge_kernel,
        out_shape=(jax.ShapeDtypeStruct((b, s, d), F32),
                   jax.ShapeDtypeStruct((b, s * ROW_CHUNKS, LANES), F32),
                   jax.ShapeDtypeStruct((b, s, LANES), I32),
                   jax.ShapeDtypeStruct((b, s, LANES), F32)),
        grid=(b, s // tm),
        in_specs=[tile(d), tile(512), tile(512), tile(d, 3), tile(d, 4),
                  full(w_br_a.shape), full(w_br_b.shape), full(w_out.shape),
                  modspec(2), modspec(3), modspec(4), full(g_ffn.shape),
                  full(w_router.shape), full(b_router.shape)],
        out_specs=(tile(d), pl.BlockSpec((1, tm * ROW_CHUNKS, LANES), lambda bi, i: (bi, i, 0)),
                   tile(LANES), tile(LANES)),
        compiler_params=pltpu.CompilerParams(dimension_semantics=("arbitrary", "arbitrary"),
                                             vmem_limit_bytes=VMEM_LIMIT_BYTES),
        name="merge",
    )(x, y_a, y_b, proj, proj, w_br_a, w_br_b, w_out, mod, mod, mod, g_ffn,
      w_router, b_router)


def _moe_kernel(te_ref, nu_ref, tok_ref, h2_ref, wgu_ref, bgu_ref, wd_ref, bd_ref,
                ys_ref, xbuf, sems, wgu_bf, wd_bf, gu_ref):
    r = pl.program_id(0)
    n_used = nu_ref[0]
    rc = ROW_CHUNKS
    tm = xbuf.shape[1] // rc
    cur = r % 2
    prv = 1 - cur

    def start_row(j, priority):
        src = h2_ref.at[pl.ds(pl.multiple_of(tok_ref[j], rc), rc)]
        dst = xbuf.at[cur, pl.ds(pl.multiple_of(j * rc, rc), rc)]
        pltpu.make_async_copy(src, dst, sems.at[cur]).start(priority=priority)

    def wait_tile(s):
        pltpu.make_async_copy(h2_ref.at[pl.ds(0, tm * rc)], xbuf.at[s], sems.at[s]).wait()

    @pl.when(r == 0)
    def _():
        def body(j, carry):
            start_row(j, 0)
            return carry
        lax.fori_loop(0, tm, body, 0)

    busy = jnp.logical_and(r >= 1, r <= n_used)
    e_prev = te_ref[jnp.maximum(r - 1, 0)]
    fresh = jnp.logical_or(r == 1, e_prev != te_ref[jnp.maximum(r - 2, 0)])

    @pl.when(jnp.logical_and(busy, fresh))
    def _():
        wgu_bf[...] = wgu_ref[0].astype(BF16)
        wd_bf[...] = wd_ref[0].astype(BF16)

    @pl.when(busy)
    def _():
        wait_tile(prv)
        x = _load_token_tiles(xbuf.at[prv]).astype(BF16)
        fc = MOE_FF_CHUNK
        n_gu = 2 * D_FF // fc
        rows_per = tm // n_gu
        for c in range(n_gu):
            lo, hi = c * fc, (c + 1) * fc
            gu_ref[:, lo:hi] = (jnp.dot(x, wgu_bf[:, lo:hi], preferred_element_type=F32)
                                + bgu_ref[0][:, lo:hi])
            for j in range(c * rows_per, (c + 1) * rows_per):
                start_row(j, j % 2)
        y = None
        for c in range(D_FF // fc):
            lo, hi = c * fc, (c + 1) * fc
            gate = jnp.minimum(gu_ref[:, lo:hi], SWIGLU_LIMIT)
            up = jnp.clip(gu_ref[:, D_FF + lo:D_FF + hi], -SWIGLU_LIMIT, SWIGLU_LIMIT)
            act = (up + 1.0) * gate * jax.nn.sigmoid(SWIGLU_ALPHA * gate)
            part = jnp.dot(act.astype(BF16), wd_bf[lo:hi, :], preferred_element_type=F32)
            y = part if y is None else y + part
        _store_token_tiles(ys_ref, y + bd_ref[0])

        @pl.when(r == n_used)
        def _():
            wait_tile(cur)


def _moe(h2_tiles, row_start, tile_expert, n_used, w_gate_up, b_gate_up, w_down, b_down):
    d = D_MODEL
    rc = ROW_CHUNKS
    tm = MOE_TM
    n_tiles = tile_expert.shape[0]
    e, _, f2 = w_gate_up.shape
    wsel = lambda r, te, nu: (te[jnp.clip(r - 1, 0, n_tiles - 1)], 0, 0)
    return pl.pallas_call(
        _moe_kernel,
        out_shape=jax.ShapeDtypeStruct((n_tiles * tm * rc, LANES), F32),
        grid_spec=pltpu.PrefetchScalarGridSpec(
            num_scalar_prefetch=2,
            grid=(n_tiles + 1,),
            in_specs=[pl.BlockSpec((tm,), lambda r, te, nu: (jnp.minimum(r, n_tiles - 1),),
                                   memory_space=pltpu.SMEM),
                      pl.BlockSpec(memory_space=pl.ANY),
                      pl.BlockSpec((1, d, f2), wsel),
                      pl.BlockSpec((1, 1, f2), wsel),
                      pl.BlockSpec((1, D_FF, d), wsel),
                      pl.BlockSpec((1, 1, d), wsel)],
            out_specs=pl.BlockSpec(
                (tm * rc, LANES), lambda r, te, nu: (jnp.clip(r - 1, 0, nu[0] - 1), 0)),
            scratch_shapes=[pltpu.VMEM((2, tm * rc, LANES), F32), pltpu.SemaphoreType.DMA((2,)),
                            pltpu.VMEM((d, f2), BF16), pltpu.VMEM((D_FF, d), BF16),
                            pltpu.VMEM((tm, f2), F32)]),
        compiler_params=pltpu.CompilerParams(dimension_semantics=("arbitrary",),
                                             vmem_limit_bytes=VMEM_LIMIT_BYTES),
        name="moe",
    )(tile_expert, n_used, row_start, h2_tiles, w_gate_up, b_gate_up, w_down, b_down)


def _combine_kernel(pos_ref, ys_ref, x1_ref, tw_ref, gt_ref, gf_ref, o_ref, gbuf, sem):
    tm = x1_ref.shape[0]
    rc = ROW_CHUNKS

    def issue(j, carry):
        for k in range(TOP_K):
            src = ys_ref.at[pl.ds(pl.multiple_of(pos_ref[j * TOP_K + k], rc), rc)]
            dst = gbuf.at[k, pl.ds(pl.multiple_of(j * rc, rc), rc)]
            pltpu.make_async_copy(src, dst, sem).start(priority=k % 2)
        return carry

    lax.fori_loop(0, tm, issue, 0)
    for k in range(TOP_K):
        pltpu.make_async_copy(ys_ref.at[pl.ds(0, tm * rc)], gbuf.at[k], sem).wait()

    tw = tw_ref[...]
    acc = tw[:, 0:1] * _load_token_tiles(gbuf.at[0])
    for k in range(1, TOP_K):
        acc = acc + tw[:, k:k + 1] * _load_token_tiles(gbuf.at[k])
    x2 = x1_ref[...] + gt_ref[0] * acc
    o_ref[...] = _rms(x2, gf_ref[...])


def _combine(ys, pos_flat, x1, tw, mod, g_final, seq):
    t, d = x1.shape
    tm = min(COMBINE_TM, t)
    return pl.pallas_call(
        _combine_kernel,
        out_shape=jax.ShapeDtypeStruct((t, d), F32),
        grid=(t // tm,),
        in_specs=[pl.BlockSpec((tm * TOP_K,), lambda s: (s,), memory_space=pltpu.SMEM),
                  pl.BlockSpec(memory_space=pl.ANY),
                  pl.BlockSpec((tm, d), lambda s: (s, 0)),
                  pl.BlockSpec((tm, LANES), lambda s: (s, 0)),
                  pl.BlockSpec((1, 1, d), lambda s: ((s * tm) // seq * 6 + 5, 0, 0)),
                  pl.BlockSpec((1, d), lambda s: (0, 0))],
        out_specs=pl.BlockSpec((tm, d), lambda s: (s, 0)),
        scratch_shapes=[pltpu.VMEM((TOP_K, tm * ROW_CHUNKS, LANES), F32),
                        pltpu.SemaphoreType.DMA],
        compiler_params=pltpu.CompilerParams(dimension_semantics=("arbitrary",),
                                             vmem_limit_bytes=VMEM_LIMIT_BYTES),
        name="combine",
    )(pos_flat, ys, x1, tw, mod, g_final)


def _routing_tables(top_idx, n_tiles):
    tm = MOE_TM
    e_flat = top_idx.reshape(-1)
    onehot = (e_flat[:, None] == jnp.arange(N_EXPERTS, dtype=I32)[None, :]).astype(I32)
    csum = jnp.cumsum(onehot, axis=0)
    counts = csum[-1]
    rank = jnp.sum(csum * onehot, axis=1) - 1
    tiles_per = (counts + tm - 1) // tm
    tile_end = jnp.cumsum(tiles_per)
    tile_start = tile_end - tiles_per
    pos = (jnp.sum(onehot * tile_start[None, :], axis=1) * tm + rank).astype(I32)
    n_used = tile_end[-1:].astype(I32)
    tile_ids = jnp.arange(n_tiles, dtype=I32)
    tile_expert = jnp.minimum(
        jnp.sum((tile_ids[:, None] >= tile_end[None, :]).astype(I32), axis=1),
        N_EXPERTS - 1).astype(I32)
    big = 1 << 17
    assert e_flat.shape[0] < big
    experts = jnp.arange(N_EXPERTS, dtype=I32)[:, None]
    slot = jnp.arange(tm, dtype=I32)[None, :]
    n_pad = (tiles_per * tm - counts)[:, None]
    key_pad = jnp.where(slot < n_pad, experts * big + counts[:, None] + slot,
                        jnp.iinfo(jnp.int32).max)
    keys = jnp.concatenate([e_flat * big + rank, key_pad.reshape(-1)]).astype(I32)
    toks = jnp.concatenate([jnp.arange(e_flat.shape[0], dtype=I32) // TOP_K,
                            jnp.zeros((N_EXPERTS * tm,), I32)])
    _, row_token = lax.sort((keys, toks), num_keys=1)
    return pos, tile_expert, n_used, row_token


def kernel(x, c, rel_bias, w_ada, b_ada, g_mix, w_in, lambda_q1, lambda_k1, lambda_q2,
           lambda_k2, subln_g, w_br_a, w_br_b, w_out, g_ffn, w_router, b_router,
           w_gate_up, b_gate_up, w_down, b_down, g_final):
    b, s, d = x.shape
    assert d == D_MODEL and s % ATT_T == 0 and MOBA_BLOCK == ATT_T
    t_tok = b * s

    c8 = jnp.zeros((8, d), F32).at[:b].set(c)
    ada = _ada(c8, w_ada[0], b_ada[0].reshape(1, -1))
    mod = ada[:b].reshape(b * 6, 1, d)

    dtiles = _bias_tiles(rel_bias)
    tab = rel_bias.T.reshape(-1)

    proj = _proj(x, g_mix[0].reshape(1, d), mod, w_in[0].astype(BF16))

    lamv = jnp.concatenate([lambda_q1, lambda_k1, lambda_q2, lambda_k2], axis=0)
    y_a = _attention("diff", tab, proj, dtiles, (lamv, subln_g[0].reshape(-1, 1)), (0, 512, 1024))
    y_b = _attention("moba", tab, proj, dtiles, None, (1536, 2048, 2560))

    wr = jnp.zeros((d, LANES), BF16).at[:, :N_EXPERTS].set(w_router[0].astype(BF16))
    br = jnp.zeros((1, LANES), F32).at[0, :N_EXPERTS].set(b_router[0])
    x1, h2, ti, tw = _merge(x, proj, y_a, y_b, w_br_a[0].astype(BF16), w_br_b[0].astype(BF16),
                            w_out[0].astype(BF16), mod, g_ffn[0].reshape(1, d), wr, br)

    n_tiles = (t_tok * TOP_K) // MOE_TM + N_EXPERTS
    top_idx = ti.reshape(t_tok, LANES)[:, :TOP_K]
    pos, tile_expert, n_used, row_token = _routing_tables(top_idx, n_tiles)

    ys = _moe(h2.reshape(t_tok * ROW_CHUNKS, LANES), row_token * ROW_CHUNKS, tile_expert, n_used,
              w_gate_up[0], b_gate_up[0].reshape(N_EXPERTS, 1, -1), w_down[0],
              b_down[0].reshape(N_EXPERTS, 1, -1))
    out = _combine(ys, pos * ROW_CHUNKS, x1.reshape(t_tok, d), tw.reshape(t_tok, LANES), mod,
                   g_final.reshape(1, d), s)
    return out.reshape(b, s, d)
```

```python
import functools
import math

import jax
import jax.numpy as jnp
import numpy as np
from jax import lax
from jax.experimental import pallas as pl
from jax.experimental.pallas import tpu as pltpu

F32 = jnp.float32
BF16 = jnp.bfloat16
I32 = jnp.int32

D_MODEL = 1024
N_DIFF_HEADS = 4
DIFF_HEAD_DIM = 64
N_MOBA_HEADS = 8
MOBA_HEAD_DIM = 64
MOBA_BLOCK = 256
MOBA_TOPK = 3
N_HEADS_TOTAL = N_DIFF_HEADS + N_MOBA_HEADS
N_BUCKETS = 32
MAX_DISTANCE = 128
N_EXPERTS = 32
TOP_K = 4
D_FF = D_MODEL
SWIGLU_LIMIT = 7.0
SWIGLU_ALPHA = 1.702
RMS_EPS = 1e-5
LAM_INIT = 0.8 - 0.6 * math.exp(-0.3 * 0)
IN_WIDTH = 5120

LANES = 128
VMEM_LIMIT_BYTES = 56 * 1024 * 1024

ATT_T = 256
N_PAIRS = 4
V_ROWS = LANES + 16
PROJ_TM = 512
MERGE_TM = 512
MOE_TM = 256
MOE_FF_CHUNK = 256
COMBINE_TM = 256
NEG = -1e30


def _t5_bucket_np(dist):
    n = np.maximum(dist, 0)
    max_exact = N_BUCKETS // 2
    nf = np.maximum(n, max_exact).astype(np.float32)
    large = max_exact + (np.log(nf / max_exact) / math.log(MAX_DISTANCE / max_exact)
                         * (N_BUCKETS - max_exact)).astype(np.int32)
    large = np.minimum(large, N_BUCKETS - 1)
    return np.where(n < max_exact, n, large).astype(np.int32)


def _rms(x, g):
    return x * lax.rsqrt(jnp.mean(x * x, axis=-1, keepdims=True) + RMS_EPS) * g


ROW_CHUNKS = D_MODEL // LANES


def _store_token_tiles(ref, x):
    n = x.shape[0]
    for c in range(ROW_CHUNKS):
        ref[pl.ds(c, n, stride=ROW_CHUNKS), :] = x[:, c * LANES:(c + 1) * LANES]


def _load_token_tiles(ref):
    n = ref.shape[0] // ROW_CHUNKS
    return jnp.concatenate([ref[pl.ds(c, n, stride=ROW_CHUNKS), :] for c in range(ROW_CHUNKS)],
                           axis=1)


def _ada_kernel(c_ref, w_ref, b_ref, o_ref):
    o_ref[...] = jnp.dot(c_ref[...].astype(BF16), w_ref[...].astype(BF16),
                         preferred_element_type=F32) + b_ref[...]


def _ada(c8, w, b):
    d, n = w.shape
    tn = 1536
    return pl.pallas_call(
        _ada_kernel,
        out_shape=jax.ShapeDtypeStruct((8, n), F32),
        grid=(n // tn,),
        in_specs=[pl.BlockSpec((8, d), lambda j: (0, 0)),
                  pl.BlockSpec((d, tn), lambda j: (0, j)),
                  pl.BlockSpec((1, tn), lambda j: (0, j))],
        out_specs=pl.BlockSpec((8, tn), lambda j: (0, j)),
        compiler_params=pltpu.CompilerParams(dimension_semantics=("arbitrary",),
                                             vmem_limit_bytes=VMEM_LIMIT_BYTES),
        name="ada",
    )(c8, w, b)


def _bias_kernel(tab_ref, bd_ref, bp_ref, o_ref):
    g = pl.program_id(0)
    p = pl.program_id(1)
    t = pl.program_id(2)
    head = jnp.where(g == 0, p, N_DIFF_HEADS + 2 * p + t)
    bd = bd_ref[...]
    bp = bp_ref[...]
    d0 = jnp.where(bd < 0, NEG, 0.0).astype(F32)
    d1 = jnp.zeros(bp.shape, F32)
    for b in range(N_BUCKETS):
        val = tab_ref[head * N_BUCKETS + b]
        d0 = jnp.where(bd == b, val, d0)
        d1 = jnp.where(bp == b, val, d1)
    o_ref[0, 0, 0] = d0
    o_ref[0, 0, 1] = d1


def _bias_tiles(rel_bias):
    t = ATT_T
    jj = np.arange(t)[:, None]
    ii = np.arange(t)[None, :]
    bd = np.where(ii >= jj, _t5_bucket_np(ii - jj), -1).astype(np.int32)
    bp = _t5_bucket_np(t + ii - jj)
    tab = rel_bias.T.reshape(-1)
    return pl.pallas_call(
        _bias_kernel,
        out_shape=jax.ShapeDtypeStruct((2, N_PAIRS, 2, t, 2 * t), F32),
        grid=(2, N_PAIRS, 2),
        in_specs=[pl.BlockSpec(memory_space=pltpu.SMEM),
                  pl.BlockSpec((t, t), lambda g, p, h: (0, 0)),
                  pl.BlockSpec((t, t), lambda g, p, h: (0, 0))],
        out_specs=pl.BlockSpec((1, 1, 2, t, t), lambda g, p, h: (g, p, 0, 0, h)),
        compiler_params=pltpu.CompilerParams(
            dimension_semantics=("arbitrary", "arbitrary", "arbitrary")),
        name="bias_tiles",
    )(tab, jnp.asarray(bd), jnp.asarray(bp))


def _proj_kernel(x_ref, g_ref, sh_ref, sc_ref, w_ref, o_ref):
    x = x_ref[0]
    h = _rms(x, g_ref[...]) * (1.0 + sc_ref[0]) + sh_ref[0]
    hb = h.astype(BF16)
    n_total = w_ref.shape[1]
    for n0 in range(0, n_total, 512):
        o_ref[0, :, n0:n0 + 512] = jnp.dot(
            hb, w_ref[:, n0:n0 + 512], preferred_element_type=F32).astype(BF16)


def _proj(x, g_mix, mod, w_in_bf):
    b, s, d = x.shape
    n = w_in_bf.shape[1]
    tm = min(PROJ_TM, s)
    return pl.pallas_call(
        _proj_kernel,
        out_shape=jax.ShapeDtypeStruct((b, s, n), BF16),
        grid=(b, s // tm),
        in_specs=[pl.BlockSpec((1, tm, d), lambda bi, i: (bi, i, 0)),
                  pl.BlockSpec((1, d), lambda bi, i: (0, 0)),
                  pl.BlockSpec((1, 1, d), lambda bi, i: (bi * 6 + 0, 0, 0)),
                  pl.BlockSpec((1, 1, d), lambda bi, i: (bi * 6 + 1, 0, 0)),
                  pl.BlockSpec((d, n), lambda bi, i: (0, 0))],
        out_specs=pl.BlockSpec((1, tm, n), lambda bi, i: (bi, i, 0)),
        compiler_params=pltpu.CompilerParams(dimension_semantics=("arbitrary", "arbitrary"),
                                             vmem_limit_bytes=VMEM_LIMIT_BYTES),
        name="proj",
    )(x, g_mix, mod, mod, w_in_bf)


def _stack_q(q_ref, q2_ref, scale):
    t = ATT_T
    row = lax.broadcasted_iota(I32, (LANES, t), 0)
    for p in range(N_PAIRS):
        q = q_ref[0, 0, p * LANES:(p + 1) * LANES, :].astype(F32) * scale
        q2_ref[p, :, 0:t] = jnp.where(row < 64, q, 0.0).astype(BF16)
        q2_ref[p, :, t:2 * t] = jnp.where(row >= 64, q, 0.0).astype(BF16)


def _flash_init(m_ref, acc_ref):
    m_ref[...] = jnp.full(m_ref.shape, NEG, F32)
    acc_ref[...] = jnp.zeros(acc_ref.shape, F32)


def _flash_scores(p, h, n, q2_ref, k_ref, s_ref):
    t = ATT_T
    kt = k_ref[0, pl.ds(pl.multiple_of(n * t, t), t), p * LANES:(p + 1) * LANES]
    s_ref[2 * p + h] = jnp.dot(kt, q2_ref[p, :, h * t:(h + 1) * t],
                               preferred_element_type=F32)


def _flash_absorb(p, h, n, tile_bias, row_bias, live, v_ref, s_ref, m_ref, acc_ref):
    t = ATT_T
    cs = slice(h * t, (h + 1) * t)
    vt = v_ref[0, n, p * V_ROWS:(p + 1) * V_ROWS, :]
    s = s_ref[2 * p + h]
    if tile_bias is not None:
        s = s + tile_bias
    mx = jnp.max(s, axis=0, keepdims=True)
    if row_bias is not None:
        mx = mx + row_bias
    if live is not None:
        mx = jnp.where(live, mx, NEG)
    m_prev = m_ref[p, :, cs]
    m_new = jnp.maximum(m_prev, mx)
    alpha = jnp.exp(m_prev - m_new)
    shift = m_new if row_bias is None else m_new - row_bias
    if live is not None:
        shift = jnp.where(live, shift, -NEG)
    pt = jnp.exp(s - shift)
    acc_ref[p, :, cs] = alpha * acc_ref[p, :, cs] + jnp.dot(vt, pt.astype(BF16),
                                                            preferred_element_type=F32)
    m_ref[p, :, cs] = m_new


def _far_bias(tab_ref, head):
    return tab_ref[head * N_BUCKETS + (N_BUCKETS - 1)]


def _flash_sweep(i, far_args, prev_args, own_args, refs):
    q2_ref, k_ref, v_ref, s_ref, m_ref, acc_ref = refs
    chains = [(p, h) for p in range(N_PAIRS) for h in range(2)]

    def tile(n, args):
        for p, h in chains:
            _flash_scores(p, h, n, q2_ref, k_ref, s_ref)
        for p, h in chains:
            _flash_absorb(p, h, n, *args(p, h), v_ref, s_ref, m_ref, acc_ref)

    def far_body(n, carry):
        tile(n, lambda p, h: far_args(p, h, n))
        return carry

    lax.fori_loop(0, jnp.maximum(i - 1, 0), far_body, 0)

    @pl.when(i >= 1)
    def _():
        tile(i - 1, prev_args)

    tile(i, own_args)


def _diff_kernel(tab_ref, q_ref, k_ref, v_ref, d_ref, lamv_ref, g_ref, o_ref,
                 q2_ref, s_ref, m_ref, acc_ref):
    t = ATT_T
    i = pl.program_id(1)
    _stack_q(q_ref, q2_ref, DIFF_HEAD_DIM ** -0.5)
    _flash_init(m_ref, acc_ref)
    refs = (q2_ref, k_ref, v_ref, s_ref, m_ref, acc_ref)
    _flash_sweep(i,
                 lambda p, h, n: (None, _far_bias(tab_ref, p), None),
                 lambda p, h: (d_ref[0, p, 1, :, h * t:(h + 1) * t], None, None),
                 lambda p, h: (d_ref[0, p, 0, :, h * t:(h + 1) * t], None, None),
                 refs)

    lv = lamv_ref[...]
    lam = (jnp.exp(jnp.sum(lv[0:1] * lv[1:2], axis=1, keepdims=True))
           - jnp.exp(jnp.sum(lv[2:3] * lv[3:4], axis=1, keepdims=True)) + LAM_INIT)
    for p in range(N_PAIRS):
        acc = acc_ref[p, 0:LANES, :]
        l = acc_ref[p, LANES:LANES + 1, :]
        o = acc[:, 0:t] / l[:, 0:t] - lam * (acc[:, t:2 * t] / l[:, t:2 * t])
        y = o * lax.rsqrt(jnp.mean(o * o, axis=0, keepdims=True) + RMS_EPS) * g_ref[...]
        o_ref[0, 0, p * LANES:(p + 1) * LANES, :] = (y * (1.0 - LAM_INIT)).astype(BF16)


def _moba_kernel(tab_ref, q_ref, k_ref, v_ref, d_ref, o_ref,
                 q2_ref, s_ref, m_ref, acc_ref, kmean_ref, sel_ref):
    t = ATT_T
    i = pl.program_id(1)
    nb = k_ref.shape[1] // t
    nbp = kmean_ref.shape[1]

    @pl.when(i == 0)
    def _():
        kmean_ref[...] = jnp.zeros(kmean_ref.shape, F32)
        for p in range(N_PAIRS):
            for n in range(nb):
                blk = k_ref[0, n * t:(n + 1) * t, p * LANES:(p + 1) * LANES].astype(F32)
                kmean_ref[p, n:n + 1, :] = jnp.sum(blk, axis=0, keepdims=True) * (1.0 / t)

    _stack_q(q_ref, q2_ref, MOBA_HEAD_DIM ** -0.5)
    _flash_init(m_ref, acc_ref)

    blk_id = lax.broadcasted_iota(I32, (nbp, 2 * t), 0)
    blk_f = blk_id.astype(F32)
    for p in range(N_PAIRS):
        gs = jnp.dot(kmean_ref[p].astype(BF16), q2_ref[p], preferred_element_type=F32)
        g = jnp.where(blk_id < i, gs, -jnp.inf)
        sel = jnp.full((nbp, 2 * t), NEG, F32)
        for _ in range(MOBA_TOPK):
            mx = jnp.max(g, axis=0, keepdims=True)
            first = jnp.min(jnp.where(g == mx, blk_f, float(nbp)), axis=0, keepdims=True)
            hit = blk_f == first
            sel = jnp.where(hit, 0.0, sel)
            g = jnp.where(hit, -jnp.inf, g)
        sel_ref[p] = sel

    def live(p, h, n):
        return sel_ref[p, pl.ds(n, 1), h * t:(h + 1) * t] > -1.0

    refs = (q2_ref, k_ref, v_ref, s_ref, m_ref, acc_ref)
    _flash_sweep(i,
                 lambda p, h, n: (None, _far_bias(tab_ref, N_DIFF_HEADS + 2 * p + h),
                                  live(p, h, n)),
                 lambda p, h: (d_ref[0, p, 1, :, h * t:(h + 1) * t], None, live(p, h, i - 1)),
                 lambda p, h: (d_ref[0, p, 0, :, h * t:(h + 1) * t], None, None),
                 refs)

    row = lax.broadcasted_iota(I32, (LANES, t), 0)
    for p in range(N_PAIRS):
        acc = acc_ref[p, 0:LANES, :]
        l = acc_ref[p, LANES:LANES + 1, :]
        o = jnp.where(row < 64, acc[:, 0:t] / l[:, 0:t], acc[:, t:2 * t] / l[:, t:2 * t])
        o_ref[0, 0, p * LANES:(p + 1) * LANES, :] = o.astype(BF16)


def _to_kv_major(proj, col0, nq):
    b, s, _ = proj.shape
    w = N_PAIRS * LANES
    return jnp.swapaxes(proj[:, :, col0:col0 + w].reshape(b, nq, ATT_T, w), 2, 3)


def _values_kv_major(proj, col0, nq):
    b, s, _ = proj.shape
    v = proj[:, :, col0:col0 + N_PAIRS * LANES].reshape(b, nq, ATT_T, N_PAIRS, LANES)
    ones = jnp.ones((b, nq, ATT_T, N_PAIRS, V_ROWS - LANES), BF16)
    v = jnp.concatenate([v, ones], axis=-1)
    return jnp.transpose(v, (0, 1, 3, 4, 2)).reshape(b, nq, N_PAIRS * V_ROWS, ATT_T)


def _attention(kind, tab, proj, dtiles, extra, cols):
    b, s, _ = proj.shape
    t = ATT_T
    nq = s // t
    w = N_PAIRS * LANES
    qc, kc, vc = cols
    gidx = 0 if kind == "diff" else 1
    q_t = _to_kv_major(proj, qc, nq)
    v_t = _values_kv_major(proj, vc, nq)
    in_specs = [pl.BlockSpec(memory_space=pltpu.SMEM),
                pl.BlockSpec((1, 1, w, t), lambda bi, i: (bi, i, 0, 0)),
                pl.BlockSpec((1, s, w), lambda bi, i: (bi, 0, kc // w)),
                pl.BlockSpec((1, nq, N_PAIRS * V_ROWS, t), lambda bi, i: (bi, 0, 0, 0)),
                pl.BlockSpec((1, N_PAIRS, 2, t, 2 * t), lambda bi, i: (gidx, 0, 0, 0, 0))]
    scratch = [pltpu.VMEM((N_PAIRS, LANES, 2 * t), BF16),
               pltpu.VMEM((2 * N_PAIRS, t, t), F32),
               pltpu.VMEM((N_PAIRS, 1, 2 * t), F32),
               pltpu.VMEM((N_PAIRS, V_ROWS, 2 * t), F32)]
    if kind == "diff":
        lamv, subln_g = extra
        in_specs += [pl.BlockSpec(lamv.shape, lambda bi, i: (0, 0)),
                     pl.BlockSpec(subln_g.shape, lambda bi, i: (0, 0))]
        args = (tab, q_t, proj, v_t, dtiles, lamv, subln_g)
        body = _diff_kernel
    else:
        nbp = max(32, nq)
        scratch += [pltpu.VMEM((N_PAIRS, nbp, LANES), F32), pltpu.VMEM((N_PAIRS, nbp, 2 * t), F32)]
        args = (tab, q_t, proj, v_t, dtiles)
        body = _moba_kernel
    y_t = pl.pallas_call(
        body,
        out_shape=jax.ShapeDtypeStruct((b, nq, w, t), BF16),
        grid=(b, nq),
        in_specs=in_specs,
        out_specs=pl.BlockSpec((1, 1, w, t), lambda bi, i: (bi, i, 0, 0)),
        scratch_shapes=scratch,
        compiler_params=pltpu.CompilerParams(
            dimension_semantics=("arbitrary", "arbitrary"),
            vmem_limit_bytes=VMEM_LIMIT_BYTES),
        name=kind + "_attn",
    )(*args)
    return jnp.swapaxes(y_t, 2, 3).reshape(b, s, w)


def _merge_kernel(x_ref, ya_ref, yb_ref, gla_ref, glb_ref, wa_ref, wb_ref, wo_ref,
                  gt_ref, shf_ref, scf_ref, gf_ref, wr_ref, br_ref,
                  x1_ref, h2_ref, ti_ref, tw_ref):
    a = jnp.dot(ya_ref[0], wa_ref[...], preferred_element_type=F32)
    b = jnp.dot(yb_ref[0], wb_ref[...], preferred_element_type=F32)
    merged = (jax.nn.sigmoid(gla_ref[0].astype(F32)) * a
              + jax.nn.sigmoid(glb_ref[0].astype(F32)) * b)
    mo = jnp.dot(merged.astype(BF16), wo_ref[...], preferred_element_type=F32)
    x1 = x_ref[0] + gt_ref[0] * mo
    x1_ref[0] = x1
    h = _rms(x1, gf_ref[...]) * (1.0 + scf_ref[0]) + shf_ref[0]
    _store_token_tiles(h2_ref.at[0], h)
    logits = jnp.dot(h.astype(BF16), wr_ref[...], preferred_element_type=F32) + br_ref[...]
    tm = logits.shape[0]
    lane = lax.broadcasted_iota(I32, (tm, LANES), 1)
    lane_f = lane.astype(F32)
    g = jnp.where(lane < N_EXPERTS, logits, -jnp.inf)
    vals, idxs = [], []
    for _ in range(TOP_K):
        mx = jnp.max(g, axis=1, keepdims=True)
        first = jnp.min(jnp.where(g == mx, lane_f, float(LANES)), axis=1, keepdims=True)
        vals.append(mx)
        idxs.append(first)
        g = jnp.where(lane_f == first, -jnp.inf, g)
    es = [jnp.exp(v - vals[0]) for v in vals]
    den = es[0] + es[1] + es[2] + es[3]
    ti = jnp.zeros((tm, LANES), F32)
    tw = jnp.zeros((tm, LANES), F32)
    for k in range(TOP_K):
        ti = jnp.where(lane == k, idxs[k], ti)
        tw = jnp.where(lane == k, es[k] / den, tw)
    ti_ref[0] = ti.astype(I32)
    tw_ref[0] = tw


def _merge(x, proj, y_a, y_b, w_br_a, w_br_b, w_out, mod, g_ffn, w_router, b_router):
    b, s, d = x.shape
    tm = min(MERGE_TM, s)
    full = lambda shape: pl.BlockSpec(shape, lambda bi, i: (0,) * len(shape))
    tile = lambda w, c=0: pl.BlockSpec((1, tm, w), lambda bi, i: (bi, i, c))
    modspec = lambda j: pl.BlockSpec((1, 1, d), lambda bi, i: (bi * 6 + j, 0, 0))
    return pl.pallas_call(
        _merge_kernel,
        out_shape=(jax.ShapeDtypeStruct((b, s, d), F32),
                   jax.ShapeDtypeStruct((b, s * ROW_CHUNKS, LANES), F32),
                   jax.ShapeDtypeStruct((b, s, LANES), I32),
                   jax.ShapeDtypeStruct((b, s, LANES), F32)),
        grid=(b, s // tm),
        in_specs=[tile(d), tile(512), tile(512), tile(d, 3), tile(d, 4),
                  full(w_br_a.shape), full(w_br_b.shape), full(w_out.shape),
                  modspec(2), modspec(3), modspec(4), full(g_ffn.shape),
                  full(w_router.shape), full(b_router.shape)],
        out_specs=(tile(d), pl.BlockSpec((1, tm * ROW_CHUNKS, LANES), lambda bi, i: (bi, i, 0)),
                   tile(LANES), tile(LANES)),
        compiler_params=pltpu.CompilerParams(dimension_semantics=("arbitrary", "arbitrary"),
                                             vmem_limit_bytes=VMEM_LIMIT_BYTES),
        name="merge",
    )(x, y_a, y_b, proj, proj, w_br_a, w_br_b, w_out, mod, mod, mod, g_ffn,
      w_router, b_router)


def _moe_kernel(te_ref, nu_ref, tok_ref, h2_ref, wgu_ref, bgu_ref, wd_ref, bd_ref,
                ys_ref, xbuf, sems, wgu_bf, wd_bf, gu_ref):
    r = pl.program_id(0)
    n_used = nu_ref[0]
    rc = ROW_CHUNKS
    tm = xbuf.shape[1] // rc
    cur = r % 2
    prv = 1 - cur

    def start_row(j, priority):
        src = h2_ref.at[pl.ds(pl.multiple_of(tok_ref[j], rc), rc)]
        dst = xbuf.at[cur, pl.ds(pl.multiple_of(j * rc, rc), rc)]
        pltpu.make_async_copy(src, dst, sems.at[cur]).start(priority=priority)

    def wait_tile(s):
        pltpu.make_async_copy(h2_ref.at[pl.ds(0, tm * rc)], xbuf.at[s], sems.at[s]).wait()

    @pl.when(r == 0)
    def _():
        def body(j, carry):
            start_row(j, 0)
            return carry
        lax.fori_loop(0, tm, body, 0)

    busy = jnp.logical_and(r >= 1, r <= n_used)
    e_prev = te_ref[jnp.maximum(r - 1, 0)]
    fresh = jnp.logical_or(r == 1, e_prev != te_ref[jnp.maximum(r - 2, 0)])

    @pl.when(jnp.logical_and(busy, fresh))
    def _():
        wgu_bf[...] = wgu_ref[0].astype(BF16)
        wd_bf[...] = wd_ref[0].astype(BF16)

    @pl.when(busy)
    def _():
        wait_tile(prv)
        x = _load_token_tiles(xbuf.at[prv]).astype(BF16)
        fc = MOE_FF_CHUNK
        n_gu = 2 * D_FF // fc
        n_dn = D_FF // fc
        n_slots = n_gu + n_dn

        def start_rows(slot):
            for j in range(slot * tm // n_slots, (slot + 1) * tm // n_slots):
                start_row(j, j % 2)

        for c in range(n_gu):
            lo, hi = c * fc, (c + 1) * fc
            gu_ref[:, lo:hi] = (jnp.dot(x, wgu_bf[:, lo:hi], preferred_element_type=F32)
                                + bgu_ref[0][:, lo:hi])
            start_rows(c)
        y = None
        for c in range(n_dn):
            lo, hi = c * fc, (c + 1) * fc
            gate = jnp.minimum(gu_ref[:, lo:hi], SWIGLU_LIMIT)
            up = jnp.clip(gu_ref[:, D_FF + lo:D_FF + hi], -SWIGLU_LIMIT, SWIGLU_LIMIT)
            act = (up + 1.0) * gate * jax.nn.sigmoid(SWIGLU_ALPHA * gate)
            part = jnp.dot(act.astype(BF16), wd_bf[lo:hi, :], preferred_element_type=F32)
            y = part if y is None else y + part
            start_rows(n_gu + c)
        _store_token_tiles(ys_ref, y + bd_ref[0])

        @pl.when(r == n_used)
        def _():
            wait_tile(cur)


def _moe(h2_tiles, row_start, tile_expert, n_used, w_gate_up, b_gate_up, w_down, b_down):
    d = D_MODEL
    rc = ROW_CHUNKS
    tm = MOE_TM
    n_tiles = tile_expert.shape[0]
    e, _, f2 = w_gate_up.shape
    wsel = lambda r, te, nu: (te[jnp.clip(r - 1, 0, n_tiles - 1)], 0, 0)
    return pl.pallas_call(
        _moe_kernel,
        out_shape=jax.ShapeDtypeStruct((n_tiles * tm * rc, LANES), F32),
        grid_spec=pltpu.PrefetchScalarGridSpec(
            num_scalar_prefetch=2,
            grid=(n_tiles + 1,),
            in_specs=[pl.BlockSpec((tm,), lambda r, te, nu: (jnp.minimum(r, n_tiles - 1),),
                                   memory_space=pltpu.SMEM),
                      pl.BlockSpec(memory_space=pl.ANY),
                      pl.BlockSpec((1, d, f2), wsel),
                      pl.BlockSpec((1, 1, f2), wsel),
                      pl.BlockSpec((1, D_FF, d), wsel),
                      pl.BlockSpec((1, 1, d), wsel)],
            out_specs=pl.BlockSpec(
                (tm * rc, LANES), lambda r, te, nu: (jnp.clip(r - 1, 0, nu[0] - 1), 0)),
            scratch_shapes=[pltpu.VMEM((2, tm * rc, LANES), F32), pltpu.SemaphoreType.DMA((2,)),
                            pltpu.VMEM((d, f2), BF16), pltpu.VMEM((D_FF, d), BF16),
                            pltpu.VMEM((tm, f2), F32)]),
        compiler_params=pltpu.CompilerParams(dimension_semantics=("arbitrary",),
                                             vmem_limit_bytes=VMEM_LIMIT_BYTES),
        name="moe",
    )(tile_expert, n_used, row_start, h2_tiles, w_gate_up, b_gate_up, w_down, b_down)


def _combine_kernel(pos_ref, ys_ref, x1_ref, tw_ref, gt_ref, gf_ref, o_ref, gbuf, sem):
    tm = x1_ref.shape[0]
    rc = ROW_CHUNKS

    def issue(j, carry):
        for k in range(TOP_K):
            src = ys_ref.at[pl.ds(pl.multiple_of(pos_ref[j * TOP_K + k], rc), rc)]
            dst = gbuf.at[k, pl.ds(pl.multiple_of(j * rc, rc), rc)]
            pltpu.make_async_copy(src, dst, sem).start(priority=k % 2)
        return carry

    lax.fori_loop(0, tm, issue, 0)
    for k in range(TOP_K):
        pltpu.make_async_copy(ys_ref.at[pl.ds(0, tm * rc)], gbuf.at[k], sem).wait()

    tw = tw_ref[...]
    acc = tw[:, 0:1] * _load_token_tiles(gbuf.at[0])
    for k in range(1, TOP_K):
        acc = acc + tw[:, k:k + 1] * _load_token_tiles(gbuf.at[k])
    x2 = x1_ref[...] + gt_ref[0] * acc
    o_ref[...] = _rms(x2, gf_ref[...])


def _combine(ys, pos_flat, x1, tw, mod, g_final, seq):
    t, d = x1.shape
    tm = min(COMBINE_TM, t)
    return pl.pallas_call(
        _combine_kernel,
        out_shape=jax.ShapeDtypeStruct((t, d), F32),
        grid=(t // tm,),
        in_specs=[pl.BlockSpec((tm * TOP_K,), lambda s: (s,), memory_space=pltpu.SMEM),
                  pl.BlockSpec(memory_space=pl.ANY),
                  pl.BlockSpec((tm, d), lambda s: (s, 0)),
                  pl.BlockSpec((tm, LANES), lambda s: (s, 0)),
                  pl.BlockSpec((1, 1, d), lambda s: ((s * tm) // seq * 6 + 5, 0, 0)),
                  pl.BlockSpec((1, d), lambda s: (0, 0))],
        out_specs=pl.BlockSpec((tm, d), lambda s: (s, 0)),
        scratch_shapes=[pltpu.VMEM((TOP_K, tm * ROW_CHUNKS, LANES), F32),
                        pltpu.SemaphoreType.DMA],
        compiler_params=pltpu.CompilerParams(dimension_semantics=("arbitrary",),
                                             vmem_limit_bytes=VMEM_LIMIT_BYTES),
        name="combine",
    )(pos_flat, ys, x1, tw, mod, g_final)


def _routing_tables(top_idx, n_tiles):
    tm = MOE_TM
    e_flat = top_idx.reshape(-1)
    onehot = (e_flat[:, None] == jnp.arange(N_EXPERTS, dtype=I32)[None, :]).astype(I32)
    csum = jnp.cumsum(onehot, axis=0)
    counts = csum[-1]
    rank = jnp.sum(csum * onehot, axis=1) - 1
    tiles_per = (counts + tm - 1) // tm
    tile_end = jnp.cumsum(tiles_per)
    tile_start = tile_end - tiles_per
    pos = (jnp.sum(onehot * tile_start[None, :], axis=1) * tm + rank).astype(I32)
    n_used = tile_end[-1:].astype(I32)
    tile_ids = jnp.arange(n_tiles, dtype=I32)
    tile_expert = jnp.minimum(
        jnp.sum((tile_ids[:, None] >= tile_end[None, :]).astype(I32), axis=1),
        N_EXPERTS - 1).astype(I32)
    big = 1 << 17
    assert e_flat.shape[0] < big
    experts = jnp.arange(N_EXPERTS, dtype=I32)[:, None]
    slot = jnp.arange(tm, dtype=I32)[None, :]
    n_pad = (tiles_per * tm - counts)[:, None]
    key_pad = jnp.where(slot < n_pad, experts * big + counts[:, None] + slot,
                        jnp.iinfo(jnp.int32).max)
    keys = jnp.concatenate([e_flat * big + rank, key_pad.reshape(-1)]).astype(I32)
    toks = jnp.concatenate([jnp.arange(e_flat.shape[0], dtype=I32) // TOP_K,
                            jnp.zeros((N_EXPERTS * tm,), I32)])
    _, row_token = lax.sort((keys, toks), num_keys=1)
    return pos, tile_expert, n_used, row_token


def kernel(x, c, rel_bias, w_ada, b_ada, g_mix, w_in, lambda_q1, lambda_k1, lambda_q2,
           lambda_k2, subln_g, w_br_a, w_br_b, w_out, g_ffn, w_router, b_router,
           w_gate_up, b_gate_up, w_down, b_down, g_final):
    b, s, d = x.shape
    assert d == D_MODEL and s % ATT_T == 0 and MOBA_BLOCK == ATT_T
    t_tok = b * s

    c8 = jnp.zeros((8, d), F32).at[:b].set(c)
    ada = _ada(c8, w_ada[0], b_ada[0].reshape(1, -1))
    mod = ada[:b].reshape(b * 6, 1, d)

    dtiles = _bias_tiles(rel_bias)
    tab = rel_bias.T.reshape(-1)

    proj = _proj(x, g_mix[0].reshape(1, d), mod, w_in[0].astype(BF16))

    lamv = jnp.concatenate([lambda_q1, lambda_k1, lambda_q2, lambda_k2], axis=0)
    y_a = _attention("diff", tab, proj, dtiles, (lamv, subln_g[0].reshape(-1, 1)), (0, 512, 1024))
    y_b = _attention("moba", tab, proj, dtiles, None, (1536, 2048, 2560))

    wr = jnp.zeros((d, LANES), BF16).at[:, :N_EXPERTS].set(w_router[0].astype(BF16))
    br = jnp.zeros((1, LANES), F32).at[0, :N_EXPERTS].set(b_router[0])
    x1, h2, ti, tw = _merge(x, proj, y_a, y_b, w_br_a[0].astype(BF16), w_br_b[0].astype(BF16),
                            w_out[0].astype(BF16), mod, g_ffn[0].reshape(1, d), wr, br)

    n_tiles = (t_tok * TOP_K) // MOE_TM + N_EXPERTS
    top_idx = ti.reshape(t_tok, LANES)[:, :TOP_K]
    pos, tile_expert, n_used, row_token = _routing_tables(top_idx, n_tiles)

    ys = _moe(h2.reshape(t_tok * ROW_CHUNKS, LANES), row_token * ROW_CHUNKS, tile_expert, n_used,
              w_gate_up[0], b_gate_up[0].reshape(N_EXPERTS, 1, -1), w_down[0],
              b_down[0].reshape(N_EXPERTS, 1, -1))
    out = _combine(ys, pos * ROW_CHUNKS, x1.reshape(t_tok, d), tw.reshape(t_tok, LANES), mod,
                   g_final.reshape(1, d), s)
    return out.reshape(b, s, d)
```

```python
import functools
import math

import jax
import jax.numpy as jnp
import numpy as np
from jax import lax
from jax.experimental import pallas as pl
from jax.experimental.pallas import tpu as pltpu

F32 = jnp.float32
BF16 = jnp.bfloat16
I32 = jnp.int32

D_MODEL = 1024
N_DIFF_HEADS = 4
DIFF_HEAD_DIM = 64
N_MOBA_HEADS = 8
MOBA_HEAD_DIM = 64
MOBA_BLOCK = 256
MOBA_TOPK = 3
N_HEADS_TOTAL = N_DIFF_HEADS + N_MOBA_HEADS
N_BUCKETS = 32
MAX_DISTANCE = 128
N_EXPERTS = 32
TOP_K = 4
D_FF = D_MODEL
SWIGLU_LIMIT = 7.0
SWIGLU_ALPHA = 1.702
RMS_EPS = 1e-5
LAM_INIT = 0.8 - 0.6 * math.exp(-0.3 * 0)
IN_WIDTH = 5120

LANES = 128
VMEM_LIMIT_BYTES = 56 * 1024 * 1024

ATT_T = 256
N_PAIRS = 4
V_ROWS = LANES + 16
PROJ_TM = 512
MERGE_TM = 512
MOE_TM = 256
MOE_FF_CHUNK = 256
COMBINE_TM = 256
NEG = -1e30


def _t5_bucket_np(dist):
    n = np.maximum(dist, 0)
    max_exact = N_BUCKETS // 2
    nf = np.maximum(n, max_exact).astype(np.float32)
    large = max_exact + (np.log(nf / max_exact) / math.log(MAX_DISTANCE / max_exact)
                         * (N_BUCKETS - max_exact)).astype(np.int32)
    large = np.minimum(large, N_BUCKETS - 1)
    return np.where(n < max_exact, n, large).astype(np.int32)


def _rms(x, g):
    return x * lax.rsqrt(jnp.mean(x * x, axis=-1, keepdims=True) + RMS_EPS) * g


ROW_CHUNKS = D_MODEL // LANES


def _store_token_tiles(ref, x):
    n = x.shape[0]
    for c in range(ROW_CHUNKS):
        ref[pl.ds(c, n, stride=ROW_CHUNKS), :] = x[:, c * LANES:(c + 1) * LANES]


def _load_token_tiles(ref):
    n = ref.shape[0] // ROW_CHUNKS
    return jnp.concatenate([ref[pl.ds(c, n, stride=ROW_CHUNKS), :] for c in range(ROW_CHUNKS)],
                           axis=1)


def _ada_kernel(c_ref, w_ref, b_ref, o_ref):
    o_ref[...] = jnp.dot(c_ref[...].astype(BF16), w_ref[...].astype(BF16),
                         preferred_element_type=F32) + b_ref[...]


def _ada(c8, w, b):
    d, n = w.shape
    tn = 1536
    return pl.pallas_call(
        _ada_kernel,
        out_shape=jax.ShapeDtypeStruct((8, n), F32),
        grid=(n // tn,),
        in_specs=[pl.BlockSpec((8, d), lambda j: (0, 0)),
                  pl.BlockSpec((d, tn), lambda j: (0, j)),
                  pl.BlockSpec((1, tn), lambda j: (0, j))],
        out_specs=pl.BlockSpec((8, tn), lambda j: (0, j)),
        compiler_params=pltpu.CompilerParams(dimension_semantics=("arbitrary",),
                                             vmem_limit_bytes=VMEM_LIMIT_BYTES),
        name="ada",
    )(c8, w, b)


def _bias_kernel(tab_ref, bd_ref, bp_ref, o_ref):
    g = pl.program_id(0)
    p = pl.program_id(1)
    t = pl.program_id(2)
    head = jnp.where(g == 0, p, N_DIFF_HEADS + 2 * p + t)
    bd = bd_ref[...]
    bp = bp_ref[...]
    d0 = jnp.where(bd < 0, NEG, 0.0).astype(F32)
    d1 = jnp.zeros(bp.shape, F32)
    for b in range(N_BUCKETS):
        val = tab_ref[head * N_BUCKETS + b]
        d0 = jnp.where(bd == b, val, d0)
        d1 = jnp.where(bp == b, val, d1)
    o_ref[0, 0, 0] = d0
    o_ref[0, 0, 1] = d1


def _bias_tiles(rel_bias):
    t = ATT_T
    jj = np.arange(t)[:, None]
    ii = np.arange(t)[None, :]
    bd = np.where(ii >= jj, _t5_bucket_np(ii - jj), -1).astype(np.int32)
    bp = _t5_bucket_np(t + ii - jj)
    tab = rel_bias.T.reshape(-1)
    return pl.pallas_call(
        _bias_kernel,
        out_shape=jax.ShapeDtypeStruct((2, N_PAIRS, 2, t, 2 * t), F32),
        grid=(2, N_PAIRS, 2),
        in_specs=[pl.BlockSpec(memory_space=pltpu.SMEM),
                  pl.BlockSpec((t, t), lambda g, p, h: (0, 0)),
                  pl.BlockSpec((t, t), lambda g, p, h: (0, 0))],
        out_specs=pl.BlockSpec((1, 1, 2, t, t), lambda g, p, h: (g, p, 0, 0, h)),
        compiler_params=pltpu.CompilerParams(
            dimension_semantics=("arbitrary", "arbitrary", "arbitrary")),
        name="bias_tiles",
    )(tab, jnp.asarray(bd), jnp.asarray(bp))


def _proj_kernel(x_ref, g_ref, sh_ref, sc_ref, w_ref, o_ref):
    x = x_ref[0]
    h = _rms(x, g_ref[...]) * (1.0 + sc_ref[0]) + sh_ref[0]
    hb = h.astype(BF16)
    n_total = w_ref.shape[1]
    for n0 in range(0, n_total, 512):
        o_ref[0, :, n0:n0 + 512] = jnp.dot(
            hb, w_ref[:, n0:n0 + 512], preferred_element_type=F32).astype(BF16)


def _proj(x, g_mix, mod, w_in_bf):
    b, s, d = x.shape
    n = w_in_bf.shape[1]
    tm = min(PROJ_TM, s)
    return pl.pallas_call(
        _proj_kernel,
        out_shape=jax.ShapeDtypeStruct((b, s, n), BF16),
        grid=(b, s // tm),
        in_specs=[pl.BlockSpec((1, tm, d), lambda bi, i: (bi, i, 0)),
                  pl.BlockSpec((1, d), lambda bi, i: (0, 0)),
                  pl.BlockSpec((1, 1, d), lambda bi, i: (bi * 6 + 0, 0, 0)),
                  pl.BlockSpec((1, 1, d), lambda bi, i: (bi * 6 + 1, 0, 0)),
                  pl.BlockSpec((d, n), lambda bi, i: (0, 0))],
        out_specs=pl.BlockSpec((1, tm, n), lambda bi, i: (bi, i, 0)),
        compiler_params=pltpu.CompilerParams(dimension_semantics=("arbitrary", "arbitrary"),
                                             vmem_limit_bytes=VMEM_LIMIT_BYTES),
        name="proj",
    )(x, g_mix, mod, mod, w_in_bf)


def _stack_q(q_ref, q2_ref, scale):
    t = ATT_T
    row = lax.broadcasted_iota(I32, (LANES, t), 0)
    for p in range(N_PAIRS):
        q = q_ref[0, 0, p * LANES:(p + 1) * LANES, :].astype(F32) * scale
        q2_ref[p, :, 0:t] = jnp.where(row < 64, q, 0.0).astype(BF16)
        q2_ref[p, :, t:2 * t] = jnp.where(row >= 64, q, 0.0).astype(BF16)


def _flash_init(m_ref, acc_ref):
    m_ref[...] = jnp.full(m_ref.shape, NEG, F32)
    acc_ref[...] = jnp.zeros(acc_ref.shape, F32)


def _score_slot(p, h, n):
    return jnp.bitwise_and(n, 1) * (2 * N_PAIRS) + 2 * p + h


def _flash_scores(p, h, n, q2_ref, k_ref, s_ref):
    t = ATT_T
    kt = k_ref[0, pl.ds(pl.multiple_of(n * t, t), t), p * LANES:(p + 1) * LANES]
    s_ref[_score_slot(p, h, n)] = jnp.dot(kt, q2_ref[p, :, h * t:(h + 1) * t],
                                          preferred_element_type=F32)


def _flash_absorb(p, h, n, tile_bias, row_bias, live, v_ref, s_ref, m_ref, acc_ref):
    t = ATT_T
    cs = slice(h * t, (h + 1) * t)
    vt = v_ref[0, n, p * V_ROWS:(p + 1) * V_ROWS, :]
    s = s_ref[_score_slot(p, h, n)]
    if tile_bias is not None:
        s = s + tile_bias
    mx = jnp.max(s, axis=0, keepdims=True)
    if row_bias is not None:
        mx = mx + row_bias
    if live is not None:
        mx = jnp.where(live, mx, NEG)
    m_prev = m_ref[p, :, cs]
    m_new = jnp.maximum(m_prev, mx)
    alpha = jnp.exp(m_prev - m_new)
    shift = m_new if row_bias is None else m_new - row_bias
    if live is not None:
        shift = jnp.where(live, shift, -NEG)
    pt = jnp.exp(s - shift)
    acc_ref[p, :, cs] = alpha * acc_ref[p, :, cs] + jnp.dot(vt, pt.astype(BF16),
                                                            preferred_element_type=F32)
    m_ref[p, :, cs] = m_new


def _far_bias(tab_ref, head):
    return tab_ref[head * N_BUCKETS + (N_BUCKETS - 1)]


def _flash_sweep(i, far_args, prev_args, own_args, refs):
    q2_ref, k_ref, v_ref, s_ref, m_ref, acc_ref = refs
    chains = [(p, h) for p in range(N_PAIRS) for h in range(2)]

    def step(n, args, with_next):
        for p, h in chains:
            if with_next:
                _flash_scores(p, h, n + 1, q2_ref, k_ref, s_ref)
            _flash_absorb(p, h, n, *args(p, h), v_ref, s_ref, m_ref, acc_ref)

    for p, h in chains:
        _flash_scores(p, h, 0, q2_ref, k_ref, s_ref)

    def far_body(n, carry):
        step(n, lambda p, h: far_args(p, h, n), True)
        return carry

    lax.fori_loop(0, jnp.maximum(i - 1, 0), far_body, 0)

    @pl.when(i >= 1)
    def _():
        step(i - 1, prev_args, True)

    step(i, own_args, False)


def _diff_kernel(tab_ref, q_ref, k_ref, v_ref, d_ref, lamv_ref, g_ref, o_ref,
                 q2_ref, s_ref, m_ref, acc_ref):
    t = ATT_T
    i = pl.program_id(1)
    _stack_q(q_ref, q2_ref, DIFF_HEAD_DIM ** -0.5)
    _flash_init(m_ref, acc_ref)
    refs = (q2_ref, k_ref, v_ref, s_ref, m_ref, acc_ref)
    _flash_sweep(i,
                 lambda p, h, n: (None, _far_bias(tab_ref, p), None),
                 lambda p, h: (d_ref[0, p, 1, :, h * t:(h + 1) * t], None, None),
                 lambda p, h: (d_ref[0, p, 0, :, h * t:(h + 1) * t], None, None),
                 refs)

    lv = lamv_ref[...]
    lam = (jnp.exp(jnp.sum(lv[0:1] * lv[1:2], axis=1, keepdims=True))
           - jnp.exp(jnp.sum(lv[2:3] * lv[3:4], axis=1, keepdims=True)) + LAM_INIT)
    for p in range(N_PAIRS):
        acc = acc_ref[p, 0:LANES, :]
        l = acc_ref[p, LANES:LANES + 1, :]
        o = acc[:, 0:t] / l[:, 0:t] - lam * (acc[:, t:2 * t] / l[:, t:2 * t])
        y = o * lax.rsqrt(jnp.mean(o * o, axis=0, keepdims=True) + RMS_EPS) * g_ref[...]
        o_ref[0, 0, p * LANES:(p + 1) * LANES, :] = (y * (1.0 - LAM_INIT)).astype(BF16)


def _moba_kernel(tab_ref, q_ref, k_ref, v_ref, d_ref, o_ref,
                 q2_ref, s_ref, m_ref, acc_ref, kmean_ref, sel_ref):
    t = ATT_T
    i = pl.program_id(1)
    nb = k_ref.shape[1] // t
    nbp = kmean_ref.shape[1]

    @pl.when(i == 0)
    def _():
        kmean_ref[...] = jnp.zeros(kmean_ref.shape, F32)
        for p in range(N_PAIRS):
            for n in range(nb):
                blk = k_ref[0, n * t:(n + 1) * t, p * LANES:(p + 1) * LANES].astype(F32)
                kmean_ref[p, n:n + 1, :] = jnp.sum(blk, axis=0, keepdims=True) * (1.0 / t)

    _stack_q(q_ref, q2_ref, MOBA_HEAD_DIM ** -0.5)
    _flash_init(m_ref, acc_ref)

    blk_id = lax.broadcasted_iota(I32, (nbp, 2 * t), 0)
    blk_f = blk_id.astype(F32)
    for p in range(N_PAIRS):
        gs = jnp.dot(kmean_ref[p].astype(BF16), q2_ref[p], preferred_element_type=F32)
        g = jnp.where(blk_id < i, gs, -jnp.inf)
        sel = jnp.full((nbp, 2 * t), NEG, F32)
        for _ in range(MOBA_TOPK):
            mx = jnp.max(g, axis=0, keepdims=True)
            first = jnp.min(jnp.where(g == mx, blk_f, float(nbp)), axis=0, keepdims=True)
            hit = blk_f == first
            sel = jnp.where(hit, 0.0, sel)
            g = jnp.where(hit, -jnp.inf, g)
        sel_ref[p] = sel

    def live(p, h, n):
        return sel_ref[p, pl.ds(n, 1), h * t:(h + 1) * t] > -1.0

    refs = (q2_ref, k_ref, v_ref, s_ref, m_ref, acc_ref)
    _flash_sweep(i,
                 lambda p, h, n: (None, _far_bias(tab_ref, N_DIFF_HEADS + 2 * p + h),
                                  live(p, h, n)),
                 lambda p, h: (d_ref[0, p, 1, :, h * t:(h + 1) * t], None, live(p, h, i - 1)),
                 lambda p, h: (d_ref[0, p, 0, :, h * t:(h + 1) * t], None, None),
                 refs)

    row = lax.broadcasted_iota(I32, (LANES, t), 0)
    for p in range(N_PAIRS):
        acc = acc_ref[p, 0:LANES, :]
        l = acc_ref[p, LANES:LANES + 1, :]
        o = jnp.where(row < 64, acc[:, 0:t] / l[:, 0:t], acc[:, t:2 * t] / l[:, t:2 * t])
        o_ref[0, 0, p * LANES:(p + 1) * LANES, :] = o.astype(BF16)


def _to_kv_major(proj, col0, nq):
    b, s, _ = proj.shape
    w = N_PAIRS * LANES
    return jnp.swapaxes(proj[:, :, col0:col0 + w].reshape(b, nq, ATT_T, w), 2, 3)


def _values_kv_major(proj, col0, nq):
    b, s, _ = proj.shape
    v = proj[:, :, col0:col0 + N_PAIRS * LANES].reshape(b, nq, ATT_T, N_PAIRS, LANES)
    ones = jnp.ones((b, nq, ATT_T, N_PAIRS, V_ROWS - LANES), BF16)
    v = jnp.concatenate([v, ones], axis=-1)
    return jnp.transpose(v, (0, 1, 3, 4, 2)).reshape(b, nq, N_PAIRS * V_ROWS, ATT_T)


def _attention(kind, tab, proj, dtiles, extra, cols):
    b, s, _ = proj.shape
    t = ATT_T
    nq = s // t
    w = N_PAIRS * LANES
    qc, kc, vc = cols
    gidx = 0 if kind == "diff" else 1
    q_t = _to_kv_major(proj, qc, nq)
    v_t = _values_kv_major(proj, vc, nq)
    in_specs = [pl.BlockSpec(memory_space=pltpu.SMEM),
                pl.BlockSpec((1, 1, w, t), lambda bi, i: (bi, i, 0, 0)),
                pl.BlockSpec((1, s, w), lambda bi, i: (bi, 0, kc // w)),
                pl.BlockSpec((1, nq, N_PAIRS * V_ROWS, t), lambda bi, i: (bi, 0, 0, 0)),
                pl.BlockSpec((1, N_PAIRS, 2, t, 2 * t), lambda bi, i: (gidx, 0, 0, 0, 0))]
    scratch = [pltpu.VMEM((N_PAIRS, LANES, 2 * t), BF16),
               pltpu.VMEM((2 * 2 * N_PAIRS, t, t), F32),
               pltpu.VMEM((N_PAIRS, 1, 2 * t), F32),
               pltpu.VMEM((N_PAIRS, V_ROWS, 2 * t), F32)]
    if kind == "diff":
        lamv, subln_g = extra
        in_specs += [pl.BlockSpec(lamv.shape, lambda bi, i: (0, 0)),
                     pl.BlockSpec(subln_g.shape, lambda bi, i: (0, 0))]
        args = (tab, q_t, proj, v_t, dtiles, lamv, subln_g)
        body = _diff_kernel
    else:
        nbp = max(32, nq)
        scratch += [pltpu.VMEM((N_PAIRS, nbp, LANES), F32), pltpu.VMEM((N_PAIRS, nbp, 2 * t), F32)]
        args = (tab, q_t, proj, v_t, dtiles)
        body = _moba_kernel
    y_t = pl.pallas_call(
        body,
        out_shape=jax.ShapeDtypeStruct((b, nq, w, t), BF16),
        grid=(b, nq),
        in_specs=in_specs,
        out_specs=pl.BlockSpec((1, 1, w, t), lambda bi, i: (bi, i, 0, 0)),
        scratch_shapes=scratch,
        compiler_params=pltpu.CompilerParams(
            dimension_semantics=("arbitrary", "arbitrary"),
            vmem_limit_bytes=VMEM_LIMIT_BYTES),
        name=kind + "_attn",
    )(*args)
    return jnp.swapaxes(y_t, 2, 3).reshape(b, s, w)


def _merge_kernel(x_ref, ya_ref, yb_ref, gla_ref, glb_ref, wa_ref, wb_ref, wo_ref,
                  gt_ref, shf_ref, scf_ref, gf_ref, wr_ref, br_ref,
                  x1_ref, h2_ref, ti_ref, tw_ref):
    a = jnp.dot(ya_ref[0], wa_ref[...], preferred_element_type=F32)
    b = jnp.dot(yb_ref[0], wb_ref[...], preferred_element_type=F32)
    merged = (jax.nn.sigmoid(gla_ref[0].astype(F32)) * a
              + jax.nn.sigmoid(glb_ref[0].astype(F32)) * b)
    mo = jnp.dot(merged.astype(BF16), wo_ref[...], preferred_element_type=F32)
    x1 = x_ref[0] + gt_ref[0] * mo
    x1_ref[0] = x1
    h = _rms(x1, gf_ref[...]) * (1.0 + scf_ref[0]) + shf_ref[0]
    _store_token_tiles(h2_ref.at[0], h)
    logits = jnp.dot(h.astype(BF16), wr_ref[...], preferred_element_type=F32) + br_ref[...]
    tm = logits.shape[0]
    lane = lax.broadcasted_iota(I32, (tm, LANES), 1)
    lane_f = lane.astype(F32)
    g = jnp.where(lane < N_EXPERTS, logits, -jnp.inf)
    vals, idxs = [], []
    for _ in range(TOP_K):
        mx = jnp.max(g, axis=1, keepdims=True)
        first = jnp.min(jnp.where(g == mx, lane_f, float(LANES)), axis=1, keepdims=True)
        vals.append(mx)
        idxs.append(first)
        g = jnp.where(lane_f == first, -jnp.inf, g)
    es = [jnp.exp(v - vals[0]) for v in vals]
    den = es[0] + es[1] + es[2] + es[3]
    ti = jnp.zeros((tm, LANES), F32)
    tw = jnp.zeros((tm, LANES), F32)
    for k in range(TOP_K):
        ti = jnp.where(lane == k, idxs[k], ti)
        tw = jnp.where(lane == k, es[k] / den, tw)
    ti_ref[0] = ti.astype(I32)
    tw_ref[0] = tw


def _merge(x, proj, y_a, y_b, w_br_a, w_br_b, w_out, mod, g_ffn, w_router, b_router):
    b, s, d = x.shape
    tm = min(MERGE_TM, s)
    full = lambda shape: pl.BlockSpec(shape, lambda bi, i: (0,) * len(shape))
    tile = lambda w, c=0: pl.BlockSpec((1, tm, w), lambda bi, i: (bi, i, c))
    modspec = lambda j: pl.BlockSpec((1, 1, d), lambda bi, i: (bi * 6 + j, 0, 0))
    return pl.pallas_call(
        _merge_kernel,
        out_shape=(jax.ShapeDtypeStruct((b, s, d), F32),
                   jax.ShapeDtypeStruct((b, s * ROW_CHUNKS, LANES), F32),
                   jax.ShapeDtypeStruct((b, s, LANES), I32),
                   jax.ShapeDtypeStruct((b, s, LANES), F32)),
        grid=(b, s // tm),
        in_specs=[tile(d), tile(512), tile(512), tile(d, 3), tile(d, 4),
                  full(w_br_a.shape), full(w_br_b.shape), full(w_out.shape),
                  modspec(2), modspec(3), modspec(4), full(g_ffn.shape),
                  full(w_router.shape), full(b_router.shape)],
        out_specs=(tile(d), pl.BlockSpec((1, tm * ROW_CHUNKS, LANES), lambda bi, i: (bi, i, 0)),
                   tile(LANES), tile(LANES)),
        compiler_params=pltpu.CompilerParams(dimension_semantics=("arbitrary", "arbitrary"),
                                             vmem_limit_bytes=VMEM_LIMIT_BYTES),
        name="merge",
    )(x, y_a, y_b, proj, proj, w_br_a, w_br_b, w_out, mod, mod, mod, g_ffn,
      w_router, b_router)


def _moe_kernel(te_ref, nu_ref, tok_ref, h2_ref, wgu_ref, bgu_ref, wd_ref, bd_ref,
                ys_ref, xbuf, sems, wgu_bf, wd_bf, gu_ref):
    r = pl.program_id(0)
    n_used = nu_ref[0]
    rc = ROW_CHUNKS
    tm = xbuf.shape[1] // rc
    cur = r % 2
    prv = 1 - cur

    def start_row(j, priority):
        src = h2_ref.at[pl.ds(pl.multiple_of(tok_ref[j], rc), rc)]
        dst = xbuf.at[cur, pl.ds(pl.multiple_of(j * rc, rc), rc)]
        pltpu.make_async_copy(src, dst, sems.at[cur]).start(priority=priority)

    def wait_tile(s):
        pltpu.make_async_copy(h2_ref.at[pl.ds(0, tm * rc)], xbuf.at[s], sems.at[s]).wait()

    @pl.when(r == 0)
    def _():
        def body(j, carry):
            start_row(j, 0)
            return carry
        lax.fori_loop(0, tm, body, 0)

    busy = jnp.logical_and(r >= 1, r <= n_used)
    e_prev = te_ref[jnp.maximum(r - 1, 0)]
    fresh = jnp.logical_or(r == 1, e_prev != te_ref[jnp.maximum(r - 2, 0)])

    @pl.when(jnp.logical_and(busy, fresh))
    def _():
        wgu_bf[...] = wgu_ref[0].astype(BF16)
        wd_bf[...] = wd_ref[0].astype(BF16)

    @pl.when(busy)
    def _():
        wait_tile(prv)
        x = _load_token_tiles(xbuf.at[prv]).astype(BF16)
        fc = MOE_FF_CHUNK
        n_gu = 2 * D_FF // fc
        n_dn = D_FF // fc
        n_slots = n_gu + n_dn

        def start_rows(slot):
            for j in range(slot * tm // n_slots, (slot + 1) * tm // n_slots):
                start_row(j, j % 2)

        for c in range(n_gu):
            lo, hi = c * fc, (c + 1) * fc
            gu_ref[:, lo:hi] = (jnp.dot(x, wgu_bf[:, lo:hi], preferred_element_type=F32)
                                + bgu_ref[0][:, lo:hi])
            start_rows(c)
        y = None
        for c in range(n_dn):
            lo, hi = c * fc, (c + 1) * fc
            gate = jnp.minimum(gu_ref[:, lo:hi], SWIGLU_LIMIT)
            up = jnp.clip(gu_ref[:, D_FF + lo:D_FF + hi], -SWIGLU_LIMIT, SWIGLU_LIMIT)
            act = (up + 1.0) * gate * jax.nn.sigmoid(SWIGLU_ALPHA * gate)
            part = jnp.dot(act.astype(BF16), wd_bf[lo:hi, :], preferred_element_type=F32)
            y = part if y is None else y + part
            start_rows(n_gu + c)
        _store_token_tiles(ys_ref, y + bd_ref[0])

        @pl.when(r == n_used)
        def _():
            wait_tile(cur)


def _moe(h2_tiles, row_start, tile_expert, n_used, w_gate_up, b_gate_up, w_down, b_down):
    d = D_MODEL
    rc = ROW_CHUNKS
    tm = MOE_TM
    n_tiles = tile_expert.shape[0]
    e, _, f2 = w_gate_up.shape
    wsel = lambda r, te, nu: (te[jnp.clip(r - 1, 0, n_tiles - 1)], 0, 0)
    return pl.pallas_call(
        _moe_kernel,
        out_shape=jax.ShapeDtypeStruct((n_tiles * tm * rc, LANES), F32),
        grid_spec=pltpu.PrefetchScalarGridSpec(
            num_scalar_prefetch=2,
            grid=(n_tiles + 1,),
            in_specs=[pl.BlockSpec((tm,), lambda r, te, nu: (jnp.minimum(r, n_tiles - 1),),
                                   memory_space=pltpu.SMEM),
                      pl.BlockSpec(memory_space=pl.ANY),
                      pl.BlockSpec((1, d, f2), wsel),
                      pl.BlockSpec((1, 1, f2), wsel),
                      pl.BlockSpec((1, D_FF, d), wsel),
                      pl.BlockSpec((1, 1, d), wsel)],
            out_specs=pl.BlockSpec(
                (tm * rc, LANES), lambda r, te, nu: (jnp.clip(r - 1, 0, nu[0] - 1), 0)),
            scratch_shapes=[pltpu.VMEM((2, tm * rc, LANES), F32), pltpu.SemaphoreType.DMA((2,)),
                            pltpu.VMEM((d, f2), BF16), pltpu.VMEM((D_FF, d), BF16),
                            pltpu.VMEM((tm, f2), F32)]),
        compiler_params=pltpu.CompilerParams(dimension_semantics=("arbitrary",),
                                             vmem_limit_bytes=VMEM_LIMIT_BYTES),
        name="moe",
    )(tile_expert, n_used, row_start, h2_tiles, w_gate_up, b_gate_up, w_down, b_down)


def _combine_kernel(pos_ref, ys_ref, x1_ref, tw_ref, gt_ref, gf_ref, o_ref, gbuf, sem):
    tm = x1_ref.shape[0]
    rc = ROW_CHUNKS

    def issue(j, carry):
        for k in range(TOP_K):
            src = ys_ref.at[pl.ds(pl.multiple_of(pos_ref[j * TOP_K + k], rc), rc)]
            dst = gbuf.at[k, pl.ds(pl.multiple_of(j * rc, rc), rc)]
            pltpu.make_async_copy(src, dst, sem).start(priority=k % 2)
        return carry

    lax.fori_loop(0, tm, issue, 0)
    for k in range(TOP_K):
        pltpu.make_async_copy(ys_ref.at[pl.ds(0, tm * rc)], gbuf.at[k], sem).wait()

    tw = tw_ref[...]
    acc = tw[:, 0:1] * _load_token_tiles(gbuf.at[0])
    for k in range(1, TOP_K):
        acc = acc + tw[:, k:k + 1] * _load_token_tiles(gbuf.at[k])
    x2 = x1_ref[...] + gt_ref[0] * acc
    o_ref[...] = _rms(x2, gf_ref[...])


def _combine(ys, pos_flat, x1, tw, mod, g_final, seq):
    t, d = x1.shape
    tm = min(COMBINE_TM, t)
    return pl.pallas_call(
        _combine_kernel,
        out_shape=jax.ShapeDtypeStruct((t, d), F32),
        grid=(t // tm,),
        in_specs=[pl.BlockSpec((tm * TOP_K,), lambda s: (s,), memory_space=pltpu.SMEM),
                  pl.BlockSpec(memory_space=pl.ANY),
                  pl.BlockSpec((tm, d), lambda s: (s, 0)),
                  pl.BlockSpec((tm, LANES), lambda s: (s, 0)),
                  pl.BlockSpec((1, 1, d), lambda s: ((s * tm) // seq * 6 + 5, 0, 0)),
                  pl.BlockSpec((1, d), lambda s: (0, 0))],
        out_specs=pl.BlockSpec((tm, d), lambda s: (s, 0)),
        scratch_shapes=[pltpu.VMEM((TOP_K, tm * ROW_CHUNKS, LANES), F32),
                        pltpu.SemaphoreType.DMA],
        compiler_params=pltpu.CompilerParams(dimension_semantics=("arbitrary",),
                                             vmem_limit_bytes=VMEM_LIMIT_BYTES),
        name="combine",
    )(pos_flat, ys, x1, tw, mod, g_final)


def _routing_tables(top_idx, n_tiles):
    tm = MOE_TM
    e_flat = top_idx.reshape(-1)
    onehot = (e_flat[:, None] == jnp.arange(N_EXPERTS, dtype=I32)[None, :]).astype(I32)
    csum = jnp.cumsum(onehot, axis=0)
    counts = csum[-1]
    rank = jnp.sum(csum * onehot, axis=1) - 1
    tiles_per = (counts + tm - 1) // tm
    tile_end = jnp.cumsum(tiles_per)
    tile_start = tile_end - tiles_per
    pos = (jnp.sum(onehot * tile_start[None, :], axis=1) * tm + rank).astype(I32)
    n_used = tile_end[-1:].astype(I32)
    tile_ids = jnp.arange(n_tiles, dtype=I32)
    tile_expert = jnp.minimum(
        jnp.sum((tile_ids[:, None] >= tile_end[None, :]).astype(I32), axis=1),
        N_EXPERTS - 1).astype(I32)
    big = 1 << 17
    assert e_flat.shape[0] < big
    experts = jnp.arange(N_EXPERTS, dtype=I32)[:, None]
    slot = jnp.arange(tm, dtype=I32)[None, :]
    n_pad = (tiles_per * tm - counts)[:, None]
    key_pad = jnp.where(slot < n_pad, experts * big + counts[:, None] + slot,
                        jnp.iinfo(jnp.int32).max)
    keys = jnp.concatenate([e_flat * big + rank, key_pad.reshape(-1)]).astype(I32)
    toks = jnp.concatenate([jnp.arange(e_flat.shape[0], dtype=I32) // TOP_K,
                            jnp.zeros((N_EXPERTS * tm,), I32)])
    _, row_token = lax.sort((keys, toks), num_keys=1)
    return pos, tile_expert, n_used, row_token


def kernel(x, c, rel_bias, w_ada, b_ada, g_mix, w_in, lambda_q1, lambda_k1, lambda_q2,
           lambda_k2, subln_g, w_br_a, w_br_b, w_out, g_ffn, w_router, b_router,
           w_gate_up, b_gate_up, w_down, b_down, g_final):
    b, s, d = x.shape
    assert d == D_MODEL and s % ATT_T == 0 and MOBA_BLOCK == ATT_T
    t_tok = b * s

    c8 = jnp.zeros((8, d), F32).at[:b].set(c)
    ada = _ada(c8, w_ada[0], b_ada[0].reshape(1, -1))
    mod = ada[:b].reshape(b * 6, 1, d)

    dtiles = _bias_tiles(rel_bias)
    tab = rel_bias.T.reshape(-1)

    proj = _proj(x, g_mix[0].reshape(1, d), mod, w_in[0].astype(BF16))

    lamv = jnp.concatenate([lambda_q1, lambda_k1, lambda_q2, lambda_k2], axis=0)
    y_a = _attention("diff", tab, proj, dtiles, (lamv, subln_g[0].reshape(-1, 1)), (0, 512, 1024))
    y_b = _attention("moba", tab, proj, dtiles, None, (1536, 2048, 2560))

    wr = jnp.zeros((d, LANES), BF16).at[:, :N_EXPERTS].set(w_router[0].astype(BF16))
    br = jnp.zeros((1, LANES), F32).at[0, :N_EXPERTS].set(b_router[0])
    x1, h2, ti, tw = _merge(x, proj, y_a, y_b, w_br_a[0].astype(BF16), w_br_b[0].astype(BF16),
                            w_out[0].astype(BF16), mod, g_ffn[0].reshape(1, d), wr, br)

    n_tiles = (t_tok * TOP_K) // MOE_TM + N_EXPERTS
    top_idx = ti.reshape(t_tok, LANES)[:, :TOP_K]
    pos, tile_expert, n_used, row_token = _routing_tables(top_idx, n_tiles)

    ys = _moe(h2.reshape(t_tok * ROW_CHUNKS, LANES), row_token * ROW_CHUNKS, tile_expert, n_used,
              w_gate_up[0], b_gate_up[0].reshape(N_EXPERTS, 1, -1), w_down[0],
              b_down[0].reshape(N_EXPERTS, 1, -1))
    out = _combine(ys, pos * ROW_CHUNKS, x1.reshape(t_tok, d), tw.reshape(t_tok, LANES), mod,
                   g_final.reshape(1, d), s)
    return out.reshape(b, s, d)
```

```python
import functools
import math

import jax
import jax.numpy as jnp
import numpy as np
from jax import lax
from jax.experimental import pallas as pl
from jax.experimental.pallas import tpu as pltpu

F32 = jnp.float32
BF16 = jnp.bfloat16
I32 = jnp.int32

D_MODEL = 1024
N_DIFF_HEADS = 4
DIFF_HEAD_DIM = 64
N_MOBA_HEADS = 8
MOBA_HEAD_DIM = 64
MOBA_BLOCK = 256
MOBA_TOPK = 3
N_HEADS_TOTAL = N_DIFF_HEADS + N_MOBA_HEADS
N_BUCKETS = 32
MAX_DISTANCE = 128
N_EXPERTS = 32
TOP_K = 4
D_FF = D_MODEL
SWIGLU_LIMIT = 7.0
SWIGLU_ALPHA = 1.702
RMS_EPS = 1e-5
LAM_INIT = 0.8 - 0.6 * math.exp(-0.3 * 0)
IN_WIDTH = 5120

LANES = 128
VMEM_LIMIT_BYTES = 56 * 1024 * 1024

ATT_T = 256
N_PAIRS = 4
V_ROWS = LANES + 16
PROJ_TM = 512
MERGE_TM = 512
MOE_TM = 256
MOE_FF_CHUNK = 256
COMBINE_TM = 256
NEG = -1e30


def _t5_bucket_np(dist):
    n = np.maximum(dist, 0)
    max_exact = N_BUCKETS // 2
    nf = np.maximum(n, max_exact).astype(np.float32)
    large = max_exact + (np.log(nf / max_exact) / math.log(MAX_DISTANCE / max_exact)
                         * (N_BUCKETS - max_exact)).astype(np.int32)
    large = np.minimum(large, N_BUCKETS - 1)
    return np.where(n < max_exact, n, large).astype(np.int32)


def _rms(x, g):
    return x * lax.rsqrt(jnp.mean(x * x, axis=-1, keepdims=True) + RMS_EPS) * g


ROW_CHUNKS = D_MODEL // LANES


def _store_token_tiles(ref, x):
    n = x.shape[0]
    for c in range(ROW_CHUNKS):
        ref[pl.ds(c, n, stride=ROW_CHUNKS), :] = x[:, c * LANES:(c + 1) * LANES]


def _load_token_tiles(ref):
    n = ref.shape[0] // ROW_CHUNKS
    return jnp.concatenate([ref[pl.ds(c, n, stride=ROW_CHUNKS), :] for c in range(ROW_CHUNKS)],
                           axis=1)


def _ada_kernel(c_ref, w_ref, b_ref, o_ref):
    o_ref[...] = jnp.dot(c_ref[...].astype(BF16), w_ref[...].astype(BF16),
                         preferred_element_type=F32) + b_ref[...]


def _ada(c8, w, b):
    d, n = w.shape
    tn = 1536
    return pl.pallas_call(
        _ada_kernel,
        out_shape=jax.ShapeDtypeStruct((8, n), F32),
        grid=(n // tn,),
        in_specs=[pl.BlockSpec((8, d), lambda j: (0, 0)),
                  pl.BlockSpec((d, tn), lambda j: (0, j)),
                  pl.BlockSpec((1, tn), lambda j: (0, j))],
        out_specs=pl.BlockSpec((8, tn), lambda j: (0, j)),
        compiler_params=pltpu.CompilerParams(dimension_semantics=("arbitrary",),
                                             vmem_limit_bytes=VMEM_LIMIT_BYTES),
        name="ada",
    )(c8, w, b)


def _bias_kernel(tab_ref, bd_ref, bp_ref, o_ref):
    g = pl.program_id(0)
    p = pl.program_id(1)
    t = pl.program_id(2)
    head = jnp.where(g == 0, p, N_DIFF_HEADS + 2 * p + t)
    bd = bd_ref[...]
    bp = bp_ref[...]
    d0 = jnp.where(bd < 0, NEG, 0.0).astype(F32)
    d1 = jnp.zeros(bp.shape, F32)
    for b in range(N_BUCKETS):
        val = tab_ref[head * N_BUCKETS + b]
        d0 = jnp.where(bd == b, val, d0)
        d1 = jnp.where(bp == b, val, d1)
    o_ref[0, 0, 0] = d0
    o_ref[0, 0, 1] = d1


def _bias_tiles(rel_bias):
    t = ATT_T
    jj = np.arange(t)[:, None]
    ii = np.arange(t)[None, :]
    bd = np.where(ii >= jj, _t5_bucket_np(ii - jj), -1).astype(np.int32)
    bp = _t5_bucket_np(t + ii - jj)
    tab = rel_bias.T.reshape(-1)
    return pl.pallas_call(
        _bias_kernel,
        out_shape=jax.ShapeDtypeStruct((2, N_PAIRS, 2, t, 2 * t), F32),
        grid=(2, N_PAIRS, 2),
        in_specs=[pl.BlockSpec(memory_space=pltpu.SMEM),
                  pl.BlockSpec((t, t), lambda g, p, h: (0, 0)),
                  pl.BlockSpec((t, t), lambda g, p, h: (0, 0))],
        out_specs=pl.BlockSpec((1, 1, 2, t, t), lambda g, p, h: (g, p, 0, 0, h)),
        compiler_params=pltpu.CompilerParams(
            dimension_semantics=("arbitrary", "arbitrary", "arbitrary")),
        name="bias_tiles",
    )(tab, jnp.asarray(bd), jnp.asarray(bp))


def _proj_kernel(x_ref, g_ref, sh_ref, sc_ref, w_ref, o_ref):
    x = x_ref[0]
    h = _rms(x, g_ref[...]) * (1.0 + sc_ref[0]) + sh_ref[0]
    hb = h.astype(BF16)
    n_total = w_ref.shape[1]
    for n0 in range(0, n_total, 512):
        o_ref[0, :, n0:n0 + 512] = jnp.dot(
            hb, w_ref[:, n0:n0 + 512], preferred_element_type=F32).astype(BF16)


def _proj(x, g_mix, mod, w_in_bf):
    b, s, d = x.shape
    n = w_in_bf.shape[1]
    tm = min(PROJ_TM, s)
    return pl.pallas_call(
        _proj_kernel,
        out_shape=jax.ShapeDtypeStruct((b, s, n), BF16),
        grid=(b, s // tm),
        in_specs=[pl.BlockSpec((1, tm, d), lambda bi, i: (bi, i, 0)),
                  pl.BlockSpec((1, d), lambda bi, i: (0, 0)),
                  pl.BlockSpec((1, 1, d), lambda bi, i: (bi * 6 + 0, 0, 0)),
                  pl.BlockSpec((1, 1, d), lambda bi, i: (bi * 6 + 1, 0, 0)),
                  pl.BlockSpec((d, n), lambda bi, i: (0, 0))],
        out_specs=pl.BlockSpec((1, tm, n), lambda bi, i: (bi, i, 0)),
        compiler_params=pltpu.CompilerParams(dimension_semantics=("arbitrary", "arbitrary"),
                                             vmem_limit_bytes=VMEM_LIMIT_BYTES),
        name="proj",
    )(x, g_mix, mod, mod, w_in_bf)


def _stack_q(q_ref, q2_ref, scale):
    t = ATT_T
    row = lax.broadcasted_iota(I32, (LANES, t), 0)
    for p in range(N_PAIRS):
        q = q_ref[0, 0, p * LANES:(p + 1) * LANES, :].astype(F32) * scale
        q2_ref[p, :, 0:t] = jnp.where(row < 64, q, 0.0).astype(BF16)
        q2_ref[p, :, t:2 * t] = jnp.where(row >= 64, q, 0.0).astype(BF16)


def _flash_init(m_ref, acc_ref):
    m_ref[...] = jnp.full(m_ref.shape, NEG, F32)
    acc_ref[...] = jnp.zeros(acc_ref.shape, F32)


def _score_slot(p, h, bank):
    return bank * (2 * N_PAIRS) + 2 * p + h


def _flash_scores(p, h, n, bank, q2_ref, k_ref, s_ref):
    t = ATT_T
    kt = k_ref[0, pl.ds(pl.multiple_of(n * t, t), t), p * LANES:(p + 1) * LANES]
    s_ref[_score_slot(p, h, bank)] = jnp.dot(kt, q2_ref[p, :, h * t:(h + 1) * t],
                                             preferred_element_type=F32)


def _flash_absorb(p, h, n, bank, tile_bias, row_bias, live, v_ref, s_ref, m_ref, acc_ref):
    t = ATT_T
    cs = slice(h * t, (h + 1) * t)
    vt = v_ref[0, n, p * V_ROWS:(p + 1) * V_ROWS, :]
    s = s_ref[_score_slot(p, h, bank)]
    if tile_bias is not None:
        s = s + tile_bias
    mx = jnp.max(s, axis=0, keepdims=True)
    if row_bias is not None:
        mx = mx + row_bias
    if live is not None:
        mx = jnp.where(live, mx, NEG)
    m_prev = m_ref[p, :, cs]
    m_new = jnp.maximum(m_prev, mx)
    alpha = jnp.exp(m_prev - m_new)
    shift = m_new if row_bias is None else m_new - row_bias
    if live is not None:
        shift = jnp.where(live, shift, -NEG)
    pt = jnp.exp(s - shift)
    acc_ref[p, :, cs] = alpha * acc_ref[p, :, cs] + jnp.dot(vt, pt.astype(BF16),
                                                            preferred_element_type=F32)
    m_ref[p, :, cs] = m_new


def _far_bias(tab_ref, head):
    return tab_ref[head * N_BUCKETS + (N_BUCKETS - 1)]


def _flash_sweep(i, far_args, prev_args, own_args, refs):
    q2_ref, k_ref, v_ref, s_ref, m_ref, acc_ref = refs
    chains = [(p, h) for p in range(N_PAIRS) for h in range(2)]

    def scores_tile(n, bank):
        for p, h in chains:
            _flash_scores(p, h, n, bank, q2_ref, k_ref, s_ref)

    def step(n, bank, args, with_next):
        for p, h in chains:
            if with_next:
                _flash_scores(p, h, n + 1, 1 - bank, q2_ref, k_ref, s_ref)
            _flash_absorb(p, h, n, bank, *args(p, h), v_ref, s_ref, m_ref, acc_ref)

    def far_step(n, bank):
        step(n, bank, lambda p, h: far_args(p, h, n), True)

    n_far = jnp.maximum(i - 1, 0)
    lead = jnp.bitwise_and(n_far, 1)

    @pl.when(jnp.bitwise_and(i, 1) == 0)
    def _():
        scores_tile(0, 0)

    @pl.when(jnp.bitwise_and(i, 1) == 1)
    def _():
        scores_tile(0, 1)

    @pl.when(lead == 1)
    def _():
        far_step(0, 0)

    def pair_body(j, carry):
        n = lead + 2 * j
        far_step(n, 1)
        far_step(n + 1, 0)
        return carry

    lax.fori_loop(0, jnp.right_shift(n_far, 1), pair_body, 0)

    @pl.when(i >= 1)
    def _():
        step(i - 1, 1, prev_args, True)

    step(i, 0, own_args, False)


def _diff_kernel(tab_ref, q_ref, k_ref, v_ref, d_ref, lamv_ref, g_ref, o_ref,
                 q2_ref, s_ref, m_ref, acc_ref):
    t = ATT_T
    i = pl.program_id(1)
    _stack_q(q_ref, q2_ref, DIFF_HEAD_DIM ** -0.5)
    _flash_init(m_ref, acc_ref)
    refs = (q2_ref, k_ref, v_ref, s_ref, m_ref, acc_ref)
    _flash_sweep(i,
                 lambda p, h, n: (None, _far_bias(tab_ref, p), None),
                 lambda p, h: (d_ref[0, p, 1, :, h * t:(h + 1) * t], None, None),
                 lambda p, h: (d_ref[0, p, 0, :, h * t:(h + 1) * t], None, None),
                 refs)

    lv = lamv_ref[...]
    lam = (jnp.exp(jnp.sum(lv[0:1] * lv[1:2], axis=1, keepdims=True))
           - jnp.exp(jnp.sum(lv[2:3] * lv[3:4], axis=1, keepdims=True)) + LAM_INIT)
    for p in range(N_PAIRS):
        acc = acc_ref[p, 0:LANES, :]
        l = acc_ref[p, LANES:LANES + 1, :]
        o = acc[:, 0:t] / l[:, 0:t] - lam * (acc[:, t:2 * t] / l[:, t:2 * t])
        y = o * lax.rsqrt(jnp.mean(o * o, axis=0, keepdims=True) + RMS_EPS) * g_ref[...]
        o_ref[0, 0, p * LANES:(p + 1) * LANES, :] = (y * (1.0 - LAM_INIT)).astype(BF16)


def _moba_kernel(tab_ref, q_ref, k_ref, v_ref, d_ref, o_ref,
                 q2_ref, s_ref, m_ref, acc_ref, kmean_ref, sel_ref):
    t = ATT_T
    i = pl.program_id(1)
    nb = k_ref.shape[1] // t
    nbp = kmean_ref.shape[1]

    @pl.when(i == 0)
    def _():
        kmean_ref[...] = jnp.zeros(kmean_ref.shape, F32)
        for p in range(N_PAIRS):
            for n in range(nb):
                blk = k_ref[0, n * t:(n + 1) * t, p * LANES:(p + 1) * LANES].astype(F32)
                kmean_ref[p, n:n + 1, :] = jnp.sum(blk, axis=0, keepdims=True) * (1.0 / t)

    _stack_q(q_ref, q2_ref, MOBA_HEAD_DIM ** -0.5)
    _flash_init(m_ref, acc_ref)

    blk_id = lax.broadcasted_iota(I32, (nbp, 2 * t), 0)
    blk_f = blk_id.astype(F32)
    for p in range(N_PAIRS):
        gs = jnp.dot(kmean_ref[p].astype(BF16), q2_ref[p], preferred_element_type=F32)
        g = jnp.where(blk_id < i, gs, -jnp.inf)
        sel = jnp.full((nbp, 2 * t), NEG, F32)
        for _ in range(MOBA_TOPK):
            mx = jnp.max(g, axis=0, keepdims=True)
            first = jnp.min(jnp.where(g == mx, blk_f, float(nbp)), axis=0, keepdims=True)
            hit = blk_f == first
            sel = jnp.where(hit, 0.0, sel)
            g = jnp.where(hit, -jnp.inf, g)
        sel_ref[p] = sel

    def live(p, h, n):
        return sel_ref[p, pl.ds(n, 1), h * t:(h + 1) * t] > -1.0

    refs = (q2_ref, k_ref, v_ref, s_ref, m_ref, acc_ref)
    _flash_sweep(i,
                 lambda p, h, n: (None, _far_bias(tab_ref, N_DIFF_HEADS + 2 * p + h),
                                  live(p, h, n)),
                 lambda p, h: (d_ref[0, p, 1, :, h * t:(h + 1) * t], None, live(p, h, i - 1)),
                 lambda p, h: (d_ref[0, p, 0, :, h * t:(h + 1) * t], None, None),
                 refs)

    row = lax.broadcasted_iota(I32, (LANES, t), 0)
    for p in range(N_PAIRS):
        acc = acc_ref[p, 0:LANES, :]
        l = acc_ref[p, LANES:LANES + 1, :]
        o = jnp.where(row < 64, acc[:, 0:t] / l[:, 0:t], acc[:, t:2 * t] / l[:, t:2 * t])
        o_ref[0, 0, p * LANES:(p + 1) * LANES, :] = o.astype(BF16)


def _to_kv_major(proj, col0, nq):
    b, s, _ = proj.shape
    w = N_PAIRS * LANES
    return jnp.swapaxes(proj[:, :, col0:col0 + w].reshape(b, nq, ATT_T, w), 2, 3)


def _values_kv_major(proj, col0, nq):
    b, s, _ = proj.shape
    v = proj[:, :, col0:col0 + N_PAIRS * LANES].reshape(b, nq, ATT_T, N_PAIRS, LANES)
    ones = jnp.ones((b, nq, ATT_T, N_PAIRS, V_ROWS - LANES), BF16)
    v = jnp.concatenate([v, ones], axis=-1)
    return jnp.transpose(v, (0, 1, 3, 4, 2)).reshape(b, nq, N_PAIRS * V_ROWS, ATT_T)


def _attention(kind, tab, proj, dtiles, extra, cols):
    b, s, _ = proj.shape
    t = ATT_T
    nq = s // t
    w = N_PAIRS * LANES
    qc, kc, vc = cols
    gidx = 0 if kind == "diff" else 1
    q_t = _to_kv_major(proj, qc, nq)
    v_t = _values_kv_major(proj, vc, nq)
    in_specs = [pl.BlockSpec(memory_space=pltpu.SMEM),
                pl.BlockSpec((1, 1, w, t), lambda bi, i: (bi, i, 0, 0)),
                pl.BlockSpec((1, s, w), lambda bi, i: (bi, 0, kc // w)),
                pl.BlockSpec((1, nq, N_PAIRS * V_ROWS, t), lambda bi, i: (bi, 0, 0, 0)),
                pl.BlockSpec((1, N_PAIRS, 2, t, 2 * t), lambda bi, i: (gidx, 0, 0, 0, 0))]
    scratch = [pltpu.VMEM((N_PAIRS, LANES, 2 * t), BF16),
               pltpu.VMEM((2 * 2 * N_PAIRS, t, t), F32),
               pltpu.VMEM((N_PAIRS, 1, 2 * t), F32),
               pltpu.VMEM((N_PAIRS, V_ROWS, 2 * t), F32)]
    if kind == "diff":
        lamv, subln_g = extra
        in_specs += [pl.BlockSpec(lamv.shape, lambda bi, i: (0, 0)),
                     pl.BlockSpec(subln_g.shape, lambda bi, i: (0, 0))]
        args = (tab, q_t, proj, v_t, dtiles, lamv, subln_g)
        body = _diff_kernel
    else:
        nbp = max(32, nq)
        scratch += [pltpu.VMEM((N_PAIRS, nbp, LANES), F32), pltpu.VMEM((N_PAIRS, nbp, 2 * t), F32)]
        args = (tab, q_t, proj, v_t, dtiles)
        body = _moba_kernel
    y_t = pl.pallas_call(
        body,
        out_shape=jax.ShapeDtypeStruct((b, nq, w, t), BF16),
        grid=(b, nq),
        in_specs=in_specs,
        out_specs=pl.BlockSpec((1, 1, w, t), lambda bi, i: (bi, i, 0, 0)),
        scratch_shapes=scratch,
        compiler_params=pltpu.CompilerParams(
            dimension_semantics=("arbitrary", "arbitrary"),
            vmem_limit_bytes=VMEM_LIMIT_BYTES),
        name=kind + "_attn",
    )(*args)
    return jnp.swapaxes(y_t, 2, 3).reshape(b, s, w)


def _merge_kernel(x_ref, ya_ref, yb_ref, gla_ref, glb_ref, wa_ref, wb_ref, wo_ref,
                  gt_ref, shf_ref, scf_ref, gf_ref, wr_ref, br_ref,
                  x1_ref, h2_ref, ti_ref, tw_ref):
    a = jnp.dot(ya_ref[0], wa_ref[...], preferred_element_type=F32)
    b = jnp.dot(yb_ref[0], wb_ref[...], preferred_element_type=F32)
    merged = (jax.nn.sigmoid(gla_ref[0].astype(F32)) * a
              + jax.nn.sigmoid(glb_ref[0].astype(F32)) * b)
    mo = jnp.dot(merged.astype(BF16), wo_ref[...], preferred_element_type=F32)
    x1 = x_ref[0] + gt_ref[0] * mo
    x1_ref[0] = x1
    h = _rms(x1, gf_ref[...]) * (1.0 + scf_ref[0]) + shf_ref[0]
    _store_token_tiles(h2_ref.at[0], h)
    logits = jnp.dot(h.astype(BF16), wr_ref[...], preferred_element_type=F32) + br_ref[...]
    tm = logits.shape[0]
    lane = lax.broadcasted_iota(I32, (tm, LANES), 1)
    lane_f = lane.astype(F32)
    g = jnp.where(lane < N_EXPERTS, logits, -jnp.inf)
    vals, idxs = [], []
    for _ in range(TOP_K):
        mx = jnp.max(g, axis=1, keepdims=True)
        first = jnp.min(jnp.where(g == mx, lane_f, float(LANES)), axis=1, keepdims=True)
        vals.append(mx)
        idxs.append(first)
        g = jnp.where(lane_f == first, -jnp.inf, g)
    es = [jnp.exp(v - vals[0]) for v in vals]
    den = es[0] + es[1] + es[2] + es[3]
    ti = jnp.zeros((tm, LANES), F32)
    tw = jnp.zeros((tm, LANES), F32)
    for k in range(TOP_K):
        ti = jnp.where(lane == k, idxs[k], ti)
        tw = jnp.where(lane == k, es[k] / den, tw)
    ti_ref[0] = ti.astype(I32)
    tw_ref[0] = tw


def _merge(x, proj, y_a, y_b, w_br_a, w_br_b, w_out, mod, g_ffn, w_router, b_router):
    b, s, d = x.shape
    tm = min(MERGE_TM, s)
    full = lambda shape: pl.BlockSpec(shape, lambda bi, i: (0,) * len(shape))
    tile = lambda w, c=0: pl.BlockSpec((1, tm, w), lambda bi, i: (bi, i, c))
    modspec = lambda j: pl.BlockSpec((1, 1, d), lambda bi, i: (bi * 6 + j, 0, 0))
    return pl.pallas_call(
        _merge_kernel,
        out_shape=(jax.ShapeDtypeStruct((b, s, d), F32),
                   jax.ShapeDtypeStruct((b, s * ROW_CHUNKS, LANES), F32),
                   jax.ShapeDtypeStruct((b, s, LANES), I32),
                   jax.ShapeDtypeStruct((b, s, LANES), F32)),
        grid=(b, s // tm),
        in_specs=[tile(d), tile(512), tile(512), tile(d, 3), tile(d, 4),
                  full(w_br_a.shape), full(w_br_b.shape), full(w_out.shape),
                  modspec(2), modspec(3), modspec(4), full(g_ffn.shape),
                  full(w_router.shape), full(b_router.shape)],
        out_specs=(tile(d), pl.BlockSpec((1, tm * ROW_CHUNKS, LANES), lambda bi, i: (bi, i, 0)),
                   tile(LANES), tile(LANES)),
        compiler_params=pltpu.CompilerParams(dimension_semantics=("arbitrary", "arbitrary"),
                                             vmem_limit_bytes=VMEM_LIMIT_BYTES),
        name="merge",
    )(x, y_a, y_b, proj, proj, w_br_a, w_br_b, w_out, mod, mod, mod, g_ffn,
      w_router, b_router)


def _moe_kernel(te_ref, nu_ref, tok_ref, h2_ref, wgu_ref, bgu_ref, wd_ref, bd_ref,
                ys_ref, xbuf, sems, wgu_bf, wd_bf, gu_ref):
    r = pl.program_id(0)
    n_used = nu_ref[0]
    rc = ROW_CHUNKS
    tm = xbuf.shape[1] // rc
    cur = r % 2
    prv = 1 - cur

    def start_row(j, priority):
        src = h2_ref.at[pl.ds(pl.multiple_of(tok_ref[j], rc), rc)]
        dst = xbuf.at[cur, pl.ds(pl.multiple_of(j * rc, rc), rc)]
        pltpu.make_async_copy(src, dst, sems.at[cur]).start(priority=priority)

    def wait_tile(s):
        pltpu.make_async_copy(h2_ref.at[pl.ds(0, tm * rc)], xbuf.at[s], sems.at[s]).wait()

    @pl.when(r == 0)
    def _():
        def body(j, carry):
            start_row(j, 0)
            return carry
        lax.fori_loop(0, tm, body, 0)

    busy = jnp.logical_and(r >= 1, r <= n_used)
    e_prev = te_ref[jnp.maximum(r - 1, 0)]
    fresh = jnp.logical_or(r == 1, e_prev != te_ref[jnp.maximum(r - 2, 0)])

    @pl.when(jnp.logical_and(busy, fresh))
    def _():
        wgu_bf[...] = wgu_ref[0].astype(BF16)
        wd_bf[...] = wd_ref[0].astype(BF16)

    @pl.when(busy)
    def _():
        wait_tile(prv)
        x = _load_token_tiles(xbuf.at[prv]).astype(BF16)
        fc = MOE_FF_CHUNK
        n_gu = 2 * D_FF // fc
        n_dn = D_FF // fc
        n_slots = n_gu + n_dn

        def start_rows(slot):
            for j in range(slot * tm // n_slots, (slot + 1) * tm // n_slots):
                start_row(j, j % 2)

        for c in range(n_gu):
            lo, hi = c * fc, (c + 1) * fc
            gu_ref[:, lo:hi] = (jnp.dot(x, wgu_bf[:, lo:hi], preferred_element_type=F32)
                                + bgu_ref[0][:, lo:hi])
            start_rows(c)
        y = None
        for c in range(n_dn):
            lo, hi = c * fc, (c + 1) * fc
            gate = jnp.minimum(gu_ref[:, lo:hi], SWIGLU_LIMIT)
            up = jnp.clip(gu_ref[:, D_FF + lo:D_FF + hi], -SWIGLU_LIMIT, SWIGLU_LIMIT)
            act = (up + 1.0) * gate * jax.nn.sigmoid(SWIGLU_ALPHA * gate)
            part = jnp.dot(act.astype(BF16), wd_bf[lo:hi, :], preferred_element_type=F32)
            y = part if y is None else y + part
            start_rows(n_gu + c)
        _store_token_tiles(ys_ref, y + bd_ref[0])

        @pl.when(r == n_used)
        def _():
            wait_tile(cur)


def _moe(h2_tiles, row_start, tile_expert, n_used, w_gate_up, b_gate_up, w_down, b_down):
    d = D_MODEL
    rc = ROW_CHUNKS
    tm = MOE_TM
    n_tiles = tile_expert.shape[0]
    e, _, f2 = w_gate_up.shape
    wsel = lambda r, te, nu: (te[jnp.clip(r - 1, 0, n_tiles - 1)], 0, 0)
    return pl.pallas_call(
        _moe_kernel,
        out_shape=jax.ShapeDtypeStruct((n_tiles * tm * rc, LANES), F32),
        grid_spec=pltpu.PrefetchScalarGridSpec(
            num_scalar_prefetch=2,
            grid=(n_tiles + 1,),
            in_specs=[pl.BlockSpec((tm,), lambda r, te, nu: (jnp.minimum(r, n_tiles - 1),),
                                   memory_space=pltpu.SMEM),
                      pl.BlockSpec(memory_space=pl.ANY),
                      pl.BlockSpec((1, d, f2), wsel),
                      pl.BlockSpec((1, 1, f2), wsel),
                      pl.BlockSpec((1, D_FF, d), wsel),
                      pl.BlockSpec((1, 1, d), wsel)],
            out_specs=pl.BlockSpec(
                (tm * rc, LANES), lambda r, te, nu: (jnp.clip(r - 1, 0, nu[0] - 1), 0)),
            scratch_shapes=[pltpu.VMEM((2, tm * rc, LANES), F32), pltpu.SemaphoreType.DMA((2,)),
                            pltpu.VMEM((d, f2), BF16), pltpu.VMEM((D_FF, d), BF16),
                            pltpu.VMEM((tm, f2), F32)]),
        compiler_params=pltpu.CompilerParams(dimension_semantics=("arbitrary",),
                                             vmem_limit_bytes=VMEM_LIMIT_BYTES),
        name="moe",
    )(tile_expert, n_used, row_start, h2_tiles, w_gate_up, b_gate_up, w_down, b_down)


def _combine_kernel(pos_ref, ys_ref, x1_ref, tw_ref, gt_ref, gf_ref, o_ref, gbuf, sem):
    tm = x1_ref.shape[0]
    rc = ROW_CHUNKS

    def issue(j, carry):
        for k in range(TOP_K):
            src = ys_ref.at[pl.ds(pl.multiple_of(pos_ref[j * TOP_K + k], rc), rc)]
            dst = gbuf.at[k, pl.ds(pl.multiple_of(j * rc, rc), rc)]
            pltpu.make_async_copy(src, dst, sem).start(priority=k % 2)
        return carry

    lax.fori_loop(0, tm, issue, 0)
    for k in range(TOP_K):
        pltpu.make_async_copy(ys_ref.at[pl.ds(0, tm * rc)], gbuf.at[k], sem).wait()

    tw = tw_ref[...]
    acc = tw[:, 0:1] * _load_token_tiles(gbuf.at[0])
    for k in range(1, TOP_K):
        acc = acc + tw[:, k:k + 1] * _load_token_tiles(gbuf.at[k])
    x2 = x1_ref[...] + gt_ref[0] * acc
    o_ref[...] = _rms(x2, gf_ref[...])


def _combine(ys, pos_flat, x1, tw, mod, g_final, seq):
    t, d = x1.shape
    tm = min(COMBINE_TM, t)
    return pl.pallas_call(
        _combine_kernel,
        out_shape=jax.ShapeDtypeStruct((t, d), F32),
        grid=(t // tm,),
        in_specs=[pl.BlockSpec((tm * TOP_K,), lambda s: (s,), memory_space=pltpu.SMEM),
                  pl.BlockSpec(memory_space=pl.ANY),
                  pl.BlockSpec((tm, d), lambda s: (s, 0)),
                  pl.BlockSpec((tm, LANES), lambda s: (s, 0)),
                  pl.BlockSpec((1, 1, d), lambda s: ((s * tm) // seq * 6 + 5, 0, 0)),
                  pl.BlockSpec((1, d), lambda s: (0, 0))],
        out_specs=pl.BlockSpec((tm, d), lambda s: (s, 0)),
        scratch_shapes=[pltpu.VMEM((TOP_K, tm * ROW_CHUNKS, LANES), F32),
                        pltpu.SemaphoreType.DMA],
        compiler_params=pltpu.CompilerParams(dimension_semantics=("arbitrary",),
                                             vmem_limit_bytes=VMEM_LIMIT_BYTES),
        name="combine",
    )(pos_flat, ys, x1, tw, mod, g_final)


def _routing_tables(top_idx, n_tiles):
    tm = MOE_TM
    e_flat = top_idx.reshape(-1)
    onehot = (e_flat[:, None] == jnp.arange(N_EXPERTS, dtype=I32)[None, :]).astype(I32)
    csum = jnp.cumsum(onehot, axis=0)
    counts = csum[-1]
    rank = jnp.sum(csum * onehot, axis=1) - 1
    tiles_per = (counts + tm - 1) // tm
    tile_end = jnp.cumsum(tiles_per)
    tile_start = tile_end - tiles_per
    pos = (jnp.sum(onehot * tile_start[None, :], axis=1) * tm + rank).astype(I32)
    n_used = tile_end[-1:].astype(I32)
    tile_ids = jnp.arange(n_tiles, dtype=I32)
    tile_expert = jnp.minimum(
        jnp.sum((tile_ids[:, None] >= tile_end[None, :]).astype(I32), axis=1),
        N_EXPERTS - 1).astype(I32)
    big = 1 << 17
    assert e_flat.shape[0] < big
    experts = jnp.arange(N_EXPERTS, dtype=I32)[:, None]
    slot = jnp.arange(tm, dtype=I32)[None, :]
    n_pad = (tiles_per * tm - counts)[:, None]
    key_pad = jnp.where(slot < n_pad, experts * big + counts[:, None] + slot,
                        jnp.iinfo(jnp.int32).max)
    keys = jnp.concatenate([e_flat * big + rank, key_pad.reshape(-1)]).astype(I32)
    toks = jnp.concatenate([jnp.arange(e_flat.shape[0], dtype=I32) // TOP_K,
                            jnp.zeros((N_EXPERTS * tm,), I32)])
    _, row_token = lax.sort((keys, toks), num_keys=1)
    return pos, tile_expert, n_used, row_token


def kernel(x, c, rel_bias, w_ada, b_ada, g_mix, w_in, lambda_q1, lambda_k1, lambda_q2,
           lambda_k2, subln_g, w_br_a, w_br_b, w_out, g_ffn, w_router, b_router,
           w_gate_up, b_gate_up, w_down, b_down, g_final):
    b, s, d = x.shape
    assert d == D_MODEL and s % ATT_T == 0 and MOBA_BLOCK == ATT_T
    t_tok = b * s

    c8 = jnp.zeros((8, d), F32).at[:b].set(c)
    ada = _ada(c8, w_ada[0], b_ada[0].reshape(1, -1))
    mod = ada[:b].reshape(b * 6, 1, d)

    dtiles = _bias_tiles(rel_bias)
    tab = rel_bias.T.reshape(-1)

    proj = _proj(x, g_mix[0].reshape(1, d), mod, w_in[0].astype(BF16))

    lamv = jnp.concatenate([lambda_q1, lambda_k1, lambda_q2, lambda_k2], axis=0)
    y_a = _attention("diff", tab, proj, dtiles, (lamv, subln_g[0].reshape(-1, 1)), (0, 512, 1024))
    y_b = _attention("moba", tab, proj, dtiles, None, (1536, 2048, 2560))

    wr = jnp.zeros((d, LANES), BF16).at[:, :N_EXPERTS].set(w_router[0].astype(BF16))
    br = jnp.zeros((1, LANES), F32).at[0, :N_EXPERTS].set(b_router[0])
    x1, h2, ti, tw = _merge(x, proj, y_a, y_b, w_br_a[0].astype(BF16), w_br_b[0].astype(BF16),
                            w_out[0].astype(BF16), mod, g_ffn[0].reshape(1, d), wr, br)

    n_tiles = (t_tok * TOP_K) // MOE_TM + N_EXPERTS
    top_idx = ti.reshape(t_tok, LANES)[:, :TOP_K]
    pos, tile_expert, n_used, row_token = _routing_tables(top_idx, n_tiles)

    ys = _moe(h2.reshape(t_tok * ROW_CHUNKS, LANES), row_token * ROW_CHUNKS, tile_expert, n_used,
              w_gate_up[0], b_gate_up[0].reshape(N_EXPERTS, 1, -1), w_down[0],
              b_down[0].reshape(N_EXPERTS, 1, -1))
    out = _combine(ys, pos * ROW_CHUNKS, x1.reshape(t_tok, d), tw.reshape(t_tok, LANES), mod,
                   g_final.reshape(1, d), s)
    return out.reshape(b, s, d)
```

```python
import functools
import math

import jax
import jax.numpy as jnp
import numpy as np
from jax import lax
from jax.experimental import pallas as pl
from jax.experimental.pallas import tpu as pltpu

F32 = jnp.float32
BF16 = jnp.bfloat16
I32 = jnp.int32

D_MODEL = 1024
N_DIFF_HEADS = 4
DIFF_HEAD_DIM = 64
N_MOBA_HEADS = 8
MOBA_HEAD_DIM = 64
MOBA_BLOCK = 256
MOBA_TOPK = 3
N_HEADS_TOTAL = N_DIFF_HEADS + N_MOBA_HEADS
N_BUCKETS = 32
MAX_DISTANCE = 128
N_EXPERTS = 32
TOP_K = 4
D_FF = D_MODEL
SWIGLU_LIMIT = 7.0
SWIGLU_ALPHA = 1.702
RMS_EPS = 1e-5
LAM_INIT = 0.8 - 0.6 * math.exp(-0.3 * 0)
IN_WIDTH = 5120

LANES = 128
VMEM_LIMIT_BYTES = 56 * 1024 * 1024

ATT_T = 256
N_PAIRS = 4
V_ROWS = LANES + 16
PROJ_TM = 512
MERGE_TM = 512
MOE_TM = 256
MOE_FF_CHUNK = 256
COMBINE_TM = 256
NEG = -1e30


def _t5_bucket_np(dist):
    n = np.maximum(dist, 0)
    max_exact = N_BUCKETS // 2
    nf = np.maximum(n, max_exact).astype(np.float32)
    large = max_exact + (np.log(nf / max_exact) / math.log(MAX_DISTANCE / max_exact)
                         * (N_BUCKETS - max_exact)).astype(np.int32)
    large = np.minimum(large, N_BUCKETS - 1)
    return np.where(n < max_exact, n, large).astype(np.int32)


def _rms(x, g):
    return x * lax.rsqrt(jnp.mean(x * x, axis=-1, keepdims=True) + RMS_EPS) * g


ROW_CHUNKS = D_MODEL // LANES


def _store_token_tiles(ref, x):
    n = x.shape[0]
    for c in range(ROW_CHUNKS):
        ref[pl.ds(c, n, stride=ROW_CHUNKS), :] = x[:, c * LANES:(c + 1) * LANES]


def _load_token_tiles(ref):
    n = ref.shape[0] // ROW_CHUNKS
    return jnp.concatenate([ref[pl.ds(c, n, stride=ROW_CHUNKS), :] for c in range(ROW_CHUNKS)],
                           axis=1)


def _ada_kernel(c_ref, w_ref, b_ref, o_ref):
    o_ref[...] = jnp.dot(c_ref[...].astype(BF16), w_ref[...].astype(BF16),
                         preferred_element_type=F32) + b_ref[...]


def _ada(c8, w, b):
    d, n = w.shape
    tn = 1536
    return pl.pallas_call(
        _ada_kernel,
        out_shape=jax.ShapeDtypeStruct((8, n), F32),
        grid=(n // tn,),
        in_specs=[pl.BlockSpec((8, d), lambda j: (0, 0)),
                  pl.BlockSpec((d, tn), lambda j: (0, j)),
                  pl.BlockSpec((1, tn), lambda j: (0, j))],
        out_specs=pl.BlockSpec((8, tn), lambda j: (0, j)),
        compiler_params=pltpu.CompilerParams(dimension_semantics=("arbitrary",),
                                             vmem_limit_bytes=VMEM_LIMIT_BYTES),
        name="ada",
    )(c8, w, b)


def _bias_kernel(tab_ref, bd_ref, bp_ref, o_ref):
    g = pl.program_id(0)
    p = pl.program_id(1)
    t = pl.program_id(2)
    head = jnp.where(g == 0, p, N_DIFF_HEADS + 2 * p + t)
    bd = bd_ref[...]
    bp = bp_ref[...]
    d0 = jnp.where(bd < 0, NEG, 0.0).astype(F32)
    d1 = jnp.zeros(bp.shape, F32)
    for b in range(N_BUCKETS):
        val = tab_ref[head * N_BUCKETS + b]
        d0 = jnp.where(bd == b, val, d0)
        d1 = jnp.where(bp == b, val, d1)
    o_ref[0, 0, 0] = d0
    o_ref[0, 0, 1] = d1


def _bias_tiles(rel_bias):
    t = ATT_T
    jj = np.arange(t)[:, None]
    ii = np.arange(t)[None, :]
    bd = np.where(ii >= jj, _t5_bucket_np(ii - jj), -1).astype(np.int32)
    bp = _t5_bucket_np(t + ii - jj)
    tab = rel_bias.T.reshape(-1)
    return pl.pallas_call(
        _bias_kernel,
        out_shape=jax.ShapeDtypeStruct((2, N_PAIRS, 2, t, 2 * t), F32),
        grid=(2, N_PAIRS, 2),
        in_specs=[pl.BlockSpec(memory_space=pltpu.SMEM),
                  pl.BlockSpec((t, t), lambda g, p, h: (0, 0)),
                  pl.BlockSpec((t, t), lambda g, p, h: (0, 0))],
        out_specs=pl.BlockSpec((1, 1, 2, t, t), lambda g, p, h: (g, p, 0, 0, h)),
        compiler_params=pltpu.CompilerParams(
            dimension_semantics=("arbitrary", "arbitrary", "arbitrary")),
        name="bias_tiles",
    )(tab, jnp.asarray(bd), jnp.asarray(bp))


def _proj_kernel(x_ref, g_ref, sh_ref, sc_ref, w_ref, o_ref):
    x = x_ref[0]
    h = _rms(x, g_ref[...]) * (1.0 + sc_ref[0]) + sh_ref[0]
    hb = h.astype(BF16)
    n_total = w_ref.shape[1]
    for n0 in range(0, n_total, 512):
        o_ref[0, :, n0:n0 + 512] = jnp.dot(
            hb, w_ref[:, n0:n0 + 512], preferred_element_type=F32).astype(BF16)


def _proj(x, g_mix, mod, w_in_bf):
    b, s, d = x.shape
    n = w_in_bf.shape[1]
    tm = min(PROJ_TM, s)
    return pl.pallas_call(
        _proj_kernel,
        out_shape=jax.ShapeDtypeStruct((b, s, n), BF16),
        grid=(b, s // tm),
        in_specs=[pl.BlockSpec((1, tm, d), lambda bi, i: (bi, i, 0)),
                  pl.BlockSpec((1, d), lambda bi, i: (0, 0)),
                  pl.BlockSpec((1, 1, d), lambda bi, i: (bi * 6 + 0, 0, 0)),
                  pl.BlockSpec((1, 1, d), lambda bi, i: (bi * 6 + 1, 0, 0)),
                  pl.BlockSpec((d, n), lambda bi, i: (0, 0))],
        out_specs=pl.BlockSpec((1, tm, n), lambda bi, i: (bi, i, 0)),
        compiler_params=pltpu.CompilerParams(dimension_semantics=("arbitrary", "arbitrary"),
                                             vmem_limit_bytes=VMEM_LIMIT_BYTES),
        name="proj",
    )(x, g_mix, mod, mod, w_in_bf)


def _stack_q(q_ref, q2_ref, scale):
    t = ATT_T
    row = lax.broadcasted_iota(I32, (LANES, t), 0)
    for p in range(N_PAIRS):
        q = q_ref[0, 0, p * LANES:(p + 1) * LANES, :].astype(F32) * scale
        q2_ref[p, :, 0:t] = jnp.where(row < 64, q, 0.0).astype(BF16)
        q2_ref[p, :, t:2 * t] = jnp.where(row >= 64, q, 0.0).astype(BF16)


def _flash_init(m_ref, acc_ref):
    m_ref[...] = jnp.full(m_ref.shape, NEG, F32)
    acc_ref[...] = jnp.zeros(acc_ref.shape, F32)


def _score_slot(p, h, bank):
    return bank * (2 * N_PAIRS) + 2 * p + h


def _flash_scores(p, h, n, bank, q2_ref, k_ref, s_ref):
    t = ATT_T
    kt = k_ref[0, pl.ds(pl.multiple_of(n * t, t), t), p * LANES:(p + 1) * LANES]
    s_ref[_score_slot(p, h, bank)] = jnp.dot(kt, q2_ref[p, :, h * t:(h + 1) * t],
                                             preferred_element_type=F32)


def _flash_absorb(p, h, n, bank, tile_bias, row_bias, live, v_ref, s_ref, m_ref, acc_ref):
    t = ATT_T
    cs = slice(h * t, (h + 1) * t)
    vt = v_ref[0, n, p * V_ROWS:(p + 1) * V_ROWS, :]
    s = s_ref[_score_slot(p, h, bank)]
    if tile_bias is not None:
        s = s + tile_bias
    mx = jnp.max(s, axis=0, keepdims=True)
    if row_bias is not None:
        mx = mx + row_bias
    if live is not None:
        mx = jnp.where(live, mx, NEG)
    m_prev = m_ref[p, :, cs]
    m_new = jnp.maximum(m_prev, mx)
    alpha = jnp.exp(m_prev - m_new)
    shift = m_new if row_bias is None else m_new - row_bias
    if live is not None:
        shift = jnp.where(live, shift, -NEG)
    pt = jnp.exp(s - shift)
    acc_ref[p, :, cs] = alpha * acc_ref[p, :, cs] + jnp.dot(vt, pt.astype(BF16),
                                                            preferred_element_type=F32)
    m_ref[p, :, cs] = m_new


def _far_bias(tab_ref, head):
    return tab_ref[head * N_BUCKETS + (N_BUCKETS - 1)]


def _flash_sweep(i, far_args, prev_args, own_args, refs):
    q2_ref, k_ref, v_ref, s_ref, m_ref, acc_ref = refs
    chains = [(p, h) for p in range(N_PAIRS) for h in range(2)]

    def scores_tile(n, bank):
        for p, h in chains:
            _flash_scores(p, h, n, bank, q2_ref, k_ref, s_ref)

    def step(n, bank, args, with_next):
        for p, h in chains:
            if with_next:
                _flash_scores(p, h, n + 1, 1 - bank, q2_ref, k_ref, s_ref)
            _flash_absorb(p, h, n, bank, *args(p, h), v_ref, s_ref, m_ref, acc_ref)

    def far_step(n, bank):
        step(n, bank, lambda p, h: far_args(p, h, n), True)

    n_far = jnp.maximum(i - 1, 0)
    lead = jnp.bitwise_and(n_far, 1)

    @pl.when(jnp.bitwise_and(i, 1) == 0)
    def _():
        scores_tile(0, 0)

    @pl.when(jnp.bitwise_and(i, 1) == 1)
    def _():
        scores_tile(0, 1)

    @pl.when(lead == 1)
    def _():
        far_step(0, 0)

    def pair_body(j, carry):
        n = lead + 2 * j
        far_step(n, 1)
        far_step(n + 1, 0)
        return carry

    lax.fori_loop(0, jnp.right_shift(n_far, 1), pair_body, 0)

    @pl.when(i >= 1)
    def _():
        step(i - 1, 1, prev_args, True)

    step(i, 0, own_args, False)


def _diff_kernel(tab_ref, q_ref, k_ref, v_ref, d_ref, lamv_ref, g_ref, o_ref,
                 q2_ref, s_ref, m_ref, acc_ref):
    t = ATT_T
    i = pl.program_id(1)
    _stack_q(q_ref, q2_ref, DIFF_HEAD_DIM ** -0.5)
    _flash_init(m_ref, acc_ref)
    refs = (q2_ref, k_ref, v_ref, s_ref, m_ref, acc_ref)
    _flash_sweep(i,
                 lambda p, h, n: (None, _far_bias(tab_ref, p), None),
                 lambda p, h: (d_ref[0, p, 1, :, h * t:(h + 1) * t], None, None),
                 lambda p, h: (d_ref[0, p, 0, :, h * t:(h + 1) * t], None, None),
                 refs)

    lv = lamv_ref[...]
    lam = (jnp.exp(jnp.sum(lv[0:1] * lv[1:2], axis=1, keepdims=True))
           - jnp.exp(jnp.sum(lv[2:3] * lv[3:4], axis=1, keepdims=True)) + LAM_INIT)
    for p in range(N_PAIRS):
        acc = acc_ref[p, 0:LANES, :]
        l = acc_ref[p, LANES:LANES + 1, :]
        o = acc[:, 0:t] / l[:, 0:t] - lam * (acc[:, t:2 * t] / l[:, t:2 * t])
        y = o * lax.rsqrt(jnp.mean(o * o, axis=0, keepdims=True) + RMS_EPS) * g_ref[...]
        o_ref[0, 0, p * LANES:(p + 1) * LANES, :] = (y * (1.0 - LAM_INIT)).astype(BF16)


def _moba_kernel(tab_ref, q_ref, k_ref, v_ref, d_ref, o_ref,
                 q2_ref, s_ref, m_ref, acc_ref, kmean_ref, sel_ref):
    t = ATT_T
    i = pl.program_id(1)
    nb = k_ref.shape[1] // t
    nbp = kmean_ref.shape[1]

    @pl.when(i == 0)
    def _():
        kmean_ref[...] = jnp.zeros(kmean_ref.shape, F32)
        for p in range(N_PAIRS):
            for n in range(nb):
                blk = k_ref[0, n * t:(n + 1) * t, p * LANES:(p + 1) * LANES].astype(F32)
                kmean_ref[p, n:n + 1, :] = jnp.sum(blk, axis=0, keepdims=True) * (1.0 / t)

    _stack_q(q_ref, q2_ref, MOBA_HEAD_DIM ** -0.5)
    _flash_init(m_ref, acc_ref)

    blk_id = lax.broadcasted_iota(I32, (nbp, 2 * t), 0)
    blk_f = blk_id.astype(F32)
    for p in range(N_PAIRS):
        gs = jnp.dot(kmean_ref[p].astype(BF16), q2_ref[p], preferred_element_type=F32)
        g = jnp.where(blk_id < i, gs, -jnp.inf)
        sel = jnp.full((nbp, 2 * t), NEG, F32)
        for _ in range(MOBA_TOPK):
            mx = jnp.max(g, axis=0, keepdims=True)
            first = jnp.min(jnp.where(g == mx, blk_f, float(nbp)), axis=0, keepdims=True)
            hit = blk_f == first
            sel = jnp.where(hit, 0.0, sel)
            g = jnp.where(hit, -jnp.inf, g)
        sel_ref[p] = sel

    def live(p, h, n):
        return sel_ref[p, pl.ds(n, 1), h * t:(h + 1) * t] > -1.0

    refs = (q2_ref, k_ref, v_ref, s_ref, m_ref, acc_ref)
    _flash_sweep(i,
                 lambda p, h, n: (None, _far_bias(tab_ref, N_DIFF_HEADS + 2 * p + h),
                                  live(p, h, n)),
                 lambda p, h: (d_ref[0, p, 1, :, h * t:(h + 1) * t], None, live(p, h, i - 1)),
                 lambda p, h: (d_ref[0, p, 0, :, h * t:(h + 1) * t], None, None),
                 refs)

    row = lax.broadcasted_iota(I32, (LANES, t), 0)
    for p in range(N_PAIRS):
        acc = acc_ref[p, 0:LANES, :]
        l = acc_ref[p, LANES:LANES + 1, :]
        o = jnp.where(row < 64, acc[:, 0:t] / l[:, 0:t], acc[:, t:2 * t] / l[:, t:2 * t])
        o_ref[0, 0, p * LANES:(p + 1) * LANES, :] = o.astype(BF16)


def _to_kv_major(proj, col0, nq):
    b, s, _ = proj.shape
    w = N_PAIRS * LANES
    return jnp.swapaxes(proj[:, :, col0:col0 + w].reshape(b, nq, ATT_T, w), 2, 3)


def _values_kv_major(proj, col0, nq):
    b, s, _ = proj.shape
    v = proj[:, :, col0:col0 + N_PAIRS * LANES].reshape(b, nq, ATT_T, N_PAIRS, LANES)
    ones = jnp.ones((b, nq, ATT_T, N_PAIRS, V_ROWS - LANES), BF16)
    v = jnp.concatenate([v, ones], axis=-1)
    return jnp.transpose(v, (0, 1, 3, 4, 2)).reshape(b, nq, N_PAIRS * V_ROWS, ATT_T)


def _attention(kind, tab, proj, dtiles, extra, cols):
    b, s, _ = proj.shape
    t = ATT_T
    nq = s // t
    w = N_PAIRS * LANES
    qc, kc, vc = cols
    gidx = 0 if kind == "diff" else 1
    q_t = _to_kv_major(proj, qc, nq)
    v_t = _values_kv_major(proj, vc, nq)
    in_specs = [pl.BlockSpec(memory_space=pltpu.SMEM),
                pl.BlockSpec((1, 1, w, t), lambda bi, i: (bi, i, 0, 0)),
                pl.BlockSpec((1, s, w), lambda bi, i: (bi, 0, kc // w)),
                pl.BlockSpec((1, nq, N_PAIRS * V_ROWS, t), lambda bi, i: (bi, 0, 0, 0)),
                pl.BlockSpec((1, N_PAIRS, 2, t, 2 * t), lambda bi, i: (gidx, 0, 0, 0, 0))]
    scratch = [pltpu.VMEM((N_PAIRS, LANES, 2 * t), BF16),
               pltpu.VMEM((2 * 2 * N_PAIRS, t, t), F32),
               pltpu.VMEM((N_PAIRS, 1, 2 * t), F32),
               pltpu.VMEM((N_PAIRS, V_ROWS, 2 * t), F32)]
    if kind == "diff":
        lamv, subln_g = extra
        in_specs += [pl.BlockSpec(lamv.shape, lambda bi, i: (0, 0)),
                     pl.BlockSpec(subln_g.shape, lambda bi, i: (0, 0))]
        args = (tab, q_t, proj, v_t, dtiles, lamv, subln_g)
        body = _diff_kernel
    else:
        nbp = max(32, nq)
        scratch += [pltpu.VMEM((N_PAIRS, nbp, LANES), F32), pltpu.VMEM((N_PAIRS, nbp, 2 * t), F32)]
        args = (tab, q_t, proj, v_t, dtiles)
        body = _moba_kernel
    y_t = pl.pallas_call(
        body,
        out_shape=jax.ShapeDtypeStruct((b, nq, w, t), BF16),
        grid=(b, nq),
        in_specs=in_specs,
        out_specs=pl.BlockSpec((1, 1, w, t), lambda bi, i: (bi, i, 0, 0)),
        scratch_shapes=scratch,
        compiler_params=pltpu.CompilerParams(
            dimension_semantics=("arbitrary", "arbitrary"),
            vmem_limit_bytes=VMEM_LIMIT_BYTES),
        name=kind + "_attn",
    )(*args)
    return jnp.swapaxes(y_t, 2, 3).reshape(b, s, w)


def _merge_kernel(x_ref, ya_ref, yb_ref, gla_ref, glb_ref, wa_ref, wb_ref, wo_ref,
                  gt_ref, shf_ref, scf_ref, gf_ref, wr_ref, br_ref,
                  x1_ref, h2_ref, ti_ref, tw_ref):
    a = jnp.dot(ya_ref[0], wa_ref[...], preferred_element_type=F32)
    b = jnp.dot(yb_ref[0], wb_ref[...], preferred_element_type=F32)
    merged = (jax.nn.sigmoid(gla_ref[0].astype(F32)) * a
              + jax.nn.sigmoid(glb_ref[0].astype(F32)) * b)
    mo = jnp.dot(merged.astype(BF16), wo_ref[...], preferred_element_type=F32)
    x1 = x_ref[0] + gt_ref[0] * mo
    x1_ref[0] = x1
    h = _rms(x1, gf_ref[...]) * (1.0 + scf_ref[0]) + shf_ref[0]
    _store_token_tiles(h2_ref.at[0], h)
    logits = jnp.dot(h.astype(BF16), wr_ref[...], preferred_element_type=F32) + br_ref[...]
    tm = logits.shape[0]
    lane = lax.broadcasted_iota(I32, (tm, LANES), 1)
    lane_f = lane.astype(F32)
    g = jnp.where(lane < N_EXPERTS, logits, -jnp.inf)
    vals, idxs = [], []
    for _ in range(TOP_K):
        mx = jnp.max(g, axis=1, keepdims=True)
        first = jnp.min(jnp.where(g == mx, lane_f, float(LANES)), axis=1, keepdims=True)
        vals.append(mx)
        idxs.append(first)
        g = jnp.where(lane_f == first, -jnp.inf, g)
    es = [jnp.exp(v - vals[0]) for v in vals]
    den = es[0] + es[1] + es[2] + es[3]
    ti = jnp.zeros((tm, LANES), F32)
    tw = jnp.zeros((tm, LANES), F32)
    for k in range(TOP_K):
        ti = jnp.where(lane == k, idxs[k], ti)
        tw = jnp.where(lane == k, es[k] / den, tw)
    ti_ref[0] = ti.astype(I32)
    tw_ref[0] = tw


def _merge(x, proj, y_a, y_b, w_br_a, w_br_b, w_out, mod, g_ffn, w_router, b_router):
    b, s, d = x.shape
    tm = min(MERGE_TM, s)
    full = lambda shape: pl.BlockSpec(shape, lambda bi, i: (0,) * len(shape))
    tile = lambda w, c=0: pl.BlockSpec((1, tm, w), lambda bi, i: (bi, i, c))
    modspec = lambda j: pl.BlockSpec((1, 1, d), lambda bi, i: (bi * 6 + j, 0, 0))
    return pl.pallas_call(
        _merge_kernel,
        out_shape=(jax.ShapeDtypeStruct((b, s, d), F32),
                   jax.ShapeDtypeStruct((b, s * ROW_CHUNKS, LANES), F32),
                   jax.ShapeDtypeStruct((b, s, LANES), I32),
                   jax.ShapeDtypeStruct((b, s, LANES), F32)),
        grid=(b, s // tm),
        in_specs=[tile(d), tile(512), tile(512), tile(d, 3), tile(d, 4),
                  full(w_br_a.shape), full(w_br_b.shape), full(w_out.shape),
                  modspec(2), modspec(3), modspec(4), full(g_ffn.shape),
                  full(w_router.shape), full(b_router.shape)],
        out_specs=(tile(d), pl.BlockSpec((1, tm * ROW_CHUNKS, LANES), lambda bi, i: (bi, i, 0)),
                   tile(LANES), tile(LANES)),
        compiler_params=pltpu.CompilerParams(dimension_semantics=("arbitrary", "arbitrary"),
                                             vmem_limit_bytes=VMEM_LIMIT_BYTES),
        name="merge",
    )(x, y_a, y_b, proj, proj, w_br_a, w_br_b, w_out, mod, mod, mod, g_ffn,
      w_router, b_router)


def _moe_kernel(te_ref, nu_ref, tok_ref, h2_ref, wgu_ref, bgu_ref, wd_ref, bd_ref,
                ys_ref, xbuf, sems, wgu_bf, wd_bf, gu_ref):
    r = pl.program_id(0)
    n_used = nu_ref[0]
    rc = ROW_CHUNKS
    tm = xbuf.shape[1] // rc
    cur = r % 2
    prv = 1 - cur

    def start_row(j, priority):
        src = h2_ref.at[pl.ds(pl.multiple_of(tok_ref[j], rc), rc)]
        dst = xbuf.at[cur, pl.ds(pl.multiple_of(j * rc, rc), rc)]
        pltpu.make_async_copy(src, dst, sems.at[cur]).start(priority=priority)

    def wait_tile(s):
        pltpu.make_async_copy(h2_ref.at[pl.ds(0, tm * rc)], xbuf.at[s], sems.at[s]).wait()

    @pl.when(r == 0)
    def _():
        def body(j, carry):
            start_row(j, 0)
            return carry
        lax.fori_loop(0, tm, body, 0)

    busy = jnp.logical_and(r >= 1, r <= n_used)
    e_prev = te_ref[jnp.maximum(r - 1, 0)]
    fresh = jnp.logical_or(r == 1, e_prev != te_ref[jnp.maximum(r - 2, 0)])

    @pl.when(jnp.logical_and(busy, fresh))
    def _():
        wgu_bf[...] = wgu_ref[0].astype(BF16)
        wd_bf[...] = wd_ref[0].astype(BF16)

    @pl.when(busy)
    def _():
        wait_tile(prv)
        x = _load_token_tiles(xbuf.at[prv]).astype(BF16)
        fc = MOE_FF_CHUNK
        n_gu = 2 * D_FF // fc
        n_dn = D_FF // fc
        n_slots = n_gu + n_dn

        def start_rows(slot):
            for j in range(slot * tm // n_slots, (slot + 1) * tm // n_slots):
                start_row(j, 1)

        for c in range(n_gu):
            lo, hi = c * fc, (c + 1) * fc
            gu_ref[:, lo:hi] = (jnp.dot(x, wgu_bf[:, lo:hi], preferred_element_type=F32)
                                + bgu_ref[0][:, lo:hi])
            start_rows(c)
        y = None
        for c in range(n_dn):
            lo, hi = c * fc, (c + 1) * fc
            gate = jnp.minimum(gu_ref[:, lo:hi], SWIGLU_LIMIT)
            up = jnp.clip(gu_ref[:, D_FF + lo:D_FF + hi], -SWIGLU_LIMIT, SWIGLU_LIMIT)
            act = (up + 1.0) * gate * jax.nn.sigmoid(SWIGLU_ALPHA * gate)
            part = jnp.dot(act.astype(BF16), wd_bf[lo:hi, :], preferred_element_type=F32)
            y = part if y is None else y + part
            start_rows(n_gu + c)
        _store_token_tiles(ys_ref, y + bd_ref[0])

        @pl.when(r == n_used)
        def _():
            wait_tile(cur)


def _moe(h2_tiles, row_start, tile_expert, n_used, w_gate_up, b_gate_up, w_down, b_down):
    d = D_MODEL
    rc = ROW_CHUNKS
    tm = MOE_TM
    n_tiles = tile_expert.shape[0]
    e, _, f2 = w_gate_up.shape
    wsel = lambda r, te, nu: (te[jnp.clip(r - 1, 0, n_tiles - 1)], 0, 0)
    return pl.pallas_call(
        _moe_kernel,
        out_shape=jax.ShapeDtypeStruct((n_tiles * tm * rc, LANES), F32),
        grid_spec=pltpu.PrefetchScalarGridSpec(
            num_scalar_prefetch=2,
            grid=(n_tiles + 1,),
            in_specs=[pl.BlockSpec((tm,), lambda r, te, nu: (jnp.minimum(r, n_tiles - 1),),
                                   memory_space=pltpu.SMEM),
                      pl.BlockSpec(memory_space=pl.ANY),
                      pl.BlockSpec((1, d, f2), wsel),
                      pl.BlockSpec((1, 1, f2), wsel),
                      pl.BlockSpec((1, D_FF, d), wsel),
                      pl.BlockSpec((1, 1, d), wsel)],
            out_specs=pl.BlockSpec(
                (tm * rc, LANES), lambda r, te, nu: (jnp.clip(r - 1, 0, nu[0] - 1), 0)),
            scratch_shapes=[pltpu.VMEM((2, tm * rc, LANES), F32), pltpu.SemaphoreType.DMA((2,)),
                            pltpu.VMEM((d, f2), BF16), pltpu.VMEM((D_FF, d), BF16),
                            pltpu.VMEM((tm, f2), F32)]),
        compiler_params=pltpu.CompilerParams(dimension_semantics=("arbitrary",),
                                             vmem_limit_bytes=VMEM_LIMIT_BYTES),
        name="moe",
    )(tile_expert, n_used, row_start, h2_tiles, w_gate_up, b_gate_up, w_down, b_down)


def _combine_kernel(pos_ref, ys_ref, x1_ref, tw_ref, gt_ref, gf_ref, o_ref, gbuf, sem):
    tm = x1_ref.shape[0]
    rc = ROW_CHUNKS

    def issue(j, carry):
        for k in range(TOP_K):
            src = ys_ref.at[pl.ds(pl.multiple_of(pos_ref[j * TOP_K + k], rc), rc)]
            dst = gbuf.at[k, pl.ds(pl.multiple_of(j * rc, rc), rc)]
            pltpu.make_async_copy(src, dst, sem).start(priority=k % 2)
        return carry

    lax.fori_loop(0, tm, issue, 0)
    for k in range(TOP_K):
        pltpu.make_async_copy(ys_ref.at[pl.ds(0, tm * rc)], gbuf.at[k], sem).wait()

    tw = tw_ref[...]
    acc = tw[:, 0:1] * _load_token_tiles(gbuf.at[0])
    for k in range(1, TOP_K):
        acc = acc + tw[:, k:k + 1] * _load_token_tiles(gbuf.at[k])
    x2 = x1_ref[...] + gt_ref[0] * acc
    o_ref[...] = _rms(x2, gf_ref[...])


def _combine(ys, pos_flat, x1, tw, mod, g_final, seq):
    t, d = x1.shape
    tm = min(COMBINE_TM, t)
    return pl.pallas_call(
        _combine_kernel,
        out_shape=jax.ShapeDtypeStruct((t, d), F32),
        grid=(t // tm,),
        in_specs=[pl.BlockSpec((tm * TOP_K,), lambda s: (s,), memory_space=pltpu.SMEM),
                  pl.BlockSpec(memory_space=pl.ANY),
                  pl.BlockSpec((tm, d), lambda s: (s, 0)),
                  pl.BlockSpec((tm, LANES), lambda s: (s, 0)),
                  pl.BlockSpec((1, 1, d), lambda s: ((s * tm) // seq * 6 + 5, 0, 0)),
                  pl.BlockSpec((1, d), lambda s: (0, 0))],
        out_specs=pl.BlockSpec((tm, d), lambda s: (s, 0)),
        scratch_shapes=[pltpu.VMEM((TOP_K, tm * ROW_CHUNKS, LANES), F32),
                        pltpu.SemaphoreType.DMA],
        compiler_params=pltpu.CompilerParams(dimension_semantics=("arbitrary",),
                                             vmem_limit_bytes=VMEM_LIMIT_BYTES),
        name="combine",
    )(pos_flat, ys, x1, tw, mod, g_final)


def _routing_tables(top_idx, n_tiles):
    tm = MOE_TM
    e_flat = top_idx.reshape(-1)
    onehot = (e_flat[:, None] == jnp.arange(N_EXPERTS, dtype=I32)[None, :]).astype(I32)
    csum = jnp.cumsum(onehot, axis=0)
    counts = csum[-1]
    rank = jnp.sum(csum * onehot, axis=1) - 1
    tiles_per = (counts + tm - 1) // tm
    tile_end = jnp.cumsum(tiles_per)
    tile_start = tile_end - tiles_per
    pos = (jnp.sum(onehot * tile_start[None, :], axis=1) * tm + rank).astype(I32)
    n_used = tile_end[-1:].astype(I32)
    tile_ids = jnp.arange(n_tiles, dtype=I32)
    tile_expert = jnp.minimum(
        jnp.sum((tile_ids[:, None] >= tile_end[None, :]).astype(I32), axis=1),
        N_EXPERTS - 1).astype(I32)
    big = 1 << 17
    assert e_flat.shape[0] < big
    experts = jnp.arange(N_EXPERTS, dtype=I32)[:, None]
    slot = jnp.arange(tm, dtype=I32)[None, :]
    n_pad = (tiles_per * tm - counts)[:, None]
    key_pad = jnp.where(slot < n_pad, experts * big + counts[:, None] + slot,
                        jnp.iinfo(jnp.int32).max)
    keys = jnp.concatenate([e_flat * big + rank, key_pad.reshape(-1)]).astype(I32)
    toks = jnp.concatenate([jnp.arange(e_flat.shape[0], dtype=I32) // TOP_K,
                            jnp.zeros((N_EXPERTS * tm,), I32)])
    _, row_token = lax.sort((keys, toks), num_keys=1)
    return pos, tile_expert, n_used, row_token


def kernel(x, c, rel_bias, w_ada, b_ada, g_mix, w_in, lambda_q1, lambda_k1, lambda_q2,
           lambda_k2, subln_g, w_br_a, w_br_b, w_out, g_ffn, w_router, b_router,
           w_gate_up, b_gate_up, w_down, b_down, g_final):
    b, s, d = x.shape
    assert d == D_MODEL and s % ATT_T == 0 and MOBA_BLOCK == ATT_T
    t_tok = b * s

    c8 = jnp.zeros((8, d), F32).at[:b].set(c)
    ada = _ada(c8, w_ada[0], b_ada[0].reshape(1, -1))
    mod = ada[:b].reshape(b * 6, 1, d)

    dtiles = _bias_tiles(rel_bias)
    tab = rel_bias.T.reshape(-1)

    proj = _proj(x, g_mix[0].reshape(1, d), mod, w_in[0].astype(BF16))

    lamv = jnp.concatenate([lambda_q1, lambda_k1, lambda_q2, lambda_k2], axis=0)
    y_a = _attention("diff", tab, proj, dtiles, (lamv, subln_g[0].reshape(-1, 1)), (0, 512, 1024))
    y_b = _attention("moba", tab, proj, dtiles, None, (1536, 2048, 2560))

    wr = jnp.zeros((d, LANES), BF16).at[:, :N_EXPERTS].set(w_router[0].astype(BF16))
    br = jnp.zeros((1, LANES), F32).at[0, :N_EXPERTS].set(b_router[0])
    x1, h2, ti, tw = _merge(x, proj, y_a, y_b, w_br_a[0].astype(BF16), w_br_b[0].astype(BF16),
                            w_out[0].astype(BF16), mod, g_ffn[0].reshape(1, d), wr, br)

    n_tiles = (t_tok * TOP_K) // MOE_TM + N_EXPERTS
    top_idx = ti.reshape(t_tok, LANES)[:, :TOP_K]
    pos, tile_expert, n_used, row_token = _routing_tables(top_idx, n_tiles)

    ys = _moe(h2.reshape(t_tok * ROW_CHUNKS, LANES), row_token * ROW_CHUNKS, tile_expert, n_used,
              w_gate_up[0], b_gate_up[0].reshape(N_EXPERTS, 1, -1), w_down[0],
              b_down[0].reshape(N_EXPERTS, 1, -1))
    out = _combine(ys, pos * ROW_CHUNKS, x1.reshape(t_tok, d), tw.reshape(t_tok, LANES), mod,
                   g_final.reshape(1, d), s)
    return out.reshape(b, s, d)
```

```python
import functools
import math

import jax
import jax.numpy as jnp
import numpy as np
from jax import lax
from jax.experimental import pallas as pl
from jax.experimental.pallas import tpu as pltpu

F32 = jnp.float32
BF16 = jnp.bfloat16
I32 = jnp.int32

D_MODEL = 1024
N_DIFF_HEADS = 4
DIFF_HEAD_DIM = 64
N_MOBA_HEADS = 8
MOBA_HEAD_DIM = 64
MOBA_BLOCK = 256
MOBA_TOPK = 3
N_HEADS_TOTAL = N_DIFF_HEADS + N_MOBA_HEADS
N_BUCKETS = 32
MAX_DISTANCE = 128
N_EXPERTS = 32
TOP_K = 4
D_FF = D_MODEL
SWIGLU_LIMIT = 7.0
SWIGLU_ALPHA = 1.702
RMS_EPS = 1e-5
LAM_INIT = 0.8 - 0.6 * math.exp(-0.3 * 0)
IN_WIDTH = 5120

LANES = 128
VMEM_LIMIT_BYTES = 56 * 1024 * 1024

ATT_T = 256
N_PAIRS = 4
V_ROWS = LANES + 16
PROJ_TM = 512
MERGE_TM = 512
MOE_TM = 256
MOE_FF_CHUNK = 256
DISPATCH_TM = 512
COMBINE_TM = 256
NEG = -1e30


def _t5_bucket_np(dist):
    n = np.maximum(dist, 0)
    max_exact = N_BUCKETS // 2
    nf = np.maximum(n, max_exact).astype(np.float32)
    large = max_exact + (np.log(nf / max_exact) / math.log(MAX_DISTANCE / max_exact)
                         * (N_BUCKETS - max_exact)).astype(np.int32)
    large = np.minimum(large, N_BUCKETS - 1)
    return np.where(n < max_exact, n, large).astype(np.int32)


def _rms(x, g):
    return x * lax.rsqrt(jnp.mean(x * x, axis=-1, keepdims=True) + RMS_EPS) * g


ROW_CHUNKS = D_MODEL // LANES


def _store_token_tiles(ref, x):
    n = x.shape[0]
    for c in range(ROW_CHUNKS):
        ref[pl.ds(c, n, stride=ROW_CHUNKS), :] = x[:, c * LANES:(c + 1) * LANES]


def _load_token_tiles(ref):
    n = ref.shape[0] // ROW_CHUNKS
    return jnp.concatenate([ref[pl.ds(c, n, stride=ROW_CHUNKS), :] for c in range(ROW_CHUNKS)],
                           axis=1)


def _ada_kernel(c_ref, w_ref, b_ref, o_ref):
    o_ref[...] = jnp.dot(c_ref[...].astype(BF16), w_ref[...].astype(BF16),
                         preferred_element_type=F32) + b_ref[...]


def _ada(c8, w, b):
    d, n = w.shape
    tn = 1536
    return pl.pallas_call(
        _ada_kernel,
        out_shape=jax.ShapeDtypeStruct((8, n), F32),
        grid=(n // tn,),
        in_specs=[pl.BlockSpec((8, d), lambda j: (0, 0)),
                  pl.BlockSpec((d, tn), lambda j: (0, j)),
                  pl.BlockSpec((1, tn), lambda j: (0, j))],
        out_specs=pl.BlockSpec((8, tn), lambda j: (0, j)),
        compiler_params=pltpu.CompilerParams(dimension_semantics=("arbitrary",),
                                             vmem_limit_bytes=VMEM_LIMIT_BYTES),
        name="ada",
    )(c8, w, b)


def _bias_kernel(tab_ref, bd_ref, bp_ref, o_ref):
    g = pl.program_id(0)
    p = pl.program_id(1)
    t = pl.program_id(2)
    head = jnp.where(g == 0, p, N_DIFF_HEADS + 2 * p + t)
    bd = bd_ref[...]
    bp = bp_ref[...]
    d0 = jnp.where(bd < 0, NEG, 0.0).astype(F32)
    d1 = jnp.zeros(bp.shape, F32)
    for b in range(N_BUCKETS):
        val = tab_ref[head * N_BUCKETS + b]
        d0 = jnp.where(bd == b, val, d0)
        d1 = jnp.where(bp == b, val, d1)
    o_ref[0, 0, 0] = d0
    o_ref[0, 0, 1] = d1


def _bias_tiles(rel_bias):
    t = ATT_T
    jj = np.arange(t)[:, None]
    ii = np.arange(t)[None, :]
    bd = np.where(ii >= jj, _t5_bucket_np(ii - jj), -1).astype(np.int32)
    bp = _t5_bucket_np(t + ii - jj)
    tab = rel_bias.T.reshape(-1)
    return pl.pallas_call(
        _bias_kernel,
        out_shape=jax.ShapeDtypeStruct((2, N_PAIRS, 2, t, 2 * t), F32),
        grid=(2, N_PAIRS, 2),
        in_specs=[pl.BlockSpec(memory_space=pltpu.SMEM),
                  pl.BlockSpec((t, t), lambda g, p, h: (0, 0)),
                  pl.BlockSpec((t, t), lambda g, p, h: (0, 0))],
        out_specs=pl.BlockSpec((1, 1, 2, t, t), lambda g, p, h: (g, p, 0, 0, h)),
        compiler_params=pltpu.CompilerParams(
            dimension_semantics=("arbitrary", "arbitrary", "arbitrary")),
        name="bias_tiles",
    )(tab, jnp.asarray(bd), jnp.asarray(bp))


def _proj_kernel(x_ref, g_ref, sh_ref, sc_ref, w_ref, o_ref):
    x = x_ref[0]
    h = _rms(x, g_ref[...]) * (1.0 + sc_ref[0]) + sh_ref[0]
    hb = h.astype(BF16)
    n_total = w_ref.shape[1]
    for n0 in range(0, n_total, 512):
        o_ref[0, :, n0:n0 + 512] = jnp.dot(
            hb, w_ref[:, n0:n0 + 512], preferred_element_type=F32).astype(BF16)


def _proj(x, g_mix, mod, w_in_bf):
    b, s, d = x.shape
    n = w_in_bf.shape[1]
    tm = min(PROJ_TM, s)
    return pl.pallas_call(
        _proj_kernel,
        out_shape=jax.ShapeDtypeStruct((b, s, n), BF16),
        grid=(b, s // tm),
        in_specs=[pl.BlockSpec((1, tm, d), lambda bi, i: (bi, i, 0)),
                  pl.BlockSpec((1, d), lambda bi, i: (0, 0)),
                  pl.BlockSpec((1, 1, d), lambda bi, i: (bi * 6 + 0, 0, 0)),
                  pl.BlockSpec((1, 1, d), lambda bi, i: (bi * 6 + 1, 0, 0)),
                  pl.BlockSpec((d, n), lambda bi, i: (0, 0))],
        out_specs=pl.BlockSpec((1, tm, n), lambda bi, i: (bi, i, 0)),
        compiler_params=pltpu.CompilerParams(dimension_semantics=("arbitrary", "arbitrary"),
                                             vmem_limit_bytes=VMEM_LIMIT_BYTES),
        name="proj",
    )(x, g_mix, mod, mod, w_in_bf)


def _stack_q(q_ref, q2_ref, scale):
    t = ATT_T
    row = lax.broadcasted_iota(I32, (LANES, t), 0)
    for p in range(N_PAIRS):
        q = q_ref[0, 0, p * LANES:(p + 1) * LANES, :].astype(F32) * scale
        q2_ref[p, :, 0:t] = jnp.where(row < 64, q, 0.0).astype(BF16)
        q2_ref[p, :, t:2 * t] = jnp.where(row >= 64, q, 0.0).astype(BF16)


def _flash_init(m_ref, acc_ref):
    m_ref[...] = jnp.full(m_ref.shape, NEG, F32)
    acc_ref[...] = jnp.zeros(acc_ref.shape, F32)


def _score_slot(p, h, bank):
    return bank * (2 * N_PAIRS) + 2 * p + h


def _flash_scores(p, h, n, bank, q2_ref, k_ref, s_ref):
    t = ATT_T
    kt = k_ref[0, pl.ds(pl.multiple_of(n * t, t), t), p * LANES:(p + 1) * LANES]
    s_ref[_score_slot(p, h, bank)] = jnp.dot(kt, q2_ref[p, :, h * t:(h + 1) * t],
                                             preferred_element_type=F32)


def _flash_absorb(p, h, n, bank, tile_bias, row_bias, live, v_ref, s_ref, m_ref, acc_ref):
    t = ATT_T
    cs = slice(h * t, (h + 1) * t)
    vt = v_ref[0, n, p * V_ROWS:(p + 1) * V_ROWS, :]
    s = s_ref[_score_slot(p, h, bank)]
    if tile_bias is not None:
        s = s + tile_bias
    mx = jnp.max(s, axis=0, keepdims=True)
    if row_bias is not None:
        mx = mx + row_bias
    if live is not None:
        mx = jnp.where(live, mx, NEG)
    m_prev = m_ref[p, :, cs]
    m_new = jnp.maximum(m_prev, mx)
    alpha = jnp.exp(m_prev - m_new)
    shift = m_new if row_bias is None else m_new - row_bias
    if live is not None:
        shift = jnp.where(live, shift, -NEG)
    pt = jnp.exp(s - shift)
    acc_ref[p, :, cs] = alpha * acc_ref[p, :, cs] + jnp.dot(vt, pt.astype(BF16),
                                                            preferred_element_type=F32)
    m_ref[p, :, cs] = m_new


def _far_bias(tab_ref, head):
    return tab_ref[head * N_BUCKETS + (N_BUCKETS - 1)]


def _flash_sweep(i, far_args, prev_args, own_args, refs):
    q2_ref, k_ref, v_ref, s_ref, m_ref, acc_ref = refs
    chains = [(p, h) for p in range(N_PAIRS) for h in range(2)]

    def scores_tile(n, bank):
        for p, h in chains:
            _flash_scores(p, h, n, bank, q2_ref, k_ref, s_ref)

    def step(n, bank, args, with_next):
        for p, h in chains:
            if with_next:
                _flash_scores(p, h, n + 1, 1 - bank, q2_ref, k_ref, s_ref)
            _flash_absorb(p, h, n, bank, *args(p, h), v_ref, s_ref, m_ref, acc_ref)

    def far_step(n, bank):
        step(n, bank, lambda p, h: far_args(p, h, n), True)

    n_far = jnp.maximum(i - 1, 0)
    lead = jnp.bitwise_and(n_far, 1)

    @pl.when(jnp.bitwise_and(i, 1) == 0)
    def _():
        scores_tile(0, 0)

    @pl.when(jnp.bitwise_and(i, 1) == 1)
    def _():
        scores_tile(0, 1)

    @pl.when(lead == 1)
    def _():
        far_step(0, 0)

    def pair_body(j, carry):
        n = lead + 2 * j
        far_step(n, 1)
        far_step(n + 1, 0)
        return carry

    lax.fori_loop(0, jnp.right_shift(n_far, 1), pair_body, 0)

    @pl.when(i >= 1)
    def _():
        step(i - 1, 1, prev_args, True)

    step(i, 0, own_args, False)


def _diff_kernel(tab_ref, q_ref, k_ref, v_ref, d_ref, lamv_ref, g_ref, o_ref,
                 q2_ref, s_ref, m_ref, acc_ref):
    t = ATT_T
    i = pl.program_id(1)
    _stack_q(q_ref, q2_ref, DIFF_HEAD_DIM ** -0.5)
    _flash_init(m_ref, acc_ref)
    refs = (q2_ref, k_ref, v_ref, s_ref, m_ref, acc_ref)
    _flash_sweep(i,
                 lambda p, h, n: (None, _far_bias(tab_ref, p), None),
                 lambda p, h: (d_ref[0, p, 1, :, h * t:(h + 1) * t], None, None),
                 lambda p, h: (d_ref[0, p, 0, :, h * t:(h + 1) * t], None, None),
                 refs)

    lv = lamv_ref[...]
    lam = (jnp.exp(jnp.sum(lv[0:1] * lv[1:2], axis=1, keepdims=True))
           - jnp.exp(jnp.sum(lv[2:3] * lv[3:4], axis=1, keepdims=True)) + LAM_INIT)
    for p in range(N_PAIRS):
        acc = acc_ref[p, 0:LANES, :]
        l = acc_ref[p, LANES:LANES + 1, :]
        o = acc[:, 0:t] / l[:, 0:t] - lam * (acc[:, t:2 * t] / l[:, t:2 * t])
        y = o * lax.rsqrt(jnp.mean(o * o, axis=0, keepdims=True) + RMS_EPS) * g_ref[...]
        o_ref[0, 0, p * LANES:(p + 1) * LANES, :] = (y * (1.0 - LAM_INIT)).astype(BF16)


def _moba_kernel(tab_ref, q_ref, k_ref, v_ref, d_ref, o_ref,
                 q2_ref, s_ref, m_ref, acc_ref, kmean_ref, sel_ref):
    t = ATT_T
    i = pl.program_id(1)
    nb = k_ref.shape[1] // t
    nbp = kmean_ref.shape[1]

    @pl.when(i == 0)
    def _():
        kmean_ref[...] = jnp.zeros(kmean_ref.shape, F32)
        for p in range(N_PAIRS):
            for n in range(nb):
                blk = k_ref[0, n * t:(n + 1) * t, p * LANES:(p + 1) * LANES].astype(F32)
                kmean_ref[p, n:n + 1, :] = jnp.sum(blk, axis=0, keepdims=True) * (1.0 / t)

    _stack_q(q_ref, q2_ref, MOBA_HEAD_DIM ** -0.5)
    _flash_init(m_ref, acc_ref)

    blk_id = lax.broadcasted_iota(I32, (nbp, 2 * t), 0)
    blk_f = blk_id.astype(F32)
    for p in range(N_PAIRS):
        gs = jnp.dot(kmean_ref[p].astype(BF16), q2_ref[p], preferred_element_type=F32)
        g = jnp.where(blk_id < i, gs, -jnp.inf)
        sel = jnp.full((nbp, 2 * t), NEG, F32)
        for _ in range(MOBA_TOPK):
            mx = jnp.max(g, axis=0, keepdims=True)
            first = jnp.min(jnp.where(g == mx, blk_f, float(nbp)), axis=0, keepdims=True)
            hit = blk_f == first
            sel = jnp.where(hit, 0.0, sel)
            g = jnp.where(hit, -jnp.inf, g)
        sel_ref[p] = sel

    def live(p, h, n):
        return sel_ref[p, pl.ds(n, 1), h * t:(h + 1) * t] > -1.0

    refs = (q2_ref, k_ref, v_ref, s_ref, m_ref, acc_ref)
    _flash_sweep(i,
                 lambda p, h, n: (None, _far_bias(tab_ref, N_DIFF_HEADS + 2 * p + h),
                                  live(p, h, n)),
                 lambda p, h: (d_ref[0, p, 1, :, h * t:(h + 1) * t], None, live(p, h, i - 1)),
                 lambda p, h: (d_ref[0, p, 0, :, h * t:(h + 1) * t], None, None),
                 refs)

    row = lax.broadcasted_iota(I32, (LANES, t), 0)
    for p in range(N_PAIRS):
        acc = acc_ref[p, 0:LANES, :]
        l = acc_ref[p, LANES:LANES + 1, :]
        o = jnp.where(row < 64, acc[:, 0:t] / l[:, 0:t], acc[:, t:2 * t] / l[:, t:2 * t])
        o_ref[0, 0, p * LANES:(p + 1) * LANES, :] = o.astype(BF16)


def _to_kv_major(proj, col0, nq):
    b, s, _ = proj.shape
    w = N_PAIRS * LANES
    return jnp.swapaxes(proj[:, :, col0:col0 + w].reshape(b, nq, ATT_T, w), 2, 3)


def _values_kv_major(proj, col0, nq):
    b, s, _ = proj.shape
    v = proj[:, :, col0:col0 + N_PAIRS * LANES].reshape(b, nq, ATT_T, N_PAIRS, LANES)
    ones = jnp.ones((b, nq, ATT_T, N_PAIRS, V_ROWS - LANES), BF16)
    v = jnp.concatenate([v, ones], axis=-1)
    return jnp.transpose(v, (0, 1, 3, 4, 2)).reshape(b, nq, N_PAIRS * V_ROWS, ATT_T)


def _attention(kind, tab, proj, dtiles, extra, cols):
    b, s, _ = proj.shape
    t = ATT_T
    nq = s // t
    w = N_PAIRS * LANES
    qc, kc, vc = cols
    gidx = 0 if kind == "diff" else 1
    q_t = _to_kv_major(proj, qc, nq)
    v_t = _values_kv_major(proj, vc, nq)
    in_specs = [pl.BlockSpec(memory_space=pltpu.SMEM),
                pl.BlockSpec((1, 1, w, t), lambda bi, i: (bi, i, 0, 0)),
                pl.BlockSpec((1, s, w), lambda bi, i: (bi, 0, kc // w)),
                pl.BlockSpec((1, nq, N_PAIRS * V_ROWS, t), lambda bi, i: (bi, 0, 0, 0)),
                pl.BlockSpec((1, N_PAIRS, 2, t, 2 * t), lambda bi, i: (gidx, 0, 0, 0, 0))]
    scratch = [pltpu.VMEM((N_PAIRS, LANES, 2 * t), BF16),
               pltpu.VMEM((2 * 2 * N_PAIRS, t, t), F32),
               pltpu.VMEM((N_PAIRS, 1, 2 * t), F32),
               pltpu.VMEM((N_PAIRS, V_ROWS, 2 * t), F32)]
    if kind == "diff":
        lamv, subln_g = extra
        in_specs += [pl.BlockSpec(lamv.shape, lambda bi, i: (0, 0)),
                     pl.BlockSpec(subln_g.shape, lambda bi, i: (0, 0))]
        args = (tab, q_t, proj, v_t, dtiles, lamv, subln_g)
        body = _diff_kernel
    else:
        nbp = max(32, nq)
        scratch += [pltpu.VMEM((N_PAIRS, nbp, LANES), F32), pltpu.VMEM((N_PAIRS, nbp, 2 * t), F32)]
        args = (tab, q_t, proj, v_t, dtiles)
        body = _moba_kernel
    y_t = pl.pallas_call(
        body,
        out_shape=jax.ShapeDtypeStruct((b, nq, w, t), BF16),
        grid=(b, nq),
        in_specs=in_specs,
        out_specs=pl.BlockSpec((1, 1, w, t), lambda bi, i: (bi, i, 0, 0)),
        scratch_shapes=scratch,
        compiler_params=pltpu.CompilerParams(
            dimension_semantics=("arbitrary", "arbitrary"),
            vmem_limit_bytes=VMEM_LIMIT_BYTES),
        name=kind + "_attn",
    )(*args)
    return jnp.swapaxes(y_t, 2, 3).reshape(b, s, w)


def _merge_kernel(x_ref, ya_ref, yb_ref, gla_ref, glb_ref, wa_ref, wb_ref, wo_ref,
                  gt_ref, shf_ref, scf_ref, gf_ref, wr_ref, br_ref,
                  x1_ref, h2_ref, ti_ref, tw_ref):
    a = jnp.dot(ya_ref[0], wa_ref[...], preferred_element_type=F32)
    b = jnp.dot(yb_ref[0], wb_ref[...], preferred_element_type=F32)
    merged = (jax.nn.sigmoid(gla_ref[0].astype(F32)) * a
              + jax.nn.sigmoid(glb_ref[0].astype(F32)) * b)
    mo = jnp.dot(merged.astype(BF16), wo_ref[...], preferred_element_type=F32)
    x1 = x_ref[0] + gt_ref[0] * mo
    x1_ref[0] = x1
    h = _rms(x1, gf_ref[...]) * (1.0 + scf_ref[0]) + shf_ref[0]
    _store_token_tiles(h2_ref.at[0], h)
    logits = jnp.dot(h.astype(BF16), wr_ref[...], preferred_element_type=F32) + br_ref[...]
    tm = logits.shape[0]
    lane = lax.broadcasted_iota(I32, (tm, LANES), 1)
    lane_f = lane.astype(F32)
    g = jnp.where(lane < N_EXPERTS, logits, -jnp.inf)
    vals, idxs = [], []
    for _ in range(TOP_K):
        mx = jnp.max(g, axis=1, keepdims=True)
        first = jnp.min(jnp.where(g == mx, lane_f, float(LANES)), axis=1, keepdims=True)
        vals.append(mx)
        idxs.append(first)
        g = jnp.where(lane_f == first, -jnp.inf, g)
    es = [jnp.exp(v - vals[0]) for v in vals]
    den = es[0] + es[1] + es[2] + es[3]
    ti = jnp.zeros((tm, LANES), F32)
    tw = jnp.zeros((tm, LANES), F32)
    for k in range(TOP_K):
        ti = jnp.where(lane == k, idxs[k], ti)
        tw = jnp.where(lane == k, es[k] / den, tw)
    ti_ref[0] = ti.astype(I32)
    tw_ref[0] = tw


def _merge(x, proj, y_a, y_b, w_br_a, w_br_b, w_out, mod, g_ffn, w_router, b_router):
    b, s, d = x.shape
    tm = min(MERGE_TM, s)
    full = lambda shape: pl.BlockSpec(shape, lambda bi, i: (0,) * len(shape))
    tile = lambda w, c=0: pl.BlockSpec((1, tm, w), lambda bi, i: (bi, i, c))
    modspec = lambda j: pl.BlockSpec((1, 1, d), lambda bi, i: (bi * 6 + j, 0, 0))
    return pl.pallas_call(
        _merge_kernel,
        out_shape=(jax.ShapeDtypeStruct((b, s, d), F32),
                   jax.ShapeDtypeStruct((b, s * ROW_CHUNKS, LANES), F32),
                   jax.ShapeDtypeStruct((b, s, LANES), I32),
                   jax.ShapeDtypeStruct((b, s, LANES), F32)),
        grid=(b, s // tm),
        in_specs=[tile(d), tile(512), tile(512), tile(d, 3), tile(d, 4),
                  full(w_br_a.shape), full(w_br_b.shape), full(w_out.shape),
                  modspec(2), modspec(3), modspec(4), full(g_ffn.shape),
                  full(w_router.shape), full(b_router.shape)],
        out_specs=(tile(d), pl.BlockSpec((1, tm * ROW_CHUNKS, LANES), lambda bi, i: (bi, i, 0)),
                   tile(LANES), tile(LANES)),
        compiler_params=pltpu.CompilerParams(dimension_semantics=("arbitrary", "arbitrary"),
                                             vmem_limit_bytes=VMEM_LIMIT_BYTES),
        name="merge",
    )(x, y_a, y_b, proj, proj, w_br_a, w_br_b, w_out, mod, mod, mod, g_ffn,
      w_router, b_router)


def _dispatch_kernel(padtile_ref, pos_ref, h2_ref, xs_ref, zeros_ref, zsem, sem):
    step = pl.program_id(0)
    rc = ROW_CHUNKS
    n_tok = h2_ref.shape[0] // rc
    pad_rows = zeros_ref.shape[0]

    def zero_fill(e):
        start = pl.multiple_of(padtile_ref[e] * pad_rows, pad_rows)
        return pltpu.make_async_copy(zeros_ref, xs_ref.at[pl.ds(start, pad_rows)], zsem)

    @pl.when(step == 0)
    def _():
        zeros_ref[...] = jnp.zeros(zeros_ref.shape, F32)
        for e in range(N_EXPERTS):
            @pl.when(padtile_ref[e] >= 0)
            def _():
                zero_fill(e).start()
        for e in range(N_EXPERTS):
            @pl.when(padtile_ref[e] >= 0)
            def _():
                zero_fill(e).wait()

    def issue(j, carry):
        src = h2_ref.at[pl.ds(pl.multiple_of(j * rc, rc), rc)]
        for k in range(TOP_K):
            dst = xs_ref.at[pl.ds(pl.multiple_of(pos_ref[j * TOP_K + k], rc), rc)]
            pltpu.make_async_copy(src, dst, sem).start(priority=k % 2)
        return carry

    lax.fori_loop(0, n_tok, issue, 0)
    for k in range(TOP_K):
        pltpu.make_async_copy(h2_ref, xs_ref.at[pl.ds(0, n_tok * rc)], sem).wait()


def _dispatch(h2_tiles, pos_start, padtile, n_rows):
    rc = ROW_CHUNKS
    t_tok = h2_tiles.shape[0] // rc
    tm = min(DISPATCH_TM, t_tok)
    return pl.pallas_call(
        _dispatch_kernel,
        out_shape=jax.ShapeDtypeStruct((n_rows * rc, LANES), F32),
        grid_spec=pltpu.PrefetchScalarGridSpec(
            num_scalar_prefetch=1,
            grid=(t_tok // tm,),
            in_specs=[pl.BlockSpec((tm * TOP_K,), lambda s, pt: (s,), memory_space=pltpu.SMEM),
                      pl.BlockSpec((tm * rc, LANES), lambda s, pt: (s, 0))],
            out_specs=pl.BlockSpec(memory_space=pl.ANY),
            scratch_shapes=[pltpu.VMEM((MOE_TM * rc, LANES), F32),
                            pltpu.SemaphoreType.DMA, pltpu.SemaphoreType.DMA]),
        compiler_params=pltpu.CompilerParams(dimension_semantics=("arbitrary",),
                                             vmem_limit_bytes=VMEM_LIMIT_BYTES),
        name="dispatch",
    )(padtile, pos_start, h2_tiles)


def _moe_kernel(te_ref, nu_ref, xs_ref, wgu_ref, bgu_ref, wd_ref, bd_ref,
                ys_ref, wgu_bf, wd_bf, gu_ref):
    r = pl.program_id(0)
    busy = r < nu_ref[0]
    fresh = jnp.logical_or(r == 0, te_ref[r] != te_ref[jnp.maximum(r - 1, 0)])

    @pl.when(jnp.logical_and(busy, fresh))
    def _():
        wgu_bf[...] = wgu_ref[0].astype(BF16)
        wd_bf[...] = wd_ref[0].astype(BF16)

    @pl.when(busy)
    def _():
        x = _load_token_tiles(xs_ref).astype(BF16)
        fc = MOE_FF_CHUNK
        for c in range(2 * D_FF // fc):
            lo, hi = c * fc, (c + 1) * fc
            gu_ref[:, lo:hi] = (jnp.dot(x, wgu_bf[:, lo:hi], preferred_element_type=F32)
                                + bgu_ref[0][:, lo:hi])
        y = None
        for c in range(D_FF // fc):
            lo, hi = c * fc, (c + 1) * fc
            gate = jnp.minimum(gu_ref[:, lo:hi], SWIGLU_LIMIT)
            up = jnp.clip(gu_ref[:, D_FF + lo:D_FF + hi], -SWIGLU_LIMIT, SWIGLU_LIMIT)
            act = (up + 1.0) * gate * jax.nn.sigmoid(SWIGLU_ALPHA * gate)
            part = jnp.dot(act.astype(BF16), wd_bf[lo:hi, :], preferred_element_type=F32)
            y = part if y is None else y + part
        _store_token_tiles(ys_ref, y + bd_ref[0])


def _moe(xs_tiles, tile_expert, n_used, w_gate_up, b_gate_up, w_down, b_down):
    d = D_MODEL
    rc = ROW_CHUNKS
    tm = MOE_TM
    n_tiles = tile_expert.shape[0]
    e, _, f2 = w_gate_up.shape
    wsel = lambda r, te, nu: (te[r], 0, 0)
    rows = lambda r, te, nu: (jnp.minimum(r, nu[0] - 1), 0)
    return pl.pallas_call(
        _moe_kernel,
        out_shape=jax.ShapeDtypeStruct((n_tiles * tm * rc, LANES), F32),
        grid_spec=pltpu.PrefetchScalarGridSpec(
            num_scalar_prefetch=2,
            grid=(n_tiles,),
            in_specs=[pl.BlockSpec((tm * rc, LANES), rows),
                      pl.BlockSpec((1, d, f2), wsel),
                      pl.BlockSpec((1, 1, f2), wsel),
                      pl.BlockSpec((1, D_FF, d), wsel),
                      pl.BlockSpec((1, 1, d), wsel)],
            out_specs=pl.BlockSpec((tm * rc, LANES), rows),
            scratch_shapes=[pltpu.VMEM((d, f2), BF16), pltpu.VMEM((D_FF, d), BF16),
                            pltpu.VMEM((tm, f2), F32)]),
        compiler_params=pltpu.CompilerParams(dimension_semantics=("arbitrary",),
                                             vmem_limit_bytes=VMEM_LIMIT_BYTES),
        name="moe",
    )(tile_expert, n_used, xs_tiles, w_gate_up, b_gate_up, w_down, b_down)


def _combine_kernel(pos_ref, ys_ref, x1_ref, tw_ref, gt_ref, gf_ref, o_ref, gbuf, sem):
    tm = x1_ref.shape[0]
    rc = ROW_CHUNKS

    def issue(j, carry):
        for k in range(TOP_K):
            src = ys_ref.at[pl.ds(pl.multiple_of(pos_ref[j * TOP_K + k], rc), rc)]
            dst = gbuf.at[k, pl.ds(pl.multiple_of(j * rc, rc), rc)]
            pltpu.make_async_copy(src, dst, sem).start(priority=k % 2)
        return carry

    lax.fori_loop(0, tm, issue, 0)
    for k in range(TOP_K):
        pltpu.make_async_copy(ys_ref.at[pl.ds(0, tm * rc)], gbuf.at[k], sem).wait()

    tw = tw_ref[...]
    acc = tw[:, 0:1] * _load_token_tiles(gbuf.at[0])
    for k in range(1, TOP_K):
        acc = acc + tw[:, k:k + 1] * _load_token_tiles(gbuf.at[k])
    x2 = x1_ref[...] + gt_ref[0] * acc
    o_ref[...] = _rms(x2, gf_ref[...])


def _combine(ys, pos_flat, x1, tw, mod, g_final, seq):
    t, d = x1.shape
    tm = min(COMBINE_TM, t)
    return pl.pallas_call(
        _combine_kernel,
        out_shape=jax.ShapeDtypeStruct((t, d), F32),
        grid=(t // tm,),
        in_specs=[pl.BlockSpec((tm * TOP_K,), lambda s: (s,), memory_space=pltpu.SMEM),
                  pl.BlockSpec(memory_space=pl.ANY),
                  pl.BlockSpec((tm, d), lambda s: (s, 0)),
                  pl.BlockSpec((tm, LANES), lambda s: (s, 0)),
                  pl.BlockSpec((1, 1, d), lambda s: ((s * tm) // seq * 6 + 5, 0, 0)),
                  pl.BlockSpec((1, d), lambda s: (0, 0))],
        out_specs=pl.BlockSpec((tm, d), lambda s: (s, 0)),
        scratch_shapes=[pltpu.VMEM((TOP_K, tm * ROW_CHUNKS, LANES), F32),
                        pltpu.SemaphoreType.DMA],
        compiler_params=pltpu.CompilerParams(dimension_semantics=("arbitrary",),
                                             vmem_limit_bytes=VMEM_LIMIT_BYTES),
        name="combine",
    )(pos_flat, ys, x1, tw, mod, g_final)


def _routing_tables(top_idx, n_tiles):
    tm = MOE_TM
    e_flat = top_idx.reshape(-1)
    onehot = (e_flat[:, None] == jnp.arange(N_EXPERTS, dtype=I32)[None, :]).astype(I32)
    csum = jnp.cumsum(onehot, axis=0)
    counts = csum[-1]
    rank = jnp.sum(csum * onehot, axis=1) - 1
    tiles_per = (counts + tm - 1) // tm
    tile_end = jnp.cumsum(tiles_per)
    tile_start = tile_end - tiles_per
    pos = (jnp.sum(onehot * tile_start[None, :], axis=1) * tm + rank).astype(I32)
    n_used = tile_end[-1:].astype(I32)
    tile_ids = jnp.arange(n_tiles, dtype=I32)
    tile_expert = jnp.minimum(
        jnp.sum((tile_ids[:, None] >= tile_end[None, :]).astype(I32), axis=1),
        N_EXPERTS - 1).astype(I32)
    padtile = jnp.where(counts % tm != 0, tile_end - 1, -1).astype(I32)
    return pos, tile_expert, n_used, padtile


def kernel(x, c, rel_bias, w_ada, b_ada, g_mix, w_in, lambda_q1, lambda_k1, lambda_q2,
           lambda_k2, subln_g, w_br_a, w_br_b, w_out, g_ffn, w_router, b_router,
           w_gate_up, b_gate_up, w_down, b_down, g_final):
    b, s, d = x.shape
    assert d == D_MODEL and s % ATT_T == 0 and MOBA_BLOCK == ATT_T
    t_tok = b * s

    c8 = jnp.zeros((8, d), F32).at[:b].set(c)
    ada = _ada(c8, w_ada[0], b_ada[0].reshape(1, -1))
    mod = ada[:b].reshape(b * 6, 1, d)

    dtiles = _bias_tiles(rel_bias)
    tab = rel_bias.T.reshape(-1)

    proj = _proj(x, g_mix[0].reshape(1, d), mod, w_in[0].astype(BF16))

    lamv = jnp.concatenate([lambda_q1, lambda_k1, lambda_q2, lambda_k2], axis=0)
    y_a = _attention("diff", tab, proj, dtiles, (lamv, subln_g[0].reshape(-1, 1)), (0, 512, 1024))
    y_b = _attention("moba", tab, proj, dtiles, None, (1536, 2048, 2560))

    wr = jnp.zeros((d, LANES), BF16).at[:, :N_EXPERTS].set(w_router[0].astype(BF16))
    br = jnp.zeros((1, LANES), F32).at[0, :N_EXPERTS].set(b_router[0])
    x1, h2, ti, tw = _merge(x, proj, y_a, y_b, w_br_a[0].astype(BF16), w_br_b[0].astype(BF16),
                            w_out[0].astype(BF16), mod, g_ffn[0].reshape(1, d), wr, br)

    n_tiles = (t_tok * TOP_K) // MOE_TM + N_EXPERTS
    top_idx = ti.reshape(t_tok, LANES)[:, :TOP_K]
    pos, tile_expert, n_used, padtile = _routing_tables(top_idx, n_tiles)
    pos_start = pos * ROW_CHUNKS

    xs = _dispatch(h2.reshape(t_tok * ROW_CHUNKS, LANES), pos_start, padtile, n_tiles * MOE_TM)
    ys = _moe(xs, tile_expert, n_used, w_gate_up[0], b_gate_up[0].reshape(N_EXPERTS, 1, -1),
              w_down[0], b_down[0].reshape(N_EXPERTS, 1, -1))
    out = _combine(ys, pos_start, x1.reshape(t_tok, d), tw.reshape(t_tok, LANES), mod,
                   g_final.reshape(1, d), s)
    return out.reshape(b, s, d)
```

```python
import functools
import math

import jax
import jax.numpy as jnp
import numpy as np
from jax import lax
from jax.experimental import pallas as pl
from jax.experimental.pallas import tpu as pltpu

F32 = jnp.float32
BF16 = jnp.bfloat16
I32 = jnp.int32

D_MODEL = 1024
N_DIFF_HEADS = 4
DIFF_HEAD_DIM = 64
N_MOBA_HEADS = 8
MOBA_HEAD_DIM = 64
MOBA_BLOCK = 256
MOBA_TOPK = 3
N_HEADS_TOTAL = N_DIFF_HEADS + N_MOBA_HEADS
N_BUCKETS = 32
MAX_DISTANCE = 128
N_EXPERTS = 32
TOP_K = 4
D_FF = D_MODEL
SWIGLU_LIMIT = 7.0
SWIGLU_ALPHA = 1.702
RMS_EPS = 1e-5
LAM_INIT = 0.8 - 0.6 * math.exp(-0.3 * 0)
IN_WIDTH = 5120

LANES = 128
VMEM_LIMIT_BYTES = 56 * 1024 * 1024

ATT_T = 256
N_PAIRS = 4
V_ROWS = LANES + 16
PROJ_TM = 512
MERGE_TM = 512
MOE_TM = 512
MOE_FF_CHUNK = 256
DISPATCH_TM = 512
ROW_DMA_UNROLL = 8
COMBINE_TM = 256
NEG = -1e30


def _t5_bucket_np(dist):
    n = np.maximum(dist, 0)
    max_exact = N_BUCKETS // 2
    nf = np.maximum(n, max_exact).astype(np.float32)
    large = max_exact + (np.log(nf / max_exact) / math.log(MAX_DISTANCE / max_exact)
                         * (N_BUCKETS - max_exact)).astype(np.int32)
    large = np.minimum(large, N_BUCKETS - 1)
    return np.where(n < max_exact, n, large).astype(np.int32)


def _rms(x, g):
    return x * lax.rsqrt(jnp.mean(x * x, axis=-1, keepdims=True) + RMS_EPS) * g


ROW_CHUNKS = D_MODEL // LANES


def _store_token_tiles(ref, x):
    n = x.shape[0]
    for c in range(ROW_CHUNKS):
        ref[pl.ds(c, n, stride=ROW_CHUNKS), :] = x[:, c * LANES:(c + 1) * LANES]


def _load_token_tiles(ref):
    n = ref.shape[0] // ROW_CHUNKS
    return jnp.concatenate([ref[pl.ds(c, n, stride=ROW_CHUNKS), :] for c in range(ROW_CHUNKS)],
                           axis=1)


def _ada_kernel(c_ref, w_ref, b_ref, o_ref):
    o_ref[...] = jnp.dot(c_ref[...].astype(BF16), w_ref[...].astype(BF16),
                         preferred_element_type=F32) + b_ref[...]


def _ada(c8, w, b):
    d, n = w.shape
    tn = 1536
    return pl.pallas_call(
        _ada_kernel,
        out_shape=jax.ShapeDtypeStruct((8, n), F32),
        grid=(n // tn,),
        in_specs=[pl.BlockSpec((8, d), lambda j: (0, 0)),
                  pl.BlockSpec((d, tn), lambda j: (0, j)),
                  pl.BlockSpec((1, tn), lambda j: (0, j))],
        out_specs=pl.BlockSpec((8, tn), lambda j: (0, j)),
        compiler_params=pltpu.CompilerParams(dimension_semantics=("arbitrary",),
                                             vmem_limit_bytes=VMEM_LIMIT_BYTES),
        name="ada",
    )(c8, w, b)


def _bias_kernel(tab_ref, bd_ref, bp_ref, o_ref):
    g = pl.program_id(0)
    p = pl.program_id(1)
    t = pl.program_id(2)
    head = jnp.where(g == 0, p, N_DIFF_HEADS + 2 * p + t)
    bd = bd_ref[...]
    bp = bp_ref[...]
    d0 = jnp.where(bd < 0, NEG, 0.0).astype(F32)
    d1 = jnp.zeros(bp.shape, F32)
    for b in range(N_BUCKETS):
        val = tab_ref[head * N_BUCKETS + b]
        d0 = jnp.where(bd == b, val, d0)
        d1 = jnp.where(bp == b, val, d1)
    o_ref[0, 0, 0] = d0
    o_ref[0, 0, 1] = d1


def _bias_tiles(rel_bias):
    t = ATT_T
    jj = np.arange(t)[:, None]
    ii = np.arange(t)[None, :]
    bd = np.where(ii >= jj, _t5_bucket_np(ii - jj), -1).astype(np.int32)
    bp = _t5_bucket_np(t + ii - jj)
    tab = rel_bias.T.reshape(-1)
    return pl.pallas_call(
        _bias_kernel,
        out_shape=jax.ShapeDtypeStruct((2, N_PAIRS, 2, t, 2 * t), F32),
        grid=(2, N_PAIRS, 2),
        in_specs=[pl.BlockSpec(memory_space=pltpu.SMEM),
                  pl.BlockSpec((t, t), lambda g, p, h: (0, 0)),
                  pl.BlockSpec((t, t), lambda g, p, h: (0, 0))],
        out_specs=pl.BlockSpec((1, 1, 2, t, t), lambda g, p, h: (g, p, 0, 0, h)),
        compiler_params=pltpu.CompilerParams(
            dimension_semantics=("arbitrary", "arbitrary", "arbitrary")),
        name="bias_tiles",
    )(tab, jnp.asarray(bd), jnp.asarray(bp))


def _proj_kernel(x_ref, g_ref, sh_ref, sc_ref, w_ref, o_ref):
    x = x_ref[0]
    h = _rms(x, g_ref[...]) * (1.0 + sc_ref[0]) + sh_ref[0]
    hb = h.astype(BF16)
    n_total = w_ref.shape[1]
    for n0 in range(0, n_total, 512):
        o_ref[0, :, n0:n0 + 512] = jnp.dot(
            hb, w_ref[:, n0:n0 + 512], preferred_element_type=F32).astype(BF16)


def _proj(x, g_mix, mod, w_in_bf):
    b, s, d = x.shape
    n = w_in_bf.shape[1]
    tm = min(PROJ_TM, s)
    return pl.pallas_call(
        _proj_kernel,
        out_shape=jax.ShapeDtypeStruct((b, s, n), BF16),
        grid=(b, s // tm),
        in_specs=[pl.BlockSpec((1, tm, d), lambda bi, i: (bi, i, 0)),
                  pl.BlockSpec((1, d), lambda bi, i: (0, 0)),
                  pl.BlockSpec((1, 1, d), lambda bi, i: (bi * 6 + 0, 0, 0)),
                  pl.BlockSpec((1, 1, d), lambda bi, i: (bi * 6 + 1, 0, 0)),
                  pl.BlockSpec((d, n), lambda bi, i: (0, 0))],
        out_specs=pl.BlockSpec((1, tm, n), lambda bi, i: (bi, i, 0)),
        compiler_params=pltpu.CompilerParams(dimension_semantics=("arbitrary", "arbitrary"),
                                             vmem_limit_bytes=VMEM_LIMIT_BYTES),
        name="proj",
    )(x, g_mix, mod, mod, w_in_bf)


def _stack_q(q_ref, q2_ref, scale):
    t = ATT_T
    row = lax.broadcasted_iota(I32, (LANES, t), 0)
    for p in range(N_PAIRS):
        q = q_ref[0, 0, p * LANES:(p + 1) * LANES, :].astype(F32) * scale
        q2_ref[p, :, 0:t] = jnp.where(row < 64, q, 0.0).astype(BF16)
        q2_ref[p, :, t:2 * t] = jnp.where(row >= 64, q, 0.0).astype(BF16)


def _flash_init(m_ref, acc_ref):
    m_ref[...] = jnp.full(m_ref.shape, NEG, F32)
    acc_ref[...] = jnp.zeros(acc_ref.shape, F32)


def _score_slot(p, h, bank):
    return bank * (2 * N_PAIRS) + 2 * p + h


def _flash_scores(p, h, n, bank, q2_ref, k_ref, s_ref):
    t = ATT_T
    kt = k_ref[0, pl.ds(pl.multiple_of(n * t, t), t), p * LANES:(p + 1) * LANES]
    s_ref[_score_slot(p, h, bank)] = jnp.dot(kt, q2_ref[p, :, h * t:(h + 1) * t],
                                             preferred_element_type=F32)


def _flash_absorb(p, h, n, bank, tile_bias, row_bias, live, v_ref, s_ref, m_ref, acc_ref):
    t = ATT_T
    cs = slice(h * t, (h + 1) * t)
    vt = v_ref[0, n, p * V_ROWS:(p + 1) * V_ROWS, :]
    s = s_ref[_score_slot(p, h, bank)]
    if tile_bias is not None:
        s = s + tile_bias
    mx = jnp.max(s, axis=0, keepdims=True)
    if row_bias is not None:
        mx = mx + row_bias
    if live is not None:
        mx = jnp.where(live, mx, NEG)
    m_prev = m_ref[p, :, cs]
    m_new = jnp.maximum(m_prev, mx)
    alpha = jnp.exp(m_prev - m_new)
    shift = m_new if row_bias is None else m_new - row_bias
    if live is not None:
        shift = jnp.where(live, shift, -NEG)
    pt = jnp.exp(s - shift)
    acc_ref[p, :, cs] = alpha * acc_ref[p, :, cs] + jnp.dot(vt, pt.astype(BF16),
                                                            preferred_element_type=F32)
    m_ref[p, :, cs] = m_new


def _far_bias(tab_ref, head):
    return tab_ref[head * N_BUCKETS + (N_BUCKETS - 1)]


def _flash_sweep(i, far_args, prev_args, own_args, refs):
    q2_ref, k_ref, v_ref, s_ref, m_ref, acc_ref = refs
    chains = [(p, h) for p in range(N_PAIRS) for h in range(2)]

    def scores_tile(n, bank):
        for p, h in chains:
            _flash_scores(p, h, n, bank, q2_ref, k_ref, s_ref)

    def step(n, bank, args, with_next):
        for p, h in chains:
            if with_next:
                _flash_scores(p, h, n + 1, 1 - bank, q2_ref, k_ref, s_ref)
            _flash_absorb(p, h, n, bank, *args(p, h), v_ref, s_ref, m_ref, acc_ref)

    def far_step(n, bank):
        step(n, bank, lambda p, h: far_args(p, h, n), True)

    n_far = jnp.maximum(i - 1, 0)
    lead = jnp.bitwise_and(n_far, 1)

    @pl.when(jnp.bitwise_and(i, 1) == 0)
    def _():
        scores_tile(0, 0)

    @pl.when(jnp.bitwise_and(i, 1) == 1)
    def _():
        scores_tile(0, 1)

    @pl.when(lead == 1)
    def _():
        far_step(0, 0)

    def pair_body(j, carry):
        n = lead + 2 * j
        far_step(n, 1)
        far_step(n + 1, 0)
        return carry

    lax.fori_loop(0, jnp.right_shift(n_far, 1), pair_body, 0)

    @pl.when(i >= 1)
    def _():
        step(i - 1, 1, prev_args, True)

    step(i, 0, own_args, False)


def _diff_kernel(tab_ref, q_ref, k_ref, v_ref, d_ref, lamv_ref, g_ref, o_ref,
                 q2_ref, s_ref, m_ref, acc_ref):
    t = ATT_T
    i = pl.program_id(1)
    _stack_q(q_ref, q2_ref, DIFF_HEAD_DIM ** -0.5)
    _flash_init(m_ref, acc_ref)
    refs = (q2_ref, k_ref, v_ref, s_ref, m_ref, acc_ref)
    _flash_sweep(i,
                 lambda p, h, n: (None, _far_bias(tab_ref, p), None),
                 lambda p, h: (d_ref[0, p, 1, :, h * t:(h + 1) * t], None, None),
                 lambda p, h: (d_ref[0, p, 0, :, h * t:(h + 1) * t], None, None),
                 refs)

    lv = lamv_ref[...]
    lam = (jnp.exp(jnp.sum(lv[0:1] * lv[1:2], axis=1, keepdims=True))
           - jnp.exp(jnp.sum(lv[2:3] * lv[3:4], axis=1, keepdims=True)) + LAM_INIT)
    for p in range(N_PAIRS):
        acc = acc_ref[p, 0:LANES, :]
        l = acc_ref[p, LANES:LANES + 1, :]
        o = acc[:, 0:t] / l[:, 0:t] - lam * (acc[:, t:2 * t] / l[:, t:2 * t])
        y = o * lax.rsqrt(jnp.mean(o * o, axis=0, keepdims=True) + RMS_EPS) * g_ref[...]
        o_ref[0, 0, p * LANES:(p + 1) * LANES, :] = (y * (1.0 - LAM_INIT)).astype(BF16)


def _moba_kernel(tab_ref, q_ref, k_ref, v_ref, d_ref, o_ref,
                 q2_ref, s_ref, m_ref, acc_ref, kmean_ref, sel_ref):
    t = ATT_T
    i = pl.program_id(1)
    nb = k_ref.shape[1] // t
    nbp = kmean_ref.shape[1]

    @pl.when(i == 0)
    def _():
        kmean_ref[...] = jnp.zeros(kmean_ref.shape, F32)
        for p in range(N_PAIRS):
            for n in range(nb):
                blk = k_ref[0, n * t:(n + 1) * t, p * LANES:(p + 1) * LANES].astype(F32)
                kmean_ref[p, n:n + 1, :] = jnp.sum(blk, axis=0, keepdims=True) * (1.0 / t)

    _stack_q(q_ref, q2_ref, MOBA_HEAD_DIM ** -0.5)
    _flash_init(m_ref, acc_ref)

    blk_id = lax.broadcasted_iota(I32, (nbp, 2 * t), 0)
    blk_f = blk_id.astype(F32)
    for p in range(N_PAIRS):
        gs = jnp.dot(kmean_ref[p].astype(BF16), q2_ref[p], preferred_element_type=F32)
        g = jnp.where(blk_id < i, gs, -jnp.inf)
        sel = jnp.full((nbp, 2 * t), NEG, F32)
        for _ in range(MOBA_TOPK):
            mx = jnp.max(g, axis=0, keepdims=True)
            first = jnp.min(jnp.where(g == mx, blk_f, float(nbp)), axis=0, keepdims=True)
            hit = blk_f == first
            sel = jnp.where(hit, 0.0, sel)
            g = jnp.where(hit, -jnp.inf, g)
        sel_ref[p] = sel

    def live(p, h, n):
        return sel_ref[p, pl.ds(n, 1), h * t:(h + 1) * t] > -1.0

    refs = (q2_ref, k_ref, v_ref, s_ref, m_ref, acc_ref)
    _flash_sweep(i,
                 lambda p, h, n: (None, _far_bias(tab_ref, N_DIFF_HEADS + 2 * p + h),
                                  live(p, h, n)),
                 lambda p, h: (d_ref[0, p, 1, :, h * t:(h + 1) * t], None, live(p, h, i - 1)),
                 lambda p, h: (d_ref[0, p, 0, :, h * t:(h + 1) * t], None, None),
                 refs)

    row = lax.broadcasted_iota(I32, (LANES, t), 0)
    for p in range(N_PAIRS):
        acc = acc_ref[p, 0:LANES, :]
        l = acc_ref[p, LANES:LANES + 1, :]
        o = jnp.where(row < 64, acc[:, 0:t] / l[:, 0:t], acc[:, t:2 * t] / l[:, t:2 * t])
        o_ref[0, 0, p * LANES:(p + 1) * LANES, :] = o.astype(BF16)


def _to_kv_major(proj, col0, nq):
    b, s, _ = proj.shape
    w = N_PAIRS * LANES
    return jnp.swapaxes(proj[:, :, col0:col0 + w].reshape(b, nq, ATT_T, w), 2, 3)


def _values_kv_major(proj, col0, nq):
    b, s, _ = proj.shape
    v = proj[:, :, col0:col0 + N_PAIRS * LANES].reshape(b, nq, ATT_T, N_PAIRS, LANES)
    ones = jnp.ones((b, nq, ATT_T, N_PAIRS, V_ROWS - LANES), BF16)
    v = jnp.concatenate([v, ones], axis=-1)
    return jnp.transpose(v, (0, 1, 3, 4, 2)).reshape(b, nq, N_PAIRS * V_ROWS, ATT_T)


def _attention(kind, tab, proj, dtiles, extra, cols):
    b, s, _ = proj.shape
    t = ATT_T
    nq = s // t
    w = N_PAIRS * LANES
    qc, kc, vc = cols
    gidx = 0 if kind == "diff" else 1
    q_t = _to_kv_major(proj, qc, nq)
    v_t = _values_kv_major(proj, vc, nq)
    in_specs = [pl.BlockSpec(memory_space=pltpu.SMEM),
                pl.BlockSpec((1, 1, w, t), lambda bi, i: (bi, i, 0, 0)),
                pl.BlockSpec((1, s, w), lambda bi, i: (bi, 0, kc // w)),
                pl.BlockSpec((1, nq, N_PAIRS * V_ROWS, t), lambda bi, i: (bi, 0, 0, 0)),
                pl.BlockSpec((1, N_PAIRS, 2, t, 2 * t), lambda bi, i: (gidx, 0, 0, 0, 0))]
    scratch = [pltpu.VMEM((N_PAIRS, LANES, 2 * t), BF16),
               pltpu.VMEM((2 * 2 * N_PAIRS, t, t), F32),
               pltpu.VMEM((N_PAIRS, 1, 2 * t), F32),
               pltpu.VMEM((N_PAIRS, V_ROWS, 2 * t), F32)]
    if kind == "diff":
        lamv, subln_g = extra
        in_specs += [pl.BlockSpec(lamv.shape, lambda bi, i: (0, 0)),
                     pl.BlockSpec(subln_g.shape, lambda bi, i: (0, 0))]
        args = (tab, q_t, proj, v_t, dtiles, lamv, subln_g)
        body = _diff_kernel
    else:
        nbp = max(32, nq)
        scratch += [pltpu.VMEM((N_PAIRS, nbp, LANES), F32), pltpu.VMEM((N_PAIRS, nbp, 2 * t), F32)]
        args = (tab, q_t, proj, v_t, dtiles)
        body = _moba_kernel
    y_t = pl.pallas_call(
        body,
        out_shape=jax.ShapeDtypeStruct((b, nq, w, t), BF16),
        grid=(b, nq),
        in_specs=in_specs,
        out_specs=pl.BlockSpec((1, 1, w, t), lambda bi, i: (bi, i, 0, 0)),
        scratch_shapes=scratch,
        compiler_params=pltpu.CompilerParams(
            dimension_semantics=("arbitrary", "arbitrary"),
            vmem_limit_bytes=VMEM_LIMIT_BYTES),
        name=kind + "_attn",
    )(*args)
    return jnp.swapaxes(y_t, 2, 3).reshape(b, s, w)


def _merge_kernel(x_ref, ya_ref, yb_ref, gla_ref, glb_ref, wa_ref, wb_ref, wo_ref,
                  gt_ref, shf_ref, scf_ref, gf_ref, wr_ref, br_ref,
                  x1_ref, h2_ref, ti_ref, tw_ref):
    a = jnp.dot(ya_ref[0], wa_ref[...], preferred_element_type=F32)
    b = jnp.dot(yb_ref[0], wb_ref[...], preferred_element_type=F32)
    merged = (jax.nn.sigmoid(gla_ref[0].astype(F32)) * a
              + jax.nn.sigmoid(glb_ref[0].astype(F32)) * b)
    mo = jnp.dot(merged.astype(BF16), wo_ref[...], preferred_element_type=F32)
    x1 = x_ref[0] + gt_ref[0] * mo
    x1_ref[0] = x1
    h = _rms(x1, gf_ref[...]) * (1.0 + scf_ref[0]) + shf_ref[0]
    _store_token_tiles(h2_ref.at[0], h)
    logits = jnp.dot(h.astype(BF16), wr_ref[...], preferred_element_type=F32) + br_ref[...]
    tm = logits.shape[0]
    lane = lax.broadcasted_iota(I32, (tm, LANES), 1)
    lane_f = lane.astype(F32)
    g = jnp.where(lane < N_EXPERTS, logits, -jnp.inf)
    vals, idxs = [], []
    for _ in range(TOP_K):
        mx = jnp.max(g, axis=1, keepdims=True)
        first = jnp.min(jnp.where(g == mx, lane_f, float(LANES)), axis=1, keepdims=True)
        vals.append(mx)
        idxs.append(first)
        g = jnp.where(lane_f == first, -jnp.inf, g)
    es = [jnp.exp(v - vals[0]) for v in vals]
    den = es[0] + es[1] + es[2] + es[3]
    ti = jnp.zeros((tm, LANES), F32)
    tw = jnp.zeros((tm, LANES), F32)
    for k in range(TOP_K):
        ti = jnp.where(lane == k, idxs[k], ti)
        tw = jnp.where(lane == k, es[k] / den, tw)
    ti_ref[0] = ti.astype(I32)
    tw_ref[0] = tw


def _merge(x, proj, y_a, y_b, w_br_a, w_br_b, w_out, mod, g_ffn, w_router, b_router):
    b, s, d = x.shape
    tm = min(MERGE_TM, s)
    full = lambda shape: pl.BlockSpec(shape, lambda bi, i: (0,) * len(shape))
    tile = lambda w, c=0: pl.BlockSpec((1, tm, w), lambda bi, i: (bi, i, c))
    modspec = lambda j: pl.BlockSpec((1, 1, d), lambda bi, i: (bi * 6 + j, 0, 0))
    return pl.pallas_call(
        _merge_kernel,
        out_shape=(jax.ShapeDtypeStruct((b, s, d), F32),
                   jax.ShapeDtypeStruct((b, s * ROW_CHUNKS, LANES), F32),
                   jax.ShapeDtypeStruct((b, s, LANES), I32),
                   jax.ShapeDtypeStruct((b, s, LANES), F32)),
        grid=(b, s // tm),
        in_specs=[tile(d), tile(512), tile(512), tile(d, 3), tile(d, 4),
                  full(w_br_a.shape), full(w_br_b.shape), full(w_out.shape),
                  modspec(2), modspec(3), modspec(4), full(g_ffn.shape),
                  full(w_router.shape), full(b_router.shape)],
        out_specs=(tile(d), pl.BlockSpec((1, tm * ROW_CHUNKS, LANES), lambda bi, i: (bi, i, 0)),
                   tile(LANES), tile(LANES)),
        compiler_params=pltpu.CompilerParams(dimension_semantics=("arbitrary", "arbitrary"),
                                             vmem_limit_bytes=VMEM_LIMIT_BYTES),
        name="merge",
    )(x, y_a, y_b, proj, proj, w_br_a, w_br_b, w_out, mod, mod, mod, g_ffn,
      w_router, b_router)


def _dispatch_kernel(padtile_ref, pos_ref, h2_ref, xs_ref, zeros_ref, zsem, sem):
    step = pl.program_id(0)
    rc = ROW_CHUNKS
    n_tok = h2_ref.shape[0] // rc
    pad_rows = zeros_ref.shape[0]

    def zero_fill(e):
        start = pl.multiple_of(padtile_ref[e] * pad_rows, pad_rows)
        return pltpu.make_async_copy(zeros_ref, xs_ref.at[pl.ds(start, pad_rows)], zsem)

    @pl.when(step == 0)
    def _():
        zeros_ref[...] = jnp.zeros(zeros_ref.shape, F32)
        for e in range(N_EXPERTS):
            @pl.when(padtile_ref[e] >= 0)
            def _():
                zero_fill(e).start()
        for e in range(N_EXPERTS):
            @pl.when(padtile_ref[e] >= 0)
            def _():
                zero_fill(e).wait()

    def issue(j, carry):
        src = h2_ref.at[pl.ds(pl.multiple_of(j * rc, rc), rc)]
        for k in range(TOP_K):
            dst = xs_ref.at[pl.ds(pl.multiple_of(pos_ref[j * TOP_K + k], rc), rc)]
            pltpu.make_async_copy(src, dst, sem).start(priority=k % 2)
        return carry

    lax.fori_loop(0, n_tok, issue, 0, unroll=ROW_DMA_UNROLL)
    for k in range(TOP_K):
        pltpu.make_async_copy(h2_ref, xs_ref.at[pl.ds(0, n_tok * rc)], sem).wait()


def _dispatch(h2_tiles, pos_start, padtile, n_rows):
    rc = ROW_CHUNKS
    t_tok = h2_tiles.shape[0] // rc
    tm = min(DISPATCH_TM, t_tok)
    return pl.pallas_call(
        _dispatch_kernel,
        out_shape=jax.ShapeDtypeStruct((n_rows * rc, LANES), F32),
        grid_spec=pltpu.PrefetchScalarGridSpec(
            num_scalar_prefetch=1,
            grid=(t_tok // tm,),
            in_specs=[pl.BlockSpec((tm * TOP_K,), lambda s, pt: (s,), memory_space=pltpu.SMEM),
                      pl.BlockSpec((tm * rc, LANES), lambda s, pt: (s, 0))],
            out_specs=pl.BlockSpec(memory_space=pl.ANY),
            scratch_shapes=[pltpu.VMEM((MOE_TM * rc, LANES), F32),
                            pltpu.SemaphoreType.DMA, pltpu.SemaphoreType.DMA]),
        compiler_params=pltpu.CompilerParams(dimension_semantics=("arbitrary",),
                                             vmem_limit_bytes=VMEM_LIMIT_BYTES),
        name="dispatch",
    )(padtile, pos_start, h2_tiles)


def _moe_kernel(te_ref, nu_ref, xs_ref, wgu_ref, bgu_ref, wd_ref, bd_ref,
                ys_ref, wgu_bf, wd_bf, gu_ref):
    r = pl.program_id(0)
    busy = r < nu_ref[0]
    fresh = jnp.logical_or(r == 0, te_ref[r] != te_ref[jnp.maximum(r - 1, 0)])

    @pl.when(jnp.logical_and(busy, fresh))
    def _():
        wgu_bf[...] = wgu_ref[0].astype(BF16)
        wd_bf[...] = wd_ref[0].astype(BF16)

    @pl.when(busy)
    def _():
        x = _load_token_tiles(xs_ref).astype(BF16)
        fc = MOE_FF_CHUNK
        for c in range(2 * D_FF // fc):
            lo, hi = c * fc, (c + 1) * fc
            gu_ref[:, lo:hi] = (jnp.dot(x, wgu_bf[:, lo:hi], preferred_element_type=F32)
                                + bgu_ref[0][:, lo:hi])
        y = None
        for c in range(D_FF // fc):
            lo, hi = c * fc, (c + 1) * fc
            gate = jnp.minimum(gu_ref[:, lo:hi], SWIGLU_LIMIT)
            up = jnp.clip(gu_ref[:, D_FF + lo:D_FF + hi], -SWIGLU_LIMIT, SWIGLU_LIMIT)
            act = (up + 1.0) * gate * jax.nn.sigmoid(SWIGLU_ALPHA * gate)
            part = jnp.dot(act.astype(BF16), wd_bf[lo:hi, :], preferred_element_type=F32)
            y = part if y is None else y + part
        _store_token_tiles(ys_ref, y + bd_ref[0])


def _moe(xs_tiles, tile_expert, n_used, w_gate_up, b_gate_up, w_down, b_down):
    d = D_MODEL
    rc = ROW_CHUNKS
    tm = MOE_TM
    n_tiles = tile_expert.shape[0]
    e, _, f2 = w_gate_up.shape
    wsel = lambda r, te, nu: (te[r], 0, 0)
    rows = lambda r, te, nu: (jnp.minimum(r, nu[0] - 1), 0)
    return pl.pallas_call(
        _moe_kernel,
        out_shape=jax.ShapeDtypeStruct((n_tiles * tm * rc, LANES), F32),
        grid_spec=pltpu.PrefetchScalarGridSpec(
            num_scalar_prefetch=2,
            grid=(n_tiles,),
            in_specs=[pl.BlockSpec((tm * rc, LANES), rows),
                      pl.BlockSpec((1, d, f2), wsel),
                      pl.BlockSpec((1, 1, f2), wsel),
                      pl.BlockSpec((1, D_FF, d), wsel),
                      pl.BlockSpec((1, 1, d), wsel)],
            out_specs=pl.BlockSpec((tm * rc, LANES), rows),
            scratch_shapes=[pltpu.VMEM((d, f2), BF16), pltpu.VMEM((D_FF, d), BF16),
                            pltpu.VMEM((tm, f2), F32)]),
        compiler_params=pltpu.CompilerParams(dimension_semantics=("arbitrary",),
                                             vmem_limit_bytes=VMEM_LIMIT_BYTES),
        name="moe",
    )(tile_expert, n_used, xs_tiles, w_gate_up, b_gate_up, w_down, b_down)


def _combine_kernel(pos_ref, ys_ref, x1_ref, tw_ref, gt_ref, gf_ref, o_ref, gbuf, sem):
    tm = x1_ref.shape[0]
    rc = ROW_CHUNKS

    def issue(j, carry):
        for k in range(TOP_K):
            src = ys_ref.at[pl.ds(pl.multiple_of(pos_ref[j * TOP_K + k], rc), rc)]
            dst = gbuf.at[k, pl.ds(pl.multiple_of(j * rc, rc), rc)]
            pltpu.make_async_copy(src, dst, sem).start(priority=k % 2)
        return carry

    lax.fori_loop(0, tm, issue, 0, unroll=ROW_DMA_UNROLL)
    for k in range(TOP_K):
        pltpu.make_async_copy(ys_ref.at[pl.ds(0, tm * rc)], gbuf.at[k], sem).wait()

    tw = tw_ref[...]
    acc = tw[:, 0:1] * _load_token_tiles(gbuf.at[0])
    for k in range(1, TOP_K):
        acc = acc + tw[:, k:k + 1] * _load_token_tiles(gbuf.at[k])
    x2 = x1_ref[...] + gt_ref[0] * acc
    o_ref[...] = _rms(x2, gf_ref[...])


def _combine(ys, pos_flat, x1, tw, mod, g_final, seq):
    t, d = x1.shape
    tm = min(COMBINE_TM, t)
    return pl.pallas_call(
        _combine_kernel,
        out_shape=jax.ShapeDtypeStruct((t, d), F32),
        grid=(t // tm,),
        in_specs=[pl.BlockSpec((tm * TOP_K,), lambda s: (s,), memory_space=pltpu.SMEM),
                  pl.BlockSpec(memory_space=pl.ANY),
                  pl.BlockSpec((tm, d), lambda s: (s, 0)),
                  pl.BlockSpec((tm, LANES), lambda s: (s, 0)),
                  pl.BlockSpec((1, 1, d), lambda s: ((s * tm) // seq * 6 + 5, 0, 0)),
                  pl.BlockSpec((1, d), lambda s: (0, 0))],
        out_specs=pl.BlockSpec((tm, d), lambda s: (s, 0)),
        scratch_shapes=[pltpu.VMEM((TOP_K, tm * ROW_CHUNKS, LANES), F32),
                        pltpu.SemaphoreType.DMA],
        compiler_params=pltpu.CompilerParams(dimension_semantics=("arbitrary",),
                                             vmem_limit_bytes=VMEM_LIMIT_BYTES),
        name="combine",
    )(pos_flat, ys, x1, tw, mod, g_final)


def _routing_tables(top_idx, n_tiles):
    tm = MOE_TM
    e_flat = top_idx.reshape(-1)
    onehot = (e_flat[:, None] == jnp.arange(N_EXPERTS, dtype=I32)[None, :]).astype(I32)
    csum = jnp.cumsum(onehot, axis=0)
    counts = csum[-1]
    rank = jnp.sum(csum * onehot, axis=1) - 1
    tiles_per = (counts + tm - 1) // tm
    tile_end = jnp.cumsum(tiles_per)
    tile_start = tile_end - tiles_per
    pos = (jnp.sum(onehot * tile_start[None, :], axis=1) * tm + rank).astype(I32)
    n_used = tile_end[-1:].astype(I32)
    tile_ids = jnp.arange(n_tiles, dtype=I32)
    tile_expert = jnp.minimum(
        jnp.sum((tile_ids[:, None] >= tile_end[None, :]).astype(I32), axis=1),
        N_EXPERTS - 1).astype(I32)
    padtile = jnp.where(counts % tm != 0, tile_end - 1, -1).astype(I32)
    return pos, tile_expert, n_used, padtile


def kernel(x, c, rel_bias, w_ada, b_ada, g_mix, w_in, lambda_q1, lambda_k1, lambda_q2,
           lambda_k2, subln_g, w_br_a, w_br_b, w_out, g_ffn, w_router, b_router,
           w_gate_up, b_gate_up, w_down, b_down, g_final):
    b, s, d = x.shape
    assert d == D_MODEL and s % ATT_T == 0 and MOBA_BLOCK == ATT_T
    t_tok = b * s

    c8 = jnp.zeros((8, d), F32).at[:b].set(c)
    ada = _ada(c8, w_ada[0], b_ada[0].reshape(1, -1))
    mod = ada[:b].reshape(b * 6, 1, d)

    dtiles = _bias_tiles(rel_bias)
    tab = rel_bias.T.reshape(-1)

    proj = _proj(x, g_mix[0].reshape(1, d), mod, w_in[0].astype(BF16))

    lamv = jnp.concatenate([lambda_q1, lambda_k1, lambda_q2, lambda_k2], axis=0)
    y_a = _attention("diff", tab, proj, dtiles, (lamv, subln_g[0].reshape(-1, 1)), (0, 512, 1024))
    y_b = _attention("moba", tab, proj, dtiles, None, (1536, 2048, 2560))

    wr = jnp.zeros((d, LANES), BF16).at[:, :N_EXPERTS].set(w_router[0].astype(BF16))
    br = jnp.zeros((1, LANES), F32).at[0, :N_EXPERTS].set(b_router[0])
    x1, h2, ti, tw = _merge(x, proj, y_a, y_b, w_br_a[0].astype(BF16), w_br_b[0].astype(BF16),
                            w_out[0].astype(BF16), mod, g_ffn[0].reshape(1, d), wr, br)

    n_tiles = (t_tok * TOP_K) // MOE_TM + N_EXPERTS
    top_idx = ti.reshape(t_tok, LANES)[:, :TOP_K]
    pos, tile_expert, n_used, padtile = _routing_tables(top_idx, n_tiles)
    pos_start = pos * ROW_CHUNKS

    xs = _dispatch(h2.reshape(t_tok * ROW_CHUNKS, LANES), pos_start, padtile, n_tiles * MOE_TM)
    ys = _moe(xs, tile_expert, n_used, w_gate_up[0], b_gate_up[0].reshape(N_EXPERTS, 1, -1),
              w_down[0], b_down[0].reshape(N_EXPERTS, 1, -1))
    out = _combine(ys, pos_start, x1.reshape(t_tok, d), tw.reshape(t_tok, LANES), mod,
                   g_final.reshape(1, d), s)
    return out.reshape(b, s, d)
```

```python
import functools
import math

import jax
import jax.numpy as jnp
import numpy as np
from jax import lax
from jax.experimental import pallas as pl
from jax.experimental.pallas import tpu as pltpu

F32 = jnp.float32
BF16 = jnp.bfloat16
I32 = jnp.int32

D_MODEL = 1024
N_DIFF_HEADS = 4
DIFF_HEAD_DIM = 64
N_MOBA_HEADS = 8
MOBA_HEAD_DIM = 64
MOBA_BLOCK = 256
MOBA_TOPK = 3
N_HEADS_TOTAL = N_DIFF_HEADS + N_MOBA_HEADS
N_BUCKETS = 32
MAX_DISTANCE = 128
N_EXPERTS = 32
TOP_K = 4
D_FF = D_MODEL
SWIGLU_LIMIT = 7.0
SWIGLU_ALPHA = 1.702
RMS_EPS = 1e-5
LAM_INIT = 0.8 - 0.6 * math.exp(-0.3 * 0)
IN_WIDTH = 5120

LANES = 128
VMEM_LIMIT_BYTES = 56 * 1024 * 1024

ATT_T = 256
N_PAIRS = 4
V_ROWS = LANES + 16
PROJ_TM = 512
MERGE_TM = 512
MOE_TM = 512
MOE_FF_CHUNK = 256
DISPATCH_TM = 512
ROW_DMA_UNROLL = 8
COMBINE_TM = 256
NEG = -1e30


def _t5_bucket_np(dist):
    n = np.maximum(dist, 0)
    max_exact = N_BUCKETS // 2
    nf = np.maximum(n, max_exact).astype(np.float32)
    large = max_exact + (np.log(nf / max_exact) / math.log(MAX_DISTANCE / max_exact)
                         * (N_BUCKETS - max_exact)).astype(np.int32)
    large = np.minimum(large, N_BUCKETS - 1)
    return np.where(n < max_exact, n, large).astype(np.int32)


def _rms(x, g):
    return x * lax.rsqrt(jnp.mean(x * x, axis=-1, keepdims=True) + RMS_EPS) * g


ROW_CHUNKS = D_MODEL // LANES


def _store_token_tiles(ref, x):
    n = x.shape[0]
    for c in range(ROW_CHUNKS):
        ref[pl.ds(c, n, stride=ROW_CHUNKS), :] = x[:, c * LANES:(c + 1) * LANES]


def _load_token_tiles(ref):
    n = ref.shape[0] // ROW_CHUNKS
    return jnp.concatenate([ref[pl.ds(c, n, stride=ROW_CHUNKS), :] for c in range(ROW_CHUNKS)],
                           axis=1)


def _ada_kernel(c_ref, w_ref, b_ref, o_ref):
    o_ref[...] = jnp.dot(c_ref[...].astype(BF16), w_ref[...].astype(BF16),
                         preferred_element_type=F32) + b_ref[...]


def _ada(c8, w, b):
    d, n = w.shape
    tn = 1536
    return pl.pallas_call(
        _ada_kernel,
        out_shape=jax.ShapeDtypeStruct((8, n), F32),
        grid=(n // tn,),
        in_specs=[pl.BlockSpec((8, d), lambda j: (0, 0)),
                  pl.BlockSpec((d, tn), lambda j: (0, j)),
                  pl.BlockSpec((1, tn), lambda j: (0, j))],
        out_specs=pl.BlockSpec((8, tn), lambda j: (0, j)),
        compiler_params=pltpu.CompilerParams(dimension_semantics=("arbitrary",),
                                             vmem_limit_bytes=VMEM_LIMIT_BYTES),
        name="ada",
    )(c8, w, b)


def _bias_kernel(tab_ref, bd_ref, bp_ref, o_ref):
    g = pl.program_id(0)
    p = pl.program_id(1)
    t = pl.program_id(2)
    head = jnp.where(g == 0, p, N_DIFF_HEADS + 2 * p + t)
    bd = bd_ref[...]
    bp = bp_ref[...]
    d0 = jnp.where(bd < 0, NEG, 0.0).astype(F32)
    d1 = jnp.zeros(bp.shape, F32)
    for b in range(N_BUCKETS):
        val = tab_ref[head * N_BUCKETS + b]
        d0 = jnp.where(bd == b, val, d0)
        d1 = jnp.where(bp == b, val, d1)
    o_ref[0, 0, 0] = d0
    o_ref[0, 0, 1] = d1


def _bias_tiles(rel_bias):
    t = ATT_T
    jj = np.arange(t)[:, None]
    ii = np.arange(t)[None, :]
    bd = np.where(ii >= jj, _t5_bucket_np(ii - jj), -1).astype(np.int32)
    bp = _t5_bucket_np(t + ii - jj)
    tab = rel_bias.T.reshape(-1)
    return pl.pallas_call(
        _bias_kernel,
        out_shape=jax.ShapeDtypeStruct((2, N_PAIRS, 2, t, 2 * t), F32),
        grid=(2, N_PAIRS, 2),
        in_specs=[pl.BlockSpec(memory_space=pltpu.SMEM),
                  pl.BlockSpec((t, t), lambda g, p, h: (0, 0)),
                  pl.BlockSpec((t, t), lambda g, p, h: (0, 0))],
        out_specs=pl.BlockSpec((1, 1, 2, t, t), lambda g, p, h: (g, p, 0, 0, h)),
        compiler_params=pltpu.CompilerParams(
            dimension_semantics=("arbitrary", "arbitrary", "arbitrary")),
        name="bias_tiles",
    )(tab, jnp.asarray(bd), jnp.asarray(bp))


def _proj_kernel(x_ref, g_ref, sh_ref, sc_ref, w_ref, o_ref):
    x = x_ref[0]
    h = _rms(x, g_ref[...]) * (1.0 + sc_ref[0]) + sh_ref[0]
    hb = h.astype(BF16)
    n_total = w_ref.shape[1]
    for n0 in range(0, n_total, 512):
        o_ref[0, :, n0:n0 + 512] = jnp.dot(
            hb, w_ref[:, n0:n0 + 512], preferred_element_type=F32).astype(BF16)


def _proj(x, g_mix, mod, w_in_bf):
    b, s, d = x.shape
    n = w_in_bf.shape[1]
    tm = min(PROJ_TM, s)
    return pl.pallas_call(
        _proj_kernel,
        out_shape=jax.ShapeDtypeStruct((b, s, n), BF16),
        grid=(b, s // tm),
        in_specs=[pl.BlockSpec((1, tm, d), lambda bi, i: (bi, i, 0)),
                  pl.BlockSpec((1, d), lambda bi, i: (0, 0)),
                  pl.BlockSpec((1, 1, d), lambda bi, i: (bi * 6 + 0, 0, 0)),
                  pl.BlockSpec((1, 1, d), lambda bi, i: (bi * 6 + 1, 0, 0)),
                  pl.BlockSpec((d, n), lambda bi, i: (0, 0))],
        out_specs=pl.BlockSpec((1, tm, n), lambda bi, i: (bi, i, 0)),
        compiler_params=pltpu.CompilerParams(dimension_semantics=("arbitrary", "arbitrary"),
                                             vmem_limit_bytes=VMEM_LIMIT_BYTES),
        name="proj",
    )(x, g_mix, mod, mod, w_in_bf)


def _stack_q(q_ref, q2_ref, scale):
    t = ATT_T
    row = lax.broadcasted_iota(I32, (LANES, t), 0)
    for p in range(N_PAIRS):
        q = (q_ref[0, :, p * LANES:(p + 1) * LANES].astype(F32) * scale).T
        q2_ref[p, :, 0:t] = jnp.where(row < 64, q, 0.0).astype(BF16)
        q2_ref[p, :, t:2 * t] = jnp.where(row >= 64, q, 0.0).astype(BF16)


def _transpose_values(v_ref, vt_ref):
    t = ATT_T
    nq = vt_ref.shape[0]

    def body(n, carry):
        for p in range(N_PAIRS):
            blk = v_ref[0, pl.ds(pl.multiple_of(n * t, t), t), p * LANES:(p + 1) * LANES]
            vt_ref[n, p * V_ROWS:p * V_ROWS + LANES, :] = blk.astype(F32).T.astype(BF16)
            vt_ref[n, p * V_ROWS + LANES:(p + 1) * V_ROWS, :] = jnp.ones((V_ROWS - LANES, t), BF16)
        return carry

    lax.fori_loop(0, nq, body, 0)


def _flash_init(m_ref, acc_ref):
    m_ref[...] = jnp.full(m_ref.shape, NEG, F32)
    acc_ref[...] = jnp.zeros(acc_ref.shape, F32)


def _score_slot(p, h, bank):
    return bank * (2 * N_PAIRS) + 2 * p + h


def _flash_scores(p, h, n, bank, q2_ref, k_ref, s_ref):
    t = ATT_T
    kt = k_ref[0, pl.ds(pl.multiple_of(n * t, t), t), p * LANES:(p + 1) * LANES]
    s_ref[_score_slot(p, h, bank)] = jnp.dot(kt, q2_ref[p, :, h * t:(h + 1) * t],
                                             preferred_element_type=F32)


def _flash_absorb(p, h, n, bank, tile_bias, row_bias, live, v_ref, s_ref, m_ref, acc_ref):
    t = ATT_T
    cs = slice(h * t, (h + 1) * t)
    vt = v_ref[n, p * V_ROWS:(p + 1) * V_ROWS, :]
    s = s_ref[_score_slot(p, h, bank)]
    if tile_bias is not None:
        s = s + tile_bias
    mx = jnp.max(s, axis=0, keepdims=True)
    if row_bias is not None:
        mx = mx + row_bias
    if live is not None:
        mx = jnp.where(live, mx, NEG)
    m_prev = m_ref[p, :, cs]
    m_new = jnp.maximum(m_prev, mx)
    alpha = jnp.exp(m_prev - m_new)
    shift = m_new if row_bias is None else m_new - row_bias
    if live is not None:
        shift = jnp.where(live, shift, -NEG)
    pt = jnp.exp(s - shift)
    acc_ref[p, :, cs] = alpha * acc_ref[p, :, cs] + jnp.dot(vt, pt.astype(BF16),
                                                            preferred_element_type=F32)
    m_ref[p, :, cs] = m_new


def _far_bias(tab_ref, head):
    return tab_ref[head * N_BUCKETS + (N_BUCKETS - 1)]


def _flash_sweep(i, far_args, prev_args, own_args, refs):
    q2_ref, k_ref, v_ref, s_ref, m_ref, acc_ref = refs
    chains = [(p, h) for p in range(N_PAIRS) for h in range(2)]

    def scores_tile(n, bank):
        for p, h in chains:
            _flash_scores(p, h, n, bank, q2_ref, k_ref, s_ref)

    def step(n, bank, args, with_next):
        for p, h in chains:
            if with_next:
                _flash_scores(p, h, n + 1, 1 - bank, q2_ref, k_ref, s_ref)
            _flash_absorb(p, h, n, bank, *args(p, h), v_ref, s_ref, m_ref, acc_ref)

    def far_step(n, bank):
        step(n, bank, lambda p, h: far_args(p, h, n), True)

    n_far = jnp.maximum(i - 1, 0)
    lead = jnp.bitwise_and(n_far, 1)

    @pl.when(jnp.bitwise_and(i, 1) == 0)
    def _():
        scores_tile(0, 0)

    @pl.when(jnp.bitwise_and(i, 1) == 1)
    def _():
        scores_tile(0, 1)

    @pl.when(lead == 1)
    def _():
        far_step(0, 0)

    def pair_body(j, carry):
        n = lead + 2 * j
        far_step(n, 1)
        far_step(n + 1, 0)
        return carry

    lax.fori_loop(0, jnp.right_shift(n_far, 1), pair_body, 0)

    @pl.when(i >= 1)
    def _():
        step(i - 1, 1, prev_args, True)

    step(i, 0, own_args, False)


def _diff_kernel(tab_ref, q_ref, k_ref, v_ref, d_ref, lamv_ref, g_ref, o_ref,
                 q2_ref, s_ref, m_ref, acc_ref, vt_ref):
    t = ATT_T
    i = pl.program_id(1)

    @pl.when(i == 0)
    def _():
        _transpose_values(v_ref, vt_ref)

    _stack_q(q_ref, q2_ref, DIFF_HEAD_DIM ** -0.5)
    _flash_init(m_ref, acc_ref)
    refs = (q2_ref, k_ref, vt_ref, s_ref, m_ref, acc_ref)
    _flash_sweep(i,
                 lambda p, h, n: (None, _far_bias(tab_ref, p), None),
                 lambda p, h: (d_ref[0, p, 1, :, h * t:(h + 1) * t], None, None),
                 lambda p, h: (d_ref[0, p, 0, :, h * t:(h + 1) * t], None, None),
                 refs)

    lv = lamv_ref[...]
    lam = (jnp.exp(jnp.sum(lv[0:1] * lv[1:2], axis=1, keepdims=True))
           - jnp.exp(jnp.sum(lv[2:3] * lv[3:4], axis=1, keepdims=True)) + LAM_INIT)
    for p in range(N_PAIRS):
        acc = acc_ref[p, 0:LANES, :]
        l = acc_ref[p, LANES:LANES + 1, :]
        o = acc[:, 0:t] / l[:, 0:t] - lam * (acc[:, t:2 * t] / l[:, t:2 * t])
        y = o * lax.rsqrt(jnp.mean(o * o, axis=0, keepdims=True) + RMS_EPS) * g_ref[...]
        o_ref[0, :, p * LANES:(p + 1) * LANES] = (y * (1.0 - LAM_INIT)).T.astype(BF16)


def _moba_kernel(tab_ref, q_ref, k_ref, v_ref, d_ref, o_ref,
                 q2_ref, s_ref, m_ref, acc_ref, vt_ref, kmean_ref, sel_ref):
    t = ATT_T
    i = pl.program_id(1)
    nb = k_ref.shape[1] // t
    nbp = kmean_ref.shape[1]

    @pl.when(i == 0)
    def _():
        _transpose_values(v_ref, vt_ref)
        kmean_ref[...] = jnp.zeros(kmean_ref.shape, F32)
        for p in range(N_PAIRS):
            for n in range(nb):
                blk = k_ref[0, n * t:(n + 1) * t, p * LANES:(p + 1) * LANES].astype(F32)
                kmean_ref[p, n:n + 1, :] = jnp.sum(blk, axis=0, keepdims=True) * (1.0 / t)

    _stack_q(q_ref, q2_ref, MOBA_HEAD_DIM ** -0.5)
    _flash_init(m_ref, acc_ref)

    blk_id = lax.broadcasted_iota(I32, (nbp, 2 * t), 0)
    blk_f = blk_id.astype(F32)
    for p in range(N_PAIRS):
        gs = jnp.dot(kmean_ref[p].astype(BF16), q2_ref[p], preferred_element_type=F32)
        g = jnp.where(blk_id < i, gs, -jnp.inf)
        sel = jnp.full((nbp, 2 * t), NEG, F32)
        for _ in range(MOBA_TOPK):
            mx = jnp.max(g, axis=0, keepdims=True)
            first = jnp.min(jnp.where(g == mx, blk_f, float(nbp)), axis=0, keepdims=True)
            hit = blk_f == first
            sel = jnp.where(hit, 0.0, sel)
            g = jnp.where(hit, -jnp.inf, g)
        sel_ref[p] = sel

    def live(p, h, n):
        return sel_ref[p, pl.ds(n, 1), h * t:(h + 1) * t] > -1.0

    refs = (q2_ref, k_ref, vt_ref, s_ref, m_ref, acc_ref)
    _flash_sweep(i,
                 lambda p, h, n: (None, _far_bias(tab_ref, N_DIFF_HEADS + 2 * p + h),
                                  live(p, h, n)),
                 lambda p, h: (d_ref[0, p, 1, :, h * t:(h + 1) * t], None, live(p, h, i - 1)),
                 lambda p, h: (d_ref[0, p, 0, :, h * t:(h + 1) * t], None, None),
                 refs)

    row = lax.broadcasted_iota(I32, (LANES, t), 0)
    for p in range(N_PAIRS):
        acc = acc_ref[p, 0:LANES, :]
        l = acc_ref[p, LANES:LANES + 1, :]
        o = jnp.where(row < 64, acc[:, 0:t] / l[:, 0:t], acc[:, t:2 * t] / l[:, t:2 * t])
        o_ref[0, :, p * LANES:(p + 1) * LANES] = o.T.astype(BF16)


def _attention(kind, tab, proj, dtiles, extra, cols):
    b, s, _ = proj.shape
    t = ATT_T
    nq = s // t
    w = N_PAIRS * LANES
    qc, kc, vc = cols
    gidx = 0 if kind == "diff" else 1
    in_specs = [pl.BlockSpec(memory_space=pltpu.SMEM),
                pl.BlockSpec((1, t, w), lambda bi, i: (bi, i, qc // w)),
                pl.BlockSpec((1, s, w), lambda bi, i: (bi, 0, kc // w)),
                pl.BlockSpec((1, s, w), lambda bi, i: (bi, 0, vc // w)),
                pl.BlockSpec((1, N_PAIRS, 2, t, 2 * t), lambda bi, i: (gidx, 0, 0, 0, 0))]
    scratch = [pltpu.VMEM((N_PAIRS, LANES, 2 * t), BF16),
               pltpu.VMEM((2 * 2 * N_PAIRS, t, t), F32),
               pltpu.VMEM((N_PAIRS, 1, 2 * t), F32),
               pltpu.VMEM((N_PAIRS, V_ROWS, 2 * t), F32),
               pltpu.VMEM((nq, N_PAIRS * V_ROWS, t), BF16)]
    if kind == "diff":
        lamv, subln_g = extra
        in_specs += [pl.BlockSpec(lamv.shape, lambda bi, i: (0, 0)),
                     pl.BlockSpec(subln_g.shape, lambda bi, i: (0, 0))]
        args = (tab, proj, proj, proj, dtiles, lamv, subln_g)
        body = _diff_kernel
    else:
        nbp = max(32, nq)
        scratch += [pltpu.VMEM((N_PAIRS, nbp, LANES), F32), pltpu.VMEM((N_PAIRS, nbp, 2 * t), F32)]
        args = (tab, proj, proj, proj, dtiles)
        body = _moba_kernel
    return pl.pallas_call(
        body,
        out_shape=jax.ShapeDtypeStruct((b, s, w), BF16),
        grid=(b, nq),
        in_specs=in_specs,
        out_specs=pl.BlockSpec((1, t, w), lambda bi, i: (bi, i, 0)),
        scratch_shapes=scratch,
        compiler_params=pltpu.CompilerParams(
            dimension_semantics=("arbitrary", "arbitrary"),
            vmem_limit_bytes=VMEM_LIMIT_BYTES),
        name=kind + "_attn",
    )(*args)


def _merge_kernel(x_ref, ya_ref, yb_ref, gla_ref, glb_ref, wa_ref, wb_ref, wo_ref,
                  gt_ref, shf_ref, scf_ref, gf_ref, wr_ref, br_ref,
                  x1_ref, h2_ref, ti_ref, tw_ref):
    a = jnp.dot(ya_ref[0], wa_ref[...], preferred_element_type=F32)
    b = jnp.dot(yb_ref[0], wb_ref[...], preferred_element_type=F32)
    merged = (jax.nn.sigmoid(gla_ref[0].astype(F32)) * a
              + jax.nn.sigmoid(glb_ref[0].astype(F32)) * b)
    mo = jnp.dot(merged.astype(BF16), wo_ref[...], preferred_element_type=F32)
    x1 = x_ref[0] + gt_ref[0] * mo
    x1_ref[0] = x1
    h = _rms(x1, gf_ref[...]) * (1.0 + scf_ref[0]) + shf_ref[0]
    _store_token_tiles(h2_ref.at[0], h)
    logits = jnp.dot(h.astype(BF16), wr_ref[...], preferred_element_type=F32) + br_ref[...]
    tm = logits.shape[0]
    lane = lax.broadcasted_iota(I32, (tm, LANES), 1)
    lane_f = lane.astype(F32)
    g = jnp.where(lane < N_EXPERTS, logits, -jnp.inf)
    vals, idxs = [], []
    for _ in range(TOP_K):
        mx = jnp.max(g, axis=1, keepdims=True)
        first = jnp.min(jnp.where(g == mx, lane_f, float(LANES)), axis=1, keepdims=True)
        vals.append(mx)
        idxs.append(first)
        g = jnp.where(lane_f == first, -jnp.inf, g)
    es = [jnp.exp(v - vals[0]) for v in vals]
    den = es[0] + es[1] + es[2] + es[3]
    ti = jnp.zeros((tm, LANES), F32)
    tw = jnp.zeros((tm, LANES), F32)
    for k in range(TOP_K):
        ti = jnp.where(lane == k, idxs[k], ti)
        tw = jnp.where(lane == k, es[k] / den, tw)
    ti_ref[0] = ti.astype(I32)
    tw_ref[0] = tw


def _merge(x, proj, y_a, y_b, w_br_a, w_br_b, w_out, mod, g_ffn, w_router, b_router):
    b, s, d = x.shape
    tm = min(MERGE_TM, s)
    full = lambda shape: pl.BlockSpec(shape, lambda bi, i: (0,) * len(shape))
    tile = lambda w, c=0: pl.BlockSpec((1, tm, w), lambda bi, i: (bi, i, c))
    modspec = lambda j: pl.BlockSpec((1, 1, d), lambda bi, i: (bi * 6 + j, 0, 0))
    return pl.pallas_call(
        _merge_kernel,
        out_shape=(jax.ShapeDtypeStruct((b, s, d), F32),
                   jax.ShapeDtypeStruct((b, s * ROW_CHUNKS, LANES), F32),
                   jax.ShapeDtypeStruct((b, s, LANES), I32),
                   jax.ShapeDtypeStruct((b, s, LANES), F32)),
        grid=(b, s // tm),
        in_specs=[tile(d), tile(512), tile(512), tile(d, 3), tile(d, 4),
                  full(w_br_a.shape), full(w_br_b.shape), full(w_out.shape),
                  modspec(2), modspec(3), modspec(4), full(g_ffn.shape),
                  full(w_router.shape), full(b_router.shape)],
        out_specs=(tile(d), pl.BlockSpec((1, tm * ROW_CHUNKS, LANES), lambda bi, i: (bi, i, 0)),
                   tile(LANES), tile(LANES)),
        compiler_params=pltpu.CompilerParams(dimension_semantics=("arbitrary", "arbitrary"),
                                             vmem_limit_bytes=VMEM_LIMIT_BYTES),
        name="merge",
    )(x, y_a, y_b, proj, proj, w_br_a, w_br_b, w_out, mod, mod, mod, g_ffn,
      w_router, b_router)


def _dispatch_kernel(padtile_ref, pos_ref, h2_ref, xs_ref, zeros_ref, zsem, sem):
    step = pl.program_id(0)
    rc = ROW_CHUNKS
    n_tok = h2_ref.shape[0] // rc
    pad_rows = zeros_ref.shape[0]

    def zero_fill(e):
        start = pl.multiple_of(padtile_ref[e] * pad_rows, pad_rows)
        return pltpu.make_async_copy(zeros_ref, xs_ref.at[pl.ds(start, pad_rows)], zsem)

    @pl.when(step == 0)
    def _():
        zeros_ref[...] = jnp.zeros(zeros_ref.shape, F32)
        for e in range(N_EXPERTS):
            @pl.when(padtile_ref[e] >= 0)
            def _():
                zero_fill(e).start()
        for e in range(N_EXPERTS):
            @pl.when(padtile_ref[e] >= 0)
            def _():
                zero_fill(e).wait()

    def issue(j, carry):
        src = h2_ref.at[pl.ds(pl.multiple_of(j * rc, rc), rc)]
        for k in range(TOP_K):
            dst = xs_ref.at[pl.ds(pl.multiple_of(pos_ref[j * TOP_K + k], rc), rc)]
            pltpu.make_async_copy(src, dst, sem).start(priority=k % 2)
        return carry

    lax.fori_loop(0, n_tok, issue, 0, unroll=ROW_DMA_UNROLL)
    for k in range(TOP_K):
        pltpu.make_async_copy(h2_ref, xs_ref.at[pl.ds(0, n_tok * rc)], sem).wait()


def _dispatch(h2_tiles, pos_start, padtile, n_rows):
    rc = ROW_CHUNKS
    t_tok = h2_tiles.shape[0] // rc
    tm = min(DISPATCH_TM, t_tok)
    return pl.pallas_call(
        _dispatch_kernel,
        out_shape=jax.ShapeDtypeStruct((n_rows * rc, LANES), F32),
        grid_spec=pltpu.PrefetchScalarGridSpec(
            num_scalar_prefetch=1,
            grid=(t_tok // tm,),
            in_specs=[pl.BlockSpec((tm * TOP_K,), lambda s, pt: (s,), memory_space=pltpu.SMEM),
                      pl.BlockSpec((tm * rc, LANES), lambda s, pt: (s, 0))],
            out_specs=pl.BlockSpec(memory_space=pl.ANY),
            scratch_shapes=[pltpu.VMEM((MOE_TM * rc, LANES), F32),
                            pltpu.SemaphoreType.DMA, pltpu.SemaphoreType.DMA]),
        compiler_params=pltpu.CompilerParams(dimension_semantics=("arbitrary",),
                                             vmem_limit_bytes=VMEM_LIMIT_BYTES),
        name="dispatch",
    )(padtile, pos_start, h2_tiles)


def _moe_kernel(te_ref, nu_ref, xs_ref, wgu_ref, bgu_ref, wd_ref, bd_ref,
                ys_ref, wgu_bf, wd_bf, gu_ref):
    r = pl.program_id(0)
    busy = r < nu_ref[0]
    fresh = jnp.logical_or(r == 0, te_ref[r] != te_ref[jnp.maximum(r - 1, 0)])

    @pl.when(jnp.logical_and(busy, fresh))
    def _():
        wgu_bf[...] = wgu_ref[0].astype(BF16)
        wd_bf[...] = wd_ref[0].astype(BF16)

    @pl.when(busy)
    def _():
        x = _load_token_tiles(xs_ref).astype(BF16)
        fc = MOE_FF_CHUNK
        for c in range(2 * D_FF // fc):
            lo, hi = c * fc, (c + 1) * fc
            gu_ref[:, lo:hi] = (jnp.dot(x, wgu_bf[:, lo:hi], preferred_element_type=F32)
                                + bgu_ref[0][:, lo:hi])
        y = None
        for c in range(D_FF // fc):
            lo, hi = c * fc, (c + 1) * fc
            gate = jnp.minimum(gu_ref[:, lo:hi], SWIGLU_LIMIT)
            up = jnp.clip(gu_ref[:, D_FF + lo:D_FF + hi], -SWIGLU_LIMIT, SWIGLU_LIMIT)
            act = (up + 1.0) * gate * jax.nn.sigmoid(SWIGLU_ALPHA * gate)
            part = jnp.dot(act.astype(BF16), wd_bf[lo:hi, :], preferred_element_type=F32)
            y = part if y is None else y + part
        _store_token_tiles(ys_ref, y + bd_ref[0])


def _moe(xs_tiles, tile_expert, n_used, w_gate_up, b_gate_up, w_down, b_down):
    d = D_MODEL
    rc = ROW_CHUNKS
    tm = MOE_TM
    n_tiles = tile_expert.shape[0]
    e, _, f2 = w_gate_up.shape
    wsel = lambda r, te, nu: (te[r], 0, 0)
    rows = lambda r, te, nu: (jnp.minimum(r, nu[0] - 1), 0)
    return pl.pallas_call(
        _moe_kernel,
        out_shape=jax.ShapeDtypeStruct((n_tiles * tm * rc, LANES), F32),
        grid_spec=pltpu.PrefetchScalarGridSpec(
            num_scalar_prefetch=2,
            grid=(n_tiles,),
            in_specs=[pl.BlockSpec((tm * rc, LANES), rows),
                      pl.BlockSpec((1, d, f2), wsel),
                      pl.BlockSpec((1, 1, f2), wsel),
                      pl.BlockSpec((1, D_FF, d), wsel),
                      pl.BlockSpec((1, 1, d), wsel)],
            out_specs=pl.BlockSpec((tm * rc, LANES), rows),
            scratch_shapes=[pltpu.VMEM((d, f2), BF16), pltpu.VMEM((D_FF, d), BF16),
                            pltpu.VMEM((tm, f2), F32)]),
        compiler_params=pltpu.CompilerParams(dimension_semantics=("arbitrary",),
                                             vmem_limit_bytes=VMEM_LIMIT_BYTES),
        name="moe",
    )(tile_expert, n_used, xs_tiles, w_gate_up, b_gate_up, w_down, b_down)


def _combine_kernel(pos_ref, ys_ref, x1_ref, tw_ref, gt_ref, gf_ref, o_ref, gbuf, sem):
    tm = x1_ref.shape[0]
    rc = ROW_CHUNKS

    def issue(j, carry):
        for k in range(TOP_K):
            src = ys_ref.at[pl.ds(pl.multiple_of(pos_ref[j * TOP_K + k], rc), rc)]
            dst = gbuf.at[k, pl.ds(pl.multiple_of(j * rc, rc), rc)]
            pltpu.make_async_copy(src, dst, sem).start(priority=k % 2)
        return carry

    lax.fori_loop(0, tm, issue, 0, unroll=ROW_DMA_UNROLL)
    for k in range(TOP_K):
        pltpu.make_async_copy(ys_ref.at[pl.ds(0, tm * rc)], gbuf.at[k], sem).wait()

    tw = tw_ref[...]
    acc = tw[:, 0:1] * _load_token_tiles(gbuf.at[0])
    for k in range(1, TOP_K):
        acc = acc + tw[:, k:k + 1] * _load_token_tiles(gbuf.at[k])
    x2 = x1_ref[...] + gt_ref[0] * acc
    o_ref[...] = _rms(x2, gf_ref[...])


def _combine(ys, pos_flat, x1, tw, mod, g_final, seq):
    t, d = x1.shape
    tm = min(COMBINE_TM, t)
    return pl.pallas_call(
        _combine_kernel,
        out_shape=jax.ShapeDtypeStruct((t, d), F32),
        grid=(t // tm,),
        in_specs=[pl.BlockSpec((tm * TOP_K,), lambda s: (s,), memory_space=pltpu.SMEM),
                  pl.BlockSpec(memory_space=pl.ANY),
                  pl.BlockSpec((tm, d), lambda s: (s, 0)),
                  pl.BlockSpec((tm, LANES), lambda s: (s, 0)),
                  pl.BlockSpec((1, 1, d), lambda s: ((s * tm) // seq * 6 + 5, 0, 0)),
                  pl.BlockSpec((1, d), lambda s: (0, 0))],
        out_specs=pl.BlockSpec((tm, d), lambda s: (s, 0)),
        scratch_shapes=[pltpu.VMEM((TOP_K, tm * ROW_CHUNKS, LANES), F32),
                        pltpu.SemaphoreType.DMA],
        compiler_params=pltpu.CompilerParams(dimension_semantics=("arbitrary",),
                                             vmem_limit_bytes=VMEM_LIMIT_BYTES),
        name="combine",
    )(pos_flat, ys, x1, tw, mod, g_final)


def _routing_tables(top_idx, n_tiles):
    tm = MOE_TM
    e_flat = top_idx.reshape(-1)
    onehot = (e_flat[:, None] == jnp.arange(N_EXPERTS, dtype=I32)[None, :]).astype(I32)
    csum = jnp.cumsum(onehot, axis=0)
    counts = csum[-1]
    rank = jnp.sum(csum * onehot, axis=1) - 1
    tiles_per = (counts + tm - 1) // tm
    tile_end = jnp.cumsum(tiles_per)
    tile_start = tile_end - tiles_per
    pos = (jnp.sum(onehot * tile_start[None, :], axis=1) * tm + rank).astype(I32)
    n_used = tile_end[-1:].astype(I32)
    tile_ids = jnp.arange(n_tiles, dtype=I32)
    tile_expert = jnp.minimum(
        jnp.sum((tile_ids[:, None] >= tile_end[None, :]).astype(I32), axis=1),
        N_EXPERTS - 1).astype(I32)
    padtile = jnp.where(counts % tm != 0, tile_end - 1, -1).astype(I32)
    return pos, tile_expert, n_used, padtile


def kernel(x, c, rel_bias, w_ada, b_ada, g_mix, w_in, lambda_q1, lambda_k1, lambda_q2,
           lambda_k2, subln_g, w_br_a, w_br_b, w_out, g_ffn, w_router, b_router,
           w_gate_up, b_gate_up, w_down, b_down, g_final):
    b, s, d = x.shape
    assert d == D_MODEL and s % ATT_T == 0 and MOBA_BLOCK == ATT_T
    t_tok = b * s

    c8 = jnp.zeros((8, d), F32).at[:b].set(c)
    ada = _ada(c8, w_ada[0], b_ada[0].reshape(1, -1))
    mod = ada[:b].reshape(b * 6, 1, d)

    dtiles = _bias_tiles(rel_bias)
    tab = rel_bias.T.reshape(-1)

    proj = _proj(x, g_mix[0].reshape(1, d), mod, w_in[0].astype(BF16))

    lamv = jnp.concatenate([lambda_q1, lambda_k1, lambda_q2, lambda_k2], axis=0)
    y_a = _attention("diff", tab, proj, dtiles, (lamv, subln_g[0].reshape(-1, 1)), (0, 512, 1024))
    y_b = _attention("moba", tab, proj, dtiles, None, (1536, 2048, 2560))

    wr = jnp.zeros((d, LANES), BF16).at[:, :N_EXPERTS].set(w_router[0].astype(BF16))
    br = jnp.zeros((1, LANES), F32).at[0, :N_EXPERTS].set(b_router[0])
    x1, h2, ti, tw = _merge(x, proj, y_a, y_b, w_br_a[0].astype(BF16), w_br_b[0].astype(BF16),
                            w_out[0].astype(BF16), mod, g_ffn[0].reshape(1, d), wr, br)

    n_tiles = (t_tok * TOP_K) // MOE_TM + N_EXPERTS
    top_idx = ti.reshape(t_tok, LANES)[:, :TOP_K]
    pos, tile_expert, n_used, padtile = _routing_tables(top_idx, n_tiles)
    pos_start = pos * ROW_CHUNKS

    xs = _dispatch(h2.reshape(t_tok * ROW_CHUNKS, LANES), pos_start, padtile, n_tiles * MOE_TM)
    ys = _moe(xs, tile_expert, n_used, w_gate_up[0], b_gate_up[0].reshape(N_EXPERTS, 1, -1),
              w_down[0], b_down[0].reshape(N_EXPERTS, 1, -1))
    out = _combine(ys, pos_start, x1.reshape(t_tok, d), tw.reshape(t_tok, LANES), mod,
                   g_final.reshape(1, d), s)
    return out.reshape(b, s, d)
```

```python
import functools
import math

import jax
import jax.numpy as jnp
import numpy as np
from jax import lax
from jax.experimental import pallas as pl
from jax.experimental.pallas import tpu as pltpu

F32 = jnp.float32
BF16 = jnp.bfloat16
I32 = jnp.int32

D_MODEL = 1024
N_DIFF_HEADS = 4
DIFF_HEAD_DIM = 64
N_MOBA_HEADS = 8
MOBA_HEAD_DIM = 64
MOBA_BLOCK = 256
MOBA_TOPK = 3
N_HEADS_TOTAL = N_DIFF_HEADS + N_MOBA_HEADS
N_BUCKETS = 32
MAX_DISTANCE = 128
N_EXPERTS = 32
TOP_K = 4
D_FF = D_MODEL
SWIGLU_LIMIT = 7.0
SWIGLU_ALPHA = 1.702
RMS_EPS = 1e-5
LAM_INIT = 0.8 - 0.6 * math.exp(-0.3 * 0)
IN_WIDTH = 5120

LANES = 128
VMEM_LIMIT_BYTES = 56 * 1024 * 1024

ATT_T = 256
N_PAIRS = 4
V_ROWS = LANES + 16
PROJ_TM = 512
MERGE_TM = 512
MOE_TM = 512
MOE_FF_CHUNK = 256
DISPATCH_TM = 512
ROW_DMA_UNROLL = 8
COMBINE_TM = 256
NEG = -1e30


def _t5_bucket_np(dist):
    n = np.maximum(dist, 0)
    max_exact = N_BUCKETS // 2
    nf = np.maximum(n, max_exact).astype(np.float32)
    large = max_exact + (np.log(nf / max_exact) / math.log(MAX_DISTANCE / max_exact)
                         * (N_BUCKETS - max_exact)).astype(np.int32)
    large = np.minimum(large, N_BUCKETS - 1)
    return np.where(n < max_exact, n, large).astype(np.int32)


def _rms(x, g):
    return x * lax.rsqrt(jnp.mean(x * x, axis=-1, keepdims=True) + RMS_EPS) * g


ROW_CHUNKS = D_MODEL // LANES


def _store_token_tiles(ref, x):
    n = x.shape[0]
    for c in range(ROW_CHUNKS):
        ref[pl.ds(c, n, stride=ROW_CHUNKS), :] = x[:, c * LANES:(c + 1) * LANES]


def _load_token_tiles(ref):
    n = ref.shape[0] // ROW_CHUNKS
    return jnp.concatenate([ref[pl.ds(c, n, stride=ROW_CHUNKS), :] for c in range(ROW_CHUNKS)],
                           axis=1)


def _ada_kernel(c_ref, w_ref, b_ref, o_ref):
    o_ref[...] = jnp.dot(c_ref[...].astype(BF16), w_ref[...].astype(BF16),
                         preferred_element_type=F32) + b_ref[...]


def _ada(c8, w, b):
    d, n = w.shape
    tn = 1536
    return pl.pallas_call(
        _ada_kernel,
        out_shape=jax.ShapeDtypeStruct((8, n), F32),
        grid=(n // tn,),
        in_specs=[pl.BlockSpec((8, d), lambda j: (0, 0)),
                  pl.BlockSpec((d, tn), lambda j: (0, j)),
                  pl.BlockSpec((1, tn), lambda j: (0, j))],
        out_specs=pl.BlockSpec((8, tn), lambda j: (0, j)),
        compiler_params=pltpu.CompilerParams(dimension_semantics=("arbitrary",),
                                             vmem_limit_bytes=VMEM_LIMIT_BYTES),
        name="ada",
    )(c8, w, b)


def _bias_kernel(tab_ref, bd_ref, bp_ref, o_ref):
    g = pl.program_id(0)
    p = pl.program_id(1)
    t = pl.program_id(2)
    head = jnp.where(g == 0, p, N_DIFF_HEADS + 2 * p + t)
    bd = bd_ref[...]
    bp = bp_ref[...]
    d0 = jnp.where(bd < 0, NEG, 0.0).astype(F32)
    d1 = jnp.zeros(bp.shape, F32)
    for b in range(N_BUCKETS):
        val = tab_ref[head * N_BUCKETS + b]
        d0 = jnp.where(bd == b, val, d0)
        d1 = jnp.where(bp == b, val, d1)
    o_ref[0, 0, 0] = d0
    o_ref[0, 0, 1] = d1


def _bias_tiles(rel_bias):
    t = ATT_T
    jj = np.arange(t)[:, None]
    ii = np.arange(t)[None, :]
    bd = np.where(ii >= jj, _t5_bucket_np(ii - jj), -1).astype(np.int32)
    bp = _t5_bucket_np(t + ii - jj)
    tab = rel_bias.T.reshape(-1)
    return pl.pallas_call(
        _bias_kernel,
        out_shape=jax.ShapeDtypeStruct((2, N_PAIRS, 2, t, 2 * t), F32),
        grid=(2, N_PAIRS, 2),
        in_specs=[pl.BlockSpec(memory_space=pltpu.SMEM),
                  pl.BlockSpec((t, t), lambda g, p, h: (0, 0)),
                  pl.BlockSpec((t, t), lambda g, p, h: (0, 0))],
        out_specs=pl.BlockSpec((1, 1, 2, t, t), lambda g, p, h: (g, p, 0, 0, h)),
        compiler_params=pltpu.CompilerParams(
            dimension_semantics=("arbitrary", "arbitrary", "arbitrary")),
        name="bias_tiles",
    )(tab, jnp.asarray(bd), jnp.asarray(bp))


def _proj_kernel(x_ref, g_ref, sh_ref, sc_ref, w_ref, o_ref):
    x = x_ref[0]
    h = _rms(x, g_ref[...]) * (1.0 + sc_ref[0]) + sh_ref[0]
    hb = h.astype(BF16)
    n_total = w_ref.shape[1]
    for n0 in range(0, n_total, 512):
        o_ref[0, :, n0:n0 + 512] = jnp.dot(
            hb, w_ref[:, n0:n0 + 512], preferred_element_type=F32).astype(BF16)


def _proj(x, g_mix, mod, w_in_bf):
    b, s, d = x.shape
    n = w_in_bf.shape[1]
    tm = min(PROJ_TM, s)
    return pl.pallas_call(
        _proj_kernel,
        out_shape=jax.ShapeDtypeStruct((b, s, n), BF16),
        grid=(b, s // tm),
        in_specs=[pl.BlockSpec((1, tm, d), lambda bi, i: (bi, i, 0)),
                  pl.BlockSpec((1, d), lambda bi, i: (0, 0)),
                  pl.BlockSpec((1, 1, d), lambda bi, i: (bi * 6 + 0, 0, 0)),
                  pl.BlockSpec((1, 1, d), lambda bi, i: (bi * 6 + 1, 0, 0)),
                  pl.BlockSpec((d, n), lambda bi, i: (0, 0))],
        out_specs=pl.BlockSpec((1, tm, n), lambda bi, i: (bi, i, 0)),
        compiler_params=pltpu.CompilerParams(dimension_semantics=("arbitrary", "arbitrary"),
                                             vmem_limit_bytes=VMEM_LIMIT_BYTES),
        name="proj",
    )(x, g_mix, mod, mod, w_in_bf)


def _stack_q(q_ref, q2_ref, scale):
    t = ATT_T
    row = lax.broadcasted_iota(I32, (LANES, t), 0)
    for p in range(N_PAIRS):
        q = (q_ref[0, :, p * LANES:(p + 1) * LANES].astype(F32) * scale).T
        q2_ref[p, :, 0:t] = jnp.where(row < 64, q, 0.0).astype(BF16)
        q2_ref[p, :, t:2 * t] = jnp.where(row >= 64, q, 0.0).astype(BF16)


def _transpose_values(v_ref, vt_ref):
    t = ATT_T
    nq = vt_ref.shape[0]

    def body(n, carry):
        for p in range(N_PAIRS):
            blk = v_ref[0, pl.ds(pl.multiple_of(n * t, t), t), p * LANES:(p + 1) * LANES]
            vt_ref[n, p * V_ROWS:p * V_ROWS + LANES, :] = blk.astype(F32).T.astype(BF16)
            vt_ref[n, p * V_ROWS + LANES:(p + 1) * V_ROWS, :] = jnp.ones((V_ROWS - LANES, t), BF16)
        return carry

    lax.fori_loop(0, nq, body, 0)


def _flash_init(m_ref, acc_ref):
    m_ref[...] = jnp.full(m_ref.shape, NEG, F32)
    acc_ref[...] = jnp.zeros(acc_ref.shape, F32)


def _score_slot(p, h, bank):
    return bank * (2 * N_PAIRS) + 2 * p + h


def _flash_scores(p, h, n, bank, q2_ref, k_ref, s_ref):
    t = ATT_T
    kt = k_ref[0, pl.ds(pl.multiple_of(n * t, t), t), p * LANES:(p + 1) * LANES]
    s_ref[_score_slot(p, h, bank)] = jnp.dot(kt, q2_ref[p, :, h * t:(h + 1) * t],
                                             preferred_element_type=F32)


def _flash_absorb(p, h, n, bank, tile_bias, row_bias, live, v_ref, s_ref, m_ref, acc_ref):
    t = ATT_T
    cs = slice(h * t, (h + 1) * t)
    vt = v_ref[n, p * V_ROWS:(p + 1) * V_ROWS, :]
    s = s_ref[_score_slot(p, h, bank)]
    if tile_bias is not None:
        s = s + tile_bias
    mx = jnp.max(s, axis=0, keepdims=True)
    if row_bias is not None:
        mx = mx + row_bias
    if live is not None:
        mx = jnp.where(live, mx, NEG)
    m_prev = m_ref[p, :, cs]
    m_new = jnp.maximum(m_prev, mx)
    alpha = jnp.exp(m_prev - m_new)
    shift = m_new if row_bias is None else m_new - row_bias
    if live is not None:
        shift = jnp.where(live, shift, -NEG)
    pt = jnp.exp(s - shift)
    acc_ref[p, :, cs] = alpha * acc_ref[p, :, cs] + jnp.dot(vt, pt.astype(BF16),
                                                            preferred_element_type=F32)
    m_ref[p, :, cs] = m_new


def _far_bias(tab_ref, head):
    return tab_ref[head * N_BUCKETS + (N_BUCKETS - 1)]


def _flash_sweep(i, far_args, prev_args, own_args, refs):
    q2_ref, k_ref, v_ref, s_ref, m_ref, acc_ref = refs
    chains = [(p, h) for p in range(N_PAIRS) for h in range(2)]

    def scores_tile(n, bank):
        for p, h in chains:
            _flash_scores(p, h, n, bank, q2_ref, k_ref, s_ref)

    def step(n, bank, args, with_next):
        for p, h in chains:
            if with_next:
                _flash_scores(p, h, n + 1, 1 - bank, q2_ref, k_ref, s_ref)
            _flash_absorb(p, h, n, bank, *args(p, h), v_ref, s_ref, m_ref, acc_ref)

    def far_step(n, bank):
        step(n, bank, lambda p, h: far_args(p, h, n), True)

    n_far = jnp.maximum(i - 1, 0)
    lead = jnp.bitwise_and(n_far, 1)

    @pl.when(jnp.bitwise_and(i, 1) == 0)
    def _():
        scores_tile(0, 0)

    @pl.when(jnp.bitwise_and(i, 1) == 1)
    def _():
        scores_tile(0, 1)

    @pl.when(lead == 1)
    def _():
        far_step(0, 0)

    def pair_body(j, carry):
        n = lead + 2 * j
        far_step(n, 1)
        far_step(n + 1, 0)
        return carry

    lax.fori_loop(0, jnp.right_shift(n_far, 1), pair_body, 0)

    @pl.when(i >= 1)
    def _():
        step(i - 1, 1, prev_args, True)

    step(i, 0, own_args, False)


def _diff_kernel(tab_ref, q_ref, k_ref, v_ref, d_ref, lamv_ref, g_ref, o_ref,
                 q2_ref, s_ref, m_ref, acc_ref, vt_ref):
    t = ATT_T
    i = pl.program_id(1)

    @pl.when(i == 0)
    def _():
        _transpose_values(v_ref, vt_ref)

    _stack_q(q_ref, q2_ref, DIFF_HEAD_DIM ** -0.5)
    _flash_init(m_ref, acc_ref)
    refs = (q2_ref, k_ref, vt_ref, s_ref, m_ref, acc_ref)
    _flash_sweep(i,
                 lambda p, h, n: (None, _far_bias(tab_ref, p), None),
                 lambda p, h: (d_ref[0, p, 1, :, h * t:(h + 1) * t], None, None),
                 lambda p, h: (d_ref[0, p, 0, :, h * t:(h + 1) * t], None, None),
                 refs)

    lv = lamv_ref[...]
    lam = (jnp.exp(jnp.sum(lv[0:1] * lv[1:2], axis=1, keepdims=True))
           - jnp.exp(jnp.sum(lv[2:3] * lv[3:4], axis=1, keepdims=True)) + LAM_INIT)
    for p in range(N_PAIRS):
        acc = acc_ref[p, 0:LANES, :]
        l = acc_ref[p, LANES:LANES + 1, :]
        o = acc[:, 0:t] / l[:, 0:t] - lam * (acc[:, t:2 * t] / l[:, t:2 * t])
        y = o * lax.rsqrt(jnp.mean(o * o, axis=0, keepdims=True) + RMS_EPS) * g_ref[...]
        o_ref[0, :, p * LANES:(p + 1) * LANES] = (y * (1.0 - LAM_INIT)).T.astype(BF16)


def _moba_kernel(tab_ref, q_ref, k_ref, v_ref, d_ref, o_ref,
                 q2_ref, s_ref, m_ref, acc_ref, vt_ref, kmean_ref, sel_ref):
    t = ATT_T
    i = pl.program_id(1)
    nb = k_ref.shape[1] // t
    nbp = kmean_ref.shape[1]

    @pl.when(i == 0)
    def _():
        _transpose_values(v_ref, vt_ref)
        kmean_ref[...] = jnp.zeros(kmean_ref.shape, F32)
        for p in range(N_PAIRS):
            for n in range(nb):
                blk = k_ref[0, n * t:(n + 1) * t, p * LANES:(p + 1) * LANES].astype(F32)
                kmean_ref[p, n:n + 1, :] = jnp.sum(blk, axis=0, keepdims=True) * (1.0 / t)

    _stack_q(q_ref, q2_ref, MOBA_HEAD_DIM ** -0.5)
    _flash_init(m_ref, acc_ref)

    blk_id = lax.broadcasted_iota(I32, (nbp, 2 * t), 0)
    blk_f = blk_id.astype(F32)
    for p in range(N_PAIRS):
        gs = jnp.dot(kmean_ref[p].astype(BF16), q2_ref[p], preferred_element_type=F32)
        g = jnp.where(blk_id < i, gs, -jnp.inf)
        sel = jnp.full((nbp, 2 * t), NEG, F32)
        for _ in range(MOBA_TOPK):
            mx = jnp.max(g, axis=0, keepdims=True)
            first = jnp.min(jnp.where(g == mx, blk_f, float(nbp)), axis=0, keepdims=True)
            hit = blk_f == first
            sel = jnp.where(hit, 0.0, sel)
            g = jnp.where(hit, -jnp.inf, g)
        sel_ref[p] = sel

    def live(p, h, n):
        return sel_ref[p, pl.ds(n, 1), h * t:(h + 1) * t] > -1.0

    refs = (q2_ref, k_ref, vt_ref, s_ref, m_ref, acc_ref)
    _flash_sweep(i,
                 lambda p, h, n: (None, _far_bias(tab_ref, N_DIFF_HEADS + 2 * p + h),
                                  live(p, h, n)),
                 lambda p, h: (d_ref[0, p, 1, :, h * t:(h + 1) * t], None, live(p, h, i - 1)),
                 lambda p, h: (d_ref[0, p, 0, :, h * t:(h + 1) * t], None, None),
                 refs)

    row = lax.broadcasted_iota(I32, (LANES, t), 0)
    for p in range(N_PAIRS):
        acc = acc_ref[p, 0:LANES, :]
        l = acc_ref[p, LANES:LANES + 1, :]
        o = jnp.where(row < 64, acc[:, 0:t] / l[:, 0:t], acc[:, t:2 * t] / l[:, t:2 * t])
        o_ref[0, :, p * LANES:(p + 1) * LANES] = o.T.astype(BF16)


def _attention(kind, tab, proj, dtiles, extra, cols):
    b, s, _ = proj.shape
    t = ATT_T
    nq = s // t
    w = N_PAIRS * LANES
    qc, kc, vc = cols
    gidx = 0 if kind == "diff" else 1
    in_specs = [pl.BlockSpec(memory_space=pltpu.SMEM),
                pl.BlockSpec((1, t, w), lambda bi, i: (bi, i, qc // w)),
                pl.BlockSpec((1, s, w), lambda bi, i: (bi, 0, kc // w)),
                pl.BlockSpec((1, s, w), lambda bi, i: (bi, 0, vc // w)),
                pl.BlockSpec((1, N_PAIRS, 2, t, 2 * t), lambda bi, i: (gidx, 0, 0, 0, 0))]
    scratch = [pltpu.VMEM((N_PAIRS, LANES, 2 * t), BF16),
               pltpu.VMEM((2 * 2 * N_PAIRS, t, t), F32),
               pltpu.VMEM((N_PAIRS, 1, 2 * t), F32),
               pltpu.VMEM((N_PAIRS, V_ROWS, 2 * t), F32),
               pltpu.VMEM((nq, N_PAIRS * V_ROWS, t), BF16)]
    if kind == "diff":
        lamv, subln_g = extra
        in_specs += [pl.BlockSpec(lamv.shape, lambda bi, i: (0, 0)),
                     pl.BlockSpec(subln_g.shape, lambda bi, i: (0, 0))]
        args = (tab, proj, proj, proj, dtiles, lamv, subln_g)
        body = _diff_kernel
    else:
        nbp = max(32, nq)
        scratch += [pltpu.VMEM((N_PAIRS, nbp, LANES), F32), pltpu.VMEM((N_PAIRS, nbp, 2 * t), F32)]
        args = (tab, proj, proj, proj, dtiles)
        body = _moba_kernel
    return pl.pallas_call(
        body,
        out_shape=jax.ShapeDtypeStruct((b, s, w), BF16),
        grid=(b, nq),
        in_specs=in_specs,
        out_specs=pl.BlockSpec((1, t, w), lambda bi, i: (bi, i, 0)),
        scratch_shapes=scratch,
        compiler_params=pltpu.CompilerParams(
            dimension_semantics=("arbitrary", "arbitrary"),
            vmem_limit_bytes=VMEM_LIMIT_BYTES),
        name=kind + "_attn",
    )(*args)


def _merge_kernel(x_ref, ya_ref, yb_ref, gla_ref, glb_ref, wa_ref, wb_ref, wo_ref,
                  gt_ref, shf_ref, scf_ref, gf_ref, wr_ref, br_ref,
                  x1_ref, h2_ref, ti_ref, tw_ref):
    a = jnp.dot(ya_ref[0], wa_ref[...], preferred_element_type=F32)
    b = jnp.dot(yb_ref[0], wb_ref[...], preferred_element_type=F32)
    merged = (jax.nn.sigmoid(gla_ref[0].astype(F32)) * a
              + jax.nn.sigmoid(glb_ref[0].astype(F32)) * b)
    mo = jnp.dot(merged.astype(BF16), wo_ref[...], preferred_element_type=F32)
    x1 = x_ref[0] + gt_ref[0] * mo
    x1_ref[0] = x1
    h = _rms(x1, gf_ref[...]) * (1.0 + scf_ref[0]) + shf_ref[0]
    _store_token_tiles(h2_ref.at[0], h)
    logits = jnp.dot(h.astype(BF16), wr_ref[...], preferred_element_type=F32) + br_ref[...]
    tm = logits.shape[0]
    lane = lax.broadcasted_iota(I32, (tm, LANES), 1)
    lane_f = lane.astype(F32)
    g = jnp.where(lane < N_EXPERTS, logits, -jnp.inf)
    vals, idxs = [], []
    for _ in range(TOP_K):
        mx = jnp.max(g, axis=1, keepdims=True)
        first = jnp.min(jnp.where(g == mx, lane_f, float(LANES)), axis=1, keepdims=True)
        vals.append(mx)
        idxs.append(first)
        g = jnp.where(lane_f == first, -jnp.inf, g)
    es = [jnp.exp(v - vals[0]) for v in vals]
    den = es[0] + es[1] + es[2] + es[3]
    ti = jnp.zeros((tm, LANES), F32)
    tw = jnp.zeros((tm, LANES), F32)
    for k in range(TOP_K):
        ti = jnp.where(lane == k, idxs[k], ti)
        tw = jnp.where(lane == k, es[k] / den, tw)
    ti_ref[0] = ti.astype(I32)
    tw_ref[0] = tw


def _merge(x, proj, y_a, y_b, w_br_a, w_br_b, w_out, mod, g_ffn, w_router, b_router):
    b, s, d = x.shape
    tm = min(MERGE_TM, s)
    full = lambda shape: pl.BlockSpec(shape, lambda bi, i: (0,) * len(shape))
    tile = lambda w, c=0: pl.BlockSpec((1, tm, w), lambda bi, i: (bi, i, c))
    modspec = lambda j: pl.BlockSpec((1, 1, d), lambda bi, i: (bi * 6 + j, 0, 0))
    return pl.pallas_call(
        _merge_kernel,
        out_shape=(jax.ShapeDtypeStruct((b, s, d), F32),
                   jax.ShapeDtypeStruct((b, s * ROW_CHUNKS, LANES), F32),
                   jax.ShapeDtypeStruct((b, s, LANES), I32),
                   jax.ShapeDtypeStruct((b, s, LANES), F32)),
        grid=(b, s // tm),
        in_specs=[tile(d), tile(512), tile(512), tile(d, 3), tile(d, 4),
                  full(w_br_a.shape), full(w_br_b.shape), full(w_out.shape),
                  modspec(2), modspec(3), modspec(4), full(g_ffn.shape),
                  full(w_router.shape), full(b_router.shape)],
        out_specs=(tile(d), pl.BlockSpec((1, tm * ROW_CHUNKS, LANES), lambda bi, i: (bi, i, 0)),
                   tile(LANES), tile(LANES)),
        compiler_params=pltpu.CompilerParams(dimension_semantics=("arbitrary", "arbitrary"),
                                             vmem_limit_bytes=VMEM_LIMIT_BYTES),
        name="merge",
    )(x, y_a, y_b, proj, proj, w_br_a, w_br_b, w_out, mod, mod, mod, g_ffn,
      w_router, b_router)


def _dispatch_kernel(padtile_ref, nu_ref, pos_ref, h2_ref, xs_ref, zeros_ref, zsem, sem):
    step = pl.program_id(0)
    rc = ROW_CHUNKS
    n_tok = h2_ref.shape[0] // rc
    pad_rows = zeros_ref.shape[0]
    n_tiles = xs_ref.shape[0] // pad_rows

    def zero_tile(tile):
        start = pl.multiple_of(tile * pad_rows, pad_rows)
        return pltpu.make_async_copy(zeros_ref, xs_ref.at[pl.ds(start, pad_rows)], zsem)

    @pl.when(step == 0)
    def _():
        zeros_ref[...] = jnp.zeros(zeros_ref.shape, F32)
        for e in range(N_EXPERTS):
            @pl.when(padtile_ref[e] >= 0)
            def _():
                zero_tile(padtile_ref[e]).start()
        for e in range(N_EXPERTS):
            @pl.when(padtile_ref[e] >= 0)
            def _():
                zero_tile(padtile_ref[e]).wait()

        def start_tail(tile, carry):
            zero_tile(tile).start()
            return carry

        def wait_tail(tile, carry):
            zero_tile(tile).wait()
            return carry

        lax.fori_loop(nu_ref[0], n_tiles, start_tail, 0)
        lax.fori_loop(nu_ref[0], n_tiles, wait_tail, 0)

    def issue(j, carry):
        src = h2_ref.at[pl.ds(pl.multiple_of(j * rc, rc), rc)]
        for k in range(TOP_K):
            dst = xs_ref.at[pl.ds(pl.multiple_of(pos_ref[j * TOP_K + k], rc), rc)]
            pltpu.make_async_copy(src, dst, sem).start(priority=k % 2)
        return carry

    lax.fori_loop(0, n_tok, issue, 0, unroll=ROW_DMA_UNROLL)
    for k in range(TOP_K):
        pltpu.make_async_copy(h2_ref, xs_ref.at[pl.ds(0, n_tok * rc)], sem).wait()


def _dispatch(h2_tiles, pos_start, padtile, n_used, n_rows):
    rc = ROW_CHUNKS
    t_tok = h2_tiles.shape[0] // rc
    tm = min(DISPATCH_TM, t_tok)
    return pl.pallas_call(
        _dispatch_kernel,
        out_shape=jax.ShapeDtypeStruct((n_rows * rc, LANES), F32),
        grid_spec=pltpu.PrefetchScalarGridSpec(
            num_scalar_prefetch=2,
            grid=(t_tok // tm,),
            in_specs=[pl.BlockSpec((tm * TOP_K,), lambda s, pt, nu: (s,),
                                   memory_space=pltpu.SMEM),
                      pl.BlockSpec((tm * rc, LANES), lambda s, pt, nu: (s, 0))],
            out_specs=pl.BlockSpec(memory_space=pl.ANY),
            scratch_shapes=[pltpu.VMEM((MOE_TM * rc, LANES), F32),
                            pltpu.SemaphoreType.DMA, pltpu.SemaphoreType.DMA]),
        compiler_params=pltpu.CompilerParams(dimension_semantics=("arbitrary",),
                                             vmem_limit_bytes=VMEM_LIMIT_BYTES),
        name="dispatch",
    )(padtile, n_used, pos_start, h2_tiles)


def _moe_kernel(te_ref, nu_ref, xs_ref, wgu_ref, bgu_ref, wd_ref, bd_ref,
                ys_ref, wgu_bf, wd_bf, gu_ref):
    r = pl.program_id(0)
    busy = r < nu_ref[0]
    fresh = jnp.logical_or(r == 0, te_ref[r] != te_ref[jnp.maximum(r - 1, 0)])

    @pl.when(jnp.logical_and(busy, fresh))
    def _():
        wgu_bf[...] = wgu_ref[0].astype(BF16)
        wd_bf[...] = wd_ref[0].astype(BF16)

    @pl.when(jnp.logical_not(busy))
    def _():
        ys_ref[...] = jnp.zeros(ys_ref.shape, F32)

    @pl.when(busy)
    def _():
        x = _load_token_tiles(xs_ref).astype(BF16)
        fc = MOE_FF_CHUNK
        for c in range(2 * D_FF // fc):
            lo, hi = c * fc, (c + 1) * fc
            gu_ref[:, lo:hi] = (jnp.dot(x, wgu_bf[:, lo:hi], preferred_element_type=F32)
                                + bgu_ref[0][:, lo:hi])
        y = None
        for c in range(D_FF // fc):
            lo, hi = c * fc, (c + 1) * fc
            gate = jnp.minimum(gu_ref[:, lo:hi], SWIGLU_LIMIT)
            up = jnp.clip(gu_ref[:, D_FF + lo:D_FF + hi], -SWIGLU_LIMIT, SWIGLU_LIMIT)
            act = (up + 1.0) * gate * jax.nn.sigmoid(SWIGLU_ALPHA * gate)
            part = jnp.dot(act.astype(BF16), wd_bf[lo:hi, :], preferred_element_type=F32)
            y = part if y is None else y + part
        _store_token_tiles(ys_ref, y + bd_ref[0])


def _moe(xs_tiles, tile_expert, n_used, w_gate_up, b_gate_up, w_down, b_down):
    d = D_MODEL
    rc = ROW_CHUNKS
    tm = MOE_TM
    n_tiles = tile_expert.shape[0]
    e, _, f2 = w_gate_up.shape
    wsel = lambda r, te, nu: (te[r], 0, 0)
    rows = lambda r, te, nu: (jnp.minimum(r, nu[0] - 1), 0)
    return pl.pallas_call(
        _moe_kernel,
        out_shape=jax.ShapeDtypeStruct((n_tiles * tm * rc, LANES), F32),
        grid_spec=pltpu.PrefetchScalarGridSpec(
            num_scalar_prefetch=2,
            grid=(n_tiles,),
            in_specs=[pl.BlockSpec((tm * rc, LANES), rows),
                      pl.BlockSpec((1, d, f2), wsel),
                      pl.BlockSpec((1, 1, f2), wsel),
                      pl.BlockSpec((1, D_FF, d), wsel),
                      pl.BlockSpec((1, 1, d), wsel)],
            out_specs=pl.BlockSpec((tm * rc, LANES), lambda r, te, nu: (r, 0)),
            scratch_shapes=[pltpu.VMEM((d, f2), BF16), pltpu.VMEM((D_FF, d), BF16),
                            pltpu.VMEM((tm, f2), F32)]),
        compiler_params=pltpu.CompilerParams(dimension_semantics=("arbitrary",),
                                             vmem_limit_bytes=VMEM_LIMIT_BYTES),
        name="moe",
    )(tile_expert, n_used, xs_tiles, w_gate_up, b_gate_up, w_down, b_down)


def _combine_kernel(pos_ref, ys_ref, x1_ref, tw_ref, gt_ref, gf_ref, o_ref, gbuf, sem):
    tm = x1_ref.shape[0]
    rc = ROW_CHUNKS

    def issue(j, carry):
        for k in range(TOP_K):
            src = ys_ref.at[pl.ds(pl.multiple_of(pos_ref[j * TOP_K + k], rc), rc)]
            dst = gbuf.at[k, pl.ds(pl.multiple_of(j * rc, rc), rc)]
            pltpu.make_async_copy(src, dst, sem).start(priority=k % 2)
        return carry

    lax.fori_loop(0, tm, issue, 0, unroll=ROW_DMA_UNROLL)
    for k in range(TOP_K):
        pltpu.make_async_copy(ys_ref.at[pl.ds(0, tm * rc)], gbuf.at[k], sem).wait()

    tw = tw_ref[...]
    acc = tw[:, 0:1] * _load_token_tiles(gbuf.at[0])
    for k in range(1, TOP_K):
        acc = acc + tw[:, k:k + 1] * _load_token_tiles(gbuf.at[k])
    x2 = x1_ref[...] + gt_ref[0] * acc
    o_ref[...] = _rms(x2, gf_ref[...])


def _combine(ys, pos_flat, x1, tw, mod, g_final, seq):
    t, d = x1.shape
    tm = min(COMBINE_TM, t)
    return pl.pallas_call(
        _combine_kernel,
        out_shape=jax.ShapeDtypeStruct((t, d), F32),
        grid=(t // tm,),
        in_specs=[pl.BlockSpec((tm * TOP_K,), lambda s: (s,), memory_space=pltpu.SMEM),
                  pl.BlockSpec(memory_space=pl.ANY),
                  pl.BlockSpec((tm, d), lambda s: (s, 0)),
                  pl.BlockSpec((tm, LANES), lambda s: (s, 0)),
                  pl.BlockSpec((1, 1, d), lambda s: ((s * tm) // seq * 6 + 5, 0, 0)),
                  pl.BlockSpec((1, d), lambda s: (0, 0))],
        out_specs=pl.BlockSpec((tm, d), lambda s: (s, 0)),
        scratch_shapes=[pltpu.VMEM((TOP_K, tm * ROW_CHUNKS, LANES), F32),
                        pltpu.SemaphoreType.DMA],
        compiler_params=pltpu.CompilerParams(dimension_semantics=("arbitrary",),
                                             vmem_limit_bytes=VMEM_LIMIT_BYTES),
        name="combine",
    )(pos_flat, ys, x1, tw, mod, g_final)


def _routing_tables(top_idx, n_tiles):
    tm = MOE_TM
    e_flat = top_idx.reshape(-1)
    onehot = (e_flat[:, None] == jnp.arange(N_EXPERTS, dtype=I32)[None, :]).astype(I32)
    csum = jnp.cumsum(onehot, axis=0)
    counts = csum[-1]
    rank = jnp.sum(csum * onehot, axis=1) - 1
    tiles_per = (counts + tm - 1) // tm
    tile_end = jnp.cumsum(tiles_per)
    tile_start = tile_end - tiles_per
    pos = (jnp.sum(onehot * tile_start[None, :], axis=1) * tm + rank).astype(I32)
    n_used = tile_end[-1:].astype(I32)
    tile_ids = jnp.arange(n_tiles, dtype=I32)
    tile_expert = jnp.minimum(
        jnp.sum((tile_ids[:, None] >= tile_end[None, :]).astype(I32), axis=1),
        N_EXPERTS - 1).astype(I32)
    padtile = jnp.where(counts % tm != 0, tile_end - 1, -1).astype(I32)
    return pos, tile_expert, n_used, padtile


def kernel(x, c, rel_bias, w_ada, b_ada, g_mix, w_in, lambda_q1, lambda_k1, lambda_q2,
           lambda_k2, subln_g, w_br_a, w_br_b, w_out, g_ffn, w_router, b_router,
           w_gate_up, b_gate_up, w_down, b_down, g_final):
    b, s, d = x.shape
    assert d == D_MODEL and s % ATT_T == 0 and MOBA_BLOCK == ATT_T
    t_tok = b * s

    c8 = jnp.zeros((8, d), F32).at[:b].set(c)
    ada = _ada(c8, w_ada[0], b_ada[0].reshape(1, -1))
    mod = ada[:b].reshape(b * 6, 1, d)

    dtiles = _bias_tiles(rel_bias)
    tab = rel_bias.T.reshape(-1)

    proj = _proj(x, g_mix[0].reshape(1, d), mod, w_in[0].astype(BF16))

    lamv = jnp.concatenate([lambda_q1, lambda_k1, lambda_q2, lambda_k2], axis=0)
    y_a = _attention("diff", tab, proj, dtiles, (lamv, subln_g[0].reshape(-1, 1)), (0, 512, 1024))
    y_b = _attention("moba", tab, proj, dtiles, None, (1536, 2048, 2560))

    wr = jnp.zeros((d, LANES), BF16).at[:, :N_EXPERTS].set(w_router[0].astype(BF16))
    br = jnp.zeros((1, LANES), F32).at[0, :N_EXPERTS].set(b_router[0])
    x1, h2, ti, tw = _merge(x, proj, y_a, y_b, w_br_a[0].astype(BF16), w_br_b[0].astype(BF16),
                            w_out[0].astype(BF16), mod, g_ffn[0].reshape(1, d), wr, br)

    n_tiles = (t_tok * TOP_K) // MOE_TM + N_EXPERTS
    top_idx = ti.reshape(t_tok, LANES)[:, :TOP_K]
    pos, tile_expert, n_used, padtile = _routing_tables(top_idx, n_tiles)
    pos_start = pos * ROW_CHUNKS

    xs = _dispatch(h2.reshape(t_tok * ROW_CHUNKS, LANES), pos_start, padtile, n_used,
                   n_tiles * MOE_TM)
    ys = _moe(xs, tile_expert, n_used, w_gate_up[0], b_gate_up[0].reshape(N_EXPERTS, 1, -1),
              w_down[0], b_down[0].reshape(N_EXPERTS, 1, -1))
    out = _combine(ys, pos_start, x1.reshape(t_tok, d), tw.reshape(t_tok, LANES), mod,
                   g_final.reshape(1, d), s)
    return out.reshape(b, s, d)
```

```python
import functools
import math

import jax
import jax.numpy as jnp
import numpy as np
from jax import lax
from jax.experimental import pallas as pl
from jax.experimental.pallas import tpu as pltpu

F32 = jnp.float32
BF16 = jnp.bfloat16
I32 = jnp.int32

D_MODEL = 1024
N_DIFF_HEADS = 4
DIFF_HEAD_DIM = 64
N_MOBA_HEADS = 8
MOBA_HEAD_DIM = 64
MOBA_BLOCK = 256
MOBA_TOPK = 3
N_HEADS_TOTAL = N_DIFF_HEADS + N_MOBA_HEADS
N_BUCKETS = 32
MAX_DISTANCE = 128
N_EXPERTS = 32
TOP_K = 4
D_FF = D_MODEL
SWIGLU_LIMIT = 7.0
SWIGLU_ALPHA = 1.702
RMS_EPS = 1e-5
LAM_INIT = 0.8 - 0.6 * math.exp(-0.3 * 0)
IN_WIDTH = 5120

LANES = 128
VMEM_LIMIT_BYTES = 56 * 1024 * 1024

ATT_T = 256
N_PAIRS = 4
V_ROWS = LANES + 16
PROJ_TM = 512
MERGE_TM = 512
MERGE_SUB = 256
MOE_TM = 512
MOE_FF_CHUNK = 256
DISPATCH_TM = 512
ROW_DMA_UNROLL = 8
COMBINE_TM = 512
NEG = -1e30


def _t5_bucket_np(dist):
    n = np.maximum(dist, 0)
    max_exact = N_BUCKETS // 2
    nf = np.maximum(n, max_exact).astype(np.float32)
    large = max_exact + (np.log(nf / max_exact) / math.log(MAX_DISTANCE / max_exact)
                         * (N_BUCKETS - max_exact)).astype(np.int32)
    large = np.minimum(large, N_BUCKETS - 1)
    return np.where(n < max_exact, n, large).astype(np.int32)


def _rms(x, g):
    return x * lax.rsqrt(jnp.mean(x * x, axis=-1, keepdims=True) + RMS_EPS) * g


ROW_CHUNKS = D_MODEL // LANES


def _store_token_tiles(ref, x):
    n = x.shape[0]
    for c in range(ROW_CHUNKS):
        ref[pl.ds(c, n, stride=ROW_CHUNKS), :] = x[:, c * LANES:(c + 1) * LANES]


def _load_token_tiles(ref):
    n = ref.shape[0] // ROW_CHUNKS
    return jnp.concatenate([ref[pl.ds(c, n, stride=ROW_CHUNKS), :] for c in range(ROW_CHUNKS)],
                           axis=1)


def _ada_kernel(c_ref, w_ref, b_ref, o_ref):
    o_ref[...] = jnp.dot(c_ref[...].astype(BF16), w_ref[...].astype(BF16),
                         preferred_element_type=F32) + b_ref[...]


def _ada(c8, w, b):
    d, n = w.shape
    tn = 1536
    return pl.pallas_call(
        _ada_kernel,
        out_shape=jax.ShapeDtypeStruct((8, n), F32),
        grid=(n // tn,),
        in_specs=[pl.BlockSpec((8, d), lambda j: (0, 0)),
                  pl.BlockSpec((d, tn), lambda j: (0, j)),
                  pl.BlockSpec((1, tn), lambda j: (0, j))],
        out_specs=pl.BlockSpec((8, tn), lambda j: (0, j)),
        compiler_params=pltpu.CompilerParams(dimension_semantics=("arbitrary",),
                                             vmem_limit_bytes=VMEM_LIMIT_BYTES),
        name="ada",
    )(c8, w, b)


def _bias_kernel(tab_ref, bd_ref, bp_ref, o_ref):
    g = pl.program_id(0)
    p = pl.program_id(1)
    t = pl.program_id(2)
    head = jnp.where(g == 0, p, N_DIFF_HEADS + 2 * p + t)
    bd = bd_ref[...]
    bp = bp_ref[...]
    d0 = jnp.where(bd < 0, NEG, 0.0).astype(F32)
    d1 = jnp.zeros(bp.shape, F32)
    for b in range(N_BUCKETS):
        val = tab_ref[head * N_BUCKETS + b]
        d0 = jnp.where(bd == b, val, d0)
        d1 = jnp.where(bp == b, val, d1)
    o_ref[0, 0, 0] = d0
    o_ref[0, 0, 1] = d1


def _bias_tiles(rel_bias):
    t = ATT_T
    jj = np.arange(t)[:, None]
    ii = np.arange(t)[None, :]
    bd = np.where(ii >= jj, _t5_bucket_np(ii - jj), -1).astype(np.int32)
    bp = _t5_bucket_np(t + ii - jj)
    tab = rel_bias.T.reshape(-1)
    return pl.pallas_call(
        _bias_kernel,
        out_shape=jax.ShapeDtypeStruct((2, N_PAIRS, 2, t, 2 * t), F32),
        grid=(2, N_PAIRS, 2),
        in_specs=[pl.BlockSpec(memory_space=pltpu.SMEM),
                  pl.BlockSpec((t, t), lambda g, p, h: (0, 0)),
                  pl.BlockSpec((t, t), lambda g, p, h: (0, 0))],
        out_specs=pl.BlockSpec((1, 1, 2, t, t), lambda g, p, h: (g, p, 0, 0, h)),
        compiler_params=pltpu.CompilerParams(
            dimension_semantics=("arbitrary", "arbitrary", "arbitrary")),
        name="bias_tiles",
    )(tab, jnp.asarray(bd), jnp.asarray(bp))


def _proj_kernel(x_ref, g_ref, sh_ref, sc_ref, w_ref, o_ref):
    x = x_ref[0]
    h = _rms(x, g_ref[...]) * (1.0 + sc_ref[0]) + sh_ref[0]
    hb = h.astype(BF16)
    n_total = w_ref.shape[1]
    for n0 in range(0, n_total, 512):
        o_ref[0, :, n0:n0 + 512] = jnp.dot(
            hb, w_ref[:, n0:n0 + 512], preferred_element_type=F32).astype(BF16)


def _proj(x, g_mix, mod, w_in_bf):
    b, s, d = x.shape
    n = w_in_bf.shape[1]
    tm = min(PROJ_TM, s)
    return pl.pallas_call(
        _proj_kernel,
        out_shape=jax.ShapeDtypeStruct((b, s, n), BF16),
        grid=(b, s // tm),
        in_specs=[pl.BlockSpec((1, tm, d), lambda bi, i: (bi, i, 0)),
                  pl.BlockSpec((1, d), lambda bi, i: (0, 0)),
                  pl.BlockSpec((1, 1, d), lambda bi, i: (bi * 6 + 0, 0, 0)),
                  pl.BlockSpec((1, 1, d), lambda bi, i: (bi * 6 + 1, 0, 0)),
                  pl.BlockSpec((d, n), lambda bi, i: (0, 0))],
        out_specs=pl.BlockSpec((1, tm, n), lambda bi, i: (bi, i, 0)),
        compiler_params=pltpu.CompilerParams(dimension_semantics=("arbitrary", "arbitrary"),
                                             vmem_limit_bytes=VMEM_LIMIT_BYTES),
        name="proj",
    )(x, g_mix, mod, mod, w_in_bf)


def _stack_q(q_ref, q2_ref, scale):
    t = ATT_T
    row = lax.broadcasted_iota(I32, (LANES, t), 0)
    for p in range(N_PAIRS):
        q = (q_ref[0, :, p * LANES:(p + 1) * LANES].astype(F32) * scale).T
        q2_ref[p, :, 0:t] = jnp.where(row < 64, q, 0.0).astype(BF16)
        q2_ref[p, :, t:2 * t] = jnp.where(row >= 64, q, 0.0).astype(BF16)


def _transpose_values(v_ref, vt_ref):
    t = ATT_T
    nq = vt_ref.shape[0]

    def body(n, carry):
        for p in range(N_PAIRS):
            blk = v_ref[0, pl.ds(pl.multiple_of(n * t, t), t), p * LANES:(p + 1) * LANES]
            vt_ref[n, p * V_ROWS:p * V_ROWS + LANES, :] = blk.astype(F32).T.astype(BF16)
            vt_ref[n, p * V_ROWS + LANES:(p + 1) * V_ROWS, :] = jnp.ones((V_ROWS - LANES, t), BF16)
        return carry

    lax.fori_loop(0, nq, body, 0)


def _flash_init(m_ref, acc_ref):
    m_ref[...] = jnp.full(m_ref.shape, NEG, F32)
    acc_ref[...] = jnp.zeros(acc_ref.shape, F32)


def _score_slot(p, h, bank):
    return bank * (2 * N_PAIRS) + 2 * p + h


def _flash_scores(p, h, n, bank, q2_ref, k_ref, s_ref):
    t = ATT_T
    kt = k_ref[0, pl.ds(pl.multiple_of(n * t, t), t), p * LANES:(p + 1) * LANES]
    s_ref[_score_slot(p, h, bank)] = jnp.dot(kt, q2_ref[p, :, h * t:(h + 1) * t],
                                             preferred_element_type=F32)


def _flash_absorb(p, h, n, bank, tile_bias, row_bias, live, v_ref, s_ref, m_ref, acc_ref):
    t = ATT_T
    cs = slice(h * t, (h + 1) * t)
    vt = v_ref[n, p * V_ROWS:(p + 1) * V_ROWS, :]
    s = s_ref[_score_slot(p, h, bank)]
    if tile_bias is not None:
        s = s + tile_bias
    mx = jnp.max(s, axis=0, keepdims=True)
    if row_bias is not None:
        mx = mx + row_bias
    if live is not None:
        mx = jnp.where(live, mx, NEG)
    m_prev = m_ref[p, :, cs]
    m_new = jnp.maximum(m_prev, mx)
    alpha = jnp.exp(m_prev - m_new)
    shift = m_new if row_bias is None else m_new - row_bias
    if live is not None:
        shift = jnp.where(live, shift, -NEG)
    pt = jnp.exp(s - shift)
    acc_ref[p, :, cs] = alpha * acc_ref[p, :, cs] + jnp.dot(vt, pt.astype(BF16),
                                                            preferred_element_type=F32)
    m_ref[p, :, cs] = m_new


def _far_bias(tab_ref, head):
    return tab_ref[head * N_BUCKETS + (N_BUCKETS - 1)]


def _flash_sweep(i, far_args, prev_args, own_args, refs):
    q2_ref, k_ref, v_ref, s_ref, m_ref, acc_ref = refs
    chains = [(p, h) for p in range(N_PAIRS) for h in range(2)]

    def scores_tile(n, bank):
        for p, h in chains:
            _flash_scores(p, h, n, bank, q2_ref, k_ref, s_ref)

    def step(n, bank, args, with_next):
        for p, h in chains:
            if with_next:
                _flash_scores(p, h, n + 1, 1 - bank, q2_ref, k_ref, s_ref)
            _flash_absorb(p, h, n, bank, *args(p, h), v_ref, s_ref, m_ref, acc_ref)

    def far_step(n, bank):
        step(n, bank, lambda p, h: far_args(p, h, n), True)

    n_far = jnp.maximum(i - 1, 0)
    lead = jnp.bitwise_and(n_far, 1)

    @pl.when(jnp.bitwise_and(i, 1) == 0)
    def _():
        scores_tile(0, 0)

    @pl.when(jnp.bitwise_and(i, 1) == 1)
    def _():
        scores_tile(0, 1)

    @pl.when(lead == 1)
    def _():
        far_step(0, 0)

    def pair_body(j, carry):
        n = lead + 2 * j
        far_step(n, 1)
        far_step(n + 1, 0)
        return carry

    lax.fori_loop(0, jnp.right_shift(n_far, 1), pair_body, 0)

    @pl.when(i >= 1)
    def _():
        step(i - 1, 1, prev_args, True)

    step(i, 0, own_args, False)


def _diff_kernel(tab_ref, q_ref, k_ref, v_ref, d_ref, lamv_ref, g_ref, o_ref,
                 q2_ref, s_ref, m_ref, acc_ref, vt_ref):
    t = ATT_T
    i = pl.program_id(1)

    @pl.when(i == 0)
    def _():
        _transpose_values(v_ref, vt_ref)

    _stack_q(q_ref, q2_ref, DIFF_HEAD_DIM ** -0.5)
    _flash_init(m_ref, acc_ref)
    refs = (q2_ref, k_ref, vt_ref, s_ref, m_ref, acc_ref)
    _flash_sweep(i,
                 lambda p, h, n: (None, _far_bias(tab_ref, p), None),
                 lambda p, h: (d_ref[0, p, 1, :, h * t:(h + 1) * t], None, None),
                 lambda p, h: (d_ref[0, p, 0, :, h * t:(h + 1) * t], None, None),
                 refs)

    lv = lamv_ref[...]
    lam = (jnp.exp(jnp.sum(lv[0:1] * lv[1:2], axis=1, keepdims=True))
           - jnp.exp(jnp.sum(lv[2:3] * lv[3:4], axis=1, keepdims=True)) + LAM_INIT)
    for p in range(N_PAIRS):
        acc = acc_ref[p, 0:LANES, :]
        l = acc_ref[p, LANES:LANES + 1, :]
        o = acc[:, 0:t] / l[:, 0:t] - lam * (acc[:, t:2 * t] / l[:, t:2 * t])
        y = o * lax.rsqrt(jnp.mean(o * o, axis=0, keepdims=True) + RMS_EPS) * g_ref[...]
        o_ref[0, :, p * LANES:(p + 1) * LANES] = (y * (1.0 - LAM_INIT)).T.astype(BF16)


def _moba_kernel(tab_ref, q_ref, k_ref, v_ref, d_ref, o_ref,
                 q2_ref, s_ref, m_ref, acc_ref, vt_ref, kmean_ref, sel_ref):
    t = ATT_T
    i = pl.program_id(1)
    nb = k_ref.shape[1] // t
    nbp = kmean_ref.shape[1]

    @pl.when(i == 0)
    def _():
        _transpose_values(v_ref, vt_ref)
        kmean_ref[...] = jnp.zeros(kmean_ref.shape, F32)
        for p in range(N_PAIRS):
            for n in range(nb):
                blk = k_ref[0, n * t:(n + 1) * t, p * LANES:(p + 1) * LANES].astype(F32)
                kmean_ref[p, n:n + 1, :] = jnp.sum(blk, axis=0, keepdims=True) * (1.0 / t)

    _stack_q(q_ref, q2_ref, MOBA_HEAD_DIM ** -0.5)
    _flash_init(m_ref, acc_ref)

    blk_id = lax.broadcasted_iota(I32, (nbp, 2 * t), 0)
    blk_f = blk_id.astype(F32)
    for p in range(N_PAIRS):
        gs = jnp.dot(kmean_ref[p].astype(BF16), q2_ref[p], preferred_element_type=F32)
        g = jnp.where(blk_id < i, gs, -jnp.inf)
        sel = jnp.full((nbp, 2 * t), NEG, F32)
        for _ in range(MOBA_TOPK):
            mx = jnp.max(g, axis=0, keepdims=True)
            first = jnp.min(jnp.where(g == mx, blk_f, float(nbp)), axis=0, keepdims=True)
            hit = blk_f == first
            sel = jnp.where(hit, 0.0, sel)
            g = jnp.where(hit, -jnp.inf, g)
        sel_ref[p] = sel

    def live(p, h, n):
        return sel_ref[p, pl.ds(n, 1), h * t:(h + 1) * t] > -1.0

    refs = (q2_ref, k_ref, vt_ref, s_ref, m_ref, acc_ref)
    _flash_sweep(i,
                 lambda p, h, n: (None, _far_bias(tab_ref, N_DIFF_HEADS + 2 * p + h),
                                  live(p, h, n)),
                 lambda p, h: (d_ref[0, p, 1, :, h * t:(h + 1) * t], None, live(p, h, i - 1)),
                 lambda p, h: (d_ref[0, p, 0, :, h * t:(h + 1) * t], None, None),
                 refs)

    row = lax.broadcasted_iota(I32, (LANES, t), 0)
    for p in range(N_PAIRS):
        acc = acc_ref[p, 0:LANES, :]
        l = acc_ref[p, LANES:LANES + 1, :]
        o = jnp.where(row < 64, acc[:, 0:t] / l[:, 0:t], acc[:, t:2 * t] / l[:, t:2 * t])
        o_ref[0, :, p * LANES:(p + 1) * LANES] = o.T.astype(BF16)


def _attention(kind, tab, proj, dtiles, extra, cols):
    b, s, _ = proj.shape
    t = ATT_T
    nq = s // t
    w = N_PAIRS * LANES
    qc, kc, vc = cols
    gidx = 0 if kind == "diff" else 1
    in_specs = [pl.BlockSpec(memory_space=pltpu.SMEM),
                pl.BlockSpec((1, t, w), lambda bi, i: (bi, i, qc // w)),
                pl.BlockSpec((1, s, w), lambda bi, i: (bi, 0, kc // w)),
                pl.BlockSpec((1, s, w), lambda bi, i: (bi, 0, vc // w)),
                pl.BlockSpec((1, N_PAIRS, 2, t, 2 * t), lambda bi, i: (gidx, 0, 0, 0, 0))]
    scratch = [pltpu.VMEM((N_PAIRS, LANES, 2 * t), BF16),
               pltpu.VMEM((2 * 2 * N_PAIRS, t, t), F32),
               pltpu.VMEM((N_PAIRS, 1, 2 * t), F32),
               pltpu.VMEM((N_PAIRS, V_ROWS, 2 * t), F32),
               pltpu.VMEM((nq, N_PAIRS * V_ROWS, t), BF16)]
    if kind == "diff":
        lamv, subln_g = extra
        in_specs += [pl.BlockSpec(lamv.shape, lambda bi, i: (0, 0)),
                     pl.BlockSpec(subln_g.shape, lambda bi, i: (0, 0))]
        args = (tab, proj, proj, proj, dtiles, lamv, subln_g)
        body = _diff_kernel
    else:
        nbp = max(32, nq)
        scratch += [pltpu.VMEM((N_PAIRS, nbp, LANES), F32), pltpu.VMEM((N_PAIRS, nbp, 2 * t), F32)]
        args = (tab, proj, proj, proj, dtiles)
        body = _moba_kernel
    return pl.pallas_call(
        body,
        out_shape=jax.ShapeDtypeStruct((b, s, w), BF16),
        grid=(b, nq),
        in_specs=in_specs,
        out_specs=pl.BlockSpec((1, t, w), lambda bi, i: (bi, i, 0)),
        scratch_shapes=scratch,
        compiler_params=pltpu.CompilerParams(
            dimension_semantics=("arbitrary", "arbitrary"),
            vmem_limit_bytes=VMEM_LIMIT_BYTES),
        name=kind + "_attn",
    )(*args)


def _merge_kernel(x_ref, ya_ref, yb_ref, gla_ref, glb_ref, wa_ref, wb_ref, wo_ref,
                  gt_ref, shf_ref, scf_ref, gf_ref, wr_ref, br_ref,
                  x1_ref, h2_ref, ti_ref, tw_ref):
    ts = MERGE_SUB
    subs = [slice(j * ts, (j + 1) * ts) for j in range(x_ref.shape[1] // ts)]
    branch = [(jnp.dot(ya_ref[0, sl], wa_ref[...], preferred_element_type=F32),
               jnp.dot(yb_ref[0, sl], wb_ref[...], preferred_element_type=F32)) for sl in subs]
    mixed = []
    for sl, (a, b) in zip(subs, branch):
        merged = (jax.nn.sigmoid(gla_ref[0, sl].astype(F32)) * a
                  + jax.nn.sigmoid(glb_ref[0, sl].astype(F32)) * b)
        mixed.append(jnp.dot(merged.astype(BF16), wo_ref[...], preferred_element_type=F32))
    lane = lax.broadcasted_iota(I32, (ts, LANES), 1)
    lane_f = lane.astype(F32)
    for j, (sl, mo) in enumerate(zip(subs, mixed)):
        x1 = x_ref[0, sl] + gt_ref[0] * mo
        x1_ref[0, sl] = x1
        h = _rms(x1, gf_ref[...]) * (1.0 + scf_ref[0]) + shf_ref[0]
        _store_token_tiles(h2_ref.at[0, j * ts * ROW_CHUNKS:(j + 1) * ts * ROW_CHUNKS], h)
        logits = jnp.dot(h.astype(BF16), wr_ref[...], preferred_element_type=F32) + br_ref[...]
        g = jnp.where(lane < N_EXPERTS, logits, -jnp.inf)
        vals, idxs = [], []
        for _ in range(TOP_K):
            mx = jnp.max(g, axis=1, keepdims=True)
            first = jnp.min(jnp.where(g == mx, lane_f, float(LANES)), axis=1, keepdims=True)
            vals.append(mx)
            idxs.append(first)
            g = jnp.where(lane_f == first, -jnp.inf, g)
        es = [jnp.exp(v - vals[0]) for v in vals]
        den = es[0] + es[1] + es[2] + es[3]
        ti = jnp.zeros((ts, LANES), F32)
        tw = jnp.zeros((ts, LANES), F32)
        for k in range(TOP_K):
            ti = jnp.where(lane == k, idxs[k], ti)
            tw = jnp.where(lane == k, es[k] / den, tw)
        ti_ref[0, sl] = ti.astype(I32)
        tw_ref[0, sl] = tw


def _merge(x, proj, y_a, y_b, w_br_a, w_br_b, w_out, mod, g_ffn, w_router, b_router):
    b, s, d = x.shape
    tm = min(MERGE_TM, s)
    full = lambda shape: pl.BlockSpec(shape, lambda bi, i: (0,) * len(shape))
    tile = lambda w, c=0: pl.BlockSpec((1, tm, w), lambda bi, i: (bi, i, c))
    modspec = lambda j: pl.BlockSpec((1, 1, d), lambda bi, i: (bi * 6 + j, 0, 0))
    return pl.pallas_call(
        _merge_kernel,
        out_shape=(jax.ShapeDtypeStruct((b, s, d), F32),
                   jax.ShapeDtypeStruct((b, s * ROW_CHUNKS, LANES), F32),
                   jax.ShapeDtypeStruct((b, s, LANES), I32),
                   jax.ShapeDtypeStruct((b, s, LANES), F32)),
        grid=(b, s // tm),
        in_specs=[tile(d), tile(512), tile(512), tile(d, 3), tile(d, 4),
                  full(w_br_a.shape), full(w_br_b.shape), full(w_out.shape),
                  modspec(2), modspec(3), modspec(4), full(g_ffn.shape),
                  full(w_router.shape), full(b_router.shape)],
        out_specs=(tile(d), pl.BlockSpec((1, tm * ROW_CHUNKS, LANES), lambda bi, i: (bi, i, 0)),
                   tile(LANES), tile(LANES)),
        compiler_params=pltpu.CompilerParams(dimension_semantics=("arbitrary", "arbitrary"),
                                             vmem_limit_bytes=VMEM_LIMIT_BYTES),
        name="merge",
    )(x, y_a, y_b, proj, proj, w_br_a, w_br_b, w_out, mod, mod, mod, g_ffn,
      w_router, b_router)


def _dispatch_kernel(padtile_ref, nu_ref, pos_ref, h2_ref, xs_ref, zeros_ref, zsem, sem):
    step = pl.program_id(0)
    rc = ROW_CHUNKS
    n_tok = h2_ref.shape[0] // rc
    pad_rows = zeros_ref.shape[0]
    n_tiles = xs_ref.shape[0] // pad_rows

    def zero_tile(tile):
        start = pl.multiple_of(tile * pad_rows, pad_rows)
        return pltpu.make_async_copy(zeros_ref, xs_ref.at[pl.ds(start, pad_rows)], zsem)

    @pl.when(step == 0)
    def _():
        zeros_ref[...] = jnp.zeros(zeros_ref.shape, F32)
        for e in range(N_EXPERTS):
            @pl.when(padtile_ref[e] >= 0)
            def _():
                zero_tile(padtile_ref[e]).start()
        for e in range(N_EXPERTS):
            @pl.when(padtile_ref[e] >= 0)
            def _():
                zero_tile(padtile_ref[e]).wait()

        def start_tail(tile, carry):
            zero_tile(tile).start()
            return carry

        def wait_tail(tile, carry):
            zero_tile(tile).wait()
            return carry

        lax.fori_loop(nu_ref[0], n_tiles, start_tail, 0)
        lax.fori_loop(nu_ref[0], n_tiles, wait_tail, 0)

    def issue(j, carry):
        src = h2_ref.at[pl.ds(pl.multiple_of(j * rc, rc), rc)]
        for k in range(TOP_K):
            dst = xs_ref.at[pl.ds(pl.multiple_of(pos_ref[j * TOP_K + k], rc), rc)]
            pltpu.make_async_copy(src, dst, sem).start(priority=k % 2)
        return carry

    lax.fori_loop(0, n_tok, issue, 0, unroll=ROW_DMA_UNROLL)
    for k in range(TOP_K):
        pltpu.make_async_copy(h2_ref, xs_ref.at[pl.ds(0, n_tok * rc)], sem).wait()


def _dispatch(h2_tiles, pos_start, padtile, n_used, n_rows):
    rc = ROW_CHUNKS
    t_tok = h2_tiles.shape[0] // rc
    tm = min(DISPATCH_TM, t_tok)
    return pl.pallas_call(
        _dispatch_kernel,
        out_shape=jax.ShapeDtypeStruct((n_rows * rc, LANES), F32),
        grid_spec=pltpu.PrefetchScalarGridSpec(
            num_scalar_prefetch=2,
            grid=(t_tok // tm,),
            in_specs=[pl.BlockSpec((tm * TOP_K,), lambda s, pt, nu: (s,),
                                   memory_space=pltpu.SMEM),
                      pl.BlockSpec((tm * rc, LANES), lambda s, pt, nu: (s, 0))],
            out_specs=pl.BlockSpec(memory_space=pl.ANY),
            scratch_shapes=[pltpu.VMEM((MOE_TM * rc, LANES), F32),
                            pltpu.SemaphoreType.DMA, pltpu.SemaphoreType.DMA]),
        compiler_params=pltpu.CompilerParams(dimension_semantics=("arbitrary",),
                                             vmem_limit_bytes=VMEM_LIMIT_BYTES),
        name="dispatch",
    )(padtile, n_used, pos_start, h2_tiles)


def _moe_kernel(te_ref, nu_ref, xs_ref, wgu_ref, bgu_ref, wd_ref, bd_ref,
                ys_ref, wgu_bf, wd_bf, gu_ref):
    r = pl.program_id(0)
    busy = r < nu_ref[0]
    fresh = jnp.logical_or(r == 0, te_ref[r] != te_ref[jnp.maximum(r - 1, 0)])

    @pl.when(jnp.logical_and(busy, fresh))
    def _():
        wgu_bf[...] = wgu_ref[0].astype(BF16)
        wd_bf[...] = wd_ref[0].astype(BF16)

    @pl.when(jnp.logical_not(busy))
    def _():
        ys_ref[...] = jnp.zeros(ys_ref.shape, F32)

    @pl.when(busy)
    def _():
        x = _load_token_tiles(xs_ref).astype(BF16)
        fc = MOE_FF_CHUNK
        for c in range(2 * D_FF // fc):
            lo, hi = c * fc, (c + 1) * fc
            gu_ref[:, lo:hi] = (jnp.dot(x, wgu_bf[:, lo:hi], preferred_element_type=F32)
                                + bgu_ref[0][:, lo:hi])
        y = None
        for c in range(D_FF // fc):
            lo, hi = c * fc, (c + 1) * fc
            gate = jnp.minimum(gu_ref[:, lo:hi], SWIGLU_LIMIT)
            up = jnp.clip(gu_ref[:, D_FF + lo:D_FF + hi], -SWIGLU_LIMIT, SWIGLU_LIMIT)
            act = (up + 1.0) * gate * jax.nn.sigmoid(SWIGLU_ALPHA * gate)
            part = jnp.dot(act.astype(BF16), wd_bf[lo:hi, :], preferred_element_type=F32)
            y = part if y is None else y + part
        _store_token_tiles(ys_ref, y + bd_ref[0])


def _moe(xs_tiles, tile_expert, n_used, w_gate_up, b_gate_up, w_down, b_down):
    d = D_MODEL
    rc = ROW_CHUNKS
    tm = MOE_TM
    n_tiles = tile_expert.shape[0]
    e, _, f2 = w_gate_up.shape
    wsel = lambda r, te, nu: (te[r], 0, 0)
    rows = lambda r, te, nu: (jnp.minimum(r, nu[0] - 1), 0)
    return pl.pallas_call(
        _moe_kernel,
        out_shape=jax.ShapeDtypeStruct((n_tiles * tm * rc, LANES), F32),
        grid_spec=pltpu.PrefetchScalarGridSpec(
            num_scalar_prefetch=2,
            grid=(n_tiles,),
            in_specs=[pl.BlockSpec((tm * rc, LANES), rows),
                      pl.BlockSpec((1, d, f2), wsel),
                      pl.BlockSpec((1, 1, f2), wsel),
                      pl.BlockSpec((1, D_FF, d), wsel),
                      pl.BlockSpec((1, 1, d), wsel)],
            out_specs=pl.BlockSpec((tm * rc, LANES), lambda r, te, nu: (r, 0)),
            scratch_shapes=[pltpu.VMEM((d, f2), BF16), pltpu.VMEM((D_FF, d), BF16),
                            pltpu.VMEM((tm, f2), F32)]),
        compiler_params=pltpu.CompilerParams(dimension_semantics=("arbitrary",),
                                             vmem_limit_bytes=VMEM_LIMIT_BYTES),
        name="moe",
    )(tile_expert, n_used, xs_tiles, w_gate_up, b_gate_up, w_down, b_down)


def _combine_kernel(pos_ref, ys_ref, x1_ref, tw_ref, gt_ref, gf_ref, o_ref, gbuf, sem):
    tm = x1_ref.shape[0]
    rc = ROW_CHUNKS

    def issue(j, carry):
        for k in range(TOP_K):
            src = ys_ref.at[pl.ds(pl.multiple_of(pos_ref[j * TOP_K + k], rc), rc)]
            dst = gbuf.at[k, pl.ds(pl.multiple_of(j * rc, rc), rc)]
            pltpu.make_async_copy(src, dst, sem).start(priority=k % 2)
        return carry

    lax.fori_loop(0, tm, issue, 0, unroll=ROW_DMA_UNROLL)
    for k in range(TOP_K):
        pltpu.make_async_copy(ys_ref.at[pl.ds(0, tm * rc)], gbuf.at[k], sem).wait()

    tw = tw_ref[...]
    acc = tw[:, 0:1] * _load_token_tiles(gbuf.at[0])
    for k in range(1, TOP_K):
        acc = acc + tw[:, k:k + 1] * _load_token_tiles(gbuf.at[k])
    x2 = x1_ref[...] + gt_ref[0] * acc
    o_ref[...] = _rms(x2, gf_ref[...])


def _combine(ys, pos_flat, x1, tw, mod, g_final, seq):
    t, d = x1.shape
    tm = min(COMBINE_TM, t)
    return pl.pallas_call(
        _combine_kernel,
        out_shape=jax.ShapeDtypeStruct((t, d), F32),
        grid=(t // tm,),
        in_specs=[pl.BlockSpec((tm * TOP_K,), lambda s: (s,), memory_space=pltpu.SMEM),
                  pl.BlockSpec(memory_space=pl.ANY),
                  pl.BlockSpec((tm, d), lambda s: (s, 0)),
                  pl.BlockSpec((tm, LANES), lambda s: (s, 0)),
                  pl.BlockSpec((1, 1, d), lambda s: ((s * tm) // seq * 6 + 5, 0, 0)),
                  pl.BlockSpec((1, d), lambda s: (0, 0))],
        out_specs=pl.BlockSpec((tm, d), lambda s: (s, 0)),
        scratch_shapes=[pltpu.VMEM((TOP_K, tm * ROW_CHUNKS, LANES), F32),
                        pltpu.SemaphoreType.DMA],
        compiler_params=pltpu.CompilerParams(dimension_semantics=("arbitrary",),
                                             vmem_limit_bytes=VMEM_LIMIT_BYTES),
        name="combine",
    )(pos_flat, ys, x1, tw, mod, g_final)


def _routing_tables(top_idx, n_tiles):
    tm = MOE_TM
    e_flat = top_idx.reshape(-1)
    onehot = (e_flat[:, None] == jnp.arange(N_EXPERTS, dtype=I32)[None, :]).astype(I32)
    csum = jnp.cumsum(onehot, axis=0)
    counts = csum[-1]
    rank = jnp.sum(csum * onehot, axis=1) - 1
    tiles_per = (counts + tm - 1) // tm
    tile_end = jnp.cumsum(tiles_per)
    tile_start = tile_end - tiles_per
    pos = (jnp.sum(onehot * tile_start[None, :], axis=1) * tm + rank).astype(I32)
    n_used = tile_end[-1:].astype(I32)
    tile_ids = jnp.arange(n_tiles, dtype=I32)
    tile_expert = jnp.minimum(
        jnp.sum((tile_ids[:, None] >= tile_end[None, :]).astype(I32), axis=1),
        N_EXPERTS - 1).astype(I32)
    padtile = jnp.where(counts % tm != 0, tile_end - 1, -1).astype(I32)
    return pos, tile_expert, n_used, padtile


def kernel(x, c, rel_bias, w_ada, b_ada, g_mix, w_in, lambda_q1, lambda_k1, lambda_q2,
           lambda_k2, subln_g, w_br_a, w_br_b, w_out, g_ffn, w_router, b_router,
           w_gate_up, b_gate_up, w_down, b_down, g_final):
    b, s, d = x.shape
    assert d == D_MODEL and s % ATT_T == 0 and MOBA_BLOCK == ATT_T
    t_tok = b * s

    c8 = jnp.zeros((8, d), F32).at[:b].set(c)
    ada = _ada(c8, w_ada[0], b_ada[0].reshape(1, -1))
    mod = ada[:b].reshape(b * 6, 1, d)

    dtiles = _bias_tiles(rel_bias)
    tab = rel_bias.T.reshape(-1)

    proj = _proj(x, g_mix[0].reshape(1, d), mod, w_in[0].astype(BF16))

    lamv = jnp.concatenate([lambda_q1, lambda_k1, lambda_q2, lambda_k2], axis=0)
    y_a = _attention("diff", tab, proj, dtiles, (lamv, subln_g[0].reshape(-1, 1)), (0, 512, 1024))
    y_b = _attention("moba", tab, proj, dtiles, None, (1536, 2048, 2560))

    wr = jnp.zeros((d, LANES), BF16).at[:, :N_EXPERTS].set(w_router[0].astype(BF16))
    br = jnp.zeros((1, LANES), F32).at[0, :N_EXPERTS].set(b_router[0])
    x1, h2, ti, tw = _merge(x, proj, y_a, y_b, w_br_a[0].astype(BF16), w_br_b[0].astype(BF16),
                            w_out[0].astype(BF16), mod, g_ffn[0].reshape(1, d), wr, br)

    n_tiles = (t_tok * TOP_K) // MOE_TM + N_EXPERTS
    top_idx = ti.reshape(t_tok, LANES)[:, :TOP_K]
    pos, tile_expert, n_used, padtile = _routing_tables(top_idx, n_tiles)
    pos_start = pos * ROW_CHUNKS

    xs = _dispatch(h2.reshape(t_tok * ROW_CHUNKS, LANES), pos_start, padtile, n_used,
                   n_tiles * MOE_TM)
    ys = _moe(xs, tile_expert, n_used, w_gate_up[0], b_gate_up[0].reshape(N_EXPERTS, 1, -1),
              w_down[0], b_down[0].reshape(N_EXPERTS, 1, -1))
    out = _combine(ys, pos_start, x1.reshape(t_tok, d), tw.reshape(t_tok, LANES), mod,
                   g_final.reshape(1, d), s)
    return out.reshape(b, s, d)
```

```python
import functools
import math

import jax
import jax.numpy as jnp
import numpy as np
from jax import lax
from jax.experimental import pallas as pl
from jax.experimental.pallas import tpu as pltpu

F32 = jnp.float32
BF16 = jnp.bfloat16
I32 = jnp.int32

D_MODEL = 1024
N_DIFF_HEADS = 4
DIFF_HEAD_DIM = 64
N_MOBA_HEADS = 8
MOBA_HEAD_DIM = 64
MOBA_BLOCK = 256
MOBA_TOPK = 3
N_HEADS_TOTAL = N_DIFF_HEADS + N_MOBA_HEADS
N_BUCKETS = 32
MAX_DISTANCE = 128
N_EXPERTS = 32
TOP_K = 4
D_FF = D_MODEL
SWIGLU_LIMIT = 7.0
SWIGLU_ALPHA = 1.702
RMS_EPS = 1e-5
LAM_INIT = 0.8 - 0.6 * math.exp(-0.3 * 0)
IN_WIDTH = 5120

LANES = 128
VMEM_LIMIT_BYTES = 56 * 1024 * 1024

ATT_T = 256
N_PAIRS = 4
V_ROWS = LANES + 16
PROJ_TM = 512
MERGE_TM = 512
MERGE_SUB = 256
MOE_TM = 512
MOE_FF_CHUNK = 256
DISPATCH_TM = 1024
ROW_DMA_UNROLL = 8
COMBINE_TM = 1024
NEG = -1e30


def _t5_bucket_np(dist):
    n = np.maximum(dist, 0)
    max_exact = N_BUCKETS // 2
    nf = np.maximum(n, max_exact).astype(np.float32)
    large = max_exact + (np.log(nf / max_exact) / math.log(MAX_DISTANCE / max_exact)
                         * (N_BUCKETS - max_exact)).astype(np.int32)
    large = np.minimum(large, N_BUCKETS - 1)
    return np.where(n < max_exact, n, large).astype(np.int32)


def _rms(x, g):
    return x * lax.rsqrt(jnp.mean(x * x, axis=-1, keepdims=True) + RMS_EPS) * g


ROW_CHUNKS = D_MODEL // LANES


def _store_token_tiles(ref, x):
    n = x.shape[0]
    for c in range(ROW_CHUNKS):
        ref[pl.ds(c, n, stride=ROW_CHUNKS), :] = x[:, c * LANES:(c + 1) * LANES]


def _load_token_tiles(ref):
    n = ref.shape[0] // ROW_CHUNKS
    return jnp.concatenate([ref[pl.ds(c, n, stride=ROW_CHUNKS), :] for c in range(ROW_CHUNKS)],
                           axis=1)


def _ada_kernel(c_ref, w_ref, b_ref, o_ref):
    o_ref[...] = jnp.dot(c_ref[...].astype(BF16), w_ref[...].astype(BF16),
                         preferred_element_type=F32) + b_ref[...]


def _ada(c8, w, b):
    d, n = w.shape
    tn = 1536
    return pl.pallas_call(
        _ada_kernel,
        out_shape=jax.ShapeDtypeStruct((8, n), F32),
        grid=(n // tn,),
        in_specs=[pl.BlockSpec((8, d), lambda j: (0, 0)),
                  pl.BlockSpec((d, tn), lambda j: (0, j)),
                  pl.BlockSpec((1, tn), lambda j: (0, j))],
        out_specs=pl.BlockSpec((8, tn), lambda j: (0, j)),
        compiler_params=pltpu.CompilerParams(dimension_semantics=("arbitrary",),
                                             vmem_limit_bytes=VMEM_LIMIT_BYTES),
        name="ada",
    )(c8, w, b)


def _bias_kernel(tab_ref, bd_ref, bp_ref, o_ref):
    g = pl.program_id(0)
    p = pl.program_id(1)
    t = pl.program_id(2)
    head = jnp.where(g == 0, p, N_DIFF_HEADS + 2 * p + t)
    bd = bd_ref[...]
    bp = bp_ref[...]
    d0 = jnp.where(bd < 0, NEG, 0.0).astype(F32)
    d1 = jnp.zeros(bp.shape, F32)
    for b in range(N_BUCKETS):
        val = tab_ref[head * N_BUCKETS + b]
        d0 = jnp.where(bd == b, val, d0)
        d1 = jnp.where(bp == b, val, d1)
    o_ref[0, 0, 0] = d0
    o_ref[0, 0, 1] = d1


def _bias_tiles(rel_bias):
    t = ATT_T
    jj = np.arange(t)[:, None]
    ii = np.arange(t)[None, :]
    bd = np.where(ii >= jj, _t5_bucket_np(ii - jj), -1).astype(np.int32)
    bp = _t5_bucket_np(t + ii - jj)
    tab = rel_bias.T.reshape(-1)
    return pl.pallas_call(
        _bias_kernel,
        out_shape=jax.ShapeDtypeStruct((2, N_PAIRS, 2, t, 2 * t), F32),
        grid=(2, N_PAIRS, 2),
        in_specs=[pl.BlockSpec(memory_space=pltpu.SMEM),
                  pl.BlockSpec((t, t), lambda g, p, h: (0, 0)),
                  pl.BlockSpec((t, t), lambda g, p, h: (0, 0))],
        out_specs=pl.BlockSpec((1, 1, 2, t, t), lambda g, p, h: (g, p, 0, 0, h)),
        compiler_params=pltpu.CompilerParams(
            dimension_semantics=("arbitrary", "arbitrary", "arbitrary")),
        name="bias_tiles",
    )(tab, jnp.asarray(bd), jnp.asarray(bp))


def _proj_kernel(x_ref, g_ref, sh_ref, sc_ref, w_ref, o_ref):
    x = x_ref[0]
    h = _rms(x, g_ref[...]) * (1.0 + sc_ref[0]) + sh_ref[0]
    hb = h.astype(BF16)
    n_total = w_ref.shape[1]
    for n0 in range(0, n_total, 512):
        o_ref[0, :, n0:n0 + 512] = jnp.dot(
            hb, w_ref[:, n0:n0 + 512], preferred_element_type=F32).astype(BF16)


def _proj(x, g_mix, mod, w_in_bf):
    b, s, d = x.shape
    n = w_in_bf.shape[1]
    tm = min(PROJ_TM, s)
    return pl.pallas_call(
        _proj_kernel,
        out_shape=jax.ShapeDtypeStruct((b, s, n), BF16),
        grid=(b, s // tm),
        in_specs=[pl.BlockSpec((1, tm, d), lambda bi, i: (bi, i, 0)),
                  pl.BlockSpec((1, d), lambda bi, i: (0, 0)),
                  pl.BlockSpec((1, 1, d), lambda bi, i: (bi * 6 + 0, 0, 0)),
                  pl.BlockSpec((1, 1, d), lambda bi, i: (bi * 6 + 1, 0, 0)),
                  pl.BlockSpec((d, n), lambda bi, i: (0, 0))],
        out_specs=pl.BlockSpec((1, tm, n), lambda bi, i: (bi, i, 0)),
        compiler_params=pltpu.CompilerParams(dimension_semantics=("arbitrary", "arbitrary"),
                                             vmem_limit_bytes=VMEM_LIMIT_BYTES),
        name="proj",
    )(x, g_mix, mod, mod, w_in_bf)


def _stack_q(q_ref, q2_ref, scale):
    t = ATT_T
    row = lax.broadcasted_iota(I32, (LANES, t), 0)
    for p in range(N_PAIRS):
        q = (q_ref[0, :, p * LANES:(p + 1) * LANES].astype(F32) * scale).T
        q2_ref[p, :, 0:t] = jnp.where(row < 64, q, 0.0).astype(BF16)
        q2_ref[p, :, t:2 * t] = jnp.where(row >= 64, q, 0.0).astype(BF16)


def _transpose_values(v_ref, vt_ref):
    t = ATT_T
    nq = vt_ref.shape[0]

    def body(n, carry):
        for p in range(N_PAIRS):
            blk = v_ref[0, pl.ds(pl.multiple_of(n * t, t), t), p * LANES:(p + 1) * LANES]
            vt_ref[n, p * V_ROWS:p * V_ROWS + LANES, :] = blk.astype(F32).T.astype(BF16)
            vt_ref[n, p * V_ROWS + LANES:(p + 1) * V_ROWS, :] = jnp.ones((V_ROWS - LANES, t), BF16)
        return carry

    lax.fori_loop(0, nq, body, 0)


def _flash_init(m_ref, acc_ref):
    m_ref[...] = jnp.full(m_ref.shape, NEG, F32)
    acc_ref[...] = jnp.zeros(acc_ref.shape, F32)


def _score_slot(p, h, bank):
    return bank * (2 * N_PAIRS) + 2 * p + h


def _flash_scores(p, h, n, bank, q2_ref, k_ref, s_ref):
    t = ATT_T
    kt = k_ref[0, pl.ds(pl.multiple_of(n * t, t), t), p * LANES:(p + 1) * LANES]
    s_ref[_score_slot(p, h, bank)] = jnp.dot(kt, q2_ref[p, :, h * t:(h + 1) * t],
                                             preferred_element_type=F32)


def _flash_absorb(p, h, n, bank, tile_bias, row_bias, live, v_ref, s_ref, m_ref, acc_ref):
    t = ATT_T
    cs = slice(h * t, (h + 1) * t)
    vt = v_ref[n, p * V_ROWS:(p + 1) * V_ROWS, :]
    s = s_ref[_score_slot(p, h, bank)]
    if tile_bias is not None:
        s = s + tile_bias
    mx = jnp.max(s, axis=0, keepdims=True)
    if row_bias is not None:
        mx = mx + row_bias
    if live is not None:
        mx = jnp.where(live, mx, NEG)
    m_prev = m_ref[p, :, cs]
    m_new = jnp.maximum(m_prev, mx)
    alpha = jnp.exp(m_prev - m_new)
    shift = m_new if row_bias is None else m_new - row_bias
    if live is not None:
        shift = jnp.where(live, shift, -NEG)
    pt = jnp.exp(s - shift)
    acc_ref[p, :, cs] = alpha * acc_ref[p, :, cs] + jnp.dot(vt, pt.astype(BF16),
                                                            preferred_element_type=F32)
    m_ref[p, :, cs] = m_new


def _far_bias(tab_ref, head):
    return tab_ref[head * N_BUCKETS + (N_BUCKETS - 1)]


def _flash_sweep(i, far_args, prev_args, own_args, refs):
    q2_ref, k_ref, v_ref, s_ref, m_ref, acc_ref = refs
    chains = [(p, h) for p in range(N_PAIRS) for h in range(2)]

    def scores_tile(n, bank):
        for p, h in chains:
            _flash_scores(p, h, n, bank, q2_ref, k_ref, s_ref)

    def step(n, bank, args, with_next):
        for p, h in chains:
            if with_next:
                _flash_scores(p, h, n + 1, 1 - bank, q2_ref, k_ref, s_ref)
            _flash_absorb(p, h, n, bank, *args(p, h), v_ref, s_ref, m_ref, acc_ref)

    def far_step(n, bank):
        step(n, bank, lambda p, h: far_args(p, h, n), True)

    n_far = jnp.maximum(i - 1, 0)
    lead = jnp.bitwise_and(n_far, 1)

    @pl.when(jnp.bitwise_and(i, 1) == 0)
    def _():
        scores_tile(0, 0)

    @pl.when(jnp.bitwise_and(i, 1) == 1)
    def _():
        scores_tile(0, 1)

    @pl.when(lead == 1)
    def _():
        far_step(0, 0)

    def pair_body(j, carry):
        n = lead + 2 * j
        far_step(n, 1)
        far_step(n + 1, 0)
        return carry

    lax.fori_loop(0, jnp.right_shift(n_far, 1), pair_body, 0)

    @pl.when(i >= 1)
    def _():
        step(i - 1, 1, prev_args, True)

    step(i, 0, own_args, False)


def _diff_kernel(tab_ref, q_ref, k_ref, v_ref, d_ref, lamv_ref, g_ref, o_ref,
                 q2_ref, s_ref, m_ref, acc_ref, vt_ref):
    t = ATT_T
    i = pl.program_id(1)

    @pl.when(i == 0)
    def _():
        _transpose_values(v_ref, vt_ref)

    _stack_q(q_ref, q2_ref, DIFF_HEAD_DIM ** -0.5)
    _flash_init(m_ref, acc_ref)
    refs = (q2_ref, k_ref, vt_ref, s_ref, m_ref, acc_ref)
    _flash_sweep(i,
                 lambda p, h, n: (None, _far_bias(tab_ref, p), None),
                 lambda p, h: (d_ref[0, p, 1, :, h * t:(h + 1) * t], None, None),
                 lambda p, h: (d_ref[0, p, 0, :, h * t:(h + 1) * t], None, None),
                 refs)

    lv = lamv_ref[...]
    lam = (jnp.exp(jnp.sum(lv[0:1] * lv[1:2], axis=1, keepdims=True))
           - jnp.exp(jnp.sum(lv[2:3] * lv[3:4], axis=1, keepdims=True)) + LAM_INIT)
    for p in range(N_PAIRS):
        acc = acc_ref[p, 0:LANES, :]
        l = acc_ref[p, LANES:LANES + 1, :]
        o = acc[:, 0:t] / l[:, 0:t] - lam * (acc[:, t:2 * t] / l[:, t:2 * t])
        y = o * lax.rsqrt(jnp.mean(o * o, axis=0, keepdims=True) + RMS_EPS) * g_ref[...]
        o_ref[0, :, p * LANES:(p + 1) * LANES] = (y * (1.0 - LAM_INIT)).T.astype(BF16)


def _moba_kernel(tab_ref, q_ref, k_ref, v_ref, d_ref, o_ref,
                 q2_ref, s_ref, m_ref, acc_ref, vt_ref, kmean_ref, sel_ref):
    t = ATT_T
    i = pl.program_id(1)
    nb = k_ref.shape[1] // t
    nbp = kmean_ref.shape[1]

    @pl.when(i == 0)
    def _():
        _transpose_values(v_ref, vt_ref)
        kmean_ref[...] = jnp.zeros(kmean_ref.shape, F32)
        for p in range(N_PAIRS):
            for n in range(nb):
                blk = k_ref[0, n * t:(n + 1) * t, p * LANES:(p + 1) * LANES].astype(F32)
                kmean_ref[p, n:n + 1, :] = jnp.sum(blk, axis=0, keepdims=True) * (1.0 / t)

    _stack_q(q_ref, q2_ref, MOBA_HEAD_DIM ** -0.5)
    _flash_init(m_ref, acc_ref)

    blk_id = lax.broadcasted_iota(I32, (nbp, 2 * t), 0)
    blk_f = blk_id.astype(F32)
    for p in range(N_PAIRS):
        gs = jnp.dot(kmean_ref[p].astype(BF16), q2_ref[p], preferred_element_type=F32)
        g = jnp.where(blk_id < i, gs, -jnp.inf)
        sel = jnp.full((nbp, 2 * t), NEG, F32)
        for _ in range(MOBA_TOPK):
            mx = jnp.max(g, axis=0, keepdims=True)
            first = jnp.min(jnp.where(g == mx, blk_f, float(nbp)), axis=0, keepdims=True)
            hit = blk_f == first
            sel = jnp.where(hit, 0.0, sel)
            g = jnp.where(hit, -jnp.inf, g)
        sel_ref[p] = sel

    def live(p, h, n):
        return sel_ref[p, pl.ds(n, 1), h * t:(h + 1) * t] > -1.0

    refs = (q2_ref, k_ref, vt_ref, s_ref, m_ref, acc_ref)
    _flash_sweep(i,
                 lambda p, h, n: (None, _far_bias(tab_ref, N_DIFF_HEADS + 2 * p + h),
                                  live(p, h, n)),
                 lambda p, h: (d_ref[0, p, 1, :, h * t:(h + 1) * t], None, live(p, h, i - 1)),
                 lambda p, h: (d_ref[0, p, 0, :, h * t:(h + 1) * t], None, None),
                 refs)

    row = lax.broadcasted_iota(I32, (LANES, t), 0)
    for p in range(N_PAIRS):
        acc = acc_ref[p, 0:LANES, :]
        l = acc_ref[p, LANES:LANES + 1, :]
        o = jnp.where(row < 64, acc[:, 0:t] / l[:, 0:t], acc[:, t:2 * t] / l[:, t:2 * t])
        o_ref[0, :, p * LANES:(p + 1) * LANES] = o.T.astype(BF16)


def _attention(kind, tab, proj, dtiles, extra, cols):
    b, s, _ = proj.shape
    t = ATT_T
    nq = s // t
    w = N_PAIRS * LANES
    qc, kc, vc = cols
    gidx = 0 if kind == "diff" else 1
    in_specs = [pl.BlockSpec(memory_space=pltpu.SMEM),
                pl.BlockSpec((1, t, w), lambda bi, i: (bi, i, qc // w)),
                pl.BlockSpec((1, s, w), lambda bi, i: (bi, 0, kc // w)),
                pl.BlockSpec((1, s, w), lambda bi, i: (bi, 0, vc // w)),
                pl.BlockSpec((1, N_PAIRS, 2, t, 2 * t), lambda bi, i: (gidx, 0, 0, 0, 0))]
    scratch = [pltpu.VMEM((N_PAIRS, LANES, 2 * t), BF16),
               pltpu.VMEM((2 * 2 * N_PAIRS, t, t), F32),
               pltpu.VMEM((N_PAIRS, 1, 2 * t), F32),
               pltpu.VMEM((N_PAIRS, V_ROWS, 2 * t), F32),
               pltpu.VMEM((nq, N_PAIRS * V_ROWS, t), BF16)]
    if kind == "diff":
        lamv, subln_g = extra
        in_specs += [pl.BlockSpec(lamv.shape, lambda bi, i: (0, 0)),
                     pl.BlockSpec(subln_g.shape, lambda bi, i: (0, 0))]
        args = (tab, proj, proj, proj, dtiles, lamv, subln_g)
        body = _diff_kernel
    else:
        nbp = max(32, nq)
        scratch += [pltpu.VMEM((N_PAIRS, nbp, LANES), F32), pltpu.VMEM((N_PAIRS, nbp, 2 * t), F32)]
        args = (tab, proj, proj, proj, dtiles)
        body = _moba_kernel
    return pl.pallas_call(
        body,
        out_shape=jax.ShapeDtypeStruct((b, s, w), BF16),
        grid=(b, nq),
        in_specs=in_specs,
        out_specs=pl.BlockSpec((1, t, w), lambda bi, i: (bi, i, 0)),
        scratch_shapes=scratch,
        compiler_params=pltpu.CompilerParams(
            dimension_semantics=("arbitrary", "arbitrary"),
            vmem_limit_bytes=VMEM_LIMIT_BYTES),
        name=kind + "_attn",
    )(*args)


def _merge_kernel(x_ref, ya_ref, yb_ref, gla_ref, glb_ref, wa_ref, wb_ref, wo_ref,
                  gt_ref, shf_ref, scf_ref, gf_ref, wr_ref, br_ref,
                  x1_ref, h2_ref, ti_ref, tw_ref):
    ts = MERGE_SUB
    subs = [slice(j * ts, (j + 1) * ts) for j in range(x_ref.shape[1] // ts)]
    branch = [(jnp.dot(ya_ref[0, sl], wa_ref[...], preferred_element_type=F32),
               jnp.dot(yb_ref[0, sl], wb_ref[...], preferred_element_type=F32)) for sl in subs]
    mixed = []
    for sl, (a, b) in zip(subs, branch):
        merged = (jax.nn.sigmoid(gla_ref[0, sl].astype(F32)) * a
                  + jax.nn.sigmoid(glb_ref[0, sl].astype(F32)) * b)
        mixed.append(jnp.dot(merged.astype(BF16), wo_ref[...], preferred_element_type=F32))
    lane = lax.broadcasted_iota(I32, (ts, LANES), 1)
    lane_f = lane.astype(F32)
    for j, (sl, mo) in enumerate(zip(subs, mixed)):
        x1 = x_ref[0, sl] + gt_ref[0] * mo
        x1_ref[0, sl] = x1
        h = _rms(x1, gf_ref[...]) * (1.0 + scf_ref[0]) + shf_ref[0]
        _store_token_tiles(h2_ref.at[0, j * ts * ROW_CHUNKS:(j + 1) * ts * ROW_CHUNKS], h)
        logits = jnp.dot(h.astype(BF16), wr_ref[...], preferred_element_type=F32) + br_ref[...]
        g = jnp.where(lane < N_EXPERTS, logits, -jnp.inf)
        vals, idxs = [], []
        for _ in range(TOP_K):
            mx = jnp.max(g, axis=1, keepdims=True)
            first = jnp.min(jnp.where(g == mx, lane_f, float(LANES)), axis=1, keepdims=True)
            vals.append(mx)
            idxs.append(first)
            g = jnp.where(lane_f == first, -jnp.inf, g)
        es = [jnp.exp(v - vals[0]) for v in vals]
        den = es[0] + es[1] + es[2] + es[3]
        ti = jnp.zeros((ts, LANES), F32)
        tw = jnp.zeros((ts, LANES), F32)
        for k in range(TOP_K):
            ti = jnp.where(lane == k, idxs[k], ti)
            tw = jnp.where(lane == k, es[k] / den, tw)
        ti_ref[0, sl] = ti.astype(I32)
        tw_ref[0, sl] = tw


def _merge(x, proj, y_a, y_b, w_br_a, w_br_b, w_out, mod, g_ffn, w_router, b_router):
    b, s, d = x.shape
    tm = min(MERGE_TM, s)
    full = lambda shape: pl.BlockSpec(shape, lambda bi, i: (0,) * len(shape))
    tile = lambda w, c=0: pl.BlockSpec((1, tm, w), lambda bi, i: (bi, i, c))
    modspec = lambda j: pl.BlockSpec((1, 1, d), lambda bi, i: (bi * 6 + j, 0, 0))
    return pl.pallas_call(
        _merge_kernel,
        out_shape=(jax.ShapeDtypeStruct((b, s, d), F32),
                   jax.ShapeDtypeStruct((b, s * ROW_CHUNKS, LANES), F32),
                   jax.ShapeDtypeStruct((b, s, LANES), I32),
                   jax.ShapeDtypeStruct((b, s, LANES), F32)),
        grid=(b, s // tm),
        in_specs=[tile(d), tile(512), tile(512), tile(d, 3), tile(d, 4),
                  full(w_br_a.shape), full(w_br_b.shape), full(w_out.shape),
                  modspec(2), modspec(3), modspec(4), full(g_ffn.shape),
                  full(w_router.shape), full(b_router.shape)],
        out_specs=(tile(d), pl.BlockSpec((1, tm * ROW_CHUNKS, LANES), lambda bi, i: (bi, i, 0)),
                   tile(LANES), tile(LANES)),
        compiler_params=pltpu.CompilerParams(dimension_semantics=("arbitrary", "arbitrary"),
                                             vmem_limit_bytes=VMEM_LIMIT_BYTES),
        name="merge",
    )(x, y_a, y_b, proj, proj, w_br_a, w_br_b, w_out, mod, mod, mod, g_ffn,
      w_router, b_router)


def _dispatch_kernel(padtile_ref, nu_ref, pos_ref, h2_ref, xs_ref, zeros_ref, zsem, sem):
    step = pl.program_id(0)
    rc = ROW_CHUNKS
    n_tok = h2_ref.shape[0] // rc
    pad_rows = zeros_ref.shape[0]
    n_tiles = xs_ref.shape[0] // pad_rows

    def zero_tile(tile):
        start = pl.multiple_of(tile * pad_rows, pad_rows)
        return pltpu.make_async_copy(zeros_ref, xs_ref.at[pl.ds(start, pad_rows)], zsem)

    @pl.when(step == 0)
    def _():
        zeros_ref[...] = jnp.zeros(zeros_ref.shape, F32)
        for e in range(N_EXPERTS):
            @pl.when(padtile_ref[e] >= 0)
            def _():
                zero_tile(padtile_ref[e]).start()
        for e in range(N_EXPERTS):
            @pl.when(padtile_ref[e] >= 0)
            def _():
                zero_tile(padtile_ref[e]).wait()

        def start_tail(tile, carry):
            zero_tile(tile).start()
            return carry

        def wait_tail(tile, carry):
            zero_tile(tile).wait()
            return carry

        lax.fori_loop(nu_ref[0], n_tiles, start_tail, 0)
        lax.fori_loop(nu_ref[0], n_tiles, wait_tail, 0)

    def issue(j, carry):
        src = h2_ref.at[pl.ds(pl.multiple_of(j * rc, rc), rc)]
        for k in range(TOP_K):
            dst = xs_ref.at[pl.ds(pl.multiple_of(pos_ref[j * TOP_K + k], rc), rc)]
            pltpu.make_async_copy(src, dst, sem).start(priority=k % 2)
        return carry

    lax.fori_loop(0, n_tok, issue, 0, unroll=ROW_DMA_UNROLL)
    for k in range(TOP_K):
        pltpu.make_async_copy(h2_ref, xs_ref.at[pl.ds(0, n_tok * rc)], sem).wait()


def _dispatch(h2_tiles, pos_start, padtile, n_used, n_rows):
    rc = ROW_CHUNKS
    t_tok = h2_tiles.shape[0] // rc
    tm = min(DISPATCH_TM, t_tok)
    return pl.pallas_call(
        _dispatch_kernel,
        out_shape=jax.ShapeDtypeStruct((n_rows * rc, LANES), F32),
        grid_spec=pltpu.PrefetchScalarGridSpec(
            num_scalar_prefetch=2,
            grid=(t_tok // tm,),
            in_specs=[pl.BlockSpec((tm * TOP_K,), lambda s, pt, nu: (s,),
                                   memory_space=pltpu.SMEM),
                      pl.BlockSpec((tm * rc, LANES), lambda s, pt, nu: (s, 0))],
            out_specs=pl.BlockSpec(memory_space=pl.ANY),
            scratch_shapes=[pltpu.VMEM((MOE_TM * rc, LANES), F32),
                            pltpu.SemaphoreType.DMA, pltpu.SemaphoreType.DMA]),
        compiler_params=pltpu.CompilerParams(dimension_semantics=("arbitrary",),
                                             vmem_limit_bytes=VMEM_LIMIT_BYTES),
        name="dispatch",
    )(padtile, n_used, pos_start, h2_tiles)


def _moe_kernel(te_ref, nu_ref, xs_ref, wgu_ref, bgu_ref, wd_ref, bd_ref,
                ys_ref, wgu_bf, wd_bf, gu_ref):
    r = pl.program_id(0)
    busy = r < nu_ref[0]
    fresh = jnp.logical_or(r == 0, te_ref[r] != te_ref[jnp.maximum(r - 1, 0)])

    @pl.when(jnp.logical_and(busy, fresh))
    def _():
        wgu_bf[...] = wgu_ref[0].astype(BF16)
        wd_bf[...] = wd_ref[0].astype(BF16)

    @pl.when(jnp.logical_not(busy))
    def _():
        ys_ref[...] = jnp.zeros(ys_ref.shape, F32)

    @pl.when(busy)
    def _():
        x = _load_token_tiles(xs_ref).astype(BF16)
        fc = MOE_FF_CHUNK
        for c in range(2 * D_FF // fc):
            lo, hi = c * fc, (c + 1) * fc
            gu_ref[:, lo:hi] = (jnp.dot(x, wgu_bf[:, lo:hi], preferred_element_type=F32)
                                + bgu_ref[0][:, lo:hi])
        y = None
        for c in range(D_FF // fc):
            lo, hi = c * fc, (c + 1) * fc
            gate = jnp.minimum(gu_ref[:, lo:hi], SWIGLU_LIMIT)
            up = jnp.clip(gu_ref[:, D_FF + lo:D_FF + hi], -SWIGLU_LIMIT, SWIGLU_LIMIT)
            act = (up + 1.0) * gate * jax.nn.sigmoid(SWIGLU_ALPHA * gate)
            part = jnp.dot(act.astype(BF16), wd_bf[lo:hi, :], preferred_element_type=F32)
            y = part if y is None else y + part
        _store_token_tiles(ys_ref, y + bd_ref[0])


def _moe(xs_tiles, tile_expert, n_used, w_gate_up, b_gate_up, w_down, b_down):
    d = D_MODEL
    rc = ROW_CHUNKS
    tm = MOE_TM
    n_tiles = tile_expert.shape[0]
    e, _, f2 = w_gate_up.shape
    wsel = lambda r, te, nu: (te[r], 0, 0)
    rows = lambda r, te, nu: (jnp.minimum(r, nu[0] - 1), 0)
    return pl.pallas_call(
        _moe_kernel,
        out_shape=jax.ShapeDtypeStruct((n_tiles * tm * rc, LANES), F32),
        grid_spec=pltpu.PrefetchScalarGridSpec(
            num_scalar_prefetch=2,
            grid=(n_tiles,),
            in_specs=[pl.BlockSpec((tm * rc, LANES), rows),
                      pl.BlockSpec((1, d, f2), wsel),
                      pl.BlockSpec((1, 1, f2), wsel),
                      pl.BlockSpec((1, D_FF, d), wsel),
                      pl.BlockSpec((1, 1, d), wsel)],
            out_specs=pl.BlockSpec((tm * rc, LANES), lambda r, te, nu: (r, 0)),
            scratch_shapes=[pltpu.VMEM((d, f2), BF16), pltpu.VMEM((D_FF, d), BF16),
                            pltpu.VMEM((tm, f2), F32)]),
        compiler_params=pltpu.CompilerParams(dimension_semantics=("arbitrary",),
                                             vmem_limit_bytes=VMEM_LIMIT_BYTES),
        name="moe",
    )(tile_expert, n_used, xs_tiles, w_gate_up, b_gate_up, w_down, b_down)


def _combine_kernel(pos_ref, ys_ref, x1_ref, tw_ref, gt_ref, gf_ref, o_ref, gbuf, sem):
    tm = x1_ref.shape[0]
    rc = ROW_CHUNKS

    def issue(j, carry):
        for k in range(TOP_K):
            src = ys_ref.at[pl.ds(pl.multiple_of(pos_ref[j * TOP_K + k], rc), rc)]
            dst = gbuf.at[k, pl.ds(pl.multiple_of(j * rc, rc), rc)]
            pltpu.make_async_copy(src, dst, sem).start(priority=k % 2)
        return carry

    lax.fori_loop(0, tm, issue, 0, unroll=ROW_DMA_UNROLL)
    for k in range(TOP_K):
        pltpu.make_async_copy(ys_ref.at[pl.ds(0, tm * rc)], gbuf.at[k], sem).wait()

    tw = tw_ref[...]
    acc = tw[:, 0:1] * _load_token_tiles(gbuf.at[0])
    for k in range(1, TOP_K):
        acc = acc + tw[:, k:k + 1] * _load_token_tiles(gbuf.at[k])
    x2 = x1_ref[...] + gt_ref[0] * acc
    o_ref[...] = _rms(x2, gf_ref[...])


def _combine(ys, pos_flat, x1, tw, mod, g_final, seq):
    t, d = x1.shape
    tm = min(COMBINE_TM, t)
    return pl.pallas_call(
        _combine_kernel,
        out_shape=jax.ShapeDtypeStruct((t, d), F32),
        grid=(t // tm,),
        in_specs=[pl.BlockSpec((tm * TOP_K,), lambda s: (s,), memory_space=pltpu.SMEM),
                  pl.BlockSpec(memory_space=pl.ANY),
                  pl.BlockSpec((tm, d), lambda s: (s, 0)),
                  pl.BlockSpec((tm, LANES), lambda s: (s, 0)),
                  pl.BlockSpec((1, 1, d), lambda s: ((s * tm) // seq * 6 + 5, 0, 0)),
                  pl.BlockSpec((1, d), lambda s: (0, 0))],
        out_specs=pl.BlockSpec((tm, d), lambda s: (s, 0)),
        scratch_shapes=[pltpu.VMEM((TOP_K, tm * ROW_CHUNKS, LANES), F32),
                        pltpu.SemaphoreType.DMA],
        compiler_params=pltpu.CompilerParams(dimension_semantics=("arbitrary",),
                                             vmem_limit_bytes=VMEM_LIMIT_BYTES),
        name="combine",
    )(pos_flat, ys, x1, tw, mod, g_final)


def _routing_tables(top_idx, n_tiles):
    tm = MOE_TM
    e_flat = top_idx.reshape(-1)
    onehot = (e_flat[:, None] == jnp.arange(N_EXPERTS, dtype=I32)[None, :]).astype(I32)
    csum = jnp.cumsum(onehot, axis=0)
    counts = csum[-1]
    rank = jnp.sum(csum * onehot, axis=1) - 1
    tiles_per = (counts + tm - 1) // tm
    tile_end = jnp.cumsum(tiles_per)
    tile_start = tile_end - tiles_per
    pos = (jnp.sum(onehot * tile_start[None, :], axis=1) * tm + rank).astype(I32)
    n_used = tile_end[-1:].astype(I32)
    tile_ids = jnp.arange(n_tiles, dtype=I32)
    tile_expert = jnp.minimum(
        jnp.sum((tile_ids[:, None] >= tile_end[None, :]).astype(I32), axis=1),
        N_EXPERTS - 1).astype(I32)
    padtile = jnp.where(counts % tm != 0, tile_end - 1, -1).astype(I32)
    return pos, tile_expert, n_used, padtile


def kernel(x, c, rel_bias, w_ada, b_ada, g_mix, w_in, lambda_q1, lambda_k1, lambda_q2,
           lambda_k2, subln_g, w_br_a, w_br_b, w_out, g_ffn, w_router, b_router,
           w_gate_up, b_gate_up, w_down, b_down, g_final):
    b, s, d = x.shape
    assert d == D_MODEL and s % ATT_T == 0 and MOBA_BLOCK == ATT_T
    t_tok = b * s

    c8 = jnp.zeros((8, d), F32).at[:b].set(c)
    ada = _ada(c8, w_ada[0], b_ada[0].reshape(1, -1))
    mod = ada[:b].reshape(b * 6, 1, d)

    dtiles = _bias_tiles(rel_bias)
    tab = rel_bias.T.reshape(-1)

    proj = _proj(x, g_mix[0].reshape(1, d), mod, w_in[0].astype(BF16))

    lamv = jnp.concatenate([lambda_q1, lambda_k1, lambda_q2, lambda_k2], axis=0)
    y_a = _attention("diff", tab, proj, dtiles, (lamv, subln_g[0].reshape(-1, 1)), (0, 512, 1024))
    y_b = _attention("moba", tab, proj, dtiles, None, (1536, 2048, 2560))

    wr = jnp.zeros((d, LANES), BF16).at[:, :N_EXPERTS].set(w_router[0].astype(BF16))
    br = jnp.zeros((1, LANES), F32).at[0, :N_EXPERTS].set(b_router[0])
    x1, h2, ti, tw = _merge(x, proj, y_a, y_b, w_br_a[0].astype(BF16), w_br_b[0].astype(BF16),
                            w_out[0].astype(BF16), mod, g_ffn[0].reshape(1, d), wr, br)

    n_tiles = (t_tok * TOP_K) // MOE_TM + N_EXPERTS
    top_idx = ti.reshape(t_tok, LANES)[:, :TOP_K]
    pos, tile_expert, n_used, padtile = _routing_tables(top_idx, n_tiles)
    pos_start = pos * ROW_CHUNKS

    xs = _dispatch(h2.reshape(t_tok * ROW_CHUNKS, LANES), pos_start, padtile, n_used,
                   n_tiles * MOE_TM)
    ys = _moe(xs, tile_expert, n_used, w_gate_up[0], b_gate_up[0].reshape(N_EXPERTS, 1, -1),
              w_down[0], b_down[0].reshape(N_EXPERTS, 1, -1))
    out = _combine(ys, pos_start, x1.reshape(t_tok, d), tw.reshape(t_tok, LANES), mod,
                   g_final.reshape(1, d), s)
    return out.reshape(b, s, d)
```

```python
import functools
import math

import jax
import jax.numpy as jnp
import numpy as np
from jax import lax
from jax.experimental import pallas as pl
from jax.experimental.pallas import tpu as pltpu

F32 = jnp.float32
BF16 = jnp.bfloat16
I32 = jnp.int32

D_MODEL = 1024
N_DIFF_HEADS = 4
DIFF_HEAD_DIM = 64
N_MOBA_HEADS = 8
MOBA_HEAD_DIM = 64
MOBA_BLOCK = 256
MOBA_TOPK = 3
N_HEADS_TOTAL = N_DIFF_HEADS + N_MOBA_HEADS
N_BUCKETS = 32
MAX_DISTANCE = 128
N_EXPERTS = 32
TOP_K = 4
D_FF = D_MODEL
SWIGLU_LIMIT = 7.0
SWIGLU_ALPHA = 1.702
RMS_EPS = 1e-5
LAM_INIT = 0.8 - 0.6 * math.exp(-0.3 * 0)
IN_WIDTH = 5120

LANES = 128
VMEM_LIMIT_BYTES = 56 * 1024 * 1024

ATT_T = 256
N_PAIRS = 4
V_ROWS = LANES + 16
PROJ_TM = 512
MERGE_TM = 512
MERGE_SUB = 256
MOE_TM = 512
MOE_FF_CHUNK = 256
DISPATCH_TM = 1024
ROW_DMA_UNROLL = 8
COMBINE_TM = 1024
NEG = -1e30


def _t5_bucket_np(dist):
    n = np.maximum(dist, 0)
    max_exact = N_BUCKETS // 2
    nf = np.maximum(n, max_exact).astype(np.float32)
    large = max_exact + (np.log(nf / max_exact) / math.log(MAX_DISTANCE / max_exact)
                         * (N_BUCKETS - max_exact)).astype(np.int32)
    large = np.minimum(large, N_BUCKETS - 1)
    return np.where(n < max_exact, n, large).astype(np.int32)


def _rms(x, g):
    return x * lax.rsqrt(jnp.mean(x * x, axis=-1, keepdims=True) + RMS_EPS) * g


ROW_CHUNKS = D_MODEL // LANES


def _store_token_tiles(ref, x):
    n = x.shape[0]
    for c in range(ROW_CHUNKS):
        ref[pl.ds(c, n, stride=ROW_CHUNKS), :] = x[:, c * LANES:(c + 1) * LANES]


def _load_token_tiles(ref):
    n = ref.shape[0] // ROW_CHUNKS
    return jnp.concatenate([ref[pl.ds(c, n, stride=ROW_CHUNKS), :] for c in range(ROW_CHUNKS)],
                           axis=1)


def _ada_kernel(c_ref, w_ref, b_ref, o_ref):
    o_ref[...] = jnp.dot(c_ref[...].astype(BF16), w_ref[...].astype(BF16),
                         preferred_element_type=F32) + b_ref[...]


def _ada(c8, w, b):
    d, n = w.shape
    tn = 1536
    return pl.pallas_call(
        _ada_kernel,
        out_shape=jax.ShapeDtypeStruct((8, n), F32),
        grid=(n // tn,),
        in_specs=[pl.BlockSpec((8, d), lambda j: (0, 0)),
                  pl.BlockSpec((d, tn), lambda j: (0, j)),
                  pl.BlockSpec((1, tn), lambda j: (0, j))],
        out_specs=pl.BlockSpec((8, tn), lambda j: (0, j)),
        compiler_params=pltpu.CompilerParams(dimension_semantics=("arbitrary",),
                                             vmem_limit_bytes=VMEM_LIMIT_BYTES),
        name="ada",
    )(c8, w, b)


def _bias_kernel(tab_ref, bd_ref, bp_ref, o_ref):
    g = pl.program_id(0)
    p = pl.program_id(1)
    t = pl.program_id(2)
    head = jnp.where(g == 0, p, N_DIFF_HEADS + 2 * p + t)
    bd = bd_ref[...]
    bp = bp_ref[...]
    d0 = jnp.where(bd < 0, NEG, 0.0).astype(F32)
    d1 = jnp.zeros(bp.shape, F32)
    for b in range(N_BUCKETS):
        val = tab_ref[head * N_BUCKETS + b]
        d0 = jnp.where(bd == b, val, d0)
        d1 = jnp.where(bp == b, val, d1)
    o_ref[0, 0, 0] = d0
    o_ref[0, 0, 1] = d1


def _bias_tiles(rel_bias):
    t = ATT_T
    jj = np.arange(t)[:, None]
    ii = np.arange(t)[None, :]
    bd = np.where(ii >= jj, _t5_bucket_np(ii - jj), -1).astype(np.int32)
    bp = _t5_bucket_np(t + ii - jj)
    tab = rel_bias.T.reshape(-1)
    return pl.pallas_call(
        _bias_kernel,
        out_shape=jax.ShapeDtypeStruct((2, N_PAIRS, 2, t, 2 * t), F32),
        grid=(2, N_PAIRS, 2),
        in_specs=[pl.BlockSpec(memory_space=pltpu.SMEM),
                  pl.BlockSpec((t, t), lambda g, p, h: (0, 0)),
                  pl.BlockSpec((t, t), lambda g, p, h: (0, 0))],
        out_specs=pl.BlockSpec((1, 1, 2, t, t), lambda g, p, h: (g, p, 0, 0, h)),
        compiler_params=pltpu.CompilerParams(
            dimension_semantics=("arbitrary", "arbitrary", "arbitrary")),
        name="bias_tiles",
    )(tab, jnp.asarray(bd), jnp.asarray(bp))


def _proj_kernel(x_ref, g_ref, sh_ref, sc_ref, w_ref, o_ref):
    x = x_ref[0]
    h = _rms(x, g_ref[...]) * (1.0 + sc_ref[0]) + sh_ref[0]
    hb = h.astype(BF16)
    n_total = w_ref.shape[1]
    for n0 in range(0, n_total, 512):
        o_ref[0, :, n0:n0 + 512] = jnp.dot(
            hb, w_ref[:, n0:n0 + 512], preferred_element_type=F32).astype(BF16)


def _proj(x, g_mix, mod, w_in_bf):
    b, s, d = x.shape
    n = w_in_bf.shape[1]
    tm = min(PROJ_TM, s)
    return pl.pallas_call(
        _proj_kernel,
        out_shape=jax.ShapeDtypeStruct((b, s, n), BF16),
        grid=(b, s // tm),
        in_specs=[pl.BlockSpec((1, tm, d), lambda bi, i: (bi, i, 0)),
                  pl.BlockSpec((1, d), lambda bi, i: (0, 0)),
                  pl.BlockSpec((1, 1, d), lambda bi, i: (bi * 6 + 0, 0, 0)),
                  pl.BlockSpec((1, 1, d), lambda bi, i: (bi * 6 + 1, 0, 0)),
                  pl.BlockSpec((d, n), lambda bi, i: (0, 0))],
        out_specs=pl.BlockSpec((1, tm, n), lambda bi, i: (bi, i, 0)),
        compiler_params=pltpu.CompilerParams(dimension_semantics=("arbitrary", "arbitrary"),
                                             vmem_limit_bytes=VMEM_LIMIT_BYTES),
        name="proj",
    )(x, g_mix, mod, mod, w_in_bf)


def _stack_q(q_ref, q2_ref, scale):
    t = ATT_T
    row = lax.broadcasted_iota(I32, (LANES, t), 0)
    for p in range(N_PAIRS):
        q = (q_ref[0, :, p * LANES:(p + 1) * LANES].astype(F32) * scale).T
        q2_ref[p, :, 0:t] = jnp.where(row < 64, q, 0.0).astype(BF16)
        q2_ref[p, :, t:2 * t] = jnp.where(row >= 64, q, 0.0).astype(BF16)


def _transpose_values(v_ref, vt_ref):
    t = ATT_T
    nq = vt_ref.shape[0]

    def body(n, carry):
        for p in range(N_PAIRS):
            blk = v_ref[0, pl.ds(pl.multiple_of(n * t, t), t), p * LANES:(p + 1) * LANES]
            vt_ref[n, p * V_ROWS:p * V_ROWS + LANES, :] = blk.astype(F32).T.astype(BF16)
            vt_ref[n, p * V_ROWS + LANES:(p + 1) * V_ROWS, :] = jnp.ones((V_ROWS - LANES, t), BF16)
        return carry

    lax.fori_loop(0, nq, body, 0)


def _flash_init(m_ref, acc_ref):
    m_ref[...] = jnp.full(m_ref.shape, NEG, F32)
    acc_ref[...] = jnp.zeros(acc_ref.shape, F32)


def _score_slot(p, h, bank):
    return bank * (2 * N_PAIRS) + 2 * p + h


def _flash_scores(p, h, n, bank, q2_ref, k_ref, s_ref):
    t = ATT_T
    kt = k_ref[0, pl.ds(pl.multiple_of(n * t, t), t), p * LANES:(p + 1) * LANES]
    s_ref[_score_slot(p, h, bank)] = jnp.dot(kt, q2_ref[p, :, h * t:(h + 1) * t],
                                             preferred_element_type=F32)


def _flash_absorb(p, h, n, bank, tile_bias, row_bias, live, v_ref, s_ref, m_ref, acc_ref):
    t = ATT_T
    cs = slice(h * t, (h + 1) * t)
    vt = v_ref[n, p * V_ROWS:(p + 1) * V_ROWS, :]
    s = s_ref[_score_slot(p, h, bank)]
    if tile_bias is not None:
        s = s + tile_bias
    mx = jnp.max(s, axis=0, keepdims=True)
    if row_bias is not None:
        mx = mx + row_bias
    if live is not None:
        mx = jnp.where(live, mx, NEG)
    m_prev = m_ref[p, :, cs]
    m_new = jnp.maximum(m_prev, mx)
    alpha = jnp.exp(m_prev - m_new)
    shift = m_new if row_bias is None else m_new - row_bias
    if live is not None:
        shift = jnp.where(live, shift, -NEG)
    pt = jnp.exp(s - shift)
    acc_ref[p, :, cs] = alpha * acc_ref[p, :, cs] + jnp.dot(vt, pt.astype(BF16),
                                                            preferred_element_type=F32)
    m_ref[p, :, cs] = m_new


def _far_bias(tab_ref, head):
    return tab_ref[head * N_BUCKETS + (N_BUCKETS - 1)]


def _flash_sweep(i, far_args, prev_args, own_args, refs):
    q2_ref, k_ref, v_ref, s_ref, m_ref, acc_ref = refs
    chains = [(p, h) for p in range(N_PAIRS) for h in range(2)]

    def scores_tile(n, bank):
        for p, h in chains:
            _flash_scores(p, h, n, bank, q2_ref, k_ref, s_ref)

    def step(n, bank, args, with_next):
        for p, h in chains:
            if with_next:
                _flash_scores(p, h, n + 1, 1 - bank, q2_ref, k_ref, s_ref)
            _flash_absorb(p, h, n, bank, *args(p, h), v_ref, s_ref, m_ref, acc_ref)

    def far_step(n, bank):
        step(n, bank, lambda p, h: far_args(p, h, n), True)

    n_far = jnp.maximum(i - 1, 0)
    lead = jnp.bitwise_and(n_far, 1)

    @pl.when(jnp.bitwise_and(i, 1) == 0)
    def _():
        scores_tile(0, 0)

    @pl.when(jnp.bitwise_and(i, 1) == 1)
    def _():
        scores_tile(0, 1)

    @pl.when(lead == 1)
    def _():
        far_step(0, 0)

    def pair_body(j, carry):
        n = lead + 2 * j
        far_step(n, 1)
        far_step(n + 1, 0)
        return carry

    lax.fori_loop(0, jnp.right_shift(n_far, 1), pair_body, 0)

    @pl.when(i >= 1)
    def _():
        step(i - 1, 1, prev_args, True)

    step(i, 0, own_args, False)


def _diff_kernel(tab_ref, q_ref, k_ref, v_ref, d_ref, lamv_ref, g_ref, o_ref,
                 q2_ref, s_ref, m_ref, acc_ref, vt_ref):
    t = ATT_T
    i = pl.program_id(1)

    @pl.when(i == 0)
    def _():
        _transpose_values(v_ref, vt_ref)

    _stack_q(q_ref, q2_ref, DIFF_HEAD_DIM ** -0.5)
    _flash_init(m_ref, acc_ref)
    refs = (q2_ref, k_ref, vt_ref, s_ref, m_ref, acc_ref)
    _flash_sweep(i,
                 lambda p, h, n: (None, _far_bias(tab_ref, p), None),
                 lambda p, h: (d_ref[0, p, 1, :, h * t:(h + 1) * t], None, None),
                 lambda p, h: (d_ref[0, p, 0, :, h * t:(h + 1) * t], None, None),
                 refs)

    lv = lamv_ref[...]
    lam = (jnp.exp(jnp.sum(lv[0:1] * lv[1:2], axis=1, keepdims=True))
           - jnp.exp(jnp.sum(lv[2:3] * lv[3:4], axis=1, keepdims=True)) + LAM_INIT)
    for p in range(N_PAIRS):
        acc = acc_ref[p, 0:LANES, :]
        l = acc_ref[p, LANES:LANES + 1, :]
        o = acc[:, 0:t] / l[:, 0:t] - lam * (acc[:, t:2 * t] / l[:, t:2 * t])
        y = o * lax.rsqrt(jnp.mean(o * o, axis=0, keepdims=True) + RMS_EPS) * g_ref[...]
        o_ref[0, :, p * LANES:(p + 1) * LANES] = (y * (1.0 - LAM_INIT)).T.astype(BF16)


def _moba_kernel(tab_ref, q_ref, k_ref, v_ref, d_ref, o_ref,
                 q2_ref, s_ref, m_ref, acc_ref, vt_ref, kmean_ref, sel_ref):
    t = ATT_T
    i = pl.program_id(1)
    nb = k_ref.shape[1] // t
    nbp = kmean_ref.shape[1]

    @pl.when(i == 0)
    def _():
        _transpose_values(v_ref, vt_ref)
        kmean_ref[...] = jnp.zeros(kmean_ref.shape, F32)
        for p in range(N_PAIRS):
            for n in range(nb):
                blk = k_ref[0, n * t:(n + 1) * t, p * LANES:(p + 1) * LANES].astype(F32)
                kmean_ref[p, n:n + 1, :] = jnp.sum(blk, axis=0, keepdims=True) * (1.0 / t)

    _stack_q(q_ref, q2_ref, MOBA_HEAD_DIM ** -0.5)
    _flash_init(m_ref, acc_ref)

    blk_id = lax.broadcasted_iota(I32, (nbp, 2 * t), 0)
    blk_f = blk_id.astype(F32)
    for p in range(N_PAIRS):
        gs = jnp.dot(kmean_ref[p].astype(BF16), q2_ref[p], preferred_element_type=F32)
        g = jnp.where(blk_id < i, gs, -jnp.inf)
        sel = jnp.full((nbp, 2 * t), NEG, F32)
        for _ in range(MOBA_TOPK):
            mx = jnp.max(g, axis=0, keepdims=True)
            first = jnp.min(jnp.where(g == mx, blk_f, float(nbp)), axis=0, keepdims=True)
            hit = blk_f == first
            sel = jnp.where(hit, 0.0, sel)
            g = jnp.where(hit, -jnp.inf, g)
        sel_ref[p] = sel

    def live(p, h, n):
        return sel_ref[p, pl.ds(n, 1), h * t:(h + 1) * t] > -1.0

    refs = (q2_ref, k_ref, vt_ref, s_ref, m_ref, acc_ref)
    _flash_sweep(i,
                 lambda p, h, n: (None, _far_bias(tab_ref, N_DIFF_HEADS + 2 * p + h),
                                  live(p, h, n)),
                 lambda p, h: (d_ref[0, p, 1, :, h * t:(h + 1) * t], None, live(p, h, i - 1)),
                 lambda p, h: (d_ref[0, p, 0, :, h * t:(h + 1) * t], None, None),
                 refs)

    row = lax.broadcasted_iota(I32, (LANES, t), 0)
    for p in range(N_PAIRS):
        acc = acc_ref[p, 0:LANES, :]
        l = acc_ref[p, LANES:LANES + 1, :]
        o = jnp.where(row < 64, acc[:, 0:t] / l[:, 0:t], acc[:, t:2 * t] / l[:, t:2 * t])
        o_ref[0, :, p * LANES:(p + 1) * LANES] = o.T.astype(BF16)


def _attention(kind, tab, proj, dtiles, extra, cols):
    b, s, _ = proj.shape
    t = ATT_T
    nq = s // t
    w = N_PAIRS * LANES
    qc, kc, vc = cols
    gidx = 0 if kind == "diff" else 1
    in_specs = [pl.BlockSpec(memory_space=pltpu.SMEM),
                pl.BlockSpec((1, t, w), lambda bi, i: (bi, i, qc // w)),
                pl.BlockSpec((1, s, w), lambda bi, i: (bi, 0, kc // w)),
                pl.BlockSpec((1, s, w), lambda bi, i: (bi, 0, vc // w)),
                pl.BlockSpec((1, N_PAIRS, 2, t, 2 * t), lambda bi, i: (gidx, 0, 0, 0, 0))]
    scratch = [pltpu.VMEM((N_PAIRS, LANES, 2 * t), BF16),
               pltpu.VMEM((2 * 2 * N_PAIRS, t, t), F32),
               pltpu.VMEM((N_PAIRS, 1, 2 * t), F32),
               pltpu.VMEM((N_PAIRS, V_ROWS, 2 * t), F32),
               pltpu.VMEM((nq, N_PAIRS * V_ROWS, t), BF16)]
    if kind == "diff":
        lamv, subln_g = extra
        in_specs += [pl.BlockSpec(lamv.shape, lambda bi, i: (0, 0)),
                     pl.BlockSpec(subln_g.shape, lambda bi, i: (0, 0))]
        args = (tab, proj, proj, proj, dtiles, lamv, subln_g)
        body = _diff_kernel
    else:
        nbp = max(32, nq)
        scratch += [pltpu.VMEM((N_PAIRS, nbp, LANES), F32), pltpu.VMEM((N_PAIRS, nbp, 2 * t), F32)]
        args = (tab, proj, proj, proj, dtiles)
        body = _moba_kernel
    return pl.pallas_call(
        body,
        out_shape=jax.ShapeDtypeStruct((b, s, w), BF16),
        grid=(b, nq),
        in_specs=in_specs,
        out_specs=pl.BlockSpec((1, t, w), lambda bi, i: (bi, i, 0)),
        scratch_shapes=scratch,
        compiler_params=pltpu.CompilerParams(
            dimension_semantics=("arbitrary", "arbitrary"),
            vmem_limit_bytes=VMEM_LIMIT_BYTES),
        name=kind + "_attn",
    )(*args)


def _merge_kernel(x_ref, ya_ref, yb_ref, gla_ref, glb_ref, wa_ref, wb_ref, wo_ref,
                  gt_ref, shf_ref, scf_ref, gf_ref, wr_ref, br_ref,
                  x1_ref, h2_ref, ti_ref, tw_ref):
    ts = MERGE_SUB
    subs = [slice(j * ts, (j + 1) * ts) for j in range(x_ref.shape[1] // ts)]
    branch = [(jnp.dot(ya_ref[0, sl], wa_ref[...], preferred_element_type=F32),
               jnp.dot(yb_ref[0, sl], wb_ref[...], preferred_element_type=F32)) for sl in subs]
    mixed = []
    for sl, (a, b) in zip(subs, branch):
        merged = (jax.nn.sigmoid(gla_ref[0, sl].astype(F32)) * a
                  + jax.nn.sigmoid(glb_ref[0, sl].astype(F32)) * b)
        mixed.append(jnp.dot(merged.astype(BF16), wo_ref[...], preferred_element_type=F32))
    lane = lax.broadcasted_iota(I32, (ts, LANES), 1)
    lane_f = lane.astype(F32)
    for j, (sl, mo) in enumerate(zip(subs, mixed)):
        x1 = x_ref[0, sl] + gt_ref[0] * mo
        x1_ref[0, sl] = x1
        h = _rms(x1, gf_ref[...]) * (1.0 + scf_ref[0]) + shf_ref[0]
        _store_token_tiles(h2_ref.at[0, j * ts * ROW_CHUNKS:(j + 1) * ts * ROW_CHUNKS], h)
        logits = jnp.dot(h.astype(BF16), wr_ref[...], preferred_element_type=F32) + br_ref[...]
        g = jnp.where(lane < N_EXPERTS, logits, -jnp.inf)
        vals, idxs = [], []
        for _ in range(TOP_K):
            mx = jnp.max(g, axis=1, keepdims=True)
            first = jnp.min(jnp.where(g == mx, lane_f, float(LANES)), axis=1, keepdims=True)
            vals.append(mx)
            idxs.append(first)
            g = jnp.where(lane_f == first, -jnp.inf, g)
        es = [jnp.exp(v - vals[0]) for v in vals]
        den = es[0] + es[1] + es[2] + es[3]
        ti = jnp.zeros((ts, LANES), F32)
        tw = jnp.zeros((ts, LANES), F32)
        for k in range(TOP_K):
            ti = jnp.where(lane == k, idxs[k], ti)
            tw = jnp.where(lane == k, es[k] / den, tw)
        ti_ref[0, sl] = ti.astype(I32)
        tw_ref[0, sl] = tw


def _merge(x, proj, y_a, y_b, w_br_a, w_br_b, w_out, mod, g_ffn, w_router, b_router):
    b, s, d = x.shape
    tm = min(MERGE_TM, s)
    full = lambda shape: pl.BlockSpec(shape, lambda bi, i: (0,) * len(shape))
    tile = lambda w, c=0: pl.BlockSpec((1, tm, w), lambda bi, i: (bi, i, c))
    modspec = lambda j: pl.BlockSpec((1, 1, d), lambda bi, i: (bi * 6 + j, 0, 0))
    return pl.pallas_call(
        _merge_kernel,
        out_shape=(jax.ShapeDtypeStruct((b, s, d), F32),
                   jax.ShapeDtypeStruct((b, s * ROW_CHUNKS, LANES), F32),
                   jax.ShapeDtypeStruct((b, s, LANES), I32),
                   jax.ShapeDtypeStruct((b, s, LANES), F32)),
        grid=(b, s // tm),
        in_specs=[tile(d), tile(512), tile(512), tile(d, 3), tile(d, 4),
                  full(w_br_a.shape), full(w_br_b.shape), full(w_out.shape),
                  modspec(2), modspec(3), modspec(4), full(g_ffn.shape),
                  full(w_router.shape), full(b_router.shape)],
        out_specs=(tile(d), pl.BlockSpec((1, tm * ROW_CHUNKS, LANES), lambda bi, i: (bi, i, 0)),
                   tile(LANES), tile(LANES)),
        compiler_params=pltpu.CompilerParams(dimension_semantics=("arbitrary", "arbitrary"),
                                             vmem_limit_bytes=VMEM_LIMIT_BYTES),
        name="merge",
    )(x, y_a, y_b, proj, proj, w_br_a, w_br_b, w_out, mod, mod, mod, g_ffn,
      w_router, b_router)


def _dispatch_kernel(padtile_ref, nu_ref, pos_ref, h2_ref, xs_ref, zeros_ref, zsem, sem):
    step = pl.program_id(0)
    rc = ROW_CHUNKS
    n_tok = h2_ref.shape[0] // rc
    pad_rows = zeros_ref.shape[0]
    n_tiles = xs_ref.shape[0] // pad_rows

    def zero_tile(tile):
        start = pl.multiple_of(tile * pad_rows, pad_rows)
        return pltpu.make_async_copy(zeros_ref, xs_ref.at[pl.ds(start, pad_rows)], zsem)

    @pl.when(step == 0)
    def _():
        zeros_ref[...] = jnp.zeros(zeros_ref.shape, F32)
        for e in range(N_EXPERTS):
            @pl.when(padtile_ref[e] >= 0)
            def _():
                zero_tile(padtile_ref[e]).start()
        for e in range(N_EXPERTS):
            @pl.when(padtile_ref[e] >= 0)
            def _():
                zero_tile(padtile_ref[e]).wait()

        def start_tail(tile, carry):
            zero_tile(tile).start()
            return carry

        def wait_tail(tile, carry):
            zero_tile(tile).wait()
            return carry

        lax.fori_loop(nu_ref[0], n_tiles, start_tail, 0)
        lax.fori_loop(nu_ref[0], n_tiles, wait_tail, 0)

    def issue(j, carry):
        src = h2_ref.at[pl.ds(pl.multiple_of(j * rc, rc), rc)]
        for k in range(TOP_K):
            dst = xs_ref.at[pl.ds(pl.multiple_of(pos_ref[j * TOP_K + k], rc), rc)]
            pltpu.make_async_copy(src, dst, sem).start(priority=k % 2)
        return carry

    lax.fori_loop(0, n_tok, issue, 0, unroll=ROW_DMA_UNROLL)
    for k in range(TOP_K):
        pltpu.make_async_copy(h2_ref, xs_ref.at[pl.ds(0, n_tok * rc)], sem).wait()


def _dispatch(h2_tiles, pos_start, padtile, n_used, n_rows):
    rc = ROW_CHUNKS
    t_tok = h2_tiles.shape[0] // rc
    tm = min(DISPATCH_TM, t_tok)
    return pl.pallas_call(
        _dispatch_kernel,
        out_shape=jax.ShapeDtypeStruct((n_rows * rc, LANES), F32),
        grid_spec=pltpu.PrefetchScalarGridSpec(
            num_scalar_prefetch=2,
            grid=(t_tok // tm,),
            in_specs=[pl.BlockSpec((tm * TOP_K,), lambda s, pt, nu: (s,),
                                   memory_space=pltpu.SMEM),
                      pl.BlockSpec((tm * rc, LANES), lambda s, pt, nu: (s, 0))],
            out_specs=pl.BlockSpec(memory_space=pl.ANY),
            scratch_shapes=[pltpu.VMEM((MOE_TM * rc, LANES), F32),
                            pltpu.SemaphoreType.DMA, pltpu.SemaphoreType.DMA]),
        compiler_params=pltpu.CompilerParams(dimension_semantics=("arbitrary",),
                                             vmem_limit_bytes=VMEM_LIMIT_BYTES),
        name="dispatch",
    )(padtile, n_used, pos_start, h2_tiles)


def _moe_kernel(te_ref, nu_ref, xs_ref, wgu_ref, bgu_ref, wd_ref, bd_ref,
                ys_ref, wgu_bf, wd_bf, gu_ref):
    r = pl.program_id(0)
    busy = r < nu_ref[0]
    fresh = jnp.logical_or(r == 0, te_ref[r] != te_ref[jnp.maximum(r - 1, 0)])

    @pl.when(jnp.logical_and(busy, fresh))
    def _():
        wgu_bf[...] = wgu_ref[0].astype(BF16)
        wd_bf[...] = wd_ref[0].astype(BF16)

    @pl.when(jnp.logical_not(busy))
    def _():
        ys_ref[...] = jnp.zeros(ys_ref.shape, F32)

    @pl.when(busy)
    def _():
        x = _load_token_tiles(xs_ref).astype(BF16)
        fc = MOE_FF_CHUNK
        n_ff = D_FF // fc

        def gate_up(c):
            for lo in (c * fc, D_FF + c * fc):
                gu_ref[:, lo:lo + fc] = (jnp.dot(x, wgu_bf[:, lo:lo + fc],
                                                 preferred_element_type=F32)
                                         + bgu_ref[0][:, lo:lo + fc])

        def act_down(c):
            lo, hi = c * fc, (c + 1) * fc
            gate = jnp.minimum(gu_ref[:, lo:hi], SWIGLU_LIMIT)
            up = jnp.clip(gu_ref[:, D_FF + lo:D_FF + hi], -SWIGLU_LIMIT, SWIGLU_LIMIT)
            act = (up + 1.0) * gate * jax.nn.sigmoid(SWIGLU_ALPHA * gate)
            return jnp.dot(act.astype(BF16), wd_bf[lo:hi, :], preferred_element_type=F32)

        gate_up(0)
        y = None
        for c in range(n_ff):
            if c + 1 < n_ff:
                gate_up(c + 1)
            part = act_down(c)
            y = part if y is None else y + part
        _store_token_tiles(ys_ref, y + bd_ref[0])


def _moe(xs_tiles, tile_expert, n_used, w_gate_up, b_gate_up, w_down, b_down):
    d = D_MODEL
    rc = ROW_CHUNKS
    tm = MOE_TM
    n_tiles = tile_expert.shape[0]
    e, _, f2 = w_gate_up.shape
    wsel = lambda r, te, nu: (te[r], 0, 0)
    rows = lambda r, te, nu: (jnp.minimum(r, nu[0] - 1), 0)
    return pl.pallas_call(
        _moe_kernel,
        out_shape=jax.ShapeDtypeStruct((n_tiles * tm * rc, LANES), F32),
        grid_spec=pltpu.PrefetchScalarGridSpec(
            num_scalar_prefetch=2,
            grid=(n_tiles,),
            in_specs=[pl.BlockSpec((tm * rc, LANES), rows),
                      pl.BlockSpec((1, d, f2), wsel),
                      pl.BlockSpec((1, 1, f2), wsel),
                      pl.BlockSpec((1, D_FF, d), wsel),
                      pl.BlockSpec((1, 1, d), wsel)],
            out_specs=pl.BlockSpec((tm * rc, LANES), lambda r, te, nu: (r, 0)),
            scratch_shapes=[pltpu.VMEM((d, f2), BF16), pltpu.VMEM((D_FF, d), BF16),
                            pltpu.VMEM((tm, f2), F32)]),
        compiler_params=pltpu.CompilerParams(dimension_semantics=("arbitrary",),
                                             vmem_limit_bytes=VMEM_LIMIT_BYTES),
        name="moe",
    )(tile_expert, n_used, xs_tiles, w_gate_up, b_gate_up, w_down, b_down)


def _combine_kernel(pos_ref, ys_ref, x1_ref, tw_ref, gt_ref, gf_ref, o_ref, gbuf, sem):
    tm = x1_ref.shape[0]
    rc = ROW_CHUNKS

    def issue(j, carry):
        for k in range(TOP_K):
            src = ys_ref.at[pl.ds(pl.multiple_of(pos_ref[j * TOP_K + k], rc), rc)]
            dst = gbuf.at[k, pl.ds(pl.multiple_of(j * rc, rc), rc)]
            pltpu.make_async_copy(src, dst, sem).start(priority=k % 2)
        return carry

    lax.fori_loop(0, tm, issue, 0, unroll=ROW_DMA_UNROLL)
    for k in range(TOP_K):
        pltpu.make_async_copy(ys_ref.at[pl.ds(0, tm * rc)], gbuf.at[k], sem).wait()

    tw = tw_ref[...]
    acc = tw[:, 0:1] * _load_token_tiles(gbuf.at[0])
    for k in range(1, TOP_K):
        acc = acc + tw[:, k:k + 1] * _load_token_tiles(gbuf.at[k])
    x2 = x1_ref[...] + gt_ref[0] * acc
    o_ref[...] = _rms(x2, gf_ref[...])


def _combine(ys, pos_flat, x1, tw, mod, g_final, seq):
    t, d = x1.shape
    tm = min(COMBINE_TM, t)
    return pl.pallas_call(
        _combine_kernel,
        out_shape=jax.ShapeDtypeStruct((t, d), F32),
        grid=(t // tm,),
        in_specs=[pl.BlockSpec((tm * TOP_K,), lambda s: (s,), memory_space=pltpu.SMEM),
                  pl.BlockSpec(memory_space=pl.ANY),
                  pl.BlockSpec((tm, d), lambda s: (s, 0)),
                  pl.BlockSpec((tm, LANES), lambda s: (s, 0)),
                  pl.BlockSpec((1, 1, d), lambda s: ((s * tm) // seq * 6 + 5, 0, 0)),
                  pl.BlockSpec((1, d), lambda s: (0, 0))],
        out_specs=pl.BlockSpec((tm, d), lambda s: (s, 0)),
        scratch_shapes=[pltpu.VMEM((TOP_K, tm * ROW_CHUNKS, LANES), F32),
                        pltpu.SemaphoreType.DMA],
        compiler_params=pltpu.CompilerParams(dimension_semantics=("arbitrary",),
                                             vmem_limit_bytes=VMEM_LIMIT_BYTES),
        name="combine",
    )(pos_flat, ys, x1, tw, mod, g_final)


def _routing_tables(top_idx, n_tiles):
    tm = MOE_TM
    e_flat = top_idx.reshape(-1)
    onehot = (e_flat[:, None] == jnp.arange(N_EXPERTS, dtype=I32)[None, :]).astype(I32)
    csum = jnp.cumsum(onehot, axis=0)
    counts = csum[-1]
    rank = jnp.sum(csum * onehot, axis=1) - 1
    tiles_per = (counts + tm - 1) // tm
    tile_end = jnp.cumsum(tiles_per)
    tile_start = tile_end - tiles_per
    pos = (jnp.sum(onehot * tile_start[None, :], axis=1) * tm + rank).astype(I32)
    n_used = tile_end[-1:].astype(I32)
    tile_ids = jnp.arange(n_tiles, dtype=I32)
    tile_expert = jnp.minimum(
        jnp.sum((tile_ids[:, None] >= tile_end[None, :]).astype(I32), axis=1),
        N_EXPERTS - 1).astype(I32)
    padtile = jnp.where(counts % tm != 0, tile_end - 1, -1).astype(I32)
    return pos, tile_expert, n_used, padtile


def kernel(x, c, rel_bias, w_ada, b_ada, g_mix, w_in, lambda_q1, lambda_k1, lambda_q2,
           lambda_k2, subln_g, w_br_a, w_br_b, w_out, g_ffn, w_router, b_router,
           w_gate_up, b_gate_up, w_down, b_down, g_final):
    b, s, d = x.shape
    assert d == D_MODEL and s % ATT_T == 0 and MOBA_BLOCK == ATT_T
    t_tok = b * s

    c8 = jnp.zeros((8, d), F32).at[:b].set(c)
    ada = _ada(c8, w_ada[0], b_ada[0].reshape(1, -1))
    mod = ada[:b].reshape(b * 6, 1, d)

    dtiles = _bias_tiles(rel_bias)
    tab = rel_bias.T.reshape(-1)

    proj = _proj(x, g_mix[0].reshape(1, d), mod, w_in[0].astype(BF16))

    lamv = jnp.concatenate([lambda_q1, lambda_k1, lambda_q2, lambda_k2], axis=0)
    y_a = _attention("diff", tab, proj, dtiles, (lamv, subln_g[0].reshape(-1, 1)), (0, 512, 1024))
    y_b = _attention("moba", tab, proj, dtiles, None, (1536, 2048, 2560))

    wr = jnp.zeros((d, LANES), BF16).at[:, :N_EXPERTS].set(w_router[0].astype(BF16))
    br = jnp.zeros((1, LANES), F32).at[0, :N_EXPERTS].set(b_router[0])
    x1, h2, ti, tw = _merge(x, proj, y_a, y_b, w_br_a[0].astype(BF16), w_br_b[0].astype(BF16),
                            w_out[0].astype(BF16), mod, g_ffn[0].reshape(1, d), wr, br)

    n_tiles = (t_tok * TOP_K) // MOE_TM + N_EXPERTS
    top_idx = ti.reshape(t_tok, LANES)[:, :TOP_K]
    pos, tile_expert, n_used, padtile = _routing_tables(top_idx, n_tiles)
    pos_start = pos * ROW_CHUNKS

    xs = _dispatch(h2.reshape(t_tok * ROW_CHUNKS, LANES), pos_start, padtile, n_used,
                   n_tiles * MOE_TM)
    ys = _moe(xs, tile_expert, n_used, w_gate_up[0], b_gate_up[0].reshape(N_EXPERTS, 1, -1),
              w_down[0], b_down[0].reshape(N_EXPERTS, 1, -1))
    out = _combine(ys, pos_start, x1.reshape(t_tok, d), tw.reshape(t_tok, LANES), mod,
                   g_final.reshape(1, d), s)
    return out.reshape(b, s, d)
```

```python
import functools
import math

import jax
import jax.numpy as jnp
import numpy as np
from jax import lax
from jax.experimental import pallas as pl
from jax.experimental.pallas import tpu as pltpu

F32 = jnp.float32
BF16 = jnp.bfloat16
I32 = jnp.int32

D_MODEL = 1024
N_DIFF_HEADS = 4
DIFF_HEAD_DIM = 64
N_MOBA_HEADS = 8
MOBA_HEAD_DIM = 64
MOBA_BLOCK = 256
MOBA_TOPK = 3
N_HEADS_TOTAL = N_DIFF_HEADS + N_MOBA_HEADS
N_BUCKETS = 32
MAX_DISTANCE = 128
N_EXPERTS = 32
TOP_K = 4
D_FF = D_MODEL
SWIGLU_LIMIT = 7.0
SWIGLU_ALPHA = 1.702
RMS_EPS = 1e-5
LAM_INIT = 0.8 - 0.6 * math.exp(-0.3 * 0)
IN_WIDTH = 5120

LANES = 128
VMEM_LIMIT_BYTES = 56 * 1024 * 1024

ATT_T = 256
N_PAIRS = 4
V_ROWS = LANES + 16
PROJ_TM = 512
MERGE_TM = 512
MERGE_SUB = 256
MOE_TM = 512
MOE_FF_CHUNK = 256
DISPATCH_TM = 1024
ROW_DMA_UNROLL = 8
COMBINE_TM = 1024
NEG = -1e30
LOG2E = math.log2(math.e)


def _t5_bucket_np(dist):
    n = np.maximum(dist, 0)
    max_exact = N_BUCKETS // 2
    nf = np.maximum(n, max_exact).astype(np.float32)
    large = max_exact + (np.log(nf / max_exact) / math.log(MAX_DISTANCE / max_exact)
                         * (N_BUCKETS - max_exact)).astype(np.int32)
    large = np.minimum(large, N_BUCKETS - 1)
    return np.where(n < max_exact, n, large).astype(np.int32)


def _rms(x, g):
    return x * lax.rsqrt(jnp.mean(x * x, axis=-1, keepdims=True) + RMS_EPS) * g


ROW_CHUNKS = D_MODEL // LANES


def _store_token_tiles(ref, x):
    n = x.shape[0]
    for c in range(ROW_CHUNKS):
        ref[pl.ds(c, n, stride=ROW_CHUNKS), :] = x[:, c * LANES:(c + 1) * LANES]


def _load_token_tiles(ref):
    n = ref.shape[0] // ROW_CHUNKS
    return jnp.concatenate([ref[pl.ds(c, n, stride=ROW_CHUNKS), :] for c in range(ROW_CHUNKS)],
                           axis=1)


def _ada_kernel(c_ref, w_ref, b_ref, o_ref):
    o_ref[...] = jnp.dot(c_ref[...].astype(BF16), w_ref[...].astype(BF16),
                         preferred_element_type=F32) + b_ref[...]


def _ada(c8, w, b):
    d, n = w.shape
    tn = 1536
    return pl.pallas_call(
        _ada_kernel,
        out_shape=jax.ShapeDtypeStruct((8, n), F32),
        grid=(n // tn,),
        in_specs=[pl.BlockSpec((8, d), lambda j: (0, 0)),
                  pl.BlockSpec((d, tn), lambda j: (0, j)),
                  pl.BlockSpec((1, tn), lambda j: (0, j))],
        out_specs=pl.BlockSpec((8, tn), lambda j: (0, j)),
        compiler_params=pltpu.CompilerParams(dimension_semantics=("arbitrary",),
                                             vmem_limit_bytes=VMEM_LIMIT_BYTES),
        name="ada",
    )(c8, w, b)


def _bias_kernel(tab_ref, bd_ref, bp_ref, o_ref):
    g = pl.program_id(0)
    p = pl.program_id(1)
    t = pl.program_id(2)
    head = jnp.where(g == 0, p, N_DIFF_HEADS + 2 * p + t)
    bd = bd_ref[...]
    bp = bp_ref[...]
    d0 = jnp.where(bd < 0, NEG, 0.0).astype(F32)
    d1 = jnp.zeros(bp.shape, F32)
    for b in range(N_BUCKETS):
        val = tab_ref[head * N_BUCKETS + b] * LOG2E
        d0 = jnp.where(bd == b, val, d0)
        d1 = jnp.where(bp == b, val, d1)
    o_ref[0, 0, 0] = d0
    o_ref[0, 0, 1] = d1


def _bias_tiles(rel_bias):
    t = ATT_T
    jj = np.arange(t)[:, None]
    ii = np.arange(t)[None, :]
    bd = np.where(ii >= jj, _t5_bucket_np(ii - jj), -1).astype(np.int32)
    bp = _t5_bucket_np(t + ii - jj)
    tab = rel_bias.T.reshape(-1)
    return pl.pallas_call(
        _bias_kernel,
        out_shape=jax.ShapeDtypeStruct((2, N_PAIRS, 2, t, 2 * t), F32),
        grid=(2, N_PAIRS, 2),
        in_specs=[pl.BlockSpec(memory_space=pltpu.SMEM),
                  pl.BlockSpec((t, t), lambda g, p, h: (0, 0)),
                  pl.BlockSpec((t, t), lambda g, p, h: (0, 0))],
        out_specs=pl.BlockSpec((1, 1, 2, t, t), lambda g, p, h: (g, p, 0, 0, h)),
        compiler_params=pltpu.CompilerParams(
            dimension_semantics=("arbitrary", "arbitrary", "arbitrary")),
        name="bias_tiles",
    )(tab, jnp.asarray(bd), jnp.asarray(bp))


def _proj_kernel(x_ref, g_ref, sh_ref, sc_ref, w_ref, o_ref):
    x = x_ref[0]
    h = _rms(x, g_ref[...]) * (1.0 + sc_ref[0]) + sh_ref[0]
    hb = h.astype(BF16)
    n_total = w_ref.shape[1]
    for n0 in range(0, n_total, 512):
        o_ref[0, :, n0:n0 + 512] = jnp.dot(
            hb, w_ref[:, n0:n0 + 512], preferred_element_type=F32).astype(BF16)


def _proj(x, g_mix, mod, w_in_bf):
    b, s, d = x.shape
    n = w_in_bf.shape[1]
    tm = min(PROJ_TM, s)
    return pl.pallas_call(
        _proj_kernel,
        out_shape=jax.ShapeDtypeStruct((b, s, n), BF16),
        grid=(b, s // tm),
        in_specs=[pl.BlockSpec((1, tm, d), lambda bi, i: (bi, i, 0)),
                  pl.BlockSpec((1, d), lambda bi, i: (0, 0)),
                  pl.BlockSpec((1, 1, d), lambda bi, i: (bi * 6 + 0, 0, 0)),
                  pl.BlockSpec((1, 1, d), lambda bi, i: (bi * 6 + 1, 0, 0)),
                  pl.BlockSpec((d, n), lambda bi, i: (0, 0))],
        out_specs=pl.BlockSpec((1, tm, n), lambda bi, i: (bi, i, 0)),
        compiler_params=pltpu.CompilerParams(dimension_semantics=("arbitrary", "arbitrary"),
                                             vmem_limit_bytes=VMEM_LIMIT_BYTES),
        name="proj",
    )(x, g_mix, mod, mod, w_in_bf)


def _stack_q(q_ref, q2_ref, scale):
    t = ATT_T
    row = lax.broadcasted_iota(I32, (LANES, t), 0)
    for p in range(N_PAIRS):
        q = (q_ref[0, :, p * LANES:(p + 1) * LANES].astype(F32) * scale).T
        q2_ref[p, :, 0:t] = jnp.where(row < 64, q, 0.0).astype(BF16)
        q2_ref[p, :, t:2 * t] = jnp.where(row >= 64, q, 0.0).astype(BF16)


def _transpose_values(v_ref, vt_ref):
    t = ATT_T
    nq = vt_ref.shape[0]

    def body(n, carry):
        for p in range(N_PAIRS):
            blk = v_ref[0, pl.ds(pl.multiple_of(n * t, t), t), p * LANES:(p + 1) * LANES]
            vt_ref[n, p * V_ROWS:p * V_ROWS + LANES, :] = blk.astype(F32).T.astype(BF16)
            vt_ref[n, p * V_ROWS + LANES:(p + 1) * V_ROWS, :] = jnp.ones((V_ROWS - LANES, t), BF16)
        return carry

    lax.fori_loop(0, nq, body, 0)


def _flash_init(m_ref, acc_ref):
    m_ref[...] = jnp.full(m_ref.shape, NEG, F32)
    acc_ref[...] = jnp.zeros(acc_ref.shape, F32)


def _score_slot(p, h, bank):
    return bank * (2 * N_PAIRS) + 2 * p + h


def _flash_scores(p, h, n, bank, q2_ref, k_ref, s_ref):
    t = ATT_T
    kt = k_ref[0, pl.ds(pl.multiple_of(n * t, t), t), p * LANES:(p + 1) * LANES]
    s_ref[_score_slot(p, h, bank)] = jnp.dot(kt, q2_ref[p, :, h * t:(h + 1) * t],
                                             preferred_element_type=F32)


def _flash_absorb(p, h, n, bank, tile_bias, row_bias, live, v_ref, s_ref, m_ref, acc_ref):
    t = ATT_T
    cs = slice(h * t, (h + 1) * t)
    vt = v_ref[n, p * V_ROWS:(p + 1) * V_ROWS, :]
    s = s_ref[_score_slot(p, h, bank)]
    if tile_bias is not None:
        s = s + tile_bias
    mx = jnp.max(s, axis=0, keepdims=True)
    if row_bias is not None:
        mx = mx + row_bias
    if live is not None:
        mx = jnp.where(live, mx, NEG)
    m_prev = m_ref[p, :, cs]
    m_new = jnp.maximum(m_prev, mx)
    alpha = jnp.exp2(m_prev - m_new)
    shift = m_new if row_bias is None else m_new - row_bias
    if live is not None:
        shift = jnp.where(live, shift, -NEG)
    pt = jnp.exp2(s - shift)
    acc_ref[p, :, cs] = alpha * acc_ref[p, :, cs] + jnp.dot(vt, pt.astype(BF16),
                                                            preferred_element_type=F32)
    m_ref[p, :, cs] = m_new


def _far_bias(tab_ref, head):
    return tab_ref[head * N_BUCKETS + (N_BUCKETS - 1)] * LOG2E


def _flash_sweep(i, far_args, prev_args, own_args, refs):
    q2_ref, k_ref, v_ref, s_ref, m_ref, acc_ref = refs
    chains = [(p, h) for p in range(N_PAIRS) for h in range(2)]

    def scores_tile(n, bank):
        for p, h in chains:
            _flash_scores(p, h, n, bank, q2_ref, k_ref, s_ref)

    def step(n, bank, args, with_next):
        for p, h in chains:
            if with_next:
                _flash_scores(p, h, n + 1, 1 - bank, q2_ref, k_ref, s_ref)
            _flash_absorb(p, h, n, bank, *args(p, h), v_ref, s_ref, m_ref, acc_ref)

    def far_step(n, bank):
        step(n, bank, lambda p, h: far_args(p, h, n), True)

    n_far = jnp.maximum(i - 1, 0)
    lead = jnp.bitwise_and(n_far, 1)

    @pl.when(jnp.bitwise_and(i, 1) == 0)
    def _():
        scores_tile(0, 0)

    @pl.when(jnp.bitwise_and(i, 1) == 1)
    def _():
        scores_tile(0, 1)

    @pl.when(lead == 1)
    def _():
        far_step(0, 0)

    def pair_body(j, carry):
        n = lead + 2 * j
        far_step(n, 1)
        far_step(n + 1, 0)
        return carry

    lax.fori_loop(0, jnp.right_shift(n_far, 1), pair_body, 0)

    @pl.when(i >= 1)
    def _():
        step(i - 1, 1, prev_args, True)

    step(i, 0, own_args, False)


def _diff_kernel(tab_ref, q_ref, k_ref, v_ref, d_ref, lamv_ref, g_ref, o_ref,
                 q2_ref, s_ref, m_ref, acc_ref, vt_ref):
    t = ATT_T
    i = pl.program_id(1)

    @pl.when(i == 0)
    def _():
        _transpose_values(v_ref, vt_ref)

    _stack_q(q_ref, q2_ref, DIFF_HEAD_DIM ** -0.5 * LOG2E)
    _flash_init(m_ref, acc_ref)
    refs = (q2_ref, k_ref, vt_ref, s_ref, m_ref, acc_ref)
    _flash_sweep(i,
                 lambda p, h, n: (None, _far_bias(tab_ref, p), None),
                 lambda p, h: (d_ref[0, p, 1, :, h * t:(h + 1) * t], None, None),
                 lambda p, h: (d_ref[0, p, 0, :, h * t:(h + 1) * t], None, None),
                 refs)

    lv = lamv_ref[...]
    lam = (jnp.exp(jnp.sum(lv[0:1] * lv[1:2], axis=1, keepdims=True))
           - jnp.exp(jnp.sum(lv[2:3] * lv[3:4], axis=1, keepdims=True)) + LAM_INIT)
    for p in range(N_PAIRS):
        acc = acc_ref[p, 0:LANES, :]
        l = acc_ref[p, LANES:LANES + 1, :]
        o = acc[:, 0:t] / l[:, 0:t] - lam * (acc[:, t:2 * t] / l[:, t:2 * t])
        y = o * lax.rsqrt(jnp.mean(o * o, axis=0, keepdims=True) + RMS_EPS) * g_ref[...]
        o_ref[0, :, p * LANES:(p + 1) * LANES] = (y * (1.0 - LAM_INIT)).T.astype(BF16)


def _moba_kernel(tab_ref, q_ref, k_ref, v_ref, d_ref, o_ref,
                 q2_ref, s_ref, m_ref, acc_ref, vt_ref, kmean_ref, sel_ref):
    t = ATT_T
    i = pl.program_id(1)
    nb = k_ref.shape[1] // t
    nbp = kmean_ref.shape[1]

    @pl.when(i == 0)
    def _():
        _transpose_values(v_ref, vt_ref)
        kmean_ref[...] = jnp.zeros(kmean_ref.shape, F32)
        for p in range(N_PAIRS):
            for n in range(nb):
                blk = k_ref[0, n * t:(n + 1) * t, p * LANES:(p + 1) * LANES].astype(F32)
                kmean_ref[p, n:n + 1, :] = jnp.sum(blk, axis=0, keepdims=True) * (1.0 / t)

    _stack_q(q_ref, q2_ref, MOBA_HEAD_DIM ** -0.5 * LOG2E)
    _flash_init(m_ref, acc_ref)

    blk_id = lax.broadcasted_iota(I32, (nbp, 2 * t), 0)
    blk_f = blk_id.astype(F32)
    for p in range(N_PAIRS):
        gs = jnp.dot(kmean_ref[p].astype(BF16), q2_ref[p], preferred_element_type=F32)
        g = jnp.where(blk_id < i, gs, -jnp.inf)
        sel = jnp.full((nbp, 2 * t), NEG, F32)
        for _ in range(MOBA_TOPK):
            mx = jnp.max(g, axis=0, keepdims=True)
            first = jnp.min(jnp.where(g == mx, blk_f, float(nbp)), axis=0, keepdims=True)
            hit = blk_f == first
            sel = jnp.where(hit, 0.0, sel)
            g = jnp.where(hit, -jnp.inf, g)
        sel_ref[p] = sel

    def live(p, h, n):
        return sel_ref[p, pl.ds(n, 1), h * t:(h + 1) * t] > -1.0

    refs = (q2_ref, k_ref, vt_ref, s_ref, m_ref, acc_ref)
    _flash_sweep(i,
                 lambda p, h, n: (None, _far_bias(tab_ref, N_DIFF_HEADS + 2 * p + h),
                                  live(p, h, n)),
                 lambda p, h: (d_ref[0, p, 1, :, h * t:(h + 1) * t], None, live(p, h, i - 1)),
                 lambda p, h: (d_ref[0, p, 0, :, h * t:(h + 1) * t], None, None),
                 refs)

    row = lax.broadcasted_iota(I32, (LANES, t), 0)
    for p in range(N_PAIRS):
        acc = acc_ref[p, 0:LANES, :]
        l = acc_ref[p, LANES:LANES + 1, :]
        o = jnp.where(row < 64, acc[:, 0:t] / l[:, 0:t], acc[:, t:2 * t] / l[:, t:2 * t])
        o_ref[0, :, p * LANES:(p + 1) * LANES] = o.T.astype(BF16)


def _attention(kind, tab, proj, dtiles, extra, cols):
    b, s, _ = proj.shape
    t = ATT_T
    nq = s // t
    w = N_PAIRS * LANES
    qc, kc, vc = cols
    gidx = 0 if kind == "diff" else 1
    in_specs = [pl.BlockSpec(memory_space=pltpu.SMEM),
                pl.BlockSpec((1, t, w), lambda bi, i: (bi, i, qc // w)),
                pl.BlockSpec((1, s, w), lambda bi, i: (bi, 0, kc // w)),
                pl.BlockSpec((1, s, w), lambda bi, i: (bi, 0, vc // w)),
                pl.BlockSpec((1, N_PAIRS, 2, t, 2 * t), lambda bi, i: (gidx, 0, 0, 0, 0))]
    scratch = [pltpu.VMEM((N_PAIRS, LANES, 2 * t), BF16),
               pltpu.VMEM((2 * 2 * N_PAIRS, t, t), F32),
               pltpu.VMEM((N_PAIRS, 1, 2 * t), F32),
               pltpu.VMEM((N_PAIRS, V_ROWS, 2 * t), F32),
               pltpu.VMEM((nq, N_PAIRS * V_ROWS, t), BF16)]
    if kind == "diff":
        lamv, subln_g = extra
        in_specs += [pl.BlockSpec(lamv.shape, lambda bi, i: (0, 0)),
                     pl.BlockSpec(subln_g.shape, lambda bi, i: (0, 0))]
        args = (tab, proj, proj, proj, dtiles, lamv, subln_g)
        body = _diff_kernel
    else:
        nbp = max(32, nq)
        scratch += [pltpu.VMEM((N_PAIRS, nbp, LANES), F32), pltpu.VMEM((N_PAIRS, nbp, 2 * t), F32)]
        args = (tab, proj, proj, proj, dtiles)
        body = _moba_kernel
    return pl.pallas_call(
        body,
        out_shape=jax.ShapeDtypeStruct((b, s, w), BF16),
        grid=(b, nq),
        in_specs=in_specs,
        out_specs=pl.BlockSpec((1, t, w), lambda bi, i: (bi, i, 0)),
        scratch_shapes=scratch,
        compiler_params=pltpu.CompilerParams(
            dimension_semantics=("arbitrary", "arbitrary"),
            vmem_limit_bytes=VMEM_LIMIT_BYTES),
        name=kind + "_attn",
    )(*args)


def _merge_kernel(x_ref, ya_ref, yb_ref, gla_ref, glb_ref, wa_ref, wb_ref, wo_ref,
                  gt_ref, shf_ref, scf_ref, gf_ref, wr_ref, br_ref,
                  x1_ref, h2_ref, ti_ref, tw_ref):
    ts = MERGE_SUB
    subs = [slice(j * ts, (j + 1) * ts) for j in range(x_ref.shape[1] // ts)]
    branch = [(jnp.dot(ya_ref[0, sl], wa_ref[...], preferred_element_type=F32),
               jnp.dot(yb_ref[0, sl], wb_ref[...], preferred_element_type=F32)) for sl in subs]
    mixed = []
    for sl, (a, b) in zip(subs, branch):
        merged = (jax.nn.sigmoid(gla_ref[0, sl].astype(F32)) * a
                  + jax.nn.sigmoid(glb_ref[0, sl].astype(F32)) * b)
        mixed.append(jnp.dot(merged.astype(BF16), wo_ref[...], preferred_element_type=F32))
    lane = lax.broadcasted_iota(I32, (ts, LANES), 1)
    lane_f = lane.astype(F32)
    for j, (sl, mo) in enumerate(zip(subs, mixed)):
        x1 = x_ref[0, sl] + gt_ref[0] * mo
        x1_ref[0, sl] = x1
        h = _rms(x1, gf_ref[...]) * (1.0 + scf_ref[0]) + shf_ref[0]
        _store_token_tiles(h2_ref.at[0, j * ts * ROW_CHUNKS:(j + 1) * ts * ROW_CHUNKS], h)
        logits = jnp.dot(h.astype(BF16), wr_ref[...], preferred_element_type=F32) + br_ref[...]
        g = jnp.where(lane < N_EXPERTS, logits, -jnp.inf)
        vals, idxs = [], []
        for _ in range(TOP_K):
            mx = jnp.max(g, axis=1, keepdims=True)
            first = jnp.min(jnp.where(g == mx, lane_f, float(LANES)), axis=1, keepdims=True)
            vals.append(mx)
            idxs.append(first)
            g = jnp.where(lane_f == first, -jnp.inf, g)
        es = [jnp.exp(v - vals[0]) for v in vals]
        den = es[0] + es[1] + es[2] + es[3]
        ti = jnp.zeros((ts, LANES), F32)
        tw = jnp.zeros((ts, LANES), F32)
        for k in range(TOP_K):
            ti = jnp.where(lane == k, idxs[k], ti)
            tw = jnp.where(lane == k, es[k] / den, tw)
        ti_ref[0, sl] = ti.astype(I32)
        tw_ref[0, sl] = tw


def _merge(x, proj, y_a, y_b, w_br_a, w_br_b, w_out, mod, g_ffn, w_router, b_router):
    b, s, d = x.shape
    tm = min(MERGE_TM, s)
    full = lambda shape: pl.BlockSpec(shape, lambda bi, i: (0,) * len(shape))
    tile = lambda w, c=0: pl.BlockSpec((1, tm, w), lambda bi, i: (bi, i, c))
    modspec = lambda j: pl.BlockSpec((1, 1, d), lambda bi, i: (bi * 6 + j, 0, 0))
    return pl.pallas_call(
        _merge_kernel,
        out_shape=(jax.ShapeDtypeStruct((b, s, d), F32),
                   jax.ShapeDtypeStruct((b, s * ROW_CHUNKS, LANES), F32),
                   jax.ShapeDtypeStruct((b, s, LANES), I32),
                   jax.ShapeDtypeStruct((b, s, LANES), F32)),
        grid=(b, s // tm),
        in_specs=[tile(d), tile(512), tile(512), tile(d, 3), tile(d, 4),
                  full(w_br_a.shape), full(w_br_b.shape), full(w_out.shape),
                  modspec(2), modspec(3), modspec(4), full(g_ffn.shape),
                  full(w_router.shape), full(b_router.shape)],
        out_specs=(tile(d), pl.BlockSpec((1, tm * ROW_CHUNKS, LANES), lambda bi, i: (bi, i, 0)),
                   tile(LANES), tile(LANES)),
        compiler_params=pltpu.CompilerParams(dimension_semantics=("arbitrary", "arbitrary"),
                                             vmem_limit_bytes=VMEM_LIMIT_BYTES),
        name="merge",
    )(x, y_a, y_b, proj, proj, w_br_a, w_br_b, w_out, mod, mod, mod, g_ffn,
      w_router, b_router)


def _dispatch_kernel(padtile_ref, nu_ref, pos_ref, h2_ref, xs_ref, zeros_ref, zsem, sem):
    step = pl.program_id(0)
    rc = ROW_CHUNKS
    n_tok = h2_ref.shape[0] // rc
    pad_rows = zeros_ref.shape[0]
    n_tiles = xs_ref.shape[0] // pad_rows

    def zero_tile(tile):
        start = pl.multiple_of(tile * pad_rows, pad_rows)
        return pltpu.make_async_copy(zeros_ref, xs_ref.at[pl.ds(start, pad_rows)], zsem)

    @pl.when(step == 0)
    def _():
        zeros_ref[...] = jnp.zeros(zeros_ref.shape, F32)
        for e in range(N_EXPERTS):
            @pl.when(padtile_ref[e] >= 0)
            def _():
                zero_tile(padtile_ref[e]).start()
        for e in range(N_EXPERTS):
            @pl.when(padtile_ref[e] >= 0)
            def _():
                zero_tile(padtile_ref[e]).wait()

        def start_tail(tile, carry):
            zero_tile(tile).start()
            return carry

        def wait_tail(tile, carry):
            zero_tile(tile).wait()
            return carry

        lax.fori_loop(nu_ref[0], n_tiles, start_tail, 0)
        lax.fori_loop(nu_ref[0], n_tiles, wait_tail, 0)

    def issue(j, carry):
        src = h2_ref.at[pl.ds(pl.multiple_of(j * rc, rc), rc)]
        for k in range(TOP_K):
            dst = xs_ref.at[pl.ds(pl.multiple_of(pos_ref[j * TOP_K + k], rc), rc)]
            pltpu.make_async_copy(src, dst, sem).start(priority=k % 2)
        return carry

    lax.fori_loop(0, n_tok, issue, 0, unroll=ROW_DMA_UNROLL)
    for k in range(TOP_K):
        pltpu.make_async_copy(h2_ref, xs_ref.at[pl.ds(0, n_tok * rc)], sem).wait()


def _dispatch(h2_tiles, pos_start, padtile, n_used, n_rows):
    rc = ROW_CHUNKS
    t_tok = h2_tiles.shape[0] // rc
    tm = min(DISPATCH_TM, t_tok)
    return pl.pallas_call(
        _dispatch_kernel,
        out_shape=jax.ShapeDtypeStruct((n_rows * rc, LANES), F32),
        grid_spec=pltpu.PrefetchScalarGridSpec(
            num_scalar_prefetch=2,
            grid=(t_tok // tm,),
            in_specs=[pl.BlockSpec((tm * TOP_K,), lambda s, pt, nu: (s,),
                                   memory_space=pltpu.SMEM),
                      pl.BlockSpec((tm * rc, LANES), lambda s, pt, nu: (s, 0))],
            out_specs=pl.BlockSpec(memory_space=pl.ANY),
            scratch_shapes=[pltpu.VMEM((MOE_TM * rc, LANES), F32),
                            pltpu.SemaphoreType.DMA, pltpu.SemaphoreType.DMA]),
        compiler_params=pltpu.CompilerParams(dimension_semantics=("arbitrary",),
                                             vmem_limit_bytes=VMEM_LIMIT_BYTES),
        name="dispatch",
    )(padtile, n_used, pos_start, h2_tiles)


def _moe_kernel(te_ref, nu_ref, xs_ref, wgu_ref, bgu_ref, wd_ref, bd_ref,
                ys_ref, wgu_bf, wd_bf, gu_ref):
    r = pl.program_id(0)
    busy = r < nu_ref[0]
    fresh = jnp.logical_or(r == 0, te_ref[r] != te_ref[jnp.maximum(r - 1, 0)])

    @pl.when(jnp.logical_and(busy, fresh))
    def _():
        wgu_bf[...] = wgu_ref[0].astype(BF16)
        wd_bf[...] = wd_ref[0].astype(BF16)

    @pl.when(jnp.logical_not(busy))
    def _():
        ys_ref[...] = jnp.zeros(ys_ref.shape, F32)

    @pl.when(busy)
    def _():
        x = _load_token_tiles(xs_ref).astype(BF16)
        fc = MOE_FF_CHUNK
        for c in range(2 * D_FF // fc):
            lo, hi = c * fc, (c + 1) * fc
            gu_ref[:, lo:hi] = (jnp.dot(x, wgu_bf[:, lo:hi], preferred_element_type=F32)
                                + bgu_ref[0][:, lo:hi])
        y = None
        for c in range(D_FF // fc):
            lo, hi = c * fc, (c + 1) * fc
            gate = jnp.minimum(gu_ref[:, lo:hi], SWIGLU_LIMIT)
            up = jnp.clip(gu_ref[:, D_FF + lo:D_FF + hi], -SWIGLU_LIMIT, SWIGLU_LIMIT)
            act = (up + 1.0) * gate * jax.nn.sigmoid(SWIGLU_ALPHA * gate)
            part = jnp.dot(act.astype(BF16), wd_bf[lo:hi, :], preferred_element_type=F32)
            y = part if y is None else y + part
        _store_token_tiles(ys_ref, y + bd_ref[0])


def _moe(xs_tiles, tile_expert, n_used, w_gate_up, b_gate_up, w_down, b_down):
    d = D_MODEL
    rc = ROW_CHUNKS
    tm = MOE_TM
    n_tiles = tile_expert.shape[0]
    e, _, f2 = w_gate_up.shape
    wsel = lambda r, te, nu: (te[r], 0, 0)
    rows = lambda r, te, nu: (jnp.minimum(r, nu[0] - 1), 0)
    return pl.pallas_call(
        _moe_kernel,
        out_shape=jax.ShapeDtypeStruct((n_tiles * tm * rc, LANES), F32),
        grid_spec=pltpu.PrefetchScalarGridSpec(
            num_scalar_prefetch=2,
            grid=(n_tiles,),
            in_specs=[pl.BlockSpec((tm * rc, LANES), rows),
                      pl.BlockSpec((1, d, f2), wsel),
                      pl.BlockSpec((1, 1, f2), wsel),
                      pl.BlockSpec((1, D_FF, d), wsel),
                      pl.BlockSpec((1, 1, d), wsel)],
            out_specs=pl.BlockSpec((tm * rc, LANES), lambda r, te, nu: (r, 0)),
            scratch_shapes=[pltpu.VMEM((d, f2), BF16), pltpu.VMEM((D_FF, d), BF16),
                            pltpu.VMEM((tm, f2), F32)]),
        compiler_params=pltpu.CompilerParams(dimension_semantics=("arbitrary",),
                                             vmem_limit_bytes=VMEM_LIMIT_BYTES),
        name="moe",
    )(tile_expert, n_used, xs_tiles, w_gate_up, b_gate_up, w_down, b_down)


def _combine_kernel(pos_ref, ys_ref, x1_ref, tw_ref, gt_ref, gf_ref, o_ref, gbuf, sem):
    tm = x1_ref.shape[0]
    rc = ROW_CHUNKS

    def issue(j, carry):
        for k in range(TOP_K):
            src = ys_ref.at[pl.ds(pl.multiple_of(pos_ref[j * TOP_K + k], rc), rc)]
            dst = gbuf.at[k, pl.ds(pl.multiple_of(j * rc, rc), rc)]
            pltpu.make_async_copy(src, dst, sem).start(priority=k % 2)
        return carry

    lax.fori_loop(0, tm, issue, 0, unroll=ROW_DMA_UNROLL)
    for k in range(TOP_K):
        pltpu.make_async_copy(ys_ref.at[pl.ds(0, tm * rc)], gbuf.at[k], sem).wait()

    tw = tw_ref[...]
    acc = tw[:, 0:1] * _load_token_tiles(gbuf.at[0])
    for k in range(1, TOP_K):
        acc = acc + tw[:, k:k + 1] * _load_token_tiles(gbuf.at[k])
    x2 = x1_ref[...] + gt_ref[0] * acc
    o_ref[...] = _rms(x2, gf_ref[...])


def _combine(ys, pos_flat, x1, tw, mod, g_final, seq):
    t, d = x1.shape
    tm = min(COMBINE_TM, t)
    return pl.pallas_call(
        _combine_kernel,
        out_shape=jax.ShapeDtypeStruct((t, d), F32),
        grid=(t // tm,),
        in_specs=[pl.BlockSpec((tm * TOP_K,), lambda s: (s,), memory_space=pltpu.SMEM),
                  pl.BlockSpec(memory_space=pl.ANY),
                  pl.BlockSpec((tm, d), lambda s: (s, 0)),
                  pl.BlockSpec((tm, LANES), lambda s: (s, 0)),
                  pl.BlockSpec((1, 1, d), lambda s: ((s * tm) // seq * 6 + 5, 0, 0)),
                  pl.BlockSpec((1, d), lambda s: (0, 0))],
        out_specs=pl.BlockSpec((tm, d), lambda s: (s, 0)),
        scratch_shapes=[pltpu.VMEM((TOP_K, tm * ROW_CHUNKS, LANES), F32),
                        pltpu.SemaphoreType.DMA],
        compiler_params=pltpu.CompilerParams(dimension_semantics=("arbitrary",),
                                             vmem_limit_bytes=VMEM_LIMIT_BYTES),
        name="combine",
    )(pos_flat, ys, x1, tw, mod, g_final)


def _routing_tables(top_idx, n_tiles):
    tm = MOE_TM
    e_flat = top_idx.reshape(-1)
    onehot = (e_flat[:, None] == jnp.arange(N_EXPERTS, dtype=I32)[None, :]).astype(I32)
    csum = jnp.cumsum(onehot, axis=0)
    counts = csum[-1]
    rank = jnp.sum(csum * onehot, axis=1) - 1
    tiles_per = (counts + tm - 1) // tm
    tile_end = jnp.cumsum(tiles_per)
    tile_start = tile_end - tiles_per
    pos = (jnp.sum(onehot * tile_start[None, :], axis=1) * tm + rank).astype(I32)
    n_used = tile_end[-1:].astype(I32)
    tile_ids = jnp.arange(n_tiles, dtype=I32)
    tile_expert = jnp.minimum(
        jnp.sum((tile_ids[:, None] >= tile_end[None, :]).astype(I32), axis=1),
        N_EXPERTS - 1).astype(I32)
    padtile = jnp.where(counts % tm != 0, tile_end - 1, -1).astype(I32)
    return pos, tile_expert, n_used, padtile


def kernel(x, c, rel_bias, w_ada, b_ada, g_mix, w_in, lambda_q1, lambda_k1, lambda_q2,
           lambda_k2, subln_g, w_br_a, w_br_b, w_out, g_ffn, w_router, b_router,
           w_gate_up, b_gate_up, w_down, b_down, g_final):
    b, s, d = x.shape
    assert d == D_MODEL and s % ATT_T == 0 and MOBA_BLOCK == ATT_T
    t_tok = b * s

    c8 = jnp.zeros((8, d), F32).at[:b].set(c)
    ada = _ada(c8, w_ada[0], b_ada[0].reshape(1, -1))
    mod = ada[:b].reshape(b * 6, 1, d)

    dtiles = _bias_tiles(rel_bias)
    tab = rel_bias.T.reshape(-1)

    proj = _proj(x, g_mix[0].reshape(1, d), mod, w_in[0].astype(BF16))

    lamv = jnp.concatenate([lambda_q1, lambda_k1, lambda_q2, lambda_k2], axis=0)
    y_a = _attention("diff", tab, proj, dtiles, (lamv, subln_g[0].reshape(-1, 1)), (0, 512, 1024))
    y_b = _attention("moba", tab, proj, dtiles, None, (1536, 2048, 2560))

    wr = jnp.zeros((d, LANES), BF16).at[:, :N_EXPERTS].set(w_router[0].astype(BF16))
    br = jnp.zeros((1, LANES), F32).at[0, :N_EXPERTS].set(b_router[0])
    x1, h2, ti, tw = _merge(x, proj, y_a, y_b, w_br_a[0].astype(BF16), w_br_b[0].astype(BF16),
                            w_out[0].astype(BF16), mod, g_ffn[0].reshape(1, d), wr, br)

    n_tiles = (t_tok * TOP_K) // MOE_TM + N_EXPERTS
    top_idx = ti.reshape(t_tok, LANES)[:, :TOP_K]
    pos, tile_expert, n_used, padtile = _routing_tables(top_idx, n_tiles)
    pos_start = pos * ROW_CHUNKS

    xs = _dispatch(h2.reshape(t_tok * ROW_CHUNKS, LANES), pos_start, padtile, n_used,
                   n_tiles * MOE_TM)
    ys = _moe(xs, tile_expert, n_used, w_gate_up[0], b_gate_up[0].reshape(N_EXPERTS, 1, -1),
              w_down[0], b_down[0].reshape(N_EXPERTS, 1, -1))
    out = _combine(ys, pos_start, x1.reshape(t_tok, d), tw.reshape(t_tok, LANES), mod,
                   g_final.reshape(1, d), s)
    return out.reshape(b, s, d)
```

```python
import functools
import math

import jax
import jax.numpy as jnp
import numpy as np
from jax import lax
from jax.experimental import pallas as pl
from jax.experimental.pallas import tpu as pltpu

F32 = jnp.float32
BF16 = jnp.bfloat16
I32 = jnp.int32

D_MODEL = 1024
N_DIFF_HEADS = 4
DIFF_HEAD_DIM = 64
N_MOBA_HEADS = 8
MOBA_HEAD_DIM = 64
MOBA_BLOCK = 256
MOBA_TOPK = 3
N_HEADS_TOTAL = N_DIFF_HEADS + N_MOBA_HEADS
N_BUCKETS = 32
MAX_DISTANCE = 128
N_EXPERTS = 32
TOP_K = 4
D_FF = D_MODEL
SWIGLU_LIMIT = 7.0
SWIGLU_ALPHA = 1.702
RMS_EPS = 1e-5
LAM_INIT = 0.8 - 0.6 * math.exp(-0.3 * 0)
IN_WIDTH = 5120

LANES = 128
VMEM_LIMIT_BYTES = 56 * 1024 * 1024

ATT_T = 256
N_PAIRS = 4
V_ROWS = LANES + 16
PROJ_TM = 512
MERGE_TM = 512
MERGE_SUB = 256
MOE_TM = 512
MOE_FF_CHUNK = 256
DISPATCH_TM = 1024
ROW_DMA_UNROLL = 8
COMBINE_TM = 1024
NEG = -1e30
LOG2E = math.log2(math.e)


def _t5_bucket_np(dist):
    n = np.maximum(dist, 0)
    max_exact = N_BUCKETS // 2
    nf = np.maximum(n, max_exact).astype(np.float32)
    large = max_exact + (np.log(nf / max_exact) / math.log(MAX_DISTANCE / max_exact)
                         * (N_BUCKETS - max_exact)).astype(np.int32)
    large = np.minimum(large, N_BUCKETS - 1)
    return np.where(n < max_exact, n, large).astype(np.int32)


def _rms(x, g):
    return x * lax.rsqrt(jnp.mean(x * x, axis=-1, keepdims=True) + RMS_EPS) * g


ROW_CHUNKS = D_MODEL // LANES


def _store_token_tiles(ref, x):
    n = x.shape[0]
    for c in range(ROW_CHUNKS):
        ref[pl.ds(c, n, stride=ROW_CHUNKS), :] = x[:, c * LANES:(c + 1) * LANES]


def _load_token_tiles(ref):
    n = ref.shape[0] // ROW_CHUNKS
    return jnp.concatenate([ref[pl.ds(c, n, stride=ROW_CHUNKS), :] for c in range(ROW_CHUNKS)],
                           axis=1)


def _ada_kernel(c_ref, w_ref, b_ref, o_ref):
    o_ref[...] = jnp.dot(c_ref[...].astype(BF16), w_ref[...].astype(BF16),
                         preferred_element_type=F32) + b_ref[...]


def _ada(c8, w, b):
    d, n = w.shape
    tn = 1536
    return pl.pallas_call(
        _ada_kernel,
        out_shape=jax.ShapeDtypeStruct((8, n), F32),
        grid=(n // tn,),
        in_specs=[pl.BlockSpec((8, d), lambda j: (0, 0)),
                  pl.BlockSpec((d, tn), lambda j: (0, j)),
                  pl.BlockSpec((1, tn), lambda j: (0, j))],
        out_specs=pl.BlockSpec((8, tn), lambda j: (0, j)),
        compiler_params=pltpu.CompilerParams(dimension_semantics=("arbitrary",),
                                             vmem_limit_bytes=VMEM_LIMIT_BYTES),
        name="ada",
    )(c8, w, b)


def _bias_kernel(tab_ref, bd_ref, bp_ref, o_ref):
    g = pl.program_id(0)
    p = pl.program_id(1)
    t = pl.program_id(2)
    head = jnp.where(g == 0, p, N_DIFF_HEADS + 2 * p + t)
    bd = bd_ref[...]
    bp = bp_ref[...]
    d0 = jnp.where(bd < 0, NEG, 0.0).astype(F32)
    d1 = jnp.zeros(bp.shape, F32)
    for b in range(N_BUCKETS):
        val = tab_ref[head * N_BUCKETS + b] * LOG2E
        d0 = jnp.where(bd == b, val, d0)
        d1 = jnp.where(bp == b, val, d1)
    o_ref[0, 0, 0] = d0
    o_ref[0, 0, 1] = d1


def _bias_tiles(rel_bias):
    t = ATT_T
    jj = np.arange(t)[:, None]
    ii = np.arange(t)[None, :]
    bd = np.where(ii >= jj, _t5_bucket_np(ii - jj), -1).astype(np.int32)
    bp = _t5_bucket_np(t + ii - jj)
    tab = rel_bias.T.reshape(-1)
    return pl.pallas_call(
        _bias_kernel,
        out_shape=jax.ShapeDtypeStruct((2, N_PAIRS, 2, t, 2 * t), F32),
        grid=(2, N_PAIRS, 2),
        in_specs=[pl.BlockSpec(memory_space=pltpu.SMEM),
                  pl.BlockSpec((t, t), lambda g, p, h: (0, 0)),
                  pl.BlockSpec((t, t), lambda g, p, h: (0, 0))],
        out_specs=pl.BlockSpec((1, 1, 2, t, t), lambda g, p, h: (g, p, 0, 0, h)),
        compiler_params=pltpu.CompilerParams(
            dimension_semantics=("arbitrary", "arbitrary", "arbitrary")),
        name="bias_tiles",
    )(tab, jnp.asarray(bd), jnp.asarray(bp))


def _proj_kernel(x_ref, g_ref, sh_ref, sc_ref, w_ref, o_ref):
    x = x_ref[0]
    h = _rms(x, g_ref[...]) * (1.0 + sc_ref[0]) + sh_ref[0]
    hb = h.astype(BF16)
    n_total = w_ref.shape[1]
    for n0 in range(0, n_total, 512):
        o_ref[0, :, n0:n0 + 512] = jnp.dot(
            hb, w_ref[:, n0:n0 + 512], preferred_element_type=F32).astype(BF16)


def _proj(x, g_mix, mod, w_in_bf):
    b, s, d = x.shape
    n = w_in_bf.shape[1]
    tm = min(PROJ_TM, s)
    return pl.pallas_call(
        _proj_kernel,
        out_shape=jax.ShapeDtypeStruct((b, s, n), BF16),
        grid=(b, s // tm),
        in_specs=[pl.BlockSpec((1, tm, d), lambda bi, i: (bi, i, 0)),
                  pl.BlockSpec((1, d), lambda bi, i: (0, 0)),
                  pl.BlockSpec((1, 1, d), lambda bi, i: (bi * 6 + 0, 0, 0)),
                  pl.BlockSpec((1, 1, d), lambda bi, i: (bi * 6 + 1, 0, 0)),
                  pl.BlockSpec((d, n), lambda bi, i: (0, 0))],
        out_specs=pl.BlockSpec((1, tm, n), lambda bi, i: (bi, i, 0)),
        compiler_params=pltpu.CompilerParams(dimension_semantics=("arbitrary", "arbitrary"),
                                             vmem_limit_bytes=VMEM_LIMIT_BYTES),
        name="proj",
    )(x, g_mix, mod, mod, w_in_bf)


def _stack_q(q_ref, q2_ref, scale):
    t = ATT_T
    row = lax.broadcasted_iota(I32, (LANES, t), 0)
    for p in range(N_PAIRS):
        q = (q_ref[0, :, p * LANES:(p + 1) * LANES].astype(F32) * scale).T
        q2_ref[p, :, 0:t] = jnp.where(row < 64, q, 0.0).astype(BF16)
        q2_ref[p, :, t:2 * t] = jnp.where(row >= 64, q, 0.0).astype(BF16)


def _transpose_values(v_ref, vt_ref):
    t = ATT_T
    nq = vt_ref.shape[0]

    def body(n, carry):
        for p in range(N_PAIRS):
            blk = v_ref[0, pl.ds(pl.multiple_of(n * t, t), t), p * LANES:(p + 1) * LANES]
            vt_ref[n, p * V_ROWS:p * V_ROWS + LANES, :] = blk.astype(F32).T.astype(BF16)
            vt_ref[n, p * V_ROWS + LANES:(p + 1) * V_ROWS, :] = jnp.ones((V_ROWS - LANES, t), BF16)
        return carry

    lax.fori_loop(0, nq, body, 0)


def _flash_init(m_ref, acc_ref):
    m_ref[...] = jnp.full(m_ref.shape, NEG, F32)
    acc_ref[...] = jnp.zeros(acc_ref.shape, F32)


def _score_slot(p, h, bank):
    return bank * (2 * N_PAIRS) + 2 * p + h


def _flash_scores(p, h, n, bank, q2_ref, k_ref, s_ref, smax_ref):
    t = ATT_T
    kt = k_ref[0, pl.ds(pl.multiple_of(n * t, t), t), p * LANES:(p + 1) * LANES]
    s = jnp.dot(kt, q2_ref[p, :, h * t:(h + 1) * t], preferred_element_type=F32)
    s_ref[_score_slot(p, h, bank)] = s
    smax_ref[_score_slot(p, h, bank)] = jnp.max(s, axis=0, keepdims=True)


def _flash_absorb(p, h, n, bank, tile_bias, row_bias, live, v_ref, s_ref, smax_ref, m_ref,
                  acc_ref):
    t = ATT_T
    cs = slice(h * t, (h + 1) * t)
    vt = v_ref[n, p * V_ROWS:(p + 1) * V_ROWS, :]
    s = s_ref[_score_slot(p, h, bank)]
    if tile_bias is not None:
        s = s + tile_bias
        mx = jnp.max(s, axis=0, keepdims=True)
    else:
        mx = smax_ref[_score_slot(p, h, bank)]
    if row_bias is not None:
        mx = mx + row_bias
    if live is not None:
        mx = jnp.where(live, mx, NEG)
    m_prev = m_ref[p, :, cs]
    m_new = jnp.maximum(m_prev, mx)
    alpha = jnp.exp2(m_prev - m_new)
    shift = m_new if row_bias is None else m_new - row_bias
    if live is not None:
        shift = jnp.where(live, shift, -NEG)
    pt = jnp.exp2(s - shift)
    acc_ref[p, :, cs] = alpha * acc_ref[p, :, cs] + jnp.dot(vt, pt.astype(BF16),
                                                            preferred_element_type=F32)
    m_ref[p, :, cs] = m_new


def _far_bias(tab_ref, head):
    return tab_ref[head * N_BUCKETS + (N_BUCKETS - 1)] * LOG2E


def _flash_sweep(i, far_args, prev_args, own_args, refs):
    q2_ref, k_ref, v_ref, s_ref, smax_ref, m_ref, acc_ref = refs
    chains = [(p, h) for p in range(N_PAIRS) for h in range(2)]

    def scores_tile(n, bank):
        for p, h in chains:
            _flash_scores(p, h, n, bank, q2_ref, k_ref, s_ref, smax_ref)

    def step(n, bank, args, with_next):
        for p, h in chains:
            if with_next:
                _flash_scores(p, h, n + 1, 1 - bank, q2_ref, k_ref, s_ref, smax_ref)
            _flash_absorb(p, h, n, bank, *args(p, h), v_ref, s_ref, smax_ref, m_ref, acc_ref)

    def far_step(n, bank):
        step(n, bank, lambda p, h: far_args(p, h, n), True)

    n_far = jnp.maximum(i - 1, 0)
    lead = jnp.bitwise_and(n_far, 1)

    @pl.when(jnp.bitwise_and(i, 1) == 0)
    def _():
        scores_tile(0, 0)

    @pl.when(jnp.bitwise_and(i, 1) == 1)
    def _():
        scores_tile(0, 1)

    @pl.when(lead == 1)
    def _():
        far_step(0, 0)

    def pair_body(j, carry):
        n = lead + 2 * j
        far_step(n, 1)
        far_step(n + 1, 0)
        return carry

    lax.fori_loop(0, jnp.right_shift(n_far, 1), pair_body, 0)

    @pl.when(i >= 1)
    def _():
        step(i - 1, 1, prev_args, True)

    step(i, 0, own_args, False)


def _diff_kernel(tab_ref, q_ref, k_ref, v_ref, d_ref, lamv_ref, g_ref, o_ref,
                 q2_ref, s_ref, smax_ref, m_ref, acc_ref, vt_ref):
    t = ATT_T
    i = pl.program_id(1)

    @pl.when(i == 0)
    def _():
        _transpose_values(v_ref, vt_ref)

    _stack_q(q_ref, q2_ref, DIFF_HEAD_DIM ** -0.5 * LOG2E)
    _flash_init(m_ref, acc_ref)
    refs = (q2_ref, k_ref, vt_ref, s_ref, smax_ref, m_ref, acc_ref)
    _flash_sweep(i,
                 lambda p, h, n: (None, _far_bias(tab_ref, p), None),
                 lambda p, h: (d_ref[0, p, 1, :, h * t:(h + 1) * t], None, None),
                 lambda p, h: (d_ref[0, p, 0, :, h * t:(h + 1) * t], None, None),
                 refs)

    lv = lamv_ref[...]
    lam = (jnp.exp(jnp.sum(lv[0:1] * lv[1:2], axis=1, keepdims=True))
           - jnp.exp(jnp.sum(lv[2:3] * lv[3:4], axis=1, keepdims=True)) + LAM_INIT)
    for p in range(N_PAIRS):
        acc = acc_ref[p, 0:LANES, :]
        l = acc_ref[p, LANES:LANES + 1, :]
        o = acc[:, 0:t] / l[:, 0:t] - lam * (acc[:, t:2 * t] / l[:, t:2 * t])
        y = o * lax.rsqrt(jnp.mean(o * o, axis=0, keepdims=True) + RMS_EPS) * g_ref[...]
        o_ref[0, :, p * LANES:(p + 1) * LANES] = (y * (1.0 - LAM_INIT)).T.astype(BF16)


def _moba_kernel(tab_ref, q_ref, k_ref, v_ref, d_ref, o_ref,
                 q2_ref, s_ref, smax_ref, m_ref, acc_ref, vt_ref, kmean_ref, sel_ref):
    t = ATT_T
    i = pl.program_id(1)
    nb = k_ref.shape[1] // t
    nbp = kmean_ref.shape[1]

    @pl.when(i == 0)
    def _():
        _transpose_values(v_ref, vt_ref)
        kmean_ref[...] = jnp.zeros(kmean_ref.shape, F32)
        for p in range(N_PAIRS):
            for n in range(nb):
                blk = k_ref[0, n * t:(n + 1) * t, p * LANES:(p + 1) * LANES].astype(F32)
                kmean_ref[p, n:n + 1, :] = jnp.sum(blk, axis=0, keepdims=True) * (1.0 / t)

    _stack_q(q_ref, q2_ref, MOBA_HEAD_DIM ** -0.5 * LOG2E)
    _flash_init(m_ref, acc_ref)

    blk_id = lax.broadcasted_iota(I32, (nbp, 2 * t), 0)
    blk_f = blk_id.astype(F32)
    for p in range(N_PAIRS):
        gs = jnp.dot(kmean_ref[p].astype(BF16), q2_ref[p], preferred_element_type=F32)
        g = jnp.where(blk_id < i, gs, -jnp.inf)
        sel = jnp.full((nbp, 2 * t), NEG, F32)
        for _ in range(MOBA_TOPK):
            mx = jnp.max(g, axis=0, keepdims=True)
            first = jnp.min(jnp.where(g == mx, blk_f, float(nbp)), axis=0, keepdims=True)
            hit = blk_f == first
            sel = jnp.where(hit, 0.0, sel)
            g = jnp.where(hit, -jnp.inf, g)
        sel_ref[p] = sel

    def live(p, h, n):
        return sel_ref[p, pl.ds(n, 1), h * t:(h + 1) * t] > -1.0

    refs = (q2_ref, k_ref, vt_ref, s_ref, smax_ref, m_ref, acc_ref)
    _flash_sweep(i,
                 lambda p, h, n: (None, _far_bias(tab_ref, N_DIFF_HEADS + 2 * p + h),
                                  live(p, h, n)),
                 lambda p, h: (d_ref[0, p, 1, :, h * t:(h + 1) * t], None, live(p, h, i - 1)),
                 lambda p, h: (d_ref[0, p, 0, :, h * t:(h + 1) * t], None, None),
                 refs)

    row = lax.broadcasted_iota(I32, (LANES, t), 0)
    for p in range(N_PAIRS):
        acc = acc_ref[p, 0:LANES, :]
        l = acc_ref[p, LANES:LANES + 1, :]
        o = jnp.where(row < 64, acc[:, 0:t] / l[:, 0:t], acc[:, t:2 * t] / l[:, t:2 * t])
        o_ref[0, :, p * LANES:(p + 1) * LANES] = o.T.astype(BF16)


def _attention(kind, tab, proj, dtiles, extra, cols):
    b, s, _ = proj.shape
    t = ATT_T
    nq = s // t
    w = N_PAIRS * LANES
    qc, kc, vc = cols
    gidx = 0 if kind == "diff" else 1
    in_specs = [pl.BlockSpec(memory_space=pltpu.SMEM),
                pl.BlockSpec((1, t, w), lambda bi, i: (bi, i, qc // w)),
                pl.BlockSpec((1, s, w), lambda bi, i: (bi, 0, kc // w)),
                pl.BlockSpec((1, s, w), lambda bi, i: (bi, 0, vc // w)),
                pl.BlockSpec((1, N_PAIRS, 2, t, 2 * t), lambda bi, i: (gidx, 0, 0, 0, 0))]
    scratch = [pltpu.VMEM((N_PAIRS, LANES, 2 * t), BF16),
               pltpu.VMEM((2 * 2 * N_PAIRS, t, t), F32),
               pltpu.VMEM((2 * 2 * N_PAIRS, 1, t), F32),
               pltpu.VMEM((N_PAIRS, 1, 2 * t), F32),
               pltpu.VMEM((N_PAIRS, V_ROWS, 2 * t), F32),
               pltpu.VMEM((nq, N_PAIRS * V_ROWS, t), BF16)]
    if kind == "diff":
        lamv, subln_g = extra
        in_specs += [pl.BlockSpec(lamv.shape, lambda bi, i: (0, 0)),
                     pl.BlockSpec(subln_g.shape, lambda bi, i: (0, 0))]
        args = (tab, proj, proj, proj, dtiles, lamv, subln_g)
        body = _diff_kernel
    else:
        nbp = max(32, nq)
        scratch += [pltpu.VMEM((N_PAIRS, nbp, LANES), F32), pltpu.VMEM((N_PAIRS, nbp, 2 * t), F32)]
        args = (tab, proj, proj, proj, dtiles)
        body = _moba_kernel
    return pl.pallas_call(
        body,
        out_shape=jax.ShapeDtypeStruct((b, s, w), BF16),
        grid=(b, nq),
        in_specs=in_specs,
        out_specs=pl.BlockSpec((1, t, w), lambda bi, i: (bi, i, 0)),
        scratch_shapes=scratch,
        compiler_params=pltpu.CompilerParams(
            dimension_semantics=("arbitrary", "arbitrary"),
            vmem_limit_bytes=VMEM_LIMIT_BYTES),
        name=kind + "_attn",
    )(*args)


def _merge_kernel(x_ref, ya_ref, yb_ref, gla_ref, glb_ref, wa_ref, wb_ref, wo_ref,
                  gt_ref, shf_ref, scf_ref, gf_ref, wr_ref, br_ref,
                  x1_ref, h2_ref, ti_ref, tw_ref):
    ts = MERGE_SUB
    subs = [slice(j * ts, (j + 1) * ts) for j in range(x_ref.shape[1] // ts)]
    branch = [(jnp.dot(ya_ref[0, sl], wa_ref[...], preferred_element_type=F32),
               jnp.dot(yb_ref[0, sl], wb_ref[...], preferred_element_type=F32)) for sl in subs]
    mixed = []
    for sl, (a, b) in zip(subs, branch):
        merged = (jax.nn.sigmoid(gla_ref[0, sl].astype(F32)) * a
                  + jax.nn.sigmoid(glb_ref[0, sl].astype(F32)) * b)
        mixed.append(jnp.dot(merged.astype(BF16), wo_ref[...], preferred_element_type=F32))
    lane = lax.broadcasted_iota(I32, (ts, LANES), 1)
    lane_f = lane.astype(F32)
    for j, (sl, mo) in enumerate(zip(subs, mixed)):
        x1 = x_ref[0, sl] + gt_ref[0] * mo
        x1_ref[0, sl] = x1
        h = _rms(x1, gf_ref[...]) * (1.0 + scf_ref[0]) + shf_ref[0]
        _store_token_tiles(h2_ref.at[0, j * ts * ROW_CHUNKS:(j + 1) * ts * ROW_CHUNKS], h)
        logits = jnp.dot(h.astype(BF16), wr_ref[...], preferred_element_type=F32) + br_ref[...]
        g = jnp.where(lane < N_EXPERTS, logits, -jnp.inf)
        vals, idxs = [], []
        for _ in range(TOP_K):
            mx = jnp.max(g, axis=1, keepdims=True)
            first = jnp.min(jnp.where(g == mx, lane_f, float(LANES)), axis=1, keepdims=True)
            vals.append(mx)
            idxs.append(first)
            g = jnp.where(lane_f == first, -jnp.inf, g)
        es = [jnp.exp(v - vals[0]) for v in vals]
        den = es[0] + es[1] + es[2] + es[3]
        ti = jnp.zeros((ts, LANES), F32)
        tw = jnp.zeros((ts, LANES), F32)
        for k in range(TOP_K):
            ti = jnp.where(lane == k, idxs[k], ti)
            tw = jnp.where(lane == k, es[k] / den, tw)
        ti_ref[0, sl] = ti.astype(I32)
        tw_ref[0, sl] = tw


def _merge(x, proj, y_a, y_b, w_br_a, w_br_b, w_out, mod, g_ffn, w_router, b_router):
    b, s, d = x.shape
    tm = min(MERGE_TM, s)
    full = lambda shape: pl.BlockSpec(shape, lambda bi, i: (0,) * len(shape))
    tile = lambda w, c=0: pl.BlockSpec((1, tm, w), lambda bi, i: (bi, i, c))
    modspec = lambda j: pl.BlockSpec((1, 1, d), lambda bi, i: (bi * 6 + j, 0, 0))
    return pl.pallas_call(
        _merge_kernel,
        out_shape=(jax.ShapeDtypeStruct((b, s, d), F32),
                   jax.ShapeDtypeStruct((b, s * ROW_CHUNKS, LANES), F32),
                   jax.ShapeDtypeStruct((b, s, LANES), I32),
                   jax.ShapeDtypeStruct((b, s, LANES), F32)),
        grid=(b, s // tm),
        in_specs=[tile(d), tile(512), tile(512), tile(d, 3), tile(d, 4),
                  full(w_br_a.shape), full(w_br_b.shape), full(w_out.shape),
                  modspec(2), modspec(3), modspec(4), full(g_ffn.shape),
                  full(w_router.shape), full(b_router.shape)],
        out_specs=(tile(d), pl.BlockSpec((1, tm * ROW_CHUNKS, LANES), lambda bi, i: (bi, i, 0)),
                   tile(LANES), tile(LANES)),
        compiler_params=pltpu.CompilerParams(dimension_semantics=("arbitrary", "arbitrary"),
                                             vmem_limit_bytes=VMEM_LIMIT_BYTES),
        name="merge",
    )(x, y_a, y_b, proj, proj, w_br_a, w_br_b, w_out, mod, mod, mod, g_ffn,
      w_router, b_router)


def _dispatch_kernel(padtile_ref, nu_ref, pos_ref, h2_ref, xs_ref, zeros_ref, zsem, sem):
    step = pl.program_id(0)
    rc = ROW_CHUNKS
    n_tok = h2_ref.shape[0] // rc
    pad_rows = zeros_ref.shape[0]
    n_tiles = xs_ref.shape[0] // pad_rows

    def zero_tile(tile):
        start = pl.multiple_of(tile * pad_rows, pad_rows)
        return pltpu.make_async_copy(zeros_ref, xs_ref.at[pl.ds(start, pad_rows)], zsem)

    @pl.when(step == 0)
    def _():
        zeros_ref[...] = jnp.zeros(zeros_ref.shape, F32)
        for e in range(N_EXPERTS):
            @pl.when(padtile_ref[e] >= 0)
            def _():
                zero_tile(padtile_ref[e]).start()
        for e in range(N_EXPERTS):
            @pl.when(padtile_ref[e] >= 0)
            def _():
                zero_tile(padtile_ref[e]).wait()

        def start_tail(tile, carry):
            zero_tile(tile).start()
            return carry

        def wait_tail(tile, carry):
            zero_tile(tile).wait()
            return carry

        lax.fori_loop(nu_ref[0], n_tiles, start_tail, 0)
        lax.fori_loop(nu_ref[0], n_tiles, wait_tail, 0)

    def issue(j, carry):
        src = h2_ref.at[pl.ds(pl.multiple_of(j * rc, rc), rc)]
        for k in range(TOP_K):
            dst = xs_ref.at[pl.ds(pl.multiple_of(pos_ref[j * TOP_K + k], rc), rc)]
            pltpu.make_async_copy(src, dst, sem).start(priority=k % 2)
        return carry

    lax.fori_loop(0, n_tok, issue, 0, unroll=ROW_DMA_UNROLL)
    for k in range(TOP_K):
        pltpu.make_async_copy(h2_ref, xs_ref.at[pl.ds(0, n_tok * rc)], sem).wait()


def _dispatch(h2_tiles, pos_start, padtile, n_used, n_rows):
    rc = ROW_CHUNKS
    t_tok = h2_tiles.shape[0] // rc
    tm = min(DISPATCH_TM, t_tok)
    return pl.pallas_call(
        _dispatch_kernel,
        out_shape=jax.ShapeDtypeStruct((n_rows * rc, LANES), F32),
        grid_spec=pltpu.PrefetchScalarGridSpec(
            num_scalar_prefetch=2,
            grid=(t_tok // tm,),
            in_specs=[pl.BlockSpec((tm * TOP_K,), lambda s, pt, nu: (s,),
                                   memory_space=pltpu.SMEM),
                      pl.BlockSpec((tm * rc, LANES), lambda s, pt, nu: (s, 0))],
            out_specs=pl.BlockSpec(memory_space=pl.ANY),
            scratch_shapes=[pltpu.VMEM((MOE_TM * rc, LANES), F32),
                            pltpu.SemaphoreType.DMA, pltpu.SemaphoreType.DMA]),
        compiler_params=pltpu.CompilerParams(dimension_semantics=("arbitrary",),
                                             vmem_limit_bytes=VMEM_LIMIT_BYTES),
        name="dispatch",
    )(padtile, n_used, pos_start, h2_tiles)


def _moe_kernel(te_ref, nu_ref, xs_ref, wgu_ref, bgu_ref, wd_ref, bd_ref,
                ys_ref, wgu_bf, wd_bf, gu_ref):
    r = pl.program_id(0)
    busy = r < nu_ref[0]
    fresh = jnp.logical_or(r == 0, te_ref[r] != te_ref[jnp.maximum(r - 1, 0)])

    @pl.when(jnp.logical_and(busy, fresh))
    def _():
        wgu_bf[...] = wgu_ref[0].astype(BF16)
        wd_bf[...] = wd_ref[0].astype(BF16)

    @pl.when(jnp.logical_not(busy))
    def _():
        ys_ref[...] = jnp.zeros(ys_ref.shape, F32)

    @pl.when(busy)
    def _():
        x = _load_token_tiles(xs_ref).astype(BF16)
        fc = MOE_FF_CHUNK
        for c in range(2 * D_FF // fc):
            lo, hi = c * fc, (c + 1) * fc
            gu_ref[:, lo:hi] = (jnp.dot(x, wgu_bf[:, lo:hi], preferred_element_type=F32)
                                + bgu_ref[0][:, lo:hi])
        y = None
        for c in range(D_FF // fc):
            lo, hi = c * fc, (c + 1) * fc
            gate = jnp.minimum(gu_ref[:, lo:hi], SWIGLU_LIMIT)
            up = jnp.clip(gu_ref[:, D_FF + lo:D_FF + hi], -SWIGLU_LIMIT, SWIGLU_LIMIT)
            act = (up + 1.0) * gate * jax.nn.sigmoid(SWIGLU_ALPHA * gate)
            part = jnp.dot(act.astype(BF16), wd_bf[lo:hi, :], preferred_element_type=F32)
            y = part if y is None else y + part
        _store_token_tiles(ys_ref, y + bd_ref[0])


def _moe(xs_tiles, tile_expert, n_used, w_gate_up, b_gate_up, w_down, b_down):
    d = D_MODEL
    rc = ROW_CHUNKS
    tm = MOE_TM
    n_tiles = tile_expert.shape[0]
    e, _, f2 = w_gate_up.shape
    wsel = lambda r, te, nu: (te[r], 0, 0)
    rows = lambda r, te, nu: (jnp.minimum(r, nu[0] - 1), 0)
    return pl.pallas_call(
        _moe_kernel,
        out_shape=jax.ShapeDtypeStruct((n_tiles * tm * rc, LANES), F32),
        grid_spec=pltpu.PrefetchScalarGridSpec(
            num_scalar_prefetch=2,
            grid=(n_tiles,),
            in_specs=[pl.BlockSpec((tm * rc, LANES), rows),
                      pl.BlockSpec((1, d, f2), wsel),
                      pl.BlockSpec((1, 1, f2), wsel),
                      pl.BlockSpec((1, D_FF, d), wsel),
                      pl.BlockSpec((1, 1, d), wsel)],
            out_specs=pl.BlockSpec((tm * rc, LANES), lambda r, te, nu: (r, 0)),
            scratch_shapes=[pltpu.VMEM((d, f2), BF16), pltpu.VMEM((D_FF, d), BF16),
                            pltpu.VMEM((tm, f2), F32)]),
        compiler_params=pltpu.CompilerParams(dimension_semantics=("arbitrary",),
                                             vmem_limit_bytes=VMEM_LIMIT_BYTES),
        name="moe",
    )(tile_expert, n_used, xs_tiles, w_gate_up, b_gate_up, w_down, b_down)


def _combine_kernel(pos_ref, ys_ref, x1_ref, tw_ref, gt_ref, gf_ref, o_ref, gbuf, sem):
    tm = x1_ref.shape[0]
    rc = ROW_CHUNKS

    def issue(j, carry):
        for k in range(TOP_K):
            src = ys_ref.at[pl.ds(pl.multiple_of(pos_ref[j * TOP_K + k], rc), rc)]
            dst = gbuf.at[k, pl.ds(pl.multiple_of(j * rc, rc), rc)]
            pltpu.make_async_copy(src, dst, sem).start(priority=k % 2)
        return carry

    lax.fori_loop(0, tm, issue, 0, unroll=ROW_DMA_UNROLL)
    for k in range(TOP_K):
        pltpu.make_async_copy(ys_ref.at[pl.ds(0, tm * rc)], gbuf.at[k], sem).wait()

    tw = tw_ref[...]
    acc = tw[:, 0:1] * _load_token_tiles(gbuf.at[0])
    for k in range(1, TOP_K):
        acc = acc + tw[:, k:k + 1] * _load_token_tiles(gbuf.at[k])
    x2 = x1_ref[...] + gt_ref[0] * acc
    o_ref[...] = _rms(x2, gf_ref[...])


def _combine(ys, pos_flat, x1, tw, mod, g_final, seq):
    t, d = x1.shape
    tm = min(COMBINE_TM, t)
    return pl.pallas_call(
        _combine_kernel,
        out_shape=jax.ShapeDtypeStruct((t, d), F32),
        grid=(t // tm,),
        in_specs=[pl.BlockSpec((tm * TOP_K,), lambda s: (s,), memory_space=pltpu.SMEM),
                  pl.BlockSpec(memory_space=pl.ANY),
                  pl.BlockSpec((tm, d), lambda s: (s, 0)),
                  pl.BlockSpec((tm, LANES), lambda s: (s, 0)),
                  pl.BlockSpec((1, 1, d), lambda s: ((s * tm) // seq * 6 + 5, 0, 0)),
                  pl.BlockSpec((1, d), lambda s: (0, 0))],
        out_specs=pl.BlockSpec((tm, d), lambda s: (s, 0)),
        scratch_shapes=[pltpu.VMEM((TOP_K, tm * ROW_CHUNKS, LANES), F32),
                        pltpu.SemaphoreType.DMA],
        compiler_params=pltpu.CompilerParams(dimension_semantics=("arbitrary",),
                                             vmem_limit_bytes=VMEM_LIMIT_BYTES),
        name="combine",
    )(pos_flat, ys, x1, tw, mod, g_final)


def _routing_tables(top_idx, n_tiles):
    tm = MOE_TM
    e_flat = top_idx.reshape(-1)
    onehot = (e_flat[:, None] == jnp.arange(N_EXPERTS, dtype=I32)[None, :]).astype(I32)
    csum = jnp.cumsum(onehot, axis=0)
    counts = csum[-1]
    rank = jnp.sum(csum * onehot, axis=1) - 1
    tiles_per = (counts + tm - 1) // tm
    tile_end = jnp.cumsum(tiles_per)
    tile_start = tile_end - tiles_per
    pos = (jnp.sum(onehot * tile_start[None, :], axis=1) * tm + rank).astype(I32)
    n_used = tile_end[-1:].astype(I32)
    tile_ids = jnp.arange(n_tiles, dtype=I32)
    tile_expert = jnp.minimum(
        jnp.sum((tile_ids[:, None] >= tile_end[None, :]).astype(I32), axis=1),
        N_EXPERTS - 1).astype(I32)
    padtile = jnp.where(counts % tm != 0, tile_end - 1, -1).astype(I32)
    return pos, tile_expert, n_used, padtile


def kernel(x, c, rel_bias, w_ada, b_ada, g_mix, w_in, lambda_q1, lambda_k1, lambda_q2,
           lambda_k2, subln_g, w_br_a, w_br_b, w_out, g_ffn, w_router, b_router,
           w_gate_up, b_gate_up, w_down, b_down, g_final):
    b, s, d = x.shape
    assert d == D_MODEL and s % ATT_T == 0 and MOBA_BLOCK == ATT_T
    t_tok = b * s

    c8 = jnp.zeros((8, d), F32).at[:b].set(c)
    ada = _ada(c8, w_ada[0], b_ada[0].reshape(1, -1))
    mod = ada[:b].reshape(b * 6, 1, d)

    dtiles = _bias_tiles(rel_bias)
    tab = rel_bias.T.reshape(-1)

    proj = _proj(x, g_mix[0].reshape(1, d), mod, w_in[0].astype(BF16))

    lamv = jnp.concatenate([lambda_q1, lambda_k1, lambda_q2, lambda_k2], axis=0)
    y_a = _attention("diff", tab, proj, dtiles, (lamv, subln_g[0].reshape(-1, 1)), (0, 512, 1024))
    y_b = _attention("moba", tab, proj, dtiles, None, (1536, 2048, 2560))

    wr = jnp.zeros((d, LANES), BF16).at[:, :N_EXPERTS].set(w_router[0].astype(BF16))
    br = jnp.zeros((1, LANES), F32).at[0, :N_EXPERTS].set(b_router[0])
    x1, h2, ti, tw = _merge(x, proj, y_a, y_b, w_br_a[0].astype(BF16), w_br_b[0].astype(BF16),
                            w_out[0].astype(BF16), mod, g_ffn[0].reshape(1, d), wr, br)

    n_tiles = (t_tok * TOP_K) // MOE_TM + N_EXPERTS
    top_idx = ti.reshape(t_tok, LANES)[:, :TOP_K]
    pos, tile_expert, n_used, padtile = _routing_tables(top_idx, n_tiles)
    pos_start = pos * ROW_CHUNKS

    xs = _dispatch(h2.reshape(t_tok * ROW_CHUNKS, LANES), pos_start, padtile, n_used,
                   n_tiles * MOE_TM)
    ys = _moe(xs, tile_expert, n_used, w_gate_up[0], b_gate_up[0].reshape(N_EXPERTS, 1, -1),
              w_down[0], b_down[0].reshape(N_EXPERTS, 1, -1))
    out = _combine(ys, pos_start, x1.reshape(t_tok, d), tw.reshape(t_tok, LANES), mod,
                   g_final.reshape(1, d), s)
    return out.reshape(b, s, d)
```

```python
import functools
import math

import jax
import jax.numpy as jnp
import numpy as np
from jax import lax
from jax.experimental import pallas as pl
from jax.experimental.pallas import tpu as pltpu

F32 = jnp.float32
BF16 = jnp.bfloat16
I32 = jnp.int32

D_MODEL = 1024
N_DIFF_HEADS = 4
DIFF_HEAD_DIM = 64
N_MOBA_HEADS = 8
MOBA_HEAD_DIM = 64
MOBA_BLOCK = 256
MOBA_TOPK = 3
N_HEADS_TOTAL = N_DIFF_HEADS + N_MOBA_HEADS
N_BUCKETS = 32
MAX_DISTANCE = 128
N_EXPERTS = 32
TOP_K = 4
D_FF = D_MODEL
SWIGLU_LIMIT = 7.0
SWIGLU_ALPHA = 1.702
RMS_EPS = 1e-5
LAM_INIT = 0.8 - 0.6 * math.exp(-0.3 * 0)
IN_WIDTH = 5120

LANES = 128
VMEM_LIMIT_BYTES = 56 * 1024 * 1024

ATT_T = 256
N_PAIRS = 4
ONES_ROWS = 16
PROJ_TM = 512
MERGE_TM = 512
MERGE_SUB = 256
MOE_TM = 512
MOE_FF_CHUNK = 256
DISPATCH_TM = 1024
ROW_DMA_UNROLL = 8
COMBINE_TM = 1024
NEG = -1e30
LOG2E = math.log2(math.e)


def _t5_bucket_np(dist):
    n = np.maximum(dist, 0)
    max_exact = N_BUCKETS // 2
    nf = np.maximum(n, max_exact).astype(np.float32)
    large = max_exact + (np.log(nf / max_exact) / math.log(MAX_DISTANCE / max_exact)
                         * (N_BUCKETS - max_exact)).astype(np.int32)
    large = np.minimum(large, N_BUCKETS - 1)
    return np.where(n < max_exact, n, large).astype(np.int32)


def _rms(x, g):
    return x * lax.rsqrt(jnp.mean(x * x, axis=-1, keepdims=True) + RMS_EPS) * g


ROW_CHUNKS = D_MODEL // LANES


def _store_token_tiles(ref, x):
    n = x.shape[0]
    for c in range(ROW_CHUNKS):
        ref[pl.ds(c, n, stride=ROW_CHUNKS), :] = x[:, c * LANES:(c + 1) * LANES]


def _load_token_tiles(ref):
    n = ref.shape[0] // ROW_CHUNKS
    return jnp.concatenate([ref[pl.ds(c, n, stride=ROW_CHUNKS), :] for c in range(ROW_CHUNKS)],
                           axis=1)


def _ada_kernel(c_ref, w_ref, b_ref, o_ref):
    o_ref[...] = jnp.dot(c_ref[...].astype(BF16), w_ref[...].astype(BF16),
                         preferred_element_type=F32) + b_ref[...]


def _ada(c8, w, b):
    d, n = w.shape
    tn = 1536
    return pl.pallas_call(
        _ada_kernel,
        out_shape=jax.ShapeDtypeStruct((8, n), F32),
        grid=(n // tn,),
        in_specs=[pl.BlockSpec((8, d), lambda j: (0, 0)),
                  pl.BlockSpec((d, tn), lambda j: (0, j)),
                  pl.BlockSpec((1, tn), lambda j: (0, j))],
        out_specs=pl.BlockSpec((8, tn), lambda j: (0, j)),
        compiler_params=pltpu.CompilerParams(dimension_semantics=("arbitrary",),
                                             vmem_limit_bytes=VMEM_LIMIT_BYTES),
        name="ada",
    )(c8, w, b)


def _bias_kernel(tab_ref, bd_ref, bp_ref, o_ref):
    g = pl.program_id(0)
    p = pl.program_id(1)
    t = pl.program_id(2)
    head = jnp.where(g == 0, p, N_DIFF_HEADS + 2 * p + t)
    bd = bd_ref[...]
    bp = bp_ref[...]
    d0 = jnp.where(bd < 0, NEG, 0.0).astype(F32)
    d1 = jnp.zeros(bp.shape, F32)
    for b in range(N_BUCKETS):
        val = tab_ref[head * N_BUCKETS + b] * LOG2E
        d0 = jnp.where(bd == b, val, d0)
        d1 = jnp.where(bp == b, val, d1)
    o_ref[0, 0, 0] = d0
    o_ref[0, 0, 1] = d1


def _bias_tiles(rel_bias):
    t = ATT_T
    jj = np.arange(t)[:, None]
    ii = np.arange(t)[None, :]
    bd = np.where(ii >= jj, _t5_bucket_np(ii - jj), -1).astype(np.int32)
    bp = _t5_bucket_np(t + ii - jj)
    tab = rel_bias.T.reshape(-1)
    return pl.pallas_call(
        _bias_kernel,
        out_shape=jax.ShapeDtypeStruct((2, N_PAIRS, 2, t, 2 * t), F32),
        grid=(2, N_PAIRS, 2),
        in_specs=[pl.BlockSpec(memory_space=pltpu.SMEM),
                  pl.BlockSpec((t, t), lambda g, p, h: (0, 0)),
                  pl.BlockSpec((t, t), lambda g, p, h: (0, 0))],
        out_specs=pl.BlockSpec((1, 1, 2, t, t), lambda g, p, h: (g, p, 0, 0, h)),
        compiler_params=pltpu.CompilerParams(
            dimension_semantics=("arbitrary", "arbitrary", "arbitrary")),
        name="bias_tiles",
    )(tab, jnp.asarray(bd), jnp.asarray(bp))


def _proj_kernel(x_ref, g_ref, sh_ref, sc_ref, w_ref, o_ref):
    x = x_ref[0]
    h = _rms(x, g_ref[...]) * (1.0 + sc_ref[0]) + sh_ref[0]
    hb = h.astype(BF16)
    n_total = w_ref.shape[1]
    for n0 in range(0, n_total, 512):
        o_ref[0, :, n0:n0 + 512] = jnp.dot(
            hb, w_ref[:, n0:n0 + 512], preferred_element_type=F32).astype(BF16)


def _proj(x, g_mix, mod, w_in_bf):
    b, s, d = x.shape
    n = w_in_bf.shape[1]
    tm = min(PROJ_TM, s)
    return pl.pallas_call(
        _proj_kernel,
        out_shape=jax.ShapeDtypeStruct((b, s, n), BF16),
        grid=(b, s // tm),
        in_specs=[pl.BlockSpec((1, tm, d), lambda bi, i: (bi, i, 0)),
                  pl.BlockSpec((1, d), lambda bi, i: (0, 0)),
                  pl.BlockSpec((1, 1, d), lambda bi, i: (bi * 6 + 0, 0, 0)),
                  pl.BlockSpec((1, 1, d), lambda bi, i: (bi * 6 + 1, 0, 0)),
                  pl.BlockSpec((d, n), lambda bi, i: (0, 0))],
        out_specs=pl.BlockSpec((1, tm, n), lambda bi, i: (bi, i, 0)),
        compiler_params=pltpu.CompilerParams(dimension_semantics=("arbitrary", "arbitrary"),
                                             vmem_limit_bytes=VMEM_LIMIT_BYTES),
        name="proj",
    )(x, g_mix, mod, mod, w_in_bf)


def _stack_q(q_ref, q2_ref, scale):
    t = ATT_T
    row = lax.broadcasted_iota(I32, (LANES, t), 0)
    for p in range(N_PAIRS):
        q = (q_ref[0, :, p * LANES:(p + 1) * LANES].astype(F32) * scale).T
        q2_ref[p, :, 0:t] = jnp.where(row < 64, q, 0.0).astype(BF16)
        q2_ref[p, :, t:2 * t] = jnp.where(row >= 64, q, 0.0).astype(BF16)


def _value_rows(p, h, vw):
    block = (p * (LANES // vw) + (h if vw < LANES else 0)) * (vw + ONES_ROWS)
    return block, vw + ONES_ROWS


def _transpose_values(v_ref, vt_ref, vw):
    t = ATT_T
    nq = vt_ref.shape[0]

    def body(n, carry):
        for p in range(N_PAIRS):
            blk = v_ref[0, pl.ds(pl.multiple_of(n * t, t), t), p * LANES:(p + 1) * LANES]
            blk_t = blk.astype(F32).T.astype(BF16)
            for h in range(LANES // vw):
                r0, _ = _value_rows(p, h, vw)
                vt_ref[n, r0:r0 + vw, :] = blk_t[h * vw:(h + 1) * vw, :]
                vt_ref[n, r0 + vw:r0 + vw + ONES_ROWS, :] = jnp.ones((ONES_ROWS, t), BF16)
        return carry

    lax.fori_loop(0, nq, body, 0)


def _flash_init(m_ref, acc_ref):
    m_ref[...] = jnp.full(m_ref.shape, NEG, F32)
    acc_ref[...] = jnp.zeros(acc_ref.shape, F32)


def _score_slot(p, h, bank):
    return bank * (2 * N_PAIRS) + 2 * p + h


def _flash_scores(p, h, n, bank, q2_ref, k_ref, s_ref, smax_ref):
    t = ATT_T
    kt = k_ref[0, pl.ds(pl.multiple_of(n * t, t), t), p * LANES:(p + 1) * LANES]
    s = jnp.dot(kt, q2_ref[p, :, h * t:(h + 1) * t], preferred_element_type=F32)
    s_ref[_score_slot(p, h, bank)] = s
    smax_ref[_score_slot(p, h, bank)] = jnp.max(s, axis=0, keepdims=True)


def _flash_absorb(p, h, n, bank, tile_bias, row_bias, live, v_ref, s_ref, smax_ref, m_ref,
                  acc_ref):
    t = ATT_T
    cs = slice(h * t, (h + 1) * t)
    r0, nr = _value_rows(p, h, acc_ref.shape[1] - ONES_ROWS)
    vt = v_ref[n, r0:r0 + nr, :]
    s = s_ref[_score_slot(p, h, bank)]
    if tile_bias is not None:
        s = s + tile_bias
        mx = jnp.max(s, axis=0, keepdims=True)
    else:
        mx = smax_ref[_score_slot(p, h, bank)]
    if row_bias is not None:
        mx = mx + row_bias
    if live is not None:
        mx = jnp.where(live, mx, NEG)
    m_prev = m_ref[p, :, cs]
    m_new = jnp.maximum(m_prev, mx)
    alpha = jnp.exp2(m_prev - m_new)
    shift = m_new if row_bias is None else m_new - row_bias
    if live is not None:
        shift = jnp.where(live, shift, -NEG)
    pt = jnp.exp2(s - shift)
    acc_ref[p, :, cs] = alpha * acc_ref[p, :, cs] + jnp.dot(vt, pt.astype(BF16),
                                                            preferred_element_type=F32)
    m_ref[p, :, cs] = m_new


def _far_bias(tab_ref, head):
    return tab_ref[head * N_BUCKETS + (N_BUCKETS - 1)] * LOG2E


def _flash_sweep(i, far_args, prev_args, own_args, refs):
    q2_ref, k_ref, v_ref, s_ref, smax_ref, m_ref, acc_ref = refs
    chains = [(p, h) for p in range(N_PAIRS) for h in range(2)]

    def scores_tile(n, bank):
        for p, h in chains:
            _flash_scores(p, h, n, bank, q2_ref, k_ref, s_ref, smax_ref)

    def step(n, bank, args, with_next):
        for p, h in chains:
            if with_next:
                _flash_scores(p, h, n + 1, 1 - bank, q2_ref, k_ref, s_ref, smax_ref)
            _flash_absorb(p, h, n, bank, *args(p, h), v_ref, s_ref, smax_ref, m_ref, acc_ref)

    def far_step(n, bank):
        step(n, bank, lambda p, h: far_args(p, h, n), True)

    n_far = jnp.maximum(i - 1, 0)
    lead = jnp.bitwise_and(n_far, 1)

    @pl.when(jnp.bitwise_and(i, 1) == 0)
    def _():
        scores_tile(0, 0)

    @pl.when(jnp.bitwise_and(i, 1) == 1)
    def _():
        scores_tile(0, 1)

    @pl.when(lead == 1)
    def _():
        far_step(0, 0)

    def pair_body(j, carry):
        n = lead + 2 * j
        far_step(n, 1)
        far_step(n + 1, 0)
        return carry

    lax.fori_loop(0, jnp.right_shift(n_far, 1), pair_body, 0)

    @pl.when(i >= 1)
    def _():
        step(i - 1, 1, prev_args, True)

    step(i, 0, own_args, False)


def _diff_kernel(tab_ref, q_ref, k_ref, v_ref, d_ref, lamv_ref, g_ref, o_ref,
                 q2_ref, s_ref, smax_ref, m_ref, acc_ref, vt_ref):
    t = ATT_T
    i = pl.program_id(1)

    @pl.when(i == 0)
    def _():
        _transpose_values(v_ref, vt_ref, LANES)

    _stack_q(q_ref, q2_ref, DIFF_HEAD_DIM ** -0.5 * LOG2E)
    _flash_init(m_ref, acc_ref)
    refs = (q2_ref, k_ref, vt_ref, s_ref, smax_ref, m_ref, acc_ref)
    _flash_sweep(i,
                 lambda p, h, n: (None, _far_bias(tab_ref, p), None),
                 lambda p, h: (d_ref[0, p, 1, :, h * t:(h + 1) * t], None, None),
                 lambda p, h: (d_ref[0, p, 0, :, h * t:(h + 1) * t], None, None),
                 refs)

    lv = lamv_ref[...]
    lam = (jnp.exp(jnp.sum(lv[0:1] * lv[1:2], axis=1, keepdims=True))
           - jnp.exp(jnp.sum(lv[2:3] * lv[3:4], axis=1, keepdims=True)) + LAM_INIT)
    for p in range(N_PAIRS):
        acc = acc_ref[p, 0:LANES, :]
        l = acc_ref[p, LANES:LANES + 1, :]
        o = acc[:, 0:t] / l[:, 0:t] - lam * (acc[:, t:2 * t] / l[:, t:2 * t])
        y = o * lax.rsqrt(jnp.mean(o * o, axis=0, keepdims=True) + RMS_EPS) * g_ref[...]
        o_ref[0, :, p * LANES:(p + 1) * LANES] = (y * (1.0 - LAM_INIT)).T.astype(BF16)


def _moba_kernel(tab_ref, q_ref, k_ref, v_ref, d_ref, o_ref,
                 q2_ref, s_ref, smax_ref, m_ref, acc_ref, vt_ref, kmean_ref, sel_ref):
    t = ATT_T
    i = pl.program_id(1)
    nb = k_ref.shape[1] // t
    nbp = kmean_ref.shape[1]

    @pl.when(i == 0)
    def _():
        _transpose_values(v_ref, vt_ref, MOBA_HEAD_DIM)
        kmean_ref[...] = jnp.zeros(kmean_ref.shape, F32)
        for p in range(N_PAIRS):
            for n in range(nb):
                blk = k_ref[0, n * t:(n + 1) * t, p * LANES:(p + 1) * LANES].astype(F32)
                kmean_ref[p, n:n + 1, :] = jnp.sum(blk, axis=0, keepdims=True) * (1.0 / t)

    _stack_q(q_ref, q2_ref, MOBA_HEAD_DIM ** -0.5 * LOG2E)
    _flash_init(m_ref, acc_ref)

    blk_id = lax.broadcasted_iota(I32, (nbp, 2 * t), 0)
    blk_f = blk_id.astype(F32)
    for p in range(N_PAIRS):
        gs = jnp.dot(kmean_ref[p].astype(BF16), q2_ref[p], preferred_element_type=F32)
        g = jnp.where(blk_id < i, gs, -jnp.inf)
        sel = jnp.full((nbp, 2 * t), NEG, F32)
        for _ in range(MOBA_TOPK):
            mx = jnp.max(g, axis=0, keepdims=True)
            first = jnp.min(jnp.where(g == mx, blk_f, float(nbp)), axis=0, keepdims=True)
            hit = blk_f == first
            sel = jnp.where(hit, 0.0, sel)
            g = jnp.where(hit, -jnp.inf, g)
        sel_ref[p] = sel

    def live(p, h, n):
        return sel_ref[p, pl.ds(n, 1), h * t:(h + 1) * t] > -1.0

    refs = (q2_ref, k_ref, vt_ref, s_ref, smax_ref, m_ref, acc_ref)
    _flash_sweep(i,
                 lambda p, h, n: (None, _far_bias(tab_ref, N_DIFF_HEADS + 2 * p + h),
                                  live(p, h, n)),
                 lambda p, h: (d_ref[0, p, 1, :, h * t:(h + 1) * t], None, live(p, h, i - 1)),
                 lambda p, h: (d_ref[0, p, 0, :, h * t:(h + 1) * t], None, None),
                 refs)

    hd = MOBA_HEAD_DIM
    for p in range(N_PAIRS):
        acc = acc_ref[p, 0:hd, :]
        l = acc_ref[p, hd:hd + 1, :]
        o = jnp.concatenate([acc[:, 0:t] / l[:, 0:t], acc[:, t:2 * t] / l[:, t:2 * t]], axis=0)
        o_ref[0, :, p * LANES:(p + 1) * LANES] = o.T.astype(BF16)


def _attention(kind, tab, proj, dtiles, extra, cols):
    b, s, _ = proj.shape
    t = ATT_T
    nq = s // t
    w = N_PAIRS * LANES
    qc, kc, vc = cols
    gidx = 0 if kind == "diff" else 1
    vw = LANES if kind == "diff" else MOBA_HEAD_DIM
    in_specs = [pl.BlockSpec(memory_space=pltpu.SMEM),
                pl.BlockSpec((1, t, w), lambda bi, i: (bi, i, qc // w)),
                pl.BlockSpec((1, s, w), lambda bi, i: (bi, 0, kc // w)),
                pl.BlockSpec((1, s, w), lambda bi, i: (bi, 0, vc // w)),
                pl.BlockSpec((1, N_PAIRS, 2, t, 2 * t), lambda bi, i: (gidx, 0, 0, 0, 0))]
    scratch = [pltpu.VMEM((N_PAIRS, LANES, 2 * t), BF16),
               pltpu.VMEM((2 * 2 * N_PAIRS, t, t), F32),
               pltpu.VMEM((2 * 2 * N_PAIRS, 1, t), F32),
               pltpu.VMEM((N_PAIRS, 1, 2 * t), F32),
               pltpu.VMEM((N_PAIRS, vw + ONES_ROWS, 2 * t), F32),
               pltpu.VMEM((nq, N_PAIRS * (LANES // vw) * (vw + ONES_ROWS), t), BF16)]
    if kind == "diff":
        lamv, subln_g = extra
        in_specs += [pl.BlockSpec(lamv.shape, lambda bi, i: (0, 0)),
                     pl.BlockSpec(subln_g.shape, lambda bi, i: (0, 0))]
        args = (tab, proj, proj, proj, dtiles, lamv, subln_g)
        body = _diff_kernel
    else:
        nbp = max(32, nq)
        scratch += [pltpu.VMEM((N_PAIRS, nbp, LANES), F32), pltpu.VMEM((N_PAIRS, nbp, 2 * t), F32)]
        args = (tab, proj, proj, proj, dtiles)
        body = _moba_kernel
    return pl.pallas_call(
        body,
        out_shape=jax.ShapeDtypeStruct((b, s, w), BF16),
        grid=(b, nq),
        in_specs=in_specs,
        out_specs=pl.BlockSpec((1, t, w), lambda bi, i: (bi, i, 0)),
        scratch_shapes=scratch,
        compiler_params=pltpu.CompilerParams(
            dimension_semantics=("arbitrary", "arbitrary"),
            vmem_limit_bytes=VMEM_LIMIT_BYTES),
        name=kind + "_attn",
    )(*args)


def _merge_kernel(x_ref, ya_ref, yb_ref, gla_ref, glb_ref, wa_ref, wb_ref, wo_ref,
                  gt_ref, shf_ref, scf_ref, gf_ref, wr_ref, br_ref,
                  x1_ref, h2_ref, ti_ref, tw_ref):
    ts = MERGE_SUB
    subs = [slice(j * ts, (j + 1) * ts) for j in range(x_ref.shape[1] // ts)]
    branch = [(jnp.dot(ya_ref[0, sl], wa_ref[...], preferred_element_type=F32),
               jnp.dot(yb_ref[0, sl], wb_ref[...], preferred_element_type=F32)) for sl in subs]
    mixed = []
    for sl, (a, b) in zip(subs, branch):
        merged = (jax.nn.sigmoid(gla_ref[0, sl].astype(F32)) * a
                  + jax.nn.sigmoid(glb_ref[0, sl].astype(F32)) * b)
        mixed.append(jnp.dot(merged.astype(BF16), wo_ref[...], preferred_element_type=F32))
    lane = lax.broadcasted_iota(I32, (ts, LANES), 1)
    lane_f = lane.astype(F32)
    for j, (sl, mo) in enumerate(zip(subs, mixed)):
        x1 = x_ref[0, sl] + gt_ref[0] * mo
        x1_ref[0, sl] = x1
        h = _rms(x1, gf_ref[...]) * (1.0 + scf_ref[0]) + shf_ref[0]
        _store_token_tiles(h2_ref.at[0, j * ts * ROW_CHUNKS:(j + 1) * ts * ROW_CHUNKS], h)
        logits = jnp.dot(h.astype(BF16), wr_ref[...], preferred_element_type=F32) + br_ref[...]
        g = jnp.where(lane < N_EXPERTS, logits, -jnp.inf)
        vals, idxs = [], []
        for _ in range(TOP_K):
            mx = jnp.max(g, axis=1, keepdims=True)
            first = jnp.min(jnp.where(g == mx, lane_f, float(LANES)), axis=1, keepdims=True)
            vals.append(mx)
            idxs.append(first)
            g = jnp.where(lane_f == first, -jnp.inf, g)
        es = [jnp.exp(v - vals[0]) for v in vals]
        den = es[0] + es[1] + es[2] + es[3]
        ti = jnp.zeros((ts, LANES), F32)
        tw = jnp.zeros((ts, LANES), F32)
        for k in range(TOP_K):
            ti = jnp.where(lane == k, idxs[k], ti)
            tw = jnp.where(lane == k, es[k] / den, tw)
        ti_ref[0, sl] = ti.astype(I32)
        tw_ref[0, sl] = tw


def _merge(x, proj, y_a, y_b, w_br_a, w_br_b, w_out, mod, g_ffn, w_router, b_router):
    b, s, d = x.shape
    tm = min(MERGE_TM, s)
    full = lambda shape: pl.BlockSpec(shape, lambda bi, i: (0,) * len(shape))
    tile = lambda w, c=0: pl.BlockSpec((1, tm, w), lambda bi, i: (bi, i, c))
    modspec = lambda j: pl.BlockSpec((1, 1, d), lambda bi, i: (bi * 6 + j, 0, 0))
    return pl.pallas_call(
        _merge_kernel,
        out_shape=(jax.ShapeDtypeStruct((b, s, d), F32),
                   jax.ShapeDtypeStruct((b, s * ROW_CHUNKS, LANES), F32),
                   jax.ShapeDtypeStruct((b, s, LANES), I32),
                   jax.ShapeDtypeStruct((b, s, LANES), F32)),
        grid=(b, s // tm),
        in_specs=[tile(d), tile(512), tile(512), tile(d, 3), tile(d, 4),
                  full(w_br_a.shape), full(w_br_b.shape), full(w_out.shape),
                  modspec(2), modspec(3), modspec(4), full(g_ffn.shape),
                  full(w_router.shape), full(b_router.shape)],
        out_specs=(tile(d), pl.BlockSpec((1, tm * ROW_CHUNKS, LANES), lambda bi, i: (bi, i, 0)),
                   tile(LANES), tile(LANES)),
        compiler_params=pltpu.CompilerParams(dimension_semantics=("arbitrary", "arbitrary"),
                                             vmem_limit_bytes=VMEM_LIMIT_BYTES),
        name="merge",
    )(x, y_a, y_b, proj, proj, w_br_a, w_br_b, w_out, mod, mod, mod, g_ffn,
      w_router, b_router)


def _dispatch_kernel(padtile_ref, nu_ref, pos_ref, h2_ref, xs_ref, zeros_ref, zsem, sem):
    step = pl.program_id(0)
    rc = ROW_CHUNKS
    n_tok = h2_ref.shape[0] // rc
    pad_rows = zeros_ref.shape[0]
    n_tiles = xs_ref.shape[0] // pad_rows

    def zero_tile(tile):
        start = pl.multiple_of(tile * pad_rows, pad_rows)
        return pltpu.make_async_copy(zeros_ref, xs_ref.at[pl.ds(start, pad_rows)], zsem)

    @pl.when(step == 0)
    def _():
        zeros_ref[...] = jnp.zeros(zeros_ref.shape, F32)
        for e in range(N_EXPERTS):
            @pl.when(padtile_ref[e] >= 0)
            def _():
                zero_tile(padtile_ref[e]).start()
        for e in range(N_EXPERTS):
            @pl.when(padtile_ref[e] >= 0)
            def _():
                zero_tile(padtile_ref[e]).wait()

        def start_tail(tile, carry):
            zero_tile(tile).start()
            return carry

        def wait_tail(tile, carry):
            zero_tile(tile).wait()
            return carry

        lax.fori_loop(nu_ref[0], n_tiles, start_tail, 0)
        lax.fori_loop(nu_ref[0], n_tiles, wait_tail, 0)

    def issue(j, carry):
        src = h2_ref.at[pl.ds(pl.multiple_of(j * rc, rc), rc)]
        for k in range(TOP_K):
            dst = xs_ref.at[pl.ds(pl.multiple_of(pos_ref[j * TOP_K + k], rc), rc)]
            pltpu.make_async_copy(src, dst, sem).start(priority=k % 2)
        return carry

    lax.fori_loop(0, n_tok, issue, 0, unroll=ROW_DMA_UNROLL)
    for k in range(TOP_K):
        pltpu.make_async_copy(h2_ref, xs_ref.at[pl.ds(0, n_tok * rc)], sem).wait()


def _dispatch(h2_tiles, pos_start, padtile, n_used, n_rows):
    rc = ROW_CHUNKS
    t_tok = h2_tiles.shape[0] // rc
    tm = min(DISPATCH_TM, t_tok)
    return pl.pallas_call(
        _dispatch_kernel,
        out_shape=jax.ShapeDtypeStruct((n_rows * rc, LANES), F32),
        grid_spec=pltpu.PrefetchScalarGridSpec(
            num_scalar_prefetch=2,
            grid=(t_tok // tm,),
            in_specs=[pl.BlockSpec((tm * TOP_K,), lambda s, pt, nu: (s,),
                                   memory_space=pltpu.SMEM),
                      pl.BlockSpec((tm * rc, LANES), lambda s, pt, nu: (s, 0))],
            out_specs=pl.BlockSpec(memory_space=pl.ANY),
            scratch_shapes=[pltpu.VMEM((MOE_TM * rc, LANES), F32),
                            pltpu.SemaphoreType.DMA, pltpu.SemaphoreType.DMA]),
        compiler_params=pltpu.CompilerParams(dimension_semantics=("arbitrary",),
                                             vmem_limit_bytes=VMEM_LIMIT_BYTES),
        name="dispatch",
    )(padtile, n_used, pos_start, h2_tiles)


def _moe_kernel(te_ref, nu_ref, xs_ref, wgu_ref, bgu_ref, wd_ref, bd_ref,
                ys_ref, wgu_bf, wd_bf, gu_ref):
    r = pl.program_id(0)
    busy = r < nu_ref[0]
    fresh = jnp.logical_or(r == 0, te_ref[r] != te_ref[jnp.maximum(r - 1, 0)])

    @pl.when(jnp.logical_and(busy, fresh))
    def _():
        wgu_bf[...] = wgu_ref[0].astype(BF16)
        wd_bf[...] = wd_ref[0].astype(BF16)

    @pl.when(jnp.logical_not(busy))
    def _():
        ys_ref[...] = jnp.zeros(ys_ref.shape, F32)

    @pl.when(busy)
    def _():
        x = _load_token_tiles(xs_ref).astype(BF16)
        fc = MOE_FF_CHUNK
        for c in range(2 * D_FF // fc):
            lo, hi = c * fc, (c + 1) * fc
            gu_ref[:, lo:hi] = (jnp.dot(x, wgu_bf[:, lo:hi], preferred_element_type=F32)
                                + bgu_ref[0][:, lo:hi])
        y = None
        for c in range(D_FF // fc):
            lo, hi = c * fc, (c + 1) * fc
            gate = jnp.minimum(gu_ref[:, lo:hi], SWIGLU_LIMIT)
            up = jnp.clip(gu_ref[:, D_FF + lo:D_FF + hi], -SWIGLU_LIMIT, SWIGLU_LIMIT)
            act = (up + 1.0) * gate * jax.nn.sigmoid(SWIGLU_ALPHA * gate)
            part = jnp.dot(act.astype(BF16), wd_bf[lo:hi, :], preferred_element_type=F32)
            y = part if y is None else y + part
        _store_token_tiles(ys_ref, y + bd_ref[0])


def _moe(xs_tiles, tile_expert, n_used, w_gate_up, b_gate_up, w_down, b_down):
    d = D_MODEL
    rc = ROW_CHUNKS
    tm = MOE_TM
    n_tiles = tile_expert.shape[0]
    e, _, f2 = w_gate_up.shape
    wsel = lambda r, te, nu: (te[r], 0, 0)
    rows = lambda r, te, nu: (jnp.minimum(r, nu[0] - 1), 0)
    return pl.pallas_call(
        _moe_kernel,
        out_shape=jax.ShapeDtypeStruct((n_tiles * tm * rc, LANES), F32),
        grid_spec=pltpu.PrefetchScalarGridSpec(
            num_scalar_prefetch=2,
            grid=(n_tiles,),
            in_specs=[pl.BlockSpec((tm * rc, LANES), rows),
                      pl.BlockSpec((1, d, f2), wsel),
                      pl.BlockSpec((1, 1, f2), wsel),
                      pl.BlockSpec((1, D_FF, d), wsel),
                      pl.BlockSpec((1, 1, d), wsel)],
            out_specs=pl.BlockSpec((tm * rc, LANES), lambda r, te, nu: (r, 0)),
            scratch_shapes=[pltpu.VMEM((d, f2), BF16), pltpu.VMEM((D_FF, d), BF16),
                            pltpu.VMEM((tm, f2), F32)]),
        compiler_params=pltpu.CompilerParams(dimension_semantics=("arbitrary",),
                                             vmem_limit_bytes=VMEM_LIMIT_BYTES),
        name="moe",
    )(tile_expert, n_used, xs_tiles, w_gate_up, b_gate_up, w_down, b_down)


def _combine_kernel(pos_ref, ys_ref, x1_ref, tw_ref, gt_ref, gf_ref, o_ref, gbuf, sem):
    tm = x1_ref.shape[0]
    rc = ROW_CHUNKS

    def issue(j, carry):
        for k in range(TOP_K):
            src = ys_ref.at[pl.ds(pl.multiple_of(pos_ref[j * TOP_K + k], rc), rc)]
            dst = gbuf.at[k, pl.ds(pl.multiple_of(j * rc, rc), rc)]
            pltpu.make_async_copy(src, dst, sem).start(priority=k % 2)
        return carry

    lax.fori_loop(0, tm, issue, 0, unroll=ROW_DMA_UNROLL)
    for k in range(TOP_K):
        pltpu.make_async_copy(ys_ref.at[pl.ds(0, tm * rc)], gbuf.at[k], sem).wait()

    tw = tw_ref[...]
    acc = tw[:, 0:1] * _load_token_tiles(gbuf.at[0])
    for k in range(1, TOP_K):
        acc = acc + tw[:, k:k + 1] * _load_token_tiles(gbuf.at[k])
    x2 = x1_ref[...] + gt_ref[0] * acc
    o_ref[...] = _rms(x2, gf_ref[...])


def _combine(ys, pos_flat, x1, tw, mod, g_final, seq):
    t, d = x1.shape
    tm = min(COMBINE_TM, t)
    return pl.pallas_call(
        _combine_kernel,
        out_shape=jax.ShapeDtypeStruct((t, d), F32),
        grid=(t // tm,),
        in_specs=[pl.BlockSpec((tm * TOP_K,), lambda s: (s,), memory_space=pltpu.SMEM),
                  pl.BlockSpec(memory_space=pl.ANY),
                  pl.BlockSpec((tm, d), lambda s: (s, 0)),
                  pl.BlockSpec((tm, LANES), lambda s: (s, 0)),
                  pl.BlockSpec((1, 1, d), lambda s: ((s * tm) // seq * 6 + 5, 0, 0)),
                  pl.BlockSpec((1, d), lambda s: (0, 0))],
        out_specs=pl.BlockSpec((tm, d), lambda s: (s, 0)),
        scratch_shapes=[pltpu.VMEM((TOP_K, tm * ROW_CHUNKS, LANES), F32),
                        pltpu.SemaphoreType.DMA],
        compiler_params=pltpu.CompilerParams(dimension_semantics=("arbitrary",),
                                             vmem_limit_bytes=VMEM_LIMIT_BYTES),
        name="combine",
    )(pos_flat, ys, x1, tw, mod, g_final)


def _routing_tables(top_idx, n_tiles):
    tm = MOE_TM
    e_flat = top_idx.reshape(-1)
    onehot = (e_flat[:, None] == jnp.arange(N_EXPERTS, dtype=I32)[None, :]).astype(I32)
    csum = jnp.cumsum(onehot, axis=0)
    counts = csum[-1]
    rank = jnp.sum(csum * onehot, axis=1) - 1
    tiles_per = (counts + tm - 1) // tm
    tile_end = jnp.cumsum(tiles_per)
    tile_start = tile_end - tiles_per
    pos = (jnp.sum(onehot * tile_start[None, :], axis=1) * tm + rank).astype(I32)
    n_used = tile_end[-1:].astype(I32)
    tile_ids = jnp.arange(n_tiles, dtype=I32)
    tile_expert = jnp.minimum(
        jnp.sum((tile_ids[:, None] >= tile_end[None, :]).astype(I32), axis=1),
        N_EXPERTS - 1).astype(I32)
    padtile = jnp.where(counts % tm != 0, tile_end - 1, -1).astype(I32)
    return pos, tile_expert, n_used, padtile


def kernel(x, c, rel_bias, w_ada, b_ada, g_mix, w_in, lambda_q1, lambda_k1, lambda_q2,
           lambda_k2, subln_g, w_br_a, w_br_b, w_out, g_ffn, w_router, b_router,
           w_gate_up, b_gate_up, w_down, b_down, g_final):
    b, s, d = x.shape
    assert d == D_MODEL and s % ATT_T == 0 and MOBA_BLOCK == ATT_T
    t_tok = b * s

    c8 = jnp.zeros((8, d), F32).at[:b].set(c)
    ada = _ada(c8, w_ada[0], b_ada[0].reshape(1, -1))
    mod = ada[:b].reshape(b * 6, 1, d)

    dtiles = _bias_tiles(rel_bias)
    tab = rel_bias.T.reshape(-1)

    proj = _proj(x, g_mix[0].reshape(1, d), mod, w_in[0].astype(BF16))

    lamv = jnp.concatenate([lambda_q1, lambda_k1, lambda_q2, lambda_k2], axis=0)
    y_a = _attention("diff", tab, proj, dtiles, (lamv, subln_g[0].reshape(-1, 1)), (0, 512, 1024))
    y_b = _attention("moba", tab, proj, dtiles, None, (1536, 2048, 2560))

    wr = jnp.zeros((d, LANES), BF16).at[:, :N_EXPERTS].set(w_router[0].astype(BF16))
    br = jnp.zeros((1, LANES), F32).at[0, :N_EXPERTS].set(b_router[0])
    x1, h2, ti, tw = _merge(x, proj, y_a, y_b, w_br_a[0].astype(BF16), w_br_b[0].astype(BF16),
                            w_out[0].astype(BF16), mod, g_ffn[0].reshape(1, d), wr, br)

    n_tiles = (t_tok * TOP_K) // MOE_TM + N_EXPERTS
    top_idx = ti.reshape(t_tok, LANES)[:, :TOP_K]
    pos, tile_expert, n_used, padtile = _routing_tables(top_idx, n_tiles)
    pos_start = pos * ROW_CHUNKS

    xs = _dispatch(h2.reshape(t_tok * ROW_CHUNKS, LANES), pos_start, padtile, n_used,
                   n_tiles * MOE_TM)
    ys = _moe(xs, tile_expert, n_used, w_gate_up[0], b_gate_up[0].reshape(N_EXPERTS, 1, -1),
              w_down[0], b_down[0].reshape(N_EXPERTS, 1, -1))
    out = _combine(ys, pos_start, x1.reshape(t_tok, d), tw.reshape(t_tok, LANES), mod,
                   g_final.reshape(1, d), s)
    return out.reshape(b, s, d)
```

```python
import functools
import math

import jax
import jax.numpy as jnp
import numpy as np
from jax import lax
from jax.experimental import pallas as pl
from jax.experimental.pallas import tpu as pltpu

F32 = jnp.float32
BF16 = jnp.bfloat16
I32 = jnp.int32

D_MODEL = 1024
N_DIFF_HEADS = 4
DIFF_HEAD_DIM = 64
N_MOBA_HEADS = 8
MOBA_HEAD_DIM = 64
MOBA_BLOCK = 256
MOBA_TOPK = 3
N_HEADS_TOTAL = N_DIFF_HEADS + N_MOBA_HEADS
N_BUCKETS = 32
MAX_DISTANCE = 128
N_EXPERTS = 32
TOP_K = 4
D_FF = D_MODEL
SWIGLU_LIMIT = 7.0
SWIGLU_ALPHA = 1.702
RMS_EPS = 1e-5
LAM_INIT = 0.8 - 0.6 * math.exp(-0.3 * 0)
IN_WIDTH = 5120

LANES = 128
VMEM_LIMIT_BYTES = 56 * 1024 * 1024

ATT_T = 256
N_PAIRS = 4
ONES_ROWS = 16
PROJ_TM = 512
MERGE_TM = 512
MERGE_SUB = 256
MOE_TM = 512
MOE_FF_CHUNK = 256
DISPATCH_TM = 1024
ROW_DMA_UNROLL = 8
COMBINE_TM = 1024
NEG = -1e30
LOG2E = math.log2(math.e)


def _t5_bucket_np(dist):
    n = np.maximum(dist, 0)
    max_exact = N_BUCKETS // 2
    nf = np.maximum(n, max_exact).astype(np.float32)
    large = max_exact + (np.log(nf / max_exact) / math.log(MAX_DISTANCE / max_exact)
                         * (N_BUCKETS - max_exact)).astype(np.int32)
    large = np.minimum(large, N_BUCKETS - 1)
    return np.where(n < max_exact, n, large).astype(np.int32)


def _rms(x, g):
    return x * lax.rsqrt(jnp.mean(x * x, axis=-1, keepdims=True) + RMS_EPS) * g


ROW_CHUNKS = D_MODEL // LANES


def _store_token_tiles(ref, x):
    n = x.shape[0]
    for c in range(ROW_CHUNKS):
        ref[pl.ds(c, n, stride=ROW_CHUNKS), :] = x[:, c * LANES:(c + 1) * LANES]


def _load_token_tiles(ref):
    n = ref.shape[0] // ROW_CHUNKS
    return jnp.concatenate([ref[pl.ds(c, n, stride=ROW_CHUNKS), :] for c in range(ROW_CHUNKS)],
                           axis=1)


def _ada_kernel(c_ref, w_ref, b_ref, o_ref):
    o_ref[...] = jnp.dot(c_ref[...].astype(BF16), w_ref[...].astype(BF16),
                         preferred_element_type=F32) + b_ref[...]


def _ada(c8, w, b):
    d, n = w.shape
    tn = 1536
    return pl.pallas_call(
        _ada_kernel,
        out_shape=jax.ShapeDtypeStruct((8, n), F32),
        grid=(n // tn,),
        in_specs=[pl.BlockSpec((8, d), lambda j: (0, 0)),
                  pl.BlockSpec((d, tn), lambda j: (0, j)),
                  pl.BlockSpec((1, tn), lambda j: (0, j))],
        out_specs=pl.BlockSpec((8, tn), lambda j: (0, j)),
        compiler_params=pltpu.CompilerParams(dimension_semantics=("arbitrary",),
                                             vmem_limit_bytes=VMEM_LIMIT_BYTES),
        name="ada",
    )(c8, w, b)


def _bias_kernel(tab_ref, bd_ref, bp_ref, o_ref):
    g = pl.program_id(0)
    p = pl.program_id(1)
    t = pl.program_id(2)
    head = jnp.where(g == 0, p, N_DIFF_HEADS + 2 * p + t)
    bd = bd_ref[...]
    bp = bp_ref[...]
    d0 = jnp.where(bd < 0, NEG, 0.0).astype(F32)
    d1 = jnp.zeros(bp.shape, F32)
    for b in range(N_BUCKETS):
        val = tab_ref[head * N_BUCKETS + b] * LOG2E
        d0 = jnp.where(bd == b, val, d0)
        d1 = jnp.where(bp == b, val, d1)
    o_ref[0, 0, 0] = d0
    o_ref[0, 0, 1] = d1


def _bias_tiles(rel_bias):
    t = ATT_T
    jj = np.arange(t)[:, None]
    ii = np.arange(t)[None, :]
    bd = np.where(ii >= jj, _t5_bucket_np(ii - jj), -1).astype(np.int32)
    bp = _t5_bucket_np(t + ii - jj)
    tab = rel_bias.T.reshape(-1)
    return pl.pallas_call(
        _bias_kernel,
        out_shape=jax.ShapeDtypeStruct((2, N_PAIRS, 2, t, 2 * t), F32),
        grid=(2, N_PAIRS, 2),
        in_specs=[pl.BlockSpec(memory_space=pltpu.SMEM),
                  pl.BlockSpec((t, t), lambda g, p, h: (0, 0)),
                  pl.BlockSpec((t, t), lambda g, p, h: (0, 0))],
        out_specs=pl.BlockSpec((1, 1, 2, t, t), lambda g, p, h: (g, p, 0, 0, h)),
        compiler_params=pltpu.CompilerParams(
            dimension_semantics=("arbitrary", "arbitrary", "arbitrary")),
        name="bias_tiles",
    )(tab, jnp.asarray(bd), jnp.asarray(bp))


def _proj_kernel(x_ref, g_ref, sh_ref, sc_ref, w_ref, o_ref):
    x = x_ref[0]
    h = _rms(x, g_ref[...]) * (1.0 + sc_ref[0]) + sh_ref[0]
    hb = h.astype(BF16)
    n_total = w_ref.shape[1]
    for n0 in range(0, n_total, 512):
        o_ref[0, :, n0:n0 + 512] = jnp.dot(
            hb, w_ref[:, n0:n0 + 512], preferred_element_type=F32).astype(BF16)


def _proj(x, g_mix, mod, w_in_bf):
    b, s, d = x.shape
    n = w_in_bf.shape[1]
    tm = min(PROJ_TM, s)
    return pl.pallas_call(
        _proj_kernel,
        out_shape=jax.ShapeDtypeStruct((b, s, n), BF16),
        grid=(b, s // tm),
        in_specs=[pl.BlockSpec((1, tm, d), lambda bi, i: (bi, i, 0)),
                  pl.BlockSpec((1, d), lambda bi, i: (0, 0)),
                  pl.BlockSpec((1, 1, d), lambda bi, i: (bi * 6 + 0, 0, 0)),
                  pl.BlockSpec((1, 1, d), lambda bi, i: (bi * 6 + 1, 0, 0)),
                  pl.BlockSpec((d, n), lambda bi, i: (0, 0))],
        out_specs=pl.BlockSpec((1, tm, n), lambda bi, i: (bi, i, 0)),
        compiler_params=pltpu.CompilerParams(dimension_semantics=("arbitrary", "arbitrary"),
                                             vmem_limit_bytes=VMEM_LIMIT_BYTES),
        name="proj",
    )(x, g_mix, mod, mod, w_in_bf)


def _stack_q(q_ref, q2_ref, scale):
    t = ATT_T
    row = lax.broadcasted_iota(I32, (LANES, t), 0)
    for p in range(N_PAIRS):
        q = (q_ref[0, :, p * LANES:(p + 1) * LANES].astype(F32) * scale).T
        q2_ref[p, :, 0:t] = jnp.where(row < 64, q, 0.0).astype(BF16)
        q2_ref[p, :, t:2 * t] = jnp.where(row >= 64, q, 0.0).astype(BF16)


def _value_rows(p, h, vw):
    block = (p * (LANES // vw) + (h if vw < LANES else 0)) * (vw + ONES_ROWS)
    return block, vw + ONES_ROWS


def _transpose_values(v_ref, vt_ref, vw):
    t = ATT_T
    nq = vt_ref.shape[0]

    def body(n, carry):
        for p in range(N_PAIRS):
            blk = v_ref[0, pl.ds(pl.multiple_of(n * t, t), t), p * LANES:(p + 1) * LANES]
            blk_t = blk.astype(F32).T.astype(BF16)
            for h in range(LANES // vw):
                r0, _ = _value_rows(p, h, vw)
                vt_ref[n, r0:r0 + vw, :] = blk_t[h * vw:(h + 1) * vw, :]
                vt_ref[n, r0 + vw:r0 + vw + ONES_ROWS, :] = jnp.ones((ONES_ROWS, t), BF16)
        return carry

    lax.fori_loop(0, nq, body, 0)


def _flash_init(m_ref, acc_ref):
    m_ref[...] = jnp.full(m_ref.shape, NEG, F32)
    acc_ref[...] = jnp.zeros(acc_ref.shape, F32)


def _score_slot(p, h, bank):
    return bank * (2 * N_PAIRS) + 2 * p + h


def _flash_scores(p, h, n, bank, q2_ref, k_ref, s_ref, smax_ref):
    t = ATT_T
    kt = k_ref[0, pl.ds(pl.multiple_of(n * t, t), t), p * LANES:(p + 1) * LANES]
    s = jnp.dot(kt, q2_ref[p, :, h * t:(h + 1) * t], preferred_element_type=F32)
    s_ref[_score_slot(p, h, bank)] = s
    smax_ref[_score_slot(p, h, bank)] = jnp.max(s, axis=0, keepdims=True)


def _flash_absorb(p, h, n, bank, tile_bias, row_bias, live, v_ref, s_ref, smax_ref, m_ref,
                  acc_ref):
    t = ATT_T
    cs = slice(h * t, (h + 1) * t)
    r0, nr = _value_rows(p, h, acc_ref.shape[1] - ONES_ROWS)
    vt = v_ref[n, r0:r0 + nr, :]
    s = s_ref[_score_slot(p, h, bank)]
    if tile_bias is not None:
        s = s + tile_bias
        mx = jnp.max(s, axis=0, keepdims=True)
    else:
        mx = smax_ref[_score_slot(p, h, bank)]
    if row_bias is not None:
        mx = mx + row_bias
    if live is not None:
        mx = jnp.where(live, mx, NEG)
    m_prev = m_ref[p, :, cs]
    m_new = jnp.maximum(m_prev, mx)
    alpha = jnp.exp2(m_prev - m_new)
    shift = m_new if row_bias is None else m_new - row_bias
    if live is not None:
        shift = jnp.where(live, shift, -NEG)
    pt = jnp.exp2(s - shift)
    acc_ref[p, :, cs] = alpha * acc_ref[p, :, cs] + jnp.dot(vt, pt.astype(BF16),
                                                            preferred_element_type=F32)
    m_ref[p, :, cs] = m_new


def _far_bias(tab_ref, head):
    return tab_ref[head * N_BUCKETS + (N_BUCKETS - 1)] * LOG2E


def _flash_sweep(i, far_args, prev_args, own_args, refs):
    q2_ref, k_ref, v_ref, s_ref, smax_ref, m_ref, acc_ref = refs
    chains = [(p, h) for p in range(N_PAIRS) for h in range(2)]

    def scores_tile(n, bank):
        for p, h in chains:
            _flash_scores(p, h, n, bank, q2_ref, k_ref, s_ref, smax_ref)

    def step(n, bank, args, with_next):
        for p, h in chains:
            if with_next:
                _flash_scores(p, h, n + 1, 1 - bank, q2_ref, k_ref, s_ref, smax_ref)
            _flash_absorb(p, h, n, bank, *args(p, h), v_ref, s_ref, smax_ref, m_ref, acc_ref)

    def far_step(n, bank):
        step(n, bank, lambda p, h: far_args(p, h, n), True)

    n_far = jnp.maximum(i - 1, 0)
    lead = jnp.bitwise_and(n_far, 1)

    @pl.when(jnp.bitwise_and(i, 1) == 0)
    def _():
        scores_tile(0, 0)

    @pl.when(jnp.bitwise_and(i, 1) == 1)
    def _():
        scores_tile(0, 1)

    @pl.when(lead == 1)
    def _():
        far_step(0, 0)

    def pair_body(j, carry):
        n = lead + 2 * j
        far_step(n, 1)
        far_step(n + 1, 0)
        return carry

    lax.fori_loop(0, jnp.right_shift(n_far, 1), pair_body, 0)

    @pl.when(i >= 1)
    def _():
        step(i - 1, 1, prev_args, True)

    step(i, 0, own_args, False)


def _diff_kernel(tab_ref, q_ref, k_ref, v_ref, d_ref, lamv_ref, g_ref, o_ref,
                 q2_ref, s_ref, smax_ref, m_ref, acc_ref, vt_ref):
    t = ATT_T
    i = pl.program_id(1)

    @pl.when(i == 0)
    def _():
        _transpose_values(v_ref, vt_ref, LANES)

    _stack_q(q_ref, q2_ref, DIFF_HEAD_DIM ** -0.5 * LOG2E)
    _flash_init(m_ref, acc_ref)
    refs = (q2_ref, k_ref, vt_ref, s_ref, smax_ref, m_ref, acc_ref)
    _flash_sweep(i,
                 lambda p, h, n: (None, _far_bias(tab_ref, p), None),
                 lambda p, h: (d_ref[0, p, 1, :, h * t:(h + 1) * t], None, None),
                 lambda p, h: (d_ref[0, p, 0, :, h * t:(h + 1) * t], None, None),
                 refs)

    lv = lamv_ref[...]
    lam = (jnp.exp(jnp.sum(lv[0:1] * lv[1:2], axis=1, keepdims=True))
           - jnp.exp(jnp.sum(lv[2:3] * lv[3:4], axis=1, keepdims=True)) + LAM_INIT)
    for p in range(N_PAIRS):
        acc = acc_ref[p, 0:LANES, :]
        l = acc_ref[p, LANES:LANES + 1, :]
        o = acc[:, 0:t] / l[:, 0:t] - lam * (acc[:, t:2 * t] / l[:, t:2 * t])
        y = o * lax.rsqrt(jnp.mean(o * o, axis=0, keepdims=True) + RMS_EPS) * g_ref[...]
        o_ref[0, :, p * LANES:(p + 1) * LANES] = (y * (1.0 - LAM_INIT)).T.astype(BF16)


def _moba_kernel(tab_ref, q_ref, k_ref, v_ref, d_ref, o_ref,
                 q2_ref, s_ref, smax_ref, m_ref, acc_ref, vt_ref, kmean_ref, sel_ref):
    t = ATT_T
    i = pl.program_id(1)
    nb = k_ref.shape[1] // t
    nbp = kmean_ref.shape[1]

    @pl.when(i == 0)
    def _():
        _transpose_values(v_ref, vt_ref, MOBA_HEAD_DIM)
        kmean_ref[...] = jnp.zeros(kmean_ref.shape, F32)
        for p in range(N_PAIRS):
            for n in range(nb):
                blk = k_ref[0, n * t:(n + 1) * t, p * LANES:(p + 1) * LANES].astype(F32)
                kmean_ref[p, n:n + 1, :] = jnp.sum(blk, axis=0, keepdims=True) * (1.0 / t)

    _stack_q(q_ref, q2_ref, MOBA_HEAD_DIM ** -0.5 * LOG2E)
    _flash_init(m_ref, acc_ref)

    blk_id = lax.broadcasted_iota(I32, (nbp, 2 * t), 0)
    blk_f = blk_id.astype(F32)
    for p in range(N_PAIRS):
        gs = jnp.dot(kmean_ref[p].astype(BF16), q2_ref[p], preferred_element_type=F32)
        g = jnp.where(blk_id < i, gs, -jnp.inf)
        sel = jnp.full((nbp, 2 * t), NEG, F32)
        for _ in range(MOBA_TOPK):
            mx = jnp.max(g, axis=0, keepdims=True)
            first = jnp.min(jnp.where(g == mx, blk_f, float(nbp)), axis=0, keepdims=True)
            hit = blk_f == first
            sel = jnp.where(hit, 0.0, sel)
            g = jnp.where(hit, -jnp.inf, g)
        sel_ref[p] = sel

    def live(p, h, n):
        return sel_ref[p, pl.ds(n, 1), h * t:(h + 1) * t] > -1.0

    refs = (q2_ref, k_ref, vt_ref, s_ref, smax_ref, m_ref, acc_ref)
    _flash_sweep(i,
                 lambda p, h, n: (None, _far_bias(tab_ref, N_DIFF_HEADS + 2 * p + h),
                                  live(p, h, n)),
                 lambda p, h: (d_ref[0, p, 1, :, h * t:(h + 1) * t], None, live(p, h, i - 1)),
                 lambda p, h: (d_ref[0, p, 0, :, h * t:(h + 1) * t], None, None),
                 refs)

    hd = MOBA_HEAD_DIM
    for p in range(N_PAIRS):
        acc = acc_ref[p, 0:hd, :]
        l = acc_ref[p, hd:hd + 1, :]
        o = jnp.concatenate([acc[:, 0:t] / l[:, 0:t], acc[:, t:2 * t] / l[:, t:2 * t]], axis=0)
        o_ref[0, :, p * LANES:(p + 1) * LANES] = o.T.astype(BF16)


def _attention(kind, tab, proj, dtiles, extra, cols):
    b, s, _ = proj.shape
    t = ATT_T
    nq = s // t
    w = N_PAIRS * LANES
    qc, kc, vc = cols
    gidx = 0 if kind == "diff" else 1
    vw = LANES if kind == "diff" else MOBA_HEAD_DIM
    in_specs = [pl.BlockSpec(memory_space=pltpu.SMEM),
                pl.BlockSpec((1, t, w), lambda bi, i: (bi, i, qc // w)),
                pl.BlockSpec((1, s, w), lambda bi, i: (bi, 0, kc // w)),
                pl.BlockSpec((1, s, w), lambda bi, i: (bi, 0, vc // w)),
                pl.BlockSpec((1, N_PAIRS, 2, t, 2 * t), lambda bi, i: (gidx, 0, 0, 0, 0))]
    scratch = [pltpu.VMEM((N_PAIRS, LANES, 2 * t), BF16),
               pltpu.VMEM((2 * 2 * N_PAIRS, t, t), F32),
               pltpu.VMEM((2 * 2 * N_PAIRS, 1, t), F32),
               pltpu.VMEM((N_PAIRS, 1, 2 * t), F32),
               pltpu.VMEM((N_PAIRS, vw + ONES_ROWS, 2 * t), F32),
               pltpu.VMEM((nq, N_PAIRS * (LANES // vw) * (vw + ONES_ROWS), t), BF16)]
    if kind == "diff":
        lamv, subln_g = extra
        in_specs += [pl.BlockSpec(lamv.shape, lambda bi, i: (0, 0)),
                     pl.BlockSpec(subln_g.shape, lambda bi, i: (0, 0))]
        args = (tab, proj, proj, proj, dtiles, lamv, subln_g)
        body = _diff_kernel
    else:
        nbp = max(32, nq)
        scratch += [pltpu.VMEM((N_PAIRS, nbp, LANES), F32), pltpu.VMEM((N_PAIRS, nbp, 2 * t), F32)]
        args = (tab, proj, proj, proj, dtiles)
        body = _moba_kernel
    return pl.pallas_call(
        body,
        out_shape=jax.ShapeDtypeStruct((b, s, w), BF16),
        grid=(b, nq),
        in_specs=in_specs,
        out_specs=pl.BlockSpec((1, t, w), lambda bi, i: (bi, i, 0)),
        scratch_shapes=scratch,
        compiler_params=pltpu.CompilerParams(
            dimension_semantics=("arbitrary", "arbitrary"),
            vmem_limit_bytes=VMEM_LIMIT_BYTES),
        name=kind + "_attn",
    )(*args)


def _merge_kernel(x_ref, ya_ref, yb_ref, gla_ref, glb_ref, wa_ref, wb_ref, wo_ref,
                  gt_ref, shf_ref, scf_ref, gf_ref, wr_ref, br_ref,
                  x1_ref, h2_ref, ti_ref, tw_ref):
    ts = MERGE_SUB
    subs = [slice(j * ts, (j + 1) * ts) for j in range(x_ref.shape[1] // ts)]
    branch = [(jnp.dot(ya_ref[0, sl], wa_ref[...], preferred_element_type=F32),
               jnp.dot(yb_ref[0, sl], wb_ref[...], preferred_element_type=F32)) for sl in subs]
    mixed = []
    for sl, (a, b) in zip(subs, branch):
        merged = (jax.nn.sigmoid(gla_ref[0, sl].astype(F32)) * a
                  + jax.nn.sigmoid(glb_ref[0, sl].astype(F32)) * b)
        mixed.append(jnp.dot(merged.astype(BF16), wo_ref[...], preferred_element_type=F32))
    lane = lax.broadcasted_iota(I32, (ts, LANES), 1)
    lane_f = lane.astype(F32)
    for j, (sl, mo) in enumerate(zip(subs, mixed)):
        x1 = x_ref[0, sl] + gt_ref[0] * mo
        x1_ref[0, sl] = x1
        h = _rms(x1, gf_ref[...]) * (1.0 + scf_ref[0]) + shf_ref[0]
        _store_token_tiles(h2_ref.at[0, j * ts * ROW_CHUNKS:(j + 1) * ts * ROW_CHUNKS], h)
        logits = jnp.dot(h.astype(BF16), wr_ref[...], preferred_element_type=F32) + br_ref[...]
        g = jnp.where(lane < N_EXPERTS, logits, -jnp.inf)
        vals, idxs = [], []
        for _ in range(TOP_K):
            mx = jnp.max(g, axis=1, keepdims=True)
            first = jnp.min(jnp.where(g == mx, lane_f, float(LANES)), axis=1, keepdims=True)
            vals.append(mx)
            idxs.append(first)
            g = jnp.where(lane_f == first, -jnp.inf, g)
        es = [jnp.exp(v - vals[0]) for v in vals]
        den = es[0] + es[1] + es[2] + es[3]
        ti = jnp.zeros((ts, LANES), F32)
        tw = jnp.zeros((ts, LANES), F32)
        for k in range(TOP_K):
            ti = jnp.where(lane == k, idxs[k], ti)
            tw = jnp.where(lane == k, es[k] / den, tw)
        ti_ref[0, sl] = ti.astype(I32)
        tw_ref[0, sl] = tw


def _merge(x, proj, y_a, y_b, w_br_a, w_br_b, w_out, mod, g_ffn, w_router, b_router):
    b, s, d = x.shape
    tm = min(MERGE_TM, s)
    full = lambda shape: pl.BlockSpec(shape, lambda bi, i: (0,) * len(shape))
    tile = lambda w, c=0: pl.BlockSpec((1, tm, w), lambda bi, i: (bi, i, c))
    modspec = lambda j: pl.BlockSpec((1, 1, d), lambda bi, i: (bi * 6 + j, 0, 0))
    return pl.pallas_call(
        _merge_kernel,
        out_shape=(jax.ShapeDtypeStruct((b, s, d), F32),
                   jax.ShapeDtypeStruct((b, s * ROW_CHUNKS, LANES), F32),
                   jax.ShapeDtypeStruct((b, s, LANES), I32),
                   jax.ShapeDtypeStruct((b, s, LANES), F32)),
        grid=(b, s // tm),
        in_specs=[tile(d), tile(512), tile(512), tile(d, 3), tile(d, 4),
                  full(w_br_a.shape), full(w_br_b.shape), full(w_out.shape),
                  modspec(2), modspec(3), modspec(4), full(g_ffn.shape),
                  full(w_router.shape), full(b_router.shape)],
        out_specs=(tile(d), pl.BlockSpec((1, tm * ROW_CHUNKS, LANES), lambda bi, i: (bi, i, 0)),
                   tile(LANES), tile(LANES)),
        compiler_params=pltpu.CompilerParams(dimension_semantics=("arbitrary", "arbitrary"),
                                             vmem_limit_bytes=VMEM_LIMIT_BYTES),
        name="merge",
    )(x, y_a, y_b, proj, proj, w_br_a, w_br_b, w_out, mod, mod, mod, g_ffn,
      w_router, b_router)


def _dispatch_kernel(padtile_ref, nu_ref, pos_ref, h2_ref, xs_ref, zeros_ref, zsem, sem):
    step = pl.program_id(0)
    rc = ROW_CHUNKS
    n_tok = h2_ref.shape[0] // rc
    pad_rows = zeros_ref.shape[0]
    n_tiles = xs_ref.shape[0] // pad_rows

    def zero_tile(tile):
        start = pl.multiple_of(tile * pad_rows, pad_rows)
        return pltpu.make_async_copy(zeros_ref, xs_ref.at[pl.ds(start, pad_rows)], zsem)

    @pl.when(step == 0)
    def _():
        zeros_ref[...] = jnp.zeros(zeros_ref.shape, F32)
        for e in range(N_EXPERTS):
            @pl.when(padtile_ref[e] >= 0)
            def _():
                zero_tile(padtile_ref[e]).start()
        for e in range(N_EXPERTS):
            @pl.when(padtile_ref[e] >= 0)
            def _():
                zero_tile(padtile_ref[e]).wait()

        def start_tail(tile, carry):
            zero_tile(tile).start()
            return carry

        def wait_tail(tile, carry):
            zero_tile(tile).wait()
            return carry

        lax.fori_loop(nu_ref[0], n_tiles, start_tail, 0)
        lax.fori_loop(nu_ref[0], n_tiles, wait_tail, 0)

    def issue(j, carry):
        src = h2_ref.at[pl.ds(pl.multiple_of(j * rc, rc), rc)]
        for k in range(TOP_K):
            dst = xs_ref.at[pl.ds(pl.multiple_of(pos_ref[j * TOP_K + k], rc), rc)]
            pltpu.make_async_copy(src, dst, sem).start(priority=k % 2)
        return carry

    lax.fori_loop(0, n_tok, issue, 0, unroll=ROW_DMA_UNROLL)
    for k in range(TOP_K):
        pltpu.make_async_copy(h2_ref, xs_ref.at[pl.ds(0, n_tok * rc)], sem).wait()


def _dispatch(h2_tiles, pos_start, padtile, n_used, n_rows):
    rc = ROW_CHUNKS
    t_tok = h2_tiles.shape[0] // rc
    tm = min(DISPATCH_TM, t_tok)
    return pl.pallas_call(
        _dispatch_kernel,
        out_shape=jax.ShapeDtypeStruct((n_rows * rc, LANES), F32),
        grid_spec=pltpu.PrefetchScalarGridSpec(
            num_scalar_prefetch=2,
            grid=(t_tok // tm,),
            in_specs=[pl.BlockSpec((tm * TOP_K,), lambda s, pt, nu: (s,),
                                   memory_space=pltpu.SMEM),
                      pl.BlockSpec((tm * rc, LANES), lambda s, pt, nu: (s, 0))],
            out_specs=pl.BlockSpec(memory_space=pl.ANY),
            scratch_shapes=[pltpu.VMEM((MOE_TM * rc, LANES), F32),
                            pltpu.SemaphoreType.DMA, pltpu.SemaphoreType.DMA]),
        compiler_params=pltpu.CompilerParams(dimension_semantics=("arbitrary",),
                                             vmem_limit_bytes=VMEM_LIMIT_BYTES),
        name="dispatch",
    )(padtile, n_used, pos_start, h2_tiles)


def _moe_kernel(te_ref, nu_ref, half_ref, xs_ref, wgu_ref, bgu_ref, wd_ref, bd_ref,
                ys_ref, wgu_bf, wd_bf, gu_ref):
    r = pl.program_id(0)
    busy = r < nu_ref[0]
    half = half_ref[r] == 1
    fresh = jnp.logical_or(r == 0, te_ref[r] != te_ref[jnp.maximum(r - 1, 0)])

    @pl.when(jnp.logical_and(busy, fresh))
    def _():
        wgu_bf[...] = wgu_ref[0].astype(BF16)
        wd_bf[...] = wd_ref[0].astype(BF16)

    @pl.when(jnp.logical_not(busy))
    def _():
        ys_ref[...] = jnp.zeros(ys_ref.shape, F32)

    def expert_rows(m):
        rc = ROW_CHUNKS
        x = _load_token_tiles(xs_ref.at[0:m * rc]).astype(BF16)
        fc = MOE_FF_CHUNK
        for c in range(2 * D_FF // fc):
            lo, hi = c * fc, (c + 1) * fc
            gu_ref[0:m, lo:hi] = (jnp.dot(x, wgu_bf[:, lo:hi], preferred_element_type=F32)
                                  + bgu_ref[0][:, lo:hi])
        y = None
        for c in range(D_FF // fc):
            lo, hi = c * fc, (c + 1) * fc
            gate = jnp.minimum(gu_ref[0:m, lo:hi], SWIGLU_LIMIT)
            up = jnp.clip(gu_ref[0:m, D_FF + lo:D_FF + hi], -SWIGLU_LIMIT, SWIGLU_LIMIT)
            act = (up + 1.0) * gate * jax.nn.sigmoid(SWIGLU_ALPHA * gate)
            part = jnp.dot(act.astype(BF16), wd_bf[lo:hi, :], preferred_element_type=F32)
            y = part if y is None else y + part
        _store_token_tiles(ys_ref.at[0:m * rc], y + bd_ref[0])
        if m * rc < ys_ref.shape[0]:
            ys_ref[m * rc:, :] = jnp.zeros((ys_ref.shape[0] - m * rc, LANES), F32)

    tm = gu_ref.shape[0]

    @pl.when(jnp.logical_and(busy, jnp.logical_not(half)))
    def _():
        expert_rows(tm)

    @pl.when(jnp.logical_and(busy, half))
    def _():
        expert_rows(tm // 2)


def _moe(xs_tiles, tile_expert, n_used, half_tile, w_gate_up, b_gate_up, w_down, b_down):
    d = D_MODEL
    rc = ROW_CHUNKS
    tm = MOE_TM
    n_tiles = tile_expert.shape[0]
    e, _, f2 = w_gate_up.shape
    wsel = lambda r, te, nu, hf: (te[r], 0, 0)
    rows = lambda r, te, nu, hf: (jnp.minimum(r, nu[0] - 1), 0)
    return pl.pallas_call(
        _moe_kernel,
        out_shape=jax.ShapeDtypeStruct((n_tiles * tm * rc, LANES), F32),
        grid_spec=pltpu.PrefetchScalarGridSpec(
            num_scalar_prefetch=3,
            grid=(n_tiles,),
            in_specs=[pl.BlockSpec((tm * rc, LANES), rows),
                      pl.BlockSpec((1, d, f2), wsel),
                      pl.BlockSpec((1, 1, f2), wsel),
                      pl.BlockSpec((1, D_FF, d), wsel),
                      pl.BlockSpec((1, 1, d), wsel)],
            out_specs=pl.BlockSpec((tm * rc, LANES), lambda r, te, nu, hf: (r, 0)),
            scratch_shapes=[pltpu.VMEM((d, f2), BF16), pltpu.VMEM((D_FF, d), BF16),
                            pltpu.VMEM((tm, f2), F32)]),
        compiler_params=pltpu.CompilerParams(dimension_semantics=("arbitrary",),
                                             vmem_limit_bytes=VMEM_LIMIT_BYTES),
        name="moe",
    )(tile_expert, n_used, half_tile, xs_tiles, w_gate_up, b_gate_up, w_down, b_down)


def _combine_kernel(pos_ref, ys_ref, x1_ref, tw_ref, gt_ref, gf_ref, o_ref, gbuf, sem):
    tm = x1_ref.shape[0]
    rc = ROW_CHUNKS

    def issue(j, carry):
        for k in range(TOP_K):
            src = ys_ref.at[pl.ds(pl.multiple_of(pos_ref[j * TOP_K + k], rc), rc)]
            dst = gbuf.at[k, pl.ds(pl.multiple_of(j * rc, rc), rc)]
            pltpu.make_async_copy(src, dst, sem).start(priority=k % 2)
        return carry

    lax.fori_loop(0, tm, issue, 0, unroll=ROW_DMA_UNROLL)
    for k in range(TOP_K):
        pltpu.make_async_copy(ys_ref.at[pl.ds(0, tm * rc)], gbuf.at[k], sem).wait()

    tw = tw_ref[...]
    acc = tw[:, 0:1] * _load_token_tiles(gbuf.at[0])
    for k in range(1, TOP_K):
        acc = acc + tw[:, k:k + 1] * _load_token_tiles(gbuf.at[k])
    x2 = x1_ref[...] + gt_ref[0] * acc
    o_ref[...] = _rms(x2, gf_ref[...])


def _combine(ys, pos_flat, x1, tw, mod, g_final, seq):
    t, d = x1.shape
    tm = min(COMBINE_TM, t)
    return pl.pallas_call(
        _combine_kernel,
        out_shape=jax.ShapeDtypeStruct((t, d), F32),
        grid=(t // tm,),
        in_specs=[pl.BlockSpec((tm * TOP_K,), lambda s: (s,), memory_space=pltpu.SMEM),
                  pl.BlockSpec(memory_space=pl.ANY),
                  pl.BlockSpec((tm, d), lambda s: (s, 0)),
                  pl.BlockSpec((tm, LANES), lambda s: (s, 0)),
                  pl.BlockSpec((1, 1, d), lambda s: ((s * tm) // seq * 6 + 5, 0, 0)),
                  pl.BlockSpec((1, d), lambda s: (0, 0))],
        out_specs=pl.BlockSpec((tm, d), lambda s: (s, 0)),
        scratch_shapes=[pltpu.VMEM((TOP_K, tm * ROW_CHUNKS, LANES), F32),
                        pltpu.SemaphoreType.DMA],
        compiler_params=pltpu.CompilerParams(dimension_semantics=("arbitrary",),
                                             vmem_limit_bytes=VMEM_LIMIT_BYTES),
        name="combine",
    )(pos_flat, ys, x1, tw, mod, g_final)


def _routing_tables(top_idx, n_tiles):
    tm = MOE_TM
    e_flat = top_idx.reshape(-1)
    onehot = (e_flat[:, None] == jnp.arange(N_EXPERTS, dtype=I32)[None, :]).astype(I32)
    csum = jnp.cumsum(onehot, axis=0)
    counts = csum[-1]
    rank = jnp.sum(csum * onehot, axis=1) - 1
    tiles_per = (counts + tm - 1) // tm
    tile_end = jnp.cumsum(tiles_per)
    tile_start = tile_end - tiles_per
    pos = (jnp.sum(onehot * tile_start[None, :], axis=1) * tm + rank).astype(I32)
    n_used = tile_end[-1:].astype(I32)
    tile_ids = jnp.arange(n_tiles, dtype=I32)
    tile_expert = jnp.minimum(
        jnp.sum((tile_ids[:, None] >= tile_end[None, :]).astype(I32), axis=1),
        N_EXPERTS - 1).astype(I32)
    padtile = jnp.where(counts % tm != 0, tile_end - 1, -1).astype(I32)
    rem = counts % tm
    half_e = jnp.where(jnp.logical_and(rem > 0, rem <= tm // 2), tile_end - 1, -1)
    half_tile = jnp.sum((tile_ids[:, None] == half_e[None, :]).astype(I32), axis=1).astype(I32)
    return pos, tile_expert, n_used, padtile, half_tile


def kernel(x, c, rel_bias, w_ada, b_ada, g_mix, w_in, lambda_q1, lambda_k1, lambda_q2,
           lambda_k2, subln_g, w_br_a, w_br_b, w_out, g_ffn, w_router, b_router,
           w_gate_up, b_gate_up, w_down, b_down, g_final):
    b, s, d = x.shape
    assert d == D_MODEL and s % ATT_T == 0 and MOBA_BLOCK == ATT_T
    t_tok = b * s

    c8 = jnp.zeros((8, d), F32).at[:b].set(c)
    ada = _ada(c8, w_ada[0], b_ada[0].reshape(1, -1))
    mod = ada[:b].reshape(b * 6, 1, d)

    dtiles = _bias_tiles(rel_bias)
    tab = rel_bias.T.reshape(-1)

    proj = _proj(x, g_mix[0].reshape(1, d), mod, w_in[0].astype(BF16))

    lamv = jnp.concatenate([lambda_q1, lambda_k1, lambda_q2, lambda_k2], axis=0)
    y_a = _attention("diff", tab, proj, dtiles, (lamv, subln_g[0].reshape(-1, 1)), (0, 512, 1024))
    y_b = _attention("moba", tab, proj, dtiles, None, (1536, 2048, 2560))

    wr = jnp.zeros((d, LANES), BF16).at[:, :N_EXPERTS].set(w_router[0].astype(BF16))
    br = jnp.zeros((1, LANES), F32).at[0, :N_EXPERTS].set(b_router[0])
    x1, h2, ti, tw = _merge(x, proj, y_a, y_b, w_br_a[0].astype(BF16), w_br_b[0].astype(BF16),
                            w_out[0].astype(BF16), mod, g_ffn[0].reshape(1, d), wr, br)

    n_tiles = (t_tok * TOP_K) // MOE_TM + N_EXPERTS
    top_idx = ti.reshape(t_tok, LANES)[:, :TOP_K]
    pos, tile_expert, n_used, padtile, half_tile = _routing_tables(top_idx, n_tiles)
    pos_start = pos * ROW_CHUNKS

    xs = _dispatch(h2.reshape(t_tok * ROW_CHUNKS, LANES), pos_start, padtile, n_used,
                   n_tiles * MOE_TM)
    ys = _moe(xs, tile_expert, n_used, half_tile, w_gate_up[0],
              b_gate_up[0].reshape(N_EXPERTS, 1, -1),
              w_down[0], b_down[0].reshape(N_EXPERTS, 1, -1))
    out = _combine(ys, pos_start, x1.reshape(t_tok, d), tw.reshape(t_tok, LANES), mod,
                   g_final.reshape(1, d), s)
    return out.reshape(b, s, d)
```

```python
import functools
import math

import jax
import jax.numpy as jnp
import numpy as np
from jax import lax
from jax.experimental import pallas as pl
from jax.experimental.pallas import tpu as pltpu

F32 = jnp.float32
BF16 = jnp.bfloat16
I32 = jnp.int32

D_MODEL = 1024
N_DIFF_HEADS = 4
DIFF_HEAD_DIM = 64
N_MOBA_HEADS = 8
MOBA_HEAD_DIM = 64
MOBA_BLOCK = 256
MOBA_TOPK = 3
N_HEADS_TOTAL = N_DIFF_HEADS + N_MOBA_HEADS
N_BUCKETS = 32
MAX_DISTANCE = 128
N_EXPERTS = 32
TOP_K = 4
D_FF = D_MODEL
SWIGLU_LIMIT = 7.0
SWIGLU_ALPHA = 1.702
RMS_EPS = 1e-5
LAM_INIT = 0.8 - 0.6 * math.exp(-0.3 * 0)
IN_WIDTH = 5120

LANES = 128
VMEM_LIMIT_BYTES = 56 * 1024 * 1024

ATT_T = 256
N_PAIRS = 4
ONES_ROWS = 16
PROJ_TM = 512
MERGE_TM = 512
MERGE_SUB = 256
MOE_TM = 512
MOE_FF_CHUNK = 256
DISPATCH_TM = 1024
ROW_DMA_UNROLL = 8
COMBINE_TM = 1024
NEG = -1e30
LOG2E = math.log2(math.e)


def _t5_bucket_np(dist):
    n = np.maximum(dist, 0)
    max_exact = N_BUCKETS // 2
    nf = np.maximum(n, max_exact).astype(np.float32)
    large = max_exact + (np.log(nf / max_exact) / math.log(MAX_DISTANCE / max_exact)
                         * (N_BUCKETS - max_exact)).astype(np.int32)
    large = np.minimum(large, N_BUCKETS - 1)
    return np.where(n < max_exact, n, large).astype(np.int32)


def _rms(x, g):
    return x * lax.rsqrt(jnp.mean(x * x, axis=-1, keepdims=True) + RMS_EPS) * g


ROW_CHUNKS = D_MODEL // LANES


def _store_token_tiles(ref, x):
    n = x.shape[0]
    for c in range(ROW_CHUNKS):
        ref[pl.ds(c, n, stride=ROW_CHUNKS), :] = x[:, c * LANES:(c + 1) * LANES]


def _load_token_tiles(ref):
    n = ref.shape[0] // ROW_CHUNKS
    return jnp.concatenate([ref[pl.ds(c, n, stride=ROW_CHUNKS), :] for c in range(ROW_CHUNKS)],
                           axis=1)


def _ada_kernel(c_ref, w_ref, b_ref, o_ref):
    o_ref[...] = jnp.dot(c_ref[...].astype(BF16), w_ref[...].astype(BF16),
                         preferred_element_type=F32) + b_ref[...]


def _ada(c8, w, b):
    d, n = w.shape
    tn = 1536
    return pl.pallas_call(
        _ada_kernel,
        out_shape=jax.ShapeDtypeStruct((8, n), F32),
        grid=(n // tn,),
        in_specs=[pl.BlockSpec((8, d), lambda j: (0, 0)),
                  pl.BlockSpec((d, tn), lambda j: (0, j)),
                  pl.BlockSpec((1, tn), lambda j: (0, j))],
        out_specs=pl.BlockSpec((8, tn), lambda j: (0, j)),
        compiler_params=pltpu.CompilerParams(dimension_semantics=("arbitrary",),
                                             vmem_limit_bytes=VMEM_LIMIT_BYTES),
        name="ada",
    )(c8, w, b)


def _bias_kernel(tab_ref, bd_ref, bp_ref, o_ref):
    g = pl.program_id(0)
    p = pl.program_id(1)
    t = pl.program_id(2)
    head = jnp.where(g == 0, p, N_DIFF_HEADS + 2 * p + t)
    bd = bd_ref[...]
    bp = bp_ref[...]
    d0 = jnp.where(bd < 0, NEG, 0.0).astype(F32)
    d1 = jnp.zeros(bp.shape, F32)
    for b in range(N_BUCKETS):
        val = tab_ref[head * N_BUCKETS + b] * LOG2E
        d0 = jnp.where(bd == b, val, d0)
        d1 = jnp.where(bp == b, val, d1)
    o_ref[0, 0, 0] = d0
    o_ref[0, 0, 1] = d1


def _bias_tiles(rel_bias):
    t = ATT_T
    jj = np.arange(t)[:, None]
    ii = np.arange(t)[None, :]
    bd = np.where(ii >= jj, _t5_bucket_np(ii - jj), -1).astype(np.int32)
    bp = _t5_bucket_np(t + ii - jj)
    tab = rel_bias.T.reshape(-1)
    return pl.pallas_call(
        _bias_kernel,
        out_shape=jax.ShapeDtypeStruct((2, N_PAIRS, 2, t, 2 * t), F32),
        grid=(2, N_PAIRS, 2),
        in_specs=[pl.BlockSpec(memory_space=pltpu.SMEM),
                  pl.BlockSpec((t, t), lambda g, p, h: (0, 0)),
                  pl.BlockSpec((t, t), lambda g, p, h: (0, 0))],
        out_specs=pl.BlockSpec((1, 1, 2, t, t), lambda g, p, h: (g, p, 0, 0, h)),
        compiler_params=pltpu.CompilerParams(
            dimension_semantics=("arbitrary", "arbitrary", "arbitrary")),
        name="bias_tiles",
    )(tab, jnp.asarray(bd), jnp.asarray(bp))


def _proj_kernel(x_ref, g_ref, sh_ref, sc_ref, w_ref, o_ref):
    x = x_ref[0]
    h = _rms(x, g_ref[...]) * (1.0 + sc_ref[0]) + sh_ref[0]
    hb = h.astype(BF16)
    n_total = w_ref.shape[1]
    for n0 in range(0, n_total, 512):
        o_ref[0, :, n0:n0 + 512] = jnp.dot(
            hb, w_ref[:, n0:n0 + 512], preferred_element_type=F32).astype(BF16)


def _proj(x, g_mix, mod, w_in_bf):
    b, s, d = x.shape
    n = w_in_bf.shape[1]
    tm = min(PROJ_TM, s)
    return pl.pallas_call(
        _proj_kernel,
        out_shape=jax.ShapeDtypeStruct((b, s, n), BF16),
        grid=(b, s // tm),
        in_specs=[pl.BlockSpec((1, tm, d), lambda bi, i: (bi, i, 0)),
                  pl.BlockSpec((1, d), lambda bi, i: (0, 0)),
                  pl.BlockSpec((1, 1, d), lambda bi, i: (bi * 6 + 0, 0, 0)),
                  pl.BlockSpec((1, 1, d), lambda bi, i: (bi * 6 + 1, 0, 0)),
                  pl.BlockSpec((d, n), lambda bi, i: (0, 0))],
        out_specs=pl.BlockSpec((1, tm, n), lambda bi, i: (bi, i, 0)),
        compiler_params=pltpu.CompilerParams(dimension_semantics=("arbitrary", "arbitrary"),
                                             vmem_limit_bytes=VMEM_LIMIT_BYTES),
        name="proj",
    )(x, g_mix, mod, mod, w_in_bf)


def _stack_q(q_ref, q2_ref, scale):
    t = ATT_T
    row = lax.broadcasted_iota(I32, (LANES, t), 0)
    for p in range(N_PAIRS):
        q = (q_ref[0, :, p * LANES:(p + 1) * LANES].astype(F32) * scale).T
        q2_ref[p, :, 0:t] = jnp.where(row < 64, q, 0.0).astype(BF16)
        q2_ref[p, :, t:2 * t] = jnp.where(row >= 64, q, 0.0).astype(BF16)


def _value_rows(p, h, vw):
    block = (p * (LANES // vw) + (h if vw < LANES else 0)) * (vw + ONES_ROWS)
    return block, vw + ONES_ROWS


def _transpose_values(v_ref, vt_ref, vw):
    t = ATT_T
    nq = vt_ref.shape[0]

    def body(n, carry):
        for p in range(N_PAIRS):
            blk = v_ref[0, pl.ds(pl.multiple_of(n * t, t), t), p * LANES:(p + 1) * LANES]
            blk_t = blk.astype(F32).T.astype(BF16)
            for h in range(LANES // vw):
                r0, _ = _value_rows(p, h, vw)
                vt_ref[n, r0:r0 + vw, :] = blk_t[h * vw:(h + 1) * vw, :]
                vt_ref[n, r0 + vw:r0 + vw + ONES_ROWS, :] = jnp.ones((ONES_ROWS, t), BF16)
        return carry

    lax.fori_loop(0, nq, body, 0)


def _flash_init(m_ref, acc_ref):
    m_ref[...] = jnp.full(m_ref.shape, NEG, F32)
    acc_ref[...] = jnp.zeros(acc_ref.shape, F32)


def _score_slot(p, h, bank):
    return bank * (2 * N_PAIRS) + 2 * p + h


def _flash_scores(p, h, n, bank, q2_ref, k_ref, s_ref, smax_ref):
    t = ATT_T
    kt = k_ref[0, pl.ds(pl.multiple_of(n * t, t), t), p * LANES:(p + 1) * LANES]
    s = jnp.dot(kt, q2_ref[p, :, h * t:(h + 1) * t], preferred_element_type=F32)
    s_ref[_score_slot(p, h, bank)] = s
    smax_ref[_score_slot(p, h, bank)] = jnp.max(s, axis=0, keepdims=True)


def _flash_absorb(p, h, n, bank, tile_bias, row_bias, live, v_ref, s_ref, smax_ref, m_ref,
                  acc_ref):
    t = ATT_T
    cs = slice(h * t, (h + 1) * t)
    r0, nr = _value_rows(p, h, acc_ref.shape[1] - ONES_ROWS)
    vt = v_ref[n, r0:r0 + nr, :]
    s = s_ref[_score_slot(p, h, bank)]
    if tile_bias is not None:
        s = s + tile_bias
        mx = jnp.max(s, axis=0, keepdims=True)
    else:
        mx = smax_ref[_score_slot(p, h, bank)]
    if row_bias is not None:
        mx = mx + row_bias
    if live is not None:
        mx = jnp.where(live, mx, NEG)
    m_prev = m_ref[p, :, cs]
    m_new = jnp.maximum(m_prev, mx)
    alpha = jnp.exp2(m_prev - m_new)
    shift = m_new if row_bias is None else m_new - row_bias
    if live is not None:
        shift = jnp.where(live, shift, -NEG)
    pt = jnp.exp2(s - shift)
    acc_ref[p, :, cs] = alpha * acc_ref[p, :, cs] + jnp.dot(vt, pt.astype(BF16),
                                                            preferred_element_type=F32)
    m_ref[p, :, cs] = m_new


def _far_bias(tab_ref, head):
    return tab_ref[head * N_BUCKETS + (N_BUCKETS - 1)] * LOG2E


def _flash_sweep(i, far_args, prev_args, own_args, refs):
    q2_ref, k_ref, v_ref, s_ref, smax_ref, m_ref, acc_ref = refs
    chains = [(p, h) for p in range(N_PAIRS) for h in range(2)]

    def scores_tile(n, bank):
        for p, h in chains:
            _flash_scores(p, h, n, bank, q2_ref, k_ref, s_ref, smax_ref)

    def step(n, bank, args, with_next):
        for p, h in chains:
            if with_next:
                _flash_scores(p, h, n + 1, 1 - bank, q2_ref, k_ref, s_ref, smax_ref)
            _flash_absorb(p, h, n, bank, *args(p, h), v_ref, s_ref, smax_ref, m_ref, acc_ref)

    def far_step(n, bank):
        step(n, bank, lambda p, h: far_args(p, h, n), True)

    n_far = jnp.maximum(i - 1, 0)
    lead = jnp.bitwise_and(n_far, 1)

    @pl.when(jnp.bitwise_and(i, 1) == 0)
    def _():
        scores_tile(0, 0)

    @pl.when(jnp.bitwise_and(i, 1) == 1)
    def _():
        scores_tile(0, 1)

    @pl.when(lead == 1)
    def _():
        far_step(0, 0)

    def pair_body(j, carry):
        n = lead + 2 * j
        far_step(n, 1)
        far_step(n + 1, 0)
        return carry

    lax.fori_loop(0, jnp.right_shift(n_far, 1), pair_body, 0)

    @pl.when(i >= 1)
    def _():
        step(i - 1, 1, prev_args, True)

    step(i, 0, own_args, False)


def _diff_kernel(tab_ref, q_ref, k_ref, v_ref, d_ref, lamv_ref, g_ref, o_ref,
                 q2_ref, s_ref, smax_ref, m_ref, acc_ref, vt_ref):
    t = ATT_T
    i = pl.program_id(1)

    @pl.when(i == 0)
    def _():
        _transpose_values(v_ref, vt_ref, LANES)

    _stack_q(q_ref, q2_ref, DIFF_HEAD_DIM ** -0.5 * LOG2E)
    _flash_init(m_ref, acc_ref)
    refs = (q2_ref, k_ref, vt_ref, s_ref, smax_ref, m_ref, acc_ref)
    _flash_sweep(i,
                 lambda p, h, n: (None, _far_bias(tab_ref, p), None),
                 lambda p, h: (d_ref[0, p, 1, :, h * t:(h + 1) * t], None, None),
                 lambda p, h: (d_ref[0, p, 0, :, h * t:(h + 1) * t], None, None),
                 refs)

    lv = lamv_ref[...]
    lam = (jnp.exp(jnp.sum(lv[0:1] * lv[1:2], axis=1, keepdims=True))
           - jnp.exp(jnp.sum(lv[2:3] * lv[3:4], axis=1, keepdims=True)) + LAM_INIT)
    for p in range(N_PAIRS):
        acc = acc_ref[p, 0:LANES, :]
        l = acc_ref[p, LANES:LANES + 1, :]
        o = acc[:, 0:t] / l[:, 0:t] - lam * (acc[:, t:2 * t] / l[:, t:2 * t])
        y = o * lax.rsqrt(jnp.mean(o * o, axis=0, keepdims=True) + RMS_EPS) * g_ref[...]
        o_ref[0, :, p * LANES:(p + 1) * LANES] = (y * (1.0 - LAM_INIT)).T.astype(BF16)


def _moba_kernel(tab_ref, q_ref, k_ref, v_ref, d_ref, o_ref,
                 q2_ref, s_ref, smax_ref, m_ref, acc_ref, vt_ref, kmean_ref, sel_ref):
    t = ATT_T
    i = pl.program_id(1)
    nb = k_ref.shape[1] // t
    nbp = kmean_ref.shape[1]

    @pl.when(i == 0)
    def _():
        _transpose_values(v_ref, vt_ref, MOBA_HEAD_DIM)
        kmean_ref[...] = jnp.zeros(kmean_ref.shape, F32)
        for p in range(N_PAIRS):
            for n in range(nb):
                blk = k_ref[0, n * t:(n + 1) * t, p * LANES:(p + 1) * LANES].astype(F32)
                kmean_ref[p, n:n + 1, :] = jnp.sum(blk, axis=0, keepdims=True) * (1.0 / t)

    _stack_q(q_ref, q2_ref, MOBA_HEAD_DIM ** -0.5 * LOG2E)
    _flash_init(m_ref, acc_ref)

    blk_id = lax.broadcasted_iota(I32, (nbp, 2 * t), 0)
    blk_f = blk_id.astype(F32)
    for p in range(N_PAIRS):
        gs = jnp.dot(kmean_ref[p].astype(BF16), q2_ref[p], preferred_element_type=F32)
        g = jnp.where(blk_id < i, gs, -jnp.inf)
        sel = jnp.full((nbp, 2 * t), NEG, F32)
        for _ in range(MOBA_TOPK):
            mx = jnp.max(g, axis=0, keepdims=True)
            first = jnp.min(jnp.where(g == mx, blk_f, float(nbp)), axis=0, keepdims=True)
            hit = blk_f == first
            sel = jnp.where(hit, 0.0, sel)
            g = jnp.where(hit, -jnp.inf, g)
        sel_ref[p] = sel

    def live(p, h, n):
        return sel_ref[p, pl.ds(n, 1), h * t:(h + 1) * t] > -1.0

    refs = (q2_ref, k_ref, vt_ref, s_ref, smax_ref, m_ref, acc_ref)
    _flash_sweep(i,
                 lambda p, h, n: (None, _far_bias(tab_ref, N_DIFF_HEADS + 2 * p + h),
                                  live(p, h, n)),
                 lambda p, h: (d_ref[0, p, 1, :, h * t:(h + 1) * t], None, live(p, h, i - 1)),
                 lambda p, h: (d_ref[0, p, 0, :, h * t:(h + 1) * t], None, None),
                 refs)

    hd = MOBA_HEAD_DIM
    for p in range(N_PAIRS):
        acc = acc_ref[p, 0:hd, :]
        l = acc_ref[p, hd:hd + 1, :]
        o = jnp.concatenate([acc[:, 0:t] / l[:, 0:t], acc[:, t:2 * t] / l[:, t:2 * t]], axis=0)
        o_ref[0, :, p * LANES:(p + 1) * LANES] = o.T.astype(BF16)


def _attention(kind, tab, proj, dtiles, extra, cols):
    b, s, _ = proj.shape
    t = ATT_T
    nq = s // t
    w = N_PAIRS * LANES
    qc, kc, vc = cols
    gidx = 0 if kind == "diff" else 1
    vw = LANES if kind == "diff" else MOBA_HEAD_DIM
    in_specs = [pl.BlockSpec(memory_space=pltpu.SMEM),
                pl.BlockSpec((1, t, w), lambda bi, i: (bi, i, qc // w)),
                pl.BlockSpec((1, s, w), lambda bi, i: (bi, 0, kc // w)),
                pl.BlockSpec((1, s, w), lambda bi, i: (bi, 0, vc // w)),
                pl.BlockSpec((1, N_PAIRS, 2, t, 2 * t), lambda bi, i: (gidx, 0, 0, 0, 0))]
    scratch = [pltpu.VMEM((N_PAIRS, LANES, 2 * t), BF16),
               pltpu.VMEM((2 * 2 * N_PAIRS, t, t), F32),
               pltpu.VMEM((2 * 2 * N_PAIRS, 1, t), F32),
               pltpu.VMEM((N_PAIRS, 1, 2 * t), F32),
               pltpu.VMEM((N_PAIRS, vw + ONES_ROWS, 2 * t), F32),
               pltpu.VMEM((nq, N_PAIRS * (LANES // vw) * (vw + ONES_ROWS), t), BF16)]
    if kind == "diff":
        lamv, subln_g = extra
        in_specs += [pl.BlockSpec(lamv.shape, lambda bi, i: (0, 0)),
                     pl.BlockSpec(subln_g.shape, lambda bi, i: (0, 0))]
        args = (tab, proj, proj, proj, dtiles, lamv, subln_g)
        body = _diff_kernel
    else:
        nbp = max(32, nq)
        scratch += [pltpu.VMEM((N_PAIRS, nbp, LANES), F32), pltpu.VMEM((N_PAIRS, nbp, 2 * t), F32)]
        args = (tab, proj, proj, proj, dtiles)
        body = _moba_kernel
    return pl.pallas_call(
        body,
        out_shape=jax.ShapeDtypeStruct((b, s, w), BF16),
        grid=(b, nq),
        in_specs=in_specs,
        out_specs=pl.BlockSpec((1, t, w), lambda bi, i: (bi, i, 0)),
        scratch_shapes=scratch,
        compiler_params=pltpu.CompilerParams(
            dimension_semantics=("arbitrary", "arbitrary"),
            vmem_limit_bytes=VMEM_LIMIT_BYTES),
        name=kind + "_attn",
    )(*args)


def _merge_kernel(x_ref, ya_ref, yb_ref, gla_ref, glb_ref, wa_ref, wb_ref, wo_ref,
                  gt_ref, shf_ref, scf_ref, gf_ref, wr_ref, br_ref,
                  x1_ref, h2_ref, ti_ref, tw_ref):
    ts = MERGE_SUB
    subs = [slice(j * ts, (j + 1) * ts) for j in range(x_ref.shape[1] // ts)]
    branch = [(jnp.dot(ya_ref[0, sl], wa_ref[...], preferred_element_type=F32),
               jnp.dot(yb_ref[0, sl], wb_ref[...], preferred_element_type=F32)) for sl in subs]
    mixed = []
    for sl, (a, b) in zip(subs, branch):
        merged = (jax.nn.sigmoid(gla_ref[0, sl].astype(F32)) * a
                  + jax.nn.sigmoid(glb_ref[0, sl].astype(F32)) * b)
        mixed.append(jnp.dot(merged.astype(BF16), wo_ref[...], preferred_element_type=F32))
    lane = lax.broadcasted_iota(I32, (ts, LANES), 1)
    lane_f = lane.astype(F32)
    for j, (sl, mo) in enumerate(zip(subs, mixed)):
        x1 = x_ref[0, sl] + gt_ref[0] * mo
        x1_ref[0, sl] = x1
        h = _rms(x1, gf_ref[...]) * (1.0 + scf_ref[0]) + shf_ref[0]
        _store_token_tiles(h2_ref.at[0, j * ts * ROW_CHUNKS:(j + 1) * ts * ROW_CHUNKS], h)
        logits = jnp.dot(h.astype(BF16), wr_ref[...], preferred_element_type=F32) + br_ref[...]
        g = jnp.where(lane < N_EXPERTS, logits, -jnp.inf)
        vals, idxs = [], []
        for _ in range(TOP_K):
            mx = jnp.max(g, axis=1, keepdims=True)
            first = jnp.min(jnp.where(g == mx, lane_f, float(LANES)), axis=1, keepdims=True)
            vals.append(mx)
            idxs.append(first)
            g = jnp.where(lane_f == first, -jnp.inf, g)
        es = [jnp.exp(v - vals[0]) for v in vals]
        den = es[0] + es[1] + es[2] + es[3]
        ti = jnp.zeros((ts, LANES), F32)
        tw = jnp.zeros((ts, LANES), F32)
        for k in range(TOP_K):
            ti = jnp.where(lane == k, idxs[k], ti)
            tw = jnp.where(lane == k, es[k] / den, tw)
        ti_ref[0, sl] = ti.astype(I32)
        tw_ref[0, sl] = tw


def _merge(x, proj, y_a, y_b, w_br_a, w_br_b, w_out, mod, g_ffn, w_router, b_router):
    b, s, d = x.shape
    tm = min(MERGE_TM, s)
    full = lambda shape: pl.BlockSpec(shape, lambda bi, i: (0,) * len(shape))
    tile = lambda w, c=0: pl.BlockSpec((1, tm, w), lambda bi, i: (bi, i, c))
    modspec = lambda j: pl.BlockSpec((1, 1, d), lambda bi, i: (bi * 6 + j, 0, 0))
    return pl.pallas_call(
        _merge_kernel,
        out_shape=(jax.ShapeDtypeStruct((b, s, d), F32),
                   jax.ShapeDtypeStruct((b, s * ROW_CHUNKS, LANES), F32),
                   jax.ShapeDtypeStruct((b, s, LANES), I32),
                   jax.ShapeDtypeStruct((b, s, LANES), F32)),
        grid=(b, s // tm),
        in_specs=[tile(d), tile(512), tile(512), tile(d, 3), tile(d, 4),
                  full(w_br_a.shape), full(w_br_b.shape), full(w_out.shape),
                  modspec(2), modspec(3), modspec(4), full(g_ffn.shape),
                  full(w_router.shape), full(b_router.shape)],
        out_specs=(tile(d), pl.BlockSpec((1, tm * ROW_CHUNKS, LANES), lambda bi, i: (bi, i, 0)),
                   tile(LANES), tile(LANES)),
        compiler_params=pltpu.CompilerParams(dimension_semantics=("arbitrary", "arbitrary"),
                                             vmem_limit_bytes=VMEM_LIMIT_BYTES),
        name="merge",
    )(x, y_a, y_b, proj, proj, w_br_a, w_br_b, w_out, mod, mod, mod, g_ffn,
      w_router, b_router)


def _dispatch_kernel(padtile_ref, nu_ref, pos_ref, h2_ref, xs_ref, zeros_ref, zsem, sem):
    step = pl.program_id(0)
    rc = ROW_CHUNKS
    n_tok = h2_ref.shape[0] // rc
    pad_rows = zeros_ref.shape[0]
    n_tiles = xs_ref.shape[0] // pad_rows

    def zero_tile(tile):
        start = pl.multiple_of(tile * pad_rows, pad_rows)
        return pltpu.make_async_copy(zeros_ref, xs_ref.at[pl.ds(start, pad_rows)], zsem)

    @pl.when(step == 0)
    def _():
        zeros_ref[...] = jnp.zeros(zeros_ref.shape, F32)
        for e in range(N_EXPERTS):
            @pl.when(padtile_ref[e] >= 0)
            def _():
                zero_tile(padtile_ref[e]).start()
        for e in range(N_EXPERTS):
            @pl.when(padtile_ref[e] >= 0)
            def _():
                zero_tile(padtile_ref[e]).wait()

        def start_tail(tile, carry):
            zero_tile(tile).start()
            return carry

        def wait_tail(tile, carry):
            zero_tile(tile).wait()
            return carry

        lax.fori_loop(nu_ref[0], n_tiles, start_tail, 0)
        lax.fori_loop(nu_ref[0], n_tiles, wait_tail, 0)

    def issue(j, carry):
        src = h2_ref.at[pl.ds(pl.multiple_of(j * rc, rc), rc)]
        for k in range(TOP_K):
            dst = xs_ref.at[pl.ds(pl.multiple_of(pos_ref[j * TOP_K + k], rc), rc)]
            pltpu.make_async_copy(src, dst, sem).start(priority=k % 2)
        return carry

    lax.fori_loop(0, n_tok, issue, 0, unroll=ROW_DMA_UNROLL)
    for k in range(TOP_K):
        pltpu.make_async_copy(h2_ref, xs_ref.at[pl.ds(0, n_tok * rc)], sem).wait()


def _dispatch(h2_tiles, pos_start, padtile, n_used, n_rows):
    rc = ROW_CHUNKS
    t_tok = h2_tiles.shape[0] // rc
    tm = min(DISPATCH_TM, t_tok)
    return pl.pallas_call(
        _dispatch_kernel,
        out_shape=jax.ShapeDtypeStruct((n_rows * rc, LANES), F32),
        grid_spec=pltpu.PrefetchScalarGridSpec(
            num_scalar_prefetch=2,
            grid=(t_tok // tm,),
            in_specs=[pl.BlockSpec((tm * TOP_K,), lambda s, pt, nu: (s,),
                                   memory_space=pltpu.SMEM),
                      pl.BlockSpec((tm * rc, LANES), lambda s, pt, nu: (s, 0))],
            out_specs=pl.BlockSpec(memory_space=pl.ANY),
            scratch_shapes=[pltpu.VMEM((MOE_TM * rc, LANES), F32),
                            pltpu.SemaphoreType.DMA, pltpu.SemaphoreType.DMA]),
        compiler_params=pltpu.CompilerParams(dimension_semantics=("arbitrary",),
                                             vmem_limit_bytes=VMEM_LIMIT_BYTES),
        name="dispatch",
    )(padtile, n_used, pos_start, h2_tiles)


def _moe_kernel(te_ref, nu_ref, slot_ref, next_ref, xs_ref, wgu_hbm, bgu_ref, wd_hbm, bd_ref,
                ys_ref, wgu_bf, wd_bf, gu_ref, wgu_f32, wd_f32, wsem):
    r = pl.program_id(0)
    n_used = nu_ref[0]
    busy = r < n_used
    fresh = jnp.logical_or(r == 0, te_ref[r] != te_ref[jnp.maximum(r - 1, 0)])
    slot = slot_ref[r]

    def weight_copies(expert, s):
        return (pltpu.make_async_copy(wgu_hbm.at[expert], wgu_f32.at[s], wsem.at[0, s]),
                pltpu.make_async_copy(wd_hbm.at[expert], wd_f32.at[s], wsem.at[1, s]))

    @pl.when(r == 0)
    def _():
        for cp in weight_copies(te_ref[0], 0):
            cp.start()

    @pl.when(jnp.logical_and(busy, fresh))
    def _():
        for cp in weight_copies(te_ref[r], slot):
            cp.wait()
        nxt = next_ref[r]

        @pl.when(nxt < n_used)
        def _():
            for cp in weight_copies(te_ref[jnp.minimum(nxt, te_ref.shape[0] - 1)], 1 - slot):
                cp.start()

        wgu_bf[...] = wgu_f32[slot].astype(BF16)
        wd_bf[...] = wd_f32[slot].astype(BF16)

    @pl.when(jnp.logical_not(busy))
    def _():
        ys_ref[...] = jnp.zeros(ys_ref.shape, F32)

    @pl.when(busy)
    def _():
        x = _load_token_tiles(xs_ref).astype(BF16)
        fc = MOE_FF_CHUNK
        for c in range(2 * D_FF // fc):
            lo, hi = c * fc, (c + 1) * fc
            gu_ref[:, lo:hi] = (jnp.dot(x, wgu_bf[:, lo:hi], preferred_element_type=F32)
                                + bgu_ref[0][:, lo:hi])
        y = None
        for c in range(D_FF // fc):
            lo, hi = c * fc, (c + 1) * fc
            gate = jnp.minimum(gu_ref[:, lo:hi], SWIGLU_LIMIT)
            up = jnp.clip(gu_ref[:, D_FF + lo:D_FF + hi], -SWIGLU_LIMIT, SWIGLU_LIMIT)
            act = (up + 1.0) * gate * jax.nn.sigmoid(SWIGLU_ALPHA * gate)
            part = jnp.dot(act.astype(BF16), wd_bf[lo:hi, :], preferred_element_type=F32)
            y = part if y is None else y + part
        _store_token_tiles(ys_ref, y + bd_ref[0])


def _moe(xs_tiles, tile_expert, n_used, group_slot, next_group, w_gate_up, b_gate_up, w_down,
         b_down):
    d = D_MODEL
    rc = ROW_CHUNKS
    tm = MOE_TM
    n_tiles = tile_expert.shape[0]
    e, _, f2 = w_gate_up.shape
    bsel = lambda r, te, nu, gs, ng: (te[r], 0, 0)
    rows = lambda r, te, nu, gs, ng: (jnp.minimum(r, nu[0] - 1), 0)
    return pl.pallas_call(
        _moe_kernel,
        out_shape=jax.ShapeDtypeStruct((n_tiles * tm * rc, LANES), F32),
        grid_spec=pltpu.PrefetchScalarGridSpec(
            num_scalar_prefetch=4,
            grid=(n_tiles,),
            in_specs=[pl.BlockSpec((tm * rc, LANES), rows),
                      pl.BlockSpec(memory_space=pl.ANY),
                      pl.BlockSpec((1, 1, f2), bsel),
                      pl.BlockSpec(memory_space=pl.ANY),
                      pl.BlockSpec((1, 1, d), bsel)],
            out_specs=pl.BlockSpec((tm * rc, LANES), lambda r, te, nu, gs, ng: (r, 0)),
            scratch_shapes=[pltpu.VMEM((d, f2), BF16), pltpu.VMEM((D_FF, d), BF16),
                            pltpu.VMEM((tm, f2), F32),
                            pltpu.VMEM((2, d, f2), F32), pltpu.VMEM((2, D_FF, d), F32),
                            pltpu.SemaphoreType.DMA((2, 2))]),
        compiler_params=pltpu.CompilerParams(dimension_semantics=("arbitrary",),
                                             vmem_limit_bytes=VMEM_LIMIT_BYTES),
        name="moe",
    )(tile_expert, n_used, group_slot, next_group, xs_tiles, w_gate_up, b_gate_up, w_down, b_down)


def _combine_kernel(pos_ref, ys_ref, x1_ref, tw_ref, gt_ref, gf_ref, o_ref, gbuf, sem):
    tm = x1_ref.shape[0]
    rc = ROW_CHUNKS

    def issue(j, carry):
        for k in range(TOP_K):
            src = ys_ref.at[pl.ds(pl.multiple_of(pos_ref[j * TOP_K + k], rc), rc)]
            dst = gbuf.at[k, pl.ds(pl.multiple_of(j * rc, rc), rc)]
            pltpu.make_async_copy(src, dst, sem).start(priority=k % 2)
        return carry

    lax.fori_loop(0, tm, issue, 0, unroll=ROW_DMA_UNROLL)
    for k in range(TOP_K):
        pltpu.make_async_copy(ys_ref.at[pl.ds(0, tm * rc)], gbuf.at[k], sem).wait()

    tw = tw_ref[...]
    acc = tw[:, 0:1] * _load_token_tiles(gbuf.at[0])
    for k in range(1, TOP_K):
        acc = acc + tw[:, k:k + 1] * _load_token_tiles(gbuf.at[k])
    x2 = x1_ref[...] + gt_ref[0] * acc
    o_ref[...] = _rms(x2, gf_ref[...])


def _combine(ys, pos_flat, x1, tw, mod, g_final, seq):
    t, d = x1.shape
    tm = min(COMBINE_TM, t)
    return pl.pallas_call(
        _combine_kernel,
        out_shape=jax.ShapeDtypeStruct((t, d), F32),
        grid=(t // tm,),
        in_specs=[pl.BlockSpec((tm * TOP_K,), lambda s: (s,), memory_space=pltpu.SMEM),
                  pl.BlockSpec(memory_space=pl.ANY),
                  pl.BlockSpec((tm, d), lambda s: (s, 0)),
                  pl.BlockSpec((tm, LANES), lambda s: (s, 0)),
                  pl.BlockSpec((1, 1, d), lambda s: ((s * tm) // seq * 6 + 5, 0, 0)),
                  pl.BlockSpec((1, d), lambda s: (0, 0))],
        out_specs=pl.BlockSpec((tm, d), lambda s: (s, 0)),
        scratch_shapes=[pltpu.VMEM((TOP_K, tm * ROW_CHUNKS, LANES), F32),
                        pltpu.SemaphoreType.DMA],
        compiler_params=pltpu.CompilerParams(dimension_semantics=("arbitrary",),
                                             vmem_limit_bytes=VMEM_LIMIT_BYTES),
        name="combine",
    )(pos_flat, ys, x1, tw, mod, g_final)


def _routing_tables(top_idx, n_tiles):
    tm = MOE_TM
    e_flat = top_idx.reshape(-1)
    onehot = (e_flat[:, None] == jnp.arange(N_EXPERTS, dtype=I32)[None, :]).astype(I32)
    csum = jnp.cumsum(onehot, axis=0)
    counts = csum[-1]
    rank = jnp.sum(csum * onehot, axis=1) - 1
    tiles_per = (counts + tm - 1) // tm
    tile_end = jnp.cumsum(tiles_per)
    tile_start = tile_end - tiles_per
    pos = (jnp.sum(onehot * tile_start[None, :], axis=1) * tm + rank).astype(I32)
    n_used = tile_end[-1:].astype(I32)
    tile_ids = jnp.arange(n_tiles, dtype=I32)
    tile_expert = jnp.minimum(
        jnp.sum((tile_ids[:, None] >= tile_end[None, :]).astype(I32), axis=1),
        N_EXPERTS - 1).astype(I32)
    padtile = jnp.where(counts % tm != 0, tile_end - 1, -1).astype(I32)
    starts_group = jnp.concatenate([jnp.ones((1,), I32),
                                    (tile_expert[1:] != tile_expert[:-1]).astype(I32)])
    group_slot = ((jnp.cumsum(starts_group) - 1) % 2).astype(I32)
    te_onehot = (tile_expert[:, None] == jnp.arange(N_EXPERTS, dtype=I32)[None, :]).astype(I32)
    next_group = jnp.sum(te_onehot * tile_end[None, :], axis=1).astype(I32)
    return pos, tile_expert, n_used, padtile, group_slot, next_group


def kernel(x, c, rel_bias, w_ada, b_ada, g_mix, w_in, lambda_q1, lambda_k1, lambda_q2,
           lambda_k2, subln_g, w_br_a, w_br_b, w_out, g_ffn, w_router, b_router,
           w_gate_up, b_gate_up, w_down, b_down, g_final):
    b, s, d = x.shape
    assert d == D_MODEL and s % ATT_T == 0 and MOBA_BLOCK == ATT_T
    t_tok = b * s

    c8 = jnp.zeros((8, d), F32).at[:b].set(c)
    ada = _ada(c8, w_ada[0], b_ada[0].reshape(1, -1))
    mod = ada[:b].reshape(b * 6, 1, d)

    dtiles = _bias_tiles(rel_bias)
    tab = rel_bias.T.reshape(-1)

    proj = _proj(x, g_mix[0].reshape(1, d), mod, w_in[0].astype(BF16))

    lamv = jnp.concatenate([lambda_q1, lambda_k1, lambda_q2, lambda_k2], axis=0)
    y_a = _attention("diff", tab, proj, dtiles, (lamv, subln_g[0].reshape(-1, 1)), (0, 512, 1024))
    y_b = _attention("moba", tab, proj, dtiles, None, (1536, 2048, 2560))

    wr = jnp.zeros((d, LANES), BF16).at[:, :N_EXPERTS].set(w_router[0].astype(BF16))
    br = jnp.zeros((1, LANES), F32).at[0, :N_EXPERTS].set(b_router[0])
    x1, h2, ti, tw = _merge(x, proj, y_a, y_b, w_br_a[0].astype(BF16), w_br_b[0].astype(BF16),
                            w_out[0].astype(BF16), mod, g_ffn[0].reshape(1, d), wr, br)

    n_tiles = (t_tok * TOP_K) // MOE_TM + N_EXPERTS
    top_idx = ti.reshape(t_tok, LANES)[:, :TOP_K]
    pos, tile_expert, n_used, padtile, group_slot, next_group = _routing_tables(top_idx, n_tiles)
    pos_start = pos * ROW_CHUNKS

    xs = _dispatch(h2.reshape(t_tok * ROW_CHUNKS, LANES), pos_start, padtile, n_used,
                   n_tiles * MOE_TM)
    ys = _moe(xs, tile_expert, n_used, group_slot, next_group, w_gate_up[0],
              b_gate_up[0].reshape(N_EXPERTS, 1, -1),
              w_down[0], b_down[0].reshape(N_EXPERTS, 1, -1))
    out = _combine(ys, pos_start, x1.reshape(t_tok, d), tw.reshape(t_tok, LANES), mod,
                   g_final.reshape(1, d), s)
    return out.reshape(b, s, d)
```

```python
import functools
import math

import jax
import jax.numpy as jnp
import numpy as np
from jax import lax
from jax.experimental import pallas as pl
from jax.experimental.pallas import tpu as pltpu

F32 = jnp.float32
BF16 = jnp.bfloat16
I32 = jnp.int32

D_MODEL = 1024
N_DIFF_HEADS = 4
DIFF_HEAD_DIM = 64
N_MOBA_HEADS = 8
MOBA_HEAD_DIM = 64
MOBA_BLOCK = 256
MOBA_TOPK = 3
N_HEADS_TOTAL = N_DIFF_HEADS + N_MOBA_HEADS
N_BUCKETS = 32
MAX_DISTANCE = 128
N_EXPERTS = 32
TOP_K = 4
D_FF = D_MODEL
SWIGLU_LIMIT = 7.0
SWIGLU_ALPHA = 1.702
RMS_EPS = 1e-5
LAM_INIT = 0.8 - 0.6 * math.exp(-0.3 * 0)
IN_WIDTH = 5120

LANES = 128
VMEM_LIMIT_BYTES = 56 * 1024 * 1024

ATT_T = 256
N_PAIRS = 4
ONES_ROWS = 16
PROJ_TM = 512
MERGE_TM = 512
MERGE_SUB = 256
MOE_TM = 256
MOE_FF_CHUNK = 256
DISPATCH_TM = 1024
ROW_DMA_UNROLL = 8
COMBINE_TM = 1024
NEG = -1e30
LOG2E = math.log2(math.e)


def _t5_bucket_np(dist):
    n = np.maximum(dist, 0)
    max_exact = N_BUCKETS // 2
    nf = np.maximum(n, max_exact).astype(np.float32)
    large = max_exact + (np.log(nf / max_exact) / math.log(MAX_DISTANCE / max_exact)
                         * (N_BUCKETS - max_exact)).astype(np.int32)
    large = np.minimum(large, N_BUCKETS - 1)
    return np.where(n < max_exact, n, large).astype(np.int32)


def _rms(x, g):
    return x * lax.rsqrt(jnp.mean(x * x, axis=-1, keepdims=True) + RMS_EPS) * g


ROW_CHUNKS = D_MODEL // LANES


def _store_token_tiles(ref, x):
    n = x.shape[0]
    for c in range(ROW_CHUNKS):
        ref[pl.ds(c, n, stride=ROW_CHUNKS), :] = x[:, c * LANES:(c + 1) * LANES]


def _load_token_tiles(ref):
    n = ref.shape[0] // ROW_CHUNKS
    return jnp.concatenate([ref[pl.ds(c, n, stride=ROW_CHUNKS), :] for c in range(ROW_CHUNKS)],
                           axis=1)


def _ada_kernel(c_ref, w_ref, b_ref, o_ref):
    o_ref[...] = jnp.dot(c_ref[...].astype(BF16), w_ref[...].astype(BF16),
                         preferred_element_type=F32) + b_ref[...]


def _ada(c8, w, b):
    d, n = w.shape
    tn = 1536
    return pl.pallas_call(
        _ada_kernel,
        out_shape=jax.ShapeDtypeStruct((8, n), F32),
        grid=(n // tn,),
        in_specs=[pl.BlockSpec((8, d), lambda j: (0, 0)),
                  pl.BlockSpec((d, tn), lambda j: (0, j)),
                  pl.BlockSpec((1, tn), lambda j: (0, j))],
        out_specs=pl.BlockSpec((8, tn), lambda j: (0, j)),
        compiler_params=pltpu.CompilerParams(dimension_semantics=("arbitrary",),
                                             vmem_limit_bytes=VMEM_LIMIT_BYTES),
        name="ada",
    )(c8, w, b)


def _bias_kernel(tab_ref, bd_ref, bp_ref, o_ref):
    g = pl.program_id(0)
    p = pl.program_id(1)
    t = pl.program_id(2)
    head = jnp.where(g == 0, p, N_DIFF_HEADS + 2 * p + t)
    bd = bd_ref[...]
    bp = bp_ref[...]
    d0 = jnp.where(bd < 0, NEG, 0.0).astype(F32)
    d1 = jnp.zeros(bp.shape, F32)
    for b in range(N_BUCKETS):
        val = tab_ref[head * N_BUCKETS + b] * LOG2E
        d0 = jnp.where(bd == b, val, d0)
        d1 = jnp.where(bp == b, val, d1)
    o_ref[0, 0, 0] = d0
    o_ref[0, 0, 1] = d1


def _bias_tiles(rel_bias):
    t = ATT_T
    jj = np.arange(t)[:, None]
    ii = np.arange(t)[None, :]
    bd = np.where(ii >= jj, _t5_bucket_np(ii - jj), -1).astype(np.int32)
    bp = _t5_bucket_np(t + ii - jj)
    tab = rel_bias.T.reshape(-1)
    return pl.pallas_call(
        _bias_kernel,
        out_shape=jax.ShapeDtypeStruct((2, N_PAIRS, 2, t, 2 * t), F32),
        grid=(2, N_PAIRS, 2),
        in_specs=[pl.BlockSpec(memory_space=pltpu.SMEM),
                  pl.BlockSpec((t, t), lambda g, p, h: (0, 0)),
                  pl.BlockSpec((t, t), lambda g, p, h: (0, 0))],
        out_specs=pl.BlockSpec((1, 1, 2, t, t), lambda g, p, h: (g, p, 0, 0, h)),
        compiler_params=pltpu.CompilerParams(
            dimension_semantics=("arbitrary", "arbitrary", "arbitrary")),
        name="bias_tiles",
    )(tab, jnp.asarray(bd), jnp.asarray(bp))


def _proj_kernel(x_ref, g_ref, sh_ref, sc_ref, w_ref, o_ref):
    x = x_ref[0]
    h = _rms(x, g_ref[...]) * (1.0 + sc_ref[0]) + sh_ref[0]
    hb = h.astype(BF16)
    n_total = w_ref.shape[1]
    for n0 in range(0, n_total, 512):
        o_ref[0, :, n0:n0 + 512] = jnp.dot(
            hb, w_ref[:, n0:n0 + 512], preferred_element_type=F32).astype(BF16)


def _proj(x, g_mix, mod, w_in_bf):
    b, s, d = x.shape
    n = w_in_bf.shape[1]
    tm = min(PROJ_TM, s)
    return pl.pallas_call(
        _proj_kernel,
        out_shape=jax.ShapeDtypeStruct((b, s, n), BF16),
        grid=(b, s // tm),
        in_specs=[pl.BlockSpec((1, tm, d), lambda bi, i: (bi, i, 0)),
                  pl.BlockSpec((1, d), lambda bi, i: (0, 0)),
                  pl.BlockSpec((1, 1, d), lambda bi, i: (bi * 6 + 0, 0, 0)),
                  pl.BlockSpec((1, 1, d), lambda bi, i: (bi * 6 + 1, 0, 0)),
                  pl.BlockSpec((d, n), lambda bi, i: (0, 0))],
        out_specs=pl.BlockSpec((1, tm, n), lambda bi, i: (bi, i, 0)),
        compiler_params=pltpu.CompilerParams(dimension_semantics=("arbitrary", "arbitrary"),
                                             vmem_limit_bytes=VMEM_LIMIT_BYTES),
        name="proj",
    )(x, g_mix, mod, mod, w_in_bf)


def _stack_q(q_ref, q2_ref, scale):
    t = ATT_T
    row = lax.broadcasted_iota(I32, (LANES, t), 0)
    for p in range(N_PAIRS):
        q = (q_ref[0, :, p * LANES:(p + 1) * LANES].astype(F32) * scale).T
        q2_ref[p, :, 0:t] = jnp.where(row < 64, q, 0.0).astype(BF16)
        q2_ref[p, :, t:2 * t] = jnp.where(row >= 64, q, 0.0).astype(BF16)


def _value_rows(p, h, vw):
    block = (p * (LANES // vw) + (h if vw < LANES else 0)) * (vw + ONES_ROWS)
    return block, vw + ONES_ROWS


def _transpose_values(v_ref, vt_ref, vw):
    t = ATT_T
    nq = vt_ref.shape[0]

    def body(n, carry):
        for p in range(N_PAIRS):
            blk = v_ref[0, pl.ds(pl.multiple_of(n * t, t), t), p * LANES:(p + 1) * LANES]
            blk_t = blk.astype(F32).T.astype(BF16)
            for h in range(LANES // vw):
                r0, _ = _value_rows(p, h, vw)
                vt_ref[n, r0:r0 + vw, :] = blk_t[h * vw:(h + 1) * vw, :]
                vt_ref[n, r0 + vw:r0 + vw + ONES_ROWS, :] = jnp.ones((ONES_ROWS, t), BF16)
        return carry

    lax.fori_loop(0, nq, body, 0)


def _flash_init(m_ref, acc_ref):
    m_ref[...] = jnp.full(m_ref.shape, NEG, F32)
    acc_ref[...] = jnp.zeros(acc_ref.shape, F32)


def _score_slot(p, h, bank):
    return bank * (2 * N_PAIRS) + 2 * p + h


def _flash_scores(p, h, n, bank, q2_ref, k_ref, s_ref, smax_ref):
    t = ATT_T
    kt = k_ref[0, pl.ds(pl.multiple_of(n * t, t), t), p * LANES:(p + 1) * LANES]
    s = jnp.dot(kt, q2_ref[p, :, h * t:(h + 1) * t], preferred_element_type=F32)
    s_ref[_score_slot(p, h, bank)] = s
    smax_ref[_score_slot(p, h, bank)] = jnp.max(s, axis=0, keepdims=True)


def _flash_absorb(p, h, n, bank, tile_bias, row_bias, live, v_ref, s_ref, smax_ref, m_ref,
                  acc_ref):
    t = ATT_T
    cs = slice(h * t, (h + 1) * t)
    r0, nr = _value_rows(p, h, acc_ref.shape[1] - ONES_ROWS)
    vt = v_ref[n, r0:r0 + nr, :]
    s = s_ref[_score_slot(p, h, bank)]
    if tile_bias is not None:
        s = s + tile_bias
        mx = jnp.max(s, axis=0, keepdims=True)
    else:
        mx = smax_ref[_score_slot(p, h, bank)]
    if row_bias is not None:
        mx = mx + row_bias
    if live is not None:
        mx = jnp.where(live, mx, NEG)
    m_prev = m_ref[p, :, cs]
    m_new = jnp.maximum(m_prev, mx)
    alpha = jnp.exp2(m_prev - m_new)
    shift = m_new if row_bias is None else m_new - row_bias
    if live is not None:
        shift = jnp.where(live, shift, -NEG)
    pt = jnp.exp2(s - shift)
    acc_ref[p, :, cs] = alpha * acc_ref[p, :, cs] + jnp.dot(vt, pt.astype(BF16),
                                                            preferred_element_type=F32)
    m_ref[p, :, cs] = m_new


def _far_bias(tab_ref, head):
    return tab_ref[head * N_BUCKETS + (N_BUCKETS - 1)] * LOG2E


def _flash_sweep(i, far_args, prev_args, own_args, refs):
    q2_ref, k_ref, v_ref, s_ref, smax_ref, m_ref, acc_ref = refs
    chains = [(p, h) for p in range(N_PAIRS) for h in range(2)]

    def scores_tile(n, bank):
        for p, h in chains:
            _flash_scores(p, h, n, bank, q2_ref, k_ref, s_ref, smax_ref)

    def step(n, bank, args, with_next):
        for p, h in chains:
            if with_next:
                _flash_scores(p, h, n + 1, 1 - bank, q2_ref, k_ref, s_ref, smax_ref)
            _flash_absorb(p, h, n, bank, *args(p, h), v_ref, s_ref, smax_ref, m_ref, acc_ref)

    def far_step(n, bank):
        step(n, bank, lambda p, h: far_args(p, h, n), True)

    n_far = jnp.maximum(i - 1, 0)
    lead = jnp.bitwise_and(n_far, 1)

    @pl.when(jnp.bitwise_and(i, 1) == 0)
    def _():
        scores_tile(0, 0)

    @pl.when(jnp.bitwise_and(i, 1) == 1)
    def _():
        scores_tile(0, 1)

    @pl.when(lead == 1)
    def _():
        far_step(0, 0)

    def pair_body(j, carry):
        n = lead + 2 * j
        far_step(n, 1)
        far_step(n + 1, 0)
        return carry

    lax.fori_loop(0, jnp.right_shift(n_far, 1), pair_body, 0)

    @pl.when(i >= 1)
    def _():
        step(i - 1, 1, prev_args, True)

    step(i, 0, own_args, False)


def _diff_kernel(tab_ref, q_ref, k_ref, v_ref, d_ref, lamv_ref, g_ref, o_ref,
                 q2_ref, s_ref, smax_ref, m_ref, acc_ref, vt_ref):
    t = ATT_T
    i = pl.program_id(1)

    @pl.when(i == 0)
    def _():
        _transpose_values(v_ref, vt_ref, LANES)

    _stack_q(q_ref, q2_ref, DIFF_HEAD_DIM ** -0.5 * LOG2E)
    _flash_init(m_ref, acc_ref)
    refs = (q2_ref, k_ref, vt_ref, s_ref, smax_ref, m_ref, acc_ref)
    _flash_sweep(i,
                 lambda p, h, n: (None, _far_bias(tab_ref, p), None),
                 lambda p, h: (d_ref[0, p, 1, :, h * t:(h + 1) * t], None, None),
                 lambda p, h: (d_ref[0, p, 0, :, h * t:(h + 1) * t], None, None),
                 refs)

    lv = lamv_ref[...]
    lam = (jnp.exp(jnp.sum(lv[0:1] * lv[1:2], axis=1, keepdims=True))
           - jnp.exp(jnp.sum(lv[2:3] * lv[3:4], axis=1, keepdims=True)) + LAM_INIT)
    for p in range(N_PAIRS):
        acc = acc_ref[p, 0:LANES, :]
        l = acc_ref[p, LANES:LANES + 1, :]
        o = acc[:, 0:t] / l[:, 0:t] - lam * (acc[:, t:2 * t] / l[:, t:2 * t])
        y = o * lax.rsqrt(jnp.mean(o * o, axis=0, keepdims=True) + RMS_EPS) * g_ref[...]
        o_ref[0, :, p * LANES:(p + 1) * LANES] = (y * (1.0 - LAM_INIT)).T.astype(BF16)


def _moba_kernel(tab_ref, q_ref, k_ref, v_ref, d_ref, o_ref,
                 q2_ref, s_ref, smax_ref, m_ref, acc_ref, vt_ref, kmean_ref, sel_ref):
    t = ATT_T
    i = pl.program_id(1)
    nb = k_ref.shape[1] // t
    nbp = kmean_ref.shape[1]

    @pl.when(i == 0)
    def _():
        _transpose_values(v_ref, vt_ref, MOBA_HEAD_DIM)
        kmean_ref[...] = jnp.zeros(kmean_ref.shape, F32)
        for p in range(N_PAIRS):
            for n in range(nb):
                blk = k_ref[0, n * t:(n + 1) * t, p * LANES:(p + 1) * LANES].astype(F32)
                kmean_ref[p, n:n + 1, :] = jnp.sum(blk, axis=0, keepdims=True) * (1.0 / t)

    _stack_q(q_ref, q2_ref, MOBA_HEAD_DIM ** -0.5 * LOG2E)
    _flash_init(m_ref, acc_ref)

    blk_id = lax.broadcasted_iota(I32, (nbp, 2 * t), 0)
    blk_f = blk_id.astype(F32)
    for p in range(N_PAIRS):
        gs = jnp.dot(kmean_ref[p].astype(BF16), q2_ref[p], preferred_element_type=F32)
        g = jnp.where(blk_id < i, gs, -jnp.inf)
        sel = jnp.full((nbp, 2 * t), NEG, F32)
        for _ in range(MOBA_TOPK):
            mx = jnp.max(g, axis=0, keepdims=True)
            first = jnp.min(jnp.where(g == mx, blk_f, float(nbp)), axis=0, keepdims=True)
            hit = blk_f == first
            sel = jnp.where(hit, 0.0, sel)
            g = jnp.where(hit, -jnp.inf, g)
        sel_ref[p] = sel

    def live(p, h, n):
        return sel_ref[p, pl.ds(n, 1), h * t:(h + 1) * t] > -1.0

    refs = (q2_ref, k_ref, vt_ref, s_ref, smax_ref, m_ref, acc_ref)
    _flash_sweep(i,
                 lambda p, h, n: (None, _far_bias(tab_ref, N_DIFF_HEADS + 2 * p + h),
                                  live(p, h, n)),
                 lambda p, h: (d_ref[0, p, 1, :, h * t:(h + 1) * t], None, live(p, h, i - 1)),
                 lambda p, h: (d_ref[0, p, 0, :, h * t:(h + 1) * t], None, None),
                 refs)

    hd = MOBA_HEAD_DIM
    for p in range(N_PAIRS):
        acc = acc_ref[p, 0:hd, :]
        l = acc_ref[p, hd:hd + 1, :]
        o = jnp.concatenate([acc[:, 0:t] / l[:, 0:t], acc[:, t:2 * t] / l[:, t:2 * t]], axis=0)
        o_ref[0, :, p * LANES:(p + 1) * LANES] = o.T.astype(BF16)


def _attention(kind, tab, proj, dtiles, extra, cols):
    b, s, _ = proj.shape
    t = ATT_T
    nq = s // t
    w = N_PAIRS * LANES
    qc, kc, vc = cols
    gidx = 0 if kind == "diff" else 1
    vw = LANES if kind == "diff" else MOBA_HEAD_DIM
    in_specs = [pl.BlockSpec(memory_space=pltpu.SMEM),
                pl.BlockSpec((1, t, w), lambda bi, i: (bi, i, qc // w)),
                pl.BlockSpec((1, s, w), lambda bi, i: (bi, 0, kc // w)),
                pl.BlockSpec((1, s, w), lambda bi, i: (bi, 0, vc // w)),
                pl.BlockSpec((1, N_PAIRS, 2, t, 2 * t), lambda bi, i: (gidx, 0, 0, 0, 0))]
    scratch = [pltpu.VMEM((N_PAIRS, LANES, 2 * t), BF16),
               pltpu.VMEM((2 * 2 * N_PAIRS, t, t), F32),
               pltpu.VMEM((2 * 2 * N_PAIRS, 1, t), F32),
               pltpu.VMEM((N_PAIRS, 1, 2 * t), F32),
               pltpu.VMEM((N_PAIRS, vw + ONES_ROWS, 2 * t), F32),
               pltpu.VMEM((nq, N_PAIRS * (LANES // vw) * (vw + ONES_ROWS), t), BF16)]
    if kind == "diff":
        lamv, subln_g = extra
        in_specs += [pl.BlockSpec(lamv.shape, lambda bi, i: (0, 0)),
                     pl.BlockSpec(subln_g.shape, lambda bi, i: (0, 0))]
        args = (tab, proj, proj, proj, dtiles, lamv, subln_g)
        body = _diff_kernel
    else:
        nbp = max(32, nq)
        scratch += [pltpu.VMEM((N_PAIRS, nbp, LANES), F32), pltpu.VMEM((N_PAIRS, nbp, 2 * t), F32)]
        args = (tab, proj, proj, proj, dtiles)
        body = _moba_kernel
    return pl.pallas_call(
        body,
        out_shape=jax.ShapeDtypeStruct((b, s, w), BF16),
        grid=(b, nq),
        in_specs=in_specs,
        out_specs=pl.BlockSpec((1, t, w), lambda bi, i: (bi, i, 0)),
        scratch_shapes=scratch,
        compiler_params=pltpu.CompilerParams(
            dimension_semantics=("arbitrary", "arbitrary"),
            vmem_limit_bytes=VMEM_LIMIT_BYTES),
        name=kind + "_attn",
    )(*args)


def _merge_kernel(x_ref, ya_ref, yb_ref, gla_ref, glb_ref, wa_ref, wb_ref, wo_ref,
                  gt_ref, shf_ref, scf_ref, gf_ref, wr_ref, br_ref,
                  x1_ref, h2_ref, ti_ref, tw_ref):
    ts = MERGE_SUB
    subs = [slice(j * ts, (j + 1) * ts) for j in range(x_ref.shape[1] // ts)]
    branch = [(jnp.dot(ya_ref[0, sl], wa_ref[...], preferred_element_type=F32),
               jnp.dot(yb_ref[0, sl], wb_ref[...], preferred_element_type=F32)) for sl in subs]
    mixed = []
    for sl, (a, b) in zip(subs, branch):
        merged = (jax.nn.sigmoid(gla_ref[0, sl].astype(F32)) * a
                  + jax.nn.sigmoid(glb_ref[0, sl].astype(F32)) * b)
        mixed.append(jnp.dot(merged.astype(BF16), wo_ref[...], preferred_element_type=F32))
    lane = lax.broadcasted_iota(I32, (ts, LANES), 1)
    lane_f = lane.astype(F32)
    for j, (sl, mo) in enumerate(zip(subs, mixed)):
        x1 = x_ref[0, sl] + gt_ref[0] * mo
        x1_ref[0, sl] = x1
        h = _rms(x1, gf_ref[...]) * (1.0 + scf_ref[0]) + shf_ref[0]
        _store_token_tiles(h2_ref.at[0, j * ts * ROW_CHUNKS:(j + 1) * ts * ROW_CHUNKS], h)
        logits = jnp.dot(h.astype(BF16), wr_ref[...], preferred_element_type=F32) + br_ref[...]
        g = jnp.where(lane < N_EXPERTS, logits, -jnp.inf)
        vals, idxs = [], []
        for _ in range(TOP_K):
            mx = jnp.max(g, axis=1, keepdims=True)
            first = jnp.min(jnp.where(g == mx, lane_f, float(LANES)), axis=1, keepdims=True)
            vals.append(mx)
            idxs.append(first)
            g = jnp.where(lane_f == first, -jnp.inf, g)
        es = [jnp.exp(v - vals[0]) for v in vals]
        den = es[0] + es[1] + es[2] + es[3]
        ti = jnp.zeros((ts, LANES), F32)
        tw = jnp.zeros((ts, LANES), F32)
        for k in range(TOP_K):
            ti = jnp.where(lane == k, idxs[k], ti)
            tw = jnp.where(lane == k, es[k] / den, tw)
        ti_ref[0, sl] = ti.astype(I32)
        tw_ref[0, sl] = tw


def _merge(x, proj, y_a, y_b, w_br_a, w_br_b, w_out, mod, g_ffn, w_router, b_router):
    b, s, d = x.shape
    tm = min(MERGE_TM, s)
    full = lambda shape: pl.BlockSpec(shape, lambda bi, i: (0,) * len(shape))
    tile = lambda w, c=0: pl.BlockSpec((1, tm, w), lambda bi, i: (bi, i, c))
    modspec = lambda j: pl.BlockSpec((1, 1, d), lambda bi, i: (bi * 6 + j, 0, 0))
    return pl.pallas_call(
        _merge_kernel,
        out_shape=(jax.ShapeDtypeStruct((b, s, d), F32),
                   jax.ShapeDtypeStruct((b, s * ROW_CHUNKS, LANES), F32),
                   jax.ShapeDtypeStruct((b, s, LANES), I32),
                   jax.ShapeDtypeStruct((b, s, LANES), F32)),
        grid=(b, s // tm),
        in_specs=[tile(d), tile(512), tile(512), tile(d, 3), tile(d, 4),
                  full(w_br_a.shape), full(w_br_b.shape), full(w_out.shape),
                  modspec(2), modspec(3), modspec(4), full(g_ffn.shape),
                  full(w_router.shape), full(b_router.shape)],
        out_specs=(tile(d), pl.BlockSpec((1, tm * ROW_CHUNKS, LANES), lambda bi, i: (bi, i, 0)),
                   tile(LANES), tile(LANES)),
        compiler_params=pltpu.CompilerParams(dimension_semantics=("arbitrary", "arbitrary"),
                                             vmem_limit_bytes=VMEM_LIMIT_BYTES),
        name="merge",
    )(x, y_a, y_b, proj, proj, w_br_a, w_br_b, w_out, mod, mod, mod, g_ffn,
      w_router, b_router)


def _dispatch_kernel(padtile_ref, nu_ref, pos_ref, h2_ref, xs_ref, zeros_ref, zsem, sem):
    step = pl.program_id(0)
    rc = ROW_CHUNKS
    n_tok = h2_ref.shape[0] // rc
    pad_rows = zeros_ref.shape[0]
    n_tiles = xs_ref.shape[0] // pad_rows

    def zero_tile(tile):
        start = pl.multiple_of(tile * pad_rows, pad_rows)
        return pltpu.make_async_copy(zeros_ref, xs_ref.at[pl.ds(start, pad_rows)], zsem)

    @pl.when(step == 0)
    def _():
        zeros_ref[...] = jnp.zeros(zeros_ref.shape, F32)
        for e in range(N_EXPERTS):
            @pl.when(padtile_ref[e] >= 0)
            def _():
                zero_tile(padtile_ref[e]).start()
        for e in range(N_EXPERTS):
            @pl.when(padtile_ref[e] >= 0)
            def _():
                zero_tile(padtile_ref[e]).wait()

        def start_tail(tile, carry):
            zero_tile(tile).start()
            return carry

        def wait_tail(tile, carry):
            zero_tile(tile).wait()
            return carry

        lax.fori_loop(nu_ref[0], n_tiles, start_tail, 0)
        lax.fori_loop(nu_ref[0], n_tiles, wait_tail, 0)

    def issue(j, carry):
        src = h2_ref.at[pl.ds(pl.multiple_of(j * rc, rc), rc)]
        for k in range(TOP_K):
            dst = xs_ref.at[pl.ds(pl.multiple_of(pos_ref[j * TOP_K + k], rc), rc)]
            pltpu.make_async_copy(src, dst, sem).start(priority=k % 2)
        return carry

    lax.fori_loop(0, n_tok, issue, 0, unroll=ROW_DMA_UNROLL)
    for k in range(TOP_K):
        pltpu.make_async_copy(h2_ref, xs_ref.at[pl.ds(0, n_tok * rc)], sem).wait()


def _dispatch(h2_tiles, pos_start, padtile, n_used, n_rows):
    rc = ROW_CHUNKS
    t_tok = h2_tiles.shape[0] // rc
    tm = min(DISPATCH_TM, t_tok)
    return pl.pallas_call(
        _dispatch_kernel,
        out_shape=jax.ShapeDtypeStruct((n_rows * rc, LANES), F32),
        grid_spec=pltpu.PrefetchScalarGridSpec(
            num_scalar_prefetch=2,
            grid=(t_tok // tm,),
            in_specs=[pl.BlockSpec((tm * TOP_K,), lambda s, pt, nu: (s,),
                                   memory_space=pltpu.SMEM),
                      pl.BlockSpec((tm * rc, LANES), lambda s, pt, nu: (s, 0))],
            out_specs=pl.BlockSpec(memory_space=pl.ANY),
            scratch_shapes=[pltpu.VMEM((MOE_TM * rc, LANES), F32),
                            pltpu.SemaphoreType.DMA, pltpu.SemaphoreType.DMA]),
        compiler_params=pltpu.CompilerParams(dimension_semantics=("arbitrary",),
                                             vmem_limit_bytes=VMEM_LIMIT_BYTES),
        name="dispatch",
    )(padtile, n_used, pos_start, h2_tiles)


def _moe_kernel(te_ref, nu_ref, slot_ref, next_ref, xs_ref, wgu_hbm, bgu_ref, wd_hbm, bd_ref,
                ys_ref, wgu_bf, wd_bf, gu_ref, wgu_f32, wd_f32, wsem):
    r = pl.program_id(0)
    n_used = nu_ref[0]
    busy = r < n_used
    fresh = jnp.logical_or(r == 0, te_ref[r] != te_ref[jnp.maximum(r - 1, 0)])
    slot = slot_ref[r]

    def weight_copies(expert, s):
        return (pltpu.make_async_copy(wgu_hbm.at[expert], wgu_f32.at[s], wsem.at[0, s]),
                pltpu.make_async_copy(wd_hbm.at[expert], wd_f32.at[s], wsem.at[1, s]))

    @pl.when(r == 0)
    def _():
        for cp in weight_copies(te_ref[0], 0):
            cp.start()

    @pl.when(jnp.logical_and(busy, fresh))
    def _():
        for cp in weight_copies(te_ref[r], slot):
            cp.wait()
        nxt = next_ref[r]

        @pl.when(nxt < n_used)
        def _():
            for cp in weight_copies(te_ref[jnp.minimum(nxt, te_ref.shape[0] - 1)], 1 - slot):
                cp.start()

        wgu_bf[...] = wgu_f32[slot].astype(BF16)
        wd_bf[...] = wd_f32[slot].astype(BF16)

    @pl.when(jnp.logical_not(busy))
    def _():
        ys_ref[...] = jnp.zeros(ys_ref.shape, F32)

    @pl.when(busy)
    def _():
        x = _load_token_tiles(xs_ref).astype(BF16)
        fc = MOE_FF_CHUNK
        for c in range(2 * D_FF // fc):
            lo, hi = c * fc, (c + 1) * fc
            gu_ref[:, lo:hi] = (jnp.dot(x, wgu_bf[:, lo:hi], preferred_element_type=F32)
                                + bgu_ref[0][:, lo:hi])
        y = None
        for c in range(D_FF // fc):
            lo, hi = c * fc, (c + 1) * fc
            gate = jnp.minimum(gu_ref[:, lo:hi], SWIGLU_LIMIT)
            up = jnp.clip(gu_ref[:, D_FF + lo:D_FF + hi], -SWIGLU_LIMIT, SWIGLU_LIMIT)
            act = (up + 1.0) * gate * jax.nn.sigmoid(SWIGLU_ALPHA * gate)
            part = jnp.dot(act.astype(BF16), wd_bf[lo:hi, :], preferred_element_type=F32)
            y = part if y is None else y + part
        _store_token_tiles(ys_ref, y + bd_ref[0])


def _moe(xs_tiles, tile_expert, n_used, group_slot, next_group, w_gate_up, b_gate_up, w_down,
         b_down):
    d = D_MODEL
    rc = ROW_CHUNKS
    tm = MOE_TM
    n_tiles = tile_expert.shape[0]
    e, _, f2 = w_gate_up.shape
    bsel = lambda r, te, nu, gs, ng: (te[r], 0, 0)
    rows = lambda r, te, nu, gs, ng: (jnp.minimum(r, nu[0] - 1), 0)
    return pl.pallas_call(
        _moe_kernel,
        out_shape=jax.ShapeDtypeStruct((n_tiles * tm * rc, LANES), F32),
        grid_spec=pltpu.PrefetchScalarGridSpec(
            num_scalar_prefetch=4,
            grid=(n_tiles,),
            in_specs=[pl.BlockSpec((tm * rc, LANES), rows),
                      pl.BlockSpec(memory_space=pl.ANY),
                      pl.BlockSpec((1, 1, f2), bsel),
                      pl.BlockSpec(memory_space=pl.ANY),
                      pl.BlockSpec((1, 1, d), bsel)],
            out_specs=pl.BlockSpec((tm * rc, LANES), lambda r, te, nu, gs, ng: (r, 0)),
            scratch_shapes=[pltpu.VMEM((d, f2), BF16), pltpu.VMEM((D_FF, d), BF16),
                            pltpu.VMEM((tm, f2), F32),
                            pltpu.VMEM((2, d, f2), F32), pltpu.VMEM((2, D_FF, d), F32),
                            pltpu.SemaphoreType.DMA((2, 2))]),
        compiler_params=pltpu.CompilerParams(dimension_semantics=("arbitrary",),
                                             vmem_limit_bytes=VMEM_LIMIT_BYTES),
        name="moe",
    )(tile_expert, n_used, group_slot, next_group, xs_tiles, w_gate_up, b_gate_up, w_down, b_down)


def _combine_kernel(pos_ref, ys_ref, x1_ref, tw_ref, gt_ref, gf_ref, o_ref, gbuf, sem):
    tm = x1_ref.shape[0]
    rc = ROW_CHUNKS

    def issue(j, carry):
        for k in range(TOP_K):
            src = ys_ref.at[pl.ds(pl.multiple_of(pos_ref[j * TOP_K + k], rc), rc)]
            dst = gbuf.at[k, pl.ds(pl.multiple_of(j * rc, rc), rc)]
            pltpu.make_async_copy(src, dst, sem).start(priority=k % 2)
        return carry

    lax.fori_loop(0, tm, issue, 0, unroll=ROW_DMA_UNROLL)
    for k in range(TOP_K):
        pltpu.make_async_copy(ys_ref.at[pl.ds(0, tm * rc)], gbuf.at[k], sem).wait()

    tw = tw_ref[...]
    acc = tw[:, 0:1] * _load_token_tiles(gbuf.at[0])
    for k in range(1, TOP_K):
        acc = acc + tw[:, k:k + 1] * _load_token_tiles(gbuf.at[k])
    x2 = x1_ref[...] + gt_ref[0] * acc
    o_ref[...] = _rms(x2, gf_ref[...])


def _combine(ys, pos_flat, x1, tw, mod, g_final, seq):
    t, d = x1.shape
    tm = min(COMBINE_TM, t)
    return pl.pallas_call(
        _combine_kernel,
        out_shape=jax.ShapeDtypeStruct((t, d), F32),
        grid=(t // tm,),
        in_specs=[pl.BlockSpec((tm * TOP_K,), lambda s: (s,), memory_space=pltpu.SMEM),
                  pl.BlockSpec(memory_space=pl.ANY),
                  pl.BlockSpec((tm, d), lambda s: (s, 0)),
                  pl.BlockSpec((tm, LANES), lambda s: (s, 0)),
                  pl.BlockSpec((1, 1, d), lambda s: ((s * tm) // seq * 6 + 5, 0, 0)),
                  pl.BlockSpec((1, d), lambda s: (0, 0))],
        out_specs=pl.BlockSpec((tm, d), lambda s: (s, 0)),
        scratch_shapes=[pltpu.VMEM((TOP_K, tm * ROW_CHUNKS, LANES), F32),
                        pltpu.SemaphoreType.DMA],
        compiler_params=pltpu.CompilerParams(dimension_semantics=("arbitrary",),
                                             vmem_limit_bytes=VMEM_LIMIT_BYTES),
        name="combine",
    )(pos_flat, ys, x1, tw, mod, g_final)


def _routing_tables(top_idx, n_tiles):
    tm = MOE_TM
    e_flat = top_idx.reshape(-1)
    onehot = (e_flat[:, None] == jnp.arange(N_EXPERTS, dtype=I32)[None, :]).astype(I32)
    csum = jnp.cumsum(onehot, axis=0)
    counts = csum[-1]
    rank = jnp.sum(csum * onehot, axis=1) - 1
    tiles_per = (counts + tm - 1) // tm
    tile_end = jnp.cumsum(tiles_per)
    tile_start = tile_end - tiles_per
    pos = (jnp.sum(onehot * tile_start[None, :], axis=1) * tm + rank).astype(I32)
    n_used = tile_end[-1:].astype(I32)
    tile_ids = jnp.arange(n_tiles, dtype=I32)
    tile_expert = jnp.minimum(
        jnp.sum((tile_ids[:, None] >= tile_end[None, :]).astype(I32), axis=1),
        N_EXPERTS - 1).astype(I32)
    padtile = jnp.where(counts % tm != 0, tile_end - 1, -1).astype(I32)
    starts_group = jnp.concatenate([jnp.ones((1,), I32),
                                    (tile_expert[1:] != tile_expert[:-1]).astype(I32)])
    group_slot = ((jnp.cumsum(starts_group) - 1) % 2).astype(I32)
    te_onehot = (tile_expert[:, None] == jnp.arange(N_EXPERTS, dtype=I32)[None, :]).astype(I32)
    next_group = jnp.sum(te_onehot * tile_end[None, :], axis=1).astype(I32)
    return pos, tile_expert, n_used, padtile, group_slot, next_group


def kernel(x, c, rel_bias, w_ada, b_ada, g_mix, w_in, lambda_q1, lambda_k1, lambda_q2,
           lambda_k2, subln_g, w_br_a, w_br_b, w_out, g_ffn, w_router, b_router,
           w_gate_up, b_gate_up, w_down, b_down, g_final):
    b, s, d = x.shape
    assert d == D_MODEL and s % ATT_T == 0 and MOBA_BLOCK == ATT_T
    t_tok = b * s

    c8 = jnp.zeros((8, d), F32).at[:b].set(c)
    ada = _ada(c8, w_ada[0], b_ada[0].reshape(1, -1))
    mod = ada[:b].reshape(b * 6, 1, d)

    dtiles = _bias_tiles(rel_bias)
    tab = rel_bias.T.reshape(-1)

    proj = _proj(x, g_mix[0].reshape(1, d), mod, w_in[0].astype(BF16))

    lamv = jnp.concatenate([lambda_q1, lambda_k1, lambda_q2, lambda_k2], axis=0)
    y_a = _attention("diff", tab, proj, dtiles, (lamv, subln_g[0].reshape(-1, 1)), (0, 512, 1024))
    y_b = _attention("moba", tab, proj, dtiles, None, (1536, 2048, 2560))

    wr = jnp.zeros((d, LANES), BF16).at[:, :N_EXPERTS].set(w_router[0].astype(BF16))
    br = jnp.zeros((1, LANES), F32).at[0, :N_EXPERTS].set(b_router[0])
    x1, h2, ti, tw = _merge(x, proj, y_a, y_b, w_br_a[0].astype(BF16), w_br_b[0].astype(BF16),
                            w_out[0].astype(BF16), mod, g_ffn[0].reshape(1, d), wr, br)

    n_tiles = (t_tok * TOP_K) // MOE_TM + N_EXPERTS
    top_idx = ti.reshape(t_tok, LANES)[:, :TOP_K]
    pos, tile_expert, n_used, padtile, group_slot, next_group = _routing_tables(top_idx, n_tiles)
    pos_start = pos * ROW_CHUNKS

    xs = _dispatch(h2.reshape(t_tok * ROW_CHUNKS, LANES), pos_start, padtile, n_used,
                   n_tiles * MOE_TM)
    ys = _moe(xs, tile_expert, n_used, group_slot, next_group, w_gate_up[0],
              b_gate_up[0].reshape(N_EXPERTS, 1, -1),
              w_down[0], b_down[0].reshape(N_EXPERTS, 1, -1))
    out = _combine(ys, pos_start, x1.reshape(t_tok, d), tw.reshape(t_tok, LANES), mod,
                   g_final.reshape(1, d), s)
    return out.reshape(b, s, d)
```
